```python
import math
import jax, jax.numpy as jnp
from jax import lax
import numpy as np


D_MODEL = 1024
BATCH = 8
SEQ = 2048
DEPTH = 2

SSM_WIDTH = D_MODEL // 2
SSM_GROUP = 16
SSM_GROUPS = SSM_WIDTH // SSM_GROUP
SSM_STATE = 64
DT_MIN = 1e-3
DT_MAX = 1e-1
CONV_WIDTH = D_MODEL // 2
CONV_KERNEL = 31
POOL_WIDTH = D_MODEL // 2
POOL_WINDOWS = (2, 4, 8, 16)
POOL_GROUP = POOL_WIDTH // len(POOL_WINDOWS)
N_BRANCHES = 3
IN_WIDTH = SSM_WIDTH + 2 * CONV_WIDTH + POOL_WIDTH + N_BRANCHES * D_MODEL
FFN_HIDDEN = ((8 * D_MODEL + 3 * 256 - 1) // (3 * 256)) * 256
EPS = 1e-6

kernel_name = 'hybrid_s5_conformer_pool_gated_block'


def rms_norm(x, g):
    xf = x.astype(jnp.float32)
    y = xf * lax.rsqrt(jnp.mean(xf * xf, axis=-1, keepdims=True) + EPS)
    return (y * g.astype(jnp.float32)).astype(x.dtype)


def layer_norm(x, g, b):
    xf = x.astype(jnp.float32)
    mu = jnp.mean(xf, axis=-1, keepdims=True)
    var = jnp.mean(jnp.square(xf - mu), axis=-1, keepdims=True)
    y = (xf - mu) * lax.rsqrt(var + EPS)
    return (y * g.astype(jnp.float32) + b.astype(jnp.float32)).astype(x.dtype)


def _complex_linear_combine(e1, e2):
    a1r, a1i, b1r, b1i = e1
    a2r, a2i, b2r, b2i = e2
    return (a2r * a1r - a2i * a1i,
            a2r * a1i + a2i * a1r,
            a2r * b1r - a2i * b1i + b2r,
            a2r * b1i + a2i * b1r + b2i)


def s5_mixer(u, a_re, a_im, log_dt, b_re, b_im, c_re, c_im, d_skip, w_glu, b_glu):
    bsz, seq, _ = u.shape
    f32 = jnp.float32
    uf = u.astype(f32).reshape(bsz, seq, SSM_GROUPS, SSM_GROUP)
    a_re = a_re.astype(f32)
    a_im = a_im.astype(f32)
    dt = jnp.exp(log_dt.astype(f32))[:, None]
    mag = jnp.exp(dt * a_re)
    ang = dt * a_im
    abar_re = mag * jnp.cos(ang)
    abar_im = mag * jnp.sin(ang)
    den = a_re * a_re + a_im * a_im
    nr = abar_re - 1.0
    ni = abar_im
    f_re = (nr * a_re + ni * a_im) / den
    f_im = (ni * a_re - nr * a_im) / den
    b_re = b_re.astype(f32)
    b_im = b_im.astype(f32)
    bbar_re = f_re[..., None] * b_re - f_im[..., None] * b_im
    bbar_im = f_re[..., None] * b_im + f_im[..., None] * b_re
    bu_re = jnp.einsum('bsgp,gnp->bsgn', uf, bbar_re)
    bu_im = jnp.einsum('bsgp,gnp->bsgn', uf, bbar_im)
    a_seq_re = jnp.broadcast_to(abar_re, bu_re.shape)
    a_seq_im = jnp.broadcast_to(abar_im, bu_im.shape)
    _, _, h_re, h_im = lax.associative_scan(
        _complex_linear_combine, (a_seq_re, a_seq_im, bu_re, bu_im), axis=1)
    y = (jnp.einsum('bsgn,gpn->bsgp', h_re, c_re.astype(f32))
         - jnp.einsum('bsgn,gpn->bsgp', h_im, c_im.astype(f32))
         + d_skip.astype(f32) * uf)
    y = y.reshape(bsz, seq, SSM_WIDTH)
    g = jax.nn.gelu(y)
    out = g * jax.nn.sigmoid(g @ w_glu.astype(f32) + b_glu.astype(f32))
    return out.astype(u.dtype)


def conv_module(v, w_dw, b_dw, ln_g, ln_b, w_proj):
    h = v[..., :CONV_WIDTH] * jax.nn.sigmoid(v[..., CONV_WIDTH:])
    h = jnp.pad(h, ((0, 0), (CONV_KERNEL - 1, 0), (0, 0)))
    h = lax.conv_general_dilated(h, w_dw, window_strides=(1,), padding='VALID',
                                 dimension_numbers=('NWC', 'WIO', 'NWC'),
                                 feature_group_count=CONV_WIDTH) + b_dw
    h = jax.nn.silu(layer_norm(h, ln_g, ln_b))
    return h @ w_proj


def pool_mixer(u, w_group, scale, w_proj):
    bsz, seq, _ = u.shape
    uf = u.astype(jnp.float32).reshape(bsz, seq, len(POOL_WINDOWS), POOL_GROUP)
    cs = jnp.cumsum(uf, axis=1)
    pos = jnp.arange(1, seq + 1, dtype=jnp.float32)
    outs = []
    for k, w in enumerate(POOL_WINDOWS):
        c = cs[:, :, k]
        lagged = jnp.pad(c, ((0, 0), (w, 0), (0, 0)))[:, :seq]
        mean = (c - lagged) / jnp.minimum(pos, float(w))[None, :, None]
        outs.append(mean - uf[:, :, k])
    p = jnp.stack(outs, axis=2)
    p = jnp.einsum('bsgc,gcd->bsgd', p, w_group.astype(jnp.float32))
    p = p.reshape(bsz, seq, POOL_WIDTH) * scale.astype(jnp.float32)
    return p.astype(u.dtype) @ w_proj


def hybrid_layer(x, norm1, w_in, b_gate, a_re, a_im, log_dt, b_re, b_im, c_re, c_im,
                 d_skip, w_glu, b_glu, ssm_w_proj, conv_w_dw, conv_b_dw, conv_ln_g,
                 conv_ln_b, conv_w_proj, pool_w_group, pool_scale, pool_w_proj, w_out,
                 norm2, w_gate, w_up, w_down):
    bsz, seq, _ = x.shape
    h = rms_norm(x, norm1)
    z = h @ w_in
    o1 = SSM_WIDTH
    o2 = o1 + 2 * CONV_WIDTH
    o3 = o2 + POOL_WIDTH
    u_a = z[..., :o1]
    v_b = z[..., o1:o2]
    u_c = z[..., o2:o3]
    gates = jax.nn.sigmoid(z[..., o3:] + b_gate).reshape(bsz, seq, N_BRANCHES, D_MODEL)
    y_a = s5_mixer(u_a, a_re, a_im, log_dt, b_re, b_im, c_re, c_im, d_skip, w_glu, b_glu) @ ssm_w_proj
    y_b = conv_module(v_b, conv_w_dw, conv_b_dw, conv_ln_g, conv_ln_b, conv_w_proj)
    y_c = pool_mixer(u_c, pool_w_group, pool_scale, pool_w_proj)
    merged = gates[:, :, 0] * y_a + gates[:, :, 1] * y_b + gates[:, :, 2] * y_c
    x = x + merged @ w_out
    h = rms_norm(x, norm2)
    x = x + (jax.nn.silu(h @ w_gate) * (h @ w_up)) @ w_down
    return x


def _fwd_setup_inputs(seed: int = 0) -> dict:
    key = jax.random.key(seed)
    ks = iter(jax.random.split(key, 40))
    f32 = jnp.float32

    def nrm(shape, scale):
        return jax.random.normal(next(ks), shape, f32) * scale

    L, D, G, N, P = DEPTH, D_MODEL, SSM_GROUPS, SSM_STATE, SSM_GROUP
    n_idx = jnp.arange(N, dtype=f32)
    inputs = {
        'x': nrm((BATCH, SEQ, D), 1.0),
        'norm1': 1.0 + nrm((L, D), 0.02),
        'w_in': nrm((L, D, IN_WIDTH), D ** -0.5),
        'b_gate': nrm((L, N_BRANCHES * D), 0.01),
        'ssm_a_re': -0.5 + nrm((L, G, N), 0.01),
        'ssm_a_im': math.pi * n_idx + nrm((L, G, N), 0.01),
        'ssm_log_dt': jax.random.uniform(next(ks), (L, G), f32,
                                         math.log(DT_MIN), math.log(DT_MAX)),
        'ssm_b_re': nrm((L, G, N, P), (2.0 * P) ** -0.5),
        'ssm_b_im': nrm((L, G, N, P), (2.0 * P) ** -0.5),
        'ssm_c_re': nrm((L, G, P, N), (2.0 * N) ** -0.5 * 4.0),
        'ssm_c_im': nrm((L, G, P, N), (2.0 * N) ** -0.5 * 4.0),
        'ssm_d': nrm((L, G, P), 1.0),
        'ssm_w_glu': nrm((L, SSM_WIDTH, SSM_WIDTH), SSM_WIDTH ** -0.5),
        'ssm_b_glu': nrm((L, SSM_WIDTH), 0.01),
        'ssm_w_proj': nrm((L, SSM_WIDTH, D), SSM_WIDTH ** -0.5),
        'conv_w_dw': nrm((L, CONV_KERNEL, 1, CONV_WIDTH), CONV_KERNEL ** -0.5),
        'conv_b_dw': nrm((L, CONV_WIDTH), 0.01),
        'conv_ln_g': 1.0 + nrm((L, CONV_WIDTH), 0.02),
        'conv_ln_b': nrm((L, CONV_WIDTH), 0.01),
        'conv_w_proj': nrm((L, CONV_WIDTH, D), CONV_WIDTH ** -0.5),
        'pool_w_group': nrm((L, len(POOL_WINDOWS), POOL_GROUP, POOL_GROUP), POOL_GROUP ** -0.5),
        'pool_scale': 1.0 + nrm((L, POOL_WIDTH), 0.02),
        'pool_w_proj': nrm((L, POOL_WIDTH, D), POOL_WIDTH ** -0.5),
        'w_out': nrm((L, D, D), D ** -0.5),
        'norm2': 1.0 + nrm((L, D), 0.02),
        'ffn_w_gate': nrm((L, D, FFN_HIDDEN), D ** -0.5),
        'ffn_w_up': nrm((L, D, FFN_HIDDEN), D ** -0.5),
        'ffn_w_down': nrm((L, FFN_HIDDEN, D), FFN_HIDDEN ** -0.5),
        'final_norm': 1.0 + nrm((D,), 0.02),
    }
    return inputs


def _fwd_reference(x, norm1, w_in, b_gate, ssm_a_re, ssm_a_im, ssm_log_dt, ssm_b_re, ssm_b_im,
              ssm_c_re, ssm_c_im, ssm_d, ssm_w_glu, ssm_b_glu, ssm_w_proj, conv_w_dw,
              conv_b_dw, conv_ln_g, conv_ln_b, conv_w_proj, pool_w_group, pool_scale,
              pool_w_proj, w_out, norm2, ffn_w_gate, ffn_w_up, ffn_w_down, final_norm):
    for l in range(DEPTH):
        x = hybrid_layer(x, norm1[l], w_in[l], b_gate[l], ssm_a_re[l], ssm_a_im[l],
                         ssm_log_dt[l], ssm_b_re[l], ssm_b_im[l], ssm_c_re[l], ssm_c_im[l],
                         ssm_d[l], ssm_w_glu[l], ssm_b_glu[l], ssm_w_proj[l], conv_w_dw[l],
                         conv_b_dw[l], conv_ln_g[l], conv_ln_b[l], conv_w_proj[l],
                         pool_w_group[l], pool_scale[l], pool_w_proj[l], w_out[l], norm2[l],
                         ffn_w_gate[l], ffn_w_up[l], ffn_w_down[l])
    return rms_norm(x, final_norm)


import jax as _jax
import jax.numpy as _jnp

TWIN_FORMAT = 'train_step'
FWD_PARAMS = ['x', 'norm1', 'w_in', 'b_gate', 'ssm_a_re', 'ssm_a_im', 'ssm_log_dt', 'ssm_b_re', 'ssm_b_im', 'ssm_c_re', 'ssm_c_im', 'ssm_d', 'ssm_w_glu', 'ssm_b_glu', 'ssm_w_proj', 'conv_w_dw', 'conv_b_dw', 'conv_ln_g', 'conv_ln_b', 'conv_w_proj', 'pool_w_group', 'pool_scale', 'pool_w_proj', 'w_out', 'norm2', 'ffn_w_gate', 'ffn_w_up', 'ffn_w_down', 'final_norm']
TWIN_WEIGHTS = ['norm1', 'w_in', 'b_gate', 'ssm_a_re', 'ssm_a_im', 'ssm_log_dt', 'ssm_b_re', 'ssm_b_im', 'ssm_c_re', 'ssm_c_im', 'ssm_d', 'ssm_w_glu', 'ssm_b_glu', 'ssm_w_proj', 'conv_w_dw', 'conv_b_dw', 'conv_ln_g', 'conv_ln_b', 'conv_w_proj', 'pool_w_group', 'pool_scale', 'pool_w_proj', 'w_out', 'norm2', 'ffn_w_gate', 'ffn_w_up', 'ffn_w_down', 'final_norm']
TWIN_DIFF_INPUT = 'x'
TWIN_INPUTS = ['x', 'norm1', 'w_in', 'b_gate', 'ssm_a_re', 'ssm_a_im', 'ssm_log_dt', 'ssm_b_re', 'ssm_b_im', 'ssm_c_re', 'ssm_c_im', 'ssm_d', 'ssm_w_glu', 'ssm_b_glu', 'ssm_w_proj', 'conv_w_dw', 'conv_b_dw', 'conv_ln_g', 'conv_ln_b', 'conv_w_proj', 'pool_w_group', 'pool_scale', 'pool_w_proj', 'w_out', 'norm2', 'ffn_w_gate', 'ffn_w_up', 'ffn_w_down', 'final_norm', 'loss_target', 'm_norm1', 'm_w_in', 'm_b_gate', 'm_ssm_a_re', 'm_ssm_a_im', 'm_ssm_log_dt', 'm_ssm_b_re', 'm_ssm_b_im', 'm_ssm_c_re', 'm_ssm_c_im', 'm_ssm_d', 'm_ssm_w_glu', 'm_ssm_b_glu', 'm_ssm_w_proj', 'm_conv_w_dw', 'm_conv_b_dw', 'm_conv_ln_g', 'm_conv_ln_b', 'm_conv_w_proj', 'm_pool_w_group', 'm_pool_scale', 'm_pool_w_proj', 'm_w_out', 'm_norm2', 'm_ffn_w_gate', 'm_ffn_w_up', 'm_ffn_w_down', 'm_final_norm', 'v_norm1', 'v_w_in', 'v_b_gate', 'v_ssm_a_re', 'v_ssm_a_im', 'v_ssm_log_dt', 'v_ssm_b_re', 'v_ssm_b_im', 'v_ssm_c_re', 'v_ssm_c_im', 'v_ssm_d', 'v_ssm_w_glu', 'v_ssm_b_glu', 'v_ssm_w_proj', 'v_conv_w_dw', 'v_conv_b_dw', 'v_conv_ln_g', 'v_conv_ln_b', 'v_conv_w_proj', 'v_pool_w_group', 'v_pool_scale', 'v_pool_w_proj', 'v_w_out', 'v_norm2', 'v_ffn_w_gate', 'v_ffn_w_up', 'v_ffn_w_down', 'v_final_norm']
TWIN_OUTPUTS = ['loss', 'grad_x', 'grad_norm1', 'grad_w_in', 'grad_b_gate', 'grad_ssm_a_re', 'grad_ssm_a_im', 'grad_ssm_log_dt', 'grad_ssm_b_re', 'grad_ssm_b_im', 'grad_ssm_c_re', 'grad_ssm_c_im', 'grad_ssm_d', 'grad_ssm_w_glu', 'grad_ssm_b_glu', 'grad_ssm_w_proj', 'grad_conv_w_dw', 'grad_conv_b_dw', 'grad_conv_ln_g', 'grad_conv_ln_b', 'grad_conv_w_proj', 'grad_pool_w_group', 'grad_pool_scale', 'grad_pool_w_proj', 'grad_w_out', 'grad_norm2', 'grad_ffn_w_gate', 'grad_ffn_w_up', 'grad_ffn_w_down', 'grad_final_norm', 'delta_norm1', 'delta_w_in', 'delta_b_gate', 'delta_ssm_a_re', 'delta_ssm_a_im', 'delta_ssm_log_dt', 'delta_ssm_b_re', 'delta_ssm_b_im', 'delta_ssm_c_re', 'delta_ssm_c_im', 'delta_ssm_d', 'delta_ssm_w_glu', 'delta_ssm_b_glu', 'delta_ssm_w_proj', 'delta_conv_w_dw', 'delta_conv_b_dw', 'delta_conv_ln_g', 'delta_conv_ln_b', 'delta_conv_w_proj', 'delta_pool_w_group', 'delta_pool_scale', 'delta_pool_w_proj', 'delta_w_out', 'delta_norm2', 'delta_ffn_w_gate', 'delta_ffn_w_up', 'delta_ffn_w_down', 'delta_final_norm', 'new_m_norm1', 'new_m_w_in', 'new_m_b_gate', 'new_m_ssm_a_re', 'new_m_ssm_a_im', 'new_m_ssm_log_dt', 'new_m_ssm_b_re', 'new_m_ssm_b_im', 'new_m_ssm_c_re', 'new_m_ssm_c_im', 'new_m_ssm_d', 'new_m_ssm_w_glu', 'new_m_ssm_b_glu', 'new_m_ssm_w_proj', 'new_m_conv_w_dw', 'new_m_conv_b_dw', 'new_m_conv_ln_g', 'new_m_conv_ln_b', 'new_m_conv_w_proj', 'new_m_pool_w_group', 'new_m_pool_scale', 'new_m_pool_w_proj', 'new_m_w_out', 'new_m_norm2', 'new_m_ffn_w_gate', 'new_m_ffn_w_up', 'new_m_ffn_w_down', 'new_m_final_norm', 'new_v_norm1', 'new_v_w_in', 'new_v_b_gate', 'new_v_ssm_a_re', 'new_v_ssm_a_im', 'new_v_ssm_log_dt', 'new_v_ssm_b_re', 'new_v_ssm_b_im', 'new_v_ssm_c_re', 'new_v_ssm_c_im', 'new_v_ssm_d', 'new_v_ssm_w_glu', 'new_v_ssm_b_glu', 'new_v_ssm_w_proj', 'new_v_conv_w_dw', 'new_v_conv_b_dw', 'new_v_conv_ln_g', 'new_v_conv_ln_b', 'new_v_conv_w_proj', 'new_v_pool_w_group', 'new_v_pool_scale', 'new_v_pool_w_proj', 'new_v_w_out', 'new_v_norm2', 'new_v_ffn_w_gate', 'new_v_ffn_w_up', 'new_v_ffn_w_down', 'new_v_final_norm']
TWIN_LEAF_KINDS = {'loss': 'loss', 'grad_x': 'grad_x', 'grad_norm1': 'grad_w', 'grad_w_in': 'grad_w', 'grad_b_gate': 'grad_w', 'grad_ssm_a_re': 'grad_w', 'grad_ssm_a_im': 'grad_w', 'grad_ssm_log_dt': 'grad_w', 'grad_ssm_b_re': 'grad_w', 'grad_ssm_b_im': 'grad_w', 'grad_ssm_c_re': 'grad_w', 'grad_ssm_c_im': 'grad_w', 'grad_ssm_d': 'grad_w', 'grad_ssm_w_glu': 'grad_w', 'grad_ssm_b_glu': 'grad_w', 'grad_ssm_w_proj': 'grad_w', 'grad_conv_w_dw': 'grad_w', 'grad_conv_b_dw': 'grad_w', 'grad_conv_ln_g': 'grad_w', 'grad_conv_ln_b': 'grad_w', 'grad_conv_w_proj': 'grad_w', 'grad_pool_w_group': 'grad_w', 'grad_pool_scale': 'grad_w', 'grad_pool_w_proj': 'grad_w', 'grad_w_out': 'grad_w', 'grad_norm2': 'grad_w', 'grad_ffn_w_gate': 'grad_w', 'grad_ffn_w_up': 'grad_w', 'grad_ffn_w_down': 'grad_w', 'grad_final_norm': 'grad_w', 'delta_norm1': 'delta_w', 'delta_w_in': 'delta_w', 'delta_b_gate': 'delta_w', 'delta_ssm_a_re': 'delta_w', 'delta_ssm_a_im': 'delta_w', 'delta_ssm_log_dt': 'delta_w', 'delta_ssm_b_re': 'delta_w', 'delta_ssm_b_im': 'delta_w', 'delta_ssm_c_re': 'delta_w', 'delta_ssm_c_im': 'delta_w', 'delta_ssm_d': 'delta_w', 'delta_ssm_w_glu': 'delta_w', 'delta_ssm_b_glu': 'delta_w', 'delta_ssm_w_proj': 'delta_w', 'delta_conv_w_dw': 'delta_w', 'delta_conv_b_dw': 'delta_w', 'delta_conv_ln_g': 'delta_w', 'delta_conv_ln_b': 'delta_w', 'delta_conv_w_proj': 'delta_w', 'delta_pool_w_group': 'delta_w', 'delta_pool_scale': 'delta_w', 'delta_pool_w_proj': 'delta_w', 'delta_w_out': 'delta_w', 'delta_norm2': 'delta_w', 'delta_ffn_w_gate': 'delta_w', 'delta_ffn_w_up': 'delta_w', 'delta_ffn_w_down': 'delta_w', 'delta_final_norm': 'delta_w', 'new_m_norm1': 'new_m', 'new_m_w_in': 'new_m', 'new_m_b_gate': 'new_m', 'new_m_ssm_a_re': 'new_m', 'new_m_ssm_a_im': 'new_m', 'new_m_ssm_log_dt': 'new_m', 'new_m_ssm_b_re': 'new_m', 'new_m_ssm_b_im': 'new_m', 'new_m_ssm_c_re': 'new_m', 'new_m_ssm_c_im': 'new_m', 'new_m_ssm_d': 'new_m', 'new_m_ssm_w_glu': 'new_m', 'new_m_ssm_b_glu': 'new_m', 'new_m_ssm_w_proj': 'new_m', 'new_m_conv_w_dw': 'new_m', 'new_m_conv_b_dw': 'new_m', 'new_m_conv_ln_g': 'new_m', 'new_m_conv_ln_b': 'new_m', 'new_m_conv_w_proj': 'new_m', 'new_m_pool_w_group': 'new_m', 'new_m_pool_scale': 'new_m', 'new_m_pool_w_proj': 'new_m', 'new_m_w_out': 'new_m', 'new_m_norm2': 'new_m', 'new_m_ffn_w_gate': 'new_m', 'new_m_ffn_w_up': 'new_m', 'new_m_ffn_w_down': 'new_m', 'new_m_final_norm': 'new_m', 'new_v_norm1': 'new_v', 'new_v_w_in': 'new_v', 'new_v_b_gate': 'new_v', 'new_v_ssm_a_re': 'new_v', 'new_v_ssm_a_im': 'new_v', 'new_v_ssm_log_dt': 'new_v', 'new_v_ssm_b_re': 'new_v', 'new_v_ssm_b_im': 'new_v', 'new_v_ssm_c_re': 'new_v', 'new_v_ssm_c_im': 'new_v', 'new_v_ssm_d': 'new_v', 'new_v_ssm_w_glu': 'new_v', 'new_v_ssm_b_glu': 'new_v', 'new_v_ssm_w_proj': 'new_v', 'new_v_conv_w_dw': 'new_v', 'new_v_conv_b_dw': 'new_v', 'new_v_conv_ln_g': 'new_v', 'new_v_conv_ln_b': 'new_v', 'new_v_conv_w_proj': 'new_v', 'new_v_pool_w_group': 'new_v', 'new_v_pool_scale': 'new_v', 'new_v_pool_w_proj': 'new_v', 'new_v_w_out': 'new_v', 'new_v_norm2': 'new_v', 'new_v_ffn_w_gate': 'new_v', 'new_v_ffn_w_up': 'new_v', 'new_v_ffn_w_down': 'new_v', 'new_v_final_norm': 'new_v'}


def _forward(args):
    return _fwd_reference(*[args[k] for k in FWD_PARAMS])


def _output_shape():
    out = _jax.eval_shape(lambda: _forward(_fwd_setup_inputs(0)))
    return out.shape, out.dtype

N_MICROBATCH = 1
ADAM_LR = 0.001
ADAM_B1 = 0.9
ADAM_B2 = 0.999
ADAM_EPS = 1e-08
ADAM_WD = 0.01
ADAM_STEP = 10
PER_EXAMPLE_BATCH_AXIS = {'x': 0, 'loss_target': 0}
SHARED_INPUTS = []
_WEIGHT_DTYPES = {'norm1': _jnp.float32, 'w_in': _jnp.float32, 'b_gate': _jnp.float32, 'ssm_a_re': _jnp.float32, 'ssm_a_im': _jnp.float32, 'ssm_log_dt': _jnp.float32, 'ssm_b_re': _jnp.float32, 'ssm_b_im': _jnp.float32, 'ssm_c_re': _jnp.float32, 'ssm_c_im': _jnp.float32, 'ssm_d': _jnp.float32, 'ssm_w_glu': _jnp.float32, 'ssm_b_glu': _jnp.float32, 'ssm_w_proj': _jnp.float32, 'conv_w_dw': _jnp.float32, 'conv_b_dw': _jnp.float32, 'conv_ln_g': _jnp.float32, 'conv_ln_b': _jnp.float32, 'conv_w_proj': _jnp.float32, 'pool_w_group': _jnp.float32, 'pool_scale': _jnp.float32, 'pool_w_proj': _jnp.float32, 'w_out': _jnp.float32, 'norm2': _jnp.float32, 'ffn_w_gate': _jnp.float32, 'ffn_w_up': _jnp.float32, 'ffn_w_down': _jnp.float32, 'final_norm': _jnp.float32}
MOMENT_SCALE = {'norm1': 7.725712e-02, 'w_in': 3.414486e-02, 'b_gate': 1.598475e-02, 'ssm_a_re': 6.458475e-03, 'ssm_a_im': 6.633910e-03, 'ssm_log_dt': 4.110025e+00, 'ssm_b_re': 4.389571e-03, 'ssm_b_im': 4.244869e-03, 'ssm_c_re': 2.180384e-03, 'ssm_c_im': 2.146873e-03, 'ssm_d': 3.369441e-02, 'ssm_w_glu': 9.528739e-03, 'ssm_b_glu': 1.412661e-02, 'ssm_w_proj': 2.198707e-02, 'conv_w_dw': 5.396597e-02, 'conv_b_dw': 1.130382e-01, 'conv_ln_g': 6.167125e-02, 'conv_ln_b': 5.555352e-02, 'conv_w_proj': 3.669039e-02, 'pool_w_group': 7.645099e-02, 'pool_scale': 7.443476e-02, 'pool_w_proj': 5.381478e-02, 'w_out': 6.876872e-02, 'norm2': 8.388168e-02, 'ffn_w_gate': 3.463064e-02, 'ffn_w_up': 3.350759e-02, 'ffn_w_down': 5.556729e-02, 'final_norm': 1.599735e+01}


def _to_microbatches(a, axis):
    t = _jnp.moveaxis(a, axis, 0)
    t = t.reshape((N_MICROBATCH, t.shape[0] // N_MICROBATCH) + t.shape[1:])
    return _jnp.moveaxis(t, 1, axis + 1)


def setup_inputs(seed: int = 0) -> dict:
    inp = _fwd_setup_inputs(seed)
    key = _jax.random.fold_in(_jax.random.key(seed), 7919)
    shape, _ = _output_shape()
    out = dict(inp)
    out["loss_target"] = _jax.random.normal(_jax.random.fold_in(key, 0), shape, _jnp.float32)
    for i, name in enumerate(TWIN_WEIGHTS):
        w = inp[name].astype(_jnp.float32)
        if MOMENT_SCALE is None:
            s = _jnp.sqrt(_jnp.mean(_jnp.square(w)) + 1e-30)
        else:
            s = MOMENT_SCALE[name]
        km, kv = _jax.random.split(_jax.random.fold_in(key, i + 1))
        out[name] = w
        out["m_" + name] = s * _jax.random.normal(km, w.shape, _jnp.float32)
        out["v_" + name] = (s * s) * _jax.random.uniform(kv, w.shape, _jnp.float32, 0.5, 1.5)
    if N_MICROBATCH > 1:
        for name, axis in PER_EXAMPLE_BATCH_AXIS.items():
            out[name] = _to_microbatches(out[name], axis)
    return {'x': out['x'], 'norm1': out['norm1'], 'w_in': out['w_in'], 'b_gate': out['b_gate'], 'ssm_a_re': out['ssm_a_re'], 'ssm_a_im': out['ssm_a_im'], 'ssm_log_dt': out['ssm_log_dt'], 'ssm_b_re': out['ssm_b_re'], 'ssm_b_im': out['ssm_b_im'], 'ssm_c_re': out['ssm_c_re'], 'ssm_c_im': out['ssm_c_im'], 'ssm_d': out['ssm_d'], 'ssm_w_glu': out['ssm_w_glu'], 'ssm_b_glu': out['ssm_b_glu'], 'ssm_w_proj': out['ssm_w_proj'], 'conv_w_dw': out['conv_w_dw'], 'conv_b_dw': out['conv_b_dw'], 'conv_ln_g': out['conv_ln_g'], 'conv_ln_b': out['conv_ln_b'], 'conv_w_proj': out['conv_w_proj'], 'pool_w_group': out['pool_w_group'], 'pool_scale': out['pool_scale'], 'pool_w_proj': out['pool_w_proj'], 'w_out': out['w_out'], 'norm2': out['norm2'], 'ffn_w_gate': out['ffn_w_gate'], 'ffn_w_up': out['ffn_w_up'], 'ffn_w_down': out['ffn_w_down'], 'final_norm': out['final_norm'], 'loss_target': out['loss_target'], 'm_norm1': out['m_norm1'], 'm_w_in': out['m_w_in'], 'm_b_gate': out['m_b_gate'], 'm_ssm_a_re': out['m_ssm_a_re'], 'm_ssm_a_im': out['m_ssm_a_im'], 'm_ssm_log_dt': out['m_ssm_log_dt'], 'm_ssm_b_re': out['m_ssm_b_re'], 'm_ssm_b_im': out['m_ssm_b_im'], 'm_ssm_c_re': out['m_ssm_c_re'], 'm_ssm_c_im': out['m_ssm_c_im'], 'm_ssm_d': out['m_ssm_d'], 'm_ssm_w_glu': out['m_ssm_w_glu'], 'm_ssm_b_glu': out['m_ssm_b_glu'], 'm_ssm_w_proj': out['m_ssm_w_proj'], 'm_conv_w_dw': out['m_conv_w_dw'], 'm_conv_b_dw': out['m_conv_b_dw'], 'm_conv_ln_g': out['m_conv_ln_g'], 'm_conv_ln_b': out['m_conv_ln_b'], 'm_conv_w_proj': out['m_conv_w_proj'], 'm_pool_w_group': out['m_pool_w_group'], 'm_pool_scale': out['m_pool_scale'], 'm_pool_w_proj': out['m_pool_w_proj'], 'm_w_out': out['m_w_out'], 'm_norm2': out['m_norm2'], 'm_ffn_w_gate': out['m_ffn_w_gate'], 'm_ffn_w_up': out['m_ffn_w_up'], 'm_ffn_w_down': out['m_ffn_w_down'], 'm_final_norm': out['m_final_norm'], 'v_norm1': out['v_norm1'], 'v_w_in': out['v_w_in'], 'v_b_gate': out['v_b_gate'], 'v_ssm_a_re': out['v_ssm_a_re'], 'v_ssm_a_im': out['v_ssm_a_im'], 'v_ssm_log_dt': out['v_ssm_log_dt'], 'v_ssm_b_re': out['v_ssm_b_re'], 'v_ssm_b_im': out['v_ssm_b_im'], 'v_ssm_c_re': out['v_ssm_c_re'], 'v_ssm_c_im': out['v_ssm_c_im'], 'v_ssm_d': out['v_ssm_d'], 'v_ssm_w_glu': out['v_ssm_w_glu'], 'v_ssm_b_glu': out['v_ssm_b_glu'], 'v_ssm_w_proj': out['v_ssm_w_proj'], 'v_conv_w_dw': out['v_conv_w_dw'], 'v_conv_b_dw': out['v_conv_b_dw'], 'v_conv_ln_g': out['v_conv_ln_g'], 'v_conv_ln_b': out['v_conv_ln_b'], 'v_conv_w_proj': out['v_conv_w_proj'], 'v_pool_w_group': out['v_pool_w_group'], 'v_pool_scale': out['v_pool_scale'], 'v_pool_w_proj': out['v_pool_w_proj'], 'v_w_out': out['v_w_out'], 'v_norm2': out['v_norm2'], 'v_ffn_w_gate': out['v_ffn_w_gate'], 'v_ffn_w_up': out['v_ffn_w_up'], 'v_ffn_w_down': out['v_ffn_w_down'], 'v_final_norm': out['v_final_norm']}


def _loss(weights, diff, rest, loss_target):
    with _jax.named_scope("forward"):
        args = {**rest, TWIN_DIFF_INPUT: diff, **{k: w.astype(_WEIGHT_DTYPES[k]) for k, w in weights.items()}}
        y = _forward(args)
    with _jax.named_scope("loss_head"):
        err = _jnp.square(y.astype(_jnp.float32) - loss_target)
        return 0.5 * _jnp.sum(_jnp.mean(err, axis=-1)) if err.ndim else 0.5 * err


def _adamw(w, g, m, v):
    m = ADAM_B1 * m + (1.0 - ADAM_B1) * g
    v = ADAM_B2 * v + (1.0 - ADAM_B2) * _jnp.square(g)
    m_hat = m / (1.0 - ADAM_B1 ** ADAM_STEP)
    v_hat = v / (1.0 - ADAM_B2 ** ADAM_STEP)
    delta = -ADAM_LR * (m_hat / (_jnp.sqrt(v_hat) + ADAM_EPS) + ADAM_WD * w)
    return delta, m, v


def reference(x, norm1, w_in, b_gate, ssm_a_re, ssm_a_im, ssm_log_dt, ssm_b_re, ssm_b_im, ssm_c_re, ssm_c_im, ssm_d, ssm_w_glu, ssm_b_glu, ssm_w_proj, conv_w_dw, conv_b_dw, conv_ln_g, conv_ln_b, conv_w_proj, pool_w_group, pool_scale, pool_w_proj, w_out, norm2, ffn_w_gate, ffn_w_up, ffn_w_down, final_norm, loss_target, m_norm1, m_w_in, m_b_gate, m_ssm_a_re, m_ssm_a_im, m_ssm_log_dt, m_ssm_b_re, m_ssm_b_im, m_ssm_c_re, m_ssm_c_im, m_ssm_d, m_ssm_w_glu, m_ssm_b_glu, m_ssm_w_proj, m_conv_w_dw, m_conv_b_dw, m_conv_ln_g, m_conv_ln_b, m_conv_w_proj, m_pool_w_group, m_pool_scale, m_pool_w_proj, m_w_out, m_norm2, m_ffn_w_gate, m_ffn_w_up, m_ffn_w_down, m_final_norm, v_norm1, v_w_in, v_b_gate, v_ssm_a_re, v_ssm_a_im, v_ssm_log_dt, v_ssm_b_re, v_ssm_b_im, v_ssm_c_re, v_ssm_c_im, v_ssm_d, v_ssm_w_glu, v_ssm_b_glu, v_ssm_w_proj, v_conv_w_dw, v_conv_b_dw, v_conv_ln_g, v_conv_ln_b, v_conv_w_proj, v_pool_w_group, v_pool_scale, v_pool_w_proj, v_w_out, v_norm2, v_ffn_w_gate, v_ffn_w_up, v_ffn_w_down, v_final_norm):
    given = dict(x=x, norm1=norm1, w_in=w_in, b_gate=b_gate, ssm_a_re=ssm_a_re, ssm_a_im=ssm_a_im, ssm_log_dt=ssm_log_dt, ssm_b_re=ssm_b_re, ssm_b_im=ssm_b_im, ssm_c_re=ssm_c_re, ssm_c_im=ssm_c_im, ssm_d=ssm_d, ssm_w_glu=ssm_w_glu, ssm_b_glu=ssm_b_glu, ssm_w_proj=ssm_w_proj, conv_w_dw=conv_w_dw, conv_b_dw=conv_b_dw, conv_ln_g=conv_ln_g, conv_ln_b=conv_ln_b, conv_w_proj=conv_w_proj, pool_w_group=pool_w_group, pool_scale=pool_scale, pool_w_proj=pool_w_proj, w_out=w_out, norm2=norm2, ffn_w_gate=ffn_w_gate, ffn_w_up=ffn_w_up, ffn_w_down=ffn_w_down, final_norm=final_norm, loss_target=loss_target, m_norm1=m_norm1, m_w_in=m_w_in, m_b_gate=m_b_gate, m_ssm_a_re=m_ssm_a_re, m_ssm_a_im=m_ssm_a_im, m_ssm_log_dt=m_ssm_log_dt, m_ssm_b_re=m_ssm_b_re, m_ssm_b_im=m_ssm_b_im, m_ssm_c_re=m_ssm_c_re, m_ssm_c_im=m_ssm_c_im, m_ssm_d=m_ssm_d, m_ssm_w_glu=m_ssm_w_glu, m_ssm_b_glu=m_ssm_b_glu, m_ssm_w_proj=m_ssm_w_proj, m_conv_w_dw=m_conv_w_dw, m_conv_b_dw=m_conv_b_dw, m_conv_ln_g=m_conv_ln_g, m_conv_ln_b=m_conv_ln_b, m_conv_w_proj=m_conv_w_proj, m_pool_w_group=m_pool_w_group, m_pool_scale=m_pool_scale, m_pool_w_proj=m_pool_w_proj, m_w_out=m_w_out, m_norm2=m_norm2, m_ffn_w_gate=m_ffn_w_gate, m_ffn_w_up=m_ffn_w_up, m_ffn_w_down=m_ffn_w_down, m_final_norm=m_final_norm, v_norm1=v_norm1, v_w_in=v_w_in, v_b_gate=v_b_gate, v_ssm_a_re=v_ssm_a_re, v_ssm_a_im=v_ssm_a_im, v_ssm_log_dt=v_ssm_log_dt, v_ssm_b_re=v_ssm_b_re, v_ssm_b_im=v_ssm_b_im, v_ssm_c_re=v_ssm_c_re, v_ssm_c_im=v_ssm_c_im, v_ssm_d=v_ssm_d, v_ssm_w_glu=v_ssm_w_glu, v_ssm_b_glu=v_ssm_b_glu, v_ssm_w_proj=v_ssm_w_proj, v_conv_w_dw=v_conv_w_dw, v_conv_b_dw=v_conv_b_dw, v_conv_ln_g=v_conv_ln_g, v_conv_ln_b=v_conv_ln_b, v_conv_w_proj=v_conv_w_proj, v_pool_w_group=v_pool_w_group, v_pool_scale=v_pool_scale, v_pool_w_proj=v_pool_w_proj, v_w_out=v_w_out, v_norm2=v_norm2, v_ffn_w_gate=v_ffn_w_gate, v_ffn_w_up=v_ffn_w_up, v_ffn_w_down=v_ffn_w_down, v_final_norm=v_final_norm)
    weights = {n: given[n] for n in TWIN_WEIGHTS}
    shared = {n: given[n] for n in SHARED_INPUTS}
    per_example = {n: given[n] for n in ['x']}
    grad_fn = _jax.value_and_grad(_loss, argnums=(0, 1))

    def one_microbatch(ex, loss_target):
        ex = dict(ex)
        diff = ex.pop(TWIN_DIFF_INPUT)
        return grad_fn(weights, diff, {**shared, **ex}, loss_target)

    if N_MICROBATCH == 1:
        loss, (grad_w, grad_x) = one_microbatch(per_example, given["loss_target"])
    else:
        def body(carry, xs):
            loss_sum, grad_sum = carry
            l_k, (gw_k, gx_k) = one_microbatch(xs[0], xs[1])
            with _jax.named_scope("update"):
                return (loss_sum + l_k, _jax.tree.map(_jnp.add, grad_sum, gw_k)), gx_k

        init = (_jnp.zeros((), _jnp.float32), _jax.tree.map(_jnp.zeros_like, weights))
        (loss, grad_w), grad_x = _jax.lax.scan(body, init, (per_example, given["loss_target"]))
    with _jax.named_scope("update"):
        delta_w, new_m, new_v = {}, {}, {}
        for n in TWIN_WEIGHTS:
            delta_w[n], new_m[n], new_v[n] = _adamw(weights[n], grad_w[n], given["m_" + n], given["v_" + n])
    return (loss, grad_x, *[grad_w[n] for n in TWIN_WEIGHTS], *[delta_w[n] for n in TWIN_WEIGHTS],
            *[new_m[n] for n in TWIN_WEIGHTS], *[new_v[n] for n in TWIN_WEIGHTS])
```

```python
import functools

import jax
import jax.numpy as jnp
from jax import lax
from jax.experimental import pallas as pl
from jax.experimental.pallas import tpu as pltpu

F32 = jnp.float32
BF16 = jnp.bfloat16

NDEV = 8
DEPTH = 2
D = 1024
BW = 512
NSTATE = 64
SGRP = 16
NGRP = BW // SGRP
GB = 8
NBLK = NGRP // GB
NS = GB * NSTATE
CONV_K = 31
HALO = 32
PHALO = 16
IN_W = 5120
HID = 2816
HSH = HID // NDEV
HPAD = 384
HIDP = HPAD * NDEV
EPS = 1e-6
VMEM_LIMIT = 56 * 1024 * 1024

ADAM_LR, ADAM_B1, ADAM_B2, ADAM_EPS, ADAM_WD, ADAM_STEP = 0.001, 0.9, 0.999, 1e-08, 0.01, 10

MESH = pl.DeviceIdType.MESH
ANY = pl.BlockSpec(memory_space=pl.ANY)


def _call(body, **kw):
    return pl.pallas_call(body, **kw)


def _cparams(sem=None):
    return pltpu.CompilerParams(dimension_semantics=sem, vmem_limit_bytes=VMEM_LIMIT)


def _dot(a, b):
    return jnp.dot(a.astype(BF16), b.astype(BF16), preferred_element_type=F32)


def _dot_nt(a, b):
    return lax.dot_general(a.astype(BF16), b.astype(BF16), (((1,), (1,)), ((), ())), preferred_element_type=F32)


def _dot_tn(a, b):
    return lax.dot_general(a.astype(BF16), b.astype(BF16), (((0,), (0,)), ((), ())), preferred_element_type=F32)


@jax.custom_vjp
def _mm(a, w):
    return _dot(a, w)


def _mm_fwd(a, w):
    return _dot(a, w), w


def _mm_bwd(w, ct):
    return _dot_nt(ct, w), jnp.zeros_like(w)


_mm.defvjp(_mm_fwd, _mm_bwd)


def _rms(x, g):
    return x * lax.rsqrt(jnp.mean(x * x, axis=-1, keepdims=True) + EPS) * g


def _disc(are, aim, ldt):
    dt = jnp.exp(ldt)
    mag = jnp.exp(dt * are)
    ang = dt * aim
    abr = mag * jnp.cos(ang)
    abi = mag * jnp.sin(ang)
    den = are * are + aim * aim
    nr = abr - 1.0
    fr = (nr * are + abi * aim) / den
    fi = (abi * are - nr * aim) / den
    return abr, abi, fr, fi


def _bbar(fr, fi, br, bi):
    return fr * br - fi * bi, fr * bi + fi * br


def _cmul(ar, ai, br, bi):
    return ar * br - ai * bi, ar * bi + ai * br


def _scan_rows(re_ref, im_ref, ar, ai, n_rows, reverse, hre_ref=None, him_ref=None):
    n = ar.shape[1]
    shape = (8, n)
    rows = lax.broadcasted_iota(jnp.int32, shape, 0)
    a1 = (jnp.broadcast_to(ar, shape), jnp.broadcast_to(ai, shape))
    a2 = _cmul(*a1, *a1)
    a4 = _cmul(*a2, *a2)
    pr = jnp.zeros(shape, F32)
    pi = jnp.zeros(shape, F32)
    pw = a1
    for k in range(8):
        sel = rows == ((7 - k) if reverse else k)
        pr = jnp.where(sel, pw[0], pr)
        pi = jnp.where(sel, pw[1], pi)
        pw = _cmul(*pw, *a1)
    nt = n_rows // 8
    with_acc = hre_ref is not None

    def body(i, carry):
        cr, ci = carry[0], carry[1]
        t = (nt - 1 - i) if reverse else i
        off = pl.multiple_of(t * 8, 8)
        xr = re_ref[pl.ds(off, 8), :]
        xi = im_ref[pl.ds(off, 8), :]
        for k, (kr, ki) in ((1, a1), (2, a2), (4, a4)):
            if reverse:
                keep, sh = rows < 8 - k, 8 - k
            else:
                keep, sh = rows >= k, k
            sr = jnp.where(keep, pltpu.roll(xr, sh, 0), 0.0)
            si = jnp.where(keep, pltpu.roll(xi, sh, 0), 0.0)
            xr, xi = xr + kr * sr - ki * si, xi + kr * si + ki * sr
        xr, xi = xr + pr * cr - pi * ci, xi + pr * ci + pi * cr
        re_ref[pl.ds(off, 8), :] = xr
        im_ref[pl.ds(off, 8), :] = xi
        edge = 0 if reverse else 7
        out = (jnp.broadcast_to(xr[edge:edge + 1, :], shape), jnp.broadcast_to(xi[edge:edge + 1, :], shape))
        if with_acc:
            hr = hre_ref[pl.ds(off, 8), :]
            hi = him_ref[pl.ds(off, 8), :]
            offp = pl.multiple_of(jnp.maximum(t - 1, 0) * 8, 8)
            live = jnp.where(t > 0, 1.0, 0.0)
            lr = jnp.broadcast_to(hre_ref[pl.ds(offp, 8), :][7:8, :], shape) * live
            li = jnp.broadcast_to(him_ref[pl.ds(offp, 8), :][7:8, :], shape) * live
            hpr = jnp.where(rows == 0, lr, pltpu.roll(hr, 1, 0))
            hpi = jnp.where(rows == 0, li, pltpu.roll(hi, 1, 0))
            out = out + (carry[2] + xr * hpr + xi * hpi, carry[3] + xi * hpr - xr * hpi)
        return out

    zero = jnp.zeros(shape, F32)
    init = (zero, zero, zero, zero) if with_acc else (zero, zero)
    res = lax.fori_loop(0, nt, body, init)
    return res[2:] if with_acc else None


def _inproj_fwd(x, gamma, w, tm=512, tn=1280):
    t_rows = x.shape[0]
    n = w.shape[1]

    def body(x_ref, g_ref, w_ref, z_ref, h_ref):
        @pl.when(pl.program_id(1) == 0)
        def _():
            h_ref[...] = _rms(x_ref[...], g_ref[...]).astype(BF16)
        z_ref[...] = jnp.dot(h_ref[...], w_ref[...], preferred_element_type=F32)

    return _call(
        body, name="inproj_fwd", grid=(t_rows // tm, n // tn),
        in_specs=[pl.BlockSpec((tm, D), lambda i, j: (i, 0)), pl.BlockSpec((1, D), lambda i, j: (0, 0)),
                  pl.BlockSpec((D, tn), lambda i, j: (0, j))],
        out_specs=pl.BlockSpec((tm, tn), lambda i, j: (i, j)),
        out_shape=jax.ShapeDtypeStruct((t_rows, n), F32),
        scratch_shapes=[pltpu.VMEM((tm, D), BF16)],
        compiler_params=_cparams(("parallel", "arbitrary")))(x, gamma, w)


def _ssm_specs(t_rows):
    row = pl.BlockSpec((1, NS), lambda j: (0, j))
    return dict(
        u=pl.BlockSpec((t_rows, GB * SGRP), lambda j: (0, j)),
        row=row,
        bexp=pl.BlockSpec((None, GB * SGRP, NS), lambda j: (j, 0, 0)),
        cexp=pl.BlockSpec((None, NS, GB * SGRP), lambda j: (j, 0, 0)),
        d=pl.BlockSpec((1, GB * SGRP), lambda j: (0, j)),
        h=pl.BlockSpec((t_rows, NS), lambda j: (0, j)),
    )


def _ssm_fwd(z, p):
    t_rows = z.shape[0]
    s = _ssm_specs(t_rows)

    def body(u_ref, are_ref, aim_ref, ldt_ref, br_ref, bi_ref, cr_ref, ci_ref, d_ref, y_ref, hr_ref, hi_ref):
        abr, abi, fr, fi = _disc(are_ref[...], aim_ref[...], ldt_ref[...])
        bbr, bbi = _bbar(fr, fi, br_ref[...], bi_ref[...])
        u = u_ref[...]
        hr_ref[...] = _dot(u, bbr)
        hi_ref[...] = _dot(u, bbi)
        _scan_rows(hr_ref, hi_ref, abr, abi, t_rows, False)
        y_ref[...] = _dot(hr_ref[...], cr_ref[...]) - _dot(hi_ref[...], ci_ref[...]) + d_ref[...] * u

    return _call(
        body, name="ssm_fwd", grid=(NBLK,),
        in_specs=[s["u"], s["row"], s["row"], s["row"], s["bexp"], s["bexp"], s["cexp"], s["cexp"], s["d"]],
        out_specs=[s["u"], s["h"], s["h"]],
        out_shape=[jax.ShapeDtypeStruct((t_rows, BW), F32), jax.ShapeDtypeStruct((t_rows, NGRP * NSTATE), F32),
                   jax.ShapeDtypeStruct((t_rows, NGRP * NSTATE), F32)],
        compiler_params=_cparams(("parallel",)))(
            z, p["are"], p["aim"], p["ldt"], p["bexp_re"], p["bexp_im"], p["cexp_re"], p["cexp_im"], p["dskip"])


def _ssm_bwd(dy, z, hre, him, p):
    t_rows = z.shape[0]
    s = _ssm_specs(t_rows)
    nstates = NGRP * NSTATE

    def body(dy_ref, u_ref, hr_ref, hi_ref, are_ref, aim_ref, ldt_ref, br_ref, bi_ref, cr_ref, ci_ref, d_ref,
             du_ref, dbr_ref, dbi_ref, dcr_ref, dci_ref, dd_ref, dar_ref, dai_ref, dldt_ref, lr_ref, li_ref):
        rows3 = (are_ref[...], aim_ref[...], ldt_ref[...])
        (abr, abi, fr, fi), disc_vjp = jax.vjp(_disc, *rows3)
        (bbr, bbi), bbar_vjp = jax.vjp(_bbar, fr, fi, br_ref[...], bi_ref[...])
        dy = dy_ref[...]
        u = u_ref[...]
        lr_ref[...] = _dot_nt(dy, cr_ref[...])
        li_ref[...] = -_dot_nt(dy, ci_ref[...])
        dcr_ref[...] = _dot_tn(hr_ref[...], dy)
        dci_ref[...] = -_dot_tn(hi_ref[...], dy)
        dd_ref[...] = jnp.sum(dy * u, axis=0, keepdims=True)
        acc_r, acc_i = _scan_rows(lr_ref, li_ref, abr, -abi, t_rows, True, hr_ref, hi_ref)
        dabr = jnp.sum(acc_r, axis=0, keepdims=True)
        dabi = jnp.sum(acc_i, axis=0, keepdims=True)
        lam_r = lr_ref[...]
        lam_i = li_ref[...]
        du = d_ref[...] * dy + _dot_nt(lam_r, bbr) + _dot_nt(lam_i, bbi)
        du_ref[...] = du.astype(BF16)
        dbbr = _dot_tn(u, lam_r)
        dbbi = _dot_tn(u, lam_i)
        dfr, dfi, dbr, dbi = bbar_vjp((dbbr, dbbi))
        dbr_ref[...] = dbr
        dbi_ref[...] = dbi
        dar, dai, dldt = disc_vjp((dabr, dabi, dfr, dfi))
        dar_ref[...] = dar
        dai_ref[...] = dai
        lane_grp = lax.broadcasted_iota(jnp.int32, (NS, 128), 0) // NSTATE
        col = lax.broadcasted_iota(jnp.int32, (NS, 128), 1)
        seg = jnp.where(lane_grp == col, 1.0, 0.0).astype(F32)
        dldt_ref[...] = jnp.dot(jnp.broadcast_to(dldt, (8, NS)), seg, preferred_element_type=F32,
                                precision=lax.Precision.HIGHEST)

    dyspec = pl.BlockSpec((t_rows, GB * SGRP), lambda j: (0, j))
    return _call(
        body, name="ssm_bwd", grid=(NBLK,),
        in_specs=[dyspec, s["u"], s["h"], s["h"], s["row"], s["row"], s["row"], s["bexp"], s["bexp"], s["cexp"],
                  s["cexp"], s["d"]],
        out_specs=[dyspec, s["bexp"], s["bexp"], s["cexp"], s["cexp"], s["d"], s["row"], s["row"],
                   pl.BlockSpec((8, 128), lambda j: (j, 0))],
        out_shape=[jax.ShapeDtypeStruct((t_rows, BW), BF16),
                   jax.ShapeDtypeStruct((NBLK, GB * SGRP, NS), F32), jax.ShapeDtypeStruct((NBLK, GB * SGRP, NS), F32),
                   jax.ShapeDtypeStruct((NBLK, NS, GB * SGRP), F32), jax.ShapeDtypeStruct((NBLK, NS, GB * SGRP), F32),
                   jax.ShapeDtypeStruct((1, BW), F32), jax.ShapeDtypeStruct((1, nstates), F32),
                   jax.ShapeDtypeStruct((1, nstates), F32), jax.ShapeDtypeStruct((NBLK * 8, 128), F32)],
        scratch_shapes=[pltpu.VMEM((t_rows, NS), F32), pltpu.VMEM((t_rows, NS), F32)],
        compiler_params=_cparams(("parallel",)))(
            dy, z, hre, him, p["are"], p["aim"], p["ldt"], p["bexp_re"], p["bexp_im"], p["cexp_re"], p["cexp_im"],
            p["dskip"])


def _conv_fwd(z, w, b, tm=256):
    t_rows = z.shape[0]
    hb = tm // HALO

    def body(va_ref, vb_ref, ha_ref, hb_ref, w_ref, b_ref, o_ref, win_ref):
        live = jnp.where(pl.program_id(0) > 0, 1.0, 0.0)
        win_ref[0:HALO, :] = ha_ref[...] * jax.nn.sigmoid(hb_ref[...]) * live
        win_ref[HALO:HALO + tm, :] = va_ref[...] * jax.nn.sigmoid(vb_ref[...])
        acc = jnp.broadcast_to(b_ref[...], (tm, BW))
        for k in range(CONV_K):
            acc = acc + w_ref[k:k + 1, :] * win_ref[pl.ds(HALO - (CONV_K - 1) + k, tm), :]
        o_ref[...] = acc

    halo = lambda col: pl.BlockSpec((HALO, BW), lambda i: (jnp.maximum(i * hb - 1, 0), col))
    return _call(
        body, name="conv_fwd", grid=(t_rows // tm,),
        in_specs=[pl.BlockSpec((tm, BW), lambda i: (i, 1)), pl.BlockSpec((tm, BW), lambda i: (i, 2)), halo(1), halo(2),
                  pl.BlockSpec((CONV_K, BW), lambda i: (0, 0)), pl.BlockSpec((1, BW), lambda i: (0, 0))],
        out_specs=pl.BlockSpec((tm, BW), lambda i: (i, 0)),
        out_shape=jax.ShapeDtypeStruct((t_rows, BW), F32),
        scratch_shapes=[pltpu.VMEM((HALO + tm, BW), F32)],
        compiler_params=_cparams(("parallel",)))(z, z, z, z, w, b)


def _conv_bwd(dcv, z, w, tm=256):
    t_rows = z.shape[0]
    nt = t_rows // tm
    hb = tm // HALO
    csh = BW // NDEV

    def body(d_ref, dn_ref, va_ref, vb_ref, ha_ref, hb_ref, w_ref, dva_ref, dvb_ref, dw8_ref, db_ref,
             hwin_ref, dwin_ref, dw_ref):
        i = pl.program_id(0)

        @pl.when(i == 0)
        def _():
            dw_ref[...] = jnp.zeros_like(dw_ref)
            db_ref[...] = jnp.zeros_like(db_ref)

        live_prev = jnp.where(i > 0, 1.0, 0.0)
        live_next = jnp.where(i < nt - 1, 1.0, 0.0)
        va = va_ref[...]
        sig = jax.nn.sigmoid(vb_ref[...])
        hwin_ref[0:HALO, :] = ha_ref[...] * jax.nn.sigmoid(hb_ref[...]) * live_prev
        hwin_ref[HALO:HALO + tm, :] = va * sig
        d = d_ref[...]
        dwin_ref[0:tm, :] = d
        dwin_ref[tm:tm + HALO, :] = dn_ref[...] * live_next
        dh = jnp.zeros((tm, BW), F32)
        dws = []
        for k in range(CONV_K):
            dh = dh + w_ref[k:k + 1, :] * dwin_ref[pl.ds(CONV_K - 1 - k, tm), :]
            dws.append(jnp.sum(d * hwin_ref[pl.ds(HALO - (CONV_K - 1) + k, tm), :], axis=0, keepdims=True))
        dws.append(jnp.zeros((1, BW), F32))
        dw_ref[...] += jnp.concatenate(dws, axis=0)
        db_ref[...] += jnp.sum(d, axis=0, keepdims=True)
        dva_ref[...] = (dh * sig).astype(BF16)
        dvb_ref[...] = (dh * va * sig * (1.0 - sig)).astype(BF16)

        @pl.when(i == nt - 1)
        def _():
            acc = dw_ref[...]
            for q in range(NDEV):
                dw8_ref[q] = acc[:, csh * q:csh * (q + 1)]

    halo = lambda col: pl.BlockSpec((HALO, BW), lambda i: (jnp.maximum(i * hb - 1, 0), col))
    return _call(
        body, name="conv_bwd", grid=(nt,),
        in_specs=[pl.BlockSpec((tm, BW), lambda i: (i, 0)),
                  pl.BlockSpec((HALO, BW), lambda i: (jnp.minimum((i + 1) * hb, t_rows // HALO - 1), 0)),
                  pl.BlockSpec((tm, BW), lambda i: (i, 1)), pl.BlockSpec((tm, BW), lambda i: (i, 2)), halo(1), halo(2),
                  pl.BlockSpec((CONV_K, BW), lambda i: (0, 0))],
        out_specs=[pl.BlockSpec((tm, BW), lambda i: (i, 0)), pl.BlockSpec((tm, BW), lambda i: (i, 0)),
                   pl.BlockSpec((NDEV, 32, csh), lambda i: (0, 0, 0)), pl.BlockSpec((1, BW), lambda i: (0, 0))],
        out_shape=[jax.ShapeDtypeStruct((t_rows, BW), BF16), jax.ShapeDtypeStruct((t_rows, BW), BF16),
                   jax.ShapeDtypeStruct((NDEV, 32, csh), F32), jax.ShapeDtypeStruct((1, BW), F32)],
        scratch_shapes=[pltpu.VMEM((HALO + tm, BW), F32), pltpu.VMEM((tm + HALO, BW), F32), pltpu.VMEM((32, BW), F32)],
        compiler_params=_cparams(("arbitrary",)))(dcv, dcv, z, z, z, z, w)


def _pool_rows(i, tm, n_rows, first_row):
    grp = lax.broadcasted_iota(jnp.int32, (1, BW), 1) // (BW // 4)
    wlen = jnp.where(grp == 0, 2.0, jnp.where(grp == 1, 4.0, jnp.where(grp == 2, 8.0, 16.0)))
    t = (i * tm + first_row + lax.broadcasted_iota(jnp.int32, (n_rows, 1), 0)).astype(F32)
    return grp, 1.0 / jnp.minimum(t + 1.0, wlen)


def _pool_pick(grp, s2, s4, s8, s16):
    return jnp.where(grp == 0, s2, jnp.where(grp == 1, s4, jnp.where(grp == 2, s8, s16)))


def _pool_fwd(z, tm=256):
    t_rows = z.shape[0]
    hb = tm // PHALO

    def body(u_ref, h_ref, o_ref):
        i = pl.program_id(0)
        u = u_ref[...]
        win = jnp.concatenate([h_ref[...] * jnp.where(i > 0, 1.0, 0.0), u], axis=0)
        s2 = win + pltpu.roll(win, 1, 0)
        s4 = s2 + pltpu.roll(s2, 2, 0)
        s8 = s4 + pltpu.roll(s4, 4, 0)
        s16 = s8 + pltpu.roll(s8, 8, 0)
        grp, inv = _pool_rows(i, tm, tm, 0)
        o_ref[...] = _pool_pick(grp, s2, s4, s8, s16)[PHALO:, :] * inv - u

    return _call(
        body, name="pool_fwd", grid=(t_rows // tm,),
        in_specs=[pl.BlockSpec((tm, BW), lambda i: (i, 3)),
                  pl.BlockSpec((PHALO, BW), lambda i: (jnp.maximum(i * hb - 1, 0), 3))],
        out_specs=pl.BlockSpec((tm, BW), lambda i: (i, 0)),
        out_shape=jax.ShapeDtypeStruct((t_rows, BW), F32),
        compiler_params=_cparams(("parallel",)))(z, z)


def _pool_bwd(dp, tm=256):
    t_rows = dp.shape[0]
    nt = t_rows // tm
    hb = tm // PHALO
    ln = tm + PHALO

    def body(d_ref, dn_ref, o_ref):
        i = pl.program_id(0)
        d = d_ref[...]
        grp, inv = _pool_rows(i, tm, ln, 0)
        win = jnp.concatenate([d, dn_ref[...] * jnp.where(i < nt - 1, 1.0, 0.0)], axis=0) * inv
        s2 = win + pltpu.roll(win, ln - 1, 0)
        s4 = s2 + pltpu.roll(s2, ln - 2, 0)
        s8 = s4 + pltpu.roll(s4, ln - 4, 0)
        s16 = s8 + pltpu.roll(s8, ln - 8, 0)
        o_ref[...] = (_pool_pick(grp, s2, s4, s8, s16)[:tm, :] - d).astype(BF16)

    return _call(
        body, name="pool_bwd", grid=(nt,),
        in_specs=[pl.BlockSpec((tm, BW), lambda i: (i, 0)),
                  pl.BlockSpec((PHALO, BW), lambda i: (jnp.minimum((i + 1) * hb, t_rows // PHALO - 1), 0))],
        out_specs=pl.BlockSpec((tm, BW), lambda i: (i, 0)),
        out_shape=jax.ShapeDtypeStruct((t_rows, BW), BF16),
        compiler_params=_cparams(("parallel",)))(dp, dp)


_MERGE_W = ("wglu", "bglu", "wpa", "lng", "lnb", "wpb", "wgrp", "scale", "wpc", "bgate", "wout")
_MERGE_SMALL = ("bglu", "lng", "lnb", "scale", "bgate")


def _merge_math(x, yssm, cv, pbar, zg, w, taps):
    t_glu, t_ya, t_yb, t_p, t_yc = taps
    g = jax.nn.gelu(yssm)
    outa = g * jax.nn.sigmoid(_mm(g, w["wglu"]) + t_glu + w["bglu"])
    ya = _mm(outa, w["wpa"]) + t_ya
    mu = jnp.mean(cv, axis=-1, keepdims=True)
    var = jnp.mean(jnp.square(cv - mu), axis=-1, keepdims=True)
    hs = jax.nn.silu((cv - mu) * lax.rsqrt(var + EPS) * w["lng"] + w["lnb"])
    yb = _mm(hs, w["wpb"]) + t_yb
    gw = BW // 4
    pk = jnp.concatenate([_mm(pbar[:, gw * k:gw * (k + 1)], w["wgrp"][k]) for k in range(4)], axis=1) + t_p
    pc = pk * w["scale"]
    yc = _mm(pc, w["wpc"]) + t_yc
    gates = jax.nn.sigmoid(zg + w["bgate"])
    merged = gates[:, :D] * ya + gates[:, D:2 * D] * yb + gates[:, 2 * D:] * yc
    x1 = x + _mm(merged, w["wout"])
    acts = tuple(a.astype(BF16) for a in (g, outa, hs, pbar, pc, merged))
    return x1, acts


def _merge_specs(tm, p):
    rows = lambda width, col=0: pl.BlockSpec((tm, width), lambda i, c=col: (i, c))
    data = [rows(D), rows(BW), rows(BW), rows(BW), pl.BlockSpec((tm, 3 * D), lambda i: (i, 0))]
    wspecs = []
    for name in _MERGE_W:
        nd = p[name].ndim
        wspecs.append(pl.BlockSpec(p[name].shape, lambda i, nd=nd: (0,) * nd))
    return rows, data, wspecs


def _merge_fwd(x, yssm, cv, pbar, zg, p, tm=256):
    t_rows = x.shape[0]
    rows, data, wspecs = _merge_specs(tm, p)

    def body(x_ref, y_ref, cv_ref, pb_ref, zg_ref, *rest):
        w = {name: r[...] for name, r in zip(_MERGE_W, rest[:len(_MERGE_W)])}
        o_ref = rest[len(_MERGE_W)]
        taps = (0.0, 0.0, 0.0, 0.0, 0.0)
        o_ref[...] = _merge_math(x_ref[...], y_ref[...], cv_ref[...], pb_ref[...], zg_ref[...], w, taps)[0]

    return _call(
        body, name="merge_fwd", grid=(t_rows // tm,), in_specs=data + wspecs, out_specs=rows(D),
        out_shape=jax.ShapeDtypeStruct((t_rows, D), F32),
        compiler_params=_cparams(("parallel",)))(x, yssm, cv, pbar, zg, *[p[n] for n in _MERGE_W])


def _merge_bwd(dx1, x, yssm, cv, pbar, zg, p, tm=256):
    t_rows = x.shape[0]
    rows, data, wspecs = _merge_specs(tm, p)
    nw = len(_MERGE_W)

    def body(dx_ref, x_ref, y_ref, cv_ref, pb_ref, zg_ref, *rest):
        w = {name: r[...] for name, r in zip(_MERGE_W, rest[:nw])}
        outs = rest[nw:]
        small = {n: w[n] for n in _MERGE_SMALL}
        taps = (jnp.zeros((tm, BW), F32), jnp.zeros((tm, D), F32), jnp.zeros((tm, D), F32),
                jnp.zeros((tm, BW), F32), jnp.zeros((tm, D), F32))

        def f(yssm_, cv_, pbar_, zg_, small_, taps_):
            return _merge_math(x_ref[...], yssm_, cv_, pbar_, zg_, {**w, **small_}, taps_)

        _, vjp, acts = jax.vjp(f, y_ref[...], cv_ref[...], pb_ref[...], zg_ref[...], small, taps, has_aux=True)
        dy, dcv, dpb, dzg, dsmall, dtaps = vjp(dx_ref[...])
        outs[0][...] = dy
        outs[1][...] = dcv
        outs[2][...] = dpb
        outs[3][...] = dzg.astype(BF16)
        for k in range(6):
            outs[4 + k][...] = acts[k]
        for k in range(5):
            outs[10 + k][...] = dtaps[k].astype(BF16)

        @pl.when(pl.program_id(0) == 0)
        def _():
            for k in range(5):
                outs[15 + k][...] = jnp.zeros_like(outs[15 + k])

        for k, n in enumerate(_MERGE_SMALL):
            outs[15 + k][...] += dsmall[n]

    f32o = lambda width: jax.ShapeDtypeStruct((t_rows, width), F32)
    bfo = lambda width: jax.ShapeDtypeStruct((t_rows, width), BF16)
    small_shapes = [jax.ShapeDtypeStruct(p[n].shape, F32) for n in _MERGE_SMALL]
    small_specs = [pl.BlockSpec(p[n].shape, lambda i: (0, 0)) for n in _MERGE_SMALL]
    out_shape = ([f32o(BW), f32o(BW), f32o(BW), bfo(3 * D)]
                 + [bfo(BW), bfo(BW), bfo(BW), bfo(BW), bfo(BW), bfo(D)]
                 + [bfo(BW), bfo(D), bfo(D), bfo(BW), bfo(D)] + small_shapes)
    out_specs = ([rows(BW), rows(BW), rows(BW), rows(3 * D)]
                 + [rows(BW)] * 5 + [rows(D)]
                 + [rows(BW), rows(D), rows(D), rows(BW), rows(D)] + small_specs)
    return _call(
        body, name="merge_bwd", grid=(t_rows // tm,), in_specs=[rows(D)] + data + wspecs, out_specs=out_specs,
        out_shape=out_shape, compiler_params=_cparams(("arbitrary",)))(
            dx1, x, yssm, cv, pbar, zg, *[p[n] for n in _MERGE_W])


def _ffn_fwd(x1, gamma, wg, wu, wd, tm=512, th=512):
    t_rows = x1.shape[0]
    nh = HIDP // th

    def body(x_ref, g_ref, wg_ref, wu_ref, wd_ref, o_ref, gp_ref, up_ref, h_ref, acc_ref):
        j = pl.program_id(1)

        @pl.when(j == 0)
        def _():
            h_ref[...] = _rms(x_ref[...], g_ref[...]).astype(BF16)
            acc_ref[...] = jnp.zeros_like(acc_ref)

        gp = jnp.dot(h_ref[...], wg_ref[...], preferred_element_type=F32)
        up = jnp.dot(h_ref[...], wu_ref[...], preferred_element_type=F32)
        gp_ref[...] = gp
        up_ref[...] = up
        acc_ref[...] += _dot(jax.nn.silu(gp) * up, wd_ref[...])

        @pl.when(j == nh - 1)
        def _():
            o_ref[...] = x_ref[...] + acc_ref[...]

    return _call(
        body, name="ffn_fwd", grid=(t_rows // tm, nh),
        in_specs=[pl.BlockSpec((tm, D), lambda i, j: (i, 0)), pl.BlockSpec((1, D), lambda i, j: (0, 0)),
                  pl.BlockSpec((D, th), lambda i, j: (0, j)), pl.BlockSpec((D, th), lambda i, j: (0, j)),
                  pl.BlockSpec((th, D), lambda i, j: (j, 0))],
        out_specs=[pl.BlockSpec((tm, D), lambda i, j: (i, 0)), pl.BlockSpec((tm, th), lambda i, j: (i, j)),
                   pl.BlockSpec((tm, th), lambda i, j: (i, j))],
        out_shape=[jax.ShapeDtypeStruct((t_rows, D), F32), jax.ShapeDtypeStruct((t_rows, HIDP), F32),
                   jax.ShapeDtypeStruct((t_rows, HIDP), F32)],
        scratch_shapes=[pltpu.VMEM((tm, D), BF16), pltpu.VMEM((tm, D), F32)],
        compiler_params=_cparams(("parallel", "arbitrary")))(x1, gamma, wg, wu, wd)


def _rms_bwd_tail(x, gamma, dh):
    _, vjp = jax.vjp(_rms, x, gamma)
    return vjp(dh)


def _ffn_bwd(dx2, x1, gamma, gpre, upre, wg, wu, wd, tm=512, th=512):
    t_rows = x1.shape[0]
    nh = HIDP // th

    def body(d_ref, x_ref, g_ref, gp_ref, up_ref, wg_ref, wu_ref, wd_ref,
             dx_ref, dgam_ref, dgp_ref, dup_ref, act_ref, h_ref, acc_ref):
        i = pl.program_id(0)
        j = pl.program_id(1)

        @pl.when(j == 0)
        def _():
            acc_ref[...] = jnp.zeros_like(acc_ref)

        @pl.when((i == 0) & (j == 0))
        def _():
            dgam_ref[...] = jnp.zeros_like(dgam_ref)

        dact = _dot_nt(d_ref[...], wd_ref[...])
        gp = gp_ref[...]
        up = up_ref[...]
        sg = jax.nn.sigmoid(gp)
        silu = gp * sg
        dgp = (dact * up * (sg * (1.0 + gp * (1.0 - sg)))).astype(BF16)
        dup = (dact * silu).astype(BF16)
        dgp_ref[...] = dgp
        dup_ref[...] = dup
        act_ref[...] = (silu * up).astype(BF16)
        acc_ref[...] += _dot_nt(dgp, wg_ref[...]) + _dot_nt(dup, wu_ref[...])

        @pl.when(j == nh - 1)
        def _():
            x = x_ref[...]
            h_ref[...] = _rms(x, g_ref[...]).astype(BF16)
            dx, dgam = _rms_bwd_tail(x, g_ref[...], acc_ref[...])
            dx_ref[...] = d_ref[...] + dx
            dgam_ref[...] += dgam

    row_d = pl.BlockSpec((tm, D), lambda i, j: (i, 0))
    row_h = pl.BlockSpec((tm, th), lambda i, j: (i, j))
    return _call(
        body, name="ffn_bwd", grid=(t_rows // tm, nh),
        in_specs=[row_d, row_d, pl.BlockSpec((1, D), lambda i, j: (0, 0)), row_h, row_h,
                  pl.BlockSpec((D, th), lambda i, j: (0, j)), pl.BlockSpec((D, th), lambda i, j: (0, j)),
                  pl.BlockSpec((th, D), lambda i, j: (j, 0))],
        out_specs=[row_d, pl.BlockSpec((1, D), lambda i, j: (0, 0)), row_h, row_h, row_h, row_d],
        out_shape=[jax.ShapeDtypeStruct((t_rows, D), F32), jax.ShapeDtypeStruct((1, D), F32),
                   jax.ShapeDtypeStruct((t_rows, HIDP), BF16), jax.ShapeDtypeStruct((t_rows, HIDP), BF16),
                   jax.ShapeDtypeStruct((t_rows, HIDP), BF16), jax.ShapeDtypeStruct((t_rows, D), BF16)],
        scratch_shapes=[pltpu.VMEM((tm, D), F32)],
        compiler_params=_cparams(("arbitrary", "arbitrary")))(dx2, x1, gamma, gpre, upre, wg, wu, wd)


def _inproj_bwd(dz, dx1, x, gamma, w, tm=512, tn=1280):
    t_rows = x.shape[0]
    nn = IN_W // tn

    def body(dz_ref, d1_ref, x_ref, g_ref, w_ref, dx_ref, dgam_ref, h_ref, acc_ref):
        i = pl.program_id(0)
        j = pl.program_id(1)

        @pl.when(j == 0)
        def _():
            acc_ref[...] = jnp.zeros_like(acc_ref)

        @pl.when((i == 0) & (j == 0))
        def _():
            dgam_ref[...] = jnp.zeros_like(dgam_ref)

        acc_ref[...] += _dot_nt(dz_ref[...], w_ref[...])

        @pl.when(j == nn - 1)
        def _():
            x = x_ref[...]
            h_ref[...] = _rms(x, g_ref[...]).astype(BF16)
            dx, dgam = _rms_bwd_tail(x, g_ref[...], acc_ref[...])
            dx_ref[...] = d1_ref[...] + dx
            dgam_ref[...] += dgam

    row_d = pl.BlockSpec((tm, D), lambda i, j: (i, 0))
    return _call(
        body, name="inproj_bwd", grid=(t_rows // tm, nn),
        in_specs=[pl.BlockSpec((tm, tn), lambda i, j: (i, j)), row_d, row_d, pl.BlockSpec((1, D), lambda i, j: (0, 0)),
                  pl.BlockSpec((D, tn), lambda i, j: (0, j))],
        out_specs=[row_d, pl.BlockSpec((1, D), lambda i, j: (0, 0)), row_d],
        out_shape=[jax.ShapeDtypeStruct((t_rows, D), F32), jax.ShapeDtypeStruct((1, D), F32),
                   jax.ShapeDtypeStruct((t_rows, D), BF16)],
        scratch_shapes=[pltpu.VMEM((tm, D), F32)],
        compiler_params=_cparams(("arbitrary", "arbitrary")))(dz, dx1, x, gamma, w)


def _matmul_tn(a, b, name, tt=512):
    t_rows, k = a.shape
    n = b.shape[1]
    tk = min(k, 1024)
    tn = min(n, 512)
    nt = t_rows // tt

    def body(a_ref, b_ref, o_ref, acc_ref):
        t = pl.program_id(2)

        @pl.when(t == 0)
        def _():
            acc_ref[...] = jnp.zeros_like(acc_ref)

        acc_ref[...] += _dot_tn(a_ref[...], b_ref[...])

        @pl.when(t == nt - 1)
        def _():
            o_ref[...] = acc_ref[...].astype(BF16)

    return _call(
        body, name=name, grid=(k // tk, n // tn, nt),
        in_specs=[pl.BlockSpec((tt, tk), lambda i, j, t: (t, i)), pl.BlockSpec((tt, tn), lambda i, j, t: (t, j))],
        out_specs=pl.BlockSpec((tk, tn), lambda i, j, t: (i, j)),
        out_shape=jax.ShapeDtypeStruct((k, n), BF16),
        scratch_shapes=[pltpu.VMEM((tk, tn), F32)],
        compiler_params=_cparams(("parallel", "parallel", "arbitrary")))(a, b)


def _group_tn(a, b):
    t_rows = a.shape[0]
    gw = BW // 4

    def body(a_ref, b_ref, o_ref):
        o_ref[...] = _dot_tn(a_ref[...], b_ref[...])

    return _call(
        body, name="pool_group_tn", grid=(4,),
        in_specs=[pl.BlockSpec((t_rows, gw), lambda k: (0, k)), pl.BlockSpec((t_rows, gw), lambda k: (0, k))],
        out_specs=pl.BlockSpec((None, gw, gw), lambda k: (k, 0, 0)),
        out_shape=jax.ShapeDtypeStruct((4, gw, gw), F32),
        compiler_params=_cparams(("parallel",)))(a, b)


def _loss_head(x2, gamma, target, tm=512):
    t_rows = x2.shape[0]

    def body(x_ref, g_ref, t_ref, loss_ref, dx_ref, dgam_ref):
        @pl.when(pl.program_id(0) == 0)
        def _():
            loss_ref[...] = jnp.zeros_like(loss_ref)
            dgam_ref[...] = jnp.zeros_like(dgam_ref)

        def f(x, g):
            err = jnp.square(_rms(x, g) - t_ref[...])
            return 0.5 * jnp.sum(jnp.mean(err, axis=-1, keepdims=True), axis=0, keepdims=True)

        loss, vjp = jax.vjp(f, x_ref[...], g_ref[...])
        dx, dgam = vjp(jnp.ones((1, 1), F32))
        loss_ref[...] += jnp.broadcast_to(loss, (1, 128))
        dx_ref[...] = dx
        dgam_ref[...] += dgam

    row_d = pl.BlockSpec((tm, D), lambda i: (i, 0))
    return _call(
        body, name="loss_head", grid=(t_rows // tm,),
        in_specs=[row_d, pl.BlockSpec((1, D), lambda i: (0, 0)), row_d],
        out_specs=[pl.BlockSpec((1, 128), lambda i: (0, 0)), row_d, pl.BlockSpec((1, D), lambda i: (0, 0))],
        out_shape=[jax.ShapeDtypeStruct((1, 128), F32), jax.ShapeDtypeStruct((t_rows, D), F32),
                   jax.ShapeDtypeStruct((1, D), F32)],
        compiler_params=_cparams(("arbitrary",)))(x2, gamma, target)


def _my_index():
    return 4 * lax.axis_index("x") + 2 * lax.axis_index("y") + lax.axis_index("c")


def _exchange(name, srcs, out_shapes, transfers):
    ns = len(srcs)
    nt = len(transfers)

    def body(*refs):
        ins = refs[:ns]
        outs = refs[ns:ns + len(out_shapes)]
        loc_sem, send_sem, recv_sem = refs[ns + len(out_shapes):]
        me = _my_index()
        local = []
        for k, (si, oi, src_view, dst_view, _) in enumerate(transfers):
            cp = pltpu.make_async_copy(src_view(ins[si], me), dst_view(outs[oi], me), loc_sem.at[k])
            cp.start()
            local.append(cp)
        for step in range(1, NDEV):
            q = (me + step) % NDEV
            peer = (q // 4, (q // 2) % 2, q % 2)
            for k, (si, oi, src_view, dst_view, _) in enumerate(transfers):
                pltpu.make_async_remote_copy(
                    src_ref=src_view(ins[si], q), dst_ref=dst_view(outs[oi], me), send_sem=send_sem.at[k],
                    recv_sem=recv_sem.at[k], device_id=peer, device_id_type=MESH).start()
        for k, (si, oi, _, _, wait_view) in enumerate(transfers):
            seven = wait_view(outs[oi])
            pltpu.make_async_remote_copy(
                src_ref=seven, dst_ref=seven, send_sem=send_sem.at[k], recv_sem=recv_sem.at[k],
                device_id=(lax.axis_index("x"), lax.axis_index("y"), lax.axis_index("c")), device_id_type=MESH).wait()
        for cp in local:
            cp.wait()

    return _call(
        body, name=name, in_specs=[ANY] * ns, out_specs=[ANY] * len(out_shapes), out_shape=out_shapes,
        scratch_shapes=[pltpu.SemaphoreType.DMA((nt,)), pltpu.SemaphoreType.DMA((nt,)), pltpu.SemaphoreType.DMA((nt,))],
    )(*srcs)


_BIG = {
    "w_in": (1, D, IN_W // NDEV, D, IN_W),
    "ssm_w_glu": (0, BW // NDEV, BW, BW, BW),
    "ssm_w_proj": (1, BW, D // NDEV, BW, D),
    "conv_w_proj": (1, BW, D // NDEV, BW, D),
    "pool_w_proj": (1, BW, D // NDEV, BW, D),
    "w_out": (0, D // NDEV, D, D, D),
    "ffn_w_gate": (1, D, HPAD, D, HIDP),
    "ffn_w_up": (1, D, HPAD, D, HIDP),
    "ffn_w_down": (0, HPAD, D, HIDP, D),
}


def _block_view(axis, size):
    if axis == 1:
        return lambda ref, q: ref.at[:, pl.ds(pl.multiple_of(q * size, 128), size)]
    return lambda ref, q: ref.at[pl.ds(pl.multiple_of(q * size, 16), size), :]


def _gather_weights(shards, conv_dw):
    srcs, outs, transfers, where = [], [], [], {}
    for name, (axis, kk, nn, kf, nf) in _BIG.items():
        srcs.append(shards[name])
        size = nn if axis == 1 else kk
        view = _block_view(axis, size)
        for l in range(DEPTH):
            where[(name, l)] = len(outs)
            outs.append(jax.ShapeDtypeStruct((kf, nf), BF16))
            seven = (lambda ref, size=size: ref.at[:, pl.ds(0, 7 * size)]) if axis == 1 else \
                    (lambda ref, size=size: ref.at[pl.ds(0, 7 * size), :])
            transfers.append((len(srcs) - 1, len(outs) - 1, lambda ref, q, l=l: ref.at[l], view, seven))
    srcs.append(conv_dw)
    outs.append(jax.ShapeDtypeStruct((NDEV,) + conv_dw.shape, conv_dw.dtype))
    transfers.append((len(srcs) - 1, len(outs) - 1, lambda ref, q: ref, lambda ref, me: ref.at[me],
                      lambda ref: ref.at[pl.ds(0, 7)]))
    res = _exchange("gather_weights", srcs, outs, transfers)
    full = {name: [res[where[(name, l)]] for l in range(DEPTH)] for name in _BIG}
    return full, res[-1]


def _exchange_grads(big, by_owner, small):
    srcs, outs, transfers, where = [], [], [], {}

    def add(name, arrays, slab_shape, dtype, src_view_of):
        where[name] = len(outs)
        stacked = len(arrays) > 1
        outs.append(jax.ShapeDtypeStruct((NDEV,) + ((len(arrays),) if stacked else ()) + slab_shape, dtype))
        for l, arr in enumerate(arrays):
            srcs.append(arr)
            dst = (lambda ref, me, l=l: ref.at[me, l]) if stacked else (lambda ref, me: ref.at[me])
            seven = (lambda ref, l=l: ref.at[pl.ds(0, 7), l]) if stacked else (lambda ref: ref.at[pl.ds(0, 7)])
            transfers.append((len(srcs) - 1, len(outs) - 1, src_view_of, dst, seven))

    for name, arrays in big.items():
        axis, kk, nn, _, _ = _BIG[name]
        add(name, arrays, (kk, nn), BF16, _block_view(axis, nn if axis == 1 else kk))
    for name, arrays in by_owner.items():
        add(name, arrays, arrays[0].shape[1:], F32, lambda ref, q: ref.at[q])
    for name, arrays in small.items():
        add(name, arrays, arrays[0].shape, F32, lambda ref, q: ref)
    res = _exchange("exchange_grads", srcs, outs, transfers)
    return {name: res[k] for name, k in where.items()}


def _adamw(w, g, m, v):
    m = ADAM_B1 * m + (1.0 - ADAM_B1) * g
    v = ADAM_B2 * v + (1.0 - ADAM_B2) * jnp.square(g)
    m_hat = m / (1.0 - ADAM_B1 ** ADAM_STEP)
    v_hat = v / (1.0 - ADAM_B2 ** ADAM_STEP)
    delta = -ADAM_LR * (m_hat / (jnp.sqrt(v_hat) + ADAM_EPS) + ADAM_WD * w)
    return delta, m, v


def _sum_senders(ref):
    g = ref[0].astype(F32)
    for q in range(1, NDEV):
        g = g + ref[q].astype(F32)
    return g


def _adam_big(name, recv, w, m, v, tk):
    kk, nn = w.shape[1], w.shape[2]
    nnp = recv.shape[3]

    def body(r_ref, w_ref, m_ref, v_ref, g_ref, d_ref, mo_ref, vo_ref):
        g = _sum_senders(r_ref)[:, :nn]
        delta, m2, v2 = _adamw(w_ref[...], g, m_ref[...], v_ref[...])
        g_ref[...] = g
        d_ref[...] = delta
        mo_ref[...] = m2
        vo_ref[...] = v2

    wspec = pl.BlockSpec((None, tk, nn), lambda l, i: (l, i, 0))
    shape = jax.ShapeDtypeStruct(w.shape, F32)
    return _call(
        body, name="adamw_" + name, grid=(DEPTH, kk // tk),
        in_specs=[pl.BlockSpec((NDEV, None, tk, nnp), lambda l, i: (0, l, i, 0)), wspec, wspec, wspec],
        out_specs=[wspec] * 4, out_shape=[shape] * 4,
        compiler_params=_cparams(("parallel", "parallel")))(recv, w, m, v)


def _adam_small(names, recv, w, m, v):
    n = len(names)

    def body(*refs):
        r, ww, mm, vv = refs[:n], refs[n:2 * n], refs[2 * n:3 * n], refs[3 * n:4 * n]
        outs = refs[4 * n:]
        for k in range(n):
            g = _sum_senders(r[k])
            if g.shape != ww[k].shape:
                g = g[:, :ww[k].shape[1]]
            delta, m2, v2 = _adamw(ww[k][...], g, mm[k][...], vv[k][...])
            outs[k][...] = g
            outs[n + k][...] = delta
            outs[2 * n + k][...] = m2
            outs[3 * n + k][...] = v2

    shapes = [jax.ShapeDtypeStruct(w[k].shape, F32) for k in names]
    res = _call(body, name="adamw_small", out_shape=shapes * 4, compiler_params=_cparams())(
        *[recv[k] for k in names], *[w[k] for k in names], *[m[k] for k in names], *[v[k] for k in names])
    return {k: (res[i], res[n + i], res[2 * n + i], res[3 * n + i]) for i, k in enumerate(names)}


def _expand_b(b):
    bt = jnp.transpose(b, (0, 2, 1)).reshape(NBLK, GB, SGRP, NSTATE)
    eye = jnp.eye(GB, dtype=b.dtype)
    return jnp.einsum("jgpn,gh->jgphn", bt, eye).reshape(NBLK, GB * SGRP, NS)


def _extract_b(db):
    x = db.reshape(NBLK, GB, SGRP, GB, NSTATE)
    eye = jnp.eye(GB, dtype=db.dtype)
    d = jnp.einsum("jgphn,gh->jgpn", x, eye).reshape(NGRP, SGRP, NSTATE)
    return jnp.transpose(d, (0, 2, 1)).reshape(NGRP, NSTATE * SGRP)


def _expand_c(c):
    ct = jnp.transpose(c, (0, 2, 1)).reshape(NBLK, GB, NSTATE, SGRP)
    eye = jnp.eye(GB, dtype=c.dtype)
    return jnp.einsum("jgnp,gh->jgnhp", ct, eye).reshape(NBLK, NS, GB * SGRP)


def _extract_c(dc):
    x = dc.reshape(NBLK, GB, NSTATE, GB, SGRP)
    eye = jnp.eye(GB, dtype=dc.dtype)
    d = jnp.einsum("jgnhp,gh->jgnp", x, eye).reshape(NGRP, NSTATE, SGRP)
    return jnp.transpose(d, (0, 2, 1))


_SMALL = ("norm1", "b_gate", "ssm_a_re", "ssm_a_im", "ssm_log_dt", "ssm_b_re", "ssm_b_im", "ssm_c_re", "ssm_c_im",
          "ssm_d", "ssm_b_glu", "conv_b_dw", "conv_ln_g", "conv_ln_b", "pool_w_group", "pool_scale", "norm2")
_ADAM_TK = {"w_in": 256, "ssm_w_glu": 64, "ssm_w_proj": 512, "conv_w_proj": 512, "pool_w_proj": 512, "w_out": 128,
            "ffn_w_gate": 256, "ffn_w_up": 256, "ffn_w_down": HSH}
_OUT_ORDER = ("norm1", "w_in", "b_gate", "ssm_a_re", "ssm_a_im", "ssm_log_dt", "ssm_b_re", "ssm_b_im", "ssm_c_re",
              "ssm_c_im", "ssm_d", "ssm_w_glu", "ssm_b_glu", "ssm_w_proj", "conv_w_dw", "conv_b_dw", "conv_ln_g",
              "conv_ln_b", "conv_w_proj", "pool_w_group", "pool_scale", "pool_w_proj", "w_out", "norm2", "ffn_w_gate",
              "ffn_w_up", "ffn_w_down", "final_norm")


def _layer_fwd(x, p):
    z = _inproj_fwd(x, p["norm1"], p["w_in"])
    yssm, hre, him = _ssm_fwd(z, p)
    cv = _conv_fwd(z, p["conv_w"], p["conv_b"])
    pbar = _pool_fwd(z)
    zg = z[:, 4 * BW:]
    x1 = _merge_fwd(x, yssm, cv, pbar, zg, p)
    x2, gpre, upre = _ffn_fwd(x1, p["norm2"], p["wg"], p["wu"], p["wd"])
    return x2, dict(x=x, z=z, zg=zg, yssm=yssm, hre=hre, him=him, cv=cv, pbar=pbar, x1=x1, gpre=gpre, upre=upre)


def _layer_bwd(dx, p, s):
    big, small = {}, {}
    dx1, d_norm2, dgp, dup, act, h2 = _ffn_bwd(dx, s["x1"], p["norm2"], s["gpre"], s["upre"], p["wg"], p["wu"], p["wd"])
    big["ffn_w_gate"] = _matmul_tn(h2, dgp, "tn_gate")
    big["ffn_w_up"] = _matmul_tn(h2, dup, "tn_up")
    big["ffn_w_down"] = _matmul_tn(act, dx, "tn_down")
    (dy, dcv, dpb, dzg, a_g, a_outa, a_hs, a_pb, a_pc, a_mg, c_glu, c_ya, c_yb, c_p, c_yc,
     d_bglu, d_lng, d_lnb, d_scale, d_bgate) = _merge_bwd(dx1, s["x"], s["yssm"], s["cv"], s["pbar"], s["zg"], p)
    big["ssm_w_glu"] = _matmul_tn(a_g, c_glu, "tn_glu")
    big["ssm_w_proj"] = _matmul_tn(a_outa, c_ya, "tn_ssm_proj")
    big["conv_w_proj"] = _matmul_tn(a_hs, c_yb, "tn_conv_proj")
    big["pool_w_proj"] = _matmul_tn(a_pc, c_yc, "tn_pool_proj")
    big["w_out"] = _matmul_tn(a_mg, dx1, "tn_out")
    d_wgrp = _group_tn(a_pb, c_p)
    du_a, dbr, dbi, dcr, dci, dd, dar, dai, dldt = _ssm_bwd(dy, s["z"], s["hre"], s["him"], p)
    dva, dvb, dw8, dcb = _conv_bwd(dcv, s["z"], p["conv_w"])
    du_c = _pool_bwd(dpb)
    dz = jnp.concatenate([du_a, dva, dvb, du_c, dzg], axis=1)
    dx0, d_norm1, h = _inproj_bwd(dz, dx1, s["x"], p["norm1"], p["w_in"])
    big["w_in"] = _matmul_tn(h, dz, "tn_in")
    small["norm1"] = d_norm1
    small["b_gate"] = d_bgate
    small["ssm_a_re"] = dar.reshape(NGRP, NSTATE)
    small["ssm_a_im"] = dai.reshape(NGRP, NSTATE)
    small["ssm_log_dt"] = dldt.reshape(NBLK, 8, 128)[:, 0, :GB].reshape(1, NGRP)
    small["ssm_b_re"] = _extract_b(dbr)
    small["ssm_b_im"] = _extract_b(dbi)
    small["ssm_c_re"] = _extract_c(dcr)
    small["ssm_c_im"] = _extract_c(dci)
    small["ssm_d"] = dd.reshape(NGRP, SGRP)
    small["ssm_b_glu"] = d_bglu
    small["conv_b_dw"] = dcb
    small["conv_ln_g"] = d_lng
    small["conv_ln_b"] = d_lnb
    small["pool_w_group"] = d_wgrp
    small["pool_scale"] = d_scale
    small["norm2"] = d_norm2
    return dx0, big, dw8, small


def _train_step(a):
    t_rows = a["x"].shape[1]
    x0 = a["x"].reshape(t_rows, D)
    target = a["loss_target"].reshape(t_rows, D)

    pad_cols = lambda w: jnp.pad(w, ((0, 0), (0, 0), (0, HPAD - HSH)))
    shards = {
        "w_in": a["w_in"], "ssm_w_glu": a["ssm_w_glu"], "ssm_w_proj": a["ssm_w_proj"],
        "conv_w_proj": a["conv_w_proj"], "pool_w_proj": a["pool_w_proj"], "w_out": a["w_out"],
        "ffn_w_gate": pad_cols(a["ffn_w_gate"]), "ffn_w_up": pad_cols(a["ffn_w_up"]),
        "ffn_w_down": jnp.pad(a["ffn_w_down"], ((0, 0), (0, HPAD - HSH), (0, 0))),
    }
    shards = {k: v.astype(BF16) for k, v in shards.items()}
    full, dw_all = _gather_weights(shards, a["conv_w_dw"].reshape(DEPTH, CONV_K, BW // NDEV))
    conv_w = jnp.transpose(dw_all, (1, 2, 0, 3)).reshape(DEPTH, CONV_K, BW)

    def layer_params(l):
        row = lambda v: v.reshape(1, -1)
        return dict(
            norm1=row(a["norm1"][l]), w_in=full["w_in"][l],
            are=row(a["ssm_a_re"][l]), aim=row(a["ssm_a_im"][l]),
            ldt=row(jnp.repeat(a["ssm_log_dt"][l], NSTATE)),
            bexp_re=_expand_b(a["ssm_b_re"][l]), bexp_im=_expand_b(a["ssm_b_im"][l]),
            cexp_re=_expand_c(a["ssm_c_re"][l]), cexp_im=_expand_c(a["ssm_c_im"][l]),
            dskip=row(a["ssm_d"][l]),
            conv_w=conv_w[l], conv_b=row(a["conv_b_dw"][l]),
            wglu=full["ssm_w_glu"][l], bglu=row(a["ssm_b_glu"][l]), wpa=full["ssm_w_proj"][l],
            lng=row(a["conv_ln_g"][l]), lnb=row(a["conv_ln_b"][l]), wpb=full["conv_w_proj"][l],
            wgrp=a["pool_w_group"][l].astype(BF16), scale=row(a["pool_scale"][l]), wpc=full["pool_w_proj"][l],
            bgate=row(a["b_gate"][l]), wout=full["w_out"][l],
            norm2=row(a["norm2"][l]), wg=full["ffn_w_gate"][l], wu=full["ffn_w_up"][l], wd=full["ffn_w_down"][l],
        )

    params = [layer_params(l) for l in range(DEPTH)]

    saved = []
    x = x0
    for l in range(DEPTH):
        x, s = _layer_fwd(x, params[l])
        saved.append(s)

    loss_part, dx, d_final = _loss_head(x, a["final_norm"].reshape(1, D), target)
    loss = lax.psum(loss_part[0, 0], ("x", "y", "c"))

    big = {name: [None] * DEPTH for name in _BIG}
    by_owner = {"conv_w_dw": [None] * DEPTH}
    small = {name: [None] * DEPTH for name in _SMALL}
    for l in reversed(range(DEPTH)):
        dx, gb, go, gs = _layer_bwd(dx, params[l], saved[l])
        for name, g in gb.items():
            big[name][l] = g
        by_owner["conv_w_dw"][l] = go
        for name, g in gs.items():
            small[name][l] = g
    small["final_norm"] = [d_final]
    grad_x = dx.reshape(1, t_rows, D)

    recv = _exchange_grads(big, by_owner, small)
    results = {}
    for name in _BIG:
        results[name] = _adam_big(name, recv[name], a[name], a["m_" + name], a["v_" + name], _ADAM_TK[name])

    lay = {
        "norm1": lambda v: v.reshape(DEPTH, 1, D), "b_gate": lambda v: v.reshape(DEPTH, 1, 3 * D),
        "ssm_log_dt": lambda v: v.reshape(DEPTH, 1, NGRP),
        "ssm_b_re": lambda v: v.reshape(DEPTH, NGRP, NSTATE * SGRP), "ssm_b_im": lambda v: v.reshape(DEPTH, NGRP, NSTATE * SGRP),
        "ssm_b_glu": lambda v: v.reshape(DEPTH, 1, BW), "conv_b_dw": lambda v: v.reshape(DEPTH, 1, BW),
        "conv_ln_g": lambda v: v.reshape(DEPTH, 1, BW), "conv_ln_b": lambda v: v.reshape(DEPTH, 1, BW),
        "pool_scale": lambda v: v.reshape(DEPTH, 1, BW), "norm2": lambda v: v.reshape(DEPTH, 1, D),
        "conv_w_dw": lambda v: v.reshape(DEPTH, CONV_K, BW // NDEV), "final_norm": lambda v: v.reshape(1, D),
    }
    names = _SMALL + ("conv_w_dw", "final_norm")
    relay = lambda k, v: lay[k](v) if k in lay else v
    sm = _adam_small(names, recv, {k: relay(k, a[k]) for k in names}, {k: relay(k, a["m_" + k]) for k in names},
                     {k: relay(k, a["v_" + k]) for k in names})
    for k in names:
        results[k] = tuple(r.reshape(a[k].shape) for r in sm[k])

    outs = [loss, grad_x]
    for part in range(4):
        outs += [results[k][part] for k in _OUT_ORDER]
    return tuple(outs)


def kernel(x, norm1, w_in, b_gate, ssm_a_re, ssm_a_im, ssm_log_dt, ssm_b_re, ssm_b_im, ssm_c_re, ssm_c_im, ssm_d, ssm_w_glu, ssm_b_glu, ssm_w_proj, conv_w_dw, conv_b_dw, conv_ln_g, conv_ln_b, conv_w_proj, pool_w_group, pool_scale, pool_w_proj, w_out, norm2, ffn_w_gate, ffn_w_up, ffn_w_down, final_norm, loss_target, m_norm1, m_w_in, m_b_gate, m_ssm_a_re, m_ssm_a_im, m_ssm_log_dt, m_ssm_b_re, m_ssm_b_im, m_ssm_c_re, m_ssm_c_im, m_ssm_d, m_ssm_w_glu, m_ssm_b_glu, m_ssm_w_proj, m_conv_w_dw, m_conv_b_dw, m_conv_ln_g, m_conv_ln_b, m_conv_w_proj, m_pool_w_group, m_pool_scale, m_pool_w_proj, m_w_out, m_norm2, m_ffn_w_gate, m_ffn_w_up, m_ffn_w_down, m_final_norm, v_norm1, v_w_in, v_b_gate, v_ssm_a_re, v_ssm_a_im, v_ssm_log_dt, v_ssm_b_re, v_ssm_b_im, v_ssm_c_re, v_ssm_c_im, v_ssm_d, v_ssm_w_glu, v_ssm_b_glu, v_ssm_w_proj, v_conv_w_dw, v_conv_b_dw, v_conv_ln_g, v_conv_ln_b, v_conv_w_proj, v_pool_w_group, v_pool_scale, v_pool_w_proj, v_w_out, v_norm2, v_ffn_w_gate, v_ffn_w_up, v_ffn_w_down, v_final_norm):
    return _train_step(dict(locals()))
```

```python
import functools

import jax
import jax.numpy as jnp
from jax import lax
from jax.experimental import pallas as pl
from jax.experimental.pallas import tpu as pltpu

F32 = jnp.float32
BF16 = jnp.bfloat16

NDEV = 8
DEPTH = 2
D = 1024
BW = 512
NSTATE = 64
SGRP = 16
NGRP = BW // SGRP
GB = 8
NBLK = NGRP // GB
NS = GB * NSTATE
CONV_K = 31
HALO = 32
PHALO = 16
IN_W = 5120
HID = 2816
HSH = HID // NDEV
HPAD = 384
HIDP = HPAD * NDEV
EPS = 1e-6
VMEM_LIMIT = 56 * 1024 * 1024

ADAM_LR, ADAM_B1, ADAM_B2, ADAM_EPS, ADAM_WD, ADAM_STEP = 0.001, 0.9, 0.999, 1e-08, 0.01, 10

MESH = pl.DeviceIdType.MESH
ANY = pl.BlockSpec(memory_space=pl.ANY)


def _call(body, **kw):
    return pl.pallas_call(body, **kw)


def _cparams(sem=None):
    return pltpu.CompilerParams(dimension_semantics=sem, vmem_limit_bytes=VMEM_LIMIT)


def _dot(a, b):
    return jnp.dot(a.astype(BF16), b.astype(BF16), preferred_element_type=F32)


def _dot_nt(a, b):
    return lax.dot_general(a.astype(BF16), b.astype(BF16), (((1,), (1,)), ((), ())), preferred_element_type=F32)


def _dot_tn(a, b):
    return lax.dot_general(a.astype(BF16), b.astype(BF16), (((0,), (0,)), ((), ())), preferred_element_type=F32)


@jax.custom_vjp
def _mm(a, w):
    return _dot(a, w)


def _mm_fwd(a, w):
    return _dot(a, w), w


def _mm_bwd(w, ct):
    return _dot_nt(ct, w), jnp.zeros_like(w)


_mm.defvjp(_mm_fwd, _mm_bwd)


def _rms(x, g):
    return x * lax.rsqrt(jnp.mean(x * x, axis=-1, keepdims=True) + EPS) * g


def _disc(are, aim, ldt):
    dt = jnp.exp(ldt)
    mag = jnp.exp(dt * are)
    ang = dt * aim
    abr = mag * jnp.cos(ang)
    abi = mag * jnp.sin(ang)
    den = are * are + aim * aim
    nr = abr - 1.0
    fr = (nr * are + abi * aim) / den
    fi = (abi * are - nr * aim) / den
    return abr, abi, fr, fi


def _bbar(fr, fi, br, bi):
    return fr * br - fi * bi, fr * bi + fi * br


def _cmul(ar, ai, br, bi):
    return ar * br - ai * bi, ar * bi + ai * br


def _scan_rows(re_ref, im_ref, ar, ai, n_rows, reverse, hre_ref=None, him_ref=None):
    n = ar.shape[1]
    shape = (8, n)
    rows = lax.broadcasted_iota(jnp.int32, shape, 0)
    a1 = (jnp.broadcast_to(ar, shape), jnp.broadcast_to(ai, shape))
    a2 = _cmul(*a1, *a1)
    a4 = _cmul(*a2, *a2)
    pr = jnp.zeros(shape, F32)
    pi = jnp.zeros(shape, F32)
    pw = a1
    for k in range(8):
        sel = rows == ((7 - k) if reverse else k)
        pr = jnp.where(sel, pw[0], pr)
        pi = jnp.where(sel, pw[1], pi)
        pw = _cmul(*pw, *a1)
    nt = n_rows // 8
    with_acc = hre_ref is not None

    def body(i, carry):
        cr, ci = carry[0], carry[1]
        t = (nt - 1 - i) if reverse else i
        off = pl.multiple_of(t * 8, 8)
        xr = re_ref[pl.ds(off, 8), :]
        xi = im_ref[pl.ds(off, 8), :]
        for k, (kr, ki) in ((1, a1), (2, a2), (4, a4)):
            if reverse:
                keep, sh = rows < 8 - k, 8 - k
            else:
                keep, sh = rows >= k, k
            sr = jnp.where(keep, pltpu.roll(xr, sh, 0), 0.0)
            si = jnp.where(keep, pltpu.roll(xi, sh, 0), 0.0)
            xr, xi = xr + kr * sr - ki * si, xi + kr * si + ki * sr
        xr, xi = xr + pr * cr - pi * ci, xi + pr * ci + pi * cr
        re_ref[pl.ds(off, 8), :] = xr
        im_ref[pl.ds(off, 8), :] = xi
        edge = 0 if reverse else 7
        out = (jnp.broadcast_to(xr[edge:edge + 1, :], shape), jnp.broadcast_to(xi[edge:edge + 1, :], shape))
        if with_acc:
            hr = hre_ref[pl.ds(off, 8), :]
            hi = him_ref[pl.ds(off, 8), :]
            offp = pl.multiple_of(jnp.maximum(t - 1, 0) * 8, 8)
            live = jnp.where(t > 0, 1.0, 0.0)
            lr = jnp.broadcast_to(hre_ref[pl.ds(offp, 8), :][7:8, :], shape) * live
            li = jnp.broadcast_to(him_ref[pl.ds(offp, 8), :][7:8, :], shape) * live
            hpr = jnp.where(rows == 0, lr, pltpu.roll(hr, 1, 0))
            hpi = jnp.where(rows == 0, li, pltpu.roll(hi, 1, 0))
            out = out + (carry[2] + xr * hpr + xi * hpi, carry[3] + xi * hpr - xr * hpi)
        return out

    zero = jnp.zeros(shape, F32)
    init = (zero, zero, zero, zero) if with_acc else (zero, zero)
    res = lax.fori_loop(0, nt, body, init)
    return res[2:] if with_acc else None


def _inproj_fwd(x, gamma, w, tm=512, tn=1280):
    t_rows = x.shape[0]
    n = w.shape[1]

    def body(x_ref, g_ref, w_ref, z_ref, h_ref):
        @pl.when(pl.program_id(1) == 0)
        def _():
            h_ref[...] = _rms(x_ref[...], g_ref[...]).astype(BF16)
        z_ref[...] = jnp.dot(h_ref[...], w_ref[...], preferred_element_type=F32)

    return _call(
        body, name="inproj_fwd", grid=(t_rows // tm, n // tn),
        in_specs=[pl.BlockSpec((tm, D), lambda i, j: (i, 0)), pl.BlockSpec((1, D), lambda i, j: (0, 0)),
                  pl.BlockSpec((D, tn), lambda i, j: (0, j))],
        out_specs=pl.BlockSpec((tm, tn), lambda i, j: (i, j)),
        out_shape=jax.ShapeDtypeStruct((t_rows, n), F32),
        scratch_shapes=[pltpu.VMEM((tm, D), BF16)],
        compiler_params=_cparams(("parallel", "arbitrary")))(x, gamma, w)


def _ssm_specs(t_rows):
    row = pl.BlockSpec((1, NS), lambda j: (0, j))
    return dict(
        u=pl.BlockSpec((t_rows, GB * SGRP), lambda j: (0, j)),
        row=row,
        bexp=pl.BlockSpec((None, GB * SGRP, NS), lambda j: (j, 0, 0)),
        cexp=pl.BlockSpec((None, NS, GB * SGRP), lambda j: (j, 0, 0)),
        d=pl.BlockSpec((1, GB * SGRP), lambda j: (0, j)),
        h=pl.BlockSpec((t_rows, NS), lambda j: (0, j)),
    )


def _ssm_fwd(z, p):
    t_rows = z.shape[0]
    s = _ssm_specs(t_rows)

    def body(u_ref, are_ref, aim_ref, ldt_ref, br_ref, bi_ref, cr_ref, ci_ref, d_ref, y_ref, hr_ref, hi_ref):
        abr, abi, fr, fi = _disc(are_ref[...], aim_ref[...], ldt_ref[...])
        bbr, bbi = _bbar(fr, fi, br_ref[...], bi_ref[...])
        u = u_ref[...]
        hr_ref[...] = _dot(u, bbr)
        hi_ref[...] = _dot(u, bbi)
        _scan_rows(hr_ref, hi_ref, abr, abi, t_rows, False)
        y_ref[...] = _dot(hr_ref[...], cr_ref[...]) - _dot(hi_ref[...], ci_ref[...]) + d_ref[...] * u

    return _call(
        body, name="ssm_fwd", grid=(NBLK,),
        in_specs=[s["u"], s["row"], s["row"], s["row"], s["bexp"], s["bexp"], s["cexp"], s["cexp"], s["d"]],
        out_specs=[s["u"], s["h"], s["h"]],
        out_shape=[jax.ShapeDtypeStruct((t_rows, BW), F32), jax.ShapeDtypeStruct((t_rows, NGRP * NSTATE), F32),
                   jax.ShapeDtypeStruct((t_rows, NGRP * NSTATE), F32)],
        compiler_params=_cparams(("parallel",)))(
            z, p["are"], p["aim"], p["ldt"], p["bexp_re"], p["bexp_im"], p["cexp_re"], p["cexp_im"], p["dskip"])


def _ssm_bwd(dy, z, hre, him, p):
    t_rows = z.shape[0]
    s = _ssm_specs(t_rows)
    nstates = NGRP * NSTATE

    def body(dy_ref, u_ref, hr_ref, hi_ref, are_ref, aim_ref, ldt_ref, br_ref, bi_ref, cr_ref, ci_ref, d_ref,
             du_ref, dbr_ref, dbi_ref, dcr_ref, dci_ref, dd_ref, dar_ref, dai_ref, dldt_ref, lr_ref, li_ref):
        rows3 = (are_ref[...], aim_ref[...], ldt_ref[...])
        (abr, abi, fr, fi), disc_vjp = jax.vjp(_disc, *rows3)
        (bbr, bbi), bbar_vjp = jax.vjp(_bbar, fr, fi, br_ref[...], bi_ref[...])
        dy = dy_ref[...]
        u = u_ref[...]
        lr_ref[...] = _dot_nt(dy, cr_ref[...])
        li_ref[...] = -_dot_nt(dy, ci_ref[...])
        dcr_ref[...] = _dot_tn(hr_ref[...], dy)
        dci_ref[...] = -_dot_tn(hi_ref[...], dy)
        dd_ref[...] = jnp.sum(dy * u, axis=0, keepdims=True)
        acc_r, acc_i = _scan_rows(lr_ref, li_ref, abr, -abi, t_rows, True, hr_ref, hi_ref)
        dabr = jnp.sum(acc_r, axis=0, keepdims=True)
        dabi = jnp.sum(acc_i, axis=0, keepdims=True)
        lam_r = lr_ref[...]
        lam_i = li_ref[...]
        du = d_ref[...] * dy + _dot_nt(lam_r, bbr) + _dot_nt(lam_i, bbi)
        du_ref[...] = du.astype(BF16)
        dbbr = _dot_tn(u, lam_r)
        dbbi = _dot_tn(u, lam_i)
        dfr, dfi, dbr, dbi = bbar_vjp((dbbr, dbbi))
        dbr_ref[...] = dbr
        dbi_ref[...] = dbi
        dar, dai, dldt = disc_vjp((dabr, dabi, dfr, dfi))
        dar_ref[...] = dar
        dai_ref[...] = dai
        lane_grp = lax.broadcasted_iota(jnp.int32, (NS, 128), 0) // NSTATE
        col = lax.broadcasted_iota(jnp.int32, (NS, 128), 1)
        seg = jnp.where(lane_grp == col, 1.0, 0.0).astype(F32)
        dldt_ref[...] = jnp.dot(jnp.broadcast_to(dldt, (8, NS)), seg, preferred_element_type=F32,
                                precision=lax.Precision.HIGHEST)

    dyspec = pl.BlockSpec((t_rows, GB * SGRP), lambda j: (0, j))
    return _call(
        body, name="ssm_bwd", grid=(NBLK,),
        in_specs=[dyspec, s["u"], s["h"], s["h"], s["row"], s["row"], s["row"], s["bexp"], s["bexp"], s["cexp"],
                  s["cexp"], s["d"]],
        out_specs=[dyspec, s["bexp"], s["bexp"], s["cexp"], s["cexp"], s["d"], s["row"], s["row"],
                   pl.BlockSpec((8, 128), lambda j: (j, 0))],
        out_shape=[jax.ShapeDtypeStruct((t_rows, BW), BF16),
                   jax.ShapeDtypeStruct((NBLK, GB * SGRP, NS), F32), jax.ShapeDtypeStruct((NBLK, GB * SGRP, NS), F32),
                   jax.ShapeDtypeStruct((NBLK, NS, GB * SGRP), F32), jax.ShapeDtypeStruct((NBLK, NS, GB * SGRP), F32),
                   jax.ShapeDtypeStruct((1, BW), F32), jax.ShapeDtypeStruct((1, nstates), F32),
                   jax.ShapeDtypeStruct((1, nstates), F32), jax.ShapeDtypeStruct((NBLK * 8, 128), F32)],
        scratch_shapes=[pltpu.VMEM((t_rows, NS), F32), pltpu.VMEM((t_rows, NS), F32)],
        compiler_params=_cparams(("parallel",)))(
            dy, z, hre, him, p["are"], p["aim"], p["ldt"], p["bexp_re"], p["bexp_im"], p["cexp_re"], p["cexp_im"],
            p["dskip"])


def _conv_fwd(z, w, b, tm=256):
    t_rows = z.shape[0]
    hb = tm // HALO

    def body(va_ref, vb_ref, ha_ref, hb_ref, w_ref, b_ref, o_ref, win_ref):
        live = jnp.where(pl.program_id(0) > 0, 1.0, 0.0)
        win_ref[0:HALO, :] = ha_ref[...] * jax.nn.sigmoid(hb_ref[...]) * live
        win_ref[HALO:HALO + tm, :] = va_ref[...] * jax.nn.sigmoid(vb_ref[...])
        acc = jnp.broadcast_to(b_ref[...], (tm, BW))
        for k in range(CONV_K):
            acc = acc + w_ref[k:k + 1, :] * win_ref[pl.ds(HALO - (CONV_K - 1) + k, tm), :]
        o_ref[...] = acc

    halo = lambda col: pl.BlockSpec((HALO, BW), lambda i: (jnp.maximum(i * hb - 1, 0), col))
    return _call(
        body, name="conv_fwd", grid=(t_rows // tm,),
        in_specs=[pl.BlockSpec((tm, BW), lambda i: (i, 1)), pl.BlockSpec((tm, BW), lambda i: (i, 2)), halo(1), halo(2),
                  pl.BlockSpec((CONV_K, BW), lambda i: (0, 0)), pl.BlockSpec((1, BW), lambda i: (0, 0))],
        out_specs=pl.BlockSpec((tm, BW), lambda i: (i, 0)),
        out_shape=jax.ShapeDtypeStruct((t_rows, BW), F32),
        scratch_shapes=[pltpu.VMEM((HALO + tm, BW), F32)],
        compiler_params=_cparams(("parallel",)))(z, z, z, z, w, b)


def _conv_bwd(dcv, z, w, tm=256):
    t_rows = z.shape[0]
    nt = t_rows // tm
    hb = tm // HALO
    csh = BW // NDEV

    def body(d_ref, dn_ref, va_ref, vb_ref, ha_ref, hb_ref, w_ref, dva_ref, dvb_ref, dw8_ref, db_ref,
             hwin_ref, dwin_ref, dw_ref):
        i = pl.program_id(0)

        @pl.when(i == 0)
        def _():
            dw_ref[...] = jnp.zeros_like(dw_ref)
            db_ref[...] = jnp.zeros_like(db_ref)

        live_prev = jnp.where(i > 0, 1.0, 0.0)
        live_next = jnp.where(i < nt - 1, 1.0, 0.0)
        va = va_ref[...]
        sig = jax.nn.sigmoid(vb_ref[...])
        hwin_ref[0:HALO, :] = ha_ref[...] * jax.nn.sigmoid(hb_ref[...]) * live_prev
        hwin_ref[HALO:HALO + tm, :] = va * sig
        d = d_ref[...]
        dwin_ref[0:tm, :] = d
        dwin_ref[tm:tm + HALO, :] = dn_ref[...] * live_next
        dh = jnp.zeros((tm, BW), F32)
        dws = []
        for k in range(CONV_K):
            dh = dh + w_ref[k:k + 1, :] * dwin_ref[pl.ds(CONV_K - 1 - k, tm), :]
            dws.append(jnp.sum(d * hwin_ref[pl.ds(HALO - (CONV_K - 1) + k, tm), :], axis=0, keepdims=True))
        dws.append(jnp.zeros((1, BW), F32))
        dw_ref[...] += jnp.concatenate(dws, axis=0)
        db_ref[...] += jnp.sum(d, axis=0, keepdims=True)
        dva_ref[...] = (dh * sig).astype(BF16)
        dvb_ref[...] = (dh * va * sig * (1.0 - sig)).astype(BF16)

        @pl.when(i == nt - 1)
        def _():
            acc = dw_ref[...]
            for q in range(NDEV):
                dw8_ref[q] = acc[:, csh * q:csh * (q + 1)]

    halo = lambda col: pl.BlockSpec((HALO, BW), lambda i: (jnp.maximum(i * hb - 1, 0), col))
    return _call(
        body, name="conv_bwd", grid=(nt,),
        in_specs=[pl.BlockSpec((tm, BW), lambda i: (i, 0)),
                  pl.BlockSpec((HALO, BW), lambda i: (jnp.minimum((i + 1) * hb, t_rows // HALO - 1), 0)),
                  pl.BlockSpec((tm, BW), lambda i: (i, 1)), pl.BlockSpec((tm, BW), lambda i: (i, 2)), halo(1), halo(2),
                  pl.BlockSpec((CONV_K, BW), lambda i: (0, 0))],
        out_specs=[pl.BlockSpec((tm, BW), lambda i: (i, 0)), pl.BlockSpec((tm, BW), lambda i: (i, 0)),
                   pl.BlockSpec((NDEV, 32, csh), lambda i: (0, 0, 0)), pl.BlockSpec((1, BW), lambda i: (0, 0))],
        out_shape=[jax.ShapeDtypeStruct((t_rows, BW), BF16), jax.ShapeDtypeStruct((t_rows, BW), BF16),
                   jax.ShapeDtypeStruct((NDEV, 32, csh), F32), jax.ShapeDtypeStruct((1, BW), F32)],
        scratch_shapes=[pltpu.VMEM((HALO + tm, BW), F32), pltpu.VMEM((tm + HALO, BW), F32), pltpu.VMEM((32, BW), F32)],
        compiler_params=_cparams(("arbitrary",)))(dcv, dcv, z, z, z, z, w)


def _pool_rows(i, tm, n_rows, first_row):
    grp = lax.broadcasted_iota(jnp.int32, (1, BW), 1) // (BW // 4)
    wlen = jnp.where(grp == 0, 2.0, jnp.where(grp == 1, 4.0, jnp.where(grp == 2, 8.0, 16.0)))
    t = (i * tm + first_row + lax.broadcasted_iota(jnp.int32, (n_rows, 1), 0)).astype(F32)
    return grp, 1.0 / jnp.minimum(t + 1.0, wlen)


def _pool_pick(grp, s2, s4, s8, s16):
    return jnp.where(grp == 0, s2, jnp.where(grp == 1, s4, jnp.where(grp == 2, s8, s16)))


def _pool_fwd(z, tm=256):
    t_rows = z.shape[0]
    hb = tm // PHALO

    def body(u_ref, h_ref, o_ref):
        i = pl.program_id(0)
        u = u_ref[...]
        win = jnp.concatenate([h_ref[...] * jnp.where(i > 0, 1.0, 0.0), u], axis=0)
        s2 = win + pltpu.roll(win, 1, 0)
        s4 = s2 + pltpu.roll(s2, 2, 0)
        s8 = s4 + pltpu.roll(s4, 4, 0)
        s16 = s8 + pltpu.roll(s8, 8, 0)
        grp, inv = _pool_rows(i, tm, tm, 0)
        o_ref[...] = _pool_pick(grp, s2, s4, s8, s16)[PHALO:, :] * inv - u

    return _call(
        body, name="pool_fwd", grid=(t_rows // tm,),
        in_specs=[pl.BlockSpec((tm, BW), lambda i: (i, 3)),
                  pl.BlockSpec((PHALO, BW), lambda i: (jnp.maximum(i * hb - 1, 0), 3))],
        out_specs=pl.BlockSpec((tm, BW), lambda i: (i, 0)),
        out_shape=jax.ShapeDtypeStruct((t_rows, BW), F32),
        compiler_params=_cparams(("parallel",)))(z, z)


def _pool_bwd(dp, tm=256):
    t_rows = dp.shape[0]
    nt = t_rows // tm
    hb = tm // PHALO
    ln = tm + PHALO

    def body(d_ref, dn_ref, o_ref):
        i = pl.program_id(0)
        d = d_ref[...]
        grp, inv = _pool_rows(i, tm, ln, 0)
        win = jnp.concatenate([d, dn_ref[...] * jnp.where(i < nt - 1, 1.0, 0.0)], axis=0) * inv
        s2 = win + pltpu.roll(win, ln - 1, 0)
        s4 = s2 + pltpu.roll(s2, ln - 2, 0)
        s8 = s4 + pltpu.roll(s4, ln - 4, 0)
        s16 = s8 + pltpu.roll(s8, ln - 8, 0)
        o_ref[...] = (_pool_pick(grp, s2, s4, s8, s16)[:tm, :] - d).astype(BF16)

    return _call(
        body, name="pool_bwd", grid=(nt,),
        in_specs=[pl.BlockSpec((tm, BW), lambda i: (i, 0)),
                  pl.BlockSpec((PHALO, BW), lambda i: (jnp.minimum((i + 1) * hb, t_rows // PHALO - 1), 0))],
        out_specs=pl.BlockSpec((tm, BW), lambda i: (i, 0)),
        out_shape=jax.ShapeDtypeStruct((t_rows, BW), BF16),
        compiler_params=_cparams(("parallel",)))(dp, dp)


_MERGE_W = ("wglu", "bglu", "wpa", "lng", "lnb", "wpb", "wgrp", "scale", "wpc", "bgate", "wout")
_MERGE_SMALL = ("bglu", "lng", "lnb", "scale", "bgate")


def _merge_math(x, yssm, cv, pbar, zg, w, taps):
    t_glu, t_ya, t_yb, t_p, t_yc = taps
    g = jax.nn.gelu(yssm)
    outa = g * jax.nn.sigmoid(_mm(g, w["wglu"]) + t_glu + w["bglu"])
    ya = _mm(outa, w["wpa"]) + t_ya
    mu = jnp.mean(cv, axis=-1, keepdims=True)
    var = jnp.mean(jnp.square(cv - mu), axis=-1, keepdims=True)
    hs = jax.nn.silu((cv - mu) * lax.rsqrt(var + EPS) * w["lng"] + w["lnb"])
    yb = _mm(hs, w["wpb"]) + t_yb
    gw = BW // 4
    pk = jnp.concatenate([_mm(pbar[:, gw * k:gw * (k + 1)], w["wgrp"][k]) for k in range(4)], axis=1) + t_p
    pc = pk * w["scale"]
    yc = _mm(pc, w["wpc"]) + t_yc
    gates = jax.nn.sigmoid(zg + w["bgate"])
    merged = gates[:, :D] * ya + gates[:, D:2 * D] * yb + gates[:, 2 * D:] * yc
    x1 = x + _mm(merged, w["wout"])
    acts = tuple(a.astype(BF16) for a in (g, outa, hs, pbar, pc, merged))
    return x1, acts


def _merge_specs(tm, p):
    rows = lambda width, col=0: pl.BlockSpec((tm, width), lambda i, c=col: (i, c))
    data = [rows(D), rows(BW), rows(BW), rows(BW), rows(D, 2), rows(D, 3), rows(D, 4)]
    wspecs = []
    for name in _MERGE_W:
        nd = p[name].ndim
        wspecs.append(pl.BlockSpec(p[name].shape, lambda i, nd=nd: (0,) * nd))
    return rows, data, wspecs


def _merge_fwd(x, yssm, cv, pbar, z, p, tm=256):
    t_rows = x.shape[0]
    rows, data, wspecs = _merge_specs(tm, p)

    def body(x_ref, y_ref, cv_ref, pb_ref, za_ref, zb_ref, zc_ref, *rest):
        w = {name: r[...] for name, r in zip(_MERGE_W, rest[:len(_MERGE_W)])}
        o_ref = rest[len(_MERGE_W)]
        taps = (0.0, 0.0, 0.0, 0.0, 0.0)
        zg = jnp.concatenate([za_ref[...], zb_ref[...], zc_ref[...]], axis=1)
        o_ref[...] = _merge_math(x_ref[...], y_ref[...], cv_ref[...], pb_ref[...], zg, w, taps)[0]

    return _call(
        body, name="merge_fwd", grid=(t_rows // tm,), in_specs=data + wspecs, out_specs=rows(D),
        out_shape=jax.ShapeDtypeStruct((t_rows, D), F32),
        compiler_params=_cparams(("parallel",)))(x, yssm, cv, pbar, z, z, z, *[p[n] for n in _MERGE_W])


def _merge_bwd(dx1, x, yssm, cv, pbar, z, p, tm=256):
    t_rows = x.shape[0]
    rows, data, wspecs = _merge_specs(tm, p)
    nw = len(_MERGE_W)

    def body(dx_ref, x_ref, y_ref, cv_ref, pb_ref, za_ref, zb_ref, zc_ref, *rest):
        w = {name: r[...] for name, r in zip(_MERGE_W, rest[:nw])}
        zg = jnp.concatenate([za_ref[...], zb_ref[...], zc_ref[...]], axis=1)
        outs = rest[nw:]
        small = {n: w[n] for n in _MERGE_SMALL}
        taps = (jnp.zeros((tm, BW), F32), jnp.zeros((tm, D), F32), jnp.zeros((tm, D), F32),
                jnp.zeros((tm, BW), F32), jnp.zeros((tm, D), F32))

        def f(yssm_, cv_, pbar_, zg_, small_, taps_):
            return _merge_math(x_ref[...], yssm_, cv_, pbar_, zg_, {**w, **small_}, taps_)

        _, vjp, acts = jax.vjp(f, y_ref[...], cv_ref[...], pb_ref[...], zg, small, taps, has_aux=True)
        dy, dcv, dpb, dzg, dsmall, dtaps = vjp(dx_ref[...])
        outs[0][...] = dy
        outs[1][...] = dcv
        outs[2][...] = dpb
        outs[3][...] = dzg.astype(BF16)
        for k in range(6):
            outs[4 + k][...] = acts[k]
        for k in range(5):
            outs[10 + k][...] = dtaps[k].astype(BF16)

        @pl.when(pl.program_id(0) == 0)
        def _():
            for k in range(5):
                outs[15 + k][...] = jnp.zeros_like(outs[15 + k])

        for k, n in enumerate(_MERGE_SMALL):
            outs[15 + k][...] += dsmall[n]

    f32o = lambda width: jax.ShapeDtypeStruct((t_rows, width), F32)
    bfo = lambda width: jax.ShapeDtypeStruct((t_rows, width), BF16)
    small_shapes = [jax.ShapeDtypeStruct(p[n].shape, F32) for n in _MERGE_SMALL]
    small_specs = [pl.BlockSpec(p[n].shape, lambda i: (0, 0)) for n in _MERGE_SMALL]
    out_shape = ([f32o(BW), f32o(BW), f32o(BW), bfo(3 * D)]
                 + [bfo(BW), bfo(BW), bfo(BW), bfo(BW), bfo(BW), bfo(D)]
                 + [bfo(BW), bfo(D), bfo(D), bfo(BW), bfo(D)] + small_shapes)
    out_specs = ([rows(BW), rows(BW), rows(BW), rows(3 * D)]
                 + [rows(BW)] * 5 + [rows(D)]
                 + [rows(BW), rows(D), rows(D), rows(BW), rows(D)] + small_specs)
    return _call(
        body, name="merge_bwd", grid=(t_rows // tm,), in_specs=[rows(D)] + data + wspecs, out_specs=out_specs,
        out_shape=out_shape, compiler_params=_cparams(("arbitrary",)))(
            dx1, x, yssm, cv, pbar, z, z, z, *[p[n] for n in _MERGE_W])


def _ffn_fwd(x1, gamma, wg, wu, wd, tm=512, th=512):
    t_rows = x1.shape[0]
    nh = HIDP // th

    def body(x_ref, g_ref, wg_ref, wu_ref, wd_ref, o_ref, gp_ref, up_ref, h_ref, acc_ref):
        j = pl.program_id(1)

        @pl.when(j == 0)
        def _():
            h_ref[...] = _rms(x_ref[...], g_ref[...]).astype(BF16)
            acc_ref[...] = jnp.zeros_like(acc_ref)

        gp = jnp.dot(h_ref[...], wg_ref[...], preferred_element_type=F32)
        up = jnp.dot(h_ref[...], wu_ref[...], preferred_element_type=F32)
        gp_ref[...] = gp
        up_ref[...] = up
        acc_ref[...] += _dot(jax.nn.silu(gp) * up, wd_ref[...])

        @pl.when(j == nh - 1)
        def _():
            o_ref[...] = x_ref[...] + acc_ref[...]

    return _call(
        body, name="ffn_fwd", grid=(t_rows // tm, nh),
        in_specs=[pl.BlockSpec((tm, D), lambda i, j: (i, 0)), pl.BlockSpec((1, D), lambda i, j: (0, 0)),
                  pl.BlockSpec((D, th), lambda i, j: (0, j)), pl.BlockSpec((D, th), lambda i, j: (0, j)),
                  pl.BlockSpec((th, D), lambda i, j: (j, 0))],
        out_specs=[pl.BlockSpec((tm, D), lambda i, j: (i, 0)), pl.BlockSpec((tm, th), lambda i, j: (i, j)),
                   pl.BlockSpec((tm, th), lambda i, j: (i, j))],
        out_shape=[jax.ShapeDtypeStruct((t_rows, D), F32), jax.ShapeDtypeStruct((t_rows, HIDP), F32),
                   jax.ShapeDtypeStruct((t_rows, HIDP), F32)],
        scratch_shapes=[pltpu.VMEM((tm, D), BF16), pltpu.VMEM((tm, D), F32)],
        compiler_params=_cparams(("parallel", "arbitrary")))(x1, gamma, wg, wu, wd)


def _rms_bwd_tail(x, gamma, dh):
    _, vjp = jax.vjp(_rms, x, gamma)
    return vjp(dh)


def _ffn_bwd(dx2, x1, gamma, gpre, upre, wg, wu, wd, tm=512, th=512):
    t_rows = x1.shape[0]
    nh = HIDP // th

    def body(d_ref, x_ref, g_ref, gp_ref, up_ref, wg_ref, wu_ref, wd_ref,
             dx_ref, dgam_ref, dgp_ref, dup_ref, act_ref, h_ref, acc_ref):
        i = pl.program_id(0)
        j = pl.program_id(1)

        @pl.when(j == 0)
        def _():
            acc_ref[...] = jnp.zeros_like(acc_ref)

        @pl.when((i == 0) & (j == 0))
        def _():
            dgam_ref[...] = jnp.zeros_like(dgam_ref)

        dact = _dot_nt(d_ref[...], wd_ref[...])
        gp = gp_ref[...]
        up = up_ref[...]
        sg = jax.nn.sigmoid(gp)
        silu = gp * sg
        dgp = (dact * up * (sg * (1.0 + gp * (1.0 - sg)))).astype(BF16)
        dup = (dact * silu).astype(BF16)
        dgp_ref[...] = dgp
        dup_ref[...] = dup
        act_ref[...] = (silu * up).astype(BF16)
        acc_ref[...] += _dot_nt(dgp, wg_ref[...]) + _dot_nt(dup, wu_ref[...])

        @pl.when(j == nh - 1)
        def _():
            x = x_ref[...]
            h_ref[...] = _rms(x, g_ref[...]).astype(BF16)
            dx, dgam = _rms_bwd_tail(x, g_ref[...], acc_ref[...])
            dx_ref[...] = d_ref[...] + dx
            dgam_ref[...] += dgam

    row_d = pl.BlockSpec((tm, D), lambda i, j: (i, 0))
    row_h = pl.BlockSpec((tm, th), lambda i, j: (i, j))
    return _call(
        body, name="ffn_bwd", grid=(t_rows // tm, nh),
        in_specs=[row_d, row_d, pl.BlockSpec((1, D), lambda i, j: (0, 0)), row_h, row_h,
                  pl.BlockSpec((D, th), lambda i, j: (0, j)), pl.BlockSpec((D, th), lambda i, j: (0, j)),
                  pl.BlockSpec((th, D), lambda i, j: (j, 0))],
        out_specs=[row_d, pl.BlockSpec((1, D), lambda i, j: (0, 0)), row_h, row_h, row_h, row_d],
        out_shape=[jax.ShapeDtypeStruct((t_rows, D), F32), jax.ShapeDtypeStruct((1, D), F32),
                   jax.ShapeDtypeStruct((t_rows, HIDP), BF16), jax.ShapeDtypeStruct((t_rows, HIDP), BF16),
                   jax.ShapeDtypeStruct((t_rows, HIDP), BF16), jax.ShapeDtypeStruct((t_rows, D), BF16)],
        scratch_shapes=[pltpu.VMEM((tm, D), F32)],
        compiler_params=_cparams(("arbitrary", "arbitrary")))(dx2, x1, gamma, gpre, upre, wg, wu, wd)


def _inproj_bwd(dz, dx1, x, gamma, w, tm=512, tn=1280):
    t_rows = x.shape[0]
    nn = IN_W // tn

    def body(dz_ref, d1_ref, x_ref, g_ref, w_ref, dx_ref, dgam_ref, h_ref, acc_ref):
        i = pl.program_id(0)
        j = pl.program_id(1)

        @pl.when(j == 0)
        def _():
            acc_ref[...] = jnp.zeros_like(acc_ref)

        @pl.when((i == 0) & (j == 0))
        def _():
            dgam_ref[...] = jnp.zeros_like(dgam_ref)

        acc_ref[...] += _dot_nt(dz_ref[...], w_ref[...])

        @pl.when(j == nn - 1)
        def _():
            x = x_ref[...]
            h_ref[...] = _rms(x, g_ref[...]).astype(BF16)
            dx, dgam = _rms_bwd_tail(x, g_ref[...], acc_ref[...])
            dx_ref[...] = d1_ref[...] + dx
            dgam_ref[...] += dgam

    row_d = pl.BlockSpec((tm, D), lambda i, j: (i, 0))
    return _call(
        body, name="inproj_bwd", grid=(t_rows // tm, nn),
        in_specs=[pl.BlockSpec((tm, tn), lambda i, j: (i, j)), row_d, row_d, pl.BlockSpec((1, D), lambda i, j: (0, 0)),
                  pl.BlockSpec((D, tn), lambda i, j: (0, j))],
        out_specs=[row_d, pl.BlockSpec((1, D), lambda i, j: (0, 0)), row_d],
        out_shape=[jax.ShapeDtypeStruct((t_rows, D), F32), jax.ShapeDtypeStruct((1, D), F32),
                   jax.ShapeDtypeStruct((t_rows, D), BF16)],
        scratch_shapes=[pltpu.VMEM((tm, D), F32)],
        compiler_params=_cparams(("arbitrary", "arbitrary")))(dz, dx1, x, gamma, w)


def _matmul_tn(a, b, name, tt=512):
    t_rows, k = a.shape
    n = b.shape[1]
    tk = min(k, 1024)
    tn = min(n, 512)
    nt = t_rows // tt

    def body(a_ref, b_ref, o_ref, acc_ref):
        t = pl.program_id(2)

        @pl.when(t == 0)
        def _():
            acc_ref[...] = jnp.zeros_like(acc_ref)

        acc_ref[...] += _dot_tn(a_ref[...], b_ref[...])

        @pl.when(t == nt - 1)
        def _():
            o_ref[...] = acc_ref[...].astype(BF16)

    return _call(
        body, name=name, grid=(k // tk, n // tn, nt),
        in_specs=[pl.BlockSpec((tt, tk), lambda i, j, t: (t, i)), pl.BlockSpec((tt, tn), lambda i, j, t: (t, j))],
        out_specs=pl.BlockSpec((tk, tn), lambda i, j, t: (i, j)),
        out_shape=jax.ShapeDtypeStruct((k, n), BF16),
        scratch_shapes=[pltpu.VMEM((tk, tn), F32)],
        compiler_params=_cparams(("parallel", "parallel", "arbitrary")))(a, b)


def _group_tn(a, b):
    t_rows = a.shape[0]
    gw = BW // 4

    def body(a_ref, b_ref, o_ref):
        o_ref[...] = _dot_tn(a_ref[...], b_ref[...])

    return _call(
        body, name="pool_group_tn", grid=(4,),
        in_specs=[pl.BlockSpec((t_rows, gw), lambda k: (0, k)), pl.BlockSpec((t_rows, gw), lambda k: (0, k))],
        out_specs=pl.BlockSpec((None, gw, gw), lambda k: (k, 0, 0)),
        out_shape=jax.ShapeDtypeStruct((4, gw, gw), F32),
        compiler_params=_cparams(("parallel",)))(a, b)


def _loss_head(x2, gamma, target, tm=512):
    t_rows = x2.shape[0]

    def body(x_ref, g_ref, t_ref, loss_ref, dx_ref, dgam_ref):
        @pl.when(pl.program_id(0) == 0)
        def _():
            loss_ref[...] = jnp.zeros_like(loss_ref)
            dgam_ref[...] = jnp.zeros_like(dgam_ref)

        def f(x, g):
            err = jnp.square(_rms(x, g) - t_ref[...])
            return 0.5 * jnp.sum(jnp.mean(err, axis=-1, keepdims=True), axis=0, keepdims=True)

        loss, vjp = jax.vjp(f, x_ref[...], g_ref[...])
        dx, dgam = vjp(jnp.ones((1, 1), F32))
        loss_ref[...] += jnp.broadcast_to(loss, (1, 128))
        dx_ref[...] = dx
        dgam_ref[...] += dgam

    row_d = pl.BlockSpec((tm, D), lambda i: (i, 0))
    return _call(
        body, name="loss_head", grid=(t_rows // tm,),
        in_specs=[row_d, pl.BlockSpec((1, D), lambda i: (0, 0)), row_d],
        out_specs=[pl.BlockSpec((1, 128), lambda i: (0, 0)), row_d, pl.BlockSpec((1, D), lambda i: (0, 0))],
        out_shape=[jax.ShapeDtypeStruct((1, 128), F32), jax.ShapeDtypeStruct((t_rows, D), F32),
                   jax.ShapeDtypeStruct((1, D), F32)],
        compiler_params=_cparams(("arbitrary",)))(x2, gamma, target)


NCHIP = NDEV // 2


def _coords():
    return lax.axis_index("x"), lax.axis_index("y"), lax.axis_index("c")


def _remote(src, dst, send_sem, recv_sem, peer):
    return pltpu.make_async_remote_copy(src_ref=src, dst_ref=dst, send_sem=send_sem, recv_sem=recv_sem,
                                        device_id=peer, device_id_type=MESH)


def _comm_call(name, srcs, out_shapes, n_rec, plan):
    ns, no = len(srcs), len(out_shapes)

    def body(*refs):
        ins, outs = refs[:ns], refs[ns:ns + no]
        loc_sem, send_sem, recv_sem = refs[ns + no:]
        x, y, c = _coords()
        recs = plan(ins, outs, x, y, c)
        assert len(recs) == n_rec
        for k, r in enumerate(recs):
            for src, dst in r.get("local", ()):
                pltpu.make_async_copy(src, dst, loc_sem.at[k]).start()
            for peer, src, dst in r.get("remote", ()):
                _remote(src, dst, send_sem.at[k], recv_sem.at[k], peer).start()
        for k, r in enumerate(recs):
            if r.get("recv_wait") is not None:
                w = r["recv_wait"]
                _remote(w, w, send_sem.at[k], recv_sem.at[k], (x, y, c)).wait_recv()
            if r.get("send_wait") is not None:
                w = r["send_wait"]
                _remote(w, w, send_sem.at[k], recv_sem.at[k], (x, y, c)).wait_send()
            if r.get("local_wait") is not None:
                w = r["local_wait"]
                pltpu.make_async_copy(w, w, loc_sem.at[k]).wait()

    return _call(
        body, name=name, in_specs=[ANY] * ns, out_specs=[ANY] * no, out_shape=out_shapes,
        scratch_shapes=[pltpu.SemaphoreType.DMA((n_rec,))] * 3)(*srcs)


def _gather_call(srcs, out_shapes, items):
    ns, no, n = len(srcs), len(out_shapes), len(items)

    def body(*refs):
        ins, outs = refs[:ns], refs[ns:ns + no]
        loc, sib_s, sib_r, ici_s, ici_r, fwd_s, fwd_r = refs[ns + no:]
        x, y, c = _coords()
        me, sib = (x, y, c), (x, y, 1 - c)
        chips = [(1 - x, y), (x, 1 - y), (1 - x, 1 - y)]
        index = lambda px, py, pc: 4 * px + 2 * py + pc
        for k, (si, oi, shard, block, _) in enumerate(items):
            src, mine = shard(ins[si]), block(outs[oi], index(*me))
            pltpu.make_async_copy(src, mine, loc.at[k]).start()
            _remote(src, mine, sib_s.at[k], sib_r.at[k], sib).start()
            for chip in chips:
                _remote(src, mine, ici_s.at[k], ici_r.at[k], (*chip, c)).start()
        for k, (si, oi, _, block, blocks) in enumerate(items):
            three = blocks(outs[oi], 3)
            _remote(three, three, ici_s.at[k], ici_r.at[k], me).wait_recv()
            for chip in chips:
                landed = block(outs[oi], index(*chip, c))
                _remote(landed, landed, fwd_s.at[k], fwd_r.at[k], sib).start()
        for k, (si, oi, _, _, blocks) in enumerate(items):
            one, three = blocks(outs[oi], 1), blocks(outs[oi], 3)
            _remote(one, one, sib_s.at[k], sib_r.at[k], me).wait()
            _remote(three, three, fwd_s.at[k], fwd_r.at[k], me).wait()
            _remote(three, three, ici_s.at[k], ici_r.at[k], me).wait_send()
            pltpu.make_async_copy(one, one, loc.at[k]).wait()

    return _call(
        body, name="gather_weights", in_specs=[ANY] * ns, out_specs=[ANY] * no, out_shape=out_shapes,
        scratch_shapes=[pltpu.SemaphoreType.DMA((n,))] * 7)(*srcs)


_BIG = {
    "w_in": (1, D, IN_W // NDEV, D, IN_W),
    "ssm_w_glu": (0, BW // NDEV, BW, BW, BW),
    "ssm_w_proj": (1, BW, D // NDEV, BW, D),
    "conv_w_proj": (1, BW, D // NDEV, BW, D),
    "pool_w_proj": (1, BW, D // NDEV, BW, D),
    "w_out": (0, D // NDEV, D, D, D),
    "ffn_w_gate": (1, D, HPAD, D, HIDP),
    "ffn_w_up": (1, D, HPAD, D, HIDP),
    "ffn_w_down": (0, HPAD, D, HIDP, D),
}


def _block_view(axis, size):
    if axis == 1:
        return lambda ref, q: ref.at[:, pl.ds(pl.multiple_of(q * size, 128), size)]
    return lambda ref, q: ref.at[pl.ds(pl.multiple_of(q * size, 16), size), :]


def _blocks_view(axis, size):
    if axis == 1:
        return lambda ref, n: ref.at[:, pl.ds(0, n * size)]
    return lambda ref, n: ref.at[pl.ds(0, n * size), :]


def _gather_weights(shards, conv_dw):
    srcs, outs, items, where = [], [], [], {}
    for name, (axis, kk, nn, kf, nf) in _BIG.items():
        srcs.append(shards[name])
        size = nn if axis == 1 else kk
        for l in range(DEPTH):
            where[(name, l)] = len(outs)
            outs.append(jax.ShapeDtypeStruct((kf, nf), BF16))
            items.append((len(srcs) - 1, len(outs) - 1, lambda ref, l=l: ref.at[l], _block_view(axis, size),
                          _blocks_view(axis, size)))
    srcs.append(conv_dw)
    outs.append(jax.ShapeDtypeStruct((NDEV,) + conv_dw.shape, conv_dw.dtype))
    items.append((len(srcs) - 1, len(outs) - 1, lambda ref: ref, lambda ref, q: ref.at[q],
                  lambda ref, n: ref.at[pl.ds(0, n)]))
    res = _gather_call(srcs, outs, items)
    full = {name: [res[where[(name, l)]] for l in range(DEPTH)] for name in _BIG}
    return full, res[-1]


def _pair_add(name, own, rcv):
    _, nl, kk, nn = own.shape
    tk = kk if kk <= 512 else 256

    def body(a_ref, b_ref, o_ref):
        o_ref[...] = (a_ref[...].astype(F32) + b_ref[...].astype(F32)).astype(BF16)

    spec = pl.BlockSpec((None, None, tk, nn), lambda h, l, i: (h, l, i, 0))
    return _call(body, name="pair_add_" + name, grid=(NCHIP, nl, kk // tk), in_specs=[spec, spec], out_specs=spec,
                 out_shape=jax.ShapeDtypeStruct(own.shape, BF16),
                 compiler_params=_cparams(("parallel", "parallel", "parallel")))(own, rcv)


def _pair_add_small(pairs, lists):
    pn, ln = list(pairs), list(lists)
    flat = [a for n in pn for a in pairs[n]]
    for n in ln:
        flat += list(lists[n][0]) + [lists[n][1]]

    def body(*refs):
        outs = refs[len(flat):]
        pos = 0
        for k, n in enumerate(pn):
            outs[k][...] = refs[pos][...] + refs[pos + 1][...]
            pos += 2
        for k, n in enumerate(ln):
            nl = len(lists[n][0])
            for l in range(nl):
                outs[len(pn) + k][l] = refs[pos + l][...] + refs[pos + nl][l]
            pos += nl + 1

    shapes = [jax.ShapeDtypeStruct(pairs[n][0].shape, F32) for n in pn]
    shapes += [jax.ShapeDtypeStruct(lists[n][1].shape, F32) for n in ln]
    res = _call(body, name="pair_add_small", out_shape=shapes, compiler_params=_cparams())(*flat)
    return dict(zip(pn + ln, res))


def _reduce_grads(big, by_owner, small):
    rs = {}
    for name, arrays in big.items():
        axis, kk, nn, _, _ = _BIG[name]
        rs[name] = (arrays, _block_view(axis, nn if axis == 1 else kk), (kk, nn), BF16)
    for name, arrays in by_owner.items():
        rs[name] = (arrays, lambda ref, q: ref.at[q], arrays[0].shape[1:], F32)

    srcs, outs, plans, own_at, rcv_at = [], [], [], {}, {}
    for name, (arrays, slab, shape, dtype) in rs.items():
        own_at[name] = len(outs)
        outs.append(jax.ShapeDtypeStruct((NCHIP, len(arrays)) + shape, dtype))
        rcv_at[name] = len(outs)
        outs.append(jax.ShapeDtypeStruct((NCHIP, len(arrays)) + shape, dtype))
        for l, arr in enumerate(arrays):
            srcs.append(arr)
            plans.append((len(srcs) - 1, own_at[name], rcv_at[name], l, slab))
    for name, arrays in small.items():
        rcv_at[name] = len(outs)
        outs.append(jax.ShapeDtypeStruct((len(arrays),) + arrays[0].shape, F32))
        for l, arr in enumerate(arrays):
            srcs.append(arr)
            plans.append((len(srcs) - 1, None, rcv_at[name], l, None))

    def plan_pair(ins, out_refs, x, y, c):
        sib = (x, y, 1 - c)
        recs = []
        for si, oo, ro, l, slab in plans:
            if slab is None:
                dst = out_refs[ro].at[l]
                recs.append(dict(remote=[(sib, ins[si], dst)], send_wait=dst, recv_wait=dst))
                continue
            four = out_refs[ro].at[pl.ds(0, NCHIP), l]
            recs.append(dict(
                local=[(slab(ins[si], 2 * h + c), out_refs[oo].at[h, l]) for h in range(NCHIP)],
                remote=[(sib, slab(ins[si], 2 * h + 1 - c), out_refs[ro].at[h, l]) for h in range(NCHIP)],
                local_wait=out_refs[oo].at[pl.ds(0, NCHIP), l], send_wait=four, recv_wait=four))
        return recs

    res = _comm_call("pair_exchange", srcs, outs, len(plans), plan_pair)
    part = {name: _pair_add(name, res[own_at[name]], res[rcv_at[name]]) for name in big}
    part.update(_pair_add_small({n: (res[own_at[n]], res[rcv_at[n]]) for n in by_owner},
                                {n: (small[n], res[rcv_at[n]]) for n in small}))

    names = list(rs) + list(small)
    srcs2 = [part[n] for n in names]
    outs2 = [jax.ShapeDtypeStruct(((() if n in rs else (NCHIP,)) + part[n].shape), part[n].dtype) for n in names]

    def plan_chip(ins, out_refs, x, y, c):
        mine = 2 * x + y
        recs = []
        for k, n in enumerate(names):
            pick = (lambda h, k=k: ins[k].at[h]) if n in rs else (lambda h, k=k: ins[k])
            remote = []
            for step in range(1, NCHIP):
                h = (mine + step) % NCHIP
                remote.append(((h // 2, h % 2, c), pick(h), out_refs[k].at[mine]))
            three = out_refs[k].at[pl.ds(0, NCHIP - 1)]
            recs.append(dict(local=[(pick(mine), out_refs[k].at[mine])], remote=remote,
                             local_wait=out_refs[k].at[0], send_wait=three, recv_wait=three))
        return recs

    res2 = _comm_call("chip_exchange", srcs2, outs2, len(names), plan_chip)
    return dict(zip(names, res2))


def _adamw(w, g, m, v):
    m = ADAM_B1 * m + (1.0 - ADAM_B1) * g
    v = ADAM_B2 * v + (1.0 - ADAM_B2) * jnp.square(g)
    m_hat = m / (1.0 - ADAM_B1 ** ADAM_STEP)
    v_hat = v / (1.0 - ADAM_B2 ** ADAM_STEP)
    delta = -ADAM_LR * (m_hat / (jnp.sqrt(v_hat) + ADAM_EPS) + ADAM_WD * w)
    return delta, m, v


def _sum_senders(ref):
    g = ref[0].astype(F32)
    for h in range(1, NCHIP):
        g = g + ref[h].astype(F32)
    return g


def _adam_big(name, recv, w, m, v, tk):
    kk, nn = w.shape[1], w.shape[2]
    nnp = recv.shape[3]

    def body(r_ref, w_ref, m_ref, v_ref, g_ref, d_ref, mo_ref, vo_ref):
        g = _sum_senders(r_ref)[:, :nn]
        delta, m2, v2 = _adamw(w_ref[...], g, m_ref[...], v_ref[...])
        g_ref[...] = g
        d_ref[...] = delta
        mo_ref[...] = m2
        vo_ref[...] = v2

    wspec = pl.BlockSpec((None, tk, nn), lambda l, i: (l, i, 0))
    shape = jax.ShapeDtypeStruct(w.shape, F32)
    return _call(
        body, name="adamw_" + name, grid=(DEPTH, kk // tk),
        in_specs=[pl.BlockSpec((NCHIP, None, tk, nnp), lambda l, i: (0, l, i, 0)), wspec, wspec, wspec],
        out_specs=[wspec] * 4, out_shape=[shape] * 4,
        compiler_params=_cparams(("parallel", "parallel")))(recv, w, m, v)


def _adam_small(names, recv, w, m, v):
    n = len(names)

    def body(*refs):
        r, ww, mm, vv = refs[:n], refs[n:2 * n], refs[2 * n:3 * n], refs[3 * n:4 * n]
        outs = refs[4 * n:]
        for k in range(n):
            g = _sum_senders(r[k])
            if g.shape != ww[k].shape:
                g = g[:, :ww[k].shape[1]]
            delta, m2, v2 = _adamw(ww[k][...], g, mm[k][...], vv[k][...])
            outs[k][...] = g
            outs[n + k][...] = delta
            outs[2 * n + k][...] = m2
            outs[3 * n + k][...] = v2

    shapes = [jax.ShapeDtypeStruct(w[k].shape, F32) for k in names]
    res = _call(body, name="adamw_small", out_shape=shapes * 4, compiler_params=_cparams())(
        *[recv[k] for k in names], *[w[k] for k in names], *[m[k] for k in names], *[v[k] for k in names])
    return {k: (res[i], res[n + i], res[2 * n + i], res[3 * n + i]) for i, k in enumerate(names)}


def _expand_b(b):
    bt = jnp.transpose(b, (0, 2, 1)).reshape(NBLK, GB, SGRP, NSTATE)
    eye = jnp.eye(GB, dtype=b.dtype)
    return jnp.einsum("jgpn,gh->jgphn", bt, eye).reshape(NBLK, GB * SGRP, NS)


def _extract_b(db):
    x = db.reshape(NBLK, GB, SGRP, GB, NSTATE)
    eye = jnp.eye(GB, dtype=db.dtype)
    d = jnp.einsum("jgphn,gh->jgpn", x, eye).reshape(NGRP, SGRP, NSTATE)
    return jnp.transpose(d, (0, 2, 1)).reshape(NGRP, NSTATE * SGRP)


def _expand_c(c):
    ct = jnp.transpose(c, (0, 2, 1)).reshape(NBLK, GB, NSTATE, SGRP)
    eye = jnp.eye(GB, dtype=c.dtype)
    return jnp.einsum("jgnp,gh->jgnhp", ct, eye).reshape(NBLK, NS, GB * SGRP)


def _extract_c(dc):
    x = dc.reshape(NBLK, GB, NSTATE, GB, SGRP)
    eye = jnp.eye(GB, dtype=dc.dtype)
    d = jnp.einsum("jgnhp,gh->jgnp", x, eye).reshape(NGRP, NSTATE, SGRP)
    return jnp.transpose(d, (0, 2, 1))


_SMALL = ("norm1", "b_gate", "ssm_a_re", "ssm_a_im", "ssm_log_dt", "ssm_b_re", "ssm_b_im", "ssm_c_re", "ssm_c_im",
          "ssm_d", "ssm_b_glu", "conv_b_dw", "conv_ln_g", "conv_ln_b", "pool_w_group", "pool_scale", "norm2")
_ADAM_TK = {"w_in": 256, "ssm_w_glu": 64, "ssm_w_proj": 512, "conv_w_proj": 512, "pool_w_proj": 512, "w_out": 128,
            "ffn_w_gate": 256, "ffn_w_up": 256, "ffn_w_down": HSH}
_OUT_ORDER = ("norm1", "w_in", "b_gate", "ssm_a_re", "ssm_a_im", "ssm_log_dt", "ssm_b_re", "ssm_b_im", "ssm_c_re",
              "ssm_c_im", "ssm_d", "ssm_w_glu", "ssm_b_glu", "ssm_w_proj", "conv_w_dw", "conv_b_dw", "conv_ln_g",
              "conv_ln_b", "conv_w_proj", "pool_w_group", "pool_scale", "pool_w_proj", "w_out", "norm2", "ffn_w_gate",
              "ffn_w_up", "ffn_w_down", "final_norm")


def _layer_fwd(x, p):
    z = _inproj_fwd(x, p["norm1"], p["w_in"])
    yssm, hre, him = _ssm_fwd(z, p)
    cv = _conv_fwd(z, p["conv_w"], p["conv_b"])
    pbar = _pool_fwd(z)
    x1 = _merge_fwd(x, yssm, cv, pbar, z, p)
    x2, gpre, upre = _ffn_fwd(x1, p["norm2"], p["wg"], p["wu"], p["wd"])
    return x2, dict(x=x, z=z, yssm=yssm, hre=hre, him=him, cv=cv, pbar=pbar, x1=x1, gpre=gpre, upre=upre)


def _layer_bwd(dx, p, s):
    big, small = {}, {}
    dx1, d_norm2, dgp, dup, act, h2 = _ffn_bwd(dx, s["x1"], p["norm2"], s["gpre"], s["upre"], p["wg"], p["wu"], p["wd"])
    big["ffn_w_gate"] = _matmul_tn(h2, dgp, "tn_gate")
    big["ffn_w_up"] = _matmul_tn(h2, dup, "tn_up")
    big["ffn_w_down"] = _matmul_tn(act, dx, "tn_down")
    (dy, dcv, dpb, dzg, a_g, a_outa, a_hs, a_pb, a_pc, a_mg, c_glu, c_ya, c_yb, c_p, c_yc,
     d_bglu, d_lng, d_lnb, d_scale, d_bgate) = _merge_bwd(dx1, s["x"], s["yssm"], s["cv"], s["pbar"], s["z"], p)
    big["ssm_w_glu"] = _matmul_tn(a_g, c_glu, "tn_glu")
    big["ssm_w_proj"] = _matmul_tn(a_outa, c_ya, "tn_ssm_proj")
    big["conv_w_proj"] = _matmul_tn(a_hs, c_yb, "tn_conv_proj")
    big["pool_w_proj"] = _matmul_tn(a_pc, c_yc, "tn_pool_proj")
    big["w_out"] = _matmul_tn(a_mg, dx1, "tn_out")
    d_wgrp = _group_tn(a_pb, c_p)
    du_a, dbr, dbi, dcr, dci, dd, dar, dai, dldt = _ssm_bwd(dy, s["z"], s["hre"], s["him"], p)
    dva, dvb, dw8, dcb = _conv_bwd(dcv, s["z"], p["conv_w"])
    du_c = _pool_bwd(dpb)
    dz = jnp.concatenate([du_a, dva, dvb, du_c, dzg], axis=1)
    dx0, d_norm1, h = _inproj_bwd(dz, dx1, s["x"], p["norm1"], p["w_in"])
    big["w_in"] = _matmul_tn(h, dz, "tn_in")
    small["norm1"] = d_norm1
    small["b_gate"] = d_bgate
    small["ssm_a_re"] = dar.reshape(NGRP, NSTATE)
    small["ssm_a_im"] = dai.reshape(NGRP, NSTATE)
    small["ssm_log_dt"] = dldt.reshape(NBLK, 8, 128)[:, 0, :GB].reshape(1, NGRP)
    small["ssm_b_re"] = _extract_b(dbr)
    small["ssm_b_im"] = _extract_b(dbi)
    small["ssm_c_re"] = _extract_c(dcr)
    small["ssm_c_im"] = _extract_c(dci)
    small["ssm_d"] = dd.reshape(NGRP, SGRP)
    small["ssm_b_glu"] = d_bglu
    small["conv_b_dw"] = dcb
    small["conv_ln_g"] = d_lng
    small["conv_ln_b"] = d_lnb
    small["pool_w_group"] = d_wgrp
    small["pool_scale"] = d_scale
    small["norm2"] = d_norm2
    return dx0, big, dw8, small


def _train_step(a):
    t_rows = a["x"].shape[1]
    x0 = a["x"].reshape(t_rows, D)
    target = a["loss_target"].reshape(t_rows, D)

    pad_cols = lambda w: jnp.pad(w, ((0, 0), (0, 0), (0, HPAD - HSH)))
    shards = {
        "w_in": a["w_in"], "ssm_w_glu": a["ssm_w_glu"], "ssm_w_proj": a["ssm_w_proj"],
        "conv_w_proj": a["conv_w_proj"], "pool_w_proj": a["pool_w_proj"], "w_out": a["w_out"],
        "ffn_w_gate": pad_cols(a["ffn_w_gate"]), "ffn_w_up": pad_cols(a["ffn_w_up"]),
        "ffn_w_down": jnp.pad(a["ffn_w_down"], ((0, 0), (0, HPAD - HSH), (0, 0))),
    }
    shards = {k: v.astype(BF16) for k, v in shards.items()}
    full, dw_all = _gather_weights(shards, a["conv_w_dw"].reshape(DEPTH, CONV_K, BW // NDEV))
    conv_w = jnp.transpose(dw_all, (1, 2, 0, 3)).reshape(DEPTH, CONV_K, BW)

    def layer_params(l):
        row = lambda v: v.reshape(1, -1)
        return dict(
            norm1=row(a["norm1"][l]), w_in=full["w_in"][l],
            are=row(a["ssm_a_re"][l]), aim=row(a["ssm_a_im"][l]),
            ldt=row(jnp.repeat(a["ssm_log_dt"][l], NSTATE)),
            bexp_re=_expand_b(a["ssm_b_re"][l]), bexp_im=_expand_b(a["ssm_b_im"][l]),
            cexp_re=_expand_c(a["ssm_c_re"][l]), cexp_im=_expand_c(a["ssm_c_im"][l]),
            dskip=row(a["ssm_d"][l]),
            conv_w=conv_w[l], conv_b=row(a["conv_b_dw"][l]),
            wglu=full["ssm_w_glu"][l], bglu=row(a["ssm_b_glu"][l]), wpa=full["ssm_w_proj"][l],
            lng=row(a["conv_ln_g"][l]), lnb=row(a["conv_ln_b"][l]), wpb=full["conv_w_proj"][l],
            wgrp=a["pool_w_group"][l].astype(BF16), scale=row(a["pool_scale"][l]), wpc=full["pool_w_proj"][l],
            bgate=row(a["b_gate"][l]), wout=full["w_out"][l],
            norm2=row(a["norm2"][l]), wg=full["ffn_w_gate"][l], wu=full["ffn_w_up"][l], wd=full["ffn_w_down"][l],
        )

    params = [layer_params(l) for l in range(DEPTH)]

    saved = []
    x = x0
    for l in range(DEPTH):
        x, s = _layer_fwd(x, params[l])
        saved.append(s)

    loss_part, dx, d_final = _loss_head(x, a["final_norm"].reshape(1, D), target)
    loss = lax.psum(loss_part[0, 0], ("x", "y", "c"))

    big = {name: [None] * DEPTH for name in _BIG}
    by_owner = {"conv_w_dw": [None] * DEPTH}
    small = {name: [None] * DEPTH for name in _SMALL}
    for l in reversed(range(DEPTH)):
        dx, gb, go, gs = _layer_bwd(dx, params[l], saved[l])
        for name, g in gb.items():
            big[name][l] = g
        by_owner["conv_w_dw"][l] = go
        for name, g in gs.items():
            small[name][l] = g
    small["final_norm"] = [d_final]
    grad_x = dx.reshape(1, t_rows, D)

    recv = _reduce_grads(big, by_owner, small)
    results = {}
    for name in _BIG:
        results[name] = _adam_big(name, recv[name], a[name], a["m_" + name], a["v_" + name], _ADAM_TK[name])

    lay = {
        "norm1": lambda v: v.reshape(DEPTH, 1, D), "b_gate": lambda v: v.reshape(DEPTH, 1, 3 * D),
        "ssm_log_dt": lambda v: v.reshape(DEPTH, 1, NGRP),
        "ssm_b_re": lambda v: v.reshape(DEPTH, NGRP, NSTATE * SGRP), "ssm_b_im": lambda v: v.reshape(DEPTH, NGRP, NSTATE * SGRP),
        "ssm_b_glu": lambda v: v.reshape(DEPTH, 1, BW), "conv_b_dw": lambda v: v.reshape(DEPTH, 1, BW),
        "conv_ln_g": lambda v: v.reshape(DEPTH, 1, BW), "conv_ln_b": lambda v: v.reshape(DEPTH, 1, BW),
        "pool_scale": lambda v: v.reshape(DEPTH, 1, BW), "norm2": lambda v: v.reshape(DEPTH, 1, D),
        "conv_w_dw": lambda v: v.reshape(DEPTH, CONV_K, BW // NDEV), "final_norm": lambda v: v.reshape(1, 1, D),
    }
    names = _SMALL + ("conv_w_dw", "final_norm")
    relay = lambda k, v: lay[k](v) if k in lay else v
    sm = _adam_small(names, recv, {k: relay(k, a[k]) for k in names}, {k: relay(k, a["m_" + k]) for k in names},
                     {k: relay(k, a["v_" + k]) for k in names})
    for k in names:
        results[k] = tuple(r.reshape(a[k].shape) for r in sm[k])

    outs = [loss, grad_x]
    for part in range(4):
        outs += [results[k][part] for k in _OUT_ORDER]
    return tuple(outs)


def kernel(x, norm1, w_in, b_gate, ssm_a_re, ssm_a_im, ssm_log_dt, ssm_b_re, ssm_b_im, ssm_c_re, ssm_c_im, ssm_d, ssm_w_glu, ssm_b_glu, ssm_w_proj, conv_w_dw, conv_b_dw, conv_ln_g, conv_ln_b, conv_w_proj, pool_w_group, pool_scale, pool_w_proj, w_out, norm2, ffn_w_gate, ffn_w_up, ffn_w_down, final_norm, loss_target, m_norm1, m_w_in, m_b_gate, m_ssm_a_re, m_ssm_a_im, m_ssm_log_dt, m_ssm_b_re, m_ssm_b_im, m_ssm_c_re, m_ssm_c_im, m_ssm_d, m_ssm_w_glu, m_ssm_b_glu, m_ssm_w_proj, m_conv_w_dw, m_conv_b_dw, m_conv_ln_g, m_conv_ln_b, m_conv_w_proj, m_pool_w_group, m_pool_scale, m_pool_w_proj, m_w_out, m_norm2, m_ffn_w_gate, m_ffn_w_up, m_ffn_w_down, m_final_norm, v_norm1, v_w_in, v_b_gate, v_ssm_a_re, v_ssm_a_im, v_ssm_log_dt, v_ssm_b_re, v_ssm_b_im, v_ssm_c_re, v_ssm_c_im, v_ssm_d, v_ssm_w_glu, v_ssm_b_glu, v_ssm_w_proj, v_conv_w_dw, v_conv_b_dw, v_conv_ln_g, v_conv_ln_b, v_conv_w_proj, v_pool_w_group, v_pool_scale, v_pool_w_proj, v_w_out, v_norm2, v_ffn_w_gate, v_ffn_w_up, v_ffn_w_down, v_final_norm):
    return _train_step(dict(locals()))
```

```python
import functools

import jax
import jax.numpy as jnp
from jax import lax
from jax.experimental import pallas as pl
from jax.experimental.pallas import tpu as pltpu

F32 = jnp.float32
BF16 = jnp.bfloat16

NDEV = 8
DEPTH = 2
D = 1024
BW = 512
NSTATE = 64
SGRP = 16
NGRP = BW // SGRP
GB = 8
NBLK = NGRP // GB
NS = GB * NSTATE
CONV_K = 31
HALO = 32
PHALO = 16
IN_W = 5120
HID = 2816
HSH = HID // NDEV
HPAD = 384
HIDP = HPAD * NDEV
EPS = 1e-6
VMEM_LIMIT = 56 * 1024 * 1024

ADAM_LR, ADAM_B1, ADAM_B2, ADAM_EPS, ADAM_WD, ADAM_STEP = 0.001, 0.9, 0.999, 1e-08, 0.01, 10

MESH = pl.DeviceIdType.MESH
ANY = pl.BlockSpec(memory_space=pl.ANY)


def _call(body, **kw):
    return pl.pallas_call(body, **kw)


def _cparams(sem=None):
    return pltpu.CompilerParams(dimension_semantics=sem, vmem_limit_bytes=VMEM_LIMIT)


def _dot(a, b):
    return jnp.dot(a.astype(BF16), b.astype(BF16), preferred_element_type=F32)


def _dot_nt(a, b):
    return lax.dot_general(a.astype(BF16), b.astype(BF16), (((1,), (1,)), ((), ())), preferred_element_type=F32)


def _dot_tn(a, b):
    return lax.dot_general(a.astype(BF16), b.astype(BF16), (((0,), (0,)), ((), ())), preferred_element_type=F32)


@jax.custom_vjp
def _mm(a, w):
    return _dot(a, w)


def _mm_fwd(a, w):
    return _dot(a, w), w


def _mm_bwd(w, ct):
    return _dot_nt(ct, w), jnp.zeros_like(w)


_mm.defvjp(_mm_fwd, _mm_bwd)


def _rms(x, g):
    return x * lax.rsqrt(jnp.mean(x * x, axis=-1, keepdims=True) + EPS) * g


def _disc(are, aim, ldt):
    dt = jnp.exp(ldt)
    mag = jnp.exp(dt * are)
    ang = dt * aim
    abr = mag * jnp.cos(ang)
    abi = mag * jnp.sin(ang)
    den = are * are + aim * aim
    nr = abr - 1.0
    fr = (nr * are + abi * aim) / den
    fi = (abi * are - nr * aim) / den
    return abr, abi, fr, fi


def _bbar(fr, fi, br, bi):
    return fr * br - fi * bi, fr * bi + fi * br


def _cmul(ar, ai, br, bi):
    return ar * br - ai * bi, ar * bi + ai * br


def _scan_rows(re_ref, im_ref, ar, ai, n_rows, reverse, hre_ref=None, him_ref=None):
    n = ar.shape[1]
    shape = (8, n)
    rows = lax.broadcasted_iota(jnp.int32, shape, 0)
    a1 = (jnp.broadcast_to(ar, shape), jnp.broadcast_to(ai, shape))
    a2 = _cmul(*a1, *a1)
    a4 = _cmul(*a2, *a2)
    pr = jnp.zeros(shape, F32)
    pi = jnp.zeros(shape, F32)
    pw = a1
    for k in range(8):
        sel = rows == ((7 - k) if reverse else k)
        pr = jnp.where(sel, pw[0], pr)
        pi = jnp.where(sel, pw[1], pi)
        pw = _cmul(*pw, *a1)
    nt = n_rows // 8
    with_acc = hre_ref is not None

    def body(i, carry):
        cr, ci = carry[0], carry[1]
        t = (nt - 1 - i) if reverse else i
        off = pl.multiple_of(t * 8, 8)
        xr = re_ref[pl.ds(off, 8), :]
        xi = im_ref[pl.ds(off, 8), :]
        for k, (kr, ki) in ((1, a1), (2, a2), (4, a4)):
            if reverse:
                keep, sh = rows < 8 - k, 8 - k
            else:
                keep, sh = rows >= k, k
            sr = jnp.where(keep, pltpu.roll(xr, sh, 0), 0.0)
            si = jnp.where(keep, pltpu.roll(xi, sh, 0), 0.0)
            xr, xi = xr + kr * sr - ki * si, xi + kr * si + ki * sr
        xr, xi = xr + pr * cr - pi * ci, xi + pr * ci + pi * cr
        re_ref[pl.ds(off, 8), :] = xr
        im_ref[pl.ds(off, 8), :] = xi
        edge = 0 if reverse else 7
        out = (jnp.broadcast_to(xr[edge:edge + 1, :], shape), jnp.broadcast_to(xi[edge:edge + 1, :], shape))
        if with_acc:
            hr = hre_ref[pl.ds(off, 8), :]
            hi = him_ref[pl.ds(off, 8), :]
            offp = pl.multiple_of(jnp.maximum(t - 1, 0) * 8, 8)
            live = jnp.where(t > 0, 1.0, 0.0)
            lr = jnp.broadcast_to(hre_ref[pl.ds(offp, 8), :][7:8, :], shape) * live
            li = jnp.broadcast_to(him_ref[pl.ds(offp, 8), :][7:8, :], shape) * live
            hpr = jnp.where(rows == 0, lr, pltpu.roll(hr, 1, 0))
            hpi = jnp.where(rows == 0, li, pltpu.roll(hi, 1, 0))
            out = out + (carry[2] + xr * hpr + xi * hpi, carry[3] + xi * hpr - xr * hpi)
        return out

    zero = jnp.zeros(shape, F32)
    init = (zero, zero, zero, zero) if with_acc else (zero, zero)
    res = lax.fori_loop(0, nt, body, init)
    return res[2:] if with_acc else None


def _inproj_fwd(x, gamma, w, tm=512, tn=1280):
    t_rows = x.shape[0]
    n = w.shape[1]

    def body(x_ref, g_ref, w_ref, z_ref, h_ref):
        @pl.when(pl.program_id(1) == 0)
        def _():
            h_ref[...] = _rms(x_ref[...], g_ref[...]).astype(BF16)
        z_ref[...] = jnp.dot(h_ref[...], w_ref[...], preferred_element_type=F32)

    return _call(
        body, name="inproj_fwd", grid=(t_rows // tm, n // tn),
        in_specs=[pl.BlockSpec((tm, D), lambda i, j: (i, 0)), pl.BlockSpec((1, D), lambda i, j: (0, 0)),
                  pl.BlockSpec((D, tn), lambda i, j: (0, j))],
        out_specs=pl.BlockSpec((tm, tn), lambda i, j: (i, j)),
        out_shape=jax.ShapeDtypeStruct((t_rows, n), F32),
        scratch_shapes=[pltpu.VMEM((tm, D), BF16)],
        compiler_params=_cparams(("parallel", "arbitrary")))(x, gamma, w)


def _ssm_specs(t_rows):
    row = pl.BlockSpec((1, NS), lambda j: (0, j))
    return dict(
        u=pl.BlockSpec((t_rows, GB * SGRP), lambda j: (0, j)),
        row=row,
        bexp=pl.BlockSpec((None, GB * SGRP, NS), lambda j: (j, 0, 0)),
        cexp=pl.BlockSpec((None, NS, GB * SGRP), lambda j: (j, 0, 0)),
        d=pl.BlockSpec((1, GB * SGRP), lambda j: (0, j)),
        h=pl.BlockSpec((t_rows, NS), lambda j: (0, j)),
    )


def _ssm_fwd(z, p):
    t_rows = z.shape[0]
    s = _ssm_specs(t_rows)

    def body(u_ref, are_ref, aim_ref, ldt_ref, br_ref, bi_ref, cr_ref, ci_ref, d_ref, y_ref, hr_ref, hi_ref):
        abr, abi, fr, fi = _disc(are_ref[...], aim_ref[...], ldt_ref[...])
        bbr, bbi = _bbar(fr, fi, br_ref[...], bi_ref[...])
        u = u_ref[...]
        hr_ref[...] = _dot(u, bbr)
        hi_ref[...] = _dot(u, bbi)
        _scan_rows(hr_ref, hi_ref, abr, abi, t_rows, False)
        y_ref[...] = _dot(hr_ref[...], cr_ref[...]) - _dot(hi_ref[...], ci_ref[...]) + d_ref[...] * u

    return _call(
        body, name="ssm_fwd", grid=(NBLK,),
        in_specs=[s["u"], s["row"], s["row"], s["row"], s["bexp"], s["bexp"], s["cexp"], s["cexp"], s["d"]],
        out_specs=[s["u"], s["h"], s["h"]],
        out_shape=[jax.ShapeDtypeStruct((t_rows, BW), F32), jax.ShapeDtypeStruct((t_rows, NGRP * NSTATE), F32),
                   jax.ShapeDtypeStruct((t_rows, NGRP * NSTATE), F32)],
        compiler_params=_cparams(("parallel",)))(
            z, p["are"], p["aim"], p["ldt"], p["bexp_re"], p["bexp_im"], p["cexp_re"], p["cexp_im"], p["dskip"])


def _ssm_bwd(dy, z, hre, him, p):
    t_rows = z.shape[0]
    s = _ssm_specs(t_rows)
    nstates = NGRP * NSTATE

    def body(dy_ref, u_ref, hr_ref, hi_ref, are_ref, aim_ref, ldt_ref, br_ref, bi_ref, cr_ref, ci_ref, d_ref,
             du_ref, dbr_ref, dbi_ref, dcr_ref, dci_ref, dd_ref, dar_ref, dai_ref, dldt_ref, lr_ref, li_ref):
        rows3 = (are_ref[...], aim_ref[...], ldt_ref[...])
        (abr, abi, fr, fi), disc_vjp = jax.vjp(_disc, *rows3)
        (bbr, bbi), bbar_vjp = jax.vjp(_bbar, fr, fi, br_ref[...], bi_ref[...])
        dy = dy_ref[...]
        u = u_ref[...]
        lr_ref[...] = _dot_nt(dy, cr_ref[...])
        li_ref[...] = -_dot_nt(dy, ci_ref[...])
        dcr_ref[...] = _dot_tn(hr_ref[...], dy)
        dci_ref[...] = -_dot_tn(hi_ref[...], dy)
        dd_ref[...] = jnp.sum(dy * u, axis=0, keepdims=True)
        acc_r, acc_i = _scan_rows(lr_ref, li_ref, abr, -abi, t_rows, True, hr_ref, hi_ref)
        dabr = jnp.sum(acc_r, axis=0, keepdims=True)
        dabi = jnp.sum(acc_i, axis=0, keepdims=True)
        lam_r = lr_ref[...]
        lam_i = li_ref[...]
        du = d_ref[...] * dy + _dot_nt(lam_r, bbr) + _dot_nt(lam_i, bbi)
        du_ref[...] = du.astype(BF16)
        dbbr = _dot_tn(u, lam_r)
        dbbi = _dot_tn(u, lam_i)
        dfr, dfi, dbr, dbi = bbar_vjp((dbbr, dbbi))
        dbr_ref[...] = dbr
        dbi_ref[...] = dbi
        dar, dai, dldt = disc_vjp((dabr, dabi, dfr, dfi))
        dar_ref[...] = dar
        dai_ref[...] = dai
        lane_grp = lax.broadcasted_iota(jnp.int32, (NS, 128), 0) // NSTATE
        col = lax.broadcasted_iota(jnp.int32, (NS, 128), 1)
        seg = jnp.where(lane_grp == col, 1.0, 0.0).astype(F32)
        dldt_ref[...] = jnp.dot(jnp.broadcast_to(dldt, (8, NS)), seg, preferred_element_type=F32,
                                precision=lax.Precision.HIGHEST)

    dyspec = pl.BlockSpec((t_rows, GB * SGRP), lambda j: (0, j))
    return _call(
        body, name="ssm_bwd", grid=(NBLK,),
        in_specs=[dyspec, s["u"], s["h"], s["h"], s["row"], s["row"], s["row"], s["bexp"], s["bexp"], s["cexp"],
                  s["cexp"], s["d"]],
        out_specs=[dyspec, s["bexp"], s["bexp"], s["cexp"], s["cexp"], s["d"], s["row"], s["row"],
                   pl.BlockSpec((8, 128), lambda j: (j, 0))],
        out_shape=[jax.ShapeDtypeStruct((t_rows, BW), BF16),
                   jax.ShapeDtypeStruct((NBLK, GB * SGRP, NS), F32), jax.ShapeDtypeStruct((NBLK, GB * SGRP, NS), F32),
                   jax.ShapeDtypeStruct((NBLK, NS, GB * SGRP), F32), jax.ShapeDtypeStruct((NBLK, NS, GB * SGRP), F32),
                   jax.ShapeDtypeStruct((1, BW), F32), jax.ShapeDtypeStruct((1, nstates), F32),
                   jax.ShapeDtypeStruct((1, nstates), F32), jax.ShapeDtypeStruct((NBLK * 8, 128), F32)],
        scratch_shapes=[pltpu.VMEM((t_rows, NS), F32), pltpu.VMEM((t_rows, NS), F32)],
        compiler_params=_cparams(("parallel",)))(
            dy, z, hre, him, p["are"], p["aim"], p["ldt"], p["bexp_re"], p["bexp_im"], p["cexp_re"], p["cexp_im"],
            p["dskip"])


def _conv_fwd(z, w, b, tm=256):
    t_rows = z.shape[0]
    hb = tm // HALO

    def body(va_ref, vb_ref, ha_ref, hb_ref, w_ref, b_ref, o_ref, win_ref):
        live = jnp.where(pl.program_id(0) > 0, 1.0, 0.0)
        win_ref[0:HALO, :] = ha_ref[...] * jax.nn.sigmoid(hb_ref[...]) * live
        win_ref[HALO:HALO + tm, :] = va_ref[...] * jax.nn.sigmoid(vb_ref[...])
        acc = jnp.broadcast_to(b_ref[...], (tm, BW))
        for k in range(CONV_K):
            acc = acc + w_ref[k:k + 1, :] * win_ref[pl.ds(HALO - (CONV_K - 1) + k, tm), :]
        o_ref[...] = acc

    halo = lambda col: pl.BlockSpec((HALO, BW), lambda i: (jnp.maximum(i * hb - 1, 0), col))
    return _call(
        body, name="conv_fwd", grid=(t_rows // tm,),
        in_specs=[pl.BlockSpec((tm, BW), lambda i: (i, 1)), pl.BlockSpec((tm, BW), lambda i: (i, 2)), halo(1), halo(2),
                  pl.BlockSpec((CONV_K, BW), lambda i: (0, 0)), pl.BlockSpec((1, BW), lambda i: (0, 0))],
        out_specs=pl.BlockSpec((tm, BW), lambda i: (i, 0)),
        out_shape=jax.ShapeDtypeStruct((t_rows, BW), F32),
        scratch_shapes=[pltpu.VMEM((HALO + tm, BW), F32)],
        compiler_params=_cparams(("parallel",)))(z, z, z, z, w, b)


def _conv_bwd(dcv, z, w, tm=256):
    t_rows = z.shape[0]
    nt = t_rows // tm
    hb = tm // HALO
    csh = BW // NDEV

    def body(d_ref, dn_ref, va_ref, vb_ref, ha_ref, hb_ref, w_ref, dva_ref, dvb_ref, dw8_ref, db_ref,
             hwin_ref, dwin_ref, dw_ref):
        i = pl.program_id(0)

        @pl.when(i == 0)
        def _():
            dw_ref[...] = jnp.zeros_like(dw_ref)
            db_ref[...] = jnp.zeros_like(db_ref)

        live_prev = jnp.where(i > 0, 1.0, 0.0)
        live_next = jnp.where(i < nt - 1, 1.0, 0.0)
        va = va_ref[...]
        sig = jax.nn.sigmoid(vb_ref[...])
        hwin_ref[0:HALO, :] = ha_ref[...] * jax.nn.sigmoid(hb_ref[...]) * live_prev
        hwin_ref[HALO:HALO + tm, :] = va * sig
        d = d_ref[...]
        dwin_ref[0:tm, :] = d
        dwin_ref[tm:tm + HALO, :] = dn_ref[...] * live_next
        dh = jnp.zeros((tm, BW), F32)
        dws = []
        for k in range(CONV_K):
            dh = dh + w_ref[k:k + 1, :] * dwin_ref[pl.ds(CONV_K - 1 - k, tm), :]
            dws.append(jnp.sum(d * hwin_ref[pl.ds(HALO - (CONV_K - 1) + k, tm), :], axis=0, keepdims=True))
        dws.append(jnp.zeros((1, BW), F32))
        dw_ref[...] += jnp.concatenate(dws, axis=0)
        db_ref[...] += jnp.sum(d, axis=0, keepdims=True)
        dva_ref[...] = (dh * sig).astype(BF16)
        dvb_ref[...] = (dh * va * sig * (1.0 - sig)).astype(BF16)

        @pl.when(i == nt - 1)
        def _():
            acc = dw_ref[...]
            for q in range(NDEV):
                dw8_ref[q] = acc[:, csh * q:csh * (q + 1)]

    halo = lambda col: pl.BlockSpec((HALO, BW), lambda i: (jnp.maximum(i * hb - 1, 0), col))
    return _call(
        body, name="conv_bwd", grid=(nt,),
        in_specs=[pl.BlockSpec((tm, BW), lambda i: (i, 0)),
                  pl.BlockSpec((HALO, BW), lambda i: (jnp.minimum((i + 1) * hb, t_rows // HALO - 1), 0)),
                  pl.BlockSpec((tm, BW), lambda i: (i, 1)), pl.BlockSpec((tm, BW), lambda i: (i, 2)), halo(1), halo(2),
                  pl.BlockSpec((CONV_K, BW), lambda i: (0, 0))],
        out_specs=[pl.BlockSpec((tm, BW), lambda i: (i, 0)), pl.BlockSpec((tm, BW), lambda i: (i, 0)),
                   pl.BlockSpec((NDEV, 32, csh), lambda i: (0, 0, 0)), pl.BlockSpec((1, BW), lambda i: (0, 0))],
        out_shape=[jax.ShapeDtypeStruct((t_rows, BW), BF16), jax.ShapeDtypeStruct((t_rows, BW), BF16),
                   jax.ShapeDtypeStruct((NDEV, 32, csh), F32), jax.ShapeDtypeStruct((1, BW), F32)],
        scratch_shapes=[pltpu.VMEM((HALO + tm, BW), F32), pltpu.VMEM((tm + HALO, BW), F32), pltpu.VMEM((32, BW), F32)],
        compiler_params=_cparams(("arbitrary",)))(dcv, dcv, z, z, z, z, w)


def _pool_rows(i, tm, n_rows, first_row):
    grp = lax.broadcasted_iota(jnp.int32, (1, BW), 1) // (BW // 4)
    wlen = jnp.where(grp == 0, 2.0, jnp.where(grp == 1, 4.0, jnp.where(grp == 2, 8.0, 16.0)))
    t = (i * tm + first_row + lax.broadcasted_iota(jnp.int32, (n_rows, 1), 0)).astype(F32)
    return grp, 1.0 / jnp.minimum(t + 1.0, wlen)


def _pool_pick(grp, s2, s4, s8, s16):
    return jnp.where(grp == 0, s2, jnp.where(grp == 1, s4, jnp.where(grp == 2, s8, s16)))


def _pool_fwd(z, tm=256):
    t_rows = z.shape[0]
    hb = tm // PHALO

    def body(u_ref, h_ref, o_ref):
        i = pl.program_id(0)
        u = u_ref[...]
        win = jnp.concatenate([h_ref[...] * jnp.where(i > 0, 1.0, 0.0), u], axis=0)
        s2 = win + pltpu.roll(win, 1, 0)
        s4 = s2 + pltpu.roll(s2, 2, 0)
        s8 = s4 + pltpu.roll(s4, 4, 0)
        s16 = s8 + pltpu.roll(s8, 8, 0)
        grp, inv = _pool_rows(i, tm, tm, 0)
        o_ref[...] = _pool_pick(grp, s2, s4, s8, s16)[PHALO:, :] * inv - u

    return _call(
        body, name="pool_fwd", grid=(t_rows // tm,),
        in_specs=[pl.BlockSpec((tm, BW), lambda i: (i, 3)),
                  pl.BlockSpec((PHALO, BW), lambda i: (jnp.maximum(i * hb - 1, 0), 3))],
        out_specs=pl.BlockSpec((tm, BW), lambda i: (i, 0)),
        out_shape=jax.ShapeDtypeStruct((t_rows, BW), F32),
        compiler_params=_cparams(("parallel",)))(z, z)


def _pool_bwd(dp, tm=256):
    t_rows = dp.shape[0]
    nt = t_rows // tm
    hb = tm // PHALO
    ln = tm + PHALO

    def body(d_ref, dn_ref, o_ref):
        i = pl.program_id(0)
        d = d_ref[...]
        grp, inv = _pool_rows(i, tm, ln, 0)
        win = jnp.concatenate([d, dn_ref[...] * jnp.where(i < nt - 1, 1.0, 0.0)], axis=0) * inv
        s2 = win + pltpu.roll(win, ln - 1, 0)
        s4 = s2 + pltpu.roll(s2, ln - 2, 0)
        s8 = s4 + pltpu.roll(s4, ln - 4, 0)
        s16 = s8 + pltpu.roll(s8, ln - 8, 0)
        o_ref[...] = (_pool_pick(grp, s2, s4, s8, s16)[:tm, :] - d).astype(BF16)

    return _call(
        body, name="pool_bwd", grid=(nt,),
        in_specs=[pl.BlockSpec((tm, BW), lambda i: (i, 0)),
                  pl.BlockSpec((PHALO, BW), lambda i: (jnp.minimum((i + 1) * hb, t_rows // PHALO - 1), 0))],
        out_specs=pl.BlockSpec((tm, BW), lambda i: (i, 0)),
        out_shape=jax.ShapeDtypeStruct((t_rows, BW), BF16),
        compiler_params=_cparams(("parallel",)))(dp, dp)


_MERGE_W = ("wglu", "bglu", "wpa", "lng", "lnb", "wpb", "wgrp", "scale", "wpc", "bgate", "wout")
_MERGE_SMALL = ("bglu", "lng", "lnb", "scale", "bgate")


def _merge_math(x, yssm, cv, pbar, zg, w, taps):
    t_glu, t_ya, t_yb, t_p, t_yc = taps
    g = jax.nn.gelu(yssm)
    outa = g * jax.nn.sigmoid(_mm(g, w["wglu"]) + t_glu + w["bglu"])
    ya = _mm(outa, w["wpa"]) + t_ya
    mu = jnp.mean(cv, axis=-1, keepdims=True)
    var = jnp.mean(jnp.square(cv - mu), axis=-1, keepdims=True)
    hs = jax.nn.silu((cv - mu) * lax.rsqrt(var + EPS) * w["lng"] + w["lnb"])
    yb = _mm(hs, w["wpb"]) + t_yb
    gw = BW // 4
    pk = jnp.concatenate([_mm(pbar[:, gw * k:gw * (k + 1)], w["wgrp"][k]) for k in range(4)], axis=1) + t_p
    pc = pk * w["scale"]
    yc = _mm(pc, w["wpc"]) + t_yc
    gates = jax.nn.sigmoid(zg + w["bgate"])
    merged = gates[:, :D] * ya + gates[:, D:2 * D] * yb + gates[:, 2 * D:] * yc
    x1 = x + _mm(merged, w["wout"])
    acts = tuple(a.astype(BF16) for a in (g, outa, hs, pbar, pc, merged))
    return x1, acts


def _merge_specs(tm, p):
    rows = lambda width, col=0: pl.BlockSpec((tm, width), lambda i, c=col: (i, c))
    data = [rows(D), rows(BW), rows(BW), rows(BW), rows(D, 2), rows(D, 3), rows(D, 4)]
    wspecs = []
    for name in _MERGE_W:
        nd = p[name].ndim
        wspecs.append(pl.BlockSpec(p[name].shape, lambda i, nd=nd: (0,) * nd))
    return rows, data, wspecs


def _merge_fwd(x, yssm, cv, pbar, z, p, tm=256):
    t_rows = x.shape[0]
    rows, data, wspecs = _merge_specs(tm, p)

    def body(x_ref, y_ref, cv_ref, pb_ref, za_ref, zb_ref, zc_ref, *rest):
        w = {name: r[...] for name, r in zip(_MERGE_W, rest[:len(_MERGE_W)])}
        o_ref = rest[len(_MERGE_W)]
        taps = (0.0, 0.0, 0.0, 0.0, 0.0)
        zg = jnp.concatenate([za_ref[...], zb_ref[...], zc_ref[...]], axis=1)
        o_ref[...] = _merge_math(x_ref[...], y_ref[...], cv_ref[...], pb_ref[...], zg, w, taps)[0]

    return _call(
        body, name="merge_fwd", grid=(t_rows // tm,), in_specs=data + wspecs, out_specs=rows(D),
        out_shape=jax.ShapeDtypeStruct((t_rows, D), F32),
        compiler_params=_cparams(("parallel",)))(x, yssm, cv, pbar, z, z, z, *[p[n] for n in _MERGE_W])


def _merge_bwd(dx1, x, yssm, cv, pbar, z, p, tm=256):
    t_rows = x.shape[0]
    rows, data, wspecs = _merge_specs(tm, p)
    nw = len(_MERGE_W)

    def body(dx_ref, x_ref, y_ref, cv_ref, pb_ref, za_ref, zb_ref, zc_ref, *rest):
        w = {name: r[...] for name, r in zip(_MERGE_W, rest[:nw])}
        zg = jnp.concatenate([za_ref[...], zb_ref[...], zc_ref[...]], axis=1)
        outs = rest[nw:]
        small = {n: w[n] for n in _MERGE_SMALL}
        taps = (jnp.zeros((tm, BW), F32), jnp.zeros((tm, D), F32), jnp.zeros((tm, D), F32),
                jnp.zeros((tm, BW), F32), jnp.zeros((tm, D), F32))

        def f(yssm_, cv_, pbar_, zg_, small_, taps_):
            return _merge_math(x_ref[...], yssm_, cv_, pbar_, zg_, {**w, **small_}, taps_)

        _, vjp, acts = jax.vjp(f, y_ref[...], cv_ref[...], pb_ref[...], zg, small, taps, has_aux=True)
        dy, dcv, dpb, dzg, dsmall, dtaps = vjp(dx_ref[...])
        outs[0][...] = dy
        outs[1][...] = dcv
        outs[2][...] = dpb
        outs[3][...] = dzg.astype(BF16)
        for k in range(6):
            outs[4 + k][...] = acts[k]
        for k in range(5):
            outs[10 + k][...] = dtaps[k].astype(BF16)

        @pl.when(pl.program_id(0) == 0)
        def _():
            for k in range(5):
                outs[15 + k][...] = jnp.zeros_like(outs[15 + k])

        for k, n in enumerate(_MERGE_SMALL):
            outs[15 + k][...] += dsmall[n]

    f32o = lambda width: jax.ShapeDtypeStruct((t_rows, width), F32)
    bfo = lambda width: jax.ShapeDtypeStruct((t_rows, width), BF16)
    small_shapes = [jax.ShapeDtypeStruct(p[n].shape, F32) for n in _MERGE_SMALL]
    small_specs = [pl.BlockSpec(p[n].shape, lambda i: (0, 0)) for n in _MERGE_SMALL]
    out_shape = ([f32o(BW), f32o(BW), f32o(BW), bfo(3 * D)]
                 + [bfo(BW), bfo(BW), bfo(BW), bfo(BW), bfo(BW), bfo(D)]
                 + [bfo(BW), bfo(D), bfo(D), bfo(BW), bfo(D)] + small_shapes)
    out_specs = ([rows(BW), rows(BW), rows(BW), rows(3 * D)]
                 + [rows(BW)] * 5 + [rows(D)]
                 + [rows(BW), rows(D), rows(D), rows(BW), rows(D)] + small_specs)
    return _call(
        body, name="merge_bwd", grid=(t_rows // tm,), in_specs=[rows(D)] + data + wspecs, out_specs=out_specs,
        out_shape=out_shape, compiler_params=_cparams(("arbitrary",)))(
            dx1, x, yssm, cv, pbar, z, z, z, *[p[n] for n in _MERGE_W])


def _ffn_fwd(x1, gamma, wg, wu, wd, tm=512, th=512):
    t_rows = x1.shape[0]
    nh = HIDP // th

    def body(x_ref, g_ref, wg_ref, wu_ref, wd_ref, o_ref, gp_ref, up_ref, h_ref, acc_ref):
        j = pl.program_id(1)

        @pl.when(j == 0)
        def _():
            h_ref[...] = _rms(x_ref[...], g_ref[...]).astype(BF16)
            acc_ref[...] = jnp.zeros_like(acc_ref)

        gp = jnp.dot(h_ref[...], wg_ref[...], preferred_element_type=F32)
        up = jnp.dot(h_ref[...], wu_ref[...], preferred_element_type=F32)
        gp_ref[...] = gp
        up_ref[...] = up
        acc_ref[...] += _dot(jax.nn.silu(gp) * up, wd_ref[...])

        @pl.when(j == nh - 1)
        def _():
            o_ref[...] = x_ref[...] + acc_ref[...]

    return _call(
        body, name="ffn_fwd", grid=(t_rows // tm, nh),
        in_specs=[pl.BlockSpec((tm, D), lambda i, j: (i, 0)), pl.BlockSpec((1, D), lambda i, j: (0, 0)),
                  pl.BlockSpec((D, th), lambda i, j: (0, j)), pl.BlockSpec((D, th), lambda i, j: (0, j)),
                  pl.BlockSpec((th, D), lambda i, j: (j, 0))],
        out_specs=[pl.BlockSpec((tm, D), lambda i, j: (i, 0)), pl.BlockSpec((tm, th), lambda i, j: (i, j)),
                   pl.BlockSpec((tm, th), lambda i, j: (i, j))],
        out_shape=[jax.ShapeDtypeStruct((t_rows, D), F32), jax.ShapeDtypeStruct((t_rows, HIDP), F32),
                   jax.ShapeDtypeStruct((t_rows, HIDP), F32)],
        scratch_shapes=[pltpu.VMEM((tm, D), BF16), pltpu.VMEM((tm, D), F32)],
        compiler_params=_cparams(("parallel", "arbitrary")))(x1, gamma, wg, wu, wd)


def _rms_bwd_tail(x, gamma, dh):
    _, vjp = jax.vjp(_rms, x, gamma)
    return vjp(dh)


def _ffn_bwd(dx2, x1, gamma, gpre, upre, wg, wu, wd, tm=512, th=512):
    t_rows = x1.shape[0]
    nh = HIDP // th

    def body(d_ref, x_ref, g_ref, gp_ref, up_ref, wg_ref, wu_ref, wd_ref,
             dx_ref, dgam_ref, dgp_ref, dup_ref, act_ref, h_ref, acc_ref):
        i = pl.program_id(0)
        j = pl.program_id(1)

        @pl.when(j == 0)
        def _():
            acc_ref[...] = jnp.zeros_like(acc_ref)

        @pl.when((i == 0) & (j == 0))
        def _():
            dgam_ref[...] = jnp.zeros_like(dgam_ref)

        dact = _dot_nt(d_ref[...], wd_ref[...])
        gp = gp_ref[...]
        up = up_ref[...]
        sg = jax.nn.sigmoid(gp)
        silu = gp * sg
        dgp = (dact * up * (sg * (1.0 + gp * (1.0 - sg)))).astype(BF16)
        dup = (dact * silu).astype(BF16)
        dgp_ref[...] = dgp
        dup_ref[...] = dup
        act_ref[...] = (silu * up).astype(BF16)
        acc_ref[...] += _dot_nt(dgp, wg_ref[...]) + _dot_nt(dup, wu_ref[...])

        @pl.when(j == nh - 1)
        def _():
            x = x_ref[...]
            h_ref[...] = _rms(x, g_ref[...]).astype(BF16)
            dx, dgam = _rms_bwd_tail(x, g_ref[...], acc_ref[...])
            dx_ref[...] = d_ref[...] + dx
            dgam_ref[...] += dgam

    row_d = pl.BlockSpec((tm, D), lambda i, j: (i, 0))
    row_h = pl.BlockSpec((tm, th), lambda i, j: (i, j))
    return _call(
        body, name="ffn_bwd", grid=(t_rows // tm, nh),
        in_specs=[row_d, row_d, pl.BlockSpec((1, D), lambda i, j: (0, 0)), row_h, row_h,
                  pl.BlockSpec((D, th), lambda i, j: (0, j)), pl.BlockSpec((D, th), lambda i, j: (0, j)),
                  pl.BlockSpec((th, D), lambda i, j: (j, 0))],
        out_specs=[row_d, pl.BlockSpec((1, D), lambda i, j: (0, 0)), row_h, row_h, row_h, row_d],
        out_shape=[jax.ShapeDtypeStruct((t_rows, D), F32), jax.ShapeDtypeStruct((1, D), F32),
                   jax.ShapeDtypeStruct((t_rows, HIDP), BF16), jax.ShapeDtypeStruct((t_rows, HIDP), BF16),
                   jax.ShapeDtypeStruct((t_rows, HIDP), BF16), jax.ShapeDtypeStruct((t_rows, D), BF16)],
        scratch_shapes=[pltpu.VMEM((tm, D), F32)],
        compiler_params=_cparams(("arbitrary", "arbitrary")))(dx2, x1, gamma, gpre, upre, wg, wu, wd)


def _inproj_bwd(dz, dx1, x, gamma, w, tm=512, tn=1280):
    t_rows = x.shape[0]
    nn = IN_W // tn

    def body(dz_ref, d1_ref, x_ref, g_ref, w_ref, dx_ref, dgam_ref, h_ref, acc_ref):
        i = pl.program_id(0)
        j = pl.program_id(1)

        @pl.when(j == 0)
        def _():
            acc_ref[...] = jnp.zeros_like(acc_ref)

        @pl.when((i == 0) & (j == 0))
        def _():
            dgam_ref[...] = jnp.zeros_like(dgam_ref)

        acc_ref[...] += _dot_nt(dz_ref[...], w_ref[...])

        @pl.when(j == nn - 1)
        def _():
            x = x_ref[...]
            h_ref[...] = _rms(x, g_ref[...]).astype(BF16)
            dx, dgam = _rms_bwd_tail(x, g_ref[...], acc_ref[...])
            dx_ref[...] = d1_ref[...] + dx
            dgam_ref[...] += dgam

    row_d = pl.BlockSpec((tm, D), lambda i, j: (i, 0))
    return _call(
        body, name="inproj_bwd", grid=(t_rows // tm, nn),
        in_specs=[pl.BlockSpec((tm, tn), lambda i, j: (i, j)), row_d, row_d, pl.BlockSpec((1, D), lambda i, j: (0, 0)),
                  pl.BlockSpec((D, tn), lambda i, j: (0, j))],
        out_specs=[row_d, pl.BlockSpec((1, D), lambda i, j: (0, 0)), row_d],
        out_shape=[jax.ShapeDtypeStruct((t_rows, D), F32), jax.ShapeDtypeStruct((1, D), F32),
                   jax.ShapeDtypeStruct((t_rows, D), BF16)],
        scratch_shapes=[pltpu.VMEM((tm, D), F32)],
        compiler_params=_cparams(("arbitrary", "arbitrary")))(dz, dx1, x, gamma, w)


def _matmul_tn(a, b, name, owner_cols=None, tt=512):
    t_rows, k = a.shape
    n = b.shape[1]
    tk = min(k, 1024)
    nt = t_rows // tt
    if owner_cols is None:
        tn, nb = min(n, 512), None
        out_spec = pl.BlockSpec((tk, tn), lambda i, j, t: (i, j))
        out_shape = jax.ShapeDtypeStruct((k, n), BF16)
    else:
        nb = max(1, 512 // owner_cols)
        tn = nb * owner_cols
        out_spec = pl.BlockSpec((nb, tk, owner_cols), lambda i, j, t: (j, i, 0))
        out_shape = jax.ShapeDtypeStruct((n // owner_cols, k, owner_cols), BF16)

    def body(a_ref, b_ref, o_ref, acc_ref):
        t = pl.program_id(2)

        @pl.when(t == 0)
        def _():
            acc_ref[...] = jnp.zeros_like(acc_ref)

        acc_ref[...] += _dot_tn(a_ref[...], b_ref[...])

        @pl.when(t == nt - 1)
        def _():
            if nb is None:
                o_ref[...] = acc_ref[...].astype(BF16)
            else:
                for q in range(nb):
                    o_ref[q] = acc_ref[:, owner_cols * q:owner_cols * (q + 1)].astype(BF16)

    return _call(
        body, name=name, grid=(k // tk, n // tn, nt),
        in_specs=[pl.BlockSpec((tt, tk), lambda i, j, t: (t, i)), pl.BlockSpec((tt, tn), lambda i, j, t: (t, j))],
        out_specs=out_spec, out_shape=out_shape,
        scratch_shapes=[pltpu.VMEM((tk, tn), F32)],
        compiler_params=_cparams(("parallel", "parallel", "arbitrary")))(a, b)


def _group_tn(a, b):
    t_rows = a.shape[0]
    gw = BW // 4

    def body(a_ref, b_ref, o_ref):
        o_ref[...] = _dot_tn(a_ref[...], b_ref[...])

    return _call(
        body, name="pool_group_tn", grid=(4,),
        in_specs=[pl.BlockSpec((t_rows, gw), lambda k: (0, k)), pl.BlockSpec((t_rows, gw), lambda k: (0, k))],
        out_specs=pl.BlockSpec((None, gw, gw), lambda k: (k, 0, 0)),
        out_shape=jax.ShapeDtypeStruct((4, gw, gw), F32),
        compiler_params=_cparams(("parallel",)))(a, b)


def _loss_head(x2, gamma, target, tm=512):
    t_rows = x2.shape[0]

    def body(x_ref, g_ref, t_ref, loss_ref, dx_ref, dgam_ref):
        @pl.when(pl.program_id(0) == 0)
        def _():
            loss_ref[...] = jnp.zeros_like(loss_ref)
            dgam_ref[...] = jnp.zeros_like(dgam_ref)

        def f(x, g):
            err = jnp.square(_rms(x, g) - t_ref[...])
            return 0.5 * jnp.sum(jnp.mean(err, axis=-1, keepdims=True), axis=0, keepdims=True)

        loss, vjp = jax.vjp(f, x_ref[...], g_ref[...])
        dx, dgam = vjp(jnp.ones((1, 1), F32))
        loss_ref[...] += jnp.broadcast_to(loss, (1, 128))
        dx_ref[...] = dx
        dgam_ref[...] += dgam

    row_d = pl.BlockSpec((tm, D), lambda i: (i, 0))
    return _call(
        body, name="loss_head", grid=(t_rows // tm,),
        in_specs=[row_d, pl.BlockSpec((1, D), lambda i: (0, 0)), row_d],
        out_specs=[pl.BlockSpec((1, 128), lambda i: (0, 0)), row_d, pl.BlockSpec((1, D), lambda i: (0, 0))],
        out_shape=[jax.ShapeDtypeStruct((1, 128), F32), jax.ShapeDtypeStruct((t_rows, D), F32),
                   jax.ShapeDtypeStruct((1, D), F32)],
        compiler_params=_cparams(("arbitrary",)))(x2, gamma, target)


NCHIP = NDEV // 2


def _coords():
    return lax.axis_index("x"), lax.axis_index("y"), lax.axis_index("c")


def _remote(src, dst, send_sem, recv_sem, peer):
    return pltpu.make_async_remote_copy(src_ref=src, dst_ref=dst, send_sem=send_sem, recv_sem=recv_sem,
                                        device_id=peer, device_id_type=MESH)


def _comm_call(name, srcs, out_shapes, n_rec, plan):
    ns, no = len(srcs), len(out_shapes)

    def body(*refs):
        ins, outs = refs[:ns], refs[ns:ns + no]
        loc_sem, send_sem, recv_sem = refs[ns + no:]
        x, y, c = _coords()
        recs = plan(ins, outs, x, y, c)
        assert len(recs) == n_rec
        for k, r in enumerate(recs):
            for src, dst in r.get("local", ()):
                pltpu.make_async_copy(src, dst, loc_sem.at[k]).start()
            for peer, src, dst in r.get("remote", ()):
                _remote(src, dst, send_sem.at[k], recv_sem.at[k], peer).start()
        for k, r in enumerate(recs):
            if r.get("recv_wait") is not None:
                w = r["recv_wait"]
                _remote(w, w, send_sem.at[k], recv_sem.at[k], (x, y, c)).wait_recv()
            if r.get("send_wait") is not None:
                w = r["send_wait"]
                _remote(w, w, send_sem.at[k], recv_sem.at[k], (x, y, c)).wait_send()
            if r.get("local_wait") is not None:
                w = r["local_wait"]
                pltpu.make_async_copy(w, w, loc_sem.at[k]).wait()

    return _call(
        body, name=name, in_specs=[ANY] * ns, out_specs=[ANY] * no, out_shape=out_shapes,
        scratch_shapes=[pltpu.SemaphoreType.DMA((n_rec,))] * 3)(*srcs)


def _gather_call(srcs, out_shapes, items):
    ns, no, n = len(srcs), len(out_shapes), len(items)

    def body(*refs):
        ins, outs = refs[:ns], refs[ns:ns + no]
        loc, sib_s, sib_r, ici_s, ici_r, fwd_s, fwd_r = refs[ns + no:]
        x, y, c = _coords()
        me, sib = (x, y, c), (x, y, 1 - c)
        chips = [(1 - x, y), (x, 1 - y), (1 - x, 1 - y)]
        index = lambda px, py, pc: 4 * px + 2 * py + pc
        for k, (si, oi, shard, block, _) in enumerate(items):
            src, mine = shard(ins[si]), block(outs[oi], index(*me))
            pltpu.make_async_copy(src, mine, loc.at[k]).start()
            _remote(src, mine, sib_s.at[k], sib_r.at[k], sib).start()
            for chip in chips:
                _remote(src, mine, ici_s.at[k], ici_r.at[k], (*chip, c)).start()
        for k, (si, oi, _, block, blocks) in enumerate(items):
            three = blocks(outs[oi], 3)
            _remote(three, three, ici_s.at[k], ici_r.at[k], me).wait_recv()
            for chip in chips:
                landed = block(outs[oi], index(*chip, c))
                _remote(landed, landed, fwd_s.at[k], fwd_r.at[k], sib).start()
        for k, (si, oi, _, _, blocks) in enumerate(items):
            one, three = blocks(outs[oi], 1), blocks(outs[oi], 3)
            _remote(one, one, sib_s.at[k], sib_r.at[k], me).wait()
            _remote(three, three, fwd_s.at[k], fwd_r.at[k], me).wait()
            _remote(three, three, ici_s.at[k], ici_r.at[k], me).wait_send()
            pltpu.make_async_copy(one, one, loc.at[k]).wait()

    return _call(
        body, name="gather_weights", in_specs=[ANY] * ns, out_specs=[ANY] * no, out_shape=out_shapes,
        scratch_shapes=[pltpu.SemaphoreType.DMA((n,))] * 7)(*srcs)


_BIG = {
    "w_in": (1, D, IN_W // NDEV, D, IN_W),
    "ssm_w_glu": (0, BW // NDEV, BW, BW, BW),
    "ssm_w_proj": (1, BW, D // NDEV, BW, D),
    "conv_w_proj": (1, BW, D // NDEV, BW, D),
    "pool_w_proj": (1, BW, D // NDEV, BW, D),
    "w_out": (0, D // NDEV, D, D, D),
    "ffn_w_gate": (1, D, HPAD, D, HIDP),
    "ffn_w_up": (1, D, HPAD, D, HIDP),
    "ffn_w_down": (0, HPAD, D, HIDP, D),
}


def _block_view(axis, size):
    if axis == 1:
        return lambda ref, q: ref.at[:, pl.ds(pl.multiple_of(q * size, 128), size)]
    return lambda ref, q: ref.at[pl.ds(pl.multiple_of(q * size, 16), size), :]


def _blocks_view(axis, size):
    if axis == 1:
        return lambda ref, n: ref.at[:, pl.ds(0, n * size)]
    return lambda ref, n: ref.at[pl.ds(0, n * size), :]


def _gather_weights(shards, conv_dw):
    srcs, outs, items, where = [], [], [], {}
    for name, (axis, kk, nn, kf, nf) in _BIG.items():
        srcs.append(shards[name])
        size = nn if axis == 1 else kk
        for l in range(DEPTH):
            where[(name, l)] = len(outs)
            outs.append(jax.ShapeDtypeStruct((kf, nf), BF16))
            items.append((len(srcs) - 1, len(outs) - 1, lambda ref, l=l: ref.at[l], _block_view(axis, size),
                          _blocks_view(axis, size)))
    srcs.append(conv_dw)
    outs.append(jax.ShapeDtypeStruct((NDEV,) + conv_dw.shape, conv_dw.dtype))
    items.append((len(srcs) - 1, len(outs) - 1, lambda ref: ref, lambda ref, q: ref.at[q],
                  lambda ref, n: ref.at[pl.ds(0, n)]))
    res = _gather_call(srcs, outs, items)
    full = {name: [res[where[(name, l)]] for l in range(DEPTH)] for name in _BIG}
    return full, res[-1]


def _pair_add(name, grads, rcv, core):
    nl = len(grads)
    _, kk, nn = grads[0].shape

    def body(c_ref, *refs):
        l = pl.program_id(0)
        own = refs[0][...]
        for j in range(1, nl):
            own = jnp.where(l == j, refs[j][...], own)
        refs[nl + 1][...] = (own.astype(F32) + refs[nl][...].astype(F32)).astype(BF16)

    gspec = lambda j: pl.BlockSpec((None, kk, nn), lambda l, h, c_ref: (jnp.where(l == j, 2 * h + c_ref[0], 0), 0, 0))
    rspec = pl.BlockSpec((None, None, kk, nn), lambda l, h, c_ref: (h, l, 0, 0))
    return _call(
        body, name="pair_add_" + name,
        grid_spec=pltpu.PrefetchScalarGridSpec(num_scalar_prefetch=1, grid=(nl, NCHIP),
                                               in_specs=[gspec(j) for j in range(nl)] + [rspec], out_specs=rspec),
        out_shape=jax.ShapeDtypeStruct(rcv.shape, BF16),
        compiler_params=_cparams(("arbitrary", "arbitrary")))(core, *grads, rcv)


def _pair_add_small(owned, lists, core):
    on, ln = list(owned), list(lists)
    flat = []
    for n in on:
        flat += list(owned[n][0]) + [owned[n][1]]
    for n in ln:
        flat += list(lists[n][0]) + [lists[n][1]]

    def body(c_ref, *refs):
        outs = refs[len(flat):]
        c = c_ref[0]
        pos = 0
        for k, n in enumerate(on):
            nl = len(owned[n][0])
            for h in range(NCHIP):
                for l in range(nl):
                    outs[k][h, l] = refs[pos + l][pl.ds(2 * h + c, 1)][0] + refs[pos + nl][h, l]
            pos += nl + 1
        for k, n in enumerate(ln):
            nl = len(lists[n][0])
            for l in range(nl):
                outs[len(on) + k][l] = refs[pos + l][...] + refs[pos + nl][l]
            pos += nl + 1

    shapes = [jax.ShapeDtypeStruct(owned[n][1].shape, F32) for n in on]
    shapes += [jax.ShapeDtypeStruct(lists[n][1].shape, F32) for n in ln]
    res = _call(body, name="pair_add_small", out_shape=shapes,
                in_specs=[pl.BlockSpec(memory_space=pltpu.SMEM)] + [pl.BlockSpec(memory_space=pltpu.VMEM)] * len(flat),
                compiler_params=_cparams())(core, *flat)
    return dict(zip(on + ln, res))


def _reduce_grads(big, by_owner, small):
    rs = {**big, **by_owner}
    srcs, outs, plans, rcv_at = [], [], [], {}
    for name, arrays in rs.items():
        rcv_at[name] = len(outs)
        outs.append(jax.ShapeDtypeStruct((NCHIP, len(arrays)) + arrays[0].shape[1:], arrays[0].dtype))
        for l, arr in enumerate(arrays):
            srcs.append(arr)
            plans.append((len(srcs) - 1, rcv_at[name], l, True))
    for name, arrays in small.items():
        rcv_at[name] = len(outs)
        outs.append(jax.ShapeDtypeStruct((len(arrays),) + arrays[0].shape, F32))
        for l, arr in enumerate(arrays):
            srcs.append(arr)
            plans.append((len(srcs) - 1, rcv_at[name], l, False))

    def plan_pair(ins, out_refs, x, y, c):
        sib = (x, y, 1 - c)
        recs = []
        for si, ro, l, slabs in plans:
            if slabs:
                four = out_refs[ro].at[pl.ds(0, NCHIP), l]
                recs.append(dict(remote=[(sib, ins[si].at[2 * h + 1 - c], out_refs[ro].at[h, l]) for h in range(NCHIP)],
                                 send_wait=four, recv_wait=four))
            else:
                dst = out_refs[ro].at[l]
                recs.append(dict(remote=[(sib, ins[si], dst)], send_wait=dst, recv_wait=dst))
        return recs

    res = _comm_call("pair_exchange", srcs, outs, len(plans), plan_pair)
    core = lax.axis_index("c").astype(jnp.int32).reshape(1)
    part = {name: _pair_add(name, big[name], res[rcv_at[name]], core) for name in big}
    part.update(_pair_add_small({n: (by_owner[n], res[rcv_at[n]]) for n in by_owner},
                                {n: (small[n], res[rcv_at[n]]) for n in small}, core))

    names = list(rs) + list(small)
    srcs2 = [part[n] for n in names]
    outs2 = [jax.ShapeDtypeStruct(((() if n in rs else (NCHIP,)) + part[n].shape), part[n].dtype) for n in names]

    def plan_chip(ins, out_refs, x, y, c):
        mine = 2 * x + y
        recs = []
        for k, n in enumerate(names):
            pick = (lambda h, k=k: ins[k].at[h]) if n in rs else (lambda h, k=k: ins[k])
            remote = []
            for step in range(1, NCHIP):
                h = (mine + step) % NCHIP
                remote.append(((h // 2, h % 2, c), pick(h), out_refs[k].at[mine]))
            three = out_refs[k].at[pl.ds(0, NCHIP - 1)]
            recs.append(dict(local=[(pick(mine), out_refs[k].at[mine])], remote=remote,
                             local_wait=out_refs[k].at[0], send_wait=three, recv_wait=three))
        return recs

    res2 = _comm_call("chip_exchange", srcs2, outs2, len(names), plan_chip)
    return dict(zip(names, res2))


def _adamw(w, g, m, v):
    m = ADAM_B1 * m + (1.0 - ADAM_B1) * g
    v = ADAM_B2 * v + (1.0 - ADAM_B2) * jnp.square(g)
    m_hat = m / (1.0 - ADAM_B1 ** ADAM_STEP)
    v_hat = v / (1.0 - ADAM_B2 ** ADAM_STEP)
    delta = -ADAM_LR * (m_hat / (jnp.sqrt(v_hat) + ADAM_EPS) + ADAM_WD * w)
    return delta, m, v


def _sum_senders(ref):
    g = ref[0].astype(F32)
    for h in range(1, NCHIP):
        g = g + ref[h].astype(F32)
    return g


def _adam_big(name, recv, w, m, v, tk):
    kk, nn = w.shape[1], w.shape[2]
    nnp = recv.shape[3]

    def body(r_ref, w_ref, m_ref, v_ref, g_ref, d_ref, mo_ref, vo_ref):
        g = _sum_senders(r_ref)[:, :nn]
        delta, m2, v2 = _adamw(w_ref[...], g, m_ref[...], v_ref[...])
        g_ref[...] = g
        d_ref[...] = delta
        mo_ref[...] = m2
        vo_ref[...] = v2

    wspec = pl.BlockSpec((None, tk, nn), lambda l, i: (l, i, 0))
    shape = jax.ShapeDtypeStruct(w.shape, F32)
    return _call(
        body, name="adamw_" + name, grid=(DEPTH, kk // tk),
        in_specs=[pl.BlockSpec((NCHIP, None, tk, nnp), lambda l, i: (0, l, i, 0)), wspec, wspec, wspec],
        out_specs=[wspec] * 4, out_shape=[shape] * 4,
        compiler_params=_cparams(("parallel", "parallel")))(recv, w, m, v)


def _adam_small(names, recv, w, m, v):
    n = len(names)

    def body(*refs):
        r, ww, mm, vv = refs[:n], refs[n:2 * n], refs[2 * n:3 * n], refs[3 * n:4 * n]
        outs = refs[4 * n:]
        for k in range(n):
            g = _sum_senders(r[k])
            if g.shape != ww[k].shape:
                g = g[:, :ww[k].shape[1]]
            delta, m2, v2 = _adamw(ww[k][...], g, mm[k][...], vv[k][...])
            outs[k][...] = g
            outs[n + k][...] = delta
            outs[2 * n + k][...] = m2
            outs[3 * n + k][...] = v2

    shapes = [jax.ShapeDtypeStruct(w[k].shape, F32) for k in names]
    res = _call(body, name="adamw_small", out_shape=shapes * 4, compiler_params=_cparams())(
        *[recv[k] for k in names], *[w[k] for k in names], *[m[k] for k in names], *[v[k] for k in names])
    return {k: (res[i], res[n + i], res[2 * n + i], res[3 * n + i]) for i, k in enumerate(names)}


def _expand_b(b):
    bt = jnp.transpose(b, (0, 2, 1)).reshape(NBLK, GB, SGRP, NSTATE)
    eye = jnp.eye(GB, dtype=b.dtype)
    return jnp.einsum("jgpn,gh->jgphn", bt, eye).reshape(NBLK, GB * SGRP, NS)


def _extract_b(db):
    x = db.reshape(NBLK, GB, SGRP, GB, NSTATE)
    eye = jnp.eye(GB, dtype=db.dtype)
    d = jnp.einsum("jgphn,gh->jgpn", x, eye).reshape(NGRP, SGRP, NSTATE)
    return jnp.transpose(d, (0, 2, 1)).reshape(NGRP, NSTATE * SGRP)


def _expand_c(c):
    ct = jnp.transpose(c, (0, 2, 1)).reshape(NBLK, GB, NSTATE, SGRP)
    eye = jnp.eye(GB, dtype=c.dtype)
    return jnp.einsum("jgnp,gh->jgnhp", ct, eye).reshape(NBLK, NS, GB * SGRP)


def _extract_c(dc):
    x = dc.reshape(NBLK, GB, NSTATE, GB, SGRP)
    eye = jnp.eye(GB, dtype=dc.dtype)
    d = jnp.einsum("jgnhp,gh->jgnp", x, eye).reshape(NGRP, NSTATE, SGRP)
    return jnp.transpose(d, (0, 2, 1))


_SMALL = ("norm1", "b_gate", "ssm_a_re", "ssm_a_im", "ssm_log_dt", "ssm_b_re", "ssm_b_im", "ssm_c_re", "ssm_c_im",
          "ssm_d", "ssm_b_glu", "conv_b_dw", "conv_ln_g", "conv_ln_b", "pool_w_group", "pool_scale", "norm2")
_ADAM_TK = {"w_in": 256, "ssm_w_glu": 64, "ssm_w_proj": 512, "conv_w_proj": 512, "pool_w_proj": 512, "w_out": 128,
            "ffn_w_gate": 256, "ffn_w_up": 256, "ffn_w_down": HSH}
_OUT_ORDER = ("norm1", "w_in", "b_gate", "ssm_a_re", "ssm_a_im", "ssm_log_dt", "ssm_b_re", "ssm_b_im", "ssm_c_re",
              "ssm_c_im", "ssm_d", "ssm_w_glu", "ssm_b_glu", "ssm_w_proj", "conv_w_dw", "conv_b_dw", "conv_ln_g",
              "conv_ln_b", "conv_w_proj", "pool_w_group", "pool_scale", "pool_w_proj", "w_out", "norm2", "ffn_w_gate",
              "ffn_w_up", "ffn_w_down", "final_norm")


def _layer_fwd(x, p):
    z = _inproj_fwd(x, p["norm1"], p["w_in"])
    yssm, hre, him = _ssm_fwd(z, p)
    cv = _conv_fwd(z, p["conv_w"], p["conv_b"])
    pbar = _pool_fwd(z)
    x1 = _merge_fwd(x, yssm, cv, pbar, z, p)
    x2, gpre, upre = _ffn_fwd(x1, p["norm2"], p["wg"], p["wu"], p["wd"])
    return x2, dict(x=x, z=z, yssm=yssm, hre=hre, him=him, cv=cv, pbar=pbar, x1=x1, gpre=gpre, upre=upre)


def _layer_bwd(dx, p, s):
    big, small = {}, {}
    dx1, d_norm2, dgp, dup, act, h2 = _ffn_bwd(dx, s["x1"], p["norm2"], s["gpre"], s["upre"], p["wg"], p["wu"], p["wd"])
    big["ffn_w_gate"] = _matmul_tn(h2, dgp, "tn_gate", HPAD)
    big["ffn_w_up"] = _matmul_tn(h2, dup, "tn_up", HPAD)
    big["ffn_w_down"] = _matmul_tn(act, dx, "tn_down").reshape(NDEV, HPAD, D)
    (dy, dcv, dpb, dzg, a_g, a_outa, a_hs, a_pb, a_pc, a_mg, c_glu, c_ya, c_yb, c_p, c_yc,
     d_bglu, d_lng, d_lnb, d_scale, d_bgate) = _merge_bwd(dx1, s["x"], s["yssm"], s["cv"], s["pbar"], s["z"], p)
    big["ssm_w_glu"] = _matmul_tn(a_g, c_glu, "tn_glu").reshape(NDEV, BW // NDEV, BW)
    big["ssm_w_proj"] = _matmul_tn(a_outa, c_ya, "tn_ssm_proj", D // NDEV)
    big["conv_w_proj"] = _matmul_tn(a_hs, c_yb, "tn_conv_proj", D // NDEV)
    big["pool_w_proj"] = _matmul_tn(a_pc, c_yc, "tn_pool_proj", D // NDEV)
    big["w_out"] = _matmul_tn(a_mg, dx1, "tn_out").reshape(NDEV, D // NDEV, D)
    d_wgrp = _group_tn(a_pb, c_p)
    du_a, dbr, dbi, dcr, dci, dd, dar, dai, dldt = _ssm_bwd(dy, s["z"], s["hre"], s["him"], p)
    dva, dvb, dw8, dcb = _conv_bwd(dcv, s["z"], p["conv_w"])
    du_c = _pool_bwd(dpb)
    dz = jnp.concatenate([du_a, dva, dvb, du_c, dzg], axis=1)
    dx0, d_norm1, h = _inproj_bwd(dz, dx1, s["x"], p["norm1"], p["w_in"])
    big["w_in"] = _matmul_tn(h, dz, "tn_in", IN_W // NDEV)
    small["norm1"] = d_norm1
    small["b_gate"] = d_bgate
    small["ssm_a_re"] = dar.reshape(NGRP, NSTATE)
    small["ssm_a_im"] = dai.reshape(NGRP, NSTATE)
    small["ssm_log_dt"] = dldt.reshape(NBLK, 8, 128)[:, 0, :GB].reshape(1, NGRP)
    small["ssm_b_re"] = _extract_b(dbr)
    small["ssm_b_im"] = _extract_b(dbi)
    small["ssm_c_re"] = _extract_c(dcr)
    small["ssm_c_im"] = _extract_c(dci)
    small["ssm_d"] = dd.reshape(NGRP, SGRP)
    small["ssm_b_glu"] = d_bglu
    small["conv_b_dw"] = dcb
    small["conv_ln_g"] = d_lng
    small["conv_ln_b"] = d_lnb
    small["pool_w_group"] = d_wgrp
    small["pool_scale"] = d_scale
    small["norm2"] = d_norm2
    return dx0, big, dw8, small


def _train_step(a):
    t_rows = a["x"].shape[1]
    x0 = a["x"].reshape(t_rows, D)
    target = a["loss_target"].reshape(t_rows, D)

    pad_cols = lambda w: jnp.pad(w, ((0, 0), (0, 0), (0, HPAD - HSH)))
    shards = {
        "w_in": a["w_in"], "ssm_w_glu": a["ssm_w_glu"], "ssm_w_proj": a["ssm_w_proj"],
        "conv_w_proj": a["conv_w_proj"], "pool_w_proj": a["pool_w_proj"], "w_out": a["w_out"],
        "ffn_w_gate": pad_cols(a["ffn_w_gate"]), "ffn_w_up": pad_cols(a["ffn_w_up"]),
        "ffn_w_down": jnp.pad(a["ffn_w_down"], ((0, 0), (0, HPAD - HSH), (0, 0))),
    }
    shards = {k: v.astype(BF16) for k, v in shards.items()}
    full, dw_all = _gather_weights(shards, a["conv_w_dw"].reshape(DEPTH, CONV_K, BW // NDEV))
    conv_w = jnp.transpose(dw_all, (1, 2, 0, 3)).reshape(DEPTH, CONV_K, BW)

    def layer_params(l):
        row = lambda v: v.reshape(1, -1)
        return dict(
            norm1=row(a["norm1"][l]), w_in=full["w_in"][l],
            are=row(a["ssm_a_re"][l]), aim=row(a["ssm_a_im"][l]),
            ldt=row(jnp.repeat(a["ssm_log_dt"][l], NSTATE)),
            bexp_re=_expand_b(a["ssm_b_re"][l]), bexp_im=_expand_b(a["ssm_b_im"][l]),
            cexp_re=_expand_c(a["ssm_c_re"][l]), cexp_im=_expand_c(a["ssm_c_im"][l]),
            dskip=row(a["ssm_d"][l]),
            conv_w=conv_w[l], conv_b=row(a["conv_b_dw"][l]),
            wglu=full["ssm_w_glu"][l], bglu=row(a["ssm_b_glu"][l]), wpa=full["ssm_w_proj"][l],
            lng=row(a["conv_ln_g"][l]), lnb=row(a["conv_ln_b"][l]), wpb=full["conv_w_proj"][l],
            wgrp=a["pool_w_group"][l].astype(BF16), scale=row(a["pool_scale"][l]), wpc=full["pool_w_proj"][l],
            bgate=row(a["b_gate"][l]), wout=full["w_out"][l],
            norm2=row(a["norm2"][l]), wg=full["ffn_w_gate"][l], wu=full["ffn_w_up"][l], wd=full["ffn_w_down"][l],
        )

    params = [layer_params(l) for l in range(DEPTH)]

    saved = []
    x = x0
    for l in range(DEPTH):
        x, s = _layer_fwd(x, params[l])
        saved.append(s)

    loss_part, dx, d_final = _loss_head(x, a["final_norm"].reshape(1, D), target)
    loss = lax.psum(loss_part[0, 0], ("x", "y", "c"))

    big = {name: [None] * DEPTH for name in _BIG}
    by_owner = {"conv_w_dw": [None] * DEPTH}
    small = {name: [None] * DEPTH for name in _SMALL}
    for l in reversed(range(DEPTH)):
        dx, gb, go, gs = _layer_bwd(dx, params[l], saved[l])
        for name, g in gb.items():
            big[name][l] = g
        by_owner["conv_w_dw"][l] = go
        for name, g in gs.items():
            small[name][l] = g
    small["final_norm"] = [d_final]
    grad_x = dx.reshape(1, t_rows, D)

    recv = _reduce_grads(big, by_owner, small)
    results = {}
    for name in _BIG:
        results[name] = _adam_big(name, recv[name], a[name], a["m_" + name], a["v_" + name], _ADAM_TK[name])

    lay = {
        "norm1": lambda v: v.reshape(DEPTH, 1, D), "b_gate": lambda v: v.reshape(DEPTH, 1, 3 * D),
        "ssm_log_dt": lambda v: v.reshape(DEPTH, 1, NGRP),
        "ssm_b_re": lambda v: v.reshape(DEPTH, NGRP, NSTATE * SGRP), "ssm_b_im": lambda v: v.reshape(DEPTH, NGRP, NSTATE * SGRP),
        "ssm_b_glu": lambda v: v.reshape(DEPTH, 1, BW), "conv_b_dw": lambda v: v.reshape(DEPTH, 1, BW),
        "conv_ln_g": lambda v: v.reshape(DEPTH, 1, BW), "conv_ln_b": lambda v: v.reshape(DEPTH, 1, BW),
        "pool_scale": lambda v: v.reshape(DEPTH, 1, BW), "norm2": lambda v: v.reshape(DEPTH, 1, D),
        "conv_w_dw": lambda v: v.reshape(DEPTH, CONV_K, BW // NDEV), "final_norm": lambda v: v.reshape(1, 1, D),
    }
    names = _SMALL + ("conv_w_dw", "final_norm")
    relay = lambda k, v: lay[k](v) if k in lay else v
    sm = _adam_small(names, recv, {k: relay(k, a[k]) for k in names}, {k: relay(k, a["m_" + k]) for k in names},
                     {k: relay(k, a["v_" + k]) for k in names})
    for k in names:
        results[k] = tuple(r.reshape(a[k].shape) for r in sm[k])

    outs = [loss, grad_x]
    for part in range(4):
        outs += [results[k][part] for k in _OUT_ORDER]
    return tuple(outs)


def kernel(x, norm1, w_in, b_gate, ssm_a_re, ssm_a_im, ssm_log_dt, ssm_b_re, ssm_b_im, ssm_c_re, ssm_c_im, ssm_d, ssm_w_glu, ssm_b_glu, ssm_w_proj, conv_w_dw, conv_b_dw, conv_ln_g, conv_ln_b, conv_w_proj, pool_w_group, pool_scale, pool_w_proj, w_out, norm2, ffn_w_gate, ffn_w_up, ffn_w_down, final_norm, loss_target, m_norm1, m_w_in, m_b_gate, m_ssm_a_re, m_ssm_a_im, m_ssm_log_dt, m_ssm_b_re, m_ssm_b_im, m_ssm_c_re, m_ssm_c_im, m_ssm_d, m_ssm_w_glu, m_ssm_b_glu, m_ssm_w_proj, m_conv_w_dw, m_conv_b_dw, m_conv_ln_g, m_conv_ln_b, m_conv_w_proj, m_pool_w_group, m_pool_scale, m_pool_w_proj, m_w_out, m_norm2, m_ffn_w_gate, m_ffn_w_up, m_ffn_w_down, m_final_norm, v_norm1, v_w_in, v_b_gate, v_ssm_a_re, v_ssm_a_im, v_ssm_log_dt, v_ssm_b_re, v_ssm_b_im, v_ssm_c_re, v_ssm_c_im, v_ssm_d, v_ssm_w_glu, v_ssm_b_glu, v_ssm_w_proj, v_conv_w_dw, v_conv_b_dw, v_conv_ln_g, v_conv_ln_b, v_conv_w_proj, v_pool_w_group, v_pool_scale, v_pool_w_proj, v_w_out, v_norm2, v_ffn_w_gate, v_ffn_w_up, v_ffn_w_down, v_final_norm):
    return _train_step(dict(locals()))
```

```python
import functools

import jax
import jax.numpy as jnp
from jax import lax
from jax.experimental import pallas as pl
from jax.experimental.pallas import tpu as pltpu

F32 = jnp.float32
BF16 = jnp.bfloat16

NDEV = 8
DEPTH = 2
D = 1024
BW = 512
NSTATE = 64
SGRP = 16
NGRP = BW // SGRP
GB = 8
NBLK = NGRP // GB
NS = GB * NSTATE
CONV_K = 31
HALO = 32
PHALO = 16
IN_W = 5120
HID = 2816
HSH = HID // NDEV
HPAD = 384
HIDP = HPAD * NDEV
EPS = 1e-6
VMEM_LIMIT = 56 * 1024 * 1024

ADAM_LR, ADAM_B1, ADAM_B2, ADAM_EPS, ADAM_WD, ADAM_STEP = 0.001, 0.9, 0.999, 1e-08, 0.01, 10

MESH = pl.DeviceIdType.MESH
ANY = pl.BlockSpec(memory_space=pl.ANY)


def _call(body, **kw):
    return pl.pallas_call(body, **kw)


def _cparams(sem=None):
    return pltpu.CompilerParams(dimension_semantics=sem, vmem_limit_bytes=VMEM_LIMIT)


def _dot(a, b):
    return jnp.dot(a.astype(BF16), b.astype(BF16), preferred_element_type=F32)


def _dot_nt(a, b):
    return lax.dot_general(a.astype(BF16), b.astype(BF16), (((1,), (1,)), ((), ())), preferred_element_type=F32)


def _dot_tn(a, b):
    return lax.dot_general(a.astype(BF16), b.astype(BF16), (((0,), (0,)), ((), ())), preferred_element_type=F32)


@jax.custom_vjp
def _mm(a, w):
    return _dot(a, w)


def _mm_fwd(a, w):
    return _dot(a, w), w


def _mm_bwd(w, ct):
    return _dot_nt(ct, w), jnp.zeros_like(w)


_mm.defvjp(_mm_fwd, _mm_bwd)


def _rms(x, g):
    return x * lax.rsqrt(jnp.mean(x * x, axis=-1, keepdims=True) + EPS) * g


def _disc(are, aim, ldt):
    dt = jnp.exp(ldt)
    mag = jnp.exp(dt * are)
    ang = dt * aim
    abr = mag * jnp.cos(ang)
    abi = mag * jnp.sin(ang)
    den = are * are + aim * aim
    nr = abr - 1.0
    fr = (nr * are + abi * aim) / den
    fi = (abi * are - nr * aim) / den
    return abr, abi, fr, fi


def _bbar(fr, fi, br, bi):
    return fr * br - fi * bi, fr * bi + fi * br


def _cmul(ar, ai, br, bi):
    return ar * br - ai * bi, ar * bi + ai * br


def _scan_rows(re_ref, im_ref, ar, ai, n_rows, reverse, hre_ref=None, him_ref=None):
    n = ar.shape[1]
    shape = (8, n)
    rows = lax.broadcasted_iota(jnp.int32, shape, 0)
    a1 = (jnp.broadcast_to(ar, shape), jnp.broadcast_to(ai, shape))
    a2 = _cmul(*a1, *a1)
    a4 = _cmul(*a2, *a2)
    pr = jnp.zeros(shape, F32)
    pi = jnp.zeros(shape, F32)
    pw = a1
    for k in range(8):
        sel = rows == ((7 - k) if reverse else k)
        pr = jnp.where(sel, pw[0], pr)
        pi = jnp.where(sel, pw[1], pi)
        pw = _cmul(*pw, *a1)
    nt = n_rows // 8
    with_acc = hre_ref is not None

    def body(i, carry):
        cr, ci = carry[0], carry[1]
        t = (nt - 1 - i) if reverse else i
        off = pl.multiple_of(t * 8, 8)
        xr = re_ref[pl.ds(off, 8), :]
        xi = im_ref[pl.ds(off, 8), :]
        for k, (kr, ki) in ((1, a1), (2, a2), (4, a4)):
            if reverse:
                keep, sh = rows < 8 - k, 8 - k
            else:
                keep, sh = rows >= k, k
            sr = jnp.where(keep, pltpu.roll(xr, sh, 0), 0.0)
            si = jnp.where(keep, pltpu.roll(xi, sh, 0), 0.0)
            xr, xi = xr + kr * sr - ki * si, xi + kr * si + ki * sr
        xr, xi = xr + pr * cr - pi * ci, xi + pr * ci + pi * cr
        re_ref[pl.ds(off, 8), :] = xr
        im_ref[pl.ds(off, 8), :] = xi
        edge = 0 if reverse else 7
        out = (jnp.broadcast_to(xr[edge:edge + 1, :], shape), jnp.broadcast_to(xi[edge:edge + 1, :], shape))
        if with_acc:
            hr = hre_ref[pl.ds(off, 8), :]
            hi = him_ref[pl.ds(off, 8), :]
            offp = pl.multiple_of(jnp.maximum(t - 1, 0) * 8, 8)
            live = jnp.where(t > 0, 1.0, 0.0)
            lr = jnp.broadcast_to(hre_ref[pl.ds(offp, 8), :][7:8, :], shape) * live
            li = jnp.broadcast_to(him_ref[pl.ds(offp, 8), :][7:8, :], shape) * live
            hpr = jnp.where(rows == 0, lr, pltpu.roll(hr, 1, 0))
            hpi = jnp.where(rows == 0, li, pltpu.roll(hi, 1, 0))
            out = out + (carry[2] + xr * hpr + xi * hpi, carry[3] + xi * hpr - xr * hpi)
        return out

    zero = jnp.zeros(shape, F32)
    init = (zero, zero, zero, zero) if with_acc else (zero, zero)
    res = lax.fori_loop(0, nt, body, init)
    return res[2:] if with_acc else None


TOKEN_SPEC = pl.BlockSpec((8, 128), lambda i, j: (0, 0))


def _inproj_fwd(x, gamma, w, token, tm=512, tn=1280):
    t_rows = x.shape[0]
    n = w.shape[1]

    def body(x_ref, g_ref, w_ref, token_ref, z_ref, h_ref):
        @pl.when(pl.program_id(1) == 0)
        def _():
            h_ref[...] = _rms(x_ref[...], g_ref[...]).astype(BF16)
        z_ref[...] = jnp.dot(h_ref[...], w_ref[...], preferred_element_type=F32)

    return _call(
        body, name="inproj_fwd", grid=(t_rows // tm, n // tn),
        in_specs=[pl.BlockSpec((tm, D), lambda i, j: (i, 0)), pl.BlockSpec((1, D), lambda i, j: (0, 0)),
                  pl.BlockSpec((D, tn), lambda i, j: (0, j)), TOKEN_SPEC],
        out_specs=pl.BlockSpec((tm, tn), lambda i, j: (i, j)),
        out_shape=jax.ShapeDtypeStruct((t_rows, n), F32),
        scratch_shapes=[pltpu.VMEM((tm, D), BF16)],
        compiler_params=_cparams(("parallel", "arbitrary")))(x, gamma, w, token)


def _ssm_specs(t_rows):
    row = pl.BlockSpec((1, NS), lambda j: (0, j))
    return dict(
        u=pl.BlockSpec((t_rows, GB * SGRP), lambda j: (0, j)),
        row=row,
        bexp=pl.BlockSpec((None, GB * SGRP, NS), lambda j: (j, 0, 0)),
        cexp=pl.BlockSpec((None, NS, GB * SGRP), lambda j: (j, 0, 0)),
        d=pl.BlockSpec((1, GB * SGRP), lambda j: (0, j)),
        h=pl.BlockSpec((t_rows, NS), lambda j: (0, j)),
    )


def _ssm_fwd(z, p):
    t_rows = z.shape[0]
    s = _ssm_specs(t_rows)

    def body(u_ref, are_ref, aim_ref, ldt_ref, br_ref, bi_ref, cr_ref, ci_ref, d_ref, y_ref, hr_ref, hi_ref):
        abr, abi, fr, fi = _disc(are_ref[...], aim_ref[...], ldt_ref[...])
        bbr, bbi = _bbar(fr, fi, br_ref[...], bi_ref[...])
        u = u_ref[...]
        hr_ref[...] = _dot(u, bbr)
        hi_ref[...] = _dot(u, bbi)
        _scan_rows(hr_ref, hi_ref, abr, abi, t_rows, False)
        y_ref[...] = _dot(hr_ref[...], cr_ref[...]) - _dot(hi_ref[...], ci_ref[...]) + d_ref[...] * u

    return _call(
        body, name="ssm_fwd", grid=(NBLK,),
        in_specs=[s["u"], s["row"], s["row"], s["row"], s["bexp"], s["bexp"], s["cexp"], s["cexp"], s["d"]],
        out_specs=[s["u"], s["h"], s["h"]],
        out_shape=[jax.ShapeDtypeStruct((t_rows, BW), F32), jax.ShapeDtypeStruct((t_rows, NGRP * NSTATE), F32),
                   jax.ShapeDtypeStruct((t_rows, NGRP * NSTATE), F32)],
        compiler_params=_cparams(("parallel",)))(
            z, p["are"], p["aim"], p["ldt"], p["bexp_re"], p["bexp_im"], p["cexp_re"], p["cexp_im"], p["dskip"])


def _ssm_bwd(dy, z, hre, him, p):
    t_rows = z.shape[0]
    s = _ssm_specs(t_rows)
    nstates = NGRP * NSTATE

    def body(dy_ref, u_ref, hr_ref, hi_ref, are_ref, aim_ref, ldt_ref, br_ref, bi_ref, cr_ref, ci_ref, d_ref,
             du_ref, dbr_ref, dbi_ref, dcr_ref, dci_ref, dd_ref, dar_ref, dai_ref, dldt_ref, lr_ref, li_ref):
        rows3 = (are_ref[...], aim_ref[...], ldt_ref[...])
        (abr, abi, fr, fi), disc_vjp = jax.vjp(_disc, *rows3)
        (bbr, bbi), bbar_vjp = jax.vjp(_bbar, fr, fi, br_ref[...], bi_ref[...])
        dy = dy_ref[...]
        u = u_ref[...]
        lr_ref[...] = _dot_nt(dy, cr_ref[...])
        li_ref[...] = -_dot_nt(dy, ci_ref[...])
        dcr_ref[...] = _dot_tn(hr_ref[...], dy)
        dci_ref[...] = -_dot_tn(hi_ref[...], dy)
        dd_ref[...] = jnp.sum(dy * u, axis=0, keepdims=True)
        acc_r, acc_i = _scan_rows(lr_ref, li_ref, abr, -abi, t_rows, True, hr_ref, hi_ref)
        dabr = jnp.sum(acc_r, axis=0, keepdims=True)
        dabi = jnp.sum(acc_i, axis=0, keepdims=True)
        lam_r = lr_ref[...]
        lam_i = li_ref[...]
        du = d_ref[...] * dy + _dot_nt(lam_r, bbr) + _dot_nt(lam_i, bbi)
        du_ref[...] = du.astype(BF16)
        dbbr = _dot_tn(u, lam_r)
        dbbi = _dot_tn(u, lam_i)
        dfr, dfi, dbr, dbi = bbar_vjp((dbbr, dbbi))
        dbr_ref[...] = dbr
        dbi_ref[...] = dbi
        dar, dai, dldt = disc_vjp((dabr, dabi, dfr, dfi))
        dar_ref[...] = dar
        dai_ref[...] = dai
        lane_grp = lax.broadcasted_iota(jnp.int32, (NS, 128), 0) // NSTATE
        col = lax.broadcasted_iota(jnp.int32, (NS, 128), 1)
        seg = jnp.where(lane_grp == col, 1.0, 0.0).astype(F32)
        dldt_ref[...] = jnp.dot(jnp.broadcast_to(dldt, (8, NS)), seg, preferred_element_type=F32,
                                precision=lax.Precision.HIGHEST)

    dyspec = pl.BlockSpec((t_rows, GB * SGRP), lambda j: (0, j))
    return _call(
        body, name="ssm_bwd", grid=(NBLK,),
        in_specs=[dyspec, s["u"], s["h"], s["h"], s["row"], s["row"], s["row"], s["bexp"], s["bexp"], s["cexp"],
                  s["cexp"], s["d"]],
        out_specs=[dyspec, s["bexp"], s["bexp"], s["cexp"], s["cexp"], s["d"], s["row"], s["row"],
                   pl.BlockSpec((8, 128), lambda j: (j, 0))],
        out_shape=[jax.ShapeDtypeStruct((t_rows, BW), BF16),
                   jax.ShapeDtypeStruct((NBLK, GB * SGRP, NS), F32), jax.ShapeDtypeStruct((NBLK, GB * SGRP, NS), F32),
                   jax.ShapeDtypeStruct((NBLK, NS, GB * SGRP), F32), jax.ShapeDtypeStruct((NBLK, NS, GB * SGRP), F32),
                   jax.ShapeDtypeStruct((1, BW), F32), jax.ShapeDtypeStruct((1, nstates), F32),
                   jax.ShapeDtypeStruct((1, nstates), F32), jax.ShapeDtypeStruct((NBLK * 8, 128), F32)],
        scratch_shapes=[pltpu.VMEM((t_rows, NS), F32), pltpu.VMEM((t_rows, NS), F32)],
        compiler_params=_cparams(("parallel",)))(
            dy, z, hre, him, p["are"], p["aim"], p["ldt"], p["bexp_re"], p["bexp_im"], p["cexp_re"], p["cexp_im"],
            p["dskip"])


def _conv_fwd(z, w, b, tm=256):
    t_rows = z.shape[0]
    hb = tm // HALO

    def body(va_ref, vb_ref, ha_ref, hb_ref, w_ref, b_ref, o_ref, win_ref):
        live = jnp.where(pl.program_id(0) > 0, 1.0, 0.0)
        win_ref[0:HALO, :] = ha_ref[...] * jax.nn.sigmoid(hb_ref[...]) * live
        win_ref[HALO:HALO + tm, :] = va_ref[...] * jax.nn.sigmoid(vb_ref[...])
        acc = jnp.broadcast_to(b_ref[...], (tm, BW))
        for k in range(CONV_K):
            acc = acc + w_ref[k:k + 1, :] * win_ref[pl.ds(HALO - (CONV_K - 1) + k, tm), :]
        o_ref[...] = acc

    halo = lambda col: pl.BlockSpec((HALO, BW), lambda i: (jnp.maximum(i * hb - 1, 0), col))
    return _call(
        body, name="conv_fwd", grid=(t_rows // tm,),
        in_specs=[pl.BlockSpec((tm, BW), lambda i: (i, 1)), pl.BlockSpec((tm, BW), lambda i: (i, 2)), halo(1), halo(2),
                  pl.BlockSpec((CONV_K, BW), lambda i: (0, 0)), pl.BlockSpec((1, BW), lambda i: (0, 0))],
        out_specs=pl.BlockSpec((tm, BW), lambda i: (i, 0)),
        out_shape=jax.ShapeDtypeStruct((t_rows, BW), F32),
        scratch_shapes=[pltpu.VMEM((HALO + tm, BW), F32)],
        compiler_params=_cparams(("parallel",)))(z, z, z, z, w, b)


def _conv_bwd(dcv, z, w, tm=256):
    t_rows = z.shape[0]
    nt = t_rows // tm
    hb = tm // HALO
    csh = BW // NDEV

    def body(d_ref, dn_ref, va_ref, vb_ref, ha_ref, hb_ref, w_ref, dva_ref, dvb_ref, dw8_ref, db_ref,
             hwin_ref, dwin_ref, dw_ref):
        i = pl.program_id(0)

        @pl.when(i == 0)
        def _():
            dw_ref[...] = jnp.zeros_like(dw_ref)
            db_ref[...] = jnp.zeros_like(db_ref)

        live_prev = jnp.where(i > 0, 1.0, 0.0)
        live_next = jnp.where(i < nt - 1, 1.0, 0.0)
        va = va_ref[...]
        sig = jax.nn.sigmoid(vb_ref[...])
        hwin_ref[0:HALO, :] = ha_ref[...] * jax.nn.sigmoid(hb_ref[...]) * live_prev
        hwin_ref[HALO:HALO + tm, :] = va * sig
        d = d_ref[...]
        dwin_ref[0:tm, :] = d
        dwin_ref[tm:tm + HALO, :] = dn_ref[...] * live_next
        dh = jnp.zeros((tm, BW), F32)
        dws = []
        for k in range(CONV_K):
            dh = dh + w_ref[k:k + 1, :] * dwin_ref[pl.ds(CONV_K - 1 - k, tm), :]
            dws.append(jnp.sum(d * hwin_ref[pl.ds(HALO - (CONV_K - 1) + k, tm), :], axis=0, keepdims=True))
        dws.append(jnp.zeros((1, BW), F32))
        dw_ref[...] += jnp.concatenate(dws, axis=0)
        db_ref[...] += jnp.sum(d, axis=0, keepdims=True)
        dva_ref[...] = (dh * sig).astype(BF16)
        dvb_ref[...] = (dh * va * sig * (1.0 - sig)).astype(BF16)

        @pl.when(i == nt - 1)
        def _():
            acc = dw_ref[...]
            for q in range(NDEV):
                dw8_ref[q] = acc[:, csh * q:csh * (q + 1)]

    halo = lambda col: pl.BlockSpec((HALO, BW), lambda i: (jnp.maximum(i * hb - 1, 0), col))
    return _call(
        body, name="conv_bwd", grid=(nt,),
        in_specs=[pl.BlockSpec((tm, BW), lambda i: (i, 0)),
                  pl.BlockSpec((HALO, BW), lambda i: (jnp.minimum((i + 1) * hb, t_rows // HALO - 1), 0)),
                  pl.BlockSpec((tm, BW), lambda i: (i, 1)), pl.BlockSpec((tm, BW), lambda i: (i, 2)), halo(1), halo(2),
                  pl.BlockSpec((CONV_K, BW), lambda i: (0, 0))],
        out_specs=[pl.BlockSpec((tm, BW), lambda i: (i, 0)), pl.BlockSpec((tm, BW), lambda i: (i, 0)),
                   pl.BlockSpec((NDEV, 32, csh), lambda i: (0, 0, 0)), pl.BlockSpec((1, BW), lambda i: (0, 0))],
        out_shape=[jax.ShapeDtypeStruct((t_rows, BW), BF16), jax.ShapeDtypeStruct((t_rows, BW), BF16),
                   jax.ShapeDtypeStruct((NDEV, 32, csh), F32), jax.ShapeDtypeStruct((1, BW), F32)],
        scratch_shapes=[pltpu.VMEM((HALO + tm, BW), F32), pltpu.VMEM((tm + HALO, BW), F32), pltpu.VMEM((32, BW), F32)],
        compiler_params=_cparams(("arbitrary",)))(dcv, dcv, z, z, z, z, w)


def _pool_rows(i, tm, n_rows, first_row):
    grp = lax.broadcasted_iota(jnp.int32, (1, BW), 1) // (BW // 4)
    wlen = jnp.where(grp == 0, 2.0, jnp.where(grp == 1, 4.0, jnp.where(grp == 2, 8.0, 16.0)))
    t = (i * tm + first_row + lax.broadcasted_iota(jnp.int32, (n_rows, 1), 0)).astype(F32)
    return grp, 1.0 / jnp.minimum(t + 1.0, wlen)


def _pool_pick(grp, s2, s4, s8, s16):
    return jnp.where(grp == 0, s2, jnp.where(grp == 1, s4, jnp.where(grp == 2, s8, s16)))


def _pool_fwd(z, tm=256):
    t_rows = z.shape[0]
    hb = tm // PHALO

    def body(u_ref, h_ref, o_ref):
        i = pl.program_id(0)
        u = u_ref[...]
        win = jnp.concatenate([h_ref[...] * jnp.where(i > 0, 1.0, 0.0), u], axis=0)
        s2 = win + pltpu.roll(win, 1, 0)
        s4 = s2 + pltpu.roll(s2, 2, 0)
        s8 = s4 + pltpu.roll(s4, 4, 0)
        s16 = s8 + pltpu.roll(s8, 8, 0)
        grp, inv = _pool_rows(i, tm, tm, 0)
        o_ref[...] = _pool_pick(grp, s2, s4, s8, s16)[PHALO:, :] * inv - u

    return _call(
        body, name="pool_fwd", grid=(t_rows // tm,),
        in_specs=[pl.BlockSpec((tm, BW), lambda i: (i, 3)),
                  pl.BlockSpec((PHALO, BW), lambda i: (jnp.maximum(i * hb - 1, 0), 3))],
        out_specs=pl.BlockSpec((tm, BW), lambda i: (i, 0)),
        out_shape=jax.ShapeDtypeStruct((t_rows, BW), F32),
        compiler_params=_cparams(("parallel",)))(z, z)


def _pool_bwd(dp, tm=256):
    t_rows = dp.shape[0]
    nt = t_rows // tm
    hb = tm // PHALO
    ln = tm + PHALO

    def body(d_ref, dn_ref, o_ref):
        i = pl.program_id(0)
        d = d_ref[...]
        grp, inv = _pool_rows(i, tm, ln, 0)
        win = jnp.concatenate([d, dn_ref[...] * jnp.where(i < nt - 1, 1.0, 0.0)], axis=0) * inv
        s2 = win + pltpu.roll(win, ln - 1, 0)
        s4 = s2 + pltpu.roll(s2, ln - 2, 0)
        s8 = s4 + pltpu.roll(s4, ln - 4, 0)
        s16 = s8 + pltpu.roll(s8, ln - 8, 0)
        o_ref[...] = (_pool_pick(grp, s2, s4, s8, s16)[:tm, :] - d).astype(BF16)

    return _call(
        body, name="pool_bwd", grid=(nt,),
        in_specs=[pl.BlockSpec((tm, BW), lambda i: (i, 0)),
                  pl.BlockSpec((PHALO, BW), lambda i: (jnp.minimum((i + 1) * hb, t_rows // PHALO - 1), 0))],
        out_specs=pl.BlockSpec((tm, BW), lambda i: (i, 0)),
        out_shape=jax.ShapeDtypeStruct((t_rows, BW), BF16),
        compiler_params=_cparams(("parallel",)))(dp, dp)


_MERGE_W = ("wglu", "bglu", "wpa", "lng", "lnb", "wpb", "wgrp", "scale", "wpc", "bgate", "wout")
_MERGE_SMALL = ("bglu", "lng", "lnb", "scale", "bgate")


def _merge_math(x, yssm, cv, pbar, zg, w, taps):
    t_glu, t_ya, t_yb, t_p, t_yc = taps
    g = jax.nn.gelu(yssm)
    outa = g * jax.nn.sigmoid(_mm(g, w["wglu"]) + t_glu + w["bglu"])
    ya = _mm(outa, w["wpa"]) + t_ya
    mu = jnp.mean(cv, axis=-1, keepdims=True)
    var = jnp.mean(jnp.square(cv - mu), axis=-1, keepdims=True)
    hs = jax.nn.silu((cv - mu) * lax.rsqrt(var + EPS) * w["lng"] + w["lnb"])
    yb = _mm(hs, w["wpb"]) + t_yb
    gw = BW // 4
    pk = jnp.concatenate([_mm(pbar[:, gw * k:gw * (k + 1)], w["wgrp"][k]) for k in range(4)], axis=1) + t_p
    pc = pk * w["scale"]
    yc = _mm(pc, w["wpc"]) + t_yc
    gates = jax.nn.sigmoid(zg + w["bgate"])
    merged = gates[:, :D] * ya + gates[:, D:2 * D] * yb + gates[:, 2 * D:] * yc
    x1 = x + _mm(merged, w["wout"])
    acts = tuple(a.astype(BF16) for a in (g, outa, hs, pbar, pc, merged))
    return x1, acts


def _merge_specs(tm, p):
    rows = lambda width, col=0: pl.BlockSpec((tm, width), lambda i, c=col: (i, c))
    data = [rows(D), rows(BW), rows(BW), rows(BW), rows(D, 2), rows(D, 3), rows(D, 4)]
    wspecs = []
    for name in _MERGE_W:
        nd = p[name].ndim
        wspecs.append(pl.BlockSpec(p[name].shape, lambda i, nd=nd: (0,) * nd))
    return rows, data, wspecs


def _merge_fwd(x, yssm, cv, pbar, z, p, tm=256):
    t_rows = x.shape[0]
    rows, data, wspecs = _merge_specs(tm, p)

    def body(x_ref, y_ref, cv_ref, pb_ref, za_ref, zb_ref, zc_ref, *rest):
        w = {name: r[...] for name, r in zip(_MERGE_W, rest[:len(_MERGE_W)])}
        o_ref = rest[len(_MERGE_W)]
        taps = (0.0, 0.0, 0.0, 0.0, 0.0)
        zg = jnp.concatenate([za_ref[...], zb_ref[...], zc_ref[...]], axis=1)
        o_ref[...] = _merge_math(x_ref[...], y_ref[...], cv_ref[...], pb_ref[...], zg, w, taps)[0]

    return _call(
        body, name="merge_fwd", grid=(t_rows // tm,), in_specs=data + wspecs, out_specs=rows(D),
        out_shape=jax.ShapeDtypeStruct((t_rows, D), F32),
        compiler_params=_cparams(("parallel",)))(x, yssm, cv, pbar, z, z, z, *[p[n] for n in _MERGE_W])


def _merge_bwd(dx1, x, yssm, cv, pbar, z, p, tm=256):
    t_rows = x.shape[0]
    rows, data, wspecs = _merge_specs(tm, p)
    nw = len(_MERGE_W)

    def body(dx_ref, x_ref, y_ref, cv_ref, pb_ref, za_ref, zb_ref, zc_ref, *rest):
        w = {name: r[...] for name, r in zip(_MERGE_W, rest[:nw])}
        zg = jnp.concatenate([za_ref[...], zb_ref[...], zc_ref[...]], axis=1)
        outs = rest[nw:]
        small = {n: w[n] for n in _MERGE_SMALL}
        taps = (jnp.zeros((tm, BW), F32), jnp.zeros((tm, D), F32), jnp.zeros((tm, D), F32),
                jnp.zeros((tm, BW), F32), jnp.zeros((tm, D), F32))

        def f(yssm_, cv_, pbar_, zg_, small_, taps_):
            return _merge_math(x_ref[...], yssm_, cv_, pbar_, zg_, {**w, **small_}, taps_)

        _, vjp, acts = jax.vjp(f, y_ref[...], cv_ref[...], pb_ref[...], zg, small, taps, has_aux=True)
        dy, dcv, dpb, dzg, dsmall, dtaps = vjp(dx_ref[...])
        outs[0][...] = dy
        outs[1][...] = dcv
        outs[2][...] = dpb
        outs[3][...] = dzg.astype(BF16)
        for k in range(6):
            outs[4 + k][...] = acts[k]
        for k in range(5):
            outs[10 + k][...] = dtaps[k].astype(BF16)

        @pl.when(pl.program_id(0) == 0)
        def _():
            for k in range(5):
                outs[15 + k][...] = jnp.zeros_like(outs[15 + k])

        for k, n in enumerate(_MERGE_SMALL):
            outs[15 + k][...] += dsmall[n]

    f32o = lambda width: jax.ShapeDtypeStruct((t_rows, width), F32)
    bfo = lambda width: jax.ShapeDtypeStruct((t_rows, width), BF16)
    small_shapes = [jax.ShapeDtypeStruct(p[n].shape, F32) for n in _MERGE_SMALL]
    small_specs = [pl.BlockSpec(p[n].shape, lambda i: (0, 0)) for n in _MERGE_SMALL]
    out_shape = ([f32o(BW), f32o(BW), f32o(BW), bfo(3 * D)]
                 + [bfo(BW), bfo(BW), bfo(BW), bfo(BW), bfo(BW), bfo(D)]
                 + [bfo(BW), bfo(D), bfo(D), bfo(BW), bfo(D)] + small_shapes)
    out_specs = ([rows(BW), rows(BW), rows(BW), rows(3 * D)]
                 + [rows(BW)] * 5 + [rows(D)]
                 + [rows(BW), rows(D), rows(D), rows(BW), rows(D)] + small_specs)
    return _call(
        body, name="merge_bwd", grid=(t_rows // tm,), in_specs=[rows(D)] + data + wspecs, out_specs=out_specs,
        out_shape=out_shape, compiler_params=_cparams(("arbitrary",)))(
            dx1, x, yssm, cv, pbar, z, z, z, *[p[n] for n in _MERGE_W])


def _ffn_fwd(x1, gamma, wg, wu, wd, tm=512, th=512):
    t_rows = x1.shape[0]
    nh = HIDP // th

    def body(x_ref, g_ref, wg_ref, wu_ref, wd_ref, o_ref, gp_ref, up_ref, h_ref, acc_ref):
        j = pl.program_id(1)

        @pl.when(j == 0)
        def _():
            h_ref[...] = _rms(x_ref[...], g_ref[...]).astype(BF16)
            acc_ref[...] = jnp.zeros_like(acc_ref)

        gp = jnp.dot(h_ref[...], wg_ref[...], preferred_element_type=F32)
        up = jnp.dot(h_ref[...], wu_ref[...], preferred_element_type=F32)
        gp_ref[...] = gp
        up_ref[...] = up
        acc_ref[...] += _dot(jax.nn.silu(gp) * up, wd_ref[...])

        @pl.when(j == nh - 1)
        def _():
            o_ref[...] = x_ref[...] + acc_ref[...]

    return _call(
        body, name="ffn_fwd", grid=(t_rows // tm, nh),
        in_specs=[pl.BlockSpec((tm, D), lambda i, j: (i, 0)), pl.BlockSpec((1, D), lambda i, j: (0, 0)),
                  pl.BlockSpec((D, th), lambda i, j: (0, j)), pl.BlockSpec((D, th), lambda i, j: (0, j)),
                  pl.BlockSpec((th, D), lambda i, j: (j, 0))],
        out_specs=[pl.BlockSpec((tm, D), lambda i, j: (i, 0)), pl.BlockSpec((tm, th), lambda i, j: (i, j)),
                   pl.BlockSpec((tm, th), lambda i, j: (i, j))],
        out_shape=[jax.ShapeDtypeStruct((t_rows, D), F32), jax.ShapeDtypeStruct((t_rows, HIDP), F32),
                   jax.ShapeDtypeStruct((t_rows, HIDP), F32)],
        scratch_shapes=[pltpu.VMEM((tm, D), BF16), pltpu.VMEM((tm, D), F32)],
        compiler_params=_cparams(("parallel", "arbitrary")))(x1, gamma, wg, wu, wd)


def _rms_bwd_tail(x, gamma, dh):
    _, vjp = jax.vjp(_rms, x, gamma)
    return vjp(dh)


def _ffn_bwd(dx2, x1, gamma, gpre, upre, wg, wu, wd, token, tm=512, th=512):
    t_rows = x1.shape[0]
    nh = HIDP // th

    def body(d_ref, x_ref, g_ref, gp_ref, up_ref, wg_ref, wu_ref, wd_ref, token_ref,
             dx_ref, dgam_ref, dgp_ref, dup_ref, act_ref, h_ref, acc_ref):
        i = pl.program_id(0)
        j = pl.program_id(1)

        @pl.when(j == 0)
        def _():
            acc_ref[...] = jnp.zeros_like(acc_ref)

        @pl.when((i == 0) & (j == 0))
        def _():
            dgam_ref[...] = jnp.zeros_like(dgam_ref)

        dact = _dot_nt(d_ref[...], wd_ref[...])
        gp = gp_ref[...]
        up = up_ref[...]
        sg = jax.nn.sigmoid(gp)
        silu = gp * sg
        dgp = (dact * up * (sg * (1.0 + gp * (1.0 - sg)))).astype(BF16)
        dup = (dact * silu).astype(BF16)
        dgp_ref[...] = dgp
        dup_ref[...] = dup
        act_ref[...] = (silu * up).astype(BF16)
        acc_ref[...] += _dot_nt(dgp, wg_ref[...]) + _dot_nt(dup, wu_ref[...])

        @pl.when(j == nh - 1)
        def _():
            x = x_ref[...]
            h_ref[...] = _rms(x, g_ref[...]).astype(BF16)
            dx, dgam = _rms_bwd_tail(x, g_ref[...], acc_ref[...])
            dx_ref[...] = d_ref[...] + dx
            dgam_ref[...] += dgam

    row_d = pl.BlockSpec((tm, D), lambda i, j: (i, 0))
    row_h = pl.BlockSpec((tm, th), lambda i, j: (i, j))
    return _call(
        body, name="ffn_bwd", grid=(t_rows // tm, nh),
        in_specs=[row_d, row_d, pl.BlockSpec((1, D), lambda i, j: (0, 0)), row_h, row_h,
                  pl.BlockSpec((D, th), lambda i, j: (0, j)), pl.BlockSpec((D, th), lambda i, j: (0, j)),
                  pl.BlockSpec((th, D), lambda i, j: (j, 0)), TOKEN_SPEC],
        out_specs=[row_d, pl.BlockSpec((1, D), lambda i, j: (0, 0)), row_h, row_h, row_h, row_d],
        out_shape=[jax.ShapeDtypeStruct((t_rows, D), F32), jax.ShapeDtypeStruct((1, D), F32),
                   jax.ShapeDtypeStruct((t_rows, HIDP), BF16), jax.ShapeDtypeStruct((t_rows, HIDP), BF16),
                   jax.ShapeDtypeStruct((t_rows, HIDP), BF16), jax.ShapeDtypeStruct((t_rows, D), BF16)],
        scratch_shapes=[pltpu.VMEM((tm, D), F32)],
        compiler_params=_cparams(("arbitrary", "arbitrary")))(dx2, x1, gamma, gpre, upre, wg, wu, wd, token)


def _inproj_bwd(dz, dx1, x, gamma, w, tm=512, tn=1280):
    t_rows = x.shape[0]
    nn = IN_W // tn

    def body(dz_ref, d1_ref, x_ref, g_ref, w_ref, dx_ref, dgam_ref, h_ref, acc_ref):
        i = pl.program_id(0)
        j = pl.program_id(1)

        @pl.when(j == 0)
        def _():
            acc_ref[...] = jnp.zeros_like(acc_ref)

        @pl.when((i == 0) & (j == 0))
        def _():
            dgam_ref[...] = jnp.zeros_like(dgam_ref)

        acc_ref[...] += _dot_nt(dz_ref[...], w_ref[...])

        @pl.when(j == nn - 1)
        def _():
            x = x_ref[...]
            h_ref[...] = _rms(x, g_ref[...]).astype(BF16)
            dx, dgam = _rms_bwd_tail(x, g_ref[...], acc_ref[...])
            dx_ref[...] = d1_ref[...] + dx
            dgam_ref[...] += dgam

    row_d = pl.BlockSpec((tm, D), lambda i, j: (i, 0))
    return _call(
        body, name="inproj_bwd", grid=(t_rows // tm, nn),
        in_specs=[pl.BlockSpec((tm, tn), lambda i, j: (i, j)), row_d, row_d, pl.BlockSpec((1, D), lambda i, j: (0, 0)),
                  pl.BlockSpec((D, tn), lambda i, j: (0, j))],
        out_specs=[row_d, pl.BlockSpec((1, D), lambda i, j: (0, 0)), row_d],
        out_shape=[jax.ShapeDtypeStruct((t_rows, D), F32), jax.ShapeDtypeStruct((1, D), F32),
                   jax.ShapeDtypeStruct((t_rows, D), BF16)],
        scratch_shapes=[pltpu.VMEM((tm, D), F32)],
        compiler_params=_cparams(("arbitrary", "arbitrary")))(dz, dx1, x, gamma, w)


def _matmul_tn(a, b, name, owner_cols=None, tt=512):
    t_rows, k = a.shape
    n = b.shape[1]
    tk = min(k, 1024)
    nt = t_rows // tt
    if owner_cols is None:
        tn, nb = min(n, 512), None
        out_spec = pl.BlockSpec((tk, tn), lambda i, j, t: (i, j))
        out_shape = jax.ShapeDtypeStruct((k, n), BF16)
    else:
        nb = max(1, 512 // owner_cols)
        tn = nb * owner_cols
        out_spec = pl.BlockSpec((nb, tk, owner_cols), lambda i, j, t: (j, i, 0))
        out_shape = jax.ShapeDtypeStruct((n // owner_cols, k, owner_cols), BF16)

    def body(a_ref, b_ref, o_ref, acc_ref):
        t = pl.program_id(2)

        @pl.when(t == 0)
        def _():
            acc_ref[...] = jnp.zeros_like(acc_ref)

        acc_ref[...] += _dot_tn(a_ref[...], b_ref[...])

        @pl.when(t == nt - 1)
        def _():
            if nb is None:
                o_ref[...] = acc_ref[...].astype(BF16)
            else:
                for q in range(nb):
                    o_ref[q] = acc_ref[:, owner_cols * q:owner_cols * (q + 1)].astype(BF16)

    return _call(
        body, name=name, grid=(k // tk, n // tn, nt),
        in_specs=[pl.BlockSpec((tt, tk), lambda i, j, t: (t, i)), pl.BlockSpec((tt, tn), lambda i, j, t: (t, j))],
        out_specs=out_spec, out_shape=out_shape,
        scratch_shapes=[pltpu.VMEM((tk, tn), F32)],
        compiler_params=_cparams(("parallel", "parallel", "arbitrary")))(a, b)


def _group_tn(a, b):
    t_rows = a.shape[0]
    gw = BW // 4

    def body(a_ref, b_ref, o_ref):
        o_ref[...] = _dot_tn(a_ref[...], b_ref[...])

    return _call(
        body, name="pool_group_tn", grid=(4,),
        in_specs=[pl.BlockSpec((t_rows, gw), lambda k: (0, k)), pl.BlockSpec((t_rows, gw), lambda k: (0, k))],
        out_specs=pl.BlockSpec((None, gw, gw), lambda k: (k, 0, 0)),
        out_shape=jax.ShapeDtypeStruct((4, gw, gw), F32),
        compiler_params=_cparams(("parallel",)))(a, b)


def _loss_head(x2, gamma, target, tm=512):
    t_rows = x2.shape[0]

    def body(x_ref, g_ref, t_ref, loss_ref, dx_ref, dgam_ref):
        @pl.when(pl.program_id(0) == 0)
        def _():
            loss_ref[...] = jnp.zeros_like(loss_ref)
            dgam_ref[...] = jnp.zeros_like(dgam_ref)

        def f(x, g):
            err = jnp.square(_rms(x, g) - t_ref[...])
            return 0.5 * jnp.sum(jnp.mean(err, axis=-1, keepdims=True), axis=0, keepdims=True)

        loss, vjp = jax.vjp(f, x_ref[...], g_ref[...])
        dx, dgam = vjp(jnp.ones((1, 1), F32))
        loss_ref[...] += jnp.broadcast_to(loss, (1, 128))
        dx_ref[...] = dx
        dgam_ref[...] += dgam

    row_d = pl.BlockSpec((tm, D), lambda i: (i, 0))
    return _call(
        body, name="loss_head", grid=(t_rows // tm,),
        in_specs=[row_d, pl.BlockSpec((1, D), lambda i: (0, 0)), row_d],
        out_specs=[pl.BlockSpec((1, 128), lambda i: (0, 0)), row_d, pl.BlockSpec((1, D), lambda i: (0, 0))],
        out_shape=[jax.ShapeDtypeStruct((1, 128), F32), jax.ShapeDtypeStruct((t_rows, D), F32),
                   jax.ShapeDtypeStruct((1, D), F32)],
        compiler_params=_cparams(("arbitrary",)))(x2, gamma, target)


NCHIP = NDEV // 2


def _coords():
    return lax.axis_index("x"), lax.axis_index("y"), lax.axis_index("c")


def _remote(src, dst, send_sem, recv_sem, peer):
    return pltpu.make_async_remote_copy(src_ref=src, dst_ref=dst, send_sem=send_sem, recv_sem=recv_sem,
                                        device_id=peer, device_id_type=MESH)


def _comm_call(name, srcs, out_shapes, n_rec, plan, aliases=None):
    ns, no = len(srcs), len(out_shapes)

    def body(*refs):
        ins, outs = refs[:ns], refs[ns:ns + no]
        loc_sem, send_sem, recv_sem = refs[ns + no:]
        x, y, c = _coords()
        recs = plan(ins, outs, x, y, c)
        assert len(recs) == n_rec
        for k, r in enumerate(recs):
            for src, dst in r.get("local", ()):
                pltpu.make_async_copy(src, dst, loc_sem.at[k]).start()
            for peer, src, dst in r.get("remote", ()):
                _remote(src, dst, send_sem.at[k], recv_sem.at[k], peer).start()
        for k, r in enumerate(recs):
            if r.get("recv_wait") is not None:
                w = r["recv_wait"]
                _remote(w, w, send_sem.at[k], recv_sem.at[k], (x, y, c)).wait_recv()
            if r.get("send_wait") is not None:
                w = r["send_wait"]
                _remote(w, w, send_sem.at[k], recv_sem.at[k], (x, y, c)).wait_send()
            if r.get("local_wait") is not None:
                w = r["local_wait"]
                pltpu.make_async_copy(w, w, loc_sem.at[k]).wait()

    return _call(
        body, name=name, in_specs=[ANY] * ns, out_specs=[ANY] * no, out_shape=out_shapes,
        input_output_aliases=aliases or {}, scratch_shapes=[pltpu.SemaphoreType.DMA((n_rec,))] * 3)(*srcs)


def _gather_call(srcs, out_shapes, items):
    ns, no, n = len(srcs), len(out_shapes), len(items)

    def body(*refs):
        ins, outs = refs[:ns], refs[ns:ns + no]
        loc, sib_s, sib_r, ici_s, ici_r, fwd_s, fwd_r = refs[ns + no:]
        x, y, c = _coords()
        me, sib = (x, y, c), (x, y, 1 - c)
        chips = [(1 - x, y), (x, 1 - y), (1 - x, 1 - y)]
        index = lambda px, py, pc: 4 * px + 2 * py + pc
        for k, (si, oi, shard, block, _) in enumerate(items):
            src, mine = shard(ins[si]), block(outs[oi], index(*me))
            pltpu.make_async_copy(src, mine, loc.at[k]).start()
            _remote(src, mine, sib_s.at[k], sib_r.at[k], sib).start()
            for chip in chips:
                _remote(src, mine, ici_s.at[k], ici_r.at[k], (*chip, c)).start()
        for k, (si, oi, _, block, blocks) in enumerate(items):
            three = blocks(outs[oi], 3)
            _remote(three, three, ici_s.at[k], ici_r.at[k], me).wait_recv()
            for chip in chips:
                landed = block(outs[oi], index(*chip, c))
                _remote(landed, landed, fwd_s.at[k], fwd_r.at[k], sib).start()
        for k, (si, oi, _, _, blocks) in enumerate(items):
            one, three = blocks(outs[oi], 1), blocks(outs[oi], 3)
            _remote(one, one, sib_s.at[k], sib_r.at[k], me).wait()
            _remote(three, three, fwd_s.at[k], fwd_r.at[k], me).wait()
            _remote(three, three, ici_s.at[k], ici_r.at[k], me).wait_send()
            pltpu.make_async_copy(one, one, loc.at[k]).wait()

    return _call(
        body, name="gather_weights", in_specs=[ANY] * ns, out_specs=[ANY] * no, out_shape=out_shapes,
        scratch_shapes=[pltpu.SemaphoreType.DMA((n,))] * 7)(*srcs)


_BIG = {
    "w_in": (1, D, IN_W // NDEV, D, IN_W),
    "ssm_w_glu": (0, BW // NDEV, BW, BW, BW),
    "ssm_w_proj": (1, BW, D // NDEV, BW, D),
    "conv_w_proj": (1, BW, D // NDEV, BW, D),
    "pool_w_proj": (1, BW, D // NDEV, BW, D),
    "w_out": (0, D // NDEV, D, D, D),
    "ffn_w_gate": (1, D, HPAD, D, HIDP),
    "ffn_w_up": (1, D, HPAD, D, HIDP),
    "ffn_w_down": (0, HPAD, D, HIDP, D),
}


def _block_view(axis, size):
    if axis == 1:
        return lambda ref, q: ref.at[:, pl.ds(pl.multiple_of(q * size, 128), size)]
    return lambda ref, q: ref.at[pl.ds(pl.multiple_of(q * size, 16), size), :]


def _blocks_view(axis, size):
    if axis == 1:
        return lambda ref, n: ref.at[:, pl.ds(0, n * size)]
    return lambda ref, n: ref.at[pl.ds(0, n * size), :]


def _gather_weights(shards, conv_dw, layers):
    srcs, outs, items, where = [], [], [], {}
    for name, (axis, kk, nn, kf, nf) in _BIG.items():
        srcs.append(shards[name])
        size = nn if axis == 1 else kk
        for l in layers:
            where[(name, l)] = len(outs)
            outs.append(jax.ShapeDtypeStruct((kf, nf), BF16))
            items.append((len(srcs) - 1, len(outs) - 1, lambda ref, l=l: ref.at[l], _block_view(axis, size),
                          _blocks_view(axis, size)))
    srcs.append(conv_dw)
    outs.append(jax.ShapeDtypeStruct((NDEV,) + conv_dw.shape, conv_dw.dtype))
    items.append((len(srcs) - 1, len(outs) - 1, lambda ref: ref, lambda ref, q: ref.at[q],
                  lambda ref, n: ref.at[pl.ds(0, n)]))
    res = _gather_call(srcs, outs, items)
    full = {name: {l: res[where[(name, l)]] for l in layers} for name in _BIG}
    return full, res[-1]


def _gather_start(shards, layer, after):
    names = list(_BIG)
    lands = [jax.ShapeDtypeStruct((_BIG[n][3], _BIG[n][4]), BF16) for n in names]

    def copies(src_refs, land_refs, x, y, c):
        me = 4 * x + 2 * y + c
        peers = [(x, y, 1 - c), (1 - x, y, c), (x, 1 - y, c), (1 - x, 1 - y, c)]
        out = []
        for k, n in enumerate(names):
            axis, kk, nn, _, _ = _BIG[n]
            mine = _block_view(axis, nn if axis == 1 else kk)(land_refs[k], me)
            out.append([(peer, src_refs[k].at[layer], mine) for peer in peers])
        return out

    return _split_start("gather_start", [shards[n] for n in names], lands, copies, after)


def _gather_finish(handle, layer, after):
    names = list(_BIG)
    sizes = [(_BIG[n][0], _BIG[n][2] if _BIG[n][0] == 1 else _BIG[n][1]) for n in names]
    four = [functools.partial(lambda ref, bv: bv(ref, 4), bv=_blocks_view(axis, size)) for axis, size in sizes]
    lands = _split_wait("gather_wait", handle, four, after)
    n = len(names)

    def plan(ins, out_refs, x, y, c):
        me, sib = 4 * x + 2 * y + c, (x, y, 1 - c)
        chips = [(1 - x, y), (x, 1 - y), (1 - x, 1 - y)]
        recs = []
        for k, (axis, size) in enumerate(sizes):
            block, blocks = _block_view(axis, size), _blocks_view(axis, size)
            land = out_refs[k]
            remote = []
            for px, py in chips:
                landed = block(land, 4 * px + 2 * py + c)
                remote.append((sib, landed, landed))
            recs.append(dict(local=[(ins[n + k].at[layer], block(land, me))], remote=remote,
                             local_wait=blocks(land, 1), send_wait=blocks(land, 3), recv_wait=blocks(land, 3)))
        return recs

    res = _comm_call("gather_pair", lands + handle["srcs"], [jax.ShapeDtypeStruct(l.shape, l.dtype) for l in lands], n, plan,
                     aliases={k: k for k in range(n)})
    return dict(zip(names, res))


def _pair_add(name, grads, rcv, core):
    nl = len(grads)
    _, kk, nn = grads[0].shape

    def body(c_ref, *refs):
        l = pl.program_id(0)
        own = refs[0][...]
        for j in range(1, nl):
            own = jnp.where(l == j, refs[j][...], own)
        refs[nl + 1][...] = (own.astype(F32) + refs[nl][...].astype(F32)).astype(BF16)

    gspec = lambda j: pl.BlockSpec((None, kk, nn), lambda l, h, c_ref: (jnp.where(l == j, 2 * h + c_ref[0], 0), 0, 0))
    rspec = pl.BlockSpec((None, None, kk, nn), lambda l, h, c_ref: (h, l, 0, 0))
    return _call(
        body, name="pair_add_" + name,
        grid_spec=pltpu.PrefetchScalarGridSpec(num_scalar_prefetch=1, grid=(nl, NCHIP),
                                               in_specs=[gspec(j) for j in range(nl)] + [rspec], out_specs=rspec),
        out_shape=jax.ShapeDtypeStruct(rcv.shape, BF16),
        compiler_params=_cparams(("arbitrary", "arbitrary")))(core, *grads, rcv)


def _pair_add_small(owned, lists, core):
    on, ln = list(owned), list(lists)
    flat = []
    for n in on:
        flat += list(owned[n][0]) + [owned[n][1]]
    for n in ln:
        flat += list(lists[n][0]) + [lists[n][1]]

    def body(c_ref, *refs):
        outs = refs[len(flat):]
        c = c_ref[0]
        pos = 0
        for k, n in enumerate(on):
            nl = len(owned[n][0])
            for h in range(NCHIP):
                for l in range(nl):
                    outs[k][h, l] = refs[pos + l][pl.ds(2 * h + c, 1)][0] + refs[pos + nl][h, l]
            pos += nl + 1
        for k, n in enumerate(ln):
            nl = len(lists[n][0])
            for l in range(nl):
                outs[len(on) + k][l] = refs[pos + l][...] + refs[pos + nl][l]
            pos += nl + 1

    shapes = [jax.ShapeDtypeStruct(owned[n][1].shape, F32) for n in on]
    shapes += [jax.ShapeDtypeStruct(lists[n][1].shape, F32) for n in ln]
    res = _call(body, name="pair_add_small", out_shape=shapes,
                in_specs=[pl.BlockSpec(memory_space=pltpu.SMEM)] + [pl.BlockSpec(memory_space=pltpu.VMEM)] * len(flat),
                compiler_params=_cparams())(core, *flat)
    return dict(zip(on + ln, res))


def _pair_reduce(tag, big, by_owner, small, core):
    rs = {**big, **by_owner}
    srcs, outs, plans, rcv_at = [], [], [], {}
    for name, arrays in rs.items():
        rcv_at[name] = len(outs)
        outs.append(jax.ShapeDtypeStruct((NCHIP, len(arrays)) + arrays[0].shape[1:], arrays[0].dtype))
        for l, arr in enumerate(arrays):
            srcs.append(arr)
            plans.append((len(srcs) - 1, rcv_at[name], l, True))
    for name, arrays in small.items():
        rcv_at[name] = len(outs)
        outs.append(jax.ShapeDtypeStruct((len(arrays),) + arrays[0].shape, F32))
        for l, arr in enumerate(arrays):
            srcs.append(arr)
            plans.append((len(srcs) - 1, rcv_at[name], l, False))

    def plan_pair(ins, out_refs, x, y, c):
        sib = (x, y, 1 - c)
        recs = []
        for si, ro, l, slabs in plans:
            if slabs:
                four = out_refs[ro].at[pl.ds(0, NCHIP), l]
                recs.append(dict(remote=[(sib, ins[si].at[2 * h + 1 - c], out_refs[ro].at[h, l]) for h in range(NCHIP)],
                                 send_wait=four, recv_wait=four))
            else:
                dst = out_refs[ro].at[l]
                recs.append(dict(remote=[(sib, ins[si], dst)], send_wait=dst, recv_wait=dst))
        return recs

    res = _comm_call("pair_exchange_" + tag, srcs, outs, len(plans), plan_pair)
    part = {name: _pair_add(name, big[name], res[rcv_at[name]], core) for name in big}
    if by_owner or small:
        part.update(_pair_add_small({n: (by_owner[n], res[rcv_at[n]]) for n in by_owner},
                                    {n: (small[n], res[rcv_at[n]]) for n in small}, core))
    return part


def _chip_copies(src, land, slabbed, x, y, c):
    mine = 2 * x + y
    copies = []
    for step in range(1, NCHIP):
        h = (mine + step) % NCHIP
        copies.append(((h // 2, h % 2, c), src.at[h] if slabbed else src, land.at[mine]))
    return copies


def _chip_exchange(part, slabbed, keep_own):
    names = list(part)
    outs = [jax.ShapeDtypeStruct((() if n in slabbed else (NCHIP,)) + part[n].shape, part[n].dtype) for n in names]

    def plan(ins, out_refs, x, y, c):
        mine = 2 * x + y
        recs = []
        for k, n in enumerate(names):
            three = out_refs[k].at[pl.ds(0, NCHIP - 1)]
            rec = dict(remote=_chip_copies(ins[k], out_refs[k], n in slabbed, x, y, c), send_wait=three, recv_wait=three)
            if n in keep_own:
                rec["local"] = [(ins[k].at[mine] if n in slabbed else ins[k], out_refs[k].at[mine])]
                rec["local_wait"] = out_refs[k].at[0]
            recs.append(rec)
        return recs

    res = _comm_call("chip_exchange", [part[n] for n in names], outs, len(names), plan)
    return dict(zip(names, res))


HBM_SPEC = pl.BlockSpec(memory_space=pltpu.HBM)
SEM_SPEC = pl.BlockSpec(memory_space=pltpu.SEMAPHORE)
SPLIT_EFFECT = pltpu.SideEffectType.DATAFLOW_SIDE_EFFECTING


def _split_start(name, srcs, land_shapes, copies_fn, after):
    n = len(srcs)
    lands = [pltpu.with_memory_space_constraint(lax.empty(s.shape, s.dtype), pltpu.HBM) for s in land_shapes]

    def body(*refs):
        src_refs, land_refs = refs[:n], refs[n:2 * n]
        send_sem, recv_sem = refs[2 * n + 1], refs[2 * n + 2]
        token = refs[-1]
        x, y, c = _coords()
        for k, copies in enumerate(copies_fn(src_refs, land_refs, x, y, c)):
            for peer, src, dst in copies:
                _remote(src, dst, send_sem.at[k], recv_sem.at[k], peer).start()
        token[...] = jnp.zeros_like(token)

    res = pl.pallas_call(
        body, name=name,
        out_shape=(pltpu.SemaphoreType.DMA((n,)), pltpu.SemaphoreType.DMA((n,)),
                   *[pltpu.HBM(s.shape, s.dtype) for s in srcs], *[pltpu.HBM(s.shape, s.dtype) for s in land_shapes],
                   jax.ShapeDtypeStruct((8, 128), F32)),
        in_specs=[HBM_SPEC] * (2 * n) + [ANY],
        out_specs=(SEM_SPEC, SEM_SPEC, *[HBM_SPEC] * (2 * n), pl.BlockSpec(memory_space=pltpu.VMEM)),
        input_output_aliases={i: 2 + i for i in range(2 * n)},
        compiler_params=pltpu.CompilerParams(has_side_effects=SPLIT_EFFECT),
    )(*[pltpu.with_memory_space_constraint(s, pltpu.HBM) for s in srcs], *lands, after)
    return dict(send=res[0], recv=res[1], srcs=list(res[2:2 + n]), lands=list(res[2 + n:2 + 2 * n]), token=res[-1])


def _split_wait(name, handle, wait_views, after):
    n = len(handle["srcs"])

    def body(*refs):
        land_refs = refs[n:2 * n]
        send_sem, recv_sem = refs[2 * n], refs[2 * n + 1]
        x, y, c = _coords()
        for k in range(n):
            w = wait_views[k](land_refs[k])
            cp = _remote(w, w, send_sem.at[k], recv_sem.at[k], (x, y, c))
            cp.wait_send()
            cp.wait_recv()

    res = pl.pallas_call(
        body, name=name,
        out_shape=(*[pltpu.HBM(s.shape, s.dtype) for s in handle["srcs"]],
                   *[pltpu.HBM(s.shape, s.dtype) for s in handle["lands"]]),
        in_specs=[HBM_SPEC] * (2 * n) + [SEM_SPEC, SEM_SPEC, ANY], out_specs=tuple([HBM_SPEC] * (2 * n)),
        input_output_aliases={i: i for i in range(2 * n)},
        compiler_params=pltpu.CompilerParams(has_side_effects=SPLIT_EFFECT),
    )(*handle["srcs"], *handle["lands"], handle["send"], handle["recv"], after)
    return list(res[n:])


def _adamw(w, g, m, v):
    m = ADAM_B1 * m + (1.0 - ADAM_B1) * g
    v = ADAM_B2 * v + (1.0 - ADAM_B2) * jnp.square(g)
    m_hat = m / (1.0 - ADAM_B1 ** ADAM_STEP)
    v_hat = v / (1.0 - ADAM_B2 ** ADAM_STEP)
    delta = -ADAM_LR * (m_hat / (jnp.sqrt(v_hat) + ADAM_EPS) + ADAM_WD * w)
    return delta, m, v


def _chip_start(part):
    names = list(part)

    def copies(src_refs, land_refs, x, y, c):
        return [_chip_copies(src_refs[k], land_refs[k], True, x, y, c) for k in range(len(names))]

    shapes = [jax.ShapeDtypeStruct(part[n].shape, part[n].dtype) for n in names]
    return names, _split_start("chip_start", [part[n] for n in names], shapes, copies, part[names[0]])


def _chip_wait(names, handle, after):
    three = [lambda ref: ref.at[pl.ds(0, NCHIP - 1)]] * len(names)
    return dict(zip(names, _split_wait("chip_wait", handle, three, after)))


def _sum_senders(ref):
    g = ref[0].astype(F32)
    for h in range(1, NCHIP):
        g = g + ref[h].astype(F32)
    return g


def _adam_big(name, own, recv, w, m, v, tk, chip):
    nl = len(own)
    kk, nn = w.shape[1], w.shape[2]
    nnp = own[0].shape[3]

    def body(chip_ref, *refs):
        l = pl.program_id(0)
        g = None
        for step in range(NCHIP):
            val = refs[step][...]
            for q in range(1, nl):
                val = jnp.where(l == q, refs[NCHIP * q + step][...], val)
            g = val.astype(F32) if g is None else g + val.astype(F32)
        w_ref, m_ref, v_ref, g_ref, d_ref, mo_ref, vo_ref = refs[NCHIP * nl:]
        g = g[:, :nn]
        delta, m2, v2 = _adamw(w_ref[...], g, m_ref[...], v_ref[...])
        g_ref[...] = g
        d_ref[...] = delta
        mo_ref[...] = m2
        vo_ref[...] = v2

    def slab(q, step):
        return pl.BlockSpec((None, None, tk, nnp), lambda l, i, chip_ref: (
            jnp.where(l == q, (chip_ref[0] + step) % NCHIP, 0), 0, jnp.where(l == q, i, 0), 0))

    in_specs, operands = [], []
    for q in range(nl):
        in_specs += [slab(q, step) for step in range(NCHIP)]
        operands += [own[q]] + [recv[q]] * (NCHIP - 1)
    wspec = pl.BlockSpec((None, tk, nn), lambda l, i, chip_ref: (l, i, 0))
    shape = jax.ShapeDtypeStruct(w.shape, F32)
    return _call(
        body, name="adamw_" + name,
        grid_spec=pltpu.PrefetchScalarGridSpec(num_scalar_prefetch=1, grid=(nl, kk // tk),
                                               in_specs=in_specs + [wspec] * 3, out_specs=[wspec] * 4),
        out_shape=[shape] * 4, compiler_params=_cparams(("arbitrary", "arbitrary")))(chip, *operands, w, m, v)


def _adam_small(names, recv, w, m, v):
    n = len(names)

    def body(*refs):
        r, ww, mm, vv = refs[:n], refs[n:2 * n], refs[2 * n:3 * n], refs[3 * n:4 * n]
        outs = refs[4 * n:]
        for k in range(n):
            g = _sum_senders(r[k])
            if g.shape != ww[k].shape:
                g = g[:, :ww[k].shape[1]]
            delta, m2, v2 = _adamw(ww[k][...], g, mm[k][...], vv[k][...])
            outs[k][...] = g
            outs[n + k][...] = delta
            outs[2 * n + k][...] = m2
            outs[3 * n + k][...] = v2

    shapes = [jax.ShapeDtypeStruct(w[k].shape, F32) for k in names]
    res = _call(body, name="adamw_small", out_shape=shapes * 4, compiler_params=_cparams())(
        *[recv[k] for k in names], *[w[k] for k in names], *[m[k] for k in names], *[v[k] for k in names])
    return {k: (res[i], res[n + i], res[2 * n + i], res[3 * n + i]) for i, k in enumerate(names)}


def _expand_b(b):
    bt = jnp.transpose(b, (0, 2, 1)).reshape(NBLK, GB, SGRP, NSTATE)
    eye = jnp.eye(GB, dtype=b.dtype)
    return jnp.einsum("jgpn,gh->jgphn", bt, eye).reshape(NBLK, GB * SGRP, NS)


def _extract_b(db):
    x = db.reshape(NBLK, GB, SGRP, GB, NSTATE)
    eye = jnp.eye(GB, dtype=db.dtype)
    d = jnp.einsum("jgphn,gh->jgpn", x, eye).reshape(NGRP, SGRP, NSTATE)
    return jnp.transpose(d, (0, 2, 1)).reshape(NGRP, NSTATE * SGRP)


def _expand_c(c):
    ct = jnp.transpose(c, (0, 2, 1)).reshape(NBLK, GB, NSTATE, SGRP)
    eye = jnp.eye(GB, dtype=c.dtype)
    return jnp.einsum("jgnp,gh->jgnhp", ct, eye).reshape(NBLK, NS, GB * SGRP)


def _extract_c(dc):
    x = dc.reshape(NBLK, GB, NSTATE, GB, SGRP)
    eye = jnp.eye(GB, dtype=dc.dtype)
    d = jnp.einsum("jgnhp,gh->jgnp", x, eye).reshape(NGRP, NSTATE, SGRP)
    return jnp.transpose(d, (0, 2, 1))


_SMALL = ("norm1", "b_gate", "ssm_a_re", "ssm_a_im", "ssm_log_dt", "ssm_b_re", "ssm_b_im", "ssm_c_re", "ssm_c_im",
          "ssm_d", "ssm_b_glu", "conv_b_dw", "conv_ln_g", "conv_ln_b", "pool_w_group", "pool_scale", "norm2")
_ADAM_TK = {"w_in": 256, "ssm_w_glu": 64, "ssm_w_proj": 512, "conv_w_proj": 512, "pool_w_proj": 512, "w_out": 128,
            "ffn_w_gate": 256, "ffn_w_up": 256, "ffn_w_down": HSH}
_OUT_ORDER = ("norm1", "w_in", "b_gate", "ssm_a_re", "ssm_a_im", "ssm_log_dt", "ssm_b_re", "ssm_b_im", "ssm_c_re",
              "ssm_c_im", "ssm_d", "ssm_w_glu", "ssm_b_glu", "ssm_w_proj", "conv_w_dw", "conv_b_dw", "conv_ln_g",
              "conv_ln_b", "conv_w_proj", "pool_w_group", "pool_scale", "pool_w_proj", "w_out", "norm2", "ffn_w_gate",
              "ffn_w_up", "ffn_w_down", "final_norm")


def _layer_fwd(x, p, token):
    z = _inproj_fwd(x, p["norm1"], p["w_in"], token)
    yssm, hre, him = _ssm_fwd(z, p)
    cv = _conv_fwd(z, p["conv_w"], p["conv_b"])
    pbar = _pool_fwd(z)
    x1 = _merge_fwd(x, yssm, cv, pbar, z, p)
    x2, gpre, upre = _ffn_fwd(x1, p["norm2"], p["wg"], p["wu"], p["wd"])
    return x2, dict(x=x, z=z, yssm=yssm, hre=hre, him=him, cv=cv, pbar=pbar, x1=x1, gpre=gpre, upre=upre)


def _layer_bwd(dx, p, s, token):
    big, small = {}, {}
    dx1, d_norm2, dgp, dup, act, h2 = _ffn_bwd(dx, s["x1"], p["norm2"], s["gpre"], s["upre"], p["wg"], p["wu"], p["wd"],
                                               token)
    big["ffn_w_gate"] = _matmul_tn(h2, dgp, "tn_gate", HPAD)
    big["ffn_w_up"] = _matmul_tn(h2, dup, "tn_up", HPAD)
    big["ffn_w_down"] = _matmul_tn(act, dx, "tn_down").reshape(NDEV, HPAD, D)
    (dy, dcv, dpb, dzg, a_g, a_outa, a_hs, a_pb, a_pc, a_mg, c_glu, c_ya, c_yb, c_p, c_yc,
     d_bglu, d_lng, d_lnb, d_scale, d_bgate) = _merge_bwd(dx1, s["x"], s["yssm"], s["cv"], s["pbar"], s["z"], p)
    big["ssm_w_glu"] = _matmul_tn(a_g, c_glu, "tn_glu").reshape(NDEV, BW // NDEV, BW)
    big["ssm_w_proj"] = _matmul_tn(a_outa, c_ya, "tn_ssm_proj", D // NDEV)
    big["conv_w_proj"] = _matmul_tn(a_hs, c_yb, "tn_conv_proj", D // NDEV)
    big["pool_w_proj"] = _matmul_tn(a_pc, c_yc, "tn_pool_proj", D // NDEV)
    big["w_out"] = _matmul_tn(a_mg, dx1, "tn_out").reshape(NDEV, D // NDEV, D)
    d_wgrp = _group_tn(a_pb, c_p)
    du_a, dbr, dbi, dcr, dci, dd, dar, dai, dldt = _ssm_bwd(dy, s["z"], s["hre"], s["him"], p)
    dva, dvb, dw8, dcb = _conv_bwd(dcv, s["z"], p["conv_w"])
    du_c = _pool_bwd(dpb)
    dz = jnp.concatenate([du_a, dva, dvb, du_c, dzg], axis=1)
    dx0, d_norm1, h = _inproj_bwd(dz, dx1, s["x"], p["norm1"], p["w_in"])
    big["w_in"] = _matmul_tn(h, dz, "tn_in", IN_W // NDEV)
    small["norm1"] = d_norm1
    small["b_gate"] = d_bgate
    small["ssm_a_re"] = dar.reshape(NGRP, NSTATE)
    small["ssm_a_im"] = dai.reshape(NGRP, NSTATE)
    small["ssm_log_dt"] = dldt.reshape(NBLK, 8, 128)[:, 0, :GB].reshape(1, NGRP)
    small["ssm_b_re"] = _extract_b(dbr)
    small["ssm_b_im"] = _extract_b(dbi)
    small["ssm_c_re"] = _extract_c(dcr)
    small["ssm_c_im"] = _extract_c(dci)
    small["ssm_d"] = dd.reshape(NGRP, SGRP)
    small["ssm_b_glu"] = d_bglu
    small["conv_b_dw"] = dcb
    small["conv_ln_g"] = d_lng
    small["conv_ln_b"] = d_lnb
    small["pool_w_group"] = d_wgrp
    small["pool_scale"] = d_scale
    small["norm2"] = d_norm2
    return dx0, big, dw8, small


def _train_step(a):
    t_rows = a["x"].shape[1]
    x0 = a["x"].reshape(t_rows, D)
    target = a["loss_target"].reshape(t_rows, D)

    pad_cols = lambda w: jnp.pad(w, ((0, 0), (0, 0), (0, HPAD - HSH)))
    shards = {
        "w_in": a["w_in"], "ssm_w_glu": a["ssm_w_glu"], "ssm_w_proj": a["ssm_w_proj"],
        "conv_w_proj": a["conv_w_proj"], "pool_w_proj": a["pool_w_proj"], "w_out": a["w_out"],
        "ffn_w_gate": pad_cols(a["ffn_w_gate"]), "ffn_w_up": pad_cols(a["ffn_w_up"]),
        "ffn_w_down": jnp.pad(a["ffn_w_down"], ((0, 0), (0, HPAD - HSH), (0, 0))),
    }
    shards = {k: v.astype(BF16) for k, v in shards.items()}
    full, dw_all = _gather_weights(shards, a["conv_w_dw"].reshape(DEPTH, CONV_K, BW // NDEV), (0,))
    conv_w = jnp.transpose(dw_all, (1, 2, 0, 3)).reshape(DEPTH, CONV_K, BW)
    late = _gather_start(shards, DEPTH - 1, dw_all)
    core = lax.axis_index("c").astype(jnp.int32).reshape(1)
    chip = (2 * lax.axis_index("x") + lax.axis_index("y")).astype(jnp.int32).reshape(1)
    no_token = jnp.zeros((8, 128), F32)

    def layer_params(l):
        row = lambda v: v.reshape(1, -1)
        return dict(
            norm1=row(a["norm1"][l]), w_in=full["w_in"][l],
            are=row(a["ssm_a_re"][l]), aim=row(a["ssm_a_im"][l]),
            ldt=row(jnp.repeat(a["ssm_log_dt"][l], NSTATE)),
            bexp_re=_expand_b(a["ssm_b_re"][l]), bexp_im=_expand_b(a["ssm_b_im"][l]),
            cexp_re=_expand_c(a["ssm_c_re"][l]), cexp_im=_expand_c(a["ssm_c_im"][l]),
            dskip=row(a["ssm_d"][l]),
            conv_w=conv_w[l], conv_b=row(a["conv_b_dw"][l]),
            wglu=full["ssm_w_glu"][l], bglu=row(a["ssm_b_glu"][l]), wpa=full["ssm_w_proj"][l],
            lng=row(a["conv_ln_g"][l]), lnb=row(a["conv_ln_b"][l]), wpb=full["conv_w_proj"][l],
            wgrp=a["pool_w_group"][l].astype(BF16), scale=row(a["pool_scale"][l]), wpc=full["pool_w_proj"][l],
            bgate=row(a["b_gate"][l]), wout=full["w_out"][l],
            norm2=row(a["norm2"][l]), wg=full["ffn_w_gate"][l], wu=full["ffn_w_up"][l], wd=full["ffn_w_down"][l],
        )

    p0 = layer_params(0)
    x, s0 = _layer_fwd(x0, p0, late["token"])
    for name, w in _gather_finish(late, DEPTH - 1, x).items():
        full[name][DEPTH - 1] = w
    params = [p0, layer_params(DEPTH - 1)]
    x, s1 = _layer_fwd(x, params[1], no_token)
    saved = [s0, s1]

    loss_part, dx, d_final = _loss_head(x, a["final_norm"].reshape(1, D), target)
    loss = lax.psum(loss_part[0, 0], ("x", "y", "c"))

    dx, gb1, go1, gs1 = _layer_bwd(dx, params[1], saved[1], no_token)
    part1 = _pair_reduce("late", {n: [g] for n, g in gb1.items()}, {}, {}, core)
    rs_names, rs_handle = _chip_start(part1)
    dx, gb0, go0, gs0 = _layer_bwd(dx, params[0], saved[0], rs_handle["token"])
    grad_x = dx.reshape(1, t_rows, D)
    small = {n: [gs0[n], gs1[n]] for n in _SMALL}
    small["final_norm"] = [d_final]
    part0 = _pair_reduce("rest", {n: [g] for n, g in gb0.items()}, {"conv_w_dw": [go0, go1]}, small, core)
    recv = _chip_exchange(part0, set(_BIG) | {"conv_w_dw"}, {"conv_w_dw"} | set(small))
    recv1 = _chip_wait(rs_names, rs_handle, dx)

    results = {}
    for name in _BIG:
        results[name] = _adam_big(name, [part0[name], part1[name]], [recv[name], recv1[name]], a[name], a["m_" + name],
                                  a["v_" + name], _ADAM_TK[name], chip)

    lay = {
        "norm1": lambda v: v.reshape(DEPTH, 1, D), "b_gate": lambda v: v.reshape(DEPTH, 1, 3 * D),
        "ssm_log_dt": lambda v: v.reshape(DEPTH, 1, NGRP),
        "ssm_b_re": lambda v: v.reshape(DEPTH, NGRP, NSTATE * SGRP), "ssm_b_im": lambda v: v.reshape(DEPTH, NGRP, NSTATE * SGRP),
        "ssm_b_glu": lambda v: v.reshape(DEPTH, 1, BW), "conv_b_dw": lambda v: v.reshape(DEPTH, 1, BW),
        "conv_ln_g": lambda v: v.reshape(DEPTH, 1, BW), "conv_ln_b": lambda v: v.reshape(DEPTH, 1, BW),
        "pool_scale": lambda v: v.reshape(DEPTH, 1, BW), "norm2": lambda v: v.reshape(DEPTH, 1, D),
        "conv_w_dw": lambda v: v.reshape(DEPTH, CONV_K, BW // NDEV), "final_norm": lambda v: v.reshape(1, 1, D),
    }
    names = _SMALL + ("conv_w_dw", "final_norm")
    relay = lambda k, v: lay[k](v) if k in lay else v
    sm = _adam_small(names, recv, {k: relay(k, a[k]) for k in names}, {k: relay(k, a["m_" + k]) for k in names},
                     {k: relay(k, a["v_" + k]) for k in names})
    for k in names:
        results[k] = tuple(r.reshape(a[k].shape) for r in sm[k])

    outs = [loss, grad_x]
    for part in range(4):
        outs += [results[k][part] for k in _OUT_ORDER]
    return tuple(outs)


def kernel(x, norm1, w_in, b_gate, ssm_a_re, ssm_a_im, ssm_log_dt, ssm_b_re, ssm_b_im, ssm_c_re, ssm_c_im, ssm_d, ssm_w_glu, ssm_b_glu, ssm_w_proj, conv_w_dw, conv_b_dw, conv_ln_g, conv_ln_b, conv_w_proj, pool_w_group, pool_scale, pool_w_proj, w_out, norm2, ffn_w_gate, ffn_w_up, ffn_w_down, final_norm, loss_target, m_norm1, m_w_in, m_b_gate, m_ssm_a_re, m_ssm_a_im, m_ssm_log_dt, m_ssm_b_re, m_ssm_b_im, m_ssm_c_re, m_ssm_c_im, m_ssm_d, m_ssm_w_glu, m_ssm_b_glu, m_ssm_w_proj, m_conv_w_dw, m_conv_b_dw, m_conv_ln_g, m_conv_ln_b, m_conv_w_proj, m_pool_w_group, m_pool_scale, m_pool_w_proj, m_w_out, m_norm2, m_ffn_w_gate, m_ffn_w_up, m_ffn_w_down, m_final_norm, v_norm1, v_w_in, v_b_gate, v_ssm_a_re, v_ssm_a_im, v_ssm_log_dt, v_ssm_b_re, v_ssm_b_im, v_ssm_c_re, v_ssm_c_im, v_ssm_d, v_ssm_w_glu, v_ssm_b_glu, v_ssm_w_proj, v_conv_w_dw, v_conv_b_dw, v_conv_ln_g, v_conv_ln_b, v_conv_w_proj, v_pool_w_group, v_pool_scale, v_pool_w_proj, v_w_out, v_norm2, v_ffn_w_gate, v_ffn_w_up, v_ffn_w_down, v_final_norm):
    return _train_step(dict(locals()))
```

```python
import functools

import jax
import jax.numpy as jnp
from jax import lax
from jax.experimental import pallas as pl
from jax.experimental.pallas import tpu as pltpu

F32 = jnp.float32
BF16 = jnp.bfloat16

NDEV = 8
DEPTH = 2
D = 1024
BW = 512
NSTATE = 64
SGRP = 16
NGRP = BW // SGRP
GB = 8
NBLK = NGRP // GB
NS = GB * NSTATE
CONV_K = 31
HALO = 32
PHALO = 16
IN_W = 5120
HID = 2816
HSH = HID // NDEV
HPAD = 384
HIDP = HPAD * NDEV
EPS = 1e-6
VMEM_LIMIT = 56 * 1024 * 1024

ADAM_LR, ADAM_B1, ADAM_B2, ADAM_EPS, ADAM_WD, ADAM_STEP = 0.001, 0.9, 0.999, 1e-08, 0.01, 10

MESH = pl.DeviceIdType.MESH
ANY = pl.BlockSpec(memory_space=pl.ANY)


def _call(body, **kw):
    return pl.pallas_call(body, **kw)


def _cparams(sem=None):
    return pltpu.CompilerParams(dimension_semantics=sem, vmem_limit_bytes=VMEM_LIMIT)


def _dot(a, b):
    return jnp.dot(a.astype(BF16), b.astype(BF16), preferred_element_type=F32)


def _dot_nt(a, b):
    return lax.dot_general(a.astype(BF16), b.astype(BF16), (((1,), (1,)), ((), ())), preferred_element_type=F32)


def _dot_tn(a, b):
    return lax.dot_general(a.astype(BF16), b.astype(BF16), (((0,), (0,)), ((), ())), preferred_element_type=F32)


@jax.custom_vjp
def _mm(a, w):
    return _dot(a, w)


def _mm_fwd(a, w):
    return _dot(a, w), w


def _mm_bwd(w, ct):
    return _dot_nt(ct, w), jnp.zeros_like(w)


_mm.defvjp(_mm_fwd, _mm_bwd)


def _rms(x, g):
    return x * lax.rsqrt(jnp.mean(x * x, axis=-1, keepdims=True) + EPS) * g


def _disc(are, aim, ldt):
    dt = jnp.exp(ldt)
    mag = jnp.exp(dt * are)
    ang = dt * aim
    abr = mag * jnp.cos(ang)
    abi = mag * jnp.sin(ang)
    den = are * are + aim * aim
    nr = abr - 1.0
    fr = (nr * are + abi * aim) / den
    fi = (abi * are - nr * aim) / den
    return abr, abi, fr, fi


def _bbar(fr, fi, br, bi):
    return fr * br - fi * bi, fr * bi + fi * br


def _cmul(ar, ai, br, bi):
    return ar * br - ai * bi, ar * bi + ai * br


def _scan_rows(re_ref, im_ref, ar, ai, n_rows, reverse, hre_ref=None, him_ref=None):
    n = ar.shape[1]
    shape = (8, n)
    rows = lax.broadcasted_iota(jnp.int32, shape, 0)
    a1 = (jnp.broadcast_to(ar, shape), jnp.broadcast_to(ai, shape))
    a2 = _cmul(*a1, *a1)
    a4 = _cmul(*a2, *a2)
    pr = jnp.zeros(shape, F32)
    pi = jnp.zeros(shape, F32)
    pw = a1
    for k in range(8):
        sel = rows == ((7 - k) if reverse else k)
        pr = jnp.where(sel, pw[0], pr)
        pi = jnp.where(sel, pw[1], pi)
        pw = _cmul(*pw, *a1)
    nt = n_rows // 8
    with_acc = hre_ref is not None

    def body(i, carry):
        cr, ci = carry[0], carry[1]
        t = (nt - 1 - i) if reverse else i
        off = pl.multiple_of(t * 8, 8)
        xr = re_ref[pl.ds(off, 8), :]
        xi = im_ref[pl.ds(off, 8), :]
        for k, (kr, ki) in ((1, a1), (2, a2), (4, a4)):
            if reverse:
                keep, sh = rows < 8 - k, 8 - k
            else:
                keep, sh = rows >= k, k
            sr = jnp.where(keep, pltpu.roll(xr, sh, 0), 0.0)
            si = jnp.where(keep, pltpu.roll(xi, sh, 0), 0.0)
            xr, xi = xr + kr * sr - ki * si, xi + kr * si + ki * sr
        xr, xi = xr + pr * cr - pi * ci, xi + pr * ci + pi * cr
        re_ref[pl.ds(off, 8), :] = xr
        im_ref[pl.ds(off, 8), :] = xi
        edge = 0 if reverse else 7
        out = (jnp.broadcast_to(xr[edge:edge + 1, :], shape), jnp.broadcast_to(xi[edge:edge + 1, :], shape))
        if with_acc:
            hr = hre_ref[pl.ds(off, 8), :]
            hi = him_ref[pl.ds(off, 8), :]
            offp = pl.multiple_of(jnp.maximum(t - 1, 0) * 8, 8)
            live = jnp.where(t > 0, 1.0, 0.0)
            lr = jnp.broadcast_to(hre_ref[pl.ds(offp, 8), :][7:8, :], shape) * live
            li = jnp.broadcast_to(him_ref[pl.ds(offp, 8), :][7:8, :], shape) * live
            hpr = jnp.where(rows == 0, lr, pltpu.roll(hr, 1, 0))
            hpi = jnp.where(rows == 0, li, pltpu.roll(hi, 1, 0))
            out = out + (carry[2] + xr * hpr + xi * hpi, carry[3] + xi * hpr - xr * hpi)
        return out

    zero = jnp.zeros(shape, F32)
    init = (zero, zero, zero, zero) if with_acc else (zero, zero)
    res = lax.fori_loop(0, nt, body, init)
    return res[2:] if with_acc else None


TOKEN_SPEC = pl.BlockSpec((8, 128), lambda i, j: (0, 0))


def _inproj_fwd(x, gamma, w, token, tm=512, tn=1280):
    t_rows = x.shape[0]
    n = w.shape[1]

    def body(x_ref, g_ref, w_ref, token_ref, z_ref, h_ref):
        @pl.when(pl.program_id(1) == 0)
        def _():
            h_ref[...] = _rms(x_ref[...], g_ref[...]).astype(BF16)
        z_ref[...] = jnp.dot(h_ref[...], w_ref[...], preferred_element_type=F32)

    return _call(
        body, name="inproj_fwd", grid=(t_rows // tm, n // tn),
        in_specs=[pl.BlockSpec((tm, D), lambda i, j: (i, 0)), pl.BlockSpec((1, D), lambda i, j: (0, 0)),
                  pl.BlockSpec((D, tn), lambda i, j: (0, j)), TOKEN_SPEC],
        out_specs=pl.BlockSpec((tm, tn), lambda i, j: (i, j)),
        out_shape=jax.ShapeDtypeStruct((t_rows, n), F32),
        scratch_shapes=[pltpu.VMEM((tm, D), BF16)],
        compiler_params=_cparams(("parallel", "arbitrary")))(x, gamma, w, token)


def _ssm_specs(t_rows):
    row = pl.BlockSpec((1, NS), lambda j: (0, j))
    return dict(
        u=pl.BlockSpec((t_rows, GB * SGRP), lambda j: (0, j)),
        row=row,
        bexp=pl.BlockSpec((None, GB * SGRP, NS), lambda j: (j, 0, 0)),
        cexp=pl.BlockSpec((None, NS, GB * SGRP), lambda j: (j, 0, 0)),
        d=pl.BlockSpec((1, GB * SGRP), lambda j: (0, j)),
        h=pl.BlockSpec((t_rows, NS), lambda j: (0, j)),
    )


def _ssm_fwd(z, p):
    t_rows = z.shape[0]
    s = _ssm_specs(t_rows)

    def body(u_ref, are_ref, aim_ref, ldt_ref, br_ref, bi_ref, cr_ref, ci_ref, d_ref, y_ref, hr_ref, hi_ref):
        abr, abi, fr, fi = _disc(are_ref[...], aim_ref[...], ldt_ref[...])
        bbr, bbi = _bbar(fr, fi, br_ref[...], bi_ref[...])
        u = u_ref[...]
        hr_ref[...] = _dot(u, bbr)
        hi_ref[...] = _dot(u, bbi)
        _scan_rows(hr_ref, hi_ref, abr, abi, t_rows, False)
        y_ref[...] = _dot(hr_ref[...], cr_ref[...]) - _dot(hi_ref[...], ci_ref[...]) + d_ref[...] * u

    return _call(
        body, name="ssm_fwd", grid=(NBLK,),
        in_specs=[s["u"], s["row"], s["row"], s["row"], s["bexp"], s["bexp"], s["cexp"], s["cexp"], s["d"]],
        out_specs=[s["u"], s["h"], s["h"]],
        out_shape=[jax.ShapeDtypeStruct((t_rows, BW), F32), jax.ShapeDtypeStruct((t_rows, NGRP * NSTATE), F32),
                   jax.ShapeDtypeStruct((t_rows, NGRP * NSTATE), F32)],
        compiler_params=_cparams(("parallel",)))(
            z, p["are"], p["aim"], p["ldt"], p["bexp_re"], p["bexp_im"], p["cexp_re"], p["cexp_im"], p["dskip"])


def _ssm_bwd(dy, z, hre, him, p):
    t_rows = z.shape[0]
    s = _ssm_specs(t_rows)
    nstates = NGRP * NSTATE

    def body(dy_ref, u_ref, hr_ref, hi_ref, are_ref, aim_ref, ldt_ref, br_ref, bi_ref, cr_ref, ci_ref, d_ref,
             du_ref, dbr_ref, dbi_ref, dcr_ref, dci_ref, dd_ref, dar_ref, dai_ref, dldt_ref, lr_ref, li_ref):
        rows3 = (are_ref[...], aim_ref[...], ldt_ref[...])
        (abr, abi, fr, fi), disc_vjp = jax.vjp(_disc, *rows3)
        (bbr, bbi), bbar_vjp = jax.vjp(_bbar, fr, fi, br_ref[...], bi_ref[...])
        dy = dy_ref[...]
        u = u_ref[...]
        lr_ref[...] = _dot_nt(dy, cr_ref[...])
        li_ref[...] = -_dot_nt(dy, ci_ref[...])
        dcr_ref[...] = _dot_tn(hr_ref[...], dy)
        dci_ref[...] = -_dot_tn(hi_ref[...], dy)
        dd_ref[...] = jnp.sum(dy * u, axis=0, keepdims=True)
        acc_r, acc_i = _scan_rows(lr_ref, li_ref, abr, -abi, t_rows, True, hr_ref, hi_ref)
        dabr = jnp.sum(acc_r, axis=0, keepdims=True)
        dabi = jnp.sum(acc_i, axis=0, keepdims=True)
        lam_r = lr_ref[...]
        lam_i = li_ref[...]
        du = d_ref[...] * dy + _dot_nt(lam_r, bbr) + _dot_nt(lam_i, bbi)
        du_ref[...] = du.astype(BF16)
        dbbr = _dot_tn(u, lam_r)
        dbbi = _dot_tn(u, lam_i)
        dfr, dfi, dbr, dbi = bbar_vjp((dbbr, dbbi))
        dbr_ref[...] = dbr
        dbi_ref[...] = dbi
        dar, dai, dldt = disc_vjp((dabr, dabi, dfr, dfi))
        dar_ref[...] = dar
        dai_ref[...] = dai
        lane_grp = lax.broadcasted_iota(jnp.int32, (NS, 128), 0) // NSTATE
        col = lax.broadcasted_iota(jnp.int32, (NS, 128), 1)
        seg = jnp.where(lane_grp == col, 1.0, 0.0).astype(F32)
        dldt_ref[...] = jnp.dot(jnp.broadcast_to(dldt, (8, NS)), seg, preferred_element_type=F32,
                                precision=lax.Precision.HIGHEST)

    dyspec = pl.BlockSpec((t_rows, GB * SGRP), lambda j: (0, j))
    return _call(
        body, name="ssm_bwd", grid=(NBLK,),
        in_specs=[dyspec, s["u"], s["h"], s["h"], s["row"], s["row"], s["row"], s["bexp"], s["bexp"], s["cexp"],
                  s["cexp"], s["d"]],
        out_specs=[dyspec, s["bexp"], s["bexp"], s["cexp"], s["cexp"], s["d"], s["row"], s["row"],
                   pl.BlockSpec((8, 128), lambda j: (j, 0))],
        out_shape=[jax.ShapeDtypeStruct((t_rows, BW), BF16),
                   jax.ShapeDtypeStruct((NBLK, GB * SGRP, NS), F32), jax.ShapeDtypeStruct((NBLK, GB * SGRP, NS), F32),
                   jax.ShapeDtypeStruct((NBLK, NS, GB * SGRP), F32), jax.ShapeDtypeStruct((NBLK, NS, GB * SGRP), F32),
                   jax.ShapeDtypeStruct((1, BW), F32), jax.ShapeDtypeStruct((1, nstates), F32),
                   jax.ShapeDtypeStruct((1, nstates), F32), jax.ShapeDtypeStruct((NBLK * 8, 128), F32)],
        scratch_shapes=[pltpu.VMEM((t_rows, NS), F32), pltpu.VMEM((t_rows, NS), F32)],
        compiler_params=_cparams(("parallel",)))(
            dy, z, hre, him, p["are"], p["aim"], p["ldt"], p["bexp_re"], p["bexp_im"], p["cexp_re"], p["cexp_im"],
            p["dskip"])


def _conv_fwd(z, w, b, tm=256):
    t_rows = z.shape[0]
    hb = tm // HALO

    def body(va_ref, vb_ref, ha_ref, hb_ref, w_ref, b_ref, o_ref, win_ref):
        live = jnp.where(pl.program_id(0) > 0, 1.0, 0.0)
        win_ref[0:HALO, :] = ha_ref[...] * jax.nn.sigmoid(hb_ref[...]) * live
        win_ref[HALO:HALO + tm, :] = va_ref[...] * jax.nn.sigmoid(vb_ref[...])
        acc = jnp.broadcast_to(b_ref[...], (tm, BW))
        for k in range(CONV_K):
            acc = acc + w_ref[k:k + 1, :] * win_ref[pl.ds(HALO - (CONV_K - 1) + k, tm), :]
        o_ref[...] = acc

    halo = lambda col: pl.BlockSpec((HALO, BW), lambda i: (jnp.maximum(i * hb - 1, 0), col))
    return _call(
        body, name="conv_fwd", grid=(t_rows // tm,),
        in_specs=[pl.BlockSpec((tm, BW), lambda i: (i, 1)), pl.BlockSpec((tm, BW), lambda i: (i, 2)), halo(1), halo(2),
                  pl.BlockSpec((CONV_K, BW), lambda i: (0, 0)), pl.BlockSpec((1, BW), lambda i: (0, 0))],
        out_specs=pl.BlockSpec((tm, BW), lambda i: (i, 0)),
        out_shape=jax.ShapeDtypeStruct((t_rows, BW), F32),
        scratch_shapes=[pltpu.VMEM((HALO + tm, BW), F32)],
        compiler_params=_cparams(("parallel",)))(z, z, z, z, w, b)


def _conv_bwd(dcv, z, w, tm=256):
    t_rows = z.shape[0]
    nt = t_rows // tm
    hb = tm // HALO
    csh = BW // NDEV

    def body(d_ref, dn_ref, va_ref, vb_ref, ha_ref, hb_ref, w_ref, dva_ref, dvb_ref, dw8_ref, db_ref,
             hwin_ref, dwin_ref, dw_ref):
        i = pl.program_id(0)

        @pl.when(i == 0)
        def _():
            dw_ref[...] = jnp.zeros_like(dw_ref)
            db_ref[...] = jnp.zeros_like(db_ref)

        live_prev = jnp.where(i > 0, 1.0, 0.0)
        live_next = jnp.where(i < nt - 1, 1.0, 0.0)
        va = va_ref[...]
        sig = jax.nn.sigmoid(vb_ref[...])
        hwin_ref[0:HALO, :] = ha_ref[...] * jax.nn.sigmoid(hb_ref[...]) * live_prev
        hwin_ref[HALO:HALO + tm, :] = va * sig
        d = d_ref[...]
        dwin_ref[0:tm, :] = d
        dwin_ref[tm:tm + HALO, :] = dn_ref[...] * live_next
        dh = jnp.zeros((tm, BW), F32)
        dws = []
        for k in range(CONV_K):
            dh = dh + w_ref[k:k + 1, :] * dwin_ref[pl.ds(CONV_K - 1 - k, tm), :]
            dws.append(jnp.sum(d * hwin_ref[pl.ds(HALO - (CONV_K - 1) + k, tm), :], axis=0, keepdims=True))
        dws.append(jnp.zeros((1, BW), F32))
        dw_ref[...] += jnp.concatenate(dws, axis=0)
        db_ref[...] += jnp.sum(d, axis=0, keepdims=True)
        dva_ref[...] = (dh * sig).astype(BF16)
        dvb_ref[...] = (dh * va * sig * (1.0 - sig)).astype(BF16)

        @pl.when(i == nt - 1)
        def _():
            acc = dw_ref[...]
            for q in range(NDEV):
                dw8_ref[q] = acc[:, csh * q:csh * (q + 1)]

    halo = lambda col: pl.BlockSpec((HALO, BW), lambda i: (jnp.maximum(i * hb - 1, 0), col))
    return _call(
        body, name="conv_bwd", grid=(nt,),
        in_specs=[pl.BlockSpec((tm, BW), lambda i: (i, 0)),
                  pl.BlockSpec((HALO, BW), lambda i: (jnp.minimum((i + 1) * hb, t_rows // HALO - 1), 0)),
                  pl.BlockSpec((tm, BW), lambda i: (i, 1)), pl.BlockSpec((tm, BW), lambda i: (i, 2)), halo(1), halo(2),
                  pl.BlockSpec((CONV_K, BW), lambda i: (0, 0))],
        out_specs=[pl.BlockSpec((tm, BW), lambda i: (i, 0)), pl.BlockSpec((tm, BW), lambda i: (i, 0)),
                   pl.BlockSpec((NDEV, 32, csh), lambda i: (0, 0, 0)), pl.BlockSpec((1, BW), lambda i: (0, 0))],
        out_shape=[jax.ShapeDtypeStruct((t_rows, BW), BF16), jax.ShapeDtypeStruct((t_rows, BW), BF16),
                   jax.ShapeDtypeStruct((NDEV, 32, csh), F32), jax.ShapeDtypeStruct((1, BW), F32)],
        scratch_shapes=[pltpu.VMEM((HALO + tm, BW), F32), pltpu.VMEM((tm + HALO, BW), F32), pltpu.VMEM((32, BW), F32)],
        compiler_params=_cparams(("arbitrary",)))(dcv, dcv, z, z, z, z, w)


def _pool_rows(i, tm, n_rows, first_row):
    grp = lax.broadcasted_iota(jnp.int32, (1, BW), 1) // (BW // 4)
    wlen = jnp.where(grp == 0, 2.0, jnp.where(grp == 1, 4.0, jnp.where(grp == 2, 8.0, 16.0)))
    t = (i * tm + first_row + lax.broadcasted_iota(jnp.int32, (n_rows, 1), 0)).astype(F32)
    return grp, 1.0 / jnp.minimum(t + 1.0, wlen)


def _pool_pick(grp, s2, s4, s8, s16):
    return jnp.where(grp == 0, s2, jnp.where(grp == 1, s4, jnp.where(grp == 2, s8, s16)))


def _pool_fwd(z, tm=256):
    t_rows = z.shape[0]
    hb = tm // PHALO

    def body(u_ref, h_ref, o_ref):
        i = pl.program_id(0)
        u = u_ref[...]
        win = jnp.concatenate([h_ref[...] * jnp.where(i > 0, 1.0, 0.0), u], axis=0)
        s2 = win + pltpu.roll(win, 1, 0)
        s4 = s2 + pltpu.roll(s2, 2, 0)
        s8 = s4 + pltpu.roll(s4, 4, 0)
        s16 = s8 + pltpu.roll(s8, 8, 0)
        grp, inv = _pool_rows(i, tm, tm, 0)
        o_ref[...] = _pool_pick(grp, s2, s4, s8, s16)[PHALO:, :] * inv - u

    return _call(
        body, name="pool_fwd", grid=(t_rows // tm,),
        in_specs=[pl.BlockSpec((tm, BW), lambda i: (i, 3)),
                  pl.BlockSpec((PHALO, BW), lambda i: (jnp.maximum(i * hb - 1, 0), 3))],
        out_specs=pl.BlockSpec((tm, BW), lambda i: (i, 0)),
        out_shape=jax.ShapeDtypeStruct((t_rows, BW), F32),
        compiler_params=_cparams(("parallel",)))(z, z)


def _pool_bwd(dp, tm=256):
    t_rows = dp.shape[0]
    nt = t_rows // tm
    hb = tm // PHALO
    ln = tm + PHALO

    def body(d_ref, dn_ref, o_ref):
        i = pl.program_id(0)
        d = d_ref[...]
        grp, inv = _pool_rows(i, tm, ln, 0)
        win = jnp.concatenate([d, dn_ref[...] * jnp.where(i < nt - 1, 1.0, 0.0)], axis=0) * inv
        s2 = win + pltpu.roll(win, ln - 1, 0)
        s4 = s2 + pltpu.roll(s2, ln - 2, 0)
        s8 = s4 + pltpu.roll(s4, ln - 4, 0)
        s16 = s8 + pltpu.roll(s8, ln - 8, 0)
        o_ref[...] = (_pool_pick(grp, s2, s4, s8, s16)[:tm, :] - d).astype(BF16)

    return _call(
        body, name="pool_bwd", grid=(nt,),
        in_specs=[pl.BlockSpec((tm, BW), lambda i: (i, 0)),
                  pl.BlockSpec((PHALO, BW), lambda i: (jnp.minimum((i + 1) * hb, t_rows // PHALO - 1), 0))],
        out_specs=pl.BlockSpec((tm, BW), lambda i: (i, 0)),
        out_shape=jax.ShapeDtypeStruct((t_rows, BW), BF16),
        compiler_params=_cparams(("parallel",)))(dp, dp)


_MERGE_W = ("wglu", "bglu", "wpa", "lng", "lnb", "wpb", "wgrp", "scale", "wpc", "bgate", "wout")
_MERGE_SMALL = ("bglu", "lng", "lnb", "scale", "bgate")


def _merge_math(x, yssm, cv, pbar, zg, w, taps):
    t_glu, t_ya, t_yb, t_p, t_yc = taps
    g = jax.nn.gelu(yssm)
    outa = g * jax.nn.sigmoid(_mm(g, w["wglu"]) + t_glu + w["bglu"])
    ya = _mm(outa, w["wpa"]) + t_ya
    mu = jnp.mean(cv, axis=-1, keepdims=True)
    var = jnp.mean(jnp.square(cv - mu), axis=-1, keepdims=True)
    hs = jax.nn.silu((cv - mu) * lax.rsqrt(var + EPS) * w["lng"] + w["lnb"])
    yb = _mm(hs, w["wpb"]) + t_yb
    gw = BW // 4
    pk = jnp.concatenate([_mm(pbar[:, gw * k:gw * (k + 1)], w["wgrp"][k]) for k in range(4)], axis=1) + t_p
    pc = pk * w["scale"]
    yc = _mm(pc, w["wpc"]) + t_yc
    gates = jax.nn.sigmoid(zg + w["bgate"])
    merged = gates[:, :D] * ya + gates[:, D:2 * D] * yb + gates[:, 2 * D:] * yc
    x1 = x + _mm(merged, w["wout"])
    acts = tuple(a.astype(BF16) for a in (g, outa, hs, pbar, pc, merged))
    return x1, acts


def _merge_specs(tm, p):
    rows = lambda width, col=0: pl.BlockSpec((tm, width), lambda i, c=col: (i, c))
    data = [rows(D), rows(BW), rows(BW), rows(BW), rows(D, 2), rows(D, 3), rows(D, 4)]
    wspecs = []
    for name in _MERGE_W:
        nd = p[name].ndim
        wspecs.append(pl.BlockSpec(p[name].shape, lambda i, nd=nd: (0,) * nd))
    return rows, data, wspecs


def _merge_fwd(x, yssm, cv, pbar, z, p, tm=256):
    t_rows = x.shape[0]
    rows, data, wspecs = _merge_specs(tm, p)

    def body(x_ref, y_ref, cv_ref, pb_ref, za_ref, zb_ref, zc_ref, *rest):
        w = {name: r[...] for name, r in zip(_MERGE_W, rest[:len(_MERGE_W)])}
        o_ref = rest[len(_MERGE_W)]
        taps = (0.0, 0.0, 0.0, 0.0, 0.0)
        zg = jnp.concatenate([za_ref[...], zb_ref[...], zc_ref[...]], axis=1)
        o_ref[...] = _merge_math(x_ref[...], y_ref[...], cv_ref[...], pb_ref[...], zg, w, taps)[0]

    return _call(
        body, name="merge_fwd", grid=(t_rows // tm,), in_specs=data + wspecs, out_specs=rows(D),
        out_shape=jax.ShapeDtypeStruct((t_rows, D), F32),
        compiler_params=_cparams(("parallel",)))(x, yssm, cv, pbar, z, z, z, *[p[n] for n in _MERGE_W])


def _merge_bwd(dx1, x, yssm, cv, pbar, z, p, token, tm=256):
    t_rows = x.shape[0]
    rows, data, wspecs = _merge_specs(tm, p)
    nw = len(_MERGE_W)

    def body(dx_ref, x_ref, y_ref, cv_ref, pb_ref, za_ref, zb_ref, zc_ref, *rest):
        w = {name: r[...] for name, r in zip(_MERGE_W, rest[:nw])}
        zg = jnp.concatenate([za_ref[...], zb_ref[...], zc_ref[...]], axis=1)
        outs = rest[nw + 1:]
        small = {n: w[n] for n in _MERGE_SMALL}
        taps = (jnp.zeros((tm, BW), F32), jnp.zeros((tm, D), F32), jnp.zeros((tm, D), F32),
                jnp.zeros((tm, BW), F32), jnp.zeros((tm, D), F32))

        def f(yssm_, cv_, pbar_, zg_, small_, taps_):
            return _merge_math(x_ref[...], yssm_, cv_, pbar_, zg_, {**w, **small_}, taps_)

        _, vjp, acts = jax.vjp(f, y_ref[...], cv_ref[...], pb_ref[...], zg, small, taps, has_aux=True)
        dy, dcv, dpb, dzg, dsmall, dtaps = vjp(dx_ref[...])
        outs[0][...] = dy
        outs[1][...] = dcv
        outs[2][...] = dpb
        outs[3][...] = dzg.astype(BF16)
        for k in range(6):
            outs[4 + k][...] = acts[k]
        for k in range(5):
            outs[10 + k][...] = dtaps[k].astype(BF16)

        @pl.when(pl.program_id(0) == 0)
        def _():
            for k in range(5):
                outs[15 + k][...] = jnp.zeros_like(outs[15 + k])

        for k, n in enumerate(_MERGE_SMALL):
            outs[15 + k][...] += dsmall[n]

    f32o = lambda width: jax.ShapeDtypeStruct((t_rows, width), F32)
    bfo = lambda width: jax.ShapeDtypeStruct((t_rows, width), BF16)
    small_shapes = [jax.ShapeDtypeStruct(p[n].shape, F32) for n in _MERGE_SMALL]
    small_specs = [pl.BlockSpec(p[n].shape, lambda i: (0, 0)) for n in _MERGE_SMALL]
    out_shape = ([f32o(BW), f32o(BW), f32o(BW), bfo(3 * D)]
                 + [bfo(BW), bfo(BW), bfo(BW), bfo(BW), bfo(BW), bfo(D)]
                 + [bfo(BW), bfo(D), bfo(D), bfo(BW), bfo(D)] + small_shapes)
    out_specs = ([rows(BW), rows(BW), rows(BW), rows(3 * D)]
                 + [rows(BW)] * 5 + [rows(D)]
                 + [rows(BW), rows(D), rows(D), rows(BW), rows(D)] + small_specs)
    return _call(
        body, name="merge_bwd", grid=(t_rows // tm,),
        in_specs=[rows(D)] + data + wspecs + [pl.BlockSpec((8, 128), lambda i: (0, 0))], out_specs=out_specs,
        out_shape=out_shape, compiler_params=_cparams(("arbitrary",)))(
            dx1, x, yssm, cv, pbar, z, z, z, *[p[n] for n in _MERGE_W], token)


def _ffn_fwd(x1, gamma, wg, wu, wd, tm=512, th=512):
    t_rows = x1.shape[0]
    nh = HIDP // th

    def body(x_ref, g_ref, wg_ref, wu_ref, wd_ref, o_ref, gp_ref, up_ref, h_ref, acc_ref):
        j = pl.program_id(1)

        @pl.when(j == 0)
        def _():
            h_ref[...] = _rms(x_ref[...], g_ref[...]).astype(BF16)
            acc_ref[...] = jnp.zeros_like(acc_ref)

        gp = jnp.dot(h_ref[...], wg_ref[...], preferred_element_type=F32)
        up = jnp.dot(h_ref[...], wu_ref[...], preferred_element_type=F32)
        gp_ref[...] = gp
        up_ref[...] = up
        acc_ref[...] += _dot(jax.nn.silu(gp) * up, wd_ref[...])

        @pl.when(j == nh - 1)
        def _():
            o_ref[...] = x_ref[...] + acc_ref[...]

    return _call(
        body, name="ffn_fwd", grid=(t_rows // tm, nh),
        in_specs=[pl.BlockSpec((tm, D), lambda i, j: (i, 0)), pl.BlockSpec((1, D), lambda i, j: (0, 0)),
                  pl.BlockSpec((D, th), lambda i, j: (0, j)), pl.BlockSpec((D, th), lambda i, j: (0, j)),
                  pl.BlockSpec((th, D), lambda i, j: (j, 0))],
        out_specs=[pl.BlockSpec((tm, D), lambda i, j: (i, 0)), pl.BlockSpec((tm, th), lambda i, j: (i, j)),
                   pl.BlockSpec((tm, th), lambda i, j: (i, j))],
        out_shape=[jax.ShapeDtypeStruct((t_rows, D), F32), jax.ShapeDtypeStruct((t_rows, HIDP), F32),
                   jax.ShapeDtypeStruct((t_rows, HIDP), F32)],
        scratch_shapes=[pltpu.VMEM((tm, D), BF16), pltpu.VMEM((tm, D), F32)],
        compiler_params=_cparams(("parallel", "arbitrary")))(x1, gamma, wg, wu, wd)


def _rms_bwd_tail(x, gamma, dh):
    _, vjp = jax.vjp(_rms, x, gamma)
    return vjp(dh)


def _ffn_bwd(dx2, x1, gamma, gpre, upre, wg, wu, wd, token, tm=512, th=512):
    t_rows = x1.shape[0]
    nh = HIDP // th

    def body(d_ref, x_ref, g_ref, gp_ref, up_ref, wg_ref, wu_ref, wd_ref, token_ref,
             dx_ref, dgam_ref, dgp_ref, dup_ref, act_ref, h_ref, acc_ref):
        i = pl.program_id(0)
        j = pl.program_id(1)

        @pl.when(j == 0)
        def _():
            acc_ref[...] = jnp.zeros_like(acc_ref)

        @pl.when((i == 0) & (j == 0))
        def _():
            dgam_ref[...] = jnp.zeros_like(dgam_ref)

        dact = _dot_nt(d_ref[...], wd_ref[...])
        gp = gp_ref[...]
        up = up_ref[...]
        sg = jax.nn.sigmoid(gp)
        silu = gp * sg
        dgp = (dact * up * (sg * (1.0 + gp * (1.0 - sg)))).astype(BF16)
        dup = (dact * silu).astype(BF16)
        dgp_ref[...] = dgp
        dup_ref[...] = dup
        act_ref[...] = (silu * up).astype(BF16)
        acc_ref[...] += _dot_nt(dgp, wg_ref[...]) + _dot_nt(dup, wu_ref[...])

        @pl.when(j == nh - 1)
        def _():
            x = x_ref[...]
            h_ref[...] = _rms(x, g_ref[...]).astype(BF16)
            dx, dgam = _rms_bwd_tail(x, g_ref[...], acc_ref[...])
            dx_ref[...] = d_ref[...] + dx
            dgam_ref[...] += dgam

    row_d = pl.BlockSpec((tm, D), lambda i, j: (i, 0))
    row_h = pl.BlockSpec((tm, th), lambda i, j: (i, j))
    return _call(
        body, name="ffn_bwd", grid=(t_rows // tm, nh),
        in_specs=[row_d, row_d, pl.BlockSpec((1, D), lambda i, j: (0, 0)), row_h, row_h,
                  pl.BlockSpec((D, th), lambda i, j: (0, j)), pl.BlockSpec((D, th), lambda i, j: (0, j)),
                  pl.BlockSpec((th, D), lambda i, j: (j, 0)), TOKEN_SPEC],
        out_specs=[row_d, pl.BlockSpec((1, D), lambda i, j: (0, 0)), row_h, row_h, row_h, row_d],
        out_shape=[jax.ShapeDtypeStruct((t_rows, D), F32), jax.ShapeDtypeStruct((1, D), F32),
                   jax.ShapeDtypeStruct((t_rows, HIDP), BF16), jax.ShapeDtypeStruct((t_rows, HIDP), BF16),
                   jax.ShapeDtypeStruct((t_rows, HIDP), BF16), jax.ShapeDtypeStruct((t_rows, D), BF16)],
        scratch_shapes=[pltpu.VMEM((tm, D), F32)],
        compiler_params=_cparams(("arbitrary", "arbitrary")))(dx2, x1, gamma, gpre, upre, wg, wu, wd, token)


def _inproj_bwd(dz, dx1, x, gamma, w, tm=512, tn=1280):
    t_rows = x.shape[0]
    nn = IN_W // tn

    def body(dz_ref, d1_ref, x_ref, g_ref, w_ref, dx_ref, dgam_ref, h_ref, acc_ref):
        i = pl.program_id(0)
        j = pl.program_id(1)

        @pl.when(j == 0)
        def _():
            acc_ref[...] = jnp.zeros_like(acc_ref)

        @pl.when((i == 0) & (j == 0))
        def _():
            dgam_ref[...] = jnp.zeros_like(dgam_ref)

        acc_ref[...] += _dot_nt(dz_ref[...], w_ref[...])

        @pl.when(j == nn - 1)
        def _():
            x = x_ref[...]
            h_ref[...] = _rms(x, g_ref[...]).astype(BF16)
            dx, dgam = _rms_bwd_tail(x, g_ref[...], acc_ref[...])
            dx_ref[...] = d1_ref[...] + dx
            dgam_ref[...] += dgam

    row_d = pl.BlockSpec((tm, D), lambda i, j: (i, 0))
    return _call(
        body, name="inproj_bwd", grid=(t_rows // tm, nn),
        in_specs=[pl.BlockSpec((tm, tn), lambda i, j: (i, j)), row_d, row_d, pl.BlockSpec((1, D), lambda i, j: (0, 0)),
                  pl.BlockSpec((D, tn), lambda i, j: (0, j))],
        out_specs=[row_d, pl.BlockSpec((1, D), lambda i, j: (0, 0)), row_d],
        out_shape=[jax.ShapeDtypeStruct((t_rows, D), F32), jax.ShapeDtypeStruct((1, D), F32),
                   jax.ShapeDtypeStruct((t_rows, D), BF16)],
        scratch_shapes=[pltpu.VMEM((tm, D), F32)],
        compiler_params=_cparams(("arbitrary", "arbitrary")))(dz, dx1, x, gamma, w)


def _matmul_tn(a, b, name, owner_cols=None, tt=512):
    t_rows, k = a.shape
    n = b.shape[1]
    tk = min(k, 1024)
    nt = t_rows // tt
    if owner_cols is None:
        tn, nb = min(n, 512), None
        out_spec = pl.BlockSpec((tk, tn), lambda i, j, t: (i, j))
        out_shape = jax.ShapeDtypeStruct((k, n), BF16)
    else:
        nb = max(1, 512 // owner_cols)
        tn = nb * owner_cols
        out_spec = pl.BlockSpec((nb, tk, owner_cols), lambda i, j, t: (j, i, 0))
        out_shape = jax.ShapeDtypeStruct((n // owner_cols, k, owner_cols), BF16)

    def body(a_ref, b_ref, o_ref, acc_ref):
        t = pl.program_id(2)

        @pl.when(t == 0)
        def _():
            acc_ref[...] = jnp.zeros_like(acc_ref)

        acc_ref[...] += _dot_tn(a_ref[...], b_ref[...])

        @pl.when(t == nt - 1)
        def _():
            if nb is None:
                o_ref[...] = acc_ref[...].astype(BF16)
            else:
                for q in range(nb):
                    o_ref[q] = acc_ref[:, owner_cols * q:owner_cols * (q + 1)].astype(BF16)

    return _call(
        body, name=name, grid=(k // tk, n // tn, nt),
        in_specs=[pl.BlockSpec((tt, tk), lambda i, j, t: (t, i)), pl.BlockSpec((tt, tn), lambda i, j, t: (t, j))],
        out_specs=out_spec, out_shape=out_shape,
        scratch_shapes=[pltpu.VMEM((tk, tn), F32)],
        compiler_params=_cparams(("parallel", "parallel", "arbitrary")))(a, b)


def _group_tn(a, b):
    t_rows = a.shape[0]
    gw = BW // 4

    def body(a_ref, b_ref, o_ref):
        o_ref[...] = _dot_tn(a_ref[...], b_ref[...])

    return _call(
        body, name="pool_group_tn", grid=(4,),
        in_specs=[pl.BlockSpec((t_rows, gw), lambda k: (0, k)), pl.BlockSpec((t_rows, gw), lambda k: (0, k))],
        out_specs=pl.BlockSpec((None, gw, gw), lambda k: (k, 0, 0)),
        out_shape=jax.ShapeDtypeStruct((4, gw, gw), F32),
        compiler_params=_cparams(("parallel",)))(a, b)


def _loss_head(x2, gamma, target, tm=512):
    t_rows = x2.shape[0]

    def body(x_ref, g_ref, t_ref, loss_ref, dx_ref, dgam_ref):
        @pl.when(pl.program_id(0) == 0)
        def _():
            loss_ref[...] = jnp.zeros_like(loss_ref)
            dgam_ref[...] = jnp.zeros_like(dgam_ref)

        def f(x, g):
            err = jnp.square(_rms(x, g) - t_ref[...])
            return 0.5 * jnp.sum(jnp.mean(err, axis=-1, keepdims=True), axis=0, keepdims=True)

        loss, vjp = jax.vjp(f, x_ref[...], g_ref[...])
        dx, dgam = vjp(jnp.ones((1, 1), F32))
        loss_ref[...] += jnp.broadcast_to(loss, (1, 128))
        dx_ref[...] = dx
        dgam_ref[...] += dgam

    row_d = pl.BlockSpec((tm, D), lambda i: (i, 0))
    return _call(
        body, name="loss_head", grid=(t_rows // tm,),
        in_specs=[row_d, pl.BlockSpec((1, D), lambda i: (0, 0)), row_d],
        out_specs=[pl.BlockSpec((1, 128), lambda i: (0, 0)), row_d, pl.BlockSpec((1, D), lambda i: (0, 0))],
        out_shape=[jax.ShapeDtypeStruct((1, 128), F32), jax.ShapeDtypeStruct((t_rows, D), F32),
                   jax.ShapeDtypeStruct((1, D), F32)],
        compiler_params=_cparams(("arbitrary",)))(x2, gamma, target)


NCHIP = NDEV // 2


def _coords():
    return lax.axis_index("x"), lax.axis_index("y"), lax.axis_index("c")


def _remote(src, dst, send_sem, recv_sem, peer):
    return pltpu.make_async_remote_copy(src_ref=src, dst_ref=dst, send_sem=send_sem, recv_sem=recv_sem,
                                        device_id=peer, device_id_type=MESH)


def _comm_call(name, srcs, out_shapes, n_rec, plan, aliases=None):
    ns, no = len(srcs), len(out_shapes)

    def body(*refs):
        ins, outs = refs[:ns], refs[ns:ns + no]
        loc_sem, send_sem, recv_sem = refs[ns + no:]
        x, y, c = _coords()
        recs = plan(ins, outs, x, y, c)
        assert len(recs) == n_rec
        for k, r in enumerate(recs):
            for src, dst in r.get("local", ()):
                pltpu.make_async_copy(src, dst, loc_sem.at[k]).start()
            for peer, src, dst in r.get("remote", ()):
                _remote(src, dst, send_sem.at[k], recv_sem.at[k], peer).start()
        for k, r in enumerate(recs):
            if r.get("recv_wait") is not None:
                w = r["recv_wait"]
                _remote(w, w, send_sem.at[k], recv_sem.at[k], (x, y, c)).wait_recv()
            if r.get("send_wait") is not None:
                w = r["send_wait"]
                _remote(w, w, send_sem.at[k], recv_sem.at[k], (x, y, c)).wait_send()
            if r.get("local_wait") is not None:
                w = r["local_wait"]
                pltpu.make_async_copy(w, w, loc_sem.at[k]).wait()

    return _call(
        body, name=name, in_specs=[ANY] * ns, out_specs=[ANY] * no, out_shape=out_shapes,
        input_output_aliases=aliases or {}, scratch_shapes=[pltpu.SemaphoreType.DMA((n_rec,))] * 3)(*srcs)


def _gather_call(srcs, out_shapes, items):
    ns, no, n = len(srcs), len(out_shapes), len(items)

    def body(*refs):
        ins, outs = refs[:ns], refs[ns:ns + no]
        loc, sib_s, sib_r, ici_s, ici_r, fwd_s, fwd_r = refs[ns + no:]
        x, y, c = _coords()
        me, sib = (x, y, c), (x, y, 1 - c)
        chips = [(1 - x, y), (x, 1 - y), (1 - x, 1 - y)]
        index = lambda px, py, pc: 4 * px + 2 * py + pc
        for k, (si, oi, shard, block, _) in enumerate(items):
            src, mine = shard(ins[si]), block(outs[oi], index(*me))
            pltpu.make_async_copy(src, mine, loc.at[k]).start()
            _remote(src, mine, sib_s.at[k], sib_r.at[k], sib).start()
            for chip in chips:
                _remote(src, mine, ici_s.at[k], ici_r.at[k], (*chip, c)).start()
        for k, (si, oi, _, block, blocks) in enumerate(items):
            three = blocks(outs[oi], 3)
            _remote(three, three, ici_s.at[k], ici_r.at[k], me).wait_recv()
            for chip in chips:
                landed = block(outs[oi], index(*chip, c))
                _remote(landed, landed, fwd_s.at[k], fwd_r.at[k], sib).start()
        for k, (si, oi, _, _, blocks) in enumerate(items):
            one, three = blocks(outs[oi], 1), blocks(outs[oi], 3)
            _remote(one, one, sib_s.at[k], sib_r.at[k], me).wait()
            _remote(three, three, fwd_s.at[k], fwd_r.at[k], me).wait()
            _remote(three, three, ici_s.at[k], ici_r.at[k], me).wait_send()
            pltpu.make_async_copy(one, one, loc.at[k]).wait()

    return _call(
        body, name="gather_weights", in_specs=[ANY] * ns, out_specs=[ANY] * no, out_shape=out_shapes,
        scratch_shapes=[pltpu.SemaphoreType.DMA((n,))] * 7)(*srcs)


_BIG = {
    "w_in": (1, D, IN_W // NDEV, D, IN_W),
    "ssm_w_glu": (0, BW // NDEV, BW, BW, BW),
    "ssm_w_proj": (1, BW, D // NDEV, BW, D),
    "conv_w_proj": (1, BW, D // NDEV, BW, D),
    "pool_w_proj": (1, BW, D // NDEV, BW, D),
    "w_out": (0, D // NDEV, D, D, D),
    "ffn_w_gate": (1, D, HPAD, D, HIDP),
    "ffn_w_up": (1, D, HPAD, D, HIDP),
    "ffn_w_down": (0, HPAD, D, HIDP, D),
}


def _block_view(axis, size):
    if axis == 1:
        return lambda ref, q: ref.at[:, pl.ds(pl.multiple_of(q * size, 128), size)]
    return lambda ref, q: ref.at[pl.ds(pl.multiple_of(q * size, 16), size), :]


def _blocks_view(axis, size):
    if axis == 1:
        return lambda ref, n: ref.at[:, pl.ds(0, n * size)]
    return lambda ref, n: ref.at[pl.ds(0, n * size), :]


def _gather_weights(shards, conv_dw, layers):
    srcs, outs, items, where = [], [], [], {}
    for name, (axis, kk, nn, kf, nf) in _BIG.items():
        srcs.append(shards[name])
        size = nn if axis == 1 else kk
        for l in layers:
            where[(name, l)] = len(outs)
            outs.append(jax.ShapeDtypeStruct((kf, nf), BF16))
            items.append((len(srcs) - 1, len(outs) - 1, lambda ref, l=l: ref.at[l], _block_view(axis, size),
                          _blocks_view(axis, size)))
    srcs.append(conv_dw)
    outs.append(jax.ShapeDtypeStruct((NDEV,) + conv_dw.shape, conv_dw.dtype))
    items.append((len(srcs) - 1, len(outs) - 1, lambda ref: ref, lambda ref, q: ref.at[q],
                  lambda ref, n: ref.at[pl.ds(0, n)]))
    res = _gather_call(srcs, outs, items)
    full = {name: {l: res[where[(name, l)]] for l in layers} for name in _BIG}
    return full, res[-1]


def _gather_start(shards, layer, after):
    names = list(_BIG)
    lands = [jax.ShapeDtypeStruct((_BIG[n][3], _BIG[n][4]), BF16) for n in names]

    def copies(src_refs, land_refs, x, y, c):
        me = 4 * x + 2 * y + c
        peers = [(x, y, 1 - c), (1 - x, y, c), (x, 1 - y, c), (1 - x, 1 - y, c)]
        out = []
        for k, n in enumerate(names):
            axis, kk, nn, _, _ = _BIG[n]
            mine = _block_view(axis, nn if axis == 1 else kk)(land_refs[k], me)
            out.append([(peer, src_refs[k].at[layer], mine) for peer in peers])
        return out

    return _split_start("gather_start", [shards[n] for n in names], lands, copies, after)


def _gather_finish(handle, layer, after):
    names = list(_BIG)
    sizes = [(_BIG[n][0], _BIG[n][2] if _BIG[n][0] == 1 else _BIG[n][1]) for n in names]
    four = [functools.partial(lambda ref, bv: bv(ref, 4), bv=_blocks_view(axis, size)) for axis, size in sizes]
    lands = _split_wait("gather_wait", handle, four, after)
    n = len(names)

    def plan(ins, out_refs, x, y, c):
        me, sib = 4 * x + 2 * y + c, (x, y, 1 - c)
        chips = [(1 - x, y), (x, 1 - y), (1 - x, 1 - y)]
        recs = []
        for k, (axis, size) in enumerate(sizes):
            block, blocks = _block_view(axis, size), _blocks_view(axis, size)
            land = out_refs[k]
            remote = []
            for px, py in chips:
                landed = block(land, 4 * px + 2 * py + c)
                remote.append((sib, landed, landed))
            recs.append(dict(local=[(ins[n + k].at[layer], block(land, me))], remote=remote,
                             local_wait=blocks(land, 1), send_wait=blocks(land, 3), recv_wait=blocks(land, 3)))
        return recs

    res = _comm_call("gather_pair", lands + handle["srcs"], [jax.ShapeDtypeStruct(l.shape, l.dtype) for l in lands], n, plan,
                     aliases={k: k for k in range(n)})
    return dict(zip(names, res))


def _pair_add(name, grads, rcv, core):
    nl = len(grads)
    _, kk, nn = grads[0].shape

    def body(c_ref, *refs):
        l = pl.program_id(0)
        own = refs[0][...]
        for j in range(1, nl):
            own = jnp.where(l == j, refs[j][...], own)
        refs[nl + 1][...] = (own.astype(F32) + refs[nl][...].astype(F32)).astype(BF16)

    gspec = lambda j: pl.BlockSpec((None, kk, nn), lambda l, h, c_ref: (jnp.where(l == j, 2 * h + c_ref[0], 0), 0, 0))
    rspec = pl.BlockSpec((None, None, kk, nn), lambda l, h, c_ref: (h, l, 0, 0))
    return _call(
        body, name="pair_add_" + name,
        grid_spec=pltpu.PrefetchScalarGridSpec(num_scalar_prefetch=1, grid=(nl, NCHIP),
                                               in_specs=[gspec(j) for j in range(nl)] + [rspec], out_specs=rspec),
        out_shape=jax.ShapeDtypeStruct(rcv.shape, BF16),
        compiler_params=_cparams(("arbitrary", "arbitrary")))(core, *grads, rcv)


def _pair_add_small(owned, lists, core):
    on, ln = list(owned), list(lists)
    flat = []
    for n in on:
        flat += list(owned[n][0]) + [owned[n][1]]
    for n in ln:
        flat += list(lists[n][0]) + [lists[n][1]]

    def body(c_ref, *refs):
        outs = refs[len(flat):]
        c = c_ref[0]
        pos = 0
        for k, n in enumerate(on):
            nl = len(owned[n][0])
            for h in range(NCHIP):
                for l in range(nl):
                    outs[k][h, l] = refs[pos + l][pl.ds(2 * h + c, 1)][0] + refs[pos + nl][h, l]
            pos += nl + 1
        for k, n in enumerate(ln):
            nl = len(lists[n][0])
            for l in range(nl):
                outs[len(on) + k][l] = refs[pos + l][...] + refs[pos + nl][l]
            pos += nl + 1

    shapes = [jax.ShapeDtypeStruct(owned[n][1].shape, F32) for n in on]
    shapes += [jax.ShapeDtypeStruct(lists[n][1].shape, F32) for n in ln]
    res = _call(body, name="pair_add_small", out_shape=shapes,
                in_specs=[pl.BlockSpec(memory_space=pltpu.SMEM)] + [pl.BlockSpec(memory_space=pltpu.VMEM)] * len(flat),
                compiler_params=_cparams())(core, *flat)
    return dict(zip(on + ln, res))


def _pair_reduce(tag, big, by_owner, small, core):
    rs = {**big, **by_owner}
    srcs, outs, plans, rcv_at = [], [], [], {}
    for name, arrays in rs.items():
        rcv_at[name] = len(outs)
        outs.append(jax.ShapeDtypeStruct((NCHIP, len(arrays)) + arrays[0].shape[1:], arrays[0].dtype))
        for l, arr in enumerate(arrays):
            srcs.append(arr)
            plans.append((len(srcs) - 1, rcv_at[name], l, True))
    for name, arrays in small.items():
        rcv_at[name] = len(outs)
        outs.append(jax.ShapeDtypeStruct((len(arrays),) + arrays[0].shape, F32))
        for l, arr in enumerate(arrays):
            srcs.append(arr)
            plans.append((len(srcs) - 1, rcv_at[name], l, False))

    def plan_pair(ins, out_refs, x, y, c):
        sib = (x, y, 1 - c)
        recs = []
        for si, ro, l, slabs in plans:
            if slabs:
                four = out_refs[ro].at[pl.ds(0, NCHIP), l]
                recs.append(dict(remote=[(sib, ins[si].at[2 * h + 1 - c], out_refs[ro].at[h, l]) for h in range(NCHIP)],
                                 send_wait=four, recv_wait=four))
            else:
                dst = out_refs[ro].at[l]
                recs.append(dict(remote=[(sib, ins[si], dst)], send_wait=dst, recv_wait=dst))
        return recs

    res = _comm_call("pair_exchange_" + tag, srcs, outs, len(plans), plan_pair)
    part = {name: _pair_add(name, big[name], res[rcv_at[name]], core) for name in big}
    if by_owner or small:
        part.update(_pair_add_small({n: (by_owner[n], res[rcv_at[n]]) for n in by_owner},
                                    {n: (small[n], res[rcv_at[n]]) for n in small}, core))
    return part


def _chip_copies(src, land, slabbed, x, y, c):
    mine = 2 * x + y
    copies = []
    for step in range(1, NCHIP):
        h = (mine + step) % NCHIP
        copies.append(((h // 2, h % 2, c), src.at[h] if slabbed else src, land.at[mine]))
    return copies


def _chip_exchange(part, slabbed, keep_own):
    names = list(part)
    outs = [jax.ShapeDtypeStruct((() if n in slabbed else (NCHIP,)) + part[n].shape, part[n].dtype) for n in names]

    def plan(ins, out_refs, x, y, c):
        mine = 2 * x + y
        recs = []
        for k, n in enumerate(names):
            three = out_refs[k].at[pl.ds(0, NCHIP - 1)]
            rec = dict(remote=_chip_copies(ins[k], out_refs[k], n in slabbed, x, y, c), send_wait=three, recv_wait=three)
            if n in keep_own:
                rec["local"] = [(ins[k].at[mine] if n in slabbed else ins[k], out_refs[k].at[mine])]
                rec["local_wait"] = out_refs[k].at[0]
            recs.append(rec)
        return recs

    res = _comm_call("chip_exchange", [part[n] for n in names], outs, len(names), plan)
    return dict(zip(names, res))


HBM_SPEC = pl.BlockSpec(memory_space=pltpu.HBM)
SEM_SPEC = pl.BlockSpec(memory_space=pltpu.SEMAPHORE)
SPLIT_EFFECT = pltpu.SideEffectType.DATAFLOW_SIDE_EFFECTING


def _split_start(name, srcs, land_shapes, copies_fn, after):
    n = len(srcs)
    lands = [pltpu.with_memory_space_constraint(lax.empty(s.shape, s.dtype), pltpu.HBM) for s in land_shapes]

    def body(*refs):
        src_refs, land_refs = refs[:n], refs[n:2 * n]
        send_sem, recv_sem = refs[2 * n + 1], refs[2 * n + 2]
        token = refs[-1]
        x, y, c = _coords()
        for k, copies in enumerate(copies_fn(src_refs, land_refs, x, y, c)):
            for peer, src, dst in copies:
                _remote(src, dst, send_sem.at[k], recv_sem.at[k], peer).start()
        token[...] = jnp.zeros_like(token)

    res = pl.pallas_call(
        body, name=name,
        out_shape=(pltpu.SemaphoreType.DMA((n,)), pltpu.SemaphoreType.DMA((n,)),
                   *[pltpu.HBM(s.shape, s.dtype) for s in srcs], *[pltpu.HBM(s.shape, s.dtype) for s in land_shapes],
                   jax.ShapeDtypeStruct((8, 128), F32)),
        in_specs=[HBM_SPEC] * (2 * n) + [ANY],
        out_specs=(SEM_SPEC, SEM_SPEC, *[HBM_SPEC] * (2 * n), pl.BlockSpec(memory_space=pltpu.VMEM)),
        input_output_aliases={i: 2 + i for i in range(2 * n)},
        compiler_params=pltpu.CompilerParams(has_side_effects=SPLIT_EFFECT),
    )(*[pltpu.with_memory_space_constraint(s, pltpu.HBM) for s in srcs], *lands, after)
    return dict(send=res[0], recv=res[1], srcs=list(res[2:2 + n]), lands=list(res[2 + n:2 + 2 * n]), token=res[-1])


def _split_wait(name, handle, wait_views, after):
    n = len(handle["srcs"])

    def body(*refs):
        land_refs = refs[n:2 * n]
        send_sem, recv_sem = refs[2 * n], refs[2 * n + 1]
        x, y, c = _coords()
        for k in range(n):
            w = wait_views[k](land_refs[k])
            cp = _remote(w, w, send_sem.at[k], recv_sem.at[k], (x, y, c))
            cp.wait_send()
            cp.wait_recv()

    res = pl.pallas_call(
        body, name=name,
        out_shape=(*[pltpu.HBM(s.shape, s.dtype) for s in handle["srcs"]],
                   *[pltpu.HBM(s.shape, s.dtype) for s in handle["lands"]]),
        in_specs=[HBM_SPEC] * (2 * n) + [SEM_SPEC, SEM_SPEC, ANY], out_specs=tuple([HBM_SPEC] * (2 * n)),
        input_output_aliases={i: i for i in range(2 * n)},
        compiler_params=pltpu.CompilerParams(has_side_effects=SPLIT_EFFECT),
    )(*handle["srcs"], *handle["lands"], handle["send"], handle["recv"], after)
    return list(res[n:])


def _adamw(w, g, m, v):
    m = ADAM_B1 * m + (1.0 - ADAM_B1) * g
    v = ADAM_B2 * v + (1.0 - ADAM_B2) * jnp.square(g)
    m_hat = m / (1.0 - ADAM_B1 ** ADAM_STEP)
    v_hat = v / (1.0 - ADAM_B2 ** ADAM_STEP)
    delta = -ADAM_LR * (m_hat / (jnp.sqrt(v_hat) + ADAM_EPS) + ADAM_WD * w)
    return delta, m, v


def _chip_start(part, tag):
    names = list(part)

    def copies(src_refs, land_refs, x, y, c):
        return [_chip_copies(src_refs[k], land_refs[k], True, x, y, c) for k in range(len(names))]

    shapes = [jax.ShapeDtypeStruct(part[n].shape, part[n].dtype) for n in names]
    return names, _split_start("chip_start_" + tag, [part[n] for n in names], shapes, copies, part[names[0]])


def _chip_wait(names, handle, after, tag):
    three = [lambda ref: ref.at[pl.ds(0, NCHIP - 1)]] * len(names)
    return dict(zip(names, _split_wait("chip_wait_" + tag, handle, three, after)))


def _sum_senders(ref):
    g = ref[0].astype(F32)
    for h in range(1, NCHIP):
        g = g + ref[h].astype(F32)
    return g


def _adam_big(name, own, recv, w, m, v, tk, chip):
    nl = len(own)
    kk, nn = w.shape[1], w.shape[2]
    nnp = own[0].shape[3]

    def body(chip_ref, *refs):
        l = pl.program_id(0)
        g = None
        for step in range(NCHIP):
            val = refs[step][...]
            for q in range(1, nl):
                val = jnp.where(l == q, refs[NCHIP * q + step][...], val)
            g = val.astype(F32) if g is None else g + val.astype(F32)
        w_ref, m_ref, v_ref, g_ref, d_ref, mo_ref, vo_ref = refs[NCHIP * nl:]
        g = g[:, :nn]
        delta, m2, v2 = _adamw(w_ref[...], g, m_ref[...], v_ref[...])
        g_ref[...] = g
        d_ref[...] = delta
        mo_ref[...] = m2
        vo_ref[...] = v2

    def slab(q, step):
        return pl.BlockSpec((None, None, tk, nnp), lambda l, i, chip_ref: (
            jnp.where(l == q, (chip_ref[0] + step) % NCHIP, 0), 0, jnp.where(l == q, i, 0), 0))

    in_specs, operands = [], []
    for q in range(nl):
        in_specs += [slab(q, step) for step in range(NCHIP)]
        operands += [own[q]] + [recv[q]] * (NCHIP - 1)
    wspec = pl.BlockSpec((None, tk, nn), lambda l, i, chip_ref: (l, i, 0))
    shape = jax.ShapeDtypeStruct(w.shape, F32)
    return _call(
        body, name="adamw_" + name,
        grid_spec=pltpu.PrefetchScalarGridSpec(num_scalar_prefetch=1, grid=(nl, kk // tk),
                                               in_specs=in_specs + [wspec] * 3, out_specs=[wspec] * 4),
        out_shape=[shape] * 4, compiler_params=_cparams(("arbitrary", "arbitrary")))(chip, *operands, w, m, v)


def _adam_small(names, recv, w, m, v):
    n = len(names)

    def body(*refs):
        r, ww, mm, vv = refs[:n], refs[n:2 * n], refs[2 * n:3 * n], refs[3 * n:4 * n]
        outs = refs[4 * n:]
        for k in range(n):
            g = _sum_senders(r[k])
            if g.shape != ww[k].shape:
                g = g[:, :ww[k].shape[1]]
            delta, m2, v2 = _adamw(ww[k][...], g, mm[k][...], vv[k][...])
            outs[k][...] = g
            outs[n + k][...] = delta
            outs[2 * n + k][...] = m2
            outs[3 * n + k][...] = v2

    shapes = [jax.ShapeDtypeStruct(w[k].shape, F32) for k in names]
    res = _call(body, name="adamw_small", out_shape=shapes * 4, compiler_params=_cparams())(
        *[recv[k] for k in names], *[w[k] for k in names], *[m[k] for k in names], *[v[k] for k in names])
    return {k: (res[i], res[n + i], res[2 * n + i], res[3 * n + i]) for i, k in enumerate(names)}


def _expand_b(b):
    bt = jnp.transpose(b, (0, 2, 1)).reshape(NBLK, GB, SGRP, NSTATE)
    eye = jnp.eye(GB, dtype=b.dtype)
    return jnp.einsum("jgpn,gh->jgphn", bt, eye).reshape(NBLK, GB * SGRP, NS)


def _extract_b(db):
    x = db.reshape(NBLK, GB, SGRP, GB, NSTATE)
    eye = jnp.eye(GB, dtype=db.dtype)
    d = jnp.einsum("jgphn,gh->jgpn", x, eye).reshape(NGRP, SGRP, NSTATE)
    return jnp.transpose(d, (0, 2, 1)).reshape(NGRP, NSTATE * SGRP)


def _expand_c(c):
    ct = jnp.transpose(c, (0, 2, 1)).reshape(NBLK, GB, NSTATE, SGRP)
    eye = jnp.eye(GB, dtype=c.dtype)
    return jnp.einsum("jgnp,gh->jgnhp", ct, eye).reshape(NBLK, NS, GB * SGRP)


def _extract_c(dc):
    x = dc.reshape(NBLK, GB, NSTATE, GB, SGRP)
    eye = jnp.eye(GB, dtype=dc.dtype)
    d = jnp.einsum("jgnhp,gh->jgnp", x, eye).reshape(NGRP, NSTATE, SGRP)
    return jnp.transpose(d, (0, 2, 1))


_SMALL = ("norm1", "b_gate", "ssm_a_re", "ssm_a_im", "ssm_log_dt", "ssm_b_re", "ssm_b_im", "ssm_c_re", "ssm_c_im",
          "ssm_d", "ssm_b_glu", "conv_b_dw", "conv_ln_g", "conv_ln_b", "pool_w_group", "pool_scale", "norm2")
_ADAM_TK = {"w_in": 256, "ssm_w_glu": 64, "ssm_w_proj": 512, "conv_w_proj": 512, "pool_w_proj": 512, "w_out": 128,
            "ffn_w_gate": 256, "ffn_w_up": 256, "ffn_w_down": HSH}
_OUT_ORDER = ("norm1", "w_in", "b_gate", "ssm_a_re", "ssm_a_im", "ssm_log_dt", "ssm_b_re", "ssm_b_im", "ssm_c_re",
              "ssm_c_im", "ssm_d", "ssm_w_glu", "ssm_b_glu", "ssm_w_proj", "conv_w_dw", "conv_b_dw", "conv_ln_g",
              "conv_ln_b", "conv_w_proj", "pool_w_group", "pool_scale", "pool_w_proj", "w_out", "norm2", "ffn_w_gate",
              "ffn_w_up", "ffn_w_down", "final_norm")


def _layer_fwd(x, p, token):
    z = _inproj_fwd(x, p["norm1"], p["w_in"], token)
    yssm, hre, him = _ssm_fwd(z, p)
    cv = _conv_fwd(z, p["conv_w"], p["conv_b"])
    pbar = _pool_fwd(z)
    x1 = _merge_fwd(x, yssm, cv, pbar, z, p)
    x2, gpre, upre = _ffn_fwd(x1, p["norm2"], p["wg"], p["wu"], p["wd"])
    return x2, dict(x=x, z=z, yssm=yssm, hre=hre, him=him, cv=cv, pbar=pbar, x1=x1, gpre=gpre, upre=upre)


def _layer_bwd(dx, p, s, token, after_ffn=None):
    big, small = {}, {}
    dx1, d_norm2, dgp, dup, act, h2 = _ffn_bwd(dx, s["x1"], p["norm2"], s["gpre"], s["upre"], p["wg"], p["wu"], p["wd"],
                                               token)
    big["ffn_w_gate"] = _matmul_tn(h2, dgp, "tn_gate", HPAD)
    big["ffn_w_up"] = _matmul_tn(h2, dup, "tn_up", HPAD)
    big["ffn_w_down"] = _matmul_tn(act, dx, "tn_down").reshape(NDEV, HPAD, D)
    (dy, dcv, dpb, dzg, a_g, a_outa, a_hs, a_pb, a_pc, a_mg, c_glu, c_ya, c_yb, c_p, c_yc,
     d_bglu, d_lng, d_lnb, d_scale, d_bgate) = _merge_bwd(dx1, s["x"], s["yssm"], s["cv"], s["pbar"], s["z"], p,
                                                          after_ffn(dict(big)) if after_ffn else token)
    big["ssm_w_glu"] = _matmul_tn(a_g, c_glu, "tn_glu").reshape(NDEV, BW // NDEV, BW)
    big["ssm_w_proj"] = _matmul_tn(a_outa, c_ya, "tn_ssm_proj", D // NDEV)
    big["conv_w_proj"] = _matmul_tn(a_hs, c_yb, "tn_conv_proj", D // NDEV)
    big["pool_w_proj"] = _matmul_tn(a_pc, c_yc, "tn_pool_proj", D // NDEV)
    big["w_out"] = _matmul_tn(a_mg, dx1, "tn_out").reshape(NDEV, D // NDEV, D)
    d_wgrp = _group_tn(a_pb, c_p)
    du_a, dbr, dbi, dcr, dci, dd, dar, dai, dldt = _ssm_bwd(dy, s["z"], s["hre"], s["him"], p)
    dva, dvb, dw8, dcb = _conv_bwd(dcv, s["z"], p["conv_w"])
    du_c = _pool_bwd(dpb)
    dz = jnp.concatenate([du_a, dva, dvb, du_c, dzg], axis=1)
    dx0, d_norm1, h = _inproj_bwd(dz, dx1, s["x"], p["norm1"], p["w_in"])
    big["w_in"] = _matmul_tn(h, dz, "tn_in", IN_W // NDEV)
    small["norm1"] = d_norm1
    small["b_gate"] = d_bgate
    small["ssm_a_re"] = dar.reshape(NGRP, NSTATE)
    small["ssm_a_im"] = dai.reshape(NGRP, NSTATE)
    small["ssm_log_dt"] = dldt.reshape(NBLK, 8, 128)[:, 0, :GB].reshape(1, NGRP)
    small["ssm_b_re"] = _extract_b(dbr)
    small["ssm_b_im"] = _extract_b(dbi)
    small["ssm_c_re"] = _extract_c(dcr)
    small["ssm_c_im"] = _extract_c(dci)
    small["ssm_d"] = dd.reshape(NGRP, SGRP)
    small["ssm_b_glu"] = d_bglu
    small["conv_b_dw"] = dcb
    small["conv_ln_g"] = d_lng
    small["conv_ln_b"] = d_lnb
    small["pool_w_group"] = d_wgrp
    small["pool_scale"] = d_scale
    small["norm2"] = d_norm2
    return dx0, big, dw8, small


def _train_step(a):
    t_rows = a["x"].shape[1]
    x0 = a["x"].reshape(t_rows, D)
    target = a["loss_target"].reshape(t_rows, D)

    pad_cols = lambda w: jnp.pad(w, ((0, 0), (0, 0), (0, HPAD - HSH)))
    shards = {
        "w_in": a["w_in"], "ssm_w_glu": a["ssm_w_glu"], "ssm_w_proj": a["ssm_w_proj"],
        "conv_w_proj": a["conv_w_proj"], "pool_w_proj": a["pool_w_proj"], "w_out": a["w_out"],
        "ffn_w_gate": pad_cols(a["ffn_w_gate"]), "ffn_w_up": pad_cols(a["ffn_w_up"]),
        "ffn_w_down": jnp.pad(a["ffn_w_down"], ((0, 0), (0, HPAD - HSH), (0, 0))),
    }
    shards = {k: v.astype(BF16) for k, v in shards.items()}
    full, dw_all = _gather_weights(shards, a["conv_w_dw"].reshape(DEPTH, CONV_K, BW // NDEV), (0,))
    conv_w = jnp.transpose(dw_all, (1, 2, 0, 3)).reshape(DEPTH, CONV_K, BW)
    late = _gather_start(shards, DEPTH - 1, dw_all)
    core = lax.axis_index("c").astype(jnp.int32).reshape(1)
    chip = (2 * lax.axis_index("x") + lax.axis_index("y")).astype(jnp.int32).reshape(1)
    no_token = jnp.zeros((8, 128), F32)

    def layer_params(l):
        row = lambda v: v.reshape(1, -1)
        return dict(
            norm1=row(a["norm1"][l]), w_in=full["w_in"][l],
            are=row(a["ssm_a_re"][l]), aim=row(a["ssm_a_im"][l]),
            ldt=row(jnp.repeat(a["ssm_log_dt"][l], NSTATE)),
            bexp_re=_expand_b(a["ssm_b_re"][l]), bexp_im=_expand_b(a["ssm_b_im"][l]),
            cexp_re=_expand_c(a["ssm_c_re"][l]), cexp_im=_expand_c(a["ssm_c_im"][l]),
            dskip=row(a["ssm_d"][l]),
            conv_w=conv_w[l], conv_b=row(a["conv_b_dw"][l]),
            wglu=full["ssm_w_glu"][l], bglu=row(a["ssm_b_glu"][l]), wpa=full["ssm_w_proj"][l],
            lng=row(a["conv_ln_g"][l]), lnb=row(a["conv_ln_b"][l]), wpb=full["conv_w_proj"][l],
            wgrp=a["pool_w_group"][l].astype(BF16), scale=row(a["pool_scale"][l]), wpc=full["pool_w_proj"][l],
            bgate=row(a["b_gate"][l]), wout=full["w_out"][l],
            norm2=row(a["norm2"][l]), wg=full["ffn_w_gate"][l], wu=full["ffn_w_up"][l], wd=full["ffn_w_down"][l],
        )

    p0 = layer_params(0)
    x, s0 = _layer_fwd(x0, p0, late["token"])
    for name, w in _gather_finish(late, DEPTH - 1, x).items():
        full[name][DEPTH - 1] = w
    params = [p0, layer_params(DEPTH - 1)]
    x, s1 = _layer_fwd(x, params[1], no_token)
    saved = [s0, s1]

    loss_part, dx, d_final = _loss_head(x, a["final_norm"].reshape(1, D), target)
    loss = lax.psum(loss_part[0, 0], ("x", "y", "c"))

    dx, gb1, go1, gs1 = _layer_bwd(dx, params[1], saved[1], no_token)
    part1 = _pair_reduce("late", {n: [g] for n, g in gb1.items()}, {}, {}, core)
    rs_names, rs_handle = _chip_start(part1, "late")
    early = {}

    def ffn_grads_leave_early(ffn):
        early["part"] = _pair_reduce("ffn", {n: [g] for n, g in ffn.items()}, {}, {}, core)
        early["names"], early["handle"] = _chip_start(early["part"], "ffn")
        return early["handle"]["token"]

    dx, gb0, go0, gs0 = _layer_bwd(dx, params[0], saved[0], rs_handle["token"], ffn_grads_leave_early)
    grad_x = dx.reshape(1, t_rows, D)
    small = {n: [gs0[n], gs1[n]] for n in _SMALL}
    small["final_norm"] = [d_final]
    part0 = _pair_reduce("rest", {n: [g] for n, g in gb0.items() if n not in early["part"]}, {"conv_w_dw": [go0, go1]},
                         small, core)
    recv = _chip_exchange(part0, set(_BIG) | {"conv_w_dw"}, {"conv_w_dw"} | set(small))
    recv1 = _chip_wait(rs_names, rs_handle, dx, "late")
    recv.update(_chip_wait(early["names"], early["handle"], dx, "ffn"))
    part0.update(early["part"])

    results = {}
    for name in _BIG:
        results[name] = _adam_big(name, [part0[name], part1[name]], [recv[name], recv1[name]], a[name], a["m_" + name],
                                  a["v_" + name], _ADAM_TK[name], chip)

    lay = {
        "norm1": lambda v: v.reshape(DEPTH, 1, D), "b_gate": lambda v: v.reshape(DEPTH, 1, 3 * D),
        "ssm_log_dt": lambda v: v.reshape(DEPTH, 1, NGRP),
        "ssm_b_re": lambda v: v.reshape(DEPTH, NGRP, NSTATE * SGRP), "ssm_b_im": lambda v: v.reshape(DEPTH, NGRP, NSTATE * SGRP),
        "ssm_b_glu": lambda v: v.reshape(DEPTH, 1, BW), "conv_b_dw": lambda v: v.reshape(DEPTH, 1, BW),
        "conv_ln_g": lambda v: v.reshape(DEPTH, 1, BW), "conv_ln_b": lambda v: v.reshape(DEPTH, 1, BW),
        "pool_scale": lambda v: v.reshape(DEPTH, 1, BW), "norm2": lambda v: v.reshape(DEPTH, 1, D),
        "conv_w_dw": lambda v: v.reshape(DEPTH, CONV_K, BW // NDEV), "final_norm": lambda v: v.reshape(1, 1, D),
    }
    names = _SMALL + ("conv_w_dw", "final_norm")
    relay = lambda k, v: lay[k](v) if k in lay else v
    sm = _adam_small(names, recv, {k: relay(k, a[k]) for k in names}, {k: relay(k, a["m_" + k]) for k in names},
                     {k: relay(k, a["v_" + k]) for k in names})
    for k in names:
        results[k] = tuple(r.reshape(a[k].shape) for r in sm[k])

    outs = [loss, grad_x]
    for part in range(4):
        outs += [results[k][part] for k in _OUT_ORDER]
    return tuple(outs)


def kernel(x, norm1, w_in, b_gate, ssm_a_re, ssm_a_im, ssm_log_dt, ssm_b_re, ssm_b_im, ssm_c_re, ssm_c_im, ssm_d, ssm_w_glu, ssm_b_glu, ssm_w_proj, conv_w_dw, conv_b_dw, conv_ln_g, conv_ln_b, conv_w_proj, pool_w_group, pool_scale, pool_w_proj, w_out, norm2, ffn_w_gate, ffn_w_up, ffn_w_down, final_norm, loss_target, m_norm1, m_w_in, m_b_gate, m_ssm_a_re, m_ssm_a_im, m_ssm_log_dt, m_ssm_b_re, m_ssm_b_im, m_ssm_c_re, m_ssm_c_im, m_ssm_d, m_ssm_w_glu, m_ssm_b_glu, m_ssm_w_proj, m_conv_w_dw, m_conv_b_dw, m_conv_ln_g, m_conv_ln_b, m_conv_w_proj, m_pool_w_group, m_pool_scale, m_pool_w_proj, m_w_out, m_norm2, m_ffn_w_gate, m_ffn_w_up, m_ffn_w_down, m_final_norm, v_norm1, v_w_in, v_b_gate, v_ssm_a_re, v_ssm_a_im, v_ssm_log_dt, v_ssm_b_re, v_ssm_b_im, v_ssm_c_re, v_ssm_c_im, v_ssm_d, v_ssm_w_glu, v_ssm_b_glu, v_ssm_w_proj, v_conv_w_dw, v_conv_b_dw, v_conv_ln_g, v_conv_ln_b, v_conv_w_proj, v_pool_w_group, v_pool_scale, v_pool_w_proj, v_w_out, v_norm2, v_ffn_w_gate, v_ffn_w_up, v_ffn_w_down, v_final_norm):
    return _train_step(dict(locals()))
```

```python
import functools

import jax
import jax.numpy as jnp
from jax import lax
from jax.experimental import pallas as pl
from jax.experimental.pallas import tpu as pltpu

F32 = jnp.float32
BF16 = jnp.bfloat16

NDEV = 8
DEPTH = 2
D = 1024
BW = 512
NSTATE = 64
SGRP = 16
NGRP = BW // SGRP
GB = 8
NBLK = NGRP // GB
NS = GB * NSTATE
CONV_K = 31
HALO = 32
PHALO = 16
IN_W = 5120
HID = 2816
HSH = HID // NDEV
HPAD = 384
HIDP = HPAD * NDEV
EPS = 1e-6
VMEM_LIMIT = 56 * 1024 * 1024

ADAM_LR, ADAM_B1, ADAM_B2, ADAM_EPS, ADAM_WD, ADAM_STEP = 0.001, 0.9, 0.999, 1e-08, 0.01, 10

MESH = pl.DeviceIdType.MESH
ANY = pl.BlockSpec(memory_space=pl.ANY)


def _call(body, **kw):
    return pl.pallas_call(body, **kw)


def _cparams(sem=None):
    return pltpu.CompilerParams(dimension_semantics=sem, vmem_limit_bytes=VMEM_LIMIT)


def _dot(a, b):
    return jnp.dot(a.astype(BF16), b.astype(BF16), preferred_element_type=F32)


def _dot_nt(a, b):
    return lax.dot_general(a.astype(BF16), b.astype(BF16), (((1,), (1,)), ((), ())), preferred_element_type=F32)


def _dot_tn(a, b):
    return lax.dot_general(a.astype(BF16), b.astype(BF16), (((0,), (0,)), ((), ())), preferred_element_type=F32)


@jax.custom_vjp
def _mm(a, w):
    return _dot(a, w)


def _mm_fwd(a, w):
    return _dot(a, w), w


def _mm_bwd(w, ct):
    return _dot_nt(ct, w), jnp.zeros_like(w)


_mm.defvjp(_mm_fwd, _mm_bwd)


def _rms(x, g):
    return x * lax.rsqrt(jnp.mean(x * x, axis=-1, keepdims=True) + EPS) * g


def _disc(are, aim, ldt):
    dt = jnp.exp(ldt)
    mag = jnp.exp(dt * are)
    ang = dt * aim
    abr = mag * jnp.cos(ang)
    abi = mag * jnp.sin(ang)
    den = are * are + aim * aim
    nr = abr - 1.0
    fr = (nr * are + abi * aim) / den
    fi = (abi * are - nr * aim) / den
    return abr, abi, fr, fi


def _bbar(fr, fi, br, bi):
    return fr * br - fi * bi, fr * bi + fi * br


def _cmul(ar, ai, br, bi):
    return ar * br - ai * bi, ar * bi + ai * br


def _scan_rows(re_ref, im_ref, ar, ai, n_rows, reverse, hre_ref=None, him_ref=None):
    n = ar.shape[1]
    shape = (8, n)
    rows = lax.broadcasted_iota(jnp.int32, shape, 0)
    a1 = (jnp.broadcast_to(ar, shape), jnp.broadcast_to(ai, shape))
    a2 = _cmul(*a1, *a1)
    a4 = _cmul(*a2, *a2)
    pr = jnp.zeros(shape, F32)
    pi = jnp.zeros(shape, F32)
    pw = a1
    for k in range(8):
        sel = rows == ((7 - k) if reverse else k)
        pr = jnp.where(sel, pw[0], pr)
        pi = jnp.where(sel, pw[1], pi)
        pw = _cmul(*pw, *a1)
    nt = n_rows // 8
    with_acc = hre_ref is not None

    def body(i, carry):
        cr, ci = carry[0], carry[1]
        t = (nt - 1 - i) if reverse else i
        off = pl.multiple_of(t * 8, 8)
        xr = re_ref[pl.ds(off, 8), :]
        xi = im_ref[pl.ds(off, 8), :]
        for k, (kr, ki) in ((1, a1), (2, a2), (4, a4)):
            if reverse:
                keep, sh = rows < 8 - k, 8 - k
            else:
                keep, sh = rows >= k, k
            sr = jnp.where(keep, pltpu.roll(xr, sh, 0), 0.0)
            si = jnp.where(keep, pltpu.roll(xi, sh, 0), 0.0)
            xr, xi = xr + kr * sr - ki * si, xi + kr * si + ki * sr
        xr, xi = xr + pr * cr - pi * ci, xi + pr * ci + pi * cr
        re_ref[pl.ds(off, 8), :] = xr
        im_ref[pl.ds(off, 8), :] = xi
        edge = 0 if reverse else 7
        out = (jnp.broadcast_to(xr[edge:edge + 1, :], shape), jnp.broadcast_to(xi[edge:edge + 1, :], shape))
        if with_acc:
            hr = hre_ref[pl.ds(off, 8), :]
            hi = him_ref[pl.ds(off, 8), :]
            offp = pl.multiple_of(jnp.maximum(t - 1, 0) * 8, 8)
            live = jnp.where(t > 0, 1.0, 0.0)
            lr = jnp.broadcast_to(hre_ref[pl.ds(offp, 8), :][7:8, :], shape) * live
            li = jnp.broadcast_to(him_ref[pl.ds(offp, 8), :][7:8, :], shape) * live
            hpr = jnp.where(rows == 0, lr, pltpu.roll(hr, 1, 0))
            hpi = jnp.where(rows == 0, li, pltpu.roll(hi, 1, 0))
            out = out + (carry[2] + xr * hpr + xi * hpi, carry[3] + xi * hpr - xr * hpi)
        return out

    zero = jnp.zeros(shape, F32)
    init = (zero, zero, zero, zero) if with_acc else (zero, zero)
    res = lax.fori_loop(0, nt, body, init)
    return res[2:] if with_acc else None


TOKEN_SPEC = pl.BlockSpec((8, 128), lambda i, j: (0, 0))


def _inproj_fwd(x, gamma, w, token, tm=512, tn=1280):
    t_rows = x.shape[0]
    n = w.shape[1]

    def body(x_ref, g_ref, w_ref, token_ref, z_ref, h_ref):
        @pl.when(pl.program_id(1) == 0)
        def _():
            h_ref[...] = _rms(x_ref[...], g_ref[...]).astype(BF16)
        z_ref[...] = jnp.dot(h_ref[...], w_ref[...], preferred_element_type=F32)

    return _call(
        body, name="inproj_fwd", grid=(t_rows // tm, n // tn),
        in_specs=[pl.BlockSpec((tm, D), lambda i, j: (i, 0)), pl.BlockSpec((1, D), lambda i, j: (0, 0)),
                  pl.BlockSpec((D, tn), lambda i, j: (0, j)), TOKEN_SPEC],
        out_specs=pl.BlockSpec((tm, tn), lambda i, j: (i, j)),
        out_shape=jax.ShapeDtypeStruct((t_rows, n), F32),
        scratch_shapes=[pltpu.VMEM((tm, D), BF16)],
        compiler_params=_cparams(("parallel", "arbitrary")))(x, gamma, w, token)


def _ssm_specs(t_rows):
    row = pl.BlockSpec((1, NS), lambda j: (0, j))
    return dict(
        u=pl.BlockSpec((t_rows, GB * SGRP), lambda j: (0, j)),
        row=row,
        bexp=pl.BlockSpec((None, GB * SGRP, NS), lambda j: (j, 0, 0)),
        cexp=pl.BlockSpec((None, NS, GB * SGRP), lambda j: (j, 0, 0)),
        d=pl.BlockSpec((1, GB * SGRP), lambda j: (0, j)),
        h=pl.BlockSpec((t_rows, NS), lambda j: (0, j)),
    )


def _ssm_fwd(z, p):
    t_rows = z.shape[0]
    s = _ssm_specs(t_rows)

    def body(u_ref, are_ref, aim_ref, ldt_ref, br_ref, bi_ref, cr_ref, ci_ref, d_ref, y_ref, hr_ref, hi_ref):
        abr, abi, fr, fi = _disc(are_ref[...], aim_ref[...], ldt_ref[...])
        bbr, bbi = _bbar(fr, fi, br_ref[...], bi_ref[...])
        u = u_ref[...]
        hr_ref[...] = _dot(u, bbr)
        hi_ref[...] = _dot(u, bbi)
        _scan_rows(hr_ref, hi_ref, abr, abi, t_rows, False)
        y_ref[...] = _dot(hr_ref[...], cr_ref[...]) - _dot(hi_ref[...], ci_ref[...]) + d_ref[...] * u

    return _call(
        body, name="ssm_fwd", grid=(NBLK,),
        in_specs=[s["u"], s["row"], s["row"], s["row"], s["bexp"], s["bexp"], s["cexp"], s["cexp"], s["d"]],
        out_specs=[s["u"], s["h"], s["h"]],
        out_shape=[jax.ShapeDtypeStruct((t_rows, BW), F32), jax.ShapeDtypeStruct((t_rows, NGRP * NSTATE), F32),
                   jax.ShapeDtypeStruct((t_rows, NGRP * NSTATE), F32)],
        compiler_params=_cparams(("parallel",)))(
            z, p["are"], p["aim"], p["ldt"], p["bexp_re"], p["bexp_im"], p["cexp_re"], p["cexp_im"], p["dskip"])


def _ssm_bwd(dy, z, hre, him, p):
    t_rows = z.shape[0]
    s = _ssm_specs(t_rows)
    nstates = NGRP * NSTATE

    def body(dy_ref, u_ref, hr_ref, hi_ref, are_ref, aim_ref, ldt_ref, br_ref, bi_ref, cr_ref, ci_ref, d_ref,
             du_ref, dbr_ref, dbi_ref, dcr_ref, dci_ref, dd_ref, dar_ref, dai_ref, dldt_ref, lr_ref, li_ref):
        rows3 = (are_ref[...], aim_ref[...], ldt_ref[...])
        (abr, abi, fr, fi), disc_vjp = jax.vjp(_disc, *rows3)
        (bbr, bbi), bbar_vjp = jax.vjp(_bbar, fr, fi, br_ref[...], bi_ref[...])
        dy = dy_ref[...]
        u = u_ref[...]
        lr_ref[...] = _dot_nt(dy, cr_ref[...])
        li_ref[...] = -_dot_nt(dy, ci_ref[...])
        dcr_ref[...] = _dot_tn(hr_ref[...], dy)
        dci_ref[...] = -_dot_tn(hi_ref[...], dy)
        dd_ref[...] = jnp.sum(dy * u, axis=0, keepdims=True)
        acc_r, acc_i = _scan_rows(lr_ref, li_ref, abr, -abi, t_rows, True, hr_ref, hi_ref)
        dabr = jnp.sum(acc_r, axis=0, keepdims=True)
        dabi = jnp.sum(acc_i, axis=0, keepdims=True)
        lam_r = lr_ref[...]
        lam_i = li_ref[...]
        du = d_ref[...] * dy + _dot_nt(lam_r, bbr) + _dot_nt(lam_i, bbi)
        du_ref[...] = du.astype(BF16)
        dbbr = _dot_tn(u, lam_r)
        dbbi = _dot_tn(u, lam_i)
        dfr, dfi, dbr, dbi = bbar_vjp((dbbr, dbbi))
        dbr_ref[...] = dbr
        dbi_ref[...] = dbi
        dar, dai, dldt = disc_vjp((dabr, dabi, dfr, dfi))
        dar_ref[...] = dar
        dai_ref[...] = dai
        lane_grp = lax.broadcasted_iota(jnp.int32, (NS, 128), 0) // NSTATE
        col = lax.broadcasted_iota(jnp.int32, (NS, 128), 1)
        seg = jnp.where(lane_grp == col, 1.0, 0.0).astype(F32)
        dldt_ref[...] = jnp.dot(jnp.broadcast_to(dldt, (8, NS)), seg, preferred_element_type=F32,
                                precision=lax.Precision.HIGHEST)

    dyspec = pl.BlockSpec((t_rows, GB * SGRP), lambda j: (0, j))
    return _call(
        body, name="ssm_bwd", grid=(NBLK,),
        in_specs=[dyspec, s["u"], s["h"], s["h"], s["row"], s["row"], s["row"], s["bexp"], s["bexp"], s["cexp"],
                  s["cexp"], s["d"]],
        out_specs=[dyspec, s["bexp"], s["bexp"], s["cexp"], s["cexp"], s["d"], s["row"], s["row"],
                   pl.BlockSpec((8, 128), lambda j: (j, 0))],
        out_shape=[jax.ShapeDtypeStruct((t_rows, BW), BF16),
                   jax.ShapeDtypeStruct((NBLK, GB * SGRP, NS), F32), jax.ShapeDtypeStruct((NBLK, GB * SGRP, NS), F32),
                   jax.ShapeDtypeStruct((NBLK, NS, GB * SGRP), F32), jax.ShapeDtypeStruct((NBLK, NS, GB * SGRP), F32),
                   jax.ShapeDtypeStruct((1, BW), F32), jax.ShapeDtypeStruct((1, nstates), F32),
                   jax.ShapeDtypeStruct((1, nstates), F32), jax.ShapeDtypeStruct((NBLK * 8, 128), F32)],
        scratch_shapes=[pltpu.VMEM((t_rows, NS), F32), pltpu.VMEM((t_rows, NS), F32)],
        compiler_params=_cparams(("parallel",)))(
            dy, z, hre, him, p["are"], p["aim"], p["ldt"], p["bexp_re"], p["bexp_im"], p["cexp_re"], p["cexp_im"],
            p["dskip"])


def _conv_fwd(z, w, b, tm=256):
    t_rows = z.shape[0]
    hb = tm // HALO

    def body(va_ref, vb_ref, ha_ref, hb_ref, w_ref, b_ref, o_ref, win_ref):
        live = jnp.where(pl.program_id(0) > 0, 1.0, 0.0)
        win_ref[0:HALO, :] = ha_ref[...] * jax.nn.sigmoid(hb_ref[...]) * live
        win_ref[HALO:HALO + tm, :] = va_ref[...] * jax.nn.sigmoid(vb_ref[...])
        acc = jnp.broadcast_to(b_ref[...], (tm, BW))
        for k in range(CONV_K):
            acc = acc + w_ref[k:k + 1, :] * win_ref[pl.ds(HALO - (CONV_K - 1) + k, tm), :]
        o_ref[...] = acc

    halo = lambda col: pl.BlockSpec((HALO, BW), lambda i: (jnp.maximum(i * hb - 1, 0), col))
    return _call(
        body, name="conv_fwd", grid=(t_rows // tm,),
        in_specs=[pl.BlockSpec((tm, BW), lambda i: (i, 1)), pl.BlockSpec((tm, BW), lambda i: (i, 2)), halo(1), halo(2),
                  pl.BlockSpec((CONV_K, BW), lambda i: (0, 0)), pl.BlockSpec((1, BW), lambda i: (0, 0))],
        out_specs=pl.BlockSpec((tm, BW), lambda i: (i, 0)),
        out_shape=jax.ShapeDtypeStruct((t_rows, BW), F32),
        scratch_shapes=[pltpu.VMEM((HALO + tm, BW), F32)],
        compiler_params=_cparams(("parallel",)))(z, z, z, z, w, b)


def _conv_bwd(dcv, z, w, tm=256):
    t_rows = z.shape[0]
    nt = t_rows // tm
    hb = tm // HALO
    csh = BW // NDEV

    def body(d_ref, dn_ref, va_ref, vb_ref, ha_ref, hb_ref, w_ref, dva_ref, dvb_ref, dw8_ref, db_ref,
             hwin_ref, dwin_ref, dw_ref):
        i = pl.program_id(0)

        @pl.when(i == 0)
        def _():
            dw_ref[...] = jnp.zeros_like(dw_ref)
            db_ref[...] = jnp.zeros_like(db_ref)

        live_prev = jnp.where(i > 0, 1.0, 0.0)
        live_next = jnp.where(i < nt - 1, 1.0, 0.0)
        va = va_ref[...]
        sig = jax.nn.sigmoid(vb_ref[...])
        hwin_ref[0:HALO, :] = ha_ref[...] * jax.nn.sigmoid(hb_ref[...]) * live_prev
        hwin_ref[HALO:HALO + tm, :] = va * sig
        d = d_ref[...]
        dwin_ref[0:tm, :] = d
        dwin_ref[tm:tm + HALO, :] = dn_ref[...] * live_next
        dh = jnp.zeros((tm, BW), F32)
        dws = []
        for k in range(CONV_K):
            dh = dh + w_ref[k:k + 1, :] * dwin_ref[pl.ds(CONV_K - 1 - k, tm), :]
            dws.append(jnp.sum(d * hwin_ref[pl.ds(HALO - (CONV_K - 1) + k, tm), :], axis=0, keepdims=True))
        dws.append(jnp.zeros((1, BW), F32))
        dw_ref[...] += jnp.concatenate(dws, axis=0)
        db_ref[...] += jnp.sum(d, axis=0, keepdims=True)
        dva_ref[...] = (dh * sig).astype(BF16)
        dvb_ref[...] = (dh * va * sig * (1.0 - sig)).astype(BF16)

        @pl.when(i == nt - 1)
        def _():
            acc = dw_ref[...]
            for q in range(NDEV):
                dw8_ref[q] = acc[:, csh * q:csh * (q + 1)]

    halo = lambda col: pl.BlockSpec((HALO, BW), lambda i: (jnp.maximum(i * hb - 1, 0), col))
    return _call(
        body, name="conv_bwd", grid=(nt,),
        in_specs=[pl.BlockSpec((tm, BW), lambda i: (i, 0)),
                  pl.BlockSpec((HALO, BW), lambda i: (jnp.minimum((i + 1) * hb, t_rows // HALO - 1), 0)),
                  pl.BlockSpec((tm, BW), lambda i: (i, 1)), pl.BlockSpec((tm, BW), lambda i: (i, 2)), halo(1), halo(2),
                  pl.BlockSpec((CONV_K, BW), lambda i: (0, 0))],
        out_specs=[pl.BlockSpec((tm, BW), lambda i: (i, 0)), pl.BlockSpec((tm, BW), lambda i: (i, 0)),
                   pl.BlockSpec((NDEV, 32, csh), lambda i: (0, 0, 0)), pl.BlockSpec((1, BW), lambda i: (0, 0))],
        out_shape=[jax.ShapeDtypeStruct((t_rows, BW), BF16), jax.ShapeDtypeStruct((t_rows, BW), BF16),
                   jax.ShapeDtypeStruct((NDEV, 32, csh), F32), jax.ShapeDtypeStruct((1, BW), F32)],
        scratch_shapes=[pltpu.VMEM((HALO + tm, BW), F32), pltpu.VMEM((tm + HALO, BW), F32), pltpu.VMEM((32, BW), F32)],
        compiler_params=_cparams(("arbitrary",)))(dcv, dcv, z, z, z, z, w)


def _pool_rows(i, tm, n_rows, first_row):
    grp = lax.broadcasted_iota(jnp.int32, (1, BW), 1) // (BW // 4)
    wlen = jnp.where(grp == 0, 2.0, jnp.where(grp == 1, 4.0, jnp.where(grp == 2, 8.0, 16.0)))
    t = (i * tm + first_row + lax.broadcasted_iota(jnp.int32, (n_rows, 1), 0)).astype(F32)
    return grp, 1.0 / jnp.minimum(t + 1.0, wlen)


def _pool_pick(grp, s2, s4, s8, s16):
    return jnp.where(grp == 0, s2, jnp.where(grp == 1, s4, jnp.where(grp == 2, s8, s16)))


def _pool_fwd(z, tm=256):
    t_rows = z.shape[0]
    hb = tm // PHALO

    def body(u_ref, h_ref, o_ref):
        i = pl.program_id(0)
        u = u_ref[...]
        win = jnp.concatenate([h_ref[...] * jnp.where(i > 0, 1.0, 0.0), u], axis=0)
        s2 = win + pltpu.roll(win, 1, 0)
        s4 = s2 + pltpu.roll(s2, 2, 0)
        s8 = s4 + pltpu.roll(s4, 4, 0)
        s16 = s8 + pltpu.roll(s8, 8, 0)
        grp, inv = _pool_rows(i, tm, tm, 0)
        o_ref[...] = _pool_pick(grp, s2, s4, s8, s16)[PHALO:, :] * inv - u

    return _call(
        body, name="pool_fwd", grid=(t_rows // tm,),
        in_specs=[pl.BlockSpec((tm, BW), lambda i: (i, 3)),
                  pl.BlockSpec((PHALO, BW), lambda i: (jnp.maximum(i * hb - 1, 0), 3))],
        out_specs=pl.BlockSpec((tm, BW), lambda i: (i, 0)),
        out_shape=jax.ShapeDtypeStruct((t_rows, BW), F32),
        compiler_params=_cparams(("parallel",)))(z, z)


def _pool_bwd(dp, tm=256):
    t_rows = dp.shape[0]
    nt = t_rows // tm
    hb = tm // PHALO
    ln = tm + PHALO

    def body(d_ref, dn_ref, o_ref):
        i = pl.program_id(0)
        d = d_ref[...]
        grp, inv = _pool_rows(i, tm, ln, 0)
        win = jnp.concatenate([d, dn_ref[...] * jnp.where(i < nt - 1, 1.0, 0.0)], axis=0) * inv
        s2 = win + pltpu.roll(win, ln - 1, 0)
        s4 = s2 + pltpu.roll(s2, ln - 2, 0)
        s8 = s4 + pltpu.roll(s4, ln - 4, 0)
        s16 = s8 + pltpu.roll(s8, ln - 8, 0)
        o_ref[...] = (_pool_pick(grp, s2, s4, s8, s16)[:tm, :] - d).astype(BF16)

    return _call(
        body, name="pool_bwd", grid=(nt,),
        in_specs=[pl.BlockSpec((tm, BW), lambda i: (i, 0)),
                  pl.BlockSpec((PHALO, BW), lambda i: (jnp.minimum((i + 1) * hb, t_rows // PHALO - 1), 0))],
        out_specs=pl.BlockSpec((tm, BW), lambda i: (i, 0)),
        out_shape=jax.ShapeDtypeStruct((t_rows, BW), BF16),
        compiler_params=_cparams(("parallel",)))(dp, dp)


_MERGE_W = ("wglu", "bglu", "wpa", "lng", "lnb", "wpb", "wgrp", "scale", "wpc", "bgate", "wout")
_MERGE_SMALL = ("bglu", "lng", "lnb", "scale", "bgate")


def _merge_math(x, yssm, cv, pbar, zg, w, taps):
    t_glu, t_ya, t_yb, t_p, t_yc = taps
    g = jax.nn.gelu(yssm)
    outa = g * jax.nn.sigmoid(_mm(g, w["wglu"]) + t_glu + w["bglu"])
    ya = _mm(outa, w["wpa"]) + t_ya
    mu = jnp.mean(cv, axis=-1, keepdims=True)
    var = jnp.mean(jnp.square(cv - mu), axis=-1, keepdims=True)
    hs = jax.nn.silu((cv - mu) * lax.rsqrt(var + EPS) * w["lng"] + w["lnb"])
    yb = _mm(hs, w["wpb"]) + t_yb
    gw = BW // 4
    pk = jnp.concatenate([_mm(pbar[:, gw * k:gw * (k + 1)], w["wgrp"][k]) for k in range(4)], axis=1) + t_p
    pc = pk * w["scale"]
    yc = _mm(pc, w["wpc"]) + t_yc
    gates = jax.nn.sigmoid(zg + w["bgate"])
    merged = gates[:, :D] * ya + gates[:, D:2 * D] * yb + gates[:, 2 * D:] * yc
    x1 = x + _mm(merged, w["wout"])
    acts = tuple(a.astype(BF16) for a in (g, outa, hs, pbar, pc, merged))
    return x1, acts


def _merge_specs(tm, p):
    rows = lambda width, col=0: pl.BlockSpec((tm, width), lambda i, c=col: (i, c))
    data = [rows(D), rows(BW), rows(BW), rows(BW), rows(D, 2), rows(D, 3), rows(D, 4)]
    wspecs = []
    for name in _MERGE_W:
        nd = p[name].ndim
        wspecs.append(pl.BlockSpec(p[name].shape, lambda i, nd=nd: (0,) * nd))
    return rows, data, wspecs


def _merge_fwd(x, yssm, cv, pbar, z, p, tm=256):
    t_rows = x.shape[0]
    rows, data, wspecs = _merge_specs(tm, p)

    def body(x_ref, y_ref, cv_ref, pb_ref, za_ref, zb_ref, zc_ref, *rest):
        w = {name: r[...] for name, r in zip(_MERGE_W, rest[:len(_MERGE_W)])}
        o_ref = rest[len(_MERGE_W)]
        taps = (0.0, 0.0, 0.0, 0.0, 0.0)
        zg = jnp.concatenate([za_ref[...], zb_ref[...], zc_ref[...]], axis=1)
        o_ref[...] = _merge_math(x_ref[...], y_ref[...], cv_ref[...], pb_ref[...], zg, w, taps)[0]

    return _call(
        body, name="merge_fwd", grid=(t_rows // tm,), in_specs=data + wspecs, out_specs=rows(D),
        out_shape=jax.ShapeDtypeStruct((t_rows, D), F32),
        compiler_params=_cparams(("parallel",)))(x, yssm, cv, pbar, z, z, z, *[p[n] for n in _MERGE_W])


def _merge_bwd(dx1, x, yssm, cv, pbar, z, p, token, tm=256):
    t_rows = x.shape[0]
    rows, data, wspecs = _merge_specs(tm, p)
    nw = len(_MERGE_W)

    def body(dx_ref, x_ref, y_ref, cv_ref, pb_ref, za_ref, zb_ref, zc_ref, *rest):
        w = {name: r[...] for name, r in zip(_MERGE_W, rest[:nw])}
        zg = jnp.concatenate([za_ref[...], zb_ref[...], zc_ref[...]], axis=1)
        outs = rest[nw + 1:]
        small = {n: w[n] for n in _MERGE_SMALL}
        taps = (jnp.zeros((tm, BW), F32), jnp.zeros((tm, D), F32), jnp.zeros((tm, D), F32),
                jnp.zeros((tm, BW), F32), jnp.zeros((tm, D), F32))

        def f(yssm_, cv_, pbar_, zg_, small_, taps_):
            return _merge_math(x_ref[...], yssm_, cv_, pbar_, zg_, {**w, **small_}, taps_)

        _, vjp, acts = jax.vjp(f, y_ref[...], cv_ref[...], pb_ref[...], zg, small, taps, has_aux=True)
        dy, dcv, dpb, dzg, dsmall, dtaps = vjp(dx_ref[...])
        outs[0][...] = dy
        outs[1][...] = dcv
        outs[2][...] = dpb
        outs[3][...] = dzg.astype(BF16)
        for k in range(6):
            outs[4 + k][...] = acts[k]
        for k in range(5):
            outs[10 + k][...] = dtaps[k].astype(BF16)

        @pl.when(pl.program_id(0) == 0)
        def _():
            for k in range(5):
                outs[15 + k][...] = jnp.zeros_like(outs[15 + k])

        for k, n in enumerate(_MERGE_SMALL):
            outs[15 + k][...] += dsmall[n]

    f32o = lambda width: jax.ShapeDtypeStruct((t_rows, width), F32)
    bfo = lambda width: jax.ShapeDtypeStruct((t_rows, width), BF16)
    small_shapes = [jax.ShapeDtypeStruct(p[n].shape, F32) for n in _MERGE_SMALL]
    small_specs = [pl.BlockSpec(p[n].shape, lambda i: (0, 0)) for n in _MERGE_SMALL]
    out_shape = ([f32o(BW), f32o(BW), f32o(BW), bfo(3 * D)]
                 + [bfo(BW), bfo(BW), bfo(BW), bfo(BW), bfo(BW), bfo(D)]
                 + [bfo(BW), bfo(D), bfo(D), bfo(BW), bfo(D)] + small_shapes)
    out_specs = ([rows(BW), rows(BW), rows(BW), rows(3 * D)]
                 + [rows(BW)] * 5 + [rows(D)]
                 + [rows(BW), rows(D), rows(D), rows(BW), rows(D)] + small_specs)
    return _call(
        body, name="merge_bwd", grid=(t_rows // tm,),
        in_specs=[rows(D)] + data + wspecs + [pl.BlockSpec((8, 128), lambda i: (0, 0))], out_specs=out_specs,
        out_shape=out_shape, compiler_params=_cparams(("arbitrary",)))(
            dx1, x, yssm, cv, pbar, z, z, z, *[p[n] for n in _MERGE_W], token)


def _ffn_fwd(x1, gamma, wg, wu, wd, tm=512, th=512):
    t_rows = x1.shape[0]
    nh = HIDP // th

    def body(x_ref, g_ref, wg_ref, wu_ref, wd_ref, o_ref, gp_ref, up_ref, h_ref, acc_ref):
        j = pl.program_id(1)

        @pl.when(j == 0)
        def _():
            h_ref[...] = _rms(x_ref[...], g_ref[...]).astype(BF16)
            acc_ref[...] = jnp.zeros_like(acc_ref)

        gp = _dot_nt(h_ref[...], wg_ref[...])
        up = _dot_nt(h_ref[...], wu_ref[...])
        gp_ref[...] = gp
        up_ref[...] = up
        acc_ref[...] += _dot(jax.nn.silu(gp) * up, wd_ref[...])

        @pl.when(j == nh - 1)
        def _():
            o_ref[...] = x_ref[...] + acc_ref[...]

    return _call(
        body, name="ffn_fwd", grid=(t_rows // tm, nh),
        in_specs=[pl.BlockSpec((tm, D), lambda i, j: (i, 0)), pl.BlockSpec((1, D), lambda i, j: (0, 0)),
                  pl.BlockSpec((th, D), lambda i, j: (j, 0)), pl.BlockSpec((th, D), lambda i, j: (j, 0)),
                  pl.BlockSpec((th, D), lambda i, j: (j, 0))],
        out_specs=[pl.BlockSpec((tm, D), lambda i, j: (i, 0)), pl.BlockSpec((tm, th), lambda i, j: (i, j)),
                   pl.BlockSpec((tm, th), lambda i, j: (i, j))],
        out_shape=[jax.ShapeDtypeStruct((t_rows, D), F32), jax.ShapeDtypeStruct((t_rows, HIDP), F32),
                   jax.ShapeDtypeStruct((t_rows, HIDP), F32)],
        scratch_shapes=[pltpu.VMEM((tm, D), BF16), pltpu.VMEM((tm, D), F32)],
        compiler_params=_cparams(("parallel", "arbitrary")))(x1, gamma, wg, wu, wd)


def _rms_bwd_tail(x, gamma, dh):
    _, vjp = jax.vjp(_rms, x, gamma)
    return vjp(dh)


def _ffn_bwd(dx2, x1, gamma, gpre, upre, wg, wu, wd, token, tm=512, th=512):
    t_rows = x1.shape[0]
    nh = HIDP // th

    def body(d_ref, x_ref, g_ref, gp_ref, up_ref, wg_ref, wu_ref, wd_ref, token_ref,
             dx_ref, dgam_ref, dgp_ref, dup_ref, act_ref, h_ref, acc_ref):
        i = pl.program_id(0)
        j = pl.program_id(1)

        @pl.when(j == 0)
        def _():
            acc_ref[...] = jnp.zeros_like(acc_ref)

        @pl.when((i == 0) & (j == 0))
        def _():
            dgam_ref[...] = jnp.zeros_like(dgam_ref)

        dact = _dot_nt(d_ref[...], wd_ref[...])
        gp = gp_ref[...]
        up = up_ref[...]
        sg = jax.nn.sigmoid(gp)
        silu = gp * sg
        dgp = (dact * up * (sg * (1.0 + gp * (1.0 - sg)))).astype(BF16)
        dup = (dact * silu).astype(BF16)
        dgp_ref[...] = dgp
        dup_ref[...] = dup
        act_ref[...] = (silu * up).astype(BF16)
        acc_ref[...] += _dot(dgp, wg_ref[...]) + _dot(dup, wu_ref[...])

        @pl.when(j == nh - 1)
        def _():
            x = x_ref[...]
            h_ref[...] = _rms(x, g_ref[...]).astype(BF16)
            dx, dgam = _rms_bwd_tail(x, g_ref[...], acc_ref[...])
            dx_ref[...] = d_ref[...] + dx
            dgam_ref[...] += dgam

    row_d = pl.BlockSpec((tm, D), lambda i, j: (i, 0))
    row_h = pl.BlockSpec((tm, th), lambda i, j: (i, j))
    return _call(
        body, name="ffn_bwd", grid=(t_rows // tm, nh),
        in_specs=[row_d, row_d, pl.BlockSpec((1, D), lambda i, j: (0, 0)), row_h, row_h,
                  pl.BlockSpec((th, D), lambda i, j: (j, 0)), pl.BlockSpec((th, D), lambda i, j: (j, 0)),
                  pl.BlockSpec((th, D), lambda i, j: (j, 0)), TOKEN_SPEC],
        out_specs=[row_d, pl.BlockSpec((1, D), lambda i, j: (0, 0)), row_h, row_h, row_h, row_d],
        out_shape=[jax.ShapeDtypeStruct((t_rows, D), F32), jax.ShapeDtypeStruct((1, D), F32),
                   jax.ShapeDtypeStruct((t_rows, HIDP), BF16), jax.ShapeDtypeStruct((t_rows, HIDP), BF16),
                   jax.ShapeDtypeStruct((t_rows, HIDP), BF16), jax.ShapeDtypeStruct((t_rows, D), BF16)],
        scratch_shapes=[pltpu.VMEM((tm, D), F32)],
        compiler_params=_cparams(("arbitrary", "arbitrary")))(dx2, x1, gamma, gpre, upre, wg, wu, wd, token)


def _inproj_bwd(dz, dx1, x, gamma, w, tm=512, tn=1280):
    t_rows = x.shape[0]
    nn = IN_W // tn

    def body(dz_ref, d1_ref, x_ref, g_ref, w_ref, dx_ref, dgam_ref, h_ref, acc_ref):
        i = pl.program_id(0)
        j = pl.program_id(1)

        @pl.when(j == 0)
        def _():
            acc_ref[...] = jnp.zeros_like(acc_ref)

        @pl.when((i == 0) & (j == 0))
        def _():
            dgam_ref[...] = jnp.zeros_like(dgam_ref)

        acc_ref[...] += _dot_nt(dz_ref[...], w_ref[...])

        @pl.when(j == nn - 1)
        def _():
            x = x_ref[...]
            h_ref[...] = _rms(x, g_ref[...]).astype(BF16)
            dx, dgam = _rms_bwd_tail(x, g_ref[...], acc_ref[...])
            dx_ref[...] = d1_ref[...] + dx
            dgam_ref[...] += dgam

    row_d = pl.BlockSpec((tm, D), lambda i, j: (i, 0))
    return _call(
        body, name="inproj_bwd", grid=(t_rows // tm, nn),
        in_specs=[pl.BlockSpec((tm, tn), lambda i, j: (i, j)), row_d, row_d, pl.BlockSpec((1, D), lambda i, j: (0, 0)),
                  pl.BlockSpec((D, tn), lambda i, j: (0, j))],
        out_specs=[row_d, pl.BlockSpec((1, D), lambda i, j: (0, 0)), row_d],
        out_shape=[jax.ShapeDtypeStruct((t_rows, D), F32), jax.ShapeDtypeStruct((1, D), F32),
                   jax.ShapeDtypeStruct((t_rows, D), BF16)],
        scratch_shapes=[pltpu.VMEM((tm, D), F32)],
        compiler_params=_cparams(("arbitrary", "arbitrary")))(dz, dx1, x, gamma, w)


def _matmul_tn(a, b, name, owner_cols=None, tt=512):
    t_rows, k = a.shape
    n = b.shape[1]
    tk = min(k, 1024)
    nt = t_rows // tt
    if owner_cols is None:
        tn, nb = min(n, 512), None
        out_spec = pl.BlockSpec((tk, tn), lambda i, j, t: (i, j))
        out_shape = jax.ShapeDtypeStruct((k, n), BF16)
    else:
        nb = max(1, 512 // owner_cols)
        tn = nb * owner_cols
        out_spec = pl.BlockSpec((nb, tk, owner_cols), lambda i, j, t: (j, i, 0))
        out_shape = jax.ShapeDtypeStruct((n // owner_cols, k, owner_cols), BF16)

    def body(a_ref, b_ref, o_ref, acc_ref):
        t = pl.program_id(2)

        @pl.when(t == 0)
        def _():
            acc_ref[...] = jnp.zeros_like(acc_ref)

        acc_ref[...] += _dot_tn(a_ref[...], b_ref[...])

        @pl.when(t == nt - 1)
        def _():
            if nb is None:
                o_ref[...] = acc_ref[...].astype(BF16)
            else:
                for q in range(nb):
                    o_ref[q] = acc_ref[:, owner_cols * q:owner_cols * (q + 1)].astype(BF16)

    return _call(
        body, name=name, grid=(k // tk, n // tn, nt),
        in_specs=[pl.BlockSpec((tt, tk), lambda i, j, t: (t, i)), pl.BlockSpec((tt, tn), lambda i, j, t: (t, j))],
        out_specs=out_spec, out_shape=out_shape,
        scratch_shapes=[pltpu.VMEM((tk, tn), F32)],
        compiler_params=_cparams(("parallel", "parallel", "arbitrary")))(a, b)


def _group_tn(a, b):
    t_rows = a.shape[0]
    gw = BW // 4

    def body(a_ref, b_ref, o_ref):
        o_ref[...] = _dot_tn(a_ref[...], b_ref[...])

    return _call(
        body, name="pool_group_tn", grid=(4,),
        in_specs=[pl.BlockSpec((t_rows, gw), lambda k: (0, k)), pl.BlockSpec((t_rows, gw), lambda k: (0, k))],
        out_specs=pl.BlockSpec((None, gw, gw), lambda k: (k, 0, 0)),
        out_shape=jax.ShapeDtypeStruct((4, gw, gw), F32),
        compiler_params=_cparams(("parallel",)))(a, b)


def _loss_head(x2, gamma, target, tm=512):
    t_rows = x2.shape[0]

    def body(x_ref, g_ref, t_ref, loss_ref, dx_ref, dgam_ref):
        @pl.when(pl.program_id(0) == 0)
        def _():
            loss_ref[...] = jnp.zeros_like(loss_ref)
            dgam_ref[...] = jnp.zeros_like(dgam_ref)

        def f(x, g):
            err = jnp.square(_rms(x, g) - t_ref[...])
            return 0.5 * jnp.sum(jnp.mean(err, axis=-1, keepdims=True), axis=0, keepdims=True)

        loss, vjp = jax.vjp(f, x_ref[...], g_ref[...])
        dx, dgam = vjp(jnp.ones((1, 1), F32))
        loss_ref[...] += jnp.broadcast_to(loss, (1, 128))
        dx_ref[...] = dx
        dgam_ref[...] += dgam

    row_d = pl.BlockSpec((tm, D), lambda i: (i, 0))
    return _call(
        body, name="loss_head", grid=(t_rows // tm,),
        in_specs=[row_d, pl.BlockSpec((1, D), lambda i: (0, 0)), row_d],
        out_specs=[pl.BlockSpec((1, 128), lambda i: (0, 0)), row_d, pl.BlockSpec((1, D), lambda i: (0, 0))],
        out_shape=[jax.ShapeDtypeStruct((1, 128), F32), jax.ShapeDtypeStruct((t_rows, D), F32),
                   jax.ShapeDtypeStruct((1, D), F32)],
        compiler_params=_cparams(("arbitrary",)))(x2, gamma, target)


NCHIP = NDEV // 2


def _coords():
    return lax.axis_index("x"), lax.axis_index("y"), lax.axis_index("c")


def _remote(src, dst, send_sem, recv_sem, peer):
    return pltpu.make_async_remote_copy(src_ref=src, dst_ref=dst, send_sem=send_sem, recv_sem=recv_sem,
                                        device_id=peer, device_id_type=MESH)


def _comm_call(name, srcs, out_shapes, n_rec, plan, aliases=None):
    ns, no = len(srcs), len(out_shapes)

    def body(*refs):
        ins, outs = refs[:ns], refs[ns:ns + no]
        loc_sem, send_sem, recv_sem = refs[ns + no:]
        x, y, c = _coords()
        recs = plan(ins, outs, x, y, c)
        assert len(recs) == n_rec
        for k, r in enumerate(recs):
            for src, dst in r.get("local", ()):
                pltpu.make_async_copy(src, dst, loc_sem.at[k]).start()
            for peer, src, dst in r.get("remote", ()):
                _remote(src, dst, send_sem.at[k], recv_sem.at[k], peer).start()
        for k, r in enumerate(recs):
            if r.get("recv_wait") is not None:
                w = r["recv_wait"]
                _remote(w, w, send_sem.at[k], recv_sem.at[k], (x, y, c)).wait_recv()
            if r.get("send_wait") is not None:
                w = r["send_wait"]
                _remote(w, w, send_sem.at[k], recv_sem.at[k], (x, y, c)).wait_send()
            if r.get("local_wait") is not None:
                w = r["local_wait"]
                pltpu.make_async_copy(w, w, loc_sem.at[k]).wait()

    return _call(
        body, name=name, in_specs=[ANY] * ns, out_specs=[ANY] * no, out_shape=out_shapes,
        input_output_aliases=aliases or {}, scratch_shapes=[pltpu.SemaphoreType.DMA((n_rec,))] * 3)(*srcs)


def _gather_call(srcs, out_shapes, items):
    ns, no, n = len(srcs), len(out_shapes), len(items)

    def body(*refs):
        ins, outs = refs[:ns], refs[ns:ns + no]
        loc, sib_s, sib_r, ici_s, ici_r, fwd_s, fwd_r = refs[ns + no:]
        x, y, c = _coords()
        me, sib = (x, y, c), (x, y, 1 - c)
        chips = [(1 - x, y), (x, 1 - y), (1 - x, 1 - y)]
        index = lambda px, py, pc: 4 * px + 2 * py + pc
        for k, (si, oi, shard, block, _) in enumerate(items):
            src, mine = shard(ins[si]), block(outs[oi], index(*me))
            pltpu.make_async_copy(src, mine, loc.at[k]).start()
            _remote(src, mine, sib_s.at[k], sib_r.at[k], sib).start()
            for chip in chips:
                _remote(src, mine, ici_s.at[k], ici_r.at[k], (*chip, c)).start()
        for k, (si, oi, _, block, blocks) in enumerate(items):
            three = blocks(outs[oi], 3)
            _remote(three, three, ici_s.at[k], ici_r.at[k], me).wait_recv()
            for chip in chips:
                landed = block(outs[oi], index(*chip, c))
                _remote(landed, landed, fwd_s.at[k], fwd_r.at[k], sib).start()
        for k, (si, oi, _, _, blocks) in enumerate(items):
            one, three = blocks(outs[oi], 1), blocks(outs[oi], 3)
            _remote(one, one, sib_s.at[k], sib_r.at[k], me).wait()
            _remote(three, three, fwd_s.at[k], fwd_r.at[k], me).wait()
            _remote(three, three, ici_s.at[k], ici_r.at[k], me).wait_send()
            pltpu.make_async_copy(one, one, loc.at[k]).wait()

    return _call(
        body, name="gather_weights", in_specs=[ANY] * ns, out_specs=[ANY] * no, out_shape=out_shapes,
        scratch_shapes=[pltpu.SemaphoreType.DMA((n,))] * 7)(*srcs)


_BIG = {
    "w_in": (1, D, IN_W // NDEV, D, IN_W),
    "ssm_w_glu": (0, BW // NDEV, BW, BW, BW),
    "ssm_w_proj": (1, BW, D // NDEV, BW, D),
    "conv_w_proj": (1, BW, D // NDEV, BW, D),
    "pool_w_proj": (1, BW, D // NDEV, BW, D),
    "w_out": (0, D // NDEV, D, D, D),
    "ffn_w_gate": (0, HPAD, D, HIDP, D),
    "ffn_w_up": (0, HPAD, D, HIDP, D),
    "ffn_w_down": (0, HPAD, D, HIDP, D),
}


def _block_view(axis, size):
    if axis == 1:
        return lambda ref, q: ref.at[:, pl.ds(pl.multiple_of(q * size, 128), size)]
    return lambda ref, q: ref.at[pl.ds(pl.multiple_of(q * size, 16), size), :]


def _blocks_view(axis, size):
    if axis == 1:
        return lambda ref, n: ref.at[:, pl.ds(0, n * size)]
    return lambda ref, n: ref.at[pl.ds(0, n * size), :]


def _gather_weights(shards, conv_dw, layers):
    srcs, outs, items, where = [], [], [], {}
    for name, (axis, kk, nn, kf, nf) in _BIG.items():
        srcs.append(shards[name])
        size = nn if axis == 1 else kk
        for l in layers:
            where[(name, l)] = len(outs)
            outs.append(jax.ShapeDtypeStruct((kf, nf), BF16))
            items.append((len(srcs) - 1, len(outs) - 1, lambda ref, l=l: ref.at[l], _block_view(axis, size),
                          _blocks_view(axis, size)))
    srcs.append(conv_dw)
    outs.append(jax.ShapeDtypeStruct((NDEV,) + conv_dw.shape, conv_dw.dtype))
    items.append((len(srcs) - 1, len(outs) - 1, lambda ref: ref, lambda ref, q: ref.at[q],
                  lambda ref, n: ref.at[pl.ds(0, n)]))
    res = _gather_call(srcs, outs, items)
    full = {name: {l: res[where[(name, l)]] for l in layers} for name in _BIG}
    return full, res[-1]


def _gather_start(shards, layer, after):
    names = list(_BIG)
    lands = [jax.ShapeDtypeStruct((_BIG[n][3], _BIG[n][4]), BF16) for n in names]

    def copies(src_refs, land_refs, x, y, c):
        me = 4 * x + 2 * y + c
        peers = [(x, y, 1 - c), (1 - x, y, c), (x, 1 - y, c), (1 - x, 1 - y, c)]
        out = []
        for k, n in enumerate(names):
            axis, kk, nn, _, _ = _BIG[n]
            mine = _block_view(axis, nn if axis == 1 else kk)(land_refs[k], me)
            out.append([(peer, src_refs[k].at[layer], mine) for peer in peers])
        return out

    return _split_start("gather_start", [shards[n] for n in names], lands, copies, after)


def _gather_finish(handle, layer, after):
    names = list(_BIG)
    sizes = [(_BIG[n][0], _BIG[n][2] if _BIG[n][0] == 1 else _BIG[n][1]) for n in names]
    four = [functools.partial(lambda ref, bv: bv(ref, 4), bv=_blocks_view(axis, size)) for axis, size in sizes]
    lands = _split_wait("gather_wait", handle, four, after)
    n = len(names)

    def plan(ins, out_refs, x, y, c):
        me, sib = 4 * x + 2 * y + c, (x, y, 1 - c)
        chips = [(1 - x, y), (x, 1 - y), (1 - x, 1 - y)]
        recs = []
        for k, (axis, size) in enumerate(sizes):
            block, blocks = _block_view(axis, size), _blocks_view(axis, size)
            land = out_refs[k]
            remote = []
            for px, py in chips:
                landed = block(land, 4 * px + 2 * py + c)
                remote.append((sib, landed, landed))
            recs.append(dict(local=[(ins[n + k].at[layer], block(land, me))], remote=remote,
                             local_wait=blocks(land, 1), send_wait=blocks(land, 3), recv_wait=blocks(land, 3)))
        return recs

    res = _comm_call("gather_pair", lands + handle["srcs"], [jax.ShapeDtypeStruct(l.shape, l.dtype) for l in lands], n, plan,
                     aliases={k: k for k in range(n)})
    return dict(zip(names, res))


def _pair_add(name, grads, rcv, core):
    nl = len(grads)
    _, kk, nn = grads[0].shape

    def body(c_ref, *refs):
        l = pl.program_id(0)
        own = refs[0][...]
        for j in range(1, nl):
            own = jnp.where(l == j, refs[j][...], own)
        refs[nl + 1][...] = (own.astype(F32) + refs[nl][...].astype(F32)).astype(BF16)

    gspec = lambda j: pl.BlockSpec((None, kk, nn), lambda l, h, c_ref: (jnp.where(l == j, 2 * h + c_ref[0], 0), 0, 0))
    rspec = pl.BlockSpec((None, None, kk, nn), lambda l, h, c_ref: (h, l, 0, 0))
    return _call(
        body, name="pair_add_" + name,
        grid_spec=pltpu.PrefetchScalarGridSpec(num_scalar_prefetch=1, grid=(nl, NCHIP),
                                               in_specs=[gspec(j) for j in range(nl)] + [rspec], out_specs=rspec),
        out_shape=jax.ShapeDtypeStruct(rcv.shape, BF16),
        compiler_params=_cparams(("arbitrary", "arbitrary")))(core, *grads, rcv)


def _pair_add_small(owned, lists, core):
    on, ln = list(owned), list(lists)
    flat = []
    for n in on:
        flat += list(owned[n][0]) + [owned[n][1]]
    for n in ln:
        flat += list(lists[n][0]) + [lists[n][1]]

    def body(c_ref, *refs):
        outs = refs[len(flat):]
        c = c_ref[0]
        pos = 0
        for k, n in enumerate(on):
            nl = len(owned[n][0])
            for h in range(NCHIP):
                for l in range(nl):
                    outs[k][h, l] = refs[pos + l][pl.ds(2 * h + c, 1)][0] + refs[pos + nl][h, l]
            pos += nl + 1
        for k, n in enumerate(ln):
            nl = len(lists[n][0])
            for l in range(nl):
                outs[len(on) + k][l] = refs[pos + l][...] + refs[pos + nl][l]
            pos += nl + 1

    shapes = [jax.ShapeDtypeStruct(owned[n][1].shape, F32) for n in on]
    shapes += [jax.ShapeDtypeStruct(lists[n][1].shape, F32) for n in ln]
    res = _call(body, name="pair_add_small", out_shape=shapes,
                in_specs=[pl.BlockSpec(memory_space=pltpu.SMEM)] + [pl.BlockSpec(memory_space=pltpu.VMEM)] * len(flat),
                compiler_params=_cparams())(core, *flat)
    return dict(zip(on + ln, res))


def _pair_reduce(tag, big, by_owner, small, core):
    rs = {**big, **by_owner}
    srcs, outs, plans, rcv_at = [], [], [], {}
    for name, arrays in rs.items():
        rcv_at[name] = len(outs)
        outs.append(jax.ShapeDtypeStruct((NCHIP, len(arrays)) + arrays[0].shape[1:], arrays[0].dtype))
        for l, arr in enumerate(arrays):
            srcs.append(arr)
            plans.append((len(srcs) - 1, rcv_at[name], l, True))
    for name, arrays in small.items():
        rcv_at[name] = len(outs)
        outs.append(jax.ShapeDtypeStruct((len(arrays),) + arrays[0].shape, F32))
        for l, arr in enumerate(arrays):
            srcs.append(arr)
            plans.append((len(srcs) - 1, rcv_at[name], l, False))

    def plan_pair(ins, out_refs, x, y, c):
        sib = (x, y, 1 - c)
        recs = []
        for si, ro, l, slabs in plans:
            if slabs:
                four = out_refs[ro].at[pl.ds(0, NCHIP), l]
                recs.append(dict(remote=[(sib, ins[si].at[2 * h + 1 - c], out_refs[ro].at[h, l]) for h in range(NCHIP)],
                                 send_wait=four, recv_wait=four))
            else:
                dst = out_refs[ro].at[l]
                recs.append(dict(remote=[(sib, ins[si], dst)], send_wait=dst, recv_wait=dst))
        return recs

    res = _comm_call("pair_exchange_" + tag, srcs, outs, len(plans), plan_pair)
    part = {name: _pair_add(name, big[name], res[rcv_at[name]], core) for name in big}
    if by_owner or small:
        part.update(_pair_add_small({n: (by_owner[n], res[rcv_at[n]]) for n in by_owner},
                                    {n: (small[n], res[rcv_at[n]]) for n in small}, core))
    return part


def _chip_copies(src, land, slabbed, x, y, c):
    mine = 2 * x + y
    copies = []
    for step in range(1, NCHIP):
        h = (mine + step) % NCHIP
        copies.append(((h // 2, h % 2, c), src.at[h] if slabbed else src, land.at[mine]))
    return copies


def _chip_exchange(part, slabbed, keep_own):
    names = list(part)
    outs = [jax.ShapeDtypeStruct((() if n in slabbed else (NCHIP,)) + part[n].shape, part[n].dtype) for n in names]

    def plan(ins, out_refs, x, y, c):
        mine = 2 * x + y
        recs = []
        for k, n in enumerate(names):
            three = out_refs[k].at[pl.ds(0, NCHIP - 1)]
            rec = dict(remote=_chip_copies(ins[k], out_refs[k], n in slabbed, x, y, c), send_wait=three, recv_wait=three)
            if n in keep_own:
                rec["local"] = [(ins[k].at[mine] if n in slabbed else ins[k], out_refs[k].at[mine])]
                rec["local_wait"] = out_refs[k].at[0]
            recs.append(rec)
        return recs

    res = _comm_call("chip_exchange", [part[n] for n in names], outs, len(names), plan)
    return dict(zip(names, res))


HBM_SPEC = pl.BlockSpec(memory_space=pltpu.HBM)
SEM_SPEC = pl.BlockSpec(memory_space=pltpu.SEMAPHORE)
SPLIT_EFFECT = pltpu.SideEffectType.DATAFLOW_SIDE_EFFECTING


def _split_start(name, srcs, land_shapes, copies_fn, after):
    n = len(srcs)
    lands = [pltpu.with_memory_space_constraint(lax.empty(s.shape, s.dtype), pltpu.HBM) for s in land_shapes]

    def body(*refs):
        src_refs, land_refs = refs[:n], refs[n:2 * n]
        send_sem, recv_sem = refs[2 * n + 1], refs[2 * n + 2]
        token = refs[-1]
        x, y, c = _coords()
        for k, copies in enumerate(copies_fn(src_refs, land_refs, x, y, c)):
            for peer, src, dst in copies:
                _remote(src, dst, send_sem.at[k], recv_sem.at[k], peer).start()
        token[...] = jnp.zeros_like(token)

    res = pl.pallas_call(
        body, name=name,
        out_shape=(pltpu.SemaphoreType.DMA((n,)), pltpu.SemaphoreType.DMA((n,)),
                   *[pltpu.HBM(s.shape, s.dtype) for s in land_shapes], jax.ShapeDtypeStruct((8, 128), F32)),
        in_specs=[HBM_SPEC] * (2 * n) + [ANY],
        out_specs=(SEM_SPEC, SEM_SPEC, *[HBM_SPEC] * n, pl.BlockSpec(memory_space=pltpu.VMEM)),
        input_output_aliases={n + i: 2 + i for i in range(n)},
        compiler_params=pltpu.CompilerParams(has_side_effects=SPLIT_EFFECT),
    )(*[pltpu.with_memory_space_constraint(s, pltpu.HBM) for s in srcs], *lands, after)
    return dict(send=res[0], recv=res[1], srcs=list(srcs), lands=list(res[2:2 + n]), token=res[-1])


def _split_wait(name, handle, wait_views, after):
    n = len(handle["lands"])

    def body(*refs):
        land_refs = refs[:n]
        send_sem, recv_sem = refs[n], refs[n + 1]
        x, y, c = _coords()
        for k in range(n):
            w = wait_views[k](land_refs[k])
            cp = _remote(w, w, send_sem.at[k], recv_sem.at[k], (x, y, c))
            cp.wait_send()
            cp.wait_recv()

    res = pl.pallas_call(
        body, name=name,
        out_shape=tuple(pltpu.HBM(s.shape, s.dtype) for s in handle["lands"]),
        in_specs=[HBM_SPEC] * n + [SEM_SPEC, SEM_SPEC, ANY], out_specs=tuple([HBM_SPEC] * n),
        input_output_aliases={i: i for i in range(n)},
        compiler_params=pltpu.CompilerParams(has_side_effects=SPLIT_EFFECT),
    )(*handle["lands"], handle["send"], handle["recv"], after)
    return list(res)


def _adamw(w, g, m, v):
    m = ADAM_B1 * m + (1.0 - ADAM_B1) * g
    v = ADAM_B2 * v + (1.0 - ADAM_B2) * jnp.square(g)
    m_hat = m / (1.0 - ADAM_B1 ** ADAM_STEP)
    v_hat = v / (1.0 - ADAM_B2 ** ADAM_STEP)
    delta = -ADAM_LR * (m_hat / (jnp.sqrt(v_hat) + ADAM_EPS) + ADAM_WD * w)
    return delta, m, v


def _chip_start(part, tag):
    names = list(part)

    def copies(src_refs, land_refs, x, y, c):
        return [_chip_copies(src_refs[k], land_refs[k], True, x, y, c) for k in range(len(names))]

    shapes = [jax.ShapeDtypeStruct(part[n].shape, part[n].dtype) for n in names]
    return names, _split_start("chip_start_" + tag, [part[n] for n in names], shapes, copies, part[names[0]])


def _chip_wait(names, handle, after, tag):
    three = [lambda ref: ref.at[pl.ds(0, NCHIP - 1)]] * len(names)
    return dict(zip(names, _split_wait("chip_wait_" + tag, handle, three, after)))


def _sum_senders(ref):
    g = ref[0].astype(F32)
    for h in range(1, NCHIP):
        g = g + ref[h].astype(F32)
    return g


def _adam_big(name, own, recv, w, m, v, tk, chip):
    nl = len(own)
    kk, nn = w.shape[1], w.shape[2]
    nnp = own[0].shape[3]

    def body(chip_ref, *refs):
        l = pl.program_id(0)
        g = None
        for step in range(NCHIP):
            val = refs[step][...]
            for q in range(1, nl):
                val = jnp.where(l == q, refs[NCHIP * q + step][...], val)
            g = val.astype(F32) if g is None else g + val.astype(F32)
        w_ref, m_ref, v_ref, g_ref, d_ref, mo_ref, vo_ref = refs[NCHIP * nl:]
        g = g[:, :nn]
        delta, m2, v2 = _adamw(w_ref[...], g, m_ref[...], v_ref[...])
        g_ref[...] = g
        d_ref[...] = delta
        mo_ref[...] = m2
        vo_ref[...] = v2

    def slab(q, step):
        return pl.BlockSpec((None, None, tk, nnp), lambda l, i, chip_ref: (
            jnp.where(l == q, (chip_ref[0] + step) % NCHIP, 0), 0, jnp.where(l == q, i, 0), 0))

    in_specs, operands = [], []
    for q in range(nl):
        in_specs += [slab(q, step) for step in range(NCHIP)]
        operands += [own[q]] + [recv[q]] * (NCHIP - 1)
    wspec = pl.BlockSpec((None, tk, nn), lambda l, i, chip_ref: (l, i, 0))
    shape = jax.ShapeDtypeStruct(w.shape, F32)
    return _call(
        body, name="adamw_" + name,
        grid_spec=pltpu.PrefetchScalarGridSpec(num_scalar_prefetch=1, grid=(nl, kk // tk),
                                               in_specs=in_specs + [wspec] * 3, out_specs=[wspec] * 4),
        out_shape=[shape] * 4, compiler_params=_cparams(("arbitrary", "arbitrary")))(chip, *operands, w, m, v)


def _adam_small(names, recv, w, m, v):
    n = len(names)

    def body(*refs):
        r, ww, mm, vv = refs[:n], refs[n:2 * n], refs[2 * n:3 * n], refs[3 * n:4 * n]
        outs = refs[4 * n:]
        for k in range(n):
            g = _sum_senders(r[k])
            if g.shape != ww[k].shape:
                g = g[:, :ww[k].shape[1]]
            delta, m2, v2 = _adamw(ww[k][...], g, mm[k][...], vv[k][...])
            outs[k][...] = g
            outs[n + k][...] = delta
            outs[2 * n + k][...] = m2
            outs[3 * n + k][...] = v2

    shapes = [jax.ShapeDtypeStruct(w[k].shape, F32) for k in names]
    res = _call(body, name="adamw_small", out_shape=shapes * 4, compiler_params=_cparams())(
        *[recv[k] for k in names], *[w[k] for k in names], *[m[k] for k in names], *[v[k] for k in names])
    return {k: (res[i], res[n + i], res[2 * n + i], res[3 * n + i]) for i, k in enumerate(names)}


def _expand_b(bt):
    eye = jnp.eye(GB, dtype=bt.dtype)
    return jnp.einsum("jgpn,gh->jgphn", bt.reshape(NBLK, GB, SGRP, NSTATE), eye).reshape(NBLK, GB * SGRP, NS)


def _extract_b(db):
    x = db.reshape(NBLK, GB, SGRP, GB, NSTATE)
    eye = jnp.eye(GB, dtype=db.dtype)
    return jnp.einsum("jgphn,gh->jgpn", x, eye).reshape(NGRP, SGRP, NSTATE)


def _expand_c(c):
    ct = jnp.transpose(c, (0, 2, 1)).reshape(NBLK, GB, NSTATE, SGRP)
    eye = jnp.eye(GB, dtype=c.dtype)
    return jnp.einsum("jgnp,gh->jgnhp", ct, eye).reshape(NBLK, NS, GB * SGRP)


def _extract_c(dc):
    x = dc.reshape(NBLK, GB, NSTATE, GB, SGRP)
    eye = jnp.eye(GB, dtype=dc.dtype)
    d = jnp.einsum("jgnhp,gh->jgnp", x, eye).reshape(NGRP, NSTATE, SGRP)
    return jnp.transpose(d, (0, 2, 1))


_SMALL = ("norm1", "b_gate", "ssm_a_re", "ssm_a_im", "ssm_log_dt", "ssm_b_re", "ssm_b_im", "ssm_c_re", "ssm_c_im",
          "ssm_d", "ssm_b_glu", "conv_b_dw", "conv_ln_g", "conv_ln_b", "pool_w_group", "pool_scale", "norm2")
_ADAM_TK = {"w_in": 256, "ssm_w_glu": 64, "ssm_w_proj": 512, "conv_w_proj": 512, "pool_w_proj": 512, "w_out": 128,
            "ffn_w_gate": HSH, "ffn_w_up": HSH, "ffn_w_down": HSH}
_OUT_ORDER = ("norm1", "w_in", "b_gate", "ssm_a_re", "ssm_a_im", "ssm_log_dt", "ssm_b_re", "ssm_b_im", "ssm_c_re",
              "ssm_c_im", "ssm_d", "ssm_w_glu", "ssm_b_glu", "ssm_w_proj", "conv_w_dw", "conv_b_dw", "conv_ln_g",
              "conv_ln_b", "conv_w_proj", "pool_w_group", "pool_scale", "pool_w_proj", "w_out", "norm2", "ffn_w_gate",
              "ffn_w_up", "ffn_w_down", "final_norm")


def _layer_fwd(x, p, token):
    z = _inproj_fwd(x, p["norm1"], p["w_in"], token)
    yssm, hre, him = _ssm_fwd(z, p)
    cv = _conv_fwd(z, p["conv_w"], p["conv_b"])
    pbar = _pool_fwd(z)
    x1 = _merge_fwd(x, yssm, cv, pbar, z, p)
    x2, gpre, upre = _ffn_fwd(x1, p["norm2"], p["wg"], p["wu"], p["wd"])
    return x2, dict(x=x, z=z, yssm=yssm, hre=hre, him=him, cv=cv, pbar=pbar, x1=x1, gpre=gpre, upre=upre)


def _layer_bwd(dx, p, s, token, after_ffn=None):
    big, small = {}, {}
    dx1, d_norm2, dgp, dup, act, h2 = _ffn_bwd(dx, s["x1"], p["norm2"], s["gpre"], s["upre"], p["wg"], p["wu"], p["wd"],
                                               token)
    big["ffn_w_gate"] = _matmul_tn(dgp, h2, "tn_gate").reshape(NDEV, HPAD, D)
    big["ffn_w_up"] = _matmul_tn(dup, h2, "tn_up").reshape(NDEV, HPAD, D)
    big["ffn_w_down"] = _matmul_tn(act, dx, "tn_down").reshape(NDEV, HPAD, D)
    (dy, dcv, dpb, dzg, a_g, a_outa, a_hs, a_pb, a_pc, a_mg, c_glu, c_ya, c_yb, c_p, c_yc,
     d_bglu, d_lng, d_lnb, d_scale, d_bgate) = _merge_bwd(dx1, s["x"], s["yssm"], s["cv"], s["pbar"], s["z"], p,
                                                          after_ffn(dict(big)) if after_ffn else token)
    big["ssm_w_glu"] = _matmul_tn(a_g, c_glu, "tn_glu").reshape(NDEV, BW // NDEV, BW)
    big["ssm_w_proj"] = _matmul_tn(a_outa, c_ya, "tn_ssm_proj", D // NDEV)
    big["conv_w_proj"] = _matmul_tn(a_hs, c_yb, "tn_conv_proj", D // NDEV)
    big["pool_w_proj"] = _matmul_tn(a_pc, c_yc, "tn_pool_proj", D // NDEV)
    big["w_out"] = _matmul_tn(a_mg, dx1, "tn_out").reshape(NDEV, D // NDEV, D)
    d_wgrp = _group_tn(a_pb, c_p)
    du_a, dbr, dbi, dcr, dci, dd, dar, dai, dldt = _ssm_bwd(dy, s["z"], s["hre"], s["him"], p)
    dva, dvb, dw8, dcb = _conv_bwd(dcv, s["z"], p["conv_w"])
    du_c = _pool_bwd(dpb)
    dz = jnp.concatenate([du_a, dva, dvb, du_c, dzg], axis=1)
    dx0, d_norm1, h = _inproj_bwd(dz, dx1, s["x"], p["norm1"], p["w_in"])
    big["w_in"] = _matmul_tn(h, dz, "tn_in", IN_W // NDEV)
    small["norm1"] = d_norm1
    small["b_gate"] = d_bgate
    small["ssm_a_re"] = dar.reshape(NGRP, NSTATE)
    small["ssm_a_im"] = dai.reshape(NGRP, NSTATE)
    small["ssm_log_dt"] = dldt.reshape(NBLK, 8, 128)[:, 0, :GB].reshape(1, NGRP)
    small["ssm_b_re"] = _extract_b(dbr)
    small["ssm_b_im"] = _extract_b(dbi)
    small["ssm_c_re"] = _extract_c(dcr)
    small["ssm_c_im"] = _extract_c(dci)
    small["ssm_d"] = dd.reshape(NGRP, SGRP)
    small["ssm_b_glu"] = d_bglu
    small["conv_b_dw"] = dcb
    small["conv_ln_g"] = d_lng
    small["conv_ln_b"] = d_lnb
    small["pool_w_group"] = d_wgrp
    small["pool_scale"] = d_scale
    small["norm2"] = d_norm2
    return dx0, big, dw8, small


def _train_step(a):
    t_rows = a["x"].shape[1]
    x0 = a["x"].reshape(t_rows, D)
    target = a["loss_target"].reshape(t_rows, D)

    tr = lambda w: jnp.transpose(w, (0, 2, 1))
    pad_rows = lambda w: jnp.pad(w, ((0, 0), (0, HPAD - HSH), (0, 0)))
    shards = {
        "w_in": a["w_in"], "ssm_w_glu": a["ssm_w_glu"], "ssm_w_proj": a["ssm_w_proj"],
        "conv_w_proj": a["conv_w_proj"], "pool_w_proj": a["pool_w_proj"], "w_out": a["w_out"],
        "ffn_w_gate": pad_rows(tr(a["ffn_w_gate"])), "ffn_w_up": pad_rows(tr(a["ffn_w_up"])),
        "ffn_w_down": pad_rows(a["ffn_w_down"]),
    }
    shards = {k: v.astype(BF16) for k, v in shards.items()}
    full, dw_all = _gather_weights(shards, a["conv_w_dw"].reshape(DEPTH, CONV_K, BW // NDEV), (0,))
    conv_w = jnp.transpose(dw_all, (1, 2, 0, 3)).reshape(DEPTH, CONV_K, BW)
    late = _gather_start(shards, DEPTH - 1, dw_all)
    core = lax.axis_index("c").astype(jnp.int32).reshape(1)
    chip = (2 * lax.axis_index("x") + lax.axis_index("y")).astype(jnp.int32).reshape(1)
    no_token = jnp.zeros((8, 128), F32)

    def layer_params(l):
        row = lambda v: v.reshape(1, -1)
        return dict(
            norm1=row(a["norm1"][l]), w_in=full["w_in"][l],
            are=row(a["ssm_a_re"][l]), aim=row(a["ssm_a_im"][l]),
            ldt=row(jnp.repeat(a["ssm_log_dt"][l], NSTATE)),
            bexp_re=_expand_b(jnp.transpose(a["ssm_b_re"][l], (0, 2, 1))),
            bexp_im=_expand_b(jnp.transpose(a["ssm_b_im"][l], (0, 2, 1))),
            cexp_re=_expand_c(a["ssm_c_re"][l]), cexp_im=_expand_c(a["ssm_c_im"][l]),
            dskip=row(a["ssm_d"][l]),
            conv_w=conv_w[l], conv_b=row(a["conv_b_dw"][l]),
            wglu=full["ssm_w_glu"][l], bglu=row(a["ssm_b_glu"][l]), wpa=full["ssm_w_proj"][l],
            lng=row(a["conv_ln_g"][l]), lnb=row(a["conv_ln_b"][l]), wpb=full["conv_w_proj"][l],
            wgrp=a["pool_w_group"][l].astype(BF16), scale=row(a["pool_scale"][l]), wpc=full["pool_w_proj"][l],
            bgate=row(a["b_gate"][l]), wout=full["w_out"][l],
            norm2=row(a["norm2"][l]), wg=full["ffn_w_gate"][l], wu=full["ffn_w_up"][l], wd=full["ffn_w_down"][l],
        )

    p0 = layer_params(0)
    x, s0 = _layer_fwd(x0, p0, late["token"])
    for name, w in _gather_finish(late, DEPTH - 1, x).items():
        full[name][DEPTH - 1] = w
    params = [p0, layer_params(DEPTH - 1)]
    x, s1 = _layer_fwd(x, params[1], no_token)
    saved = [s0, s1]

    loss_part, dx, d_final = _loss_head(x, a["final_norm"].reshape(1, D), target)
    loss = lax.psum(loss_part[0, 0], ("x", "y", "c"))

    dx, gb1, go1, gs1 = _layer_bwd(dx, params[1], saved[1], no_token)
    part1 = _pair_reduce("late", {n: [g] for n, g in gb1.items()}, {}, {}, core)
    rs_names, rs_handle = _chip_start(part1, "late")
    early = {}

    def ffn_grads_leave_early(ffn):
        early["part"] = _pair_reduce("ffn", {n: [g] for n, g in ffn.items()}, {}, {}, core)
        early["names"], early["handle"] = _chip_start(early["part"], "ffn")
        return early["handle"]["token"]

    dx, gb0, go0, gs0 = _layer_bwd(dx, params[0], saved[0], rs_handle["token"], ffn_grads_leave_early)
    grad_x = dx.reshape(1, t_rows, D)
    small = {n: [gs0[n], gs1[n]] for n in _SMALL}
    small["final_norm"] = [d_final]
    part0 = _pair_reduce("rest", {n: [g] for n, g in gb0.items() if n not in early["part"]}, {"conv_w_dw": [go0, go1]},
                         small, core)
    recv = _chip_exchange(part0, set(_BIG) | {"conv_w_dw"}, {"conv_w_dw"} | set(small))
    recv1 = _chip_wait(rs_names, rs_handle, dx, "late")
    recv.update(_chip_wait(early["names"], early["handle"], dx, "ffn"))
    part0.update(early["part"])

    results = {}
    for name in _BIG:
        fix = tr if name in ("ffn_w_gate", "ffn_w_up") else (lambda t: t)
        res = _adam_big(name, [part0[name], part1[name]], [recv[name], recv1[name]], fix(a[name]), fix(a["m_" + name]),
                        fix(a["v_" + name]), _ADAM_TK[name], chip)
        results[name] = tuple(fix(r) for r in res)

    lay = {
        "norm1": lambda v: v.reshape(DEPTH, 1, D), "b_gate": lambda v: v.reshape(DEPTH, 1, 3 * D),
        "ssm_log_dt": lambda v: v.reshape(DEPTH, 1, NGRP),
        "ssm_b_re": lambda v: jnp.transpose(v, (0, 1, 3, 2)), "ssm_b_im": lambda v: jnp.transpose(v, (0, 1, 3, 2)),
        "ssm_b_glu": lambda v: v.reshape(DEPTH, 1, BW), "conv_b_dw": lambda v: v.reshape(DEPTH, 1, BW),
        "conv_ln_g": lambda v: v.reshape(DEPTH, 1, BW), "conv_ln_b": lambda v: v.reshape(DEPTH, 1, BW),
        "pool_scale": lambda v: v.reshape(DEPTH, 1, BW), "norm2": lambda v: v.reshape(DEPTH, 1, D),
        "conv_w_dw": lambda v: v.reshape(DEPTH, CONV_K, BW // NDEV), "final_norm": lambda v: v.reshape(1, 1, D),
    }
    names = _SMALL + ("conv_w_dw", "final_norm")
    relay = lambda k, v: lay[k](v) if k in lay else v
    sm = _adam_small(names, recv, {k: relay(k, a[k]) for k in names}, {k: relay(k, a["m_" + k]) for k in names},
                     {k: relay(k, a["v_" + k]) for k in names})
    for k in names:
        back = (lambda r: jnp.transpose(r, (0, 1, 3, 2))) if k in ("ssm_b_re", "ssm_b_im") else (lambda r: r.reshape(a[k].shape))
        results[k] = tuple(back(r) for r in sm[k])

    outs = [loss, grad_x]
    for part in range(4):
        outs += [results[k][part] for k in _OUT_ORDER]
    return tuple(outs)


def kernel(x, norm1, w_in, b_gate, ssm_a_re, ssm_a_im, ssm_log_dt, ssm_b_re, ssm_b_im, ssm_c_re, ssm_c_im, ssm_d, ssm_w_glu, ssm_b_glu, ssm_w_proj, conv_w_dw, conv_b_dw, conv_ln_g, conv_ln_b, conv_w_proj, pool_w_group, pool_scale, pool_w_proj, w_out, norm2, ffn_w_gate, ffn_w_up, ffn_w_down, final_norm, loss_target, m_norm1, m_w_in, m_b_gate, m_ssm_a_re, m_ssm_a_im, m_ssm_log_dt, m_ssm_b_re, m_ssm_b_im, m_ssm_c_re, m_ssm_c_im, m_ssm_d, m_ssm_w_glu, m_ssm_b_glu, m_ssm_w_proj, m_conv_w_dw, m_conv_b_dw, m_conv_ln_g, m_conv_ln_b, m_conv_w_proj, m_pool_w_group, m_pool_scale, m_pool_w_proj, m_w_out, m_norm2, m_ffn_w_gate, m_ffn_w_up, m_ffn_w_down, m_final_norm, v_norm1, v_w_in, v_b_gate, v_ssm_a_re, v_ssm_a_im, v_ssm_log_dt, v_ssm_b_re, v_ssm_b_im, v_ssm_c_re, v_ssm_c_im, v_ssm_d, v_ssm_w_glu, v_ssm_b_glu, v_ssm_w_proj, v_conv_w_dw, v_conv_b_dw, v_conv_ln_g, v_conv_ln_b, v_conv_w_proj, v_pool_w_group, v_pool_scale, v_pool_w_proj, v_w_out, v_norm2, v_ffn_w_gate, v_ffn_w_up, v_ffn_w_down, v_final_norm):
    return _train_step(dict(locals()))
```

```python
import functools

import jax
import jax.numpy as jnp
from jax import lax
from jax.experimental import pallas as pl
from jax.experimental.pallas import tpu as pltpu

F32 = jnp.float32
BF16 = jnp.bfloat16

NDEV = 8
DEPTH = 2
D = 1024
BW = 512
NSTATE = 64
SGRP = 16
NGRP = BW // SGRP
GB = 8
NBLK = NGRP // GB
NS = GB * NSTATE
CONV_K = 31
HALO = 32
PHALO = 16
IN_W = 5120
HID = 2816
HSH = HID // NDEV
HPAD = 384
HIDP = HPAD * NDEV
EPS = 1e-6
VMEM_LIMIT = 56 * 1024 * 1024

ADAM_LR, ADAM_B1, ADAM_B2, ADAM_EPS, ADAM_WD, ADAM_STEP = 0.001, 0.9, 0.999, 1e-08, 0.01, 10

MESH = pl.DeviceIdType.MESH
ANY = pl.BlockSpec(memory_space=pl.ANY)


def _call(body, **kw):
    return pl.pallas_call(body, **kw)


def _cparams(sem=None):
    return pltpu.CompilerParams(dimension_semantics=sem, vmem_limit_bytes=VMEM_LIMIT)


def _dot(a, b):
    return jnp.dot(a.astype(BF16), b.astype(BF16), preferred_element_type=F32)


def _dot_nt(a, b):
    return lax.dot_general(a.astype(BF16), b.astype(BF16), (((1,), (1,)), ((), ())), preferred_element_type=F32)


def _dot_tn(a, b):
    return lax.dot_general(a.astype(BF16), b.astype(BF16), (((0,), (0,)), ((), ())), preferred_element_type=F32)


@jax.custom_vjp
def _mm(a, w):
    return _dot(a, w)


def _mm_fwd(a, w):
    return _dot(a, w), w


def _mm_bwd(w, ct):
    return _dot_nt(ct, w), jnp.zeros_like(w)


_mm.defvjp(_mm_fwd, _mm_bwd)


def _rms(x, g):
    return x * lax.rsqrt(jnp.mean(x * x, axis=-1, keepdims=True) + EPS) * g


def _disc(are, aim, ldt):
    dt = jnp.exp(ldt)
    mag = jnp.exp(dt * are)
    ang = dt * aim
    abr = mag * jnp.cos(ang)
    abi = mag * jnp.sin(ang)
    den = are * are + aim * aim
    nr = abr - 1.0
    fr = (nr * are + abi * aim) / den
    fi = (abi * are - nr * aim) / den
    return abr, abi, fr, fi


def _bbar(fr, fi, br, bi):
    return fr * br - fi * bi, fr * bi + fi * br


def _cmul(ar, ai, br, bi):
    return ar * br - ai * bi, ar * bi + ai * br


def _scan_rows(re_ref, im_ref, ar, ai, n_rows, reverse, hre_ref=None, him_ref=None):
    n = ar.shape[1]
    shape = (8, n)
    rows = lax.broadcasted_iota(jnp.int32, shape, 0)
    a1 = (jnp.broadcast_to(ar, shape), jnp.broadcast_to(ai, shape))
    a2 = _cmul(*a1, *a1)
    a4 = _cmul(*a2, *a2)
    pr = jnp.zeros(shape, F32)
    pi = jnp.zeros(shape, F32)
    pw = a1
    for k in range(8):
        sel = rows == ((7 - k) if reverse else k)
        pr = jnp.where(sel, pw[0], pr)
        pi = jnp.where(sel, pw[1], pi)
        pw = _cmul(*pw, *a1)
    nt = n_rows // 8
    with_acc = hre_ref is not None

    def body(i, carry):
        cr, ci = carry[0], carry[1]
        t = (nt - 1 - i) if reverse else i
        off = pl.multiple_of(t * 8, 8)
        xr = re_ref[pl.ds(off, 8), :]
        xi = im_ref[pl.ds(off, 8), :]
        for k, (kr, ki) in ((1, a1), (2, a2), (4, a4)):
            if reverse:
                keep, sh = rows < 8 - k, 8 - k
            else:
                keep, sh = rows >= k, k
            sr = jnp.where(keep, pltpu.roll(xr, sh, 0), 0.0)
            si = jnp.where(keep, pltpu.roll(xi, sh, 0), 0.0)
            xr, xi = xr + kr * sr - ki * si, xi + kr * si + ki * sr
        xr, xi = xr + pr * cr - pi * ci, xi + pr * ci + pi * cr
        re_ref[pl.ds(off, 8), :] = xr
        im_ref[pl.ds(off, 8), :] = xi
        edge = 0 if reverse else 7
        out = (jnp.broadcast_to(xr[edge:edge + 1, :], shape), jnp.broadcast_to(xi[edge:edge + 1, :], shape))
        if with_acc:
            hr = hre_ref[pl.ds(off, 8), :]
            hi = him_ref[pl.ds(off, 8), :]
            offp = pl.multiple_of(jnp.maximum(t - 1, 0) * 8, 8)
            live = jnp.where(t > 0, 1.0, 0.0)
            lr = jnp.broadcast_to(hre_ref[pl.ds(offp, 8), :][7:8, :], shape) * live
            li = jnp.broadcast_to(him_ref[pl.ds(offp, 8), :][7:8, :], shape) * live
            hpr = jnp.where(rows == 0, lr, pltpu.roll(hr, 1, 0))
            hpi = jnp.where(rows == 0, li, pltpu.roll(hi, 1, 0))
            out = out + (carry[2] + xr * hpr + xi * hpi, carry[3] + xi * hpr - xr * hpi)
        return out

    zero = jnp.zeros(shape, F32)
    init = (zero, zero, zero, zero) if with_acc else (zero, zero)
    res = lax.fori_loop(0, nt, body, init)
    return res[2:] if with_acc else None


TOKEN_SPEC = pl.BlockSpec((8, 128), lambda i, j: (0, 0))


def _inproj_fwd(x, gamma, w, token, tm=1024, tn=1280):
    t_rows = x.shape[0]
    n = w.shape[1]

    def body(x_ref, g_ref, w_ref, token_ref, z_ref, h_ref):
        @pl.when(pl.program_id(1) == 0)
        def _():
            h_ref[...] = _rms(x_ref[...], g_ref[...]).astype(BF16)
        z_ref[...] = jnp.dot(h_ref[...], w_ref[...], preferred_element_type=F32)

    return _call(
        body, name="inproj_fwd", grid=(t_rows // tm, n // tn),
        in_specs=[pl.BlockSpec((tm, D), lambda i, j: (i, 0)), pl.BlockSpec((1, D), lambda i, j: (0, 0)),
                  pl.BlockSpec((D, tn), lambda i, j: (0, j)), TOKEN_SPEC],
        out_specs=pl.BlockSpec((tm, tn), lambda i, j: (i, j)),
        out_shape=jax.ShapeDtypeStruct((t_rows, n), F32),
        scratch_shapes=[pltpu.VMEM((tm, D), BF16)],
        compiler_params=_cparams(("parallel", "arbitrary")))(x, gamma, w, token)


def _ssm_specs(t_rows):
    row = pl.BlockSpec((1, NS), lambda j: (0, j))
    return dict(
        u=pl.BlockSpec((t_rows, GB * SGRP), lambda j: (0, j)),
        row=row,
        bexp=pl.BlockSpec((None, GB * SGRP, NS), lambda j: (j, 0, 0)),
        cexp=pl.BlockSpec((None, NS, GB * SGRP), lambda j: (j, 0, 0)),
        d=pl.BlockSpec((1, GB * SGRP), lambda j: (0, j)),
        h=pl.BlockSpec((t_rows, NS), lambda j: (0, j)),
    )


def _ssm_fwd(z, p):
    t_rows = z.shape[0]
    s = _ssm_specs(t_rows)

    def body(u_ref, are_ref, aim_ref, ldt_ref, br_ref, bi_ref, cr_ref, ci_ref, d_ref, y_ref, hr_ref, hi_ref):
        abr, abi, fr, fi = _disc(are_ref[...], aim_ref[...], ldt_ref[...])
        bbr, bbi = _bbar(fr, fi, br_ref[...], bi_ref[...])
        u = u_ref[...]
        hr_ref[...] = _dot(u, bbr)
        hi_ref[...] = _dot(u, bbi)
        _scan_rows(hr_ref, hi_ref, abr, abi, t_rows, False)
        y_ref[...] = _dot(hr_ref[...], cr_ref[...]) - _dot(hi_ref[...], ci_ref[...]) + d_ref[...] * u

    return _call(
        body, name="ssm_fwd", grid=(NBLK,),
        in_specs=[s["u"], s["row"], s["row"], s["row"], s["bexp"], s["bexp"], s["cexp"], s["cexp"], s["d"]],
        out_specs=[s["u"], s["h"], s["h"]],
        out_shape=[jax.ShapeDtypeStruct((t_rows, BW), F32), jax.ShapeDtypeStruct((t_rows, NGRP * NSTATE), F32),
                   jax.ShapeDtypeStruct((t_rows, NGRP * NSTATE), F32)],
        compiler_params=_cparams(("parallel",)))(
            z, p["are"], p["aim"], p["ldt"], p["bexp_re"], p["bexp_im"], p["cexp_re"], p["cexp_im"], p["dskip"])


def _ssm_bwd(dy, z, hre, him, p):
    t_rows = z.shape[0]
    s = _ssm_specs(t_rows)
    nstates = NGRP * NSTATE

    def body(dy_ref, u_ref, hr_ref, hi_ref, are_ref, aim_ref, ldt_ref, br_ref, bi_ref, cr_ref, ci_ref, d_ref,
             du_ref, dbr_ref, dbi_ref, dcr_ref, dci_ref, dd_ref, dar_ref, dai_ref, dldt_ref, lr_ref, li_ref):
        rows3 = (are_ref[...], aim_ref[...], ldt_ref[...])
        (abr, abi, fr, fi), disc_vjp = jax.vjp(_disc, *rows3)
        (bbr, bbi), bbar_vjp = jax.vjp(_bbar, fr, fi, br_ref[...], bi_ref[...])
        dy = dy_ref[...]
        u = u_ref[...]
        lr_ref[...] = _dot_nt(dy, cr_ref[...])
        li_ref[...] = -_dot_nt(dy, ci_ref[...])
        dcr_ref[...] = _dot_tn(hr_ref[...], dy)
        dci_ref[...] = -_dot_tn(hi_ref[...], dy)
        dd_ref[...] = jnp.sum(dy * u, axis=0, keepdims=True)
        acc_r, acc_i = _scan_rows(lr_ref, li_ref, abr, -abi, t_rows, True, hr_ref, hi_ref)
        dabr = jnp.sum(acc_r, axis=0, keepdims=True)
        dabi = jnp.sum(acc_i, axis=0, keepdims=True)
        lam_r = lr_ref[...]
        lam_i = li_ref[...]
        du = d_ref[...] * dy + _dot_nt(lam_r, bbr) + _dot_nt(lam_i, bbi)
        du_ref[...] = du.astype(BF16)
        dbbr = _dot_tn(u, lam_r)
        dbbi = _dot_tn(u, lam_i)
        dfr, dfi, dbr, dbi = bbar_vjp((dbbr, dbbi))
        dbr_ref[...] = dbr
        dbi_ref[...] = dbi
        dar, dai, dldt = disc_vjp((dabr, dabi, dfr, dfi))
        dar_ref[...] = dar
        dai_ref[...] = dai
        lane_grp = lax.broadcasted_iota(jnp.int32, (NS, 128), 0) // NSTATE
        col = lax.broadcasted_iota(jnp.int32, (NS, 128), 1)
        seg = jnp.where(lane_grp == col, 1.0, 0.0).astype(F32)
        dldt_ref[...] = jnp.dot(jnp.broadcast_to(dldt, (8, NS)), seg, preferred_element_type=F32,
                                precision=lax.Precision.HIGHEST)

    dyspec = pl.BlockSpec((t_rows, GB * SGRP), lambda j: (0, j))
    return _call(
        body, name="ssm_bwd", grid=(NBLK,),
        in_specs=[dyspec, s["u"], s["h"], s["h"], s["row"], s["row"], s["row"], s["bexp"], s["bexp"], s["cexp"],
                  s["cexp"], s["d"]],
        out_specs=[dyspec, s["bexp"], s["bexp"], s["cexp"], s["cexp"], s["d"], s["row"], s["row"],
                   pl.BlockSpec((8, 128), lambda j: (j, 0))],
        out_shape=[jax.ShapeDtypeStruct((t_rows, BW), BF16),
                   jax.ShapeDtypeStruct((NBLK, GB * SGRP, NS), F32), jax.ShapeDtypeStruct((NBLK, GB * SGRP, NS), F32),
                   jax.ShapeDtypeStruct((NBLK, NS, GB * SGRP), F32), jax.ShapeDtypeStruct((NBLK, NS, GB * SGRP), F32),
                   jax.ShapeDtypeStruct((1, BW), F32), jax.ShapeDtypeStruct((1, nstates), F32),
                   jax.ShapeDtypeStruct((1, nstates), F32), jax.ShapeDtypeStruct((NBLK * 8, 128), F32)],
        scratch_shapes=[pltpu.VMEM((t_rows, NS), F32), pltpu.VMEM((t_rows, NS), F32)],
        compiler_params=_cparams(("parallel",)))(
            dy, z, hre, him, p["are"], p["aim"], p["ldt"], p["bexp_re"], p["bexp_im"], p["cexp_re"], p["cexp_im"],
            p["dskip"])


def _conv_fwd(z, w, b, tm=256):
    t_rows = z.shape[0]
    hb = tm // HALO

    def body(va_ref, vb_ref, ha_ref, hb_ref, w_ref, b_ref, o_ref, win_ref):
        live = jnp.where(pl.program_id(0) > 0, 1.0, 0.0)
        win_ref[0:HALO, :] = ha_ref[...] * jax.nn.sigmoid(hb_ref[...]) * live
        win_ref[HALO:HALO + tm, :] = va_ref[...] * jax.nn.sigmoid(vb_ref[...])
        acc = jnp.broadcast_to(b_ref[...], (tm, BW))
        for k in range(CONV_K):
            acc = acc + w_ref[k:k + 1, :] * win_ref[pl.ds(HALO - (CONV_K - 1) + k, tm), :]
        o_ref[...] = acc

    halo = lambda col: pl.BlockSpec((HALO, BW), lambda i: (jnp.maximum(i * hb - 1, 0), col))
    return _call(
        body, name="conv_fwd", grid=(t_rows // tm,),
        in_specs=[pl.BlockSpec((tm, BW), lambda i: (i, 1)), pl.BlockSpec((tm, BW), lambda i: (i, 2)), halo(1), halo(2),
                  pl.BlockSpec((CONV_K, BW), lambda i: (0, 0)), pl.BlockSpec((1, BW), lambda i: (0, 0))],
        out_specs=pl.BlockSpec((tm, BW), lambda i: (i, 0)),
        out_shape=jax.ShapeDtypeStruct((t_rows, BW), F32),
        scratch_shapes=[pltpu.VMEM((HALO + tm, BW), F32)],
        compiler_params=_cparams(("parallel",)))(z, z, z, z, w, b)


def _conv_bwd(dcv, z, w, tm=256):
    t_rows = z.shape[0]
    nt = t_rows // tm
    hb = tm // HALO
    csh = BW // NDEV

    def body(d_ref, dn_ref, va_ref, vb_ref, ha_ref, hb_ref, w_ref, dva_ref, dvb_ref, dw8_ref, db_ref,
             hwin_ref, dwin_ref, dw_ref):
        i = pl.program_id(0)

        @pl.when(i == 0)
        def _():
            dw_ref[...] = jnp.zeros_like(dw_ref)
            db_ref[...] = jnp.zeros_like(db_ref)

        live_prev = jnp.where(i > 0, 1.0, 0.0)
        live_next = jnp.where(i < nt - 1, 1.0, 0.0)
        va = va_ref[...]
        sig = jax.nn.sigmoid(vb_ref[...])
        hwin_ref[0:HALO, :] = ha_ref[...] * jax.nn.sigmoid(hb_ref[...]) * live_prev
        hwin_ref[HALO:HALO + tm, :] = va * sig
        d = d_ref[...]
        dwin_ref[0:tm, :] = d
        dwin_ref[tm:tm + HALO, :] = dn_ref[...] * live_next
        dh = jnp.zeros((tm, BW), F32)
        dws = []
        for k in range(CONV_K):
            dh = dh + w_ref[k:k + 1, :] * dwin_ref[pl.ds(CONV_K - 1 - k, tm), :]
            dws.append(jnp.sum(d * hwin_ref[pl.ds(HALO - (CONV_K - 1) + k, tm), :], axis=0, keepdims=True))
        dws.append(jnp.zeros((1, BW), F32))
        dw_ref[...] += jnp.concatenate(dws, axis=0)
        db_ref[...] += jnp.sum(d, axis=0, keepdims=True)
        dva_ref[...] = (dh * sig).astype(BF16)
        dvb_ref[...] = (dh * va * sig * (1.0 - sig)).astype(BF16)

        @pl.when(i == nt - 1)
        def _():
            acc = dw_ref[...]
            for q in range(NDEV):
                dw8_ref[q] = acc[:, csh * q:csh * (q + 1)]

    halo = lambda col: pl.BlockSpec((HALO, BW), lambda i: (jnp.maximum(i * hb - 1, 0), col))
    return _call(
        body, name="conv_bwd", grid=(nt,),
        in_specs=[pl.BlockSpec((tm, BW), lambda i: (i, 0)),
                  pl.BlockSpec((HALO, BW), lambda i: (jnp.minimum((i + 1) * hb, t_rows // HALO - 1), 0)),
                  pl.BlockSpec((tm, BW), lambda i: (i, 1)), pl.BlockSpec((tm, BW), lambda i: (i, 2)), halo(1), halo(2),
                  pl.BlockSpec((CONV_K, BW), lambda i: (0, 0))],
        out_specs=[pl.BlockSpec((tm, BW), lambda i: (i, 0)), pl.BlockSpec((tm, BW), lambda i: (i, 0)),
                   pl.BlockSpec((NDEV, 32, csh), lambda i: (0, 0, 0)), pl.BlockSpec((1, BW), lambda i: (0, 0))],
        out_shape=[jax.ShapeDtypeStruct((t_rows, BW), BF16), jax.ShapeDtypeStruct((t_rows, BW), BF16),
                   jax.ShapeDtypeStruct((NDEV, 32, csh), F32), jax.ShapeDtypeStruct((1, BW), F32)],
        scratch_shapes=[pltpu.VMEM((HALO + tm, BW), F32), pltpu.VMEM((tm + HALO, BW), F32), pltpu.VMEM((32, BW), F32)],
        compiler_params=_cparams(("arbitrary",)))(dcv, dcv, z, z, z, z, w)


def _pool_rows(i, tm, n_rows, first_row):
    grp = lax.broadcasted_iota(jnp.int32, (1, BW), 1) // (BW // 4)
    wlen = jnp.where(grp == 0, 2.0, jnp.where(grp == 1, 4.0, jnp.where(grp == 2, 8.0, 16.0)))
    t = (i * tm + first_row + lax.broadcasted_iota(jnp.int32, (n_rows, 1), 0)).astype(F32)
    return grp, 1.0 / jnp.minimum(t + 1.0, wlen)


def _pool_pick(grp, s2, s4, s8, s16):
    return jnp.where(grp == 0, s2, jnp.where(grp == 1, s4, jnp.where(grp == 2, s8, s16)))


def _pool_fwd(z, tm=256):
    t_rows = z.shape[0]
    hb = tm // PHALO

    def body(u_ref, h_ref, o_ref):
        i = pl.program_id(0)
        u = u_ref[...]
        win = jnp.concatenate([h_ref[...] * jnp.where(i > 0, 1.0, 0.0), u], axis=0)
        s2 = win + pltpu.roll(win, 1, 0)
        s4 = s2 + pltpu.roll(s2, 2, 0)
        s8 = s4 + pltpu.roll(s4, 4, 0)
        s16 = s8 + pltpu.roll(s8, 8, 0)
        grp, inv = _pool_rows(i, tm, tm, 0)
        o_ref[...] = _pool_pick(grp, s2, s4, s8, s16)[PHALO:, :] * inv - u

    return _call(
        body, name="pool_fwd", grid=(t_rows // tm,),
        in_specs=[pl.BlockSpec((tm, BW), lambda i: (i, 3)),
                  pl.BlockSpec((PHALO, BW), lambda i: (jnp.maximum(i * hb - 1, 0), 3))],
        out_specs=pl.BlockSpec((tm, BW), lambda i: (i, 0)),
        out_shape=jax.ShapeDtypeStruct((t_rows, BW), F32),
        compiler_params=_cparams(("parallel",)))(z, z)


def _pool_bwd(dp, tm=256):
    t_rows = dp.shape[0]
    nt = t_rows // tm
    hb = tm // PHALO
    ln = tm + PHALO

    def body(d_ref, dn_ref, o_ref):
        i = pl.program_id(0)
        d = d_ref[...]
        grp, inv = _pool_rows(i, tm, ln, 0)
        win = jnp.concatenate([d, dn_ref[...] * jnp.where(i < nt - 1, 1.0, 0.0)], axis=0) * inv
        s2 = win + pltpu.roll(win, ln - 1, 0)
        s4 = s2 + pltpu.roll(s2, ln - 2, 0)
        s8 = s4 + pltpu.roll(s4, ln - 4, 0)
        s16 = s8 + pltpu.roll(s8, ln - 8, 0)
        o_ref[...] = (_pool_pick(grp, s2, s4, s8, s16)[:tm, :] - d).astype(BF16)

    return _call(
        body, name="pool_bwd", grid=(nt,),
        in_specs=[pl.BlockSpec((tm, BW), lambda i: (i, 0)),
                  pl.BlockSpec((PHALO, BW), lambda i: (jnp.minimum((i + 1) * hb, t_rows // PHALO - 1), 0))],
        out_specs=pl.BlockSpec((tm, BW), lambda i: (i, 0)),
        out_shape=jax.ShapeDtypeStruct((t_rows, BW), BF16),
        compiler_params=_cparams(("parallel",)))(dp, dp)


_MERGE_W = ("wglu", "bglu", "wpa", "lng", "lnb", "wpb", "wgrp", "scale", "wpc", "bgate", "wout")
_MERGE_SMALL = ("bglu", "lng", "lnb", "scale", "bgate")


def _merge_math(x, yssm, cv, pbar, zg, w, taps):
    t_glu, t_ya, t_yb, t_p, t_yc = taps
    g = jax.nn.gelu(yssm)
    outa = g * jax.nn.sigmoid(_mm(g, w["wglu"]) + t_glu + w["bglu"])
    ya = _mm(outa, w["wpa"]) + t_ya
    mu = jnp.mean(cv, axis=-1, keepdims=True)
    var = jnp.mean(jnp.square(cv - mu), axis=-1, keepdims=True)
    hs = jax.nn.silu((cv - mu) * lax.rsqrt(var + EPS) * w["lng"] + w["lnb"])
    yb = _mm(hs, w["wpb"]) + t_yb
    gw = BW // 4
    pk = jnp.concatenate([_mm(pbar[:, gw * k:gw * (k + 1)], w["wgrp"][k]) for k in range(4)], axis=1) + t_p
    pc = pk * w["scale"]
    yc = _mm(pc, w["wpc"]) + t_yc
    gates = jax.nn.sigmoid(zg + w["bgate"])
    merged = gates[:, :D] * ya + gates[:, D:2 * D] * yb + gates[:, 2 * D:] * yc
    x1 = x + _mm(merged, w["wout"])
    acts = tuple(a.astype(BF16) for a in (g, outa, hs, pbar, pc, merged))
    return x1, acts


def _merge_specs(tm, p):
    rows = lambda width, col=0: pl.BlockSpec((tm, width), lambda i, c=col: (i, c))
    data = [rows(D), rows(BW), rows(BW), rows(BW), rows(D, 2), rows(D, 3), rows(D, 4)]
    wspecs = []
    for name in _MERGE_W:
        nd = p[name].ndim
        wspecs.append(pl.BlockSpec(p[name].shape, lambda i, nd=nd: (0,) * nd))
    return rows, data, wspecs


def _merge_fwd(x, yssm, cv, pbar, z, p, tm=256):
    t_rows = x.shape[0]
    rows, data, wspecs = _merge_specs(tm, p)

    def body(x_ref, y_ref, cv_ref, pb_ref, za_ref, zb_ref, zc_ref, *rest):
        w = {name: r[...] for name, r in zip(_MERGE_W, rest[:len(_MERGE_W)])}
        o_ref = rest[len(_MERGE_W)]
        taps = (0.0, 0.0, 0.0, 0.0, 0.0)
        zg = jnp.concatenate([za_ref[...], zb_ref[...], zc_ref[...]], axis=1)
        o_ref[...] = _merge_math(x_ref[...], y_ref[...], cv_ref[...], pb_ref[...], zg, w, taps)[0]

    return _call(
        body, name="merge_fwd", grid=(t_rows // tm,), in_specs=data + wspecs, out_specs=rows(D),
        out_shape=jax.ShapeDtypeStruct((t_rows, D), F32),
        compiler_params=_cparams(("parallel",)))(x, yssm, cv, pbar, z, z, z, *[p[n] for n in _MERGE_W])


def _merge_bwd(dx1, x, yssm, cv, pbar, z, p, token, tm=256):
    t_rows = x.shape[0]
    rows, data, wspecs = _merge_specs(tm, p)
    nw = len(_MERGE_W)

    def body(dx_ref, x_ref, y_ref, cv_ref, pb_ref, za_ref, zb_ref, zc_ref, *rest):
        w = {name: r[...] for name, r in zip(_MERGE_W, rest[:nw])}
        zg = jnp.concatenate([za_ref[...], zb_ref[...], zc_ref[...]], axis=1)
        outs = rest[nw + 1:]
        small = {n: w[n] for n in _MERGE_SMALL}
        taps = (jnp.zeros((tm, BW), F32), jnp.zeros((tm, D), F32), jnp.zeros((tm, D), F32),
                jnp.zeros((tm, BW), F32), jnp.zeros((tm, D), F32))

        def f(yssm_, cv_, pbar_, zg_, small_, taps_):
            return _merge_math(x_ref[...], yssm_, cv_, pbar_, zg_, {**w, **small_}, taps_)

        _, vjp, acts = jax.vjp(f, y_ref[...], cv_ref[...], pb_ref[...], zg, small, taps, has_aux=True)
        dy, dcv, dpb, dzg, dsmall, dtaps = vjp(dx_ref[...])
        outs[0][...] = dy
        outs[1][...] = dcv
        outs[2][...] = dpb
        outs[3][...] = dzg.astype(BF16)
        for k in range(6):
            outs[4 + k][...] = acts[k]
        for k in range(5):
            outs[10 + k][...] = dtaps[k].astype(BF16)

        @pl.when(pl.program_id(0) == 0)
        def _():
            for k in range(5):
                outs[15 + k][...] = jnp.zeros_like(outs[15 + k])

        for k, n in enumerate(_MERGE_SMALL):
            outs[15 + k][...] += dsmall[n]

    f32o = lambda width: jax.ShapeDtypeStruct((t_rows, width), F32)
    bfo = lambda width: jax.ShapeDtypeStruct((t_rows, width), BF16)
    small_shapes = [jax.ShapeDtypeStruct(p[n].shape, F32) for n in _MERGE_SMALL]
    small_specs = [pl.BlockSpec(p[n].shape, lambda i: (0, 0)) for n in _MERGE_SMALL]
    out_shape = ([f32o(BW), f32o(BW), f32o(BW), bfo(3 * D)]
                 + [bfo(BW), bfo(BW), bfo(BW), bfo(BW), bfo(BW), bfo(D)]
                 + [bfo(BW), bfo(D), bfo(D), bfo(BW), bfo(D)] + small_shapes)
    out_specs = ([rows(BW), rows(BW), rows(BW), rows(3 * D)]
                 + [rows(BW)] * 5 + [rows(D)]
                 + [rows(BW), rows(D), rows(D), rows(BW), rows(D)] + small_specs)
    return _call(
        body, name="merge_bwd", grid=(t_rows // tm,),
        in_specs=[rows(D)] + data + wspecs + [pl.BlockSpec((8, 128), lambda i: (0, 0))], out_specs=out_specs,
        out_shape=out_shape, compiler_params=_cparams(("arbitrary",)))(
            dx1, x, yssm, cv, pbar, z, z, z, *[p[n] for n in _MERGE_W], token)


def _ffn_fwd(x1, gamma, wg, wu, wd, tm=1024, th=512):
    t_rows = x1.shape[0]
    nh = HIDP // th

    def body(x_ref, g_ref, wg_ref, wu_ref, wd_ref, o_ref, gp_ref, up_ref, h_ref, acc_ref):
        j = pl.program_id(1)

        @pl.when(j == 0)
        def _():
            h_ref[...] = _rms(x_ref[...], g_ref[...]).astype(BF16)
            acc_ref[...] = jnp.zeros_like(acc_ref)

        gp = _dot_nt(h_ref[...], wg_ref[...])
        up = _dot_nt(h_ref[...], wu_ref[...])
        gp_ref[...] = gp
        up_ref[...] = up
        acc_ref[...] += _dot(jax.nn.silu(gp) * up, wd_ref[...])

        @pl.when(j == nh - 1)
        def _():
            o_ref[...] = x_ref[...] + acc_ref[...]

    return _call(
        body, name="ffn_fwd", grid=(t_rows // tm, nh),
        in_specs=[pl.BlockSpec((tm, D), lambda i, j: (i, 0)), pl.BlockSpec((1, D), lambda i, j: (0, 0)),
                  pl.BlockSpec((th, D), lambda i, j: (j, 0)), pl.BlockSpec((th, D), lambda i, j: (j, 0)),
                  pl.BlockSpec((th, D), lambda i, j: (j, 0))],
        out_specs=[pl.BlockSpec((tm, D), lambda i, j: (i, 0)), pl.BlockSpec((tm, th), lambda i, j: (i, j)),
                   pl.BlockSpec((tm, th), lambda i, j: (i, j))],
        out_shape=[jax.ShapeDtypeStruct((t_rows, D), F32), jax.ShapeDtypeStruct((t_rows, HIDP), F32),
                   jax.ShapeDtypeStruct((t_rows, HIDP), F32)],
        scratch_shapes=[pltpu.VMEM((tm, D), BF16), pltpu.VMEM((tm, D), F32)],
        compiler_params=_cparams(("parallel", "arbitrary")))(x1, gamma, wg, wu, wd)


def _rms_bwd_tail(x, gamma, dh):
    _, vjp = jax.vjp(_rms, x, gamma)
    return vjp(dh)


def _ffn_bwd(dx2, x1, gamma, gpre, upre, wg, wu, wd, token, tm=512, th=1024):
    t_rows = x1.shape[0]
    nh = HIDP // th

    def body(d_ref, x_ref, g_ref, gp_ref, up_ref, wg_ref, wu_ref, wd_ref, token_ref,
             dx_ref, dgam_ref, dgp_ref, dup_ref, act_ref, h_ref, acc_ref):
        i = pl.program_id(0)
        j = pl.program_id(1)

        @pl.when(j == 0)
        def _():
            acc_ref[...] = jnp.zeros_like(acc_ref)

        @pl.when((i == 0) & (j == 0))
        def _():
            dgam_ref[...] = jnp.zeros_like(dgam_ref)

        dact = _dot_nt(d_ref[...], wd_ref[...])
        gp = gp_ref[...]
        up = up_ref[...]
        sg = jax.nn.sigmoid(gp)
        silu = gp * sg
        dgp = (dact * up * (sg * (1.0 + gp * (1.0 - sg)))).astype(BF16)
        dup = (dact * silu).astype(BF16)
        dgp_ref[...] = dgp
        dup_ref[...] = dup
        act_ref[...] = (silu * up).astype(BF16)
        acc_ref[...] += _dot(dgp, wg_ref[...]) + _dot(dup, wu_ref[...])

        @pl.when(j == nh - 1)
        def _():
            x = x_ref[...]
            h_ref[...] = _rms(x, g_ref[...]).astype(BF16)
            dx, dgam = _rms_bwd_tail(x, g_ref[...], acc_ref[...])
            dx_ref[...] = d_ref[...] + dx
            dgam_ref[...] += dgam

    row_d = pl.BlockSpec((tm, D), lambda i, j: (i, 0))
    row_h = pl.BlockSpec((tm, th), lambda i, j: (i, j))
    return _call(
        body, name="ffn_bwd", grid=(t_rows // tm, nh),
        in_specs=[row_d, row_d, pl.BlockSpec((1, D), lambda i, j: (0, 0)), row_h, row_h,
                  pl.BlockSpec((th, D), lambda i, j: (j, 0)), pl.BlockSpec((th, D), lambda i, j: (j, 0)),
                  pl.BlockSpec((th, D), lambda i, j: (j, 0)), TOKEN_SPEC],
        out_specs=[row_d, pl.BlockSpec((1, D), lambda i, j: (0, 0)), row_h, row_h, row_h, row_d],
        out_shape=[jax.ShapeDtypeStruct((t_rows, D), F32), jax.ShapeDtypeStruct((1, D), F32),
                   jax.ShapeDtypeStruct((t_rows, HIDP), BF16), jax.ShapeDtypeStruct((t_rows, HIDP), BF16),
                   jax.ShapeDtypeStruct((t_rows, HIDP), BF16), jax.ShapeDtypeStruct((t_rows, D), BF16)],
        scratch_shapes=[pltpu.VMEM((tm, D), F32)],
        compiler_params=_cparams(("arbitrary", "arbitrary")))(dx2, x1, gamma, gpre, upre, wg, wu, wd, token)


def _inproj_bwd(dz, dx1, x, gamma, w, tm=1024, tn=1280):
    t_rows = x.shape[0]
    nn = IN_W // tn

    def body(dz_ref, d1_ref, x_ref, g_ref, w_ref, dx_ref, dgam_ref, h_ref, acc_ref):
        i = pl.program_id(0)
        j = pl.program_id(1)

        @pl.when(j == 0)
        def _():
            acc_ref[...] = jnp.zeros_like(acc_ref)

        @pl.when((i == 0) & (j == 0))
        def _():
            dgam_ref[...] = jnp.zeros_like(dgam_ref)

        acc_ref[...] += _dot_nt(dz_ref[...], w_ref[...])

        @pl.when(j == nn - 1)
        def _():
            x = x_ref[...]
            h_ref[...] = _rms(x, g_ref[...]).astype(BF16)
            dx, dgam = _rms_bwd_tail(x, g_ref[...], acc_ref[...])
            dx_ref[...] = d1_ref[...] + dx
            dgam_ref[...] += dgam

    row_d = pl.BlockSpec((tm, D), lambda i, j: (i, 0))
    return _call(
        body, name="inproj_bwd", grid=(t_rows // tm, nn),
        in_specs=[pl.BlockSpec((tm, tn), lambda i, j: (i, j)), row_d, row_d, pl.BlockSpec((1, D), lambda i, j: (0, 0)),
                  pl.BlockSpec((D, tn), lambda i, j: (0, j))],
        out_specs=[row_d, pl.BlockSpec((1, D), lambda i, j: (0, 0)), row_d],
        out_shape=[jax.ShapeDtypeStruct((t_rows, D), F32), jax.ShapeDtypeStruct((1, D), F32),
                   jax.ShapeDtypeStruct((t_rows, D), BF16)],
        scratch_shapes=[pltpu.VMEM((tm, D), F32)],
        compiler_params=_cparams(("arbitrary", "arbitrary")))(dz, dx1, x, gamma, w)


def _matmul_tn(a, b, name, owner_cols=None, tt=2048):
    t_rows, k = a.shape
    n = b.shape[1]
    tk = min(k, 1024)
    nt = t_rows // tt
    if owner_cols is None:
        tn, nb = min(n, 1024), None
        out_spec = pl.BlockSpec((tk, tn), lambda i, j, t: (i, j))
        out_shape = jax.ShapeDtypeStruct((k, n), BF16)
    else:
        nb = min(n // owner_cols, max(1, 1280 // owner_cols))
        tn = nb * owner_cols
        out_spec = pl.BlockSpec((nb, tk, owner_cols), lambda i, j, t: (j, i, 0))
        out_shape = jax.ShapeDtypeStruct((n // owner_cols, k, owner_cols), BF16)

    def body(a_ref, b_ref, o_ref, acc_ref):
        t = pl.program_id(2)

        @pl.when(t == 0)
        def _():
            acc_ref[...] = jnp.zeros_like(acc_ref)

        acc_ref[...] += _dot_tn(a_ref[...], b_ref[...])

        @pl.when(t == nt - 1)
        def _():
            if nb is None:
                o_ref[...] = acc_ref[...].astype(BF16)
            else:
                for q in range(nb):
                    o_ref[q] = acc_ref[:, owner_cols * q:owner_cols * (q + 1)].astype(BF16)

    return _call(
        body, name=name, grid=(k // tk, n // tn, nt),
        in_specs=[pl.BlockSpec((tt, tk), lambda i, j, t: (t, i)), pl.BlockSpec((tt, tn), lambda i, j, t: (t, j))],
        out_specs=out_spec, out_shape=out_shape,
        scratch_shapes=[pltpu.VMEM((tk, tn), F32)],
        compiler_params=_cparams(("parallel", "parallel", "arbitrary")))(a, b)


def _group_tn(a, b):
    t_rows = a.shape[0]
    gw = BW // 4

    def body(a_ref, b_ref, o_ref):
        o_ref[...] = _dot_tn(a_ref[...], b_ref[...])

    return _call(
        body, name="pool_group_tn", grid=(4,),
        in_specs=[pl.BlockSpec((t_rows, gw), lambda k: (0, k)), pl.BlockSpec((t_rows, gw), lambda k: (0, k))],
        out_specs=pl.BlockSpec((None, gw, gw), lambda k: (k, 0, 0)),
        out_shape=jax.ShapeDtypeStruct((4, gw, gw), F32),
        compiler_params=_cparams(("parallel",)))(a, b)


def _loss_head(x2, gamma, target, tm=512):
    t_rows = x2.shape[0]

    def body(x_ref, g_ref, t_ref, loss_ref, dx_ref, dgam_ref):
        @pl.when(pl.program_id(0) == 0)
        def _():
            loss_ref[...] = jnp.zeros_like(loss_ref)
            dgam_ref[...] = jnp.zeros_like(dgam_ref)

        def f(x, g):
            err = jnp.square(_rms(x, g) - t_ref[...])
            return 0.5 * jnp.sum(jnp.mean(err, axis=-1, keepdims=True), axis=0, keepdims=True)

        loss, vjp = jax.vjp(f, x_ref[...], g_ref[...])
        dx, dgam = vjp(jnp.ones((1, 1), F32))
        loss_ref[...] += jnp.broadcast_to(loss, (1, 128))
        dx_ref[...] = dx
        dgam_ref[...] += dgam

    row_d = pl.BlockSpec((tm, D), lambda i: (i, 0))
    return _call(
        body, name="loss_head", grid=(t_rows // tm,),
        in_specs=[row_d, pl.BlockSpec((1, D), lambda i: (0, 0)), row_d],
        out_specs=[pl.BlockSpec((1, 128), lambda i: (0, 0)), row_d, pl.BlockSpec((1, D), lambda i: (0, 0))],
        out_shape=[jax.ShapeDtypeStruct((1, 128), F32), jax.ShapeDtypeStruct((t_rows, D), F32),
                   jax.ShapeDtypeStruct((1, D), F32)],
        compiler_params=_cparams(("arbitrary",)))(x2, gamma, target)


NCHIP = NDEV // 2


def _coords():
    return lax.axis_index("x"), lax.axis_index("y"), lax.axis_index("c")


def _remote(src, dst, send_sem, recv_sem, peer):
    return pltpu.make_async_remote_copy(src_ref=src, dst_ref=dst, send_sem=send_sem, recv_sem=recv_sem,
                                        device_id=peer, device_id_type=MESH)


def _comm_call(name, srcs, out_shapes, n_rec, plan, aliases=None):
    ns, no = len(srcs), len(out_shapes)

    def body(*refs):
        ins, outs = refs[:ns], refs[ns:ns + no]
        loc_sem, send_sem, recv_sem = refs[ns + no:]
        x, y, c = _coords()
        recs = plan(ins, outs, x, y, c)
        assert len(recs) == n_rec
        for k, r in enumerate(recs):
            for src, dst in r.get("local", ()):
                pltpu.make_async_copy(src, dst, loc_sem.at[k]).start()
            for peer, src, dst in r.get("remote", ()):
                _remote(src, dst, send_sem.at[k], recv_sem.at[k], peer).start()
        for k, r in enumerate(recs):
            if r.get("recv_wait") is not None:
                w = r["recv_wait"]
                _remote(w, w, send_sem.at[k], recv_sem.at[k], (x, y, c)).wait_recv()
            if r.get("send_wait") is not None:
                w = r["send_wait"]
                _remote(w, w, send_sem.at[k], recv_sem.at[k], (x, y, c)).wait_send()
            if r.get("local_wait") is not None:
                w = r["local_wait"]
                pltpu.make_async_copy(w, w, loc_sem.at[k]).wait()

    return _call(
        body, name=name, in_specs=[ANY] * ns, out_specs=[ANY] * no, out_shape=out_shapes,
        input_output_aliases=aliases or {}, scratch_shapes=[pltpu.SemaphoreType.DMA((n_rec,))] * 3)(*srcs)


def _gather_call(srcs, out_shapes, items):
    ns, no, n = len(srcs), len(out_shapes), len(items)

    def body(*refs):
        ins, outs = refs[:ns], refs[ns:ns + no]
        loc, sib_s, sib_r, ici_s, ici_r, fwd_s, fwd_r = refs[ns + no:]
        x, y, c = _coords()
        me, sib = (x, y, c), (x, y, 1 - c)
        chips = [(1 - x, y), (x, 1 - y), (1 - x, 1 - y)]
        index = lambda px, py, pc: 4 * px + 2 * py + pc
        for k, (si, oi, shard, block, _) in enumerate(items):
            src, mine = shard(ins[si]), block(outs[oi], index(*me))
            pltpu.make_async_copy(src, mine, loc.at[k]).start()
            _remote(src, mine, sib_s.at[k], sib_r.at[k], sib).start()
            for chip in chips:
                _remote(src, mine, ici_s.at[k], ici_r.at[k], (*chip, c)).start()
        for k, (si, oi, _, block, blocks) in enumerate(items):
            three = blocks(outs[oi], 3)
            _remote(three, three, ici_s.at[k], ici_r.at[k], me).wait_recv()
            for chip in chips:
                landed = block(outs[oi], index(*chip, c))
                _remote(landed, landed, fwd_s.at[k], fwd_r.at[k], sib).start()
        for k, (si, oi, _, _, blocks) in enumerate(items):
            one, three = blocks(outs[oi], 1), blocks(outs[oi], 3)
            _remote(one, one, sib_s.at[k], sib_r.at[k], me).wait()
            _remote(three, three, fwd_s.at[k], fwd_r.at[k], me).wait()
            _remote(three, three, ici_s.at[k], ici_r.at[k], me).wait_send()
            pltpu.make_async_copy(one, one, loc.at[k]).wait()

    return _call(
        body, name="gather_weights", in_specs=[ANY] * ns, out_specs=[ANY] * no, out_shape=out_shapes,
        scratch_shapes=[pltpu.SemaphoreType.DMA((n,))] * 7)(*srcs)


_BIG = {
    "w_in": (1, D, IN_W // NDEV, D, IN_W),
    "ssm_w_glu": (0, BW // NDEV, BW, BW, BW),
    "ssm_w_proj": (1, BW, D // NDEV, BW, D),
    "conv_w_proj": (1, BW, D // NDEV, BW, D),
    "pool_w_proj": (1, BW, D // NDEV, BW, D),
    "w_out": (0, D // NDEV, D, D, D),
    "ffn_w_gate": (0, HPAD, D, HIDP, D),
    "ffn_w_up": (0, HPAD, D, HIDP, D),
    "ffn_w_down": (0, HPAD, D, HIDP, D),
}


def _block_view(axis, size):
    if axis == 1:
        return lambda ref, q: ref.at[:, pl.ds(pl.multiple_of(q * size, 128), size)]
    return lambda ref, q: ref.at[pl.ds(pl.multiple_of(q * size, 16), size), :]


def _blocks_view(axis, size):
    if axis == 1:
        return lambda ref, n: ref.at[:, pl.ds(0, n * size)]
    return lambda ref, n: ref.at[pl.ds(0, n * size), :]


def _gather_weights(shards, conv_dw, layers):
    srcs, outs, items, where = [], [], [], {}
    for name, (axis, kk, nn, kf, nf) in _BIG.items():
        srcs.append(shards[name])
        size = nn if axis == 1 else kk
        for l in layers:
            where[(name, l)] = len(outs)
            outs.append(jax.ShapeDtypeStruct((kf, nf), BF16))
            items.append((len(srcs) - 1, len(outs) - 1, lambda ref, l=l: ref.at[l], _block_view(axis, size),
                          _blocks_view(axis, size)))
    srcs.append(conv_dw)
    outs.append(jax.ShapeDtypeStruct((NDEV,) + conv_dw.shape, conv_dw.dtype))
    items.append((len(srcs) - 1, len(outs) - 1, lambda ref: ref, lambda ref, q: ref.at[q],
                  lambda ref, n: ref.at[pl.ds(0, n)]))
    res = _gather_call(srcs, outs, items)
    full = {name: {l: res[where[(name, l)]] for l in layers} for name in _BIG}
    return full, res[-1]


def _gather_start(shards, layer, after):
    names = list(_BIG)
    lands = [jax.ShapeDtypeStruct((_BIG[n][3], _BIG[n][4]), BF16) for n in names]

    def copies(src_refs, land_refs, x, y, c):
        me = 4 * x + 2 * y + c
        peers = [(x, y, 1 - c), (1 - x, y, c), (x, 1 - y, c), (1 - x, 1 - y, c)]
        out = []
        for k, n in enumerate(names):
            axis, kk, nn, _, _ = _BIG[n]
            mine = _block_view(axis, nn if axis == 1 else kk)(land_refs[k], me)
            out.append([(peer, src_refs[k].at[layer], mine) for peer in peers])
        return out

    return _split_start("gather_start", [shards[n] for n in names], lands, copies, after)


def _gather_finish(handle, layer, after):
    names = list(_BIG)
    sizes = [(_BIG[n][0], _BIG[n][2] if _BIG[n][0] == 1 else _BIG[n][1]) for n in names]
    four = [functools.partial(lambda ref, bv: bv(ref, 4), bv=_blocks_view(axis, size)) for axis, size in sizes]
    lands = _split_wait("gather_wait", handle, four, after)
    n = len(names)

    def plan(ins, out_refs, x, y, c):
        me, sib = 4 * x + 2 * y + c, (x, y, 1 - c)
        chips = [(1 - x, y), (x, 1 - y), (1 - x, 1 - y)]
        recs = []
        for k, (axis, size) in enumerate(sizes):
            block, blocks = _block_view(axis, size), _blocks_view(axis, size)
            land = out_refs[k]
            remote = []
            for px, py in chips:
                landed = block(land, 4 * px + 2 * py + c)
                remote.append((sib, landed, landed))
            recs.append(dict(local=[(ins[n + k].at[layer], block(land, me))], remote=remote,
                             local_wait=blocks(land, 1), send_wait=blocks(land, 3), recv_wait=blocks(land, 3)))
        return recs

    res = _comm_call("gather_pair", lands + handle["srcs"], [jax.ShapeDtypeStruct(l.shape, l.dtype) for l in lands], n, plan,
                     aliases={k: k for k in range(n)})
    return dict(zip(names, res))


def _pair_add(name, grads, rcv, core):
    nl = len(grads)
    _, kk, nn = grads[0].shape

    def body(c_ref, *refs):
        l = pl.program_id(0)
        own = refs[0][...]
        for j in range(1, nl):
            own = jnp.where(l == j, refs[j][...], own)
        refs[nl + 1][...] = (own.astype(F32) + refs[nl][...].astype(F32)).astype(BF16)

    gspec = lambda j: pl.BlockSpec((None, kk, nn), lambda l, h, c_ref: (jnp.where(l == j, 2 * h + c_ref[0], 0), 0, 0))
    rspec = pl.BlockSpec((None, None, kk, nn), lambda l, h, c_ref: (h, l, 0, 0))
    return _call(
        body, name="pair_add_" + name,
        grid_spec=pltpu.PrefetchScalarGridSpec(num_scalar_prefetch=1, grid=(nl, NCHIP),
                                               in_specs=[gspec(j) for j in range(nl)] + [rspec], out_specs=rspec),
        out_shape=jax.ShapeDtypeStruct(rcv.shape, BF16),
        compiler_params=_cparams(("arbitrary", "arbitrary")))(core, *grads, rcv)


def _pair_add_small(owned, lists, core):
    on, ln = list(owned), list(lists)
    flat = []
    for n in on:
        flat += list(owned[n][0]) + [owned[n][1]]
    for n in ln:
        flat += list(lists[n][0]) + [lists[n][1]]

    def body(c_ref, *refs):
        outs = refs[len(flat):]
        c = c_ref[0]
        pos = 0
        for k, n in enumerate(on):
            nl = len(owned[n][0])
            for h in range(NCHIP):
                for l in range(nl):
                    outs[k][h, l] = refs[pos + l][pl.ds(2 * h + c, 1)][0] + refs[pos + nl][h, l]
            pos += nl + 1
        for k, n in enumerate(ln):
            nl = len(lists[n][0])
            for l in range(nl):
                outs[len(on) + k][l] = refs[pos + l][...] + refs[pos + nl][l]
            pos += nl + 1

    shapes = [jax.ShapeDtypeStruct(owned[n][1].shape, F32) for n in on]
    shapes += [jax.ShapeDtypeStruct(lists[n][1].shape, F32) for n in ln]
    res = _call(body, name="pair_add_small", out_shape=shapes,
                in_specs=[pl.BlockSpec(memory_space=pltpu.SMEM)] + [pl.BlockSpec(memory_space=pltpu.VMEM)] * len(flat),
                compiler_params=_cparams())(core, *flat)
    return dict(zip(on + ln, res))


def _pair_reduce(tag, big, by_owner, small, core):
    rs = {**big, **by_owner}
    srcs, outs, plans, rcv_at = [], [], [], {}
    for name, arrays in rs.items():
        rcv_at[name] = len(outs)
        outs.append(jax.ShapeDtypeStruct((NCHIP, len(arrays)) + arrays[0].shape[1:], arrays[0].dtype))
        for l, arr in enumerate(arrays):
            srcs.append(arr)
            plans.append((len(srcs) - 1, rcv_at[name], l, True))
    for name, arrays in small.items():
        rcv_at[name] = len(outs)
        outs.append(jax.ShapeDtypeStruct((len(arrays),) + arrays[0].shape, F32))
        for l, arr in enumerate(arrays):
            srcs.append(arr)
            plans.append((len(srcs) - 1, rcv_at[name], l, False))

    def plan_pair(ins, out_refs, x, y, c):
        sib = (x, y, 1 - c)
        recs = []
        for si, ro, l, slabs in plans:
            if slabs:
                four = out_refs[ro].at[pl.ds(0, NCHIP), l]
                recs.append(dict(remote=[(sib, ins[si].at[2 * h + 1 - c], out_refs[ro].at[h, l]) for h in range(NCHIP)],
                                 send_wait=four, recv_wait=four))
            else:
                dst = out_refs[ro].at[l]
                recs.append(dict(remote=[(sib, ins[si], dst)], send_wait=dst, recv_wait=dst))
        return recs

    res = _comm_call("pair_exchange_" + tag, srcs, outs, len(plans), plan_pair)
    part = {name: _pair_add(name, big[name], res[rcv_at[name]], core) for name in big}
    if by_owner or small:
        part.update(_pair_add_small({n: (by_owner[n], res[rcv_at[n]]) for n in by_owner},
                                    {n: (small[n], res[rcv_at[n]]) for n in small}, core))
    return part


def _chip_copies(src, land, slabbed, x, y, c):
    mine = 2 * x + y
    copies = []
    for step in range(1, NCHIP):
        h = (mine + step) % NCHIP
        copies.append(((h // 2, h % 2, c), src.at[h] if slabbed else src, land.at[mine]))
    return copies


def _chip_exchange(part, slabbed, keep_own):
    names = list(part)
    outs = [jax.ShapeDtypeStruct((() if n in slabbed else (NCHIP,)) + part[n].shape, part[n].dtype) for n in names]

    def plan(ins, out_refs, x, y, c):
        mine = 2 * x + y
        recs = []
        for k, n in enumerate(names):
            three = out_refs[k].at[pl.ds(0, NCHIP - 1)]
            rec = dict(remote=_chip_copies(ins[k], out_refs[k], n in slabbed, x, y, c), send_wait=three, recv_wait=three)
            if n in keep_own:
                rec["local"] = [(ins[k].at[mine] if n in slabbed else ins[k], out_refs[k].at[mine])]
                rec["local_wait"] = out_refs[k].at[0]
            recs.append(rec)
        return recs

    res = _comm_call("chip_exchange", [part[n] for n in names], outs, len(names), plan)
    return dict(zip(names, res))


HBM_SPEC = pl.BlockSpec(memory_space=pltpu.HBM)
SEM_SPEC = pl.BlockSpec(memory_space=pltpu.SEMAPHORE)
SPLIT_EFFECT = pltpu.SideEffectType.DATAFLOW_SIDE_EFFECTING


def _split_start(name, srcs, land_shapes, copies_fn, after):
    n = len(srcs)
    lands = [pltpu.with_memory_space_constraint(lax.empty(s.shape, s.dtype), pltpu.HBM) for s in land_shapes]

    def body(*refs):
        src_refs, land_refs = refs[:n], refs[n:2 * n]
        send_sem, recv_sem = refs[2 * n + 1], refs[2 * n + 2]
        token = refs[-1]
        x, y, c = _coords()
        for k, copies in enumerate(copies_fn(src_refs, land_refs, x, y, c)):
            for peer, src, dst in copies:
                _remote(src, dst, send_sem.at[k], recv_sem.at[k], peer).start()
        token[...] = jnp.zeros_like(token)

    res = pl.pallas_call(
        body, name=name,
        out_shape=(pltpu.SemaphoreType.DMA((n,)), pltpu.SemaphoreType.DMA((n,)),
                   *[pltpu.HBM(s.shape, s.dtype) for s in land_shapes], jax.ShapeDtypeStruct((8, 128), F32)),
        in_specs=[HBM_SPEC] * (2 * n) + [ANY],
        out_specs=(SEM_SPEC, SEM_SPEC, *[HBM_SPEC] * n, pl.BlockSpec(memory_space=pltpu.VMEM)),
        input_output_aliases={n + i: 2 + i for i in range(n)},
        compiler_params=pltpu.CompilerParams(has_side_effects=SPLIT_EFFECT),
    )(*[pltpu.with_memory_space_constraint(s, pltpu.HBM) for s in srcs], *lands, after)
    return dict(send=res[0], recv=res[1], srcs=list(srcs), lands=list(res[2:2 + n]), token=res[-1])


def _split_wait(name, handle, wait_views, after):
    n = len(handle["lands"])

    def body(*refs):
        land_refs = refs[:n]
        send_sem, recv_sem = refs[n], refs[n + 1]
        x, y, c = _coords()
        for k in range(n):
            w = wait_views[k](land_refs[k])
            cp = _remote(w, w, send_sem.at[k], recv_sem.at[k], (x, y, c))
            cp.wait_send()
            cp.wait_recv()

    res = pl.pallas_call(
        body, name=name,
        out_shape=tuple(pltpu.HBM(s.shape, s.dtype) for s in handle["lands"]),
        in_specs=[HBM_SPEC] * n + [SEM_SPEC, SEM_SPEC, ANY], out_specs=tuple([HBM_SPEC] * n),
        input_output_aliases={i: i for i in range(n)},
        compiler_params=pltpu.CompilerParams(has_side_effects=SPLIT_EFFECT),
    )(*handle["lands"], handle["send"], handle["recv"], after)
    return list(res)


def _adamw(w, g, m, v):
    m = ADAM_B1 * m + (1.0 - ADAM_B1) * g
    v = ADAM_B2 * v + (1.0 - ADAM_B2) * jnp.square(g)
    m_hat = m / (1.0 - ADAM_B1 ** ADAM_STEP)
    v_hat = v / (1.0 - ADAM_B2 ** ADAM_STEP)
    delta = -ADAM_LR * (m_hat / (jnp.sqrt(v_hat) + ADAM_EPS) + ADAM_WD * w)
    return delta, m, v


def _chip_start(part, tag):
    names = list(part)

    def copies(src_refs, land_refs, x, y, c):
        return [_chip_copies(src_refs[k], land_refs[k], True, x, y, c) for k in range(len(names))]

    shapes = [jax.ShapeDtypeStruct(part[n].shape, part[n].dtype) for n in names]
    return names, _split_start("chip_start_" + tag, [part[n] for n in names], shapes, copies, part[names[0]])


def _chip_wait(names, handle, after, tag):
    three = [lambda ref: ref.at[pl.ds(0, NCHIP - 1)]] * len(names)
    return dict(zip(names, _split_wait("chip_wait_" + tag, handle, three, after)))


def _sum_senders(ref):
    g = ref[0].astype(F32)
    for h in range(1, NCHIP):
        g = g + ref[h].astype(F32)
    return g


def _adam_big(name, own, recv, w, m, v, tk, chip):
    nl = len(own)
    kk, nn = w.shape[1], w.shape[2]
    nnp = own[0].shape[3]

    def body(chip_ref, *refs):
        l = pl.program_id(0)
        g = None
        for step in range(NCHIP):
            val = refs[step][...]
            for q in range(1, nl):
                val = jnp.where(l == q, refs[NCHIP * q + step][...], val)
            g = val.astype(F32) if g is None else g + val.astype(F32)
        w_ref, m_ref, v_ref, g_ref, d_ref, mo_ref, vo_ref = refs[NCHIP * nl:]
        g = g[:, :nn]
        delta, m2, v2 = _adamw(w_ref[...], g, m_ref[...], v_ref[...])
        g_ref[...] = g
        d_ref[...] = delta
        mo_ref[...] = m2
        vo_ref[...] = v2

    def slab(q, step):
        return pl.BlockSpec((None, None, tk, nnp), lambda l, i, chip_ref: (
            jnp.where(l == q, (chip_ref[0] + step) % NCHIP, 0), 0, jnp.where(l == q, i, 0), 0))

    in_specs, operands = [], []
    for q in range(nl):
        in_specs += [slab(q, step) for step in range(NCHIP)]
        operands += [own[q]] + [recv[q]] * (NCHIP - 1)
    wspec = pl.BlockSpec((None, tk, nn), lambda l, i, chip_ref: (l, i, 0))
    shape = jax.ShapeDtypeStruct(w.shape, F32)
    return _call(
        body, name="adamw_" + name,
        grid_spec=pltpu.PrefetchScalarGridSpec(num_scalar_prefetch=1, grid=(nl, kk // tk),
                                               in_specs=in_specs + [wspec] * 3, out_specs=[wspec] * 4),
        out_shape=[shape] * 4, compiler_params=_cparams(("arbitrary", "arbitrary")))(chip, *operands, w, m, v)


def _adam_small(names, recv, w, m, v):
    n = len(names)

    def body(*refs):
        r, ww, mm, vv = refs[:n], refs[n:2 * n], refs[2 * n:3 * n], refs[3 * n:4 * n]
        outs = refs[4 * n:]
        for k in range(n):
            g = _sum_senders(r[k])
            if g.shape != ww[k].shape:
                g = g[:, :ww[k].shape[1]]
            delta, m2, v2 = _adamw(ww[k][...], g, mm[k][...], vv[k][...])
            outs[k][...] = g
            outs[n + k][...] = delta
            outs[2 * n + k][...] = m2
            outs[3 * n + k][...] = v2

    shapes = [jax.ShapeDtypeStruct(w[k].shape, F32) for k in names]
    res = _call(body, name="adamw_small", out_shape=shapes * 4, compiler_params=_cparams())(
        *[recv[k] for k in names], *[w[k] for k in names], *[m[k] for k in names], *[v[k] for k in names])
    return {k: (res[i], res[n + i], res[2 * n + i], res[3 * n + i]) for i, k in enumerate(names)}


def _expand_b(bt):
    eye = jnp.eye(GB, dtype=bt.dtype)
    return jnp.einsum("jgpn,gh->jgphn", bt.reshape(NBLK, GB, SGRP, NSTATE), eye).reshape(NBLK, GB * SGRP, NS)


def _extract_b(db):
    x = db.reshape(NBLK, GB, SGRP, GB, NSTATE)
    eye = jnp.eye(GB, dtype=db.dtype)
    return jnp.einsum("jgphn,gh->jgpn", x, eye).reshape(NGRP, SGRP, NSTATE)


def _expand_c(c):
    ct = jnp.transpose(c, (0, 2, 1)).reshape(NBLK, GB, NSTATE, SGRP)
    eye = jnp.eye(GB, dtype=c.dtype)
    return jnp.einsum("jgnp,gh->jgnhp", ct, eye).reshape(NBLK, NS, GB * SGRP)


def _extract_c(dc):
    x = dc.reshape(NBLK, GB, NSTATE, GB, SGRP)
    eye = jnp.eye(GB, dtype=dc.dtype)
    d = jnp.einsum("jgnhp,gh->jgnp", x, eye).reshape(NGRP, NSTATE, SGRP)
    return jnp.transpose(d, (0, 2, 1))


_SMALL = ("norm1", "b_gate", "ssm_a_re", "ssm_a_im", "ssm_log_dt", "ssm_b_re", "ssm_b_im", "ssm_c_re", "ssm_c_im",
          "ssm_d", "ssm_b_glu", "conv_b_dw", "conv_ln_g", "conv_ln_b", "pool_w_group", "pool_scale", "norm2")
_ADAM_TK = {"w_in": 256, "ssm_w_glu": 64, "ssm_w_proj": 512, "conv_w_proj": 512, "pool_w_proj": 512, "w_out": 128,
            "ffn_w_gate": HSH, "ffn_w_up": HSH, "ffn_w_down": HSH}
_OUT_ORDER = ("norm1", "w_in", "b_gate", "ssm_a_re", "ssm_a_im", "ssm_log_dt", "ssm_b_re", "ssm_b_im", "ssm_c_re",
              "ssm_c_im", "ssm_d", "ssm_w_glu", "ssm_b_glu", "ssm_w_proj", "conv_w_dw", "conv_b_dw", "conv_ln_g",
              "conv_ln_b", "conv_w_proj", "pool_w_group", "pool_scale", "pool_w_proj", "w_out", "norm2", "ffn_w_gate",
              "ffn_w_up", "ffn_w_down", "final_norm")


def _layer_fwd(x, p, token):
    z = _inproj_fwd(x, p["norm1"], p["w_in"], token)
    yssm, hre, him = _ssm_fwd(z, p)
    cv = _conv_fwd(z, p["conv_w"], p["conv_b"])
    pbar = _pool_fwd(z)
    x1 = _merge_fwd(x, yssm, cv, pbar, z, p)
    x2, gpre, upre = _ffn_fwd(x1, p["norm2"], p["wg"], p["wu"], p["wd"])
    return x2, dict(x=x, z=z, yssm=yssm, hre=hre, him=him, cv=cv, pbar=pbar, x1=x1, gpre=gpre, upre=upre)


def _layer_bwd(dx, p, s, token, after_ffn=None):
    big, small = {}, {}
    dx1, d_norm2, dgp, dup, act, h2 = _ffn_bwd(dx, s["x1"], p["norm2"], s["gpre"], s["upre"], p["wg"], p["wu"], p["wd"],
                                               token)
    big["ffn_w_gate"] = _matmul_tn(dgp, h2, "tn_gate").reshape(NDEV, HPAD, D)
    big["ffn_w_up"] = _matmul_tn(dup, h2, "tn_up").reshape(NDEV, HPAD, D)
    big["ffn_w_down"] = _matmul_tn(act, dx, "tn_down").reshape(NDEV, HPAD, D)
    (dy, dcv, dpb, dzg, a_g, a_outa, a_hs, a_pb, a_pc, a_mg, c_glu, c_ya, c_yb, c_p, c_yc,
     d_bglu, d_lng, d_lnb, d_scale, d_bgate) = _merge_bwd(dx1, s["x"], s["yssm"], s["cv"], s["pbar"], s["z"], p,
                                                          after_ffn(dict(big)) if after_ffn else token)
    big["ssm_w_glu"] = _matmul_tn(a_g, c_glu, "tn_glu").reshape(NDEV, BW // NDEV, BW)
    big["ssm_w_proj"] = _matmul_tn(a_outa, c_ya, "tn_ssm_proj", D // NDEV)
    big["conv_w_proj"] = _matmul_tn(a_hs, c_yb, "tn_conv_proj", D // NDEV)
    big["pool_w_proj"] = _matmul_tn(a_pc, c_yc, "tn_pool_proj", D // NDEV)
    big["w_out"] = _matmul_tn(a_mg, dx1, "tn_out").reshape(NDEV, D // NDEV, D)
    d_wgrp = _group_tn(a_pb, c_p)
    du_a, dbr, dbi, dcr, dci, dd, dar, dai, dldt = _ssm_bwd(dy, s["z"], s["hre"], s["him"], p)
    dva, dvb, dw8, dcb = _conv_bwd(dcv, s["z"], p["conv_w"])
    du_c = _pool_bwd(dpb)
    dz = jnp.concatenate([du_a, dva, dvb, du_c, dzg], axis=1)
    dx0, d_norm1, h = _inproj_bwd(dz, dx1, s["x"], p["norm1"], p["w_in"])
    big["w_in"] = _matmul_tn(h, dz, "tn_in", IN_W // NDEV)
    small["norm1"] = d_norm1
    small["b_gate"] = d_bgate
    small["ssm_a_re"] = dar.reshape(NGRP, NSTATE)
    small["ssm_a_im"] = dai.reshape(NGRP, NSTATE)
    small["ssm_log_dt"] = dldt.reshape(NBLK, 8, 128)[:, 0, :GB].reshape(1, NGRP)
    small["ssm_b_re"] = _extract_b(dbr)
    small["ssm_b_im"] = _extract_b(dbi)
    small["ssm_c_re"] = _extract_c(dcr)
    small["ssm_c_im"] = _extract_c(dci)
    small["ssm_d"] = dd.reshape(NGRP, SGRP)
    small["ssm_b_glu"] = d_bglu
    small["conv_b_dw"] = dcb
    small["conv_ln_g"] = d_lng
    small["conv_ln_b"] = d_lnb
    small["pool_w_group"] = d_wgrp
    small["pool_scale"] = d_scale
    small["norm2"] = d_norm2
    return dx0, big, dw8, small


def _train_step(a):
    t_rows = a["x"].shape[1]
    x0 = a["x"].reshape(t_rows, D)
    target = a["loss_target"].reshape(t_rows, D)

    tr = lambda w: jnp.transpose(w, (0, 2, 1))
    pad_rows = lambda w: jnp.pad(w, ((0, 0), (0, HPAD - HSH), (0, 0)))
    shards = {
        "w_in": a["w_in"], "ssm_w_glu": a["ssm_w_glu"], "ssm_w_proj": a["ssm_w_proj"],
        "conv_w_proj": a["conv_w_proj"], "pool_w_proj": a["pool_w_proj"], "w_out": a["w_out"],
        "ffn_w_gate": pad_rows(tr(a["ffn_w_gate"])), "ffn_w_up": pad_rows(tr(a["ffn_w_up"])),
        "ffn_w_down": pad_rows(a["ffn_w_down"]),
    }
    shards = {k: v.astype(BF16) for k, v in shards.items()}
    full, dw_all = _gather_weights(shards, a["conv_w_dw"].reshape(DEPTH, CONV_K, BW // NDEV), (0,))
    conv_w = jnp.transpose(dw_all, (1, 2, 0, 3)).reshape(DEPTH, CONV_K, BW)
    late = _gather_start(shards, DEPTH - 1, dw_all)
    core = lax.axis_index("c").astype(jnp.int32).reshape(1)
    chip = (2 * lax.axis_index("x") + lax.axis_index("y")).astype(jnp.int32).reshape(1)
    no_token = jnp.zeros((8, 128), F32)

    def layer_params(l):
        row = lambda v: v.reshape(1, -1)
        return dict(
            norm1=row(a["norm1"][l]), w_in=full["w_in"][l],
            are=row(a["ssm_a_re"][l]), aim=row(a["ssm_a_im"][l]),
            ldt=row(jnp.repeat(a["ssm_log_dt"][l], NSTATE)),
            bexp_re=_expand_b(jnp.transpose(a["ssm_b_re"][l], (0, 2, 1))),
            bexp_im=_expand_b(jnp.transpose(a["ssm_b_im"][l], (0, 2, 1))),
            cexp_re=_expand_c(a["ssm_c_re"][l]), cexp_im=_expand_c(a["ssm_c_im"][l]),
            dskip=row(a["ssm_d"][l]),
            conv_w=conv_w[l], conv_b=row(a["conv_b_dw"][l]),
            wglu=full["ssm_w_glu"][l], bglu=row(a["ssm_b_glu"][l]), wpa=full["ssm_w_proj"][l],
            lng=row(a["conv_ln_g"][l]), lnb=row(a["conv_ln_b"][l]), wpb=full["conv_w_proj"][l],
            wgrp=a["pool_w_group"][l].astype(BF16), scale=row(a["pool_scale"][l]), wpc=full["pool_w_proj"][l],
            bgate=row(a["b_gate"][l]), wout=full["w_out"][l],
            norm2=row(a["norm2"][l]), wg=full["ffn_w_gate"][l], wu=full["ffn_w_up"][l], wd=full["ffn_w_down"][l],
        )

    p0 = layer_params(0)
    x, s0 = _layer_fwd(x0, p0, late["token"])
    for name, w in _gather_finish(late, DEPTH - 1, x).items():
        full[name][DEPTH - 1] = w
    params = [p0, layer_params(DEPTH - 1)]
    x, s1 = _layer_fwd(x, params[1], no_token)
    saved = [s0, s1]

    loss_part, dx, d_final = _loss_head(x, a["final_norm"].reshape(1, D), target)
    loss = lax.psum(loss_part[0, 0], ("x", "y", "c"))

    dx, gb1, go1, gs1 = _layer_bwd(dx, params[1], saved[1], no_token)
    part1 = _pair_reduce("late", {n: [g] for n, g in gb1.items()}, {}, {}, core)
    rs_names, rs_handle = _chip_start(part1, "late")
    early = {}

    def ffn_grads_leave_early(ffn):
        early["part"] = _pair_reduce("ffn", {n: [g] for n, g in ffn.items()}, {}, {}, core)
        early["names"], early["handle"] = _chip_start(early["part"], "ffn")
        return early["handle"]["token"]

    dx, gb0, go0, gs0 = _layer_bwd(dx, params[0], saved[0], rs_handle["token"], ffn_grads_leave_early)
    grad_x = dx.reshape(1, t_rows, D)
    small = {n: [gs0[n], gs1[n]] for n in _SMALL}
    small["final_norm"] = [d_final]
    part0 = _pair_reduce("rest", {n: [g] for n, g in gb0.items() if n not in early["part"]}, {"conv_w_dw": [go0, go1]},
                         small, core)
    recv = _chip_exchange(part0, set(_BIG) | {"conv_w_dw"}, {"conv_w_dw"} | set(small))
    recv1 = _chip_wait(rs_names, rs_handle, dx, "late")
    recv.update(_chip_wait(early["names"], early["handle"], dx, "ffn"))
    part0.update(early["part"])

    results = {}
    for name in _BIG:
        fix = tr if name in ("ffn_w_gate", "ffn_w_up") else (lambda t: t)
        res = _adam_big(name, [part0[name], part1[name]], [recv[name], recv1[name]], fix(a[name]), fix(a["m_" + name]),
                        fix(a["v_" + name]), _ADAM_TK[name], chip)
        results[name] = tuple(fix(r) for r in res)

    lay = {
        "norm1": lambda v: v.reshape(DEPTH, 1, D), "b_gate": lambda v: v.reshape(DEPTH, 1, 3 * D),
        "ssm_log_dt": lambda v: v.reshape(DEPTH, 1, NGRP),
        "ssm_b_re": lambda v: jnp.transpose(v, (0, 1, 3, 2)), "ssm_b_im": lambda v: jnp.transpose(v, (0, 1, 3, 2)),
        "ssm_b_glu": lambda v: v.reshape(DEPTH, 1, BW), "conv_b_dw": lambda v: v.reshape(DEPTH, 1, BW),
        "conv_ln_g": lambda v: v.reshape(DEPTH, 1, BW), "conv_ln_b": lambda v: v.reshape(DEPTH, 1, BW),
        "pool_scale": lambda v: v.reshape(DEPTH, 1, BW), "norm2": lambda v: v.reshape(DEPTH, 1, D),
        "conv_w_dw": lambda v: v.reshape(DEPTH, CONV_K, BW // NDEV), "final_norm": lambda v: v.reshape(1, 1, D),
    }
    names = _SMALL + ("conv_w_dw", "final_norm")
    relay = lambda k, v: lay[k](v) if k in lay else v
    sm = _adam_small(names, recv, {k: relay(k, a[k]) for k in names}, {k: relay(k, a["m_" + k]) for k in names},
                     {k: relay(k, a["v_" + k]) for k in names})
    for k in names:
        back = (lambda r: jnp.transpose(r, (0, 1, 3, 2))) if k in ("ssm_b_re", "ssm_b_im") else (lambda r: r.reshape(a[k].shape))
        results[k] = tuple(back(r) for r in sm[k])

    outs = [loss, grad_x]
    for part in range(4):
        outs += [results[k][part] for k in _OUT_ORDER]
    return tuple(outs)


def kernel(x, norm1, w_in, b_gate, ssm_a_re, ssm_a_im, ssm_log_dt, ssm_b_re, ssm_b_im, ssm_c_re, ssm_c_im, ssm_d, ssm_w_glu, ssm_b_glu, ssm_w_proj, conv_w_dw, conv_b_dw, conv_ln_g, conv_ln_b, conv_w_proj, pool_w_group, pool_scale, pool_w_proj, w_out, norm2, ffn_w_gate, ffn_w_up, ffn_w_down, final_norm, loss_target, m_norm1, m_w_in, m_b_gate, m_ssm_a_re, m_ssm_a_im, m_ssm_log_dt, m_ssm_b_re, m_ssm_b_im, m_ssm_c_re, m_ssm_c_im, m_ssm_d, m_ssm_w_glu, m_ssm_b_glu, m_ssm_w_proj, m_conv_w_dw, m_conv_b_dw, m_conv_ln_g, m_conv_ln_b, m_conv_w_proj, m_pool_w_group, m_pool_scale, m_pool_w_proj, m_w_out, m_norm2, m_ffn_w_gate, m_ffn_w_up, m_ffn_w_down, m_final_norm, v_norm1, v_w_in, v_b_gate, v_ssm_a_re, v_ssm_a_im, v_ssm_log_dt, v_ssm_b_re, v_ssm_b_im, v_ssm_c_re, v_ssm_c_im, v_ssm_d, v_ssm_w_glu, v_ssm_b_glu, v_ssm_w_proj, v_conv_w_dw, v_conv_b_dw, v_conv_ln_g, v_conv_ln_b, v_conv_w_proj, v_pool_w_group, v_pool_scale, v_pool_w_proj, v_w_out, v_norm2, v_ffn_w_gate, v_ffn_w_up, v_ffn_w_down, v_final_norm):
    return _train_step(dict(locals()))
```

```python
import functools

import jax
import jax.numpy as jnp
from jax import lax
from jax.experimental import pallas as pl
from jax.experimental.pallas import tpu as pltpu

F32 = jnp.float32
BF16 = jnp.bfloat16

NDEV = 8
DEPTH = 2
D = 1024
BW = 512
NSTATE = 64
SGRP = 16
NGRP = BW // SGRP
GB = 8
NBLK = NGRP // GB
NS = GB * NSTATE
CONV_K = 31
HALO = 32
PHALO = 16
IN_W = 5120
HID = 2816
HSH = HID // NDEV
HPAD = 384
HIDP = HPAD * NDEV
EPS = 1e-6
VMEM_LIMIT = 56 * 1024 * 1024

ADAM_LR, ADAM_B1, ADAM_B2, ADAM_EPS, ADAM_WD, ADAM_STEP = 0.001, 0.9, 0.999, 1e-08, 0.01, 10

MESH = pl.DeviceIdType.MESH
ANY = pl.BlockSpec(memory_space=pl.ANY)


def _call(body, **kw):
    return pl.pallas_call(body, **kw)


def _cparams(sem=None):
    return pltpu.CompilerParams(dimension_semantics=sem, vmem_limit_bytes=VMEM_LIMIT)


def _dot(a, b):
    return jnp.dot(a.astype(BF16), b.astype(BF16), preferred_element_type=F32)


def _dot_nt(a, b):
    return lax.dot_general(a.astype(BF16), b.astype(BF16), (((1,), (1,)), ((), ())), preferred_element_type=F32)


def _dot_tn(a, b):
    return lax.dot_general(a.astype(BF16), b.astype(BF16), (((0,), (0,)), ((), ())), preferred_element_type=F32)


@jax.custom_vjp
def _mm(a, w):
    return _dot(a, w)


def _mm_fwd(a, w):
    return _dot(a, w), w


def _mm_bwd(w, ct):
    return _dot_nt(ct, w), jnp.zeros_like(w)


_mm.defvjp(_mm_fwd, _mm_bwd)


def _rms(x, g):
    return x * lax.rsqrt(jnp.mean(x * x, axis=-1, keepdims=True) + EPS) * g


def _disc(are, aim, ldt):
    dt = jnp.exp(ldt)
    mag = jnp.exp(dt * are)
    ang = dt * aim
    abr = mag * jnp.cos(ang)
    abi = mag * jnp.sin(ang)
    den = are * are + aim * aim
    nr = abr - 1.0
    fr = (nr * are + abi * aim) / den
    fi = (abi * are - nr * aim) / den
    return abr, abi, fr, fi


def _bbar(fr, fi, br, bi):
    return fr * br - fi * bi, fr * bi + fi * br


def _cmul(ar, ai, br, bi):
    return ar * br - ai * bi, ar * bi + ai * br


def _scan_rows(re_ref, im_ref, ar, ai, n_rows, reverse, hre_ref=None, him_ref=None):
    n = ar.shape[1]
    shape = (8, n)
    rows = lax.broadcasted_iota(jnp.int32, shape, 0)
    a1 = (jnp.broadcast_to(ar, shape), jnp.broadcast_to(ai, shape))
    a2 = _cmul(*a1, *a1)
    a4 = _cmul(*a2, *a2)
    pr = jnp.zeros(shape, F32)
    pi = jnp.zeros(shape, F32)
    pw = a1
    for k in range(8):
        sel = rows == ((7 - k) if reverse else k)
        pr = jnp.where(sel, pw[0], pr)
        pi = jnp.where(sel, pw[1], pi)
        pw = _cmul(*pw, *a1)
    nt = n_rows // 8
    with_acc = hre_ref is not None

    def body(i, carry):
        cr, ci = carry[0], carry[1]
        t = (nt - 1 - i) if reverse else i
        off = pl.multiple_of(t * 8, 8)
        xr = re_ref[pl.ds(off, 8), :]
        xi = im_ref[pl.ds(off, 8), :]
        for k, (kr, ki) in ((1, a1), (2, a2), (4, a4)):
            if reverse:
                keep, sh = rows < 8 - k, 8 - k
            else:
                keep, sh = rows >= k, k
            sr = jnp.where(keep, pltpu.roll(xr, sh, 0), 0.0)
            si = jnp.where(keep, pltpu.roll(xi, sh, 0), 0.0)
            xr, xi = xr + kr * sr - ki * si, xi + kr * si + ki * sr
        xr, xi = xr + pr * cr - pi * ci, xi + pr * ci + pi * cr
        re_ref[pl.ds(off, 8), :] = xr
        im_ref[pl.ds(off, 8), :] = xi
        edge = 0 if reverse else 7
        out = (jnp.broadcast_to(xr[edge:edge + 1, :], shape), jnp.broadcast_to(xi[edge:edge + 1, :], shape))
        if with_acc:
            hr = hre_ref[pl.ds(off, 8), :]
            hi = him_ref[pl.ds(off, 8), :]
            offp = pl.multiple_of(jnp.maximum(t - 1, 0) * 8, 8)
            live = jnp.where(t > 0, 1.0, 0.0)
            lr = jnp.broadcast_to(hre_ref[pl.ds(offp, 8), :][7:8, :], shape) * live
            li = jnp.broadcast_to(him_ref[pl.ds(offp, 8), :][7:8, :], shape) * live
            hpr = jnp.where(rows == 0, lr, pltpu.roll(hr, 1, 0))
            hpi = jnp.where(rows == 0, li, pltpu.roll(hi, 1, 0))
            out = out + (carry[2] + xr * hpr + xi * hpi, carry[3] + xi * hpr - xr * hpi)
        return out

    zero = jnp.zeros(shape, F32)
    init = (zero, zero, zero, zero) if with_acc else (zero, zero)
    res = lax.fori_loop(0, nt, body, init)
    return res[2:] if with_acc else None


TOKEN_SPEC = pl.BlockSpec((8, 128), lambda i, j: (0, 0))


def _inproj_fwd(x, gamma, w, token, tm=1024, nb=2):
    t_rows = x.shape[0]
    tm = min(tm, t_rows)
    oc = w.shape[2]
    n = NDEV * oc
    tn = nb * oc

    def body(x_ref, g_ref, w_ref, token_ref, z_ref, h_ref):
        @pl.when(pl.program_id(1) == 0)
        def _():
            h_ref[...] = _rms(x_ref[...], g_ref[...]).astype(BF16)
        for q in range(nb):
            z_ref[:, oc * q:oc * (q + 1)] = jnp.dot(h_ref[...], w_ref[q], preferred_element_type=F32)

    return _call(
        body, name="inproj_fwd", grid=(t_rows // tm, n // tn),
        in_specs=[pl.BlockSpec((tm, D), lambda i, j: (i, 0)), pl.BlockSpec((1, D), lambda i, j: (0, 0)),
                  pl.BlockSpec((nb, D, oc), lambda i, j: (j, 0, 0)), TOKEN_SPEC],
        out_specs=pl.BlockSpec((tm, tn), lambda i, j: (i, j)),
        out_shape=jax.ShapeDtypeStruct((t_rows, n), F32),
        scratch_shapes=[pltpu.VMEM((tm, D), BF16)],
        compiler_params=_cparams(("parallel", "arbitrary")))(x, gamma, w, token)


def _ssm_specs(t_rows):
    row = pl.BlockSpec((1, NS), lambda j: (0, j))
    return dict(
        u=pl.BlockSpec((t_rows, GB * SGRP), lambda j: (0, j)),
        row=row,
        bexp=pl.BlockSpec((None, GB * SGRP, NS), lambda j: (j, 0, 0)),
        cexp=pl.BlockSpec((None, NS, GB * SGRP), lambda j: (j, 0, 0)),
        d=pl.BlockSpec((1, GB * SGRP), lambda j: (0, j)),
        h=pl.BlockSpec((t_rows, NS), lambda j: (0, j)),
    )


def _ssm_fwd(z, p):
    t_rows = z.shape[0]
    s = _ssm_specs(t_rows)

    def body(u_ref, are_ref, aim_ref, ldt_ref, br_ref, bi_ref, cr_ref, ci_ref, d_ref, y_ref, hr_ref, hi_ref):
        abr, abi, fr, fi = _disc(are_ref[...], aim_ref[...], ldt_ref[...])
        bbr, bbi = _bbar(fr, fi, br_ref[...], bi_ref[...])
        u = u_ref[...]
        hr_ref[...] = _dot(u, bbr)
        hi_ref[...] = _dot(u, bbi)
        _scan_rows(hr_ref, hi_ref, abr, abi, t_rows, False)
        y_ref[...] = _dot(hr_ref[...], cr_ref[...]) - _dot(hi_ref[...], ci_ref[...]) + d_ref[...] * u

    return _call(
        body, name="ssm_fwd", grid=(NBLK,),
        in_specs=[s["u"], s["row"], s["row"], s["row"], s["bexp"], s["bexp"], s["cexp"], s["cexp"], s["d"]],
        out_specs=[s["u"], s["h"], s["h"]],
        out_shape=[jax.ShapeDtypeStruct((t_rows, BW), F32), jax.ShapeDtypeStruct((t_rows, NGRP * NSTATE), F32),
                   jax.ShapeDtypeStruct((t_rows, NGRP * NSTATE), F32)],
        compiler_params=_cparams(("parallel",)))(
            z, p["are"], p["aim"], p["ldt"], p["bexp_re"], p["bexp_im"], p["cexp_re"], p["cexp_im"], p["dskip"])


def _ssm_bwd(dy, z, hre, him, p):
    t_rows = z.shape[0]
    s = _ssm_specs(t_rows)
    nstates = NGRP * NSTATE

    def body(dy_ref, u_ref, hr_ref, hi_ref, are_ref, aim_ref, ldt_ref, br_ref, bi_ref, cr_ref, ci_ref, d_ref,
             du_ref, dbr_ref, dbi_ref, dcr_ref, dci_ref, dd_ref, dar_ref, dai_ref, dldt_ref, lr_ref, li_ref):
        rows3 = (are_ref[...], aim_ref[...], ldt_ref[...])
        (abr, abi, fr, fi), disc_vjp = jax.vjp(_disc, *rows3)
        (bbr, bbi), bbar_vjp = jax.vjp(_bbar, fr, fi, br_ref[...], bi_ref[...])
        dy = dy_ref[...]
        u = u_ref[...]
        lr_ref[...] = _dot_nt(dy, cr_ref[...])
        li_ref[...] = -_dot_nt(dy, ci_ref[...])
        dcr_ref[...] = _dot_tn(hr_ref[...], dy)
        dci_ref[...] = -_dot_tn(hi_ref[...], dy)
        dd_ref[...] = jnp.sum(dy * u, axis=0, keepdims=True)
        acc_r, acc_i = _scan_rows(lr_ref, li_ref, abr, -abi, t_rows, True, hr_ref, hi_ref)
        dabr = jnp.sum(acc_r, axis=0, keepdims=True)
        dabi = jnp.sum(acc_i, axis=0, keepdims=True)
        lam_r = lr_ref[...]
        lam_i = li_ref[...]
        du = d_ref[...] * dy + _dot_nt(lam_r, bbr) + _dot_nt(lam_i, bbi)
        du_ref[...] = du.astype(BF16)
        dbbr = _dot_tn(u, lam_r)
        dbbi = _dot_tn(u, lam_i)
        dfr, dfi, dbr, dbi = bbar_vjp((dbbr, dbbi))
        dbr_ref[...] = dbr
        dbi_ref[...] = dbi
        dar, dai, dldt = disc_vjp((dabr, dabi, dfr, dfi))
        dar_ref[...] = dar
        dai_ref[...] = dai
        lane_grp = lax.broadcasted_iota(jnp.int32, (NS, 128), 0) // NSTATE
        col = lax.broadcasted_iota(jnp.int32, (NS, 128), 1)
        seg = jnp.where(lane_grp == col, 1.0, 0.0).astype(F32)
        dldt_ref[...] = jnp.dot(jnp.broadcast_to(dldt, (8, NS)), seg, preferred_element_type=F32,
                                precision=lax.Precision.HIGHEST)

    dyspec = pl.BlockSpec((t_rows, GB * SGRP), lambda j: (0, j))
    return _call(
        body, name="ssm_bwd", grid=(NBLK,),
        in_specs=[dyspec, s["u"], s["h"], s["h"], s["row"], s["row"], s["row"], s["bexp"], s["bexp"], s["cexp"],
                  s["cexp"], s["d"]],
        out_specs=[dyspec, s["bexp"], s["bexp"], s["cexp"], s["cexp"], s["d"], s["row"], s["row"],
                   pl.BlockSpec((8, 128), lambda j: (j, 0))],
        out_shape=[jax.ShapeDtypeStruct((t_rows, BW), BF16),
                   jax.ShapeDtypeStruct((NBLK, GB * SGRP, NS), F32), jax.ShapeDtypeStruct((NBLK, GB * SGRP, NS), F32),
                   jax.ShapeDtypeStruct((NBLK, NS, GB * SGRP), F32), jax.ShapeDtypeStruct((NBLK, NS, GB * SGRP), F32),
                   jax.ShapeDtypeStruct((1, BW), F32), jax.ShapeDtypeStruct((1, nstates), F32),
                   jax.ShapeDtypeStruct((1, nstates), F32), jax.ShapeDtypeStruct((NBLK * 8, 128), F32)],
        scratch_shapes=[pltpu.VMEM((t_rows, NS), F32), pltpu.VMEM((t_rows, NS), F32)],
        compiler_params=_cparams(("parallel",)))(
            dy, z, hre, him, p["are"], p["aim"], p["ldt"], p["bexp_re"], p["bexp_im"], p["cexp_re"], p["cexp_im"],
            p["dskip"])


def _conv_fwd(z, w, b, tm=256):
    t_rows = z.shape[0]
    hb = tm // HALO

    def body(va_ref, vb_ref, ha_ref, hb_ref, w_ref, b_ref, o_ref, win_ref):
        live = jnp.where(pl.program_id(0) > 0, 1.0, 0.0)
        win_ref[0:HALO, :] = ha_ref[...] * jax.nn.sigmoid(hb_ref[...]) * live
        win_ref[HALO:HALO + tm, :] = va_ref[...] * jax.nn.sigmoid(vb_ref[...])
        acc = jnp.broadcast_to(b_ref[...], (tm, BW))
        for k in range(CONV_K):
            acc = acc + w_ref[k:k + 1, :] * win_ref[pl.ds(HALO - (CONV_K - 1) + k, tm), :]
        o_ref[...] = acc

    halo = lambda col: pl.BlockSpec((HALO, BW), lambda i: (jnp.maximum(i * hb - 1, 0), col))
    return _call(
        body, name="conv_fwd", grid=(t_rows // tm,),
        in_specs=[pl.BlockSpec((tm, BW), lambda i: (i, 1)), pl.BlockSpec((tm, BW), lambda i: (i, 2)), halo(1), halo(2),
                  pl.BlockSpec((CONV_K, BW), lambda i: (0, 0)), pl.BlockSpec((1, BW), lambda i: (0, 0))],
        out_specs=pl.BlockSpec((tm, BW), lambda i: (i, 0)),
        out_shape=jax.ShapeDtypeStruct((t_rows, BW), F32),
        scratch_shapes=[pltpu.VMEM((HALO + tm, BW), F32)],
        compiler_params=_cparams(("parallel",)))(z, z, z, z, w, b)


def _conv_bwd(dcv, z, w, tm=256):
    t_rows = z.shape[0]
    nt = t_rows // tm
    hb = tm // HALO
    csh = BW // NDEV

    def body(d_ref, dn_ref, va_ref, vb_ref, ha_ref, hb_ref, w_ref, dva_ref, dvb_ref, dw8_ref, db_ref,
             hwin_ref, dwin_ref, dw_ref):
        i = pl.program_id(0)

        @pl.when(i == 0)
        def _():
            dw_ref[...] = jnp.zeros_like(dw_ref)
            db_ref[...] = jnp.zeros_like(db_ref)

        live_prev = jnp.where(i > 0, 1.0, 0.0)
        live_next = jnp.where(i < nt - 1, 1.0, 0.0)
        va = va_ref[...]
        sig = jax.nn.sigmoid(vb_ref[...])
        hwin_ref[0:HALO, :] = ha_ref[...] * jax.nn.sigmoid(hb_ref[...]) * live_prev
        hwin_ref[HALO:HALO + tm, :] = va * sig
        d = d_ref[...]
        dwin_ref[0:tm, :] = d
        dwin_ref[tm:tm + HALO, :] = dn_ref[...] * live_next
        dh = jnp.zeros((tm, BW), F32)
        dws = []
        for k in range(CONV_K):
            dh = dh + w_ref[k:k + 1, :] * dwin_ref[pl.ds(CONV_K - 1 - k, tm), :]
            dws.append(jnp.sum(d * hwin_ref[pl.ds(HALO - (CONV_K - 1) + k, tm), :], axis=0, keepdims=True))
        dws.append(jnp.zeros((1, BW), F32))
        dw_ref[...] += jnp.concatenate(dws, axis=0)
        db_ref[...] += jnp.sum(d, axis=0, keepdims=True)
        dva_ref[...] = (dh * sig).astype(BF16)
        dvb_ref[...] = (dh * va * sig * (1.0 - sig)).astype(BF16)

        @pl.when(i == nt - 1)
        def _():
            acc = dw_ref[...]
            for q in range(NDEV):
                dw8_ref[q] = acc[:, csh * q:csh * (q + 1)]

    halo = lambda col: pl.BlockSpec((HALO, BW), lambda i: (jnp.maximum(i * hb - 1, 0), col))
    return _call(
        body, name="conv_bwd", grid=(nt,),
        in_specs=[pl.BlockSpec((tm, BW), lambda i: (i, 0)),
                  pl.BlockSpec((HALO, BW), lambda i: (jnp.minimum((i + 1) * hb, t_rows // HALO - 1), 0)),
                  pl.BlockSpec((tm, BW), lambda i: (i, 1)), pl.BlockSpec((tm, BW), lambda i: (i, 2)), halo(1), halo(2),
                  pl.BlockSpec((CONV_K, BW), lambda i: (0, 0))],
        out_specs=[pl.BlockSpec((tm, BW), lambda i: (i, 0)), pl.BlockSpec((tm, BW), lambda i: (i, 0)),
                   pl.BlockSpec((NDEV, 32, csh), lambda i: (0, 0, 0)), pl.BlockSpec((1, BW), lambda i: (0, 0))],
        out_shape=[jax.ShapeDtypeStruct((t_rows, BW), BF16), jax.ShapeDtypeStruct((t_rows, BW), BF16),
                   jax.ShapeDtypeStruct((NDEV, 32, csh), F32), jax.ShapeDtypeStruct((1, BW), F32)],
        scratch_shapes=[pltpu.VMEM((HALO + tm, BW), F32), pltpu.VMEM((tm + HALO, BW), F32), pltpu.VMEM((32, BW), F32)],
        compiler_params=_cparams(("arbitrary",)))(dcv, dcv, z, z, z, z, w)


def _pool_rows(i, tm, n_rows, first_row):
    grp = lax.broadcasted_iota(jnp.int32, (1, BW), 1) // (BW // 4)
    wlen = jnp.where(grp == 0, 2.0, jnp.where(grp == 1, 4.0, jnp.where(grp == 2, 8.0, 16.0)))
    t = (i * tm + first_row + lax.broadcasted_iota(jnp.int32, (n_rows, 1), 0)).astype(F32)
    return grp, 1.0 / jnp.minimum(t + 1.0, wlen)


def _pool_pick(grp, s2, s4, s8, s16):
    return jnp.where(grp == 0, s2, jnp.where(grp == 1, s4, jnp.where(grp == 2, s8, s16)))


def _pool_fwd(z, tm=256):
    t_rows = z.shape[0]
    hb = tm // PHALO

    def body(u_ref, h_ref, o_ref):
        i = pl.program_id(0)
        u = u_ref[...]
        win = jnp.concatenate([h_ref[...] * jnp.where(i > 0, 1.0, 0.0), u], axis=0)
        s2 = win + pltpu.roll(win, 1, 0)
        s4 = s2 + pltpu.roll(s2, 2, 0)
        s8 = s4 + pltpu.roll(s4, 4, 0)
        s16 = s8 + pltpu.roll(s8, 8, 0)
        grp, inv = _pool_rows(i, tm, tm, 0)
        o_ref[...] = _pool_pick(grp, s2, s4, s8, s16)[PHALO:, :] * inv - u

    return _call(
        body, name="pool_fwd", grid=(t_rows // tm,),
        in_specs=[pl.BlockSpec((tm, BW), lambda i: (i, 3)),
                  pl.BlockSpec((PHALO, BW), lambda i: (jnp.maximum(i * hb - 1, 0), 3))],
        out_specs=pl.BlockSpec((tm, BW), lambda i: (i, 0)),
        out_shape=jax.ShapeDtypeStruct((t_rows, BW), F32),
        compiler_params=_cparams(("parallel",)))(z, z)


def _pool_bwd(dp, tm=256):
    t_rows = dp.shape[0]
    nt = t_rows // tm
    hb = tm // PHALO
    ln = tm + PHALO

    def body(d_ref, dn_ref, o_ref):
        i = pl.program_id(0)
        d = d_ref[...]
        grp, inv = _pool_rows(i, tm, ln, 0)
        win = jnp.concatenate([d, dn_ref[...] * jnp.where(i < nt - 1, 1.0, 0.0)], axis=0) * inv
        s2 = win + pltpu.roll(win, ln - 1, 0)
        s4 = s2 + pltpu.roll(s2, ln - 2, 0)
        s8 = s4 + pltpu.roll(s4, ln - 4, 0)
        s16 = s8 + pltpu.roll(s8, ln - 8, 0)
        o_ref[...] = (_pool_pick(grp, s2, s4, s8, s16)[:tm, :] - d).astype(BF16)

    return _call(
        body, name="pool_bwd", grid=(nt,),
        in_specs=[pl.BlockSpec((tm, BW), lambda i: (i, 0)),
                  pl.BlockSpec((PHALO, BW), lambda i: (jnp.minimum((i + 1) * hb, t_rows // PHALO - 1), 0))],
        out_specs=pl.BlockSpec((tm, BW), lambda i: (i, 0)),
        out_shape=jax.ShapeDtypeStruct((t_rows, BW), BF16),
        compiler_params=_cparams(("parallel",)))(dp, dp)


_MERGE_W = ("wglu", "bglu", "wpa", "lng", "lnb", "wpb", "wgrp", "scale", "wpc", "bgate", "wout")
_MERGE_SMALL = ("bglu", "lng", "lnb", "scale", "bgate")
_MERGE_BLOCKED = ("wpa", "wpb", "wpc")


def _merge_load(name, ref):
    if name in _MERGE_BLOCKED:
        return jnp.concatenate([ref[q] for q in range(NDEV)], axis=1)
    return ref[...]


def _merge_math(x, yssm, cv, pbar, zg, w, taps):
    t_glu, t_ya, t_yb, t_p, t_yc = taps
    g = jax.nn.gelu(yssm)
    outa = g * jax.nn.sigmoid(_mm(g, w["wglu"]) + t_glu + w["bglu"])
    ya = _mm(outa, w["wpa"]) + t_ya
    mu = jnp.mean(cv, axis=-1, keepdims=True)
    var = jnp.mean(jnp.square(cv - mu), axis=-1, keepdims=True)
    hs = jax.nn.silu((cv - mu) * lax.rsqrt(var + EPS) * w["lng"] + w["lnb"])
    yb = _mm(hs, w["wpb"]) + t_yb
    gw = BW // 4
    pk = jnp.concatenate([_mm(pbar[:, gw * k:gw * (k + 1)], w["wgrp"][k]) for k in range(4)], axis=1) + t_p
    pc = pk * w["scale"]
    yc = _mm(pc, w["wpc"]) + t_yc
    gates = jax.nn.sigmoid(zg + w["bgate"])
    merged = gates[:, :D] * ya + gates[:, D:2 * D] * yb + gates[:, 2 * D:] * yc
    x1 = x + _mm(merged, w["wout"])
    acts = tuple(a.astype(BF16) for a in (g, outa, hs, pbar, pc, merged))
    return x1, acts


def _merge_specs(tm, p):
    rows = lambda width, col=0: pl.BlockSpec((tm, width), lambda i, c=col: (i, c))
    data = [rows(D), rows(BW), rows(BW), rows(BW), rows(D, 2), rows(D, 3), rows(D, 4)]
    wspecs = []
    for name in _MERGE_W:
        nd = p[name].ndim
        wspecs.append(pl.BlockSpec(p[name].shape, lambda i, nd=nd: (0,) * nd))
    return rows, data, wspecs


def _merge_fwd(x, yssm, cv, pbar, z, p, token, tm=256):
    t_rows = x.shape[0]
    rows, data, wspecs = _merge_specs(tm, p)

    def body(x_ref, y_ref, cv_ref, pb_ref, za_ref, zb_ref, zc_ref, *rest):
        w = {name: _merge_load(name, r) for name, r in zip(_MERGE_W, rest[:len(_MERGE_W)])}
        o_ref = rest[len(_MERGE_W) + 1]
        taps = (0.0, 0.0, 0.0, 0.0, 0.0)
        zg = jnp.concatenate([za_ref[...], zb_ref[...], zc_ref[...]], axis=1)
        o_ref[...] = _merge_math(x_ref[...], y_ref[...], cv_ref[...], pb_ref[...], zg, w, taps)[0]

    return _call(
        body, name="merge_fwd", grid=(t_rows // tm,),
        in_specs=data + wspecs + [pl.BlockSpec((8, 128), lambda i: (0, 0))], out_specs=rows(D),
        out_shape=jax.ShapeDtypeStruct((t_rows, D), F32),
        compiler_params=_cparams(("parallel",)))(x, yssm, cv, pbar, z, z, z, *[p[n] for n in _MERGE_W], token)


def _merge_bwd(dx1, x, yssm, cv, pbar, z, p, token, tm=256):
    t_rows = x.shape[0]
    rows, data, wspecs = _merge_specs(tm, p)
    nw = len(_MERGE_W)

    def body(dx_ref, x_ref, y_ref, cv_ref, pb_ref, za_ref, zb_ref, zc_ref, *rest):
        w = {name: _merge_load(name, r) for name, r in zip(_MERGE_W, rest[:nw])}
        zg = jnp.concatenate([za_ref[...], zb_ref[...], zc_ref[...]], axis=1)
        outs = rest[nw + 1:]
        small = {n: w[n] for n in _MERGE_SMALL}
        taps = (jnp.zeros((tm, BW), F32), jnp.zeros((tm, D), F32), jnp.zeros((tm, D), F32),
                jnp.zeros((tm, BW), F32), jnp.zeros((tm, D), F32))

        def f(yssm_, cv_, pbar_, zg_, small_, taps_):
            return _merge_math(x_ref[...], yssm_, cv_, pbar_, zg_, {**w, **small_}, taps_)

        _, vjp, acts = jax.vjp(f, y_ref[...], cv_ref[...], pb_ref[...], zg, small, taps, has_aux=True)
        dy, dcv, dpb, dzg, dsmall, dtaps = vjp(dx_ref[...])
        outs[0][...] = dy
        outs[1][...] = dcv
        outs[2][...] = dpb
        outs[3][...] = dzg.astype(BF16)
        for k in range(6):
            outs[4 + k][...] = acts[k]
        for k in range(5):
            outs[10 + k][...] = dtaps[k].astype(BF16)

        @pl.when(pl.program_id(0) == 0)
        def _():
            for k in range(5):
                outs[15 + k][...] = jnp.zeros_like(outs[15 + k])

        for k, n in enumerate(_MERGE_SMALL):
            outs[15 + k][...] += dsmall[n]

    f32o = lambda width: jax.ShapeDtypeStruct((t_rows, width), F32)
    bfo = lambda width: jax.ShapeDtypeStruct((t_rows, width), BF16)
    small_shapes = [jax.ShapeDtypeStruct(p[n].shape, F32) for n in _MERGE_SMALL]
    small_specs = [pl.BlockSpec(p[n].shape, lambda i: (0, 0)) for n in _MERGE_SMALL]
    out_shape = ([f32o(BW), f32o(BW), f32o(BW), bfo(3 * D)]
                 + [bfo(BW), bfo(BW), bfo(BW), bfo(BW), bfo(BW), bfo(D)]
                 + [bfo(BW), bfo(D), bfo(D), bfo(BW), bfo(D)] + small_shapes)
    out_specs = ([rows(BW), rows(BW), rows(BW), rows(3 * D)]
                 + [rows(BW)] * 5 + [rows(D)]
                 + [rows(BW), rows(D), rows(D), rows(BW), rows(D)] + small_specs)
    return _call(
        body, name="merge_bwd", grid=(t_rows // tm,),
        in_specs=[rows(D)] + data + wspecs + [pl.BlockSpec((8, 128), lambda i: (0, 0))], out_specs=out_specs,
        out_shape=out_shape, compiler_params=_cparams(("arbitrary",)))(
            dx1, x, yssm, cv, pbar, z, z, z, *[p[n] for n in _MERGE_W], token)


def _ffn_fwd(x1, gamma, wg, wu, wd, tm=1024, th=512):
    t_rows = x1.shape[0]
    tm = min(tm, t_rows)
    nh = HIDP // th

    def body(x_ref, g_ref, wg_ref, wu_ref, wd_ref, o_ref, gp_ref, up_ref, h_ref, acc_ref):
        j = pl.program_id(1)

        @pl.when(j == 0)
        def _():
            h_ref[...] = _rms(x_ref[...], g_ref[...]).astype(BF16)
            acc_ref[...] = jnp.zeros_like(acc_ref)

        gp = _dot_nt(h_ref[...], wg_ref[...])
        up = _dot_nt(h_ref[...], wu_ref[...])
        gp_ref[...] = gp
        up_ref[...] = up
        acc_ref[...] += _dot(jax.nn.silu(gp) * up, wd_ref[...])

        @pl.when(j == nh - 1)
        def _():
            o_ref[...] = x_ref[...] + acc_ref[...]

    return _call(
        body, name="ffn_fwd", grid=(t_rows // tm, nh),
        in_specs=[pl.BlockSpec((tm, D), lambda i, j: (i, 0)), pl.BlockSpec((1, D), lambda i, j: (0, 0)),
                  pl.BlockSpec((th, D), lambda i, j: (j, 0)), pl.BlockSpec((th, D), lambda i, j: (j, 0)),
                  pl.BlockSpec((th, D), lambda i, j: (j, 0))],
        out_specs=[pl.BlockSpec((tm, D), lambda i, j: (i, 0)), pl.BlockSpec((tm, th), lambda i, j: (i, j)),
                   pl.BlockSpec((tm, th), lambda i, j: (i, j))],
        out_shape=[jax.ShapeDtypeStruct((t_rows, D), F32), jax.ShapeDtypeStruct((t_rows, HIDP), F32),
                   jax.ShapeDtypeStruct((t_rows, HIDP), F32)],
        scratch_shapes=[pltpu.VMEM((tm, D), BF16), pltpu.VMEM((tm, D), F32)],
        compiler_params=_cparams(("parallel", "arbitrary")))(x1, gamma, wg, wu, wd)


def _rms_bwd_tail(x, gamma, dh):
    _, vjp = jax.vjp(_rms, x, gamma)
    return vjp(dh)


def _ffn_bwd(dx2, x1, gamma, gpre, upre, wg, wu, wd, token, tm=512, th=1024):
    t_rows = x1.shape[0]
    nh = HIDP // th

    def body(d_ref, x_ref, g_ref, gp_ref, up_ref, wg_ref, wu_ref, wd_ref, token_ref,
             dx_ref, dgam_ref, dgp_ref, dup_ref, act_ref, h_ref, acc_ref):
        i = pl.program_id(0)
        j = pl.program_id(1)

        @pl.when(j == 0)
        def _():
            acc_ref[...] = jnp.zeros_like(acc_ref)

        @pl.when((i == 0) & (j == 0))
        def _():
            dgam_ref[...] = jnp.zeros_like(dgam_ref)

        dact = _dot_nt(d_ref[...], wd_ref[...])
        gp = gp_ref[...]
        up = up_ref[...]
        sg = jax.nn.sigmoid(gp)
        silu = gp * sg
        dgp = (dact * up * (sg * (1.0 + gp * (1.0 - sg)))).astype(BF16)
        dup = (dact * silu).astype(BF16)
        dgp_ref[...] = dgp
        dup_ref[...] = dup
        act_ref[...] = (silu * up).astype(BF16)
        acc_ref[...] += _dot(dgp, wg_ref[...]) + _dot(dup, wu_ref[...])

        @pl.when(j == nh - 1)
        def _():
            x = x_ref[...]
            h_ref[...] = _rms(x, g_ref[...]).astype(BF16)
            dx, dgam = _rms_bwd_tail(x, g_ref[...], acc_ref[...])
            dx_ref[...] = d_ref[...] + dx
            dgam_ref[...] += dgam

    row_d = pl.BlockSpec((tm, D), lambda i, j: (i, 0))
    row_h = pl.BlockSpec((tm, th), lambda i, j: (i, j))
    return _call(
        body, name="ffn_bwd", grid=(t_rows // tm, nh),
        in_specs=[row_d, row_d, pl.BlockSpec((1, D), lambda i, j: (0, 0)), row_h, row_h,
                  pl.BlockSpec((th, D), lambda i, j: (j, 0)), pl.BlockSpec((th, D), lambda i, j: (j, 0)),
                  pl.BlockSpec((th, D), lambda i, j: (j, 0)), TOKEN_SPEC],
        out_specs=[row_d, pl.BlockSpec((1, D), lambda i, j: (0, 0)), row_h, row_h, row_h, row_d],
        out_shape=[jax.ShapeDtypeStruct((t_rows, D), F32), jax.ShapeDtypeStruct((1, D), F32),
                   jax.ShapeDtypeStruct((t_rows, HIDP), BF16), jax.ShapeDtypeStruct((t_rows, HIDP), BF16),
                   jax.ShapeDtypeStruct((t_rows, HIDP), BF16), jax.ShapeDtypeStruct((t_rows, D), BF16)],
        scratch_shapes=[pltpu.VMEM((tm, D), F32)],
        compiler_params=_cparams(("arbitrary", "arbitrary")))(dx2, x1, gamma, gpre, upre, wg, wu, wd, token)


def _inproj_bwd(dz, dx1, x, gamma, w, tm=1024, nb=2):
    t_rows = x.shape[0]
    tm = min(tm, t_rows)
    oc = w.shape[2]
    tn = nb * oc
    nn = NDEV // nb

    def body(dz_ref, d1_ref, x_ref, g_ref, w_ref, dx_ref, dgam_ref, h_ref, acc_ref):
        i = pl.program_id(0)
        j = pl.program_id(1)

        @pl.when(j == 0)
        def _():
            acc_ref[...] = jnp.zeros_like(acc_ref)

        @pl.when((i == 0) & (j == 0))
        def _():
            dgam_ref[...] = jnp.zeros_like(dgam_ref)

        for q in range(nb):
            acc_ref[...] += _dot_nt(dz_ref[:, oc * q:oc * (q + 1)], w_ref[q])

        @pl.when(j == nn - 1)
        def _():
            x = x_ref[...]
            h_ref[...] = _rms(x, g_ref[...]).astype(BF16)
            dx, dgam = _rms_bwd_tail(x, g_ref[...], acc_ref[...])
            dx_ref[...] = d1_ref[...] + dx
            dgam_ref[...] += dgam

    row_d = pl.BlockSpec((tm, D), lambda i, j: (i, 0))
    return _call(
        body, name="inproj_bwd", grid=(t_rows // tm, nn),
        in_specs=[pl.BlockSpec((tm, tn), lambda i, j: (i, j)), row_d, row_d, pl.BlockSpec((1, D), lambda i, j: (0, 0)),
                  pl.BlockSpec((nb, D, oc), lambda i, j: (j, 0, 0))],
        out_specs=[row_d, pl.BlockSpec((1, D), lambda i, j: (0, 0)), row_d],
        out_shape=[jax.ShapeDtypeStruct((t_rows, D), F32), jax.ShapeDtypeStruct((1, D), F32),
                   jax.ShapeDtypeStruct((t_rows, D), BF16)],
        scratch_shapes=[pltpu.VMEM((tm, D), F32)],
        compiler_params=_cparams(("arbitrary", "arbitrary")))(dz, dx1, x, gamma, w)


def _matmul_tn(a, b, name, owner_cols=None, tt=2048):
    t_rows, k = a.shape
    n = b.shape[1]
    tk = min(k, 1024)
    tt = min(tt, t_rows)
    nt = t_rows // tt
    if owner_cols is None:
        tn, nb = min(n, 1024), None
        out_spec = pl.BlockSpec((tk, tn), lambda i, j, t: (i, j))
        out_shape = jax.ShapeDtypeStruct((k, n), BF16)
    else:
        nb = min(n // owner_cols, max(1, 1280 // owner_cols))
        tn = nb * owner_cols
        out_spec = pl.BlockSpec((nb, tk, owner_cols), lambda i, j, t: (j, i, 0))
        out_shape = jax.ShapeDtypeStruct((n // owner_cols, k, owner_cols), BF16)

    def body(a_ref, b_ref, o_ref, acc_ref):
        t = pl.program_id(2)

        @pl.when(t == 0)
        def _():
            acc_ref[...] = jnp.zeros_like(acc_ref)

        acc_ref[...] += _dot_tn(a_ref[...], b_ref[...])

        @pl.when(t == nt - 1)
        def _():
            if nb is None:
                o_ref[...] = acc_ref[...].astype(BF16)
            else:
                for q in range(nb):
                    o_ref[q] = acc_ref[:, owner_cols * q:owner_cols * (q + 1)].astype(BF16)

    return _call(
        body, name=name, grid=(k // tk, n // tn, nt),
        in_specs=[pl.BlockSpec((tt, tk), lambda i, j, t: (t, i)), pl.BlockSpec((tt, tn), lambda i, j, t: (t, j))],
        out_specs=out_spec, out_shape=out_shape,
        scratch_shapes=[pltpu.VMEM((tk, tn), F32)],
        compiler_params=_cparams(("parallel", "parallel", "arbitrary")))(a, b)


def _group_tn(a, b):
    t_rows = a.shape[0]
    gw = BW // 4

    def body(a_ref, b_ref, o_ref):
        o_ref[...] = _dot_tn(a_ref[...], b_ref[...])

    return _call(
        body, name="pool_group_tn", grid=(4,),
        in_specs=[pl.BlockSpec((t_rows, gw), lambda k: (0, k)), pl.BlockSpec((t_rows, gw), lambda k: (0, k))],
        out_specs=pl.BlockSpec((None, gw, gw), lambda k: (k, 0, 0)),
        out_shape=jax.ShapeDtypeStruct((4, gw, gw), F32),
        compiler_params=_cparams(("parallel",)))(a, b)


def _loss_head(x2, gamma, target, tm=512):
    t_rows = x2.shape[0]

    def body(x_ref, g_ref, t_ref, loss_ref, dx_ref, dgam_ref):
        @pl.when(pl.program_id(0) == 0)
        def _():
            loss_ref[...] = jnp.zeros_like(loss_ref)
            dgam_ref[...] = jnp.zeros_like(dgam_ref)

        def f(x, g):
            err = jnp.square(_rms(x, g) - t_ref[...])
            return 0.5 * jnp.sum(jnp.mean(err, axis=-1, keepdims=True), axis=0, keepdims=True)

        loss, vjp = jax.vjp(f, x_ref[...], g_ref[...])
        dx, dgam = vjp(jnp.ones((1, 1), F32))
        loss_ref[...] += jnp.broadcast_to(loss, (1, 128))
        dx_ref[...] = dx
        dgam_ref[...] += dgam

    row_d = pl.BlockSpec((tm, D), lambda i: (i, 0))
    return _call(
        body, name="loss_head", grid=(t_rows // tm,),
        in_specs=[row_d, pl.BlockSpec((1, D), lambda i: (0, 0)), row_d],
        out_specs=[pl.BlockSpec((1, 128), lambda i: (0, 0)), row_d, pl.BlockSpec((1, D), lambda i: (0, 0))],
        out_shape=[jax.ShapeDtypeStruct((1, 128), F32), jax.ShapeDtypeStruct((t_rows, D), F32),
                   jax.ShapeDtypeStruct((1, D), F32)],
        compiler_params=_cparams(("arbitrary",)))(x2, gamma, target)


NCHIP = NDEV // 2


def _coords():
    return lax.axis_index("x"), lax.axis_index("y"), lax.axis_index("c")


def _remote(src, dst, send_sem, recv_sem, peer):
    return pltpu.make_async_remote_copy(src_ref=src, dst_ref=dst, send_sem=send_sem, recv_sem=recv_sem,
                                        device_id=peer, device_id_type=MESH)


def _comm_call(name, srcs, out_shapes, n_rec, plan, aliases=None):
    ns, no = len(srcs), len(out_shapes)

    def body(*refs):
        ins, outs = refs[:ns], refs[ns:ns + no]
        loc_sem, send_sem, recv_sem = refs[ns + no:]
        x, y, c = _coords()
        recs = plan(ins, outs, x, y, c)
        assert len(recs) == n_rec
        for k, r in enumerate(recs):
            for src, dst in r.get("local", ()):
                pltpu.make_async_copy(src, dst, loc_sem.at[k]).start()
            for peer, src, dst in r.get("remote", ()):
                _remote(src, dst, send_sem.at[k], recv_sem.at[k], peer).start()
        for k, r in enumerate(recs):
            if r.get("recv_wait") is not None:
                w = r["recv_wait"]
                _remote(w, w, send_sem.at[k], recv_sem.at[k], (x, y, c)).wait_recv()
            if r.get("send_wait") is not None:
                w = r["send_wait"]
                _remote(w, w, send_sem.at[k], recv_sem.at[k], (x, y, c)).wait_send()
            if r.get("local_wait") is not None:
                w = r["local_wait"]
                pltpu.make_async_copy(w, w, loc_sem.at[k]).wait()

    return _call(
        body, name=name, in_specs=[ANY] * ns, out_specs=[ANY] * no, out_shape=out_shapes,
        input_output_aliases=aliases or {}, scratch_shapes=[pltpu.SemaphoreType.DMA((n_rec,))] * 3)(*srcs)


def _gather_call(srcs, out_shapes, items, aliases):
    ns, no, n = len(srcs), len(out_shapes), len(items)

    def body(*refs):
        ins, outs = refs[:ns], refs[ns:ns + no]
        loc, sib_s, sib_r, ici_s, ici_r, fwd_s, fwd_r = refs[ns + no:]
        x, y, c = _coords()
        me, sib = (x, y, c), (x, y, 1 - c)
        chips = [(1 - x, y), (x, 1 - y), (1 - x, 1 - y)]
        index = lambda px, py, pc: 4 * px + 2 * py + pc
        for k, (si, oi, shard, block, _) in enumerate(items):
            mine = block(outs[oi], index(*me))
            src = mine if shard is None else shard(ins[si])
            if shard is not None:
                pltpu.make_async_copy(src, mine, loc.at[k]).start()
            _remote(src, mine, sib_s.at[k], sib_r.at[k], sib).start()
            for chip in chips:
                _remote(src, mine, ici_s.at[k], ici_r.at[k], (*chip, c)).start()
        for k, (si, oi, _, block, blocks) in enumerate(items):
            three = blocks(outs[oi], 3)
            _remote(three, three, ici_s.at[k], ici_r.at[k], me).wait_recv()
            for chip in chips:
                landed = block(outs[oi], index(*chip, c))
                _remote(landed, landed, fwd_s.at[k], fwd_r.at[k], sib).start()
        for k, (si, oi, shard, _, blocks) in enumerate(items):
            one, three = blocks(outs[oi], 1), blocks(outs[oi], 3)
            _remote(one, one, sib_s.at[k], sib_r.at[k], me).wait()
            _remote(three, three, fwd_s.at[k], fwd_r.at[k], me).wait()
            _remote(three, three, ici_s.at[k], ici_r.at[k], me).wait_send()
            if shard is not None:
                pltpu.make_async_copy(one, one, loc.at[k]).wait()

    return _call(
        body, name="gather_weights", in_specs=[ANY] * ns, out_specs=[ANY] * no, out_shape=out_shapes,
        input_output_aliases=aliases, scratch_shapes=[pltpu.SemaphoreType.DMA((n,))] * 7)(*srcs)


_BIG = {
    "w_in": (True, D, IN_W // NDEV),
    "ssm_w_glu": (False, BW // NDEV, BW),
    "ssm_w_proj": (True, BW, D // NDEV),
    "conv_w_proj": (True, BW, D // NDEV),
    "pool_w_proj": (True, BW, D // NDEV),
    "w_out": (False, D // NDEV, D),
    "ffn_w_gate": (False, HPAD, D),
    "ffn_w_up": (False, HPAD, D),
    "ffn_w_down": (False, HPAD, D),
}
GROUP_IN = ("w_in",)
GROUP_MIX = ("ssm_w_glu", "ssm_w_proj", "conv_w_proj", "pool_w_proj", "w_out")
GROUP_FFN = ("ffn_w_gate", "ffn_w_up", "ffn_w_down")


def _gathered_shape(name):
    blocked, kk, nn = _BIG[name]
    return jax.ShapeDtypeStruct((NDEV, kk, nn) if blocked else (NDEV * kk, nn), BF16)


def _block_view(name):
    blocked, kk, _ = _BIG[name]
    if blocked:
        return lambda ref, q: ref.at[q]
    return lambda ref, q: ref.at[pl.ds(pl.multiple_of(q * kk, 16), kk), :]


def _blocks_view(name):
    blocked, kk, _ = _BIG[name]
    if blocked:
        return lambda ref, n: ref.at[pl.ds(0, n)]
    return lambda ref, n: ref.at[pl.ds(0, n * kk), :]


def _place_shards(weights, names, layer, me):
    cnt = len(names)

    def body(me_ref, *refs):
        w_refs, outs, stages, sem = refs[:cnt], refs[cnt:2 * cnt], refs[2 * cnt:3 * cnt], refs[3 * cnt]
        q = me_ref[0]
        copies = []
        for k, name in enumerate(names):
            _, kk, nn = _BIG[name]
            rows = w_refs[k].shape[0]
            stages[k][0:rows, :] = w_refs[k][...].astype(BF16)
            if rows < kk:
                stages[k][rows:kk, :] = jnp.zeros((kk - rows, nn), BF16)
            copies.append(pltpu.make_async_copy(stages[k], _block_view(name)(outs[k], q), sem.at[k]))
            copies[-1].start()
        for cp in copies:
            cp.wait()

    in_specs = [pl.BlockSpec(memory_space=pltpu.SMEM)]
    in_specs += [pl.BlockSpec((None,) + weights[n].shape[1:], lambda i: (layer, 0, 0)) for n in names]
    res = _call(
        body, name="place_shards", grid=(1,), in_specs=in_specs, out_specs=[ANY] * cnt,
        out_shape=[_gathered_shape(n) for n in names],
        scratch_shapes=[pltpu.VMEM(_BIG[n][1:], BF16) for n in names] + [pltpu.SemaphoreType.DMA((cnt,))],
        compiler_params=_cparams(("arbitrary",)))(me, *[weights[n] for n in names])
    return dict(zip(names, res))


def _gather_weights(placed, conv_dw, names):
    cnt = len(names)
    srcs = [placed[n] for n in names] + [conv_dw]
    outs = [_gathered_shape(n) for n in names] + [jax.ShapeDtypeStruct((NDEV,) + conv_dw.shape, conv_dw.dtype)]
    items = [(k, k, None, _block_view(n), _blocks_view(n)) for k, n in enumerate(names)]
    items.append((cnt, cnt, lambda ref: ref, lambda ref, q: ref.at[q], lambda ref, n: ref.at[pl.ds(0, n)]))
    res = _gather_call(srcs, outs, items, {k: k for k in range(cnt)})
    return dict(zip(names, res[:-1])), res[-1]


def _gather_start(placed, names, after, tag):
    def copies(src_refs, land_refs, x, y, c):
        me = 4 * x + 2 * y + c
        peers = [(x, y, 1 - c), (1 - x, y, c), (x, 1 - y, c), (1 - x, 1 - y, c)]
        out = []
        for k, n in enumerate(names):
            mine = _block_view(n)(land_refs[k], me)
            out.append([(peer, mine, mine) for peer in peers])
        return out

    return _split_start("gather_start_" + tag, [], [placed[n] for n in names], copies, after)


def _gather_finish(handle, names, after, tag):
    four = [functools.partial(lambda ref, bv: bv(ref, 4), bv=_blocks_view(n)) for n in names]
    lands = _split_wait("gather_wait_" + tag, handle, four, after)
    cnt = len(names)

    def plan(ins, out_refs, x, y, c):
        sib = (x, y, 1 - c)
        recs = []
        for k, n in enumerate(names):
            three = _blocks_view(n)(out_refs[k], 3)
            remote = []
            for px, py in [(1 - x, y), (x, 1 - y), (1 - x, 1 - y)]:
                landed = _block_view(n)(out_refs[k], 4 * px + 2 * py + c)
                remote.append((sib, landed, landed))
            recs.append(dict(remote=remote, send_wait=three, recv_wait=three))
        return recs

    res = _comm_call("gather_pair_" + tag, lands, [jax.ShapeDtypeStruct(l.shape, l.dtype) for l in lands], cnt, plan,
                     aliases={k: k for k in range(cnt)})
    return dict(zip(names, res))


def _pair_add(name, grads, rcv, core):
    nl = len(grads)
    _, kk, nn = grads[0].shape

    def body(c_ref, *refs):
        l = pl.program_id(0)
        own = refs[0][...]
        for j in range(1, nl):
            own = jnp.where(l == j, refs[j][...], own)
        refs[nl + 1][...] = (own.astype(F32) + refs[nl][...].astype(F32)).astype(BF16)

    gspec = lambda j: pl.BlockSpec((None, kk, nn), lambda l, h, c_ref: (jnp.where(l == j, 2 * h + c_ref[0], 0), 0, 0))
    rspec = pl.BlockSpec((None, None, kk, nn), lambda l, h, c_ref: (h, l, 0, 0))
    return _call(
        body, name="pair_add_" + name,
        grid_spec=pltpu.PrefetchScalarGridSpec(num_scalar_prefetch=1, grid=(nl, NCHIP),
                                               in_specs=[gspec(j) for j in range(nl)] + [rspec], out_specs=rspec),
        out_shape=jax.ShapeDtypeStruct(rcv.shape, BF16),
        compiler_params=_cparams(("arbitrary", "arbitrary")))(core, *grads, rcv)


def _pair_add_small(owned, lists, core):
    on, ln = list(owned), list(lists)
    flat = []
    for n in on:
        flat += list(owned[n][0]) + [owned[n][1]]
    for n in ln:
        flat += list(lists[n][0]) + [lists[n][1]]

    def body(c_ref, *refs):
        outs = refs[len(flat):]
        c = c_ref[0]
        pos = 0
        for k, n in enumerate(on):
            nl = len(owned[n][0])
            for h in range(NCHIP):
                for l in range(nl):
                    outs[k][h, l] = refs[pos + l][pl.ds(2 * h + c, 1)][0] + refs[pos + nl][h, l]
            pos += nl + 1
        for k, n in enumerate(ln):
            nl = len(lists[n][0])
            for l in range(nl):
                outs[len(on) + k][l] = refs[pos + l][...] + refs[pos + nl][l]
            pos += nl + 1

    shapes = [jax.ShapeDtypeStruct(owned[n][1].shape, F32) for n in on]
    shapes += [jax.ShapeDtypeStruct(lists[n][1].shape, F32) for n in ln]
    res = _call(body, name="pair_add_small", out_shape=shapes,
                in_specs=[pl.BlockSpec(memory_space=pltpu.SMEM)] + [pl.BlockSpec(memory_space=pltpu.VMEM)] * len(flat),
                compiler_params=_cparams())(core, *flat)
    return dict(zip(on + ln, res))


def _pair_reduce(tag, big, by_owner, small, core):
    rs = {**big, **by_owner}
    srcs, outs, plans, rcv_at = [], [], [], {}
    for name, arrays in rs.items():
        rcv_at[name] = len(outs)
        outs.append(jax.ShapeDtypeStruct((NCHIP, len(arrays)) + arrays[0].shape[1:], arrays[0].dtype))
        for l, arr in enumerate(arrays):
            srcs.append(arr)
            plans.append((len(srcs) - 1, rcv_at[name], l, True))
    for name, arrays in small.items():
        rcv_at[name] = len(outs)
        outs.append(jax.ShapeDtypeStruct((len(arrays),) + arrays[0].shape, F32))
        for l, arr in enumerate(arrays):
            srcs.append(arr)
            plans.append((len(srcs) - 1, rcv_at[name], l, False))

    def plan_pair(ins, out_refs, x, y, c):
        sib = (x, y, 1 - c)
        recs = []
        for si, ro, l, slabs in plans:
            if slabs:
                four = out_refs[ro].at[pl.ds(0, NCHIP), l]
                recs.append(dict(remote=[(sib, ins[si].at[2 * h + 1 - c], out_refs[ro].at[h, l]) for h in range(NCHIP)],
                                 send_wait=four, recv_wait=four))
            else:
                dst = out_refs[ro].at[l]
                recs.append(dict(remote=[(sib, ins[si], dst)], send_wait=dst, recv_wait=dst))
        return recs

    res = _comm_call("pair_exchange_" + tag, srcs, outs, len(plans), plan_pair)
    part = {name: _pair_add(name, big[name], res[rcv_at[name]], core) for name in big}
    if by_owner or small:
        part.update(_pair_add_small({n: (by_owner[n], res[rcv_at[n]]) for n in by_owner},
                                    {n: (small[n], res[rcv_at[n]]) for n in small}, core))
    return part


def _chip_copies(src, land, slabbed, x, y, c):
    mine = 2 * x + y
    copies = []
    for step in range(1, NCHIP):
        h = (mine + step) % NCHIP
        copies.append(((h // 2, h % 2, c), src.at[h] if slabbed else src, land.at[mine]))
    return copies


def _chip_exchange(part, slabbed, keep_own):
    names = list(part)
    outs = [jax.ShapeDtypeStruct((() if n in slabbed else (NCHIP,)) + part[n].shape, part[n].dtype) for n in names]

    def plan(ins, out_refs, x, y, c):
        mine = 2 * x + y
        recs = []
        for k, n in enumerate(names):
            three = out_refs[k].at[pl.ds(0, NCHIP - 1)]
            rec = dict(remote=_chip_copies(ins[k], out_refs[k], n in slabbed, x, y, c), send_wait=three, recv_wait=three)
            if n in keep_own:
                rec["local"] = [(ins[k].at[mine] if n in slabbed else ins[k], out_refs[k].at[mine])]
                rec["local_wait"] = out_refs[k].at[0]
            recs.append(rec)
        return recs

    res = _comm_call("chip_exchange", [part[n] for n in names], outs, len(names), plan)
    return dict(zip(names, res))


HBM_SPEC = pl.BlockSpec(memory_space=pltpu.HBM)
SEM_SPEC = pl.BlockSpec(memory_space=pltpu.SEMAPHORE)
SPLIT_EFFECT = pltpu.SideEffectType.DATAFLOW_SIDE_EFFECTING


def _split_start(name, srcs, land_shapes, copies_fn, after):
    ns, n = len(srcs), len(land_shapes)
    lands = [pltpu.with_memory_space_constraint(s if isinstance(s, jax.Array) else lax.empty(s.shape, s.dtype), pltpu.HBM)
             for s in land_shapes]

    def body(*refs):
        src_refs, land_refs = refs[:ns], refs[ns:ns + n]
        send_sem, recv_sem = refs[ns + n + 1], refs[ns + n + 2]
        token = refs[-1]
        x, y, c = _coords()
        for k, copies in enumerate(copies_fn(src_refs, land_refs, x, y, c)):
            for peer, src, dst in copies:
                _remote(src, dst, send_sem.at[k], recv_sem.at[k], peer).start()
        token[...] = jnp.zeros_like(token)

    res = pl.pallas_call(
        body, name=name,
        out_shape=(pltpu.SemaphoreType.DMA((n,)), pltpu.SemaphoreType.DMA((n,)),
                   *[pltpu.HBM(s.shape, s.dtype) for s in land_shapes], jax.ShapeDtypeStruct((8, 128), F32)),
        in_specs=[HBM_SPEC] * (ns + n) + [ANY],
        out_specs=(SEM_SPEC, SEM_SPEC, *[HBM_SPEC] * n, pl.BlockSpec(memory_space=pltpu.VMEM)),
        input_output_aliases={ns + i: 2 + i for i in range(n)},
        compiler_params=pltpu.CompilerParams(has_side_effects=SPLIT_EFFECT),
    )(*[pltpu.with_memory_space_constraint(s, pltpu.HBM) for s in srcs], *lands, after)
    return dict(send=res[0], recv=res[1], srcs=list(srcs), lands=list(res[2:2 + n]), token=res[-1])


def _split_wait(name, handle, wait_views, after):
    n = len(handle["lands"])
    after = after if isinstance(after, (tuple, list)) else (after,)

    def body(*refs):
        land_refs = refs[:n]
        send_sem, recv_sem = refs[n], refs[n + 1]
        x, y, c = _coords()
        for k in range(n):
            w = wait_views[k](land_refs[k])
            cp = _remote(w, w, send_sem.at[k], recv_sem.at[k], (x, y, c))
            cp.wait_send()
            cp.wait_recv()

    res = pl.pallas_call(
        body, name=name,
        out_shape=tuple(pltpu.HBM(s.shape, s.dtype) for s in handle["lands"]),
        in_specs=[HBM_SPEC] * n + [SEM_SPEC, SEM_SPEC] + [ANY] * len(after), out_specs=tuple([HBM_SPEC] * n),
        input_output_aliases={i: i for i in range(n)},
        compiler_params=pltpu.CompilerParams(has_side_effects=SPLIT_EFFECT),
    )(*handle["lands"], handle["send"], handle["recv"], *after)
    return list(res)


def _adamw(w, g, m, v):
    m = ADAM_B1 * m + (1.0 - ADAM_B1) * g
    v = ADAM_B2 * v + (1.0 - ADAM_B2) * jnp.square(g)
    m_hat = m / (1.0 - ADAM_B1 ** ADAM_STEP)
    v_hat = v / (1.0 - ADAM_B2 ** ADAM_STEP)
    delta = -ADAM_LR * (m_hat / (jnp.sqrt(v_hat) + ADAM_EPS) + ADAM_WD * w)
    return delta, m, v


def _chip_start(part, tag):
    names = list(part)

    def copies(src_refs, land_refs, x, y, c):
        return [_chip_copies(src_refs[k], land_refs[k], True, x, y, c) for k in range(len(names))]

    shapes = [jax.ShapeDtypeStruct(part[n].shape, part[n].dtype) for n in names]
    return names, _split_start("chip_start_" + tag, [part[n] for n in names], shapes, copies, part[names[0]])


def _chip_wait(names, handle, after, tag):
    three = [lambda ref: ref.at[pl.ds(0, NCHIP - 1)]] * len(names)
    return dict(zip(names, _split_wait("chip_wait_" + tag, handle, three, after)))


def _sum_senders(ref):
    g = ref[0].astype(F32)
    for h in range(1, NCHIP):
        g = g + ref[h].astype(F32)
    return g


def _adam_big(name, own, recv, w, m, v, tk, chip):
    nl = len(own)
    kk, nn = w.shape[1], w.shape[2]
    nnp = own[0].shape[3]

    def body(chip_ref, *refs):
        l = pl.program_id(0)
        g = None
        for step in range(NCHIP):
            val = refs[step][...]
            for q in range(1, nl):
                val = jnp.where(l == q, refs[NCHIP * q + step][...], val)
            g = val.astype(F32) if g is None else g + val.astype(F32)
        w_ref, m_ref, v_ref, g_ref, d_ref, mo_ref, vo_ref = refs[NCHIP * nl:]
        g = g[:, :nn]
        delta, m2, v2 = _adamw(w_ref[...], g, m_ref[...], v_ref[...])
        g_ref[...] = g
        d_ref[...] = delta
        mo_ref[...] = m2
        vo_ref[...] = v2

    def slab(q, step):
        return pl.BlockSpec((None, None, tk, nnp), lambda l, i, chip_ref: (
            jnp.where(l == q, (chip_ref[0] + step) % NCHIP, 0), 0, jnp.where(l == q, i, 0), 0))

    in_specs, operands = [], []
    for q in range(nl):
        in_specs += [slab(q, step) for step in range(NCHIP)]
        operands += [own[q]] + [recv[q]] * (NCHIP - 1)
    wspec = pl.BlockSpec((None, tk, nn), lambda l, i, chip_ref: (l, i, 0))
    shape = jax.ShapeDtypeStruct(w.shape, F32)
    return _call(
        body, name="adamw_" + name,
        grid_spec=pltpu.PrefetchScalarGridSpec(num_scalar_prefetch=1, grid=(nl, kk // tk),
                                               in_specs=in_specs + [wspec] * 3, out_specs=[wspec] * 4),
        out_shape=[shape] * 4, compiler_params=_cparams(("arbitrary", "arbitrary")))(chip, *operands, w, m, v)


def _adam_small(names, recv, w, m, v):
    n = len(names)

    def body(*refs):
        r, ww, mm, vv = refs[:n], refs[n:2 * n], refs[2 * n:3 * n], refs[3 * n:4 * n]
        outs = refs[4 * n:]
        for k in range(n):
            g = _sum_senders(r[k])
            if g.shape != ww[k].shape:
                g = g[:, :ww[k].shape[1]]
            delta, m2, v2 = _adamw(ww[k][...], g, mm[k][...], vv[k][...])
            outs[k][...] = g
            outs[n + k][...] = delta
            outs[2 * n + k][...] = m2
            outs[3 * n + k][...] = v2

    shapes = [jax.ShapeDtypeStruct(w[k].shape, F32) for k in names]
    res = _call(body, name="adamw_small", out_shape=shapes * 4, compiler_params=_cparams())(
        *[recv[k] for k in names], *[w[k] for k in names], *[m[k] for k in names], *[v[k] for k in names])
    return {k: (res[i], res[n + i], res[2 * n + i], res[3 * n + i]) for i, k in enumerate(names)}


def _expand_b(bt):
    eye = jnp.eye(GB, dtype=bt.dtype)
    return jnp.einsum("jgpn,gh->jgphn", bt.reshape(NBLK, GB, SGRP, NSTATE), eye).reshape(NBLK, GB * SGRP, NS)


def _extract_b(db):
    x = db.reshape(NBLK, GB, SGRP, GB, NSTATE)
    eye = jnp.eye(GB, dtype=db.dtype)
    return jnp.einsum("jgphn,gh->jgpn", x, eye).reshape(NGRP, SGRP, NSTATE)


def _expand_c(c):
    ct = jnp.transpose(c, (0, 2, 1)).reshape(NBLK, GB, NSTATE, SGRP)
    eye = jnp.eye(GB, dtype=c.dtype)
    return jnp.einsum("jgnp,gh->jgnhp", ct, eye).reshape(NBLK, NS, GB * SGRP)


def _extract_c(dc):
    x = dc.reshape(NBLK, GB, NSTATE, GB, SGRP)
    eye = jnp.eye(GB, dtype=dc.dtype)
    d = jnp.einsum("jgnhp,gh->jgnp", x, eye).reshape(NGRP, NSTATE, SGRP)
    return jnp.transpose(d, (0, 2, 1))


_SMALL = ("norm1", "b_gate", "ssm_a_re", "ssm_a_im", "ssm_log_dt", "ssm_b_re", "ssm_b_im", "ssm_c_re", "ssm_c_im",
          "ssm_d", "ssm_b_glu", "conv_b_dw", "conv_ln_g", "conv_ln_b", "pool_w_group", "pool_scale", "norm2")
_ADAM_TK = {"w_in": 256, "ssm_w_glu": 64, "ssm_w_proj": 512, "conv_w_proj": 512, "pool_w_proj": 512, "w_out": 128,
            "ffn_w_gate": HSH, "ffn_w_up": HSH, "ffn_w_down": HSH}
_OUT_ORDER = ("norm1", "w_in", "b_gate", "ssm_a_re", "ssm_a_im", "ssm_log_dt", "ssm_b_re", "ssm_b_im", "ssm_c_re",
              "ssm_c_im", "ssm_d", "ssm_w_glu", "ssm_b_glu", "ssm_w_proj", "conv_w_dw", "conv_b_dw", "conv_ln_g",
              "conv_ln_b", "conv_w_proj", "pool_w_group", "pool_scale", "pool_w_proj", "w_out", "norm2", "ffn_w_gate",
              "ffn_w_up", "ffn_w_down", "final_norm")


def _layer_fwd(x, p, token, late_params=None):
    z = _inproj_fwd(x, p["norm1"], p["w_in"], token)
    yssm, hre, him = _ssm_fwd(z, p)
    cv = _conv_fwd(z, p["conv_w"], p["conv_b"])
    pbar = _pool_fwd(z)
    if late_params is not None:
        more, token = late_params((yssm, cv, pbar))
        p = {**p, **more}
    x1 = _merge_fwd(x, yssm, cv, pbar, z, p, token)
    x2, gpre, upre = _ffn_fwd(x1, p["norm2"], p["wg"], p["wu"], p["wd"])
    return x2, dict(x=x, z=z, yssm=yssm, hre=hre, him=him, cv=cv, pbar=pbar, x1=x1, gpre=gpre, upre=upre), p


def _layer_bwd(dx, p, s, token, after_ffn=None):
    big, small = {}, {}
    dx1, d_norm2, dgp, dup, act, h2 = _ffn_bwd(dx, s["x1"], p["norm2"], s["gpre"], s["upre"], p["wg"], p["wu"], p["wd"],
                                               token)
    big["ffn_w_gate"] = _matmul_tn(dgp, h2, "tn_gate").reshape(NDEV, HPAD, D)
    big["ffn_w_up"] = _matmul_tn(dup, h2, "tn_up").reshape(NDEV, HPAD, D)
    big["ffn_w_down"] = _matmul_tn(act, dx, "tn_down").reshape(NDEV, HPAD, D)
    (dy, dcv, dpb, dzg, a_g, a_outa, a_hs, a_pb, a_pc, a_mg, c_glu, c_ya, c_yb, c_p, c_yc,
     d_bglu, d_lng, d_lnb, d_scale, d_bgate) = _merge_bwd(dx1, s["x"], s["yssm"], s["cv"], s["pbar"], s["z"], p,
                                                          after_ffn(dict(big)) if after_ffn else token)
    big["ssm_w_glu"] = _matmul_tn(a_g, c_glu, "tn_glu").reshape(NDEV, BW // NDEV, BW)
    big["ssm_w_proj"] = _matmul_tn(a_outa, c_ya, "tn_ssm_proj", D // NDEV)
    big["conv_w_proj"] = _matmul_tn(a_hs, c_yb, "tn_conv_proj", D // NDEV)
    big["pool_w_proj"] = _matmul_tn(a_pc, c_yc, "tn_pool_proj", D // NDEV)
    big["w_out"] = _matmul_tn(a_mg, dx1, "tn_out").reshape(NDEV, D // NDEV, D)
    d_wgrp = _group_tn(a_pb, c_p)
    du_a, dbr, dbi, dcr, dci, dd, dar, dai, dldt = _ssm_bwd(dy, s["z"], s["hre"], s["him"], p)
    dva, dvb, dw8, dcb = _conv_bwd(dcv, s["z"], p["conv_w"])
    du_c = _pool_bwd(dpb)
    dz = jnp.concatenate([du_a, dva, dvb, du_c, dzg], axis=1)
    dx0, d_norm1, h = _inproj_bwd(dz, dx1, s["x"], p["norm1"], p["w_in"])
    big["w_in"] = _matmul_tn(h, dz, "tn_in", IN_W // NDEV)
    small["norm1"] = d_norm1
    small["b_gate"] = d_bgate
    small["ssm_a_re"] = dar.reshape(NGRP, NSTATE)
    small["ssm_a_im"] = dai.reshape(NGRP, NSTATE)
    small["ssm_log_dt"] = dldt.reshape(NBLK, 8, 128)[:, 0, :GB].reshape(1, NGRP)
    small["ssm_b_re"] = _extract_b(dbr)
    small["ssm_b_im"] = _extract_b(dbi)
    small["ssm_c_re"] = _extract_c(dcr)
    small["ssm_c_im"] = _extract_c(dci)
    small["ssm_d"] = dd.reshape(NGRP, SGRP)
    small["ssm_b_glu"] = d_bglu
    small["conv_b_dw"] = dcb
    small["conv_ln_g"] = d_lng
    small["conv_ln_b"] = d_lnb
    small["pool_w_group"] = d_wgrp
    small["pool_scale"] = d_scale
    small["norm2"] = d_norm2
    return dx0, big, dw8, small


def _train_step(a):
    t_rows = a["x"].shape[1]
    x0 = a["x"].reshape(t_rows, D)
    target = a["loss_target"].reshape(t_rows, D)

    tr = lambda w: jnp.transpose(w, (0, 2, 1))
    weights = {name: (tr(a[name]) if name in ("ffn_w_gate", "ffn_w_up") else a[name]) for name in _BIG}
    core = lax.axis_index("c").astype(jnp.int32).reshape(1)
    chip = (2 * lax.axis_index("x") + lax.axis_index("y")).astype(jnp.int32).reshape(1)
    me = 2 * chip + core
    no_token = jnp.zeros((8, 128), F32)
    row = lambda v: v.reshape(1, -1)
    rest = GROUP_MIX + GROUP_FFN
    first, dw_all = _gather_weights(_place_shards(weights, GROUP_IN, 0, me),
                                    a["conv_w_dw"].reshape(DEPTH, CONV_K, BW // NDEV), GROUP_IN)
    conv_w = jnp.transpose(dw_all, (1, 2, 0, 3)).reshape(DEPTH, CONV_K, BW)
    go_rest0 = _gather_start(_place_shards(weights, rest, 0, me), rest, dw_all, "rest0")
    going = {}

    def early_params(l, w_in):
        return dict(
            norm1=row(a["norm1"][l]), w_in=w_in,
            are=row(a["ssm_a_re"][l]), aim=row(a["ssm_a_im"][l]),
            ldt=row(jnp.repeat(a["ssm_log_dt"][l], NSTATE)),
            bexp_re=_expand_b(jnp.transpose(a["ssm_b_re"][l], (0, 2, 1))),
            bexp_im=_expand_b(jnp.transpose(a["ssm_b_im"][l], (0, 2, 1))),
            cexp_re=_expand_c(a["ssm_c_re"][l]), cexp_im=_expand_c(a["ssm_c_im"][l]),
            dskip=row(a["ssm_d"][l]), conv_w=conv_w[l], conv_b=row(a["conv_b_dw"][l]))

    def late_params(l, handle, tag, then_start):
        def get(after):
            full = _gather_finish(handle, rest, after, tag)
            token = then_start(full["ffn_w_down"]) if then_start else no_token
            return dict(
                wglu=full["ssm_w_glu"], bglu=row(a["ssm_b_glu"][l]), wpa=full["ssm_w_proj"],
                lng=row(a["conv_ln_g"][l]), lnb=row(a["conv_ln_b"][l]), wpb=full["conv_w_proj"],
                wgrp=a["pool_w_group"][l].astype(BF16), scale=row(a["pool_scale"][l]), wpc=full["pool_w_proj"],
                bgate=row(a["b_gate"][l]), wout=full["w_out"],
                norm2=row(a["norm2"][l]), wg=full["ffn_w_gate"], wu=full["ffn_w_up"], wd=full["ffn_w_down"]), token
        return get

    def start_in1(after):
        going["in1"] = _gather_start(_place_shards(weights, GROUP_IN, 1, me), GROUP_IN, after, "in1")
        return going["in1"]["token"]

    x, s0, p0 = _layer_fwd(x0, early_params(0, first["w_in"]), go_rest0["token"], late_params(0, go_rest0, "rest0", start_in1))
    w_in1 = _gather_finish(going["in1"], GROUP_IN, x, "in1")["w_in"]
    go_rest1 = _gather_start(_place_shards(weights, rest, 1, me), rest, w_in1, "rest1")
    x, s1, p1 = _layer_fwd(x, early_params(1, w_in1), go_rest1["token"], late_params(1, go_rest1, "rest1", None))
    params, saved = [p0, p1], [s0, s1]

    loss_part, dx, d_final = _loss_head(x, a["final_norm"].reshape(1, D), target)
    loss = lax.psum(loss_part[0, 0], ("x", "y", "c"))

    dx, gb1, go1, gs1 = _layer_bwd(dx, params[1], saved[1], no_token)
    part1 = _pair_reduce("late", {n: [g] for n, g in gb1.items()}, {}, {}, core)
    rs_names, rs_handle = _chip_start(part1, "late")
    early = {}

    def ffn_grads_leave_early(ffn):
        early["part"] = _pair_reduce("ffn", {n: [g] for n, g in ffn.items()}, {}, {}, core)
        early["names"], early["handle"] = _chip_start(early["part"], "ffn")
        return early["handle"]["token"]

    dx, gb0, go0, gs0 = _layer_bwd(dx, params[0], saved[0], rs_handle["token"], ffn_grads_leave_early)
    grad_x = dx.reshape(1, t_rows, D)
    small = {n: [gs0[n], gs1[n]] for n in _SMALL}
    small["final_norm"] = [d_final]
    part0 = _pair_reduce("rest", {n: [g] for n, g in gb0.items() if n not in early["part"]}, {"conv_w_dw": [go0, go1]},
                         small, core)
    recv = _chip_exchange(part0, set(_BIG) | {"conv_w_dw"}, {"conv_w_dw"} | set(small))
    recv1 = _chip_wait(rs_names, rs_handle, dx, "late")
    recv.update(_chip_wait(early["names"], early["handle"], dx, "ffn"))
    part0.update(early["part"])

    results = {}
    for name in _BIG:
        fix = tr if name in ("ffn_w_gate", "ffn_w_up") else (lambda t: t)
        res = _adam_big(name, [part0[name], part1[name]], [recv[name], recv1[name]], fix(a[name]), fix(a["m_" + name]),
                        fix(a["v_" + name]), _ADAM_TK[name], chip)
        results[name] = tuple(fix(r) for r in res)

    lay = {
        "norm1": lambda v: v.reshape(DEPTH, 1, D), "b_gate": lambda v: v.reshape(DEPTH, 1, 3 * D),
        "ssm_log_dt": lambda v: v.reshape(DEPTH, 1, NGRP),
        "ssm_b_re": lambda v: jnp.transpose(v, (0, 1, 3, 2)), "ssm_b_im": lambda v: jnp.transpose(v, (0, 1, 3, 2)),
        "ssm_b_glu": lambda v: v.reshape(DEPTH, 1, BW), "conv_b_dw": lambda v: v.reshape(DEPTH, 1, BW),
        "conv_ln_g": lambda v: v.reshape(DEPTH, 1, BW), "conv_ln_b": lambda v: v.reshape(DEPTH, 1, BW),
        "pool_scale": lambda v: v.reshape(DEPTH, 1, BW), "norm2": lambda v: v.reshape(DEPTH, 1, D),
        "conv_w_dw": lambda v: v.reshape(DEPTH, CONV_K, BW // NDEV), "final_norm": lambda v: v.reshape(1, 1, D),
    }
    names = _SMALL + ("conv_w_dw", "final_norm")
    relay = lambda k, v: lay[k](v) if k in lay else v
    sm = _adam_small(names, recv, {k: relay(k, a[k]) for k in names}, {k: relay(k, a["m_" + k]) for k in names},
                     {k: relay(k, a["v_" + k]) for k in names})
    for k in names:
        back = (lambda r: jnp.transpose(r, (0, 1, 3, 2))) if k in ("ssm_b_re", "ssm_b_im") else (lambda r: r.reshape(a[k].shape))
        results[k] = tuple(back(r) for r in sm[k])

    outs = [loss, grad_x]
    for part in range(4):
        outs += [results[k][part] for k in _OUT_ORDER]
    return tuple(outs)


def kernel(x, norm1, w_in, b_gate, ssm_a_re, ssm_a_im, ssm_log_dt, ssm_b_re, ssm_b_im, ssm_c_re, ssm_c_im, ssm_d, ssm_w_glu, ssm_b_glu, ssm_w_proj, conv_w_dw, conv_b_dw, conv_ln_g, conv_ln_b, conv_w_proj, pool_w_group, pool_scale, pool_w_proj, w_out, norm2, ffn_w_gate, ffn_w_up, ffn_w_down, final_norm, loss_target, m_norm1, m_w_in, m_b_gate, m_ssm_a_re, m_ssm_a_im, m_ssm_log_dt, m_ssm_b_re, m_ssm_b_im, m_ssm_c_re, m_ssm_c_im, m_ssm_d, m_ssm_w_glu, m_ssm_b_glu, m_ssm_w_proj, m_conv_w_dw, m_conv_b_dw, m_conv_ln_g, m_conv_ln_b, m_conv_w_proj, m_pool_w_group, m_pool_scale, m_pool_w_proj, m_w_out, m_norm2, m_ffn_w_gate, m_ffn_w_up, m_ffn_w_down, m_final_norm, v_norm1, v_w_in, v_b_gate, v_ssm_a_re, v_ssm_a_im, v_ssm_log_dt, v_ssm_b_re, v_ssm_b_im, v_ssm_c_re, v_ssm_c_im, v_ssm_d, v_ssm_w_glu, v_ssm_b_glu, v_ssm_w_proj, v_conv_w_dw, v_conv_b_dw, v_conv_ln_g, v_conv_ln_b, v_conv_w_proj, v_pool_w_group, v_pool_scale, v_pool_w_proj, v_w_out, v_norm2, v_ffn_w_gate, v_ffn_w_up, v_ffn_w_down, v_final_norm):
    return _train_step(dict(locals()))
```

```python
import functools

import jax
import jax.numpy as jnp
from jax import lax
from jax.experimental import pallas as pl
from jax.experimental.pallas import tpu as pltpu

F32 = jnp.float32
BF16 = jnp.bfloat16

NDEV = 8
DEPTH = 2
D = 1024
BW = 512
NSTATE = 64
SGRP = 16
NGRP = BW // SGRP
GB = 8
NBLK = NGRP // GB
NS = GB * NSTATE
CONV_K = 31
HALO = 32
PHALO = 16
IN_W = 5120
HID = 2816
HSH = HID // NDEV
HPAD = 384
HIDP = HPAD * NDEV
EPS = 1e-6
VMEM_LIMIT = 56 * 1024 * 1024

ADAM_LR, ADAM_B1, ADAM_B2, ADAM_EPS, ADAM_WD, ADAM_STEP = 0.001, 0.9, 0.999, 1e-08, 0.01, 10

MESH = pl.DeviceIdType.MESH
ANY = pl.BlockSpec(memory_space=pl.ANY)


def _call(body, **kw):
    return pl.pallas_call(body, **kw)


def _cparams(sem=None):
    return pltpu.CompilerParams(dimension_semantics=sem, vmem_limit_bytes=VMEM_LIMIT)


def _dot(a, b):
    return jnp.dot(a.astype(BF16), b.astype(BF16), preferred_element_type=F32)


def _dot_nt(a, b):
    return lax.dot_general(a.astype(BF16), b.astype(BF16), (((1,), (1,)), ((), ())), preferred_element_type=F32)


def _dot_tn(a, b):
    return lax.dot_general(a.astype(BF16), b.astype(BF16), (((0,), (0,)), ((), ())), preferred_element_type=F32)


@jax.custom_vjp
def _mm(a, w):
    return _dot(a, w)


def _mm_fwd(a, w):
    return _dot(a, w), w


def _mm_bwd(w, ct):
    return _dot_nt(ct, w), jnp.zeros_like(w)


_mm.defvjp(_mm_fwd, _mm_bwd)


def _rms(x, g):
    return x * lax.rsqrt(jnp.mean(x * x, axis=-1, keepdims=True) + EPS) * g


def _disc(are, aim, ldt):
    dt = jnp.exp(ldt)
    mag = jnp.exp(dt * are)
    ang = dt * aim
    abr = mag * jnp.cos(ang)
    abi = mag * jnp.sin(ang)
    den = are * are + aim * aim
    nr = abr - 1.0
    fr = (nr * are + abi * aim) / den
    fi = (abi * are - nr * aim) / den
    return abr, abi, fr, fi


def _bbar(fr, fi, br, bi):
    return fr * br - fi * bi, fr * bi + fi * br


def _cmul(ar, ai, br, bi):
    return ar * br - ai * bi, ar * bi + ai * br


def _scan_rows(re_ref, im_ref, ar, ai, n_rows, reverse, hre_ref=None, him_ref=None):
    n = ar.shape[1]
    shape = (8, n)
    rows = lax.broadcasted_iota(jnp.int32, shape, 0)
    a1 = (jnp.broadcast_to(ar, shape), jnp.broadcast_to(ai, shape))
    a2 = _cmul(*a1, *a1)
    a4 = _cmul(*a2, *a2)
    pr = jnp.zeros(shape, F32)
    pi = jnp.zeros(shape, F32)
    pw = a1
    for k in range(8):
        sel = rows == ((7 - k) if reverse else k)
        pr = jnp.where(sel, pw[0], pr)
        pi = jnp.where(sel, pw[1], pi)
        pw = _cmul(*pw, *a1)
    nt = n_rows // 8
    with_acc = hre_ref is not None

    def body(i, carry):
        cr, ci = carry[0], carry[1]
        t = (nt - 1 - i) if reverse else i
        off = pl.multiple_of(t * 8, 8)
        xr = re_ref[pl.ds(off, 8), :]
        xi = im_ref[pl.ds(off, 8), :]
        for k, (kr, ki) in ((1, a1), (2, a2), (4, a4)):
            if reverse:
                keep, sh = rows < 8 - k, 8 - k
            else:
                keep, sh = rows >= k, k
            sr = jnp.where(keep, pltpu.roll(xr, sh, 0), 0.0)
            si = jnp.where(keep, pltpu.roll(xi, sh, 0), 0.0)
            xr, xi = xr + kr * sr - ki * si, xi + kr * si + ki * sr
        xr, xi = xr + pr * cr - pi * ci, xi + pr * ci + pi * cr
        re_ref[pl.ds(off, 8), :] = xr
        im_ref[pl.ds(off, 8), :] = xi
        edge = 0 if reverse else 7
        out = (jnp.broadcast_to(xr[edge:edge + 1, :], shape), jnp.broadcast_to(xi[edge:edge + 1, :], shape))
        if with_acc:
            hr = hre_ref[pl.ds(off, 8), :]
            hi = him_ref[pl.ds(off, 8), :]
            offp = pl.multiple_of(jnp.maximum(t - 1, 0) * 8, 8)
            live = jnp.where(t > 0, 1.0, 0.0)
            lr = jnp.broadcast_to(hre_ref[pl.ds(offp, 8), :][7:8, :], shape) * live
            li = jnp.broadcast_to(him_ref[pl.ds(offp, 8), :][7:8, :], shape) * live
            hpr = jnp.where(rows == 0, lr, pltpu.roll(hr, 1, 0))
            hpi = jnp.where(rows == 0, li, pltpu.roll(hi, 1, 0))
            out = out + (carry[2] + xr * hpr + xi * hpi, carry[3] + xi * hpr - xr * hpi)
        return out

    zero = jnp.zeros(shape, F32)
    init = (zero, zero, zero, zero) if with_acc else (zero, zero)
    res = lax.fori_loop(0, nt, body, init)
    return res[2:] if with_acc else None


TOKEN_SPEC = pl.BlockSpec((8, 128), lambda i, j: (0, 0))


def _inproj_fwd(x, gamma, w, token, tm=1024, nb=2):
    t_rows = x.shape[0]
    tm = min(tm, t_rows)
    oc = w.shape[2]
    n = NDEV * oc
    tn = nb * oc

    def body(x_ref, g_ref, w_ref, token_ref, z_ref, h_ref):
        @pl.when(pl.program_id(1) == 0)
        def _():
            h_ref[...] = _rms(x_ref[...], g_ref[...]).astype(BF16)
        for q in range(nb):
            z_ref[:, oc * q:oc * (q + 1)] = jnp.dot(h_ref[...], w_ref[q], preferred_element_type=F32)

    return _call(
        body, name="inproj_fwd", grid=(t_rows // tm, n // tn),
        in_specs=[pl.BlockSpec((tm, D), lambda i, j: (i, 0)), pl.BlockSpec((1, D), lambda i, j: (0, 0)),
                  pl.BlockSpec((nb, D, oc), lambda i, j: (j, 0, 0)), TOKEN_SPEC],
        out_specs=pl.BlockSpec((tm, tn), lambda i, j: (i, j)),
        out_shape=jax.ShapeDtypeStruct((t_rows, n), F32),
        scratch_shapes=[pltpu.VMEM((tm, D), BF16)],
        compiler_params=_cparams(("parallel", "arbitrary")))(x, gamma, w, token)


def _ssm_specs(t_rows):
    row = pl.BlockSpec((1, NS), lambda j: (0, j))
    return dict(
        u=pl.BlockSpec((t_rows, GB * SGRP), lambda j: (0, j)),
        row=row,
        bexp=pl.BlockSpec((None, GB * SGRP, NS), lambda j: (j, 0, 0)),
        cexp=pl.BlockSpec((None, NS, GB * SGRP), lambda j: (j, 0, 0)),
        d=pl.BlockSpec((1, GB * SGRP), lambda j: (0, j)),
        h=pl.BlockSpec((t_rows, NS), lambda j: (0, j)),
    )


def _ssm_fwd(z, p):
    t_rows = z.shape[0]
    s = _ssm_specs(t_rows)

    def body(u_ref, are_ref, aim_ref, ldt_ref, br_ref, bi_ref, cr_ref, ci_ref, d_ref, y_ref, hr_ref, hi_ref):
        abr, abi, fr, fi = _disc(are_ref[...], aim_ref[...], ldt_ref[...])
        bbr, bbi = _bbar(fr, fi, br_ref[...], bi_ref[...])
        u = u_ref[...]
        hr_ref[...] = _dot(u, bbr)
        hi_ref[...] = _dot(u, bbi)
        _scan_rows(hr_ref, hi_ref, abr, abi, t_rows, False)
        y_ref[...] = _dot(hr_ref[...], cr_ref[...]) - _dot(hi_ref[...], ci_ref[...]) + d_ref[...] * u

    return _call(
        body, name="ssm_fwd", grid=(NBLK,),
        in_specs=[s["u"], s["row"], s["row"], s["row"], s["bexp"], s["bexp"], s["cexp"], s["cexp"], s["d"]],
        out_specs=[s["u"], s["h"], s["h"]],
        out_shape=[jax.ShapeDtypeStruct((t_rows, BW), F32), jax.ShapeDtypeStruct((t_rows, NGRP * NSTATE), F32),
                   jax.ShapeDtypeStruct((t_rows, NGRP * NSTATE), F32)],
        compiler_params=_cparams(("parallel",)))(
            z, p["are"], p["aim"], p["ldt"], p["bexp_re"], p["bexp_im"], p["cexp_re"], p["cexp_im"], p["dskip"])


def _ssm_bwd(dy, z, hre, him, p):
    t_rows = z.shape[0]
    s = _ssm_specs(t_rows)
    nstates = NGRP * NSTATE

    def body(dy_ref, u_ref, hr_ref, hi_ref, are_ref, aim_ref, ldt_ref, br_ref, bi_ref, cr_ref, ci_ref, d_ref,
             du_ref, dbr_ref, dbi_ref, dcr_ref, dci_ref, dd_ref, dar_ref, dai_ref, dldt_ref, lr_ref, li_ref):
        rows3 = (are_ref[...], aim_ref[...], ldt_ref[...])
        (abr, abi, fr, fi), disc_vjp = jax.vjp(_disc, *rows3)
        (bbr, bbi), bbar_vjp = jax.vjp(_bbar, fr, fi, br_ref[...], bi_ref[...])
        dy = dy_ref[...]
        u = u_ref[...]
        lr_ref[...] = _dot_nt(dy, cr_ref[...])
        li_ref[...] = -_dot_nt(dy, ci_ref[...])
        dcr_ref[...] = _dot_tn(hr_ref[...], dy)
        dci_ref[...] = -_dot_tn(hi_ref[...], dy)
        dd_ref[...] = jnp.sum(dy * u, axis=0, keepdims=True)
        acc_r, acc_i = _scan_rows(lr_ref, li_ref, abr, -abi, t_rows, True, hr_ref, hi_ref)
        dabr = jnp.sum(acc_r, axis=0, keepdims=True)
        dabi = jnp.sum(acc_i, axis=0, keepdims=True)
        lam_r = lr_ref[...]
        lam_i = li_ref[...]
        du = d_ref[...] * dy + _dot_nt(lam_r, bbr) + _dot_nt(lam_i, bbi)
        du_ref[...] = du.astype(BF16)
        dbbr = _dot_tn(u, lam_r)
        dbbi = _dot_tn(u, lam_i)
        dfr, dfi, dbr, dbi = bbar_vjp((dbbr, dbbi))
        dbr_ref[...] = dbr
        dbi_ref[...] = dbi
        dar, dai, dldt = disc_vjp((dabr, dabi, dfr, dfi))
        dar_ref[...] = dar
        dai_ref[...] = dai
        lane_grp = lax.broadcasted_iota(jnp.int32, (NS, 128), 0) // NSTATE
        col = lax.broadcasted_iota(jnp.int32, (NS, 128), 1)
        seg = jnp.where(lane_grp == col, 1.0, 0.0).astype(F32)
        dldt_ref[...] = jnp.dot(jnp.broadcast_to(dldt, (8, NS)), seg, preferred_element_type=F32,
                                precision=lax.Precision.HIGHEST)

    dyspec = pl.BlockSpec((t_rows, GB * SGRP), lambda j: (0, j))
    return _call(
        body, name="ssm_bwd", grid=(NBLK,),
        in_specs=[dyspec, s["u"], s["h"], s["h"], s["row"], s["row"], s["row"], s["bexp"], s["bexp"], s["cexp"],
                  s["cexp"], s["d"]],
        out_specs=[dyspec, s["bexp"], s["bexp"], s["cexp"], s["cexp"], s["d"], s["row"], s["row"],
                   pl.BlockSpec((8, 128), lambda j: (j, 0))],
        out_shape=[jax.ShapeDtypeStruct((t_rows, BW), BF16),
                   jax.ShapeDtypeStruct((NBLK, GB * SGRP, NS), F32), jax.ShapeDtypeStruct((NBLK, GB * SGRP, NS), F32),
                   jax.ShapeDtypeStruct((NBLK, NS, GB * SGRP), F32), jax.ShapeDtypeStruct((NBLK, NS, GB * SGRP), F32),
                   jax.ShapeDtypeStruct((1, BW), F32), jax.ShapeDtypeStruct((1, nstates), F32),
                   jax.ShapeDtypeStruct((1, nstates), F32), jax.ShapeDtypeStruct((NBLK * 8, 128), F32)],
        scratch_shapes=[pltpu.VMEM((t_rows, NS), F32), pltpu.VMEM((t_rows, NS), F32)],
        compiler_params=_cparams(("parallel",)))(
            dy, z, hre, him, p["are"], p["aim"], p["ldt"], p["bexp_re"], p["bexp_im"], p["cexp_re"], p["cexp_im"],
            p["dskip"])


def _conv_fwd(z, w, b, tm=256):
    t_rows = z.shape[0]
    hb = tm // HALO

    def body(va_ref, vb_ref, ha_ref, hb_ref, w_ref, b_ref, o_ref, win_ref):
        live = jnp.where(pl.program_id(0) > 0, 1.0, 0.0)
        win_ref[0:HALO, :] = ha_ref[...] * jax.nn.sigmoid(hb_ref[...]) * live
        win_ref[HALO:HALO + tm, :] = va_ref[...] * jax.nn.sigmoid(vb_ref[...])
        acc = jnp.broadcast_to(b_ref[...], (tm, BW))
        for k in range(CONV_K):
            acc = acc + w_ref[k:k + 1, :] * win_ref[pl.ds(HALO - (CONV_K - 1) + k, tm), :]
        o_ref[...] = acc

    halo = lambda col: pl.BlockSpec((HALO, BW), lambda i: (jnp.maximum(i * hb - 1, 0), col))
    return _call(
        body, name="conv_fwd", grid=(t_rows // tm,),
        in_specs=[pl.BlockSpec((tm, BW), lambda i: (i, 1)), pl.BlockSpec((tm, BW), lambda i: (i, 2)), halo(1), halo(2),
                  pl.BlockSpec((CONV_K, BW), lambda i: (0, 0)), pl.BlockSpec((1, BW), lambda i: (0, 0))],
        out_specs=pl.BlockSpec((tm, BW), lambda i: (i, 0)),
        out_shape=jax.ShapeDtypeStruct((t_rows, BW), F32),
        scratch_shapes=[pltpu.VMEM((HALO + tm, BW), F32)],
        compiler_params=_cparams(("parallel",)))(z, z, z, z, w, b)


def _conv_bwd(dcv, z, w, tm=256):
    t_rows = z.shape[0]
    nt = t_rows // tm
    hb = tm // HALO
    csh = BW // NDEV

    def body(d_ref, dn_ref, va_ref, vb_ref, ha_ref, hb_ref, w_ref, dva_ref, dvb_ref, dw8_ref, db_ref,
             hwin_ref, dwin_ref, dw_ref):
        i = pl.program_id(0)

        @pl.when(i == 0)
        def _():
            dw_ref[...] = jnp.zeros_like(dw_ref)
            db_ref[...] = jnp.zeros_like(db_ref)

        live_prev = jnp.where(i > 0, 1.0, 0.0)
        live_next = jnp.where(i < nt - 1, 1.0, 0.0)
        va = va_ref[...]
        sig = jax.nn.sigmoid(vb_ref[...])
        hwin_ref[0:HALO, :] = ha_ref[...] * jax.nn.sigmoid(hb_ref[...]) * live_prev
        hwin_ref[HALO:HALO + tm, :] = va * sig
        d = d_ref[...]
        dwin_ref[0:tm, :] = d
        dwin_ref[tm:tm + HALO, :] = dn_ref[...] * live_next
        dh = jnp.zeros((tm, BW), F32)
        dws = []
        for k in range(CONV_K):
            dh = dh + w_ref[k:k + 1, :] * dwin_ref[pl.ds(CONV_K - 1 - k, tm), :]
            dws.append(jnp.sum(d * hwin_ref[pl.ds(HALO - (CONV_K - 1) + k, tm), :], axis=0, keepdims=True))
        dws.append(jnp.zeros((1, BW), F32))
        dw_ref[...] += jnp.concatenate(dws, axis=0)
        db_ref[...] += jnp.sum(d, axis=0, keepdims=True)
        dva_ref[...] = (dh * sig).astype(BF16)
        dvb_ref[...] = (dh * va * sig * (1.0 - sig)).astype(BF16)

        @pl.when(i == nt - 1)
        def _():
            acc = dw_ref[...]
            for q in range(NDEV):
                dw8_ref[q] = acc[:, csh * q:csh * (q + 1)]

    halo = lambda col: pl.BlockSpec((HALO, BW), lambda i: (jnp.maximum(i * hb - 1, 0), col))
    return _call(
        body, name="conv_bwd", grid=(nt,),
        in_specs=[pl.BlockSpec((tm, BW), lambda i: (i, 0)),
                  pl.BlockSpec((HALO, BW), lambda i: (jnp.minimum((i + 1) * hb, t_rows // HALO - 1), 0)),
                  pl.BlockSpec((tm, BW), lambda i: (i, 1)), pl.BlockSpec((tm, BW), lambda i: (i, 2)), halo(1), halo(2),
                  pl.BlockSpec((CONV_K, BW), lambda i: (0, 0))],
        out_specs=[pl.BlockSpec((tm, BW), lambda i: (i, 0)), pl.BlockSpec((tm, BW), lambda i: (i, 0)),
                   pl.BlockSpec((NDEV, 32, csh), lambda i: (0, 0, 0)), pl.BlockSpec((1, BW), lambda i: (0, 0))],
        out_shape=[jax.ShapeDtypeStruct((t_rows, BW), BF16), jax.ShapeDtypeStruct((t_rows, BW), BF16),
                   jax.ShapeDtypeStruct((NDEV, 32, csh), F32), jax.ShapeDtypeStruct((1, BW), F32)],
        scratch_shapes=[pltpu.VMEM((HALO + tm, BW), F32), pltpu.VMEM((tm + HALO, BW), F32), pltpu.VMEM((32, BW), F32)],
        compiler_params=_cparams(("arbitrary",)))(dcv, dcv, z, z, z, z, w)


def _pool_rows(i, tm, n_rows, first_row):
    grp = lax.broadcasted_iota(jnp.int32, (1, BW), 1) // (BW // 4)
    wlen = jnp.where(grp == 0, 2.0, jnp.where(grp == 1, 4.0, jnp.where(grp == 2, 8.0, 16.0)))
    t = (i * tm + first_row + lax.broadcasted_iota(jnp.int32, (n_rows, 1), 0)).astype(F32)
    return grp, 1.0 / jnp.minimum(t + 1.0, wlen)


def _pool_pick(grp, s2, s4, s8, s16):
    return jnp.where(grp == 0, s2, jnp.where(grp == 1, s4, jnp.where(grp == 2, s8, s16)))


def _pool_fwd(z, tm=256):
    t_rows = z.shape[0]
    hb = tm // PHALO

    def body(u_ref, h_ref, o_ref):
        i = pl.program_id(0)
        u = u_ref[...]
        win = jnp.concatenate([h_ref[...] * jnp.where(i > 0, 1.0, 0.0), u], axis=0)
        s2 = win + pltpu.roll(win, 1, 0)
        s4 = s2 + pltpu.roll(s2, 2, 0)
        s8 = s4 + pltpu.roll(s4, 4, 0)
        s16 = s8 + pltpu.roll(s8, 8, 0)
        grp, inv = _pool_rows(i, tm, tm, 0)
        o_ref[...] = _pool_pick(grp, s2, s4, s8, s16)[PHALO:, :] * inv - u

    return _call(
        body, name="pool_fwd", grid=(t_rows // tm,),
        in_specs=[pl.BlockSpec((tm, BW), lambda i: (i, 3)),
                  pl.BlockSpec((PHALO, BW), lambda i: (jnp.maximum(i * hb - 1, 0), 3))],
        out_specs=pl.BlockSpec((tm, BW), lambda i: (i, 0)),
        out_shape=jax.ShapeDtypeStruct((t_rows, BW), F32),
        compiler_params=_cparams(("parallel",)))(z, z)


def _pool_bwd(dp, tm=256):
    t_rows = dp.shape[0]
    nt = t_rows // tm
    hb = tm // PHALO
    ln = tm + PHALO

    def body(d_ref, dn_ref, o_ref):
        i = pl.program_id(0)
        d = d_ref[...]
        grp, inv = _pool_rows(i, tm, ln, 0)
        win = jnp.concatenate([d, dn_ref[...] * jnp.where(i < nt - 1, 1.0, 0.0)], axis=0) * inv
        s2 = win + pltpu.roll(win, ln - 1, 0)
        s4 = s2 + pltpu.roll(s2, ln - 2, 0)
        s8 = s4 + pltpu.roll(s4, ln - 4, 0)
        s16 = s8 + pltpu.roll(s8, ln - 8, 0)
        o_ref[...] = (_pool_pick(grp, s2, s4, s8, s16)[:tm, :] - d).astype(BF16)

    return _call(
        body, name="pool_bwd", grid=(nt,),
        in_specs=[pl.BlockSpec((tm, BW), lambda i: (i, 0)),
                  pl.BlockSpec((PHALO, BW), lambda i: (jnp.minimum((i + 1) * hb, t_rows // PHALO - 1), 0))],
        out_specs=pl.BlockSpec((tm, BW), lambda i: (i, 0)),
        out_shape=jax.ShapeDtypeStruct((t_rows, BW), BF16),
        compiler_params=_cparams(("parallel",)))(dp, dp)


_MERGE_W = ("wglu", "bglu", "wpa", "lng", "lnb", "wpb", "wgrp", "scale", "wpc", "bgate", "wout")
_MERGE_SMALL = ("bglu", "lng", "lnb", "scale", "bgate")
_MERGE_BLOCKED = ("wpa", "wpb", "wpc")


def _merge_load(name, ref):
    if name in _MERGE_BLOCKED:
        return jnp.concatenate([ref[q] for q in range(NDEV)], axis=1)
    return ref[...]


def _merge_math(x, yssm, cv, pbar, zg, w, taps):
    t_glu, t_ya, t_yb, t_p, t_yc = taps
    g = jax.nn.gelu(yssm)
    outa = g * jax.nn.sigmoid(_mm(g, w["wglu"]) + t_glu + w["bglu"])
    ya = _mm(outa, w["wpa"]) + t_ya
    mu = jnp.mean(cv, axis=-1, keepdims=True)
    var = jnp.mean(jnp.square(cv - mu), axis=-1, keepdims=True)
    hs = jax.nn.silu((cv - mu) * lax.rsqrt(var + EPS) * w["lng"] + w["lnb"])
    yb = _mm(hs, w["wpb"]) + t_yb
    gw = BW // 4
    pk = jnp.concatenate([_mm(pbar[:, gw * k:gw * (k + 1)], w["wgrp"][k]) for k in range(4)], axis=1) + t_p
    pc = pk * w["scale"]
    yc = _mm(pc, w["wpc"]) + t_yc
    gates = jax.nn.sigmoid(zg + w["bgate"])
    merged = gates[:, :D] * ya + gates[:, D:2 * D] * yb + gates[:, 2 * D:] * yc
    x1 = x + _mm(merged, w["wout"])
    acts = tuple(a.astype(BF16) for a in (g, outa, hs, pbar, pc, merged))
    return x1, acts


def _merge_specs(tm, p):
    rows = lambda width, col=0: pl.BlockSpec((tm, width), lambda i, c=col: (i, c))
    data = [rows(D), rows(BW), rows(BW), rows(BW), rows(D, 2), rows(D, 3), rows(D, 4)]
    wspecs = []
    for name in _MERGE_W:
        nd = p[name].ndim
        wspecs.append(pl.BlockSpec(p[name].shape, lambda i, nd=nd: (0,) * nd))
    return rows, data, wspecs


def _merge_fwd(x, yssm, cv, pbar, z, p, token, tm=256):
    t_rows = x.shape[0]
    rows, data, wspecs = _merge_specs(tm, p)

    def body(x_ref, y_ref, cv_ref, pb_ref, za_ref, zb_ref, zc_ref, *rest):
        w = {name: _merge_load(name, r) for name, r in zip(_MERGE_W, rest[:len(_MERGE_W)])}
        o_ref = rest[len(_MERGE_W) + 1]
        taps = (0.0, 0.0, 0.0, 0.0, 0.0)
        zg = jnp.concatenate([za_ref[...], zb_ref[...], zc_ref[...]], axis=1)
        o_ref[...] = _merge_math(x_ref[...], y_ref[...], cv_ref[...], pb_ref[...], zg, w, taps)[0]

    return _call(
        body, name="merge_fwd", grid=(t_rows // tm,),
        in_specs=data + wspecs + [pl.BlockSpec((8, 128), lambda i: (0, 0))], out_specs=rows(D),
        out_shape=jax.ShapeDtypeStruct((t_rows, D), F32),
        compiler_params=_cparams(("parallel",)))(x, yssm, cv, pbar, z, z, z, *[p[n] for n in _MERGE_W], token)


def _merge_bwd(dx1, x, yssm, cv, pbar, z, p, token, tm=256):
    t_rows = x.shape[0]
    rows, data, wspecs = _merge_specs(tm, p)
    nw = len(_MERGE_W)

    def body(dx_ref, x_ref, y_ref, cv_ref, pb_ref, za_ref, zb_ref, zc_ref, *rest):
        w = {name: _merge_load(name, r) for name, r in zip(_MERGE_W, rest[:nw])}
        zg = jnp.concatenate([za_ref[...], zb_ref[...], zc_ref[...]], axis=1)
        outs = rest[nw + 1:]
        small = {n: w[n] for n in _MERGE_SMALL}
        taps = (jnp.zeros((tm, BW), F32), jnp.zeros((tm, D), F32), jnp.zeros((tm, D), F32),
                jnp.zeros((tm, BW), F32), jnp.zeros((tm, D), F32))

        def f(yssm_, cv_, pbar_, zg_, small_, taps_):
            return _merge_math(x_ref[...], yssm_, cv_, pbar_, zg_, {**w, **small_}, taps_)

        _, vjp, acts = jax.vjp(f, y_ref[...], cv_ref[...], pb_ref[...], zg, small, taps, has_aux=True)
        dy, dcv, dpb, dzg, dsmall, dtaps = vjp(dx_ref[...])
        outs[0][...] = dy
        outs[1][...] = dcv
        outs[2][...] = dpb
        outs[3][...] = dzg.astype(BF16)
        for k in range(6):
            outs[4 + k][...] = acts[k]
        for k in range(5):
            outs[10 + k][...] = dtaps[k].astype(BF16)

        @pl.when(pl.program_id(0) == 0)
        def _():
            for k in range(5):
                outs[15 + k][...] = jnp.zeros_like(outs[15 + k])

        for k, n in enumerate(_MERGE_SMALL):
            outs[15 + k][...] += dsmall[n]

    f32o = lambda width: jax.ShapeDtypeStruct((t_rows, width), F32)
    bfo = lambda width: jax.ShapeDtypeStruct((t_rows, width), BF16)
    small_shapes = [jax.ShapeDtypeStruct(p[n].shape, F32) for n in _MERGE_SMALL]
    small_specs = [pl.BlockSpec(p[n].shape, lambda i: (0, 0)) for n in _MERGE_SMALL]
    out_shape = ([f32o(BW), f32o(BW), f32o(BW), bfo(3 * D)]
                 + [bfo(BW), bfo(BW), bfo(BW), bfo(BW), bfo(BW), bfo(D)]
                 + [bfo(BW), bfo(D), bfo(D), bfo(BW), bfo(D)] + small_shapes)
    out_specs = ([rows(BW), rows(BW), rows(BW), rows(3 * D)]
                 + [rows(BW)] * 5 + [rows(D)]
                 + [rows(BW), rows(D), rows(D), rows(BW), rows(D)] + small_specs)
    return _call(
        body, name="merge_bwd", grid=(t_rows // tm,),
        in_specs=[rows(D)] + data + wspecs + [pl.BlockSpec((8, 128), lambda i: (0, 0))], out_specs=out_specs,
        out_shape=out_shape, compiler_params=_cparams(("arbitrary",)))(
            dx1, x, yssm, cv, pbar, z, z, z, *[p[n] for n in _MERGE_W], token)


def _ffn_fwd(x1, gamma, wg, wu, wd, tm=1024, th=512):
    t_rows = x1.shape[0]
    tm = min(tm, t_rows)
    nh = HIDP // th

    def body(x_ref, g_ref, wg_ref, wu_ref, wd_ref, o_ref, gp_ref, up_ref, h_ref, acc_ref):
        j = pl.program_id(1)

        @pl.when(j == 0)
        def _():
            h_ref[...] = _rms(x_ref[...], g_ref[...]).astype(BF16)
            acc_ref[...] = jnp.zeros_like(acc_ref)

        gp = _dot_nt(h_ref[...], wg_ref[...])
        up = _dot_nt(h_ref[...], wu_ref[...])
        gp_ref[...] = gp
        up_ref[...] = up
        acc_ref[...] += _dot(jax.nn.silu(gp) * up, wd_ref[...])

        @pl.when(j == nh - 1)
        def _():
            o_ref[...] = x_ref[...] + acc_ref[...]

    return _call(
        body, name="ffn_fwd", grid=(t_rows // tm, nh),
        in_specs=[pl.BlockSpec((tm, D), lambda i, j: (i, 0)), pl.BlockSpec((1, D), lambda i, j: (0, 0)),
                  pl.BlockSpec((th, D), lambda i, j: (j, 0)), pl.BlockSpec((th, D), lambda i, j: (j, 0)),
                  pl.BlockSpec((th, D), lambda i, j: (j, 0))],
        out_specs=[pl.BlockSpec((tm, D), lambda i, j: (i, 0)), pl.BlockSpec((tm, th), lambda i, j: (i, j)),
                   pl.BlockSpec((tm, th), lambda i, j: (i, j))],
        out_shape=[jax.ShapeDtypeStruct((t_rows, D), F32), jax.ShapeDtypeStruct((t_rows, HIDP), F32),
                   jax.ShapeDtypeStruct((t_rows, HIDP), F32)],
        scratch_shapes=[pltpu.VMEM((tm, D), BF16), pltpu.VMEM((tm, D), F32)],
        compiler_params=_cparams(("parallel", "arbitrary")))(x1, gamma, wg, wu, wd)


def _rms_bwd_tail(x, gamma, dh):
    _, vjp = jax.vjp(_rms, x, gamma)
    return vjp(dh)


def _ffn_bwd(dx2, x1, gamma, gpre, upre, wg, wu, wd, token, tm=512, th=1024):
    t_rows = x1.shape[0]
    nh = HIDP // th

    def body(d_ref, x_ref, g_ref, gp_ref, up_ref, wg_ref, wu_ref, wd_ref, token_ref,
             dx_ref, dgam_ref, dgp_ref, dup_ref, act_ref, h_ref, acc_ref):
        i = pl.program_id(0)
        j = pl.program_id(1)

        @pl.when(j == 0)
        def _():
            acc_ref[...] = jnp.zeros_like(acc_ref)

        @pl.when((i == 0) & (j == 0))
        def _():
            dgam_ref[...] = jnp.zeros_like(dgam_ref)

        dact = _dot_nt(d_ref[...], wd_ref[...])
        gp = gp_ref[...]
        up = up_ref[...]
        sg = jax.nn.sigmoid(gp)
        silu = gp * sg
        dgp = (dact * up * (sg * (1.0 + gp * (1.0 - sg)))).astype(BF16)
        dup = (dact * silu).astype(BF16)
        dgp_ref[...] = dgp
        dup_ref[...] = dup
        act_ref[...] = (silu * up).astype(BF16)
        acc_ref[...] += _dot(dgp, wg_ref[...]) + _dot(dup, wu_ref[...])

        @pl.when(j == nh - 1)
        def _():
            x = x_ref[...]
            h_ref[...] = _rms(x, g_ref[...]).astype(BF16)
            dx, dgam = _rms_bwd_tail(x, g_ref[...], acc_ref[...])
            dx_ref[...] = d_ref[...] + dx
            dgam_ref[...] += dgam

    row_d = pl.BlockSpec((tm, D), lambda i, j: (i, 0))
    row_h = pl.BlockSpec((tm, th), lambda i, j: (i, j))
    return _call(
        body, name="ffn_bwd", grid=(t_rows // tm, nh),
        in_specs=[row_d, row_d, pl.BlockSpec((1, D), lambda i, j: (0, 0)), row_h, row_h,
                  pl.BlockSpec((th, D), lambda i, j: (j, 0)), pl.BlockSpec((th, D), lambda i, j: (j, 0)),
                  pl.BlockSpec((th, D), lambda i, j: (j, 0)), TOKEN_SPEC],
        out_specs=[row_d, pl.BlockSpec((1, D), lambda i, j: (0, 0)), row_h, row_h, row_h, row_d],
        out_shape=[jax.ShapeDtypeStruct((t_rows, D), F32), jax.ShapeDtypeStruct((1, D), F32),
                   jax.ShapeDtypeStruct((t_rows, HIDP), BF16), jax.ShapeDtypeStruct((t_rows, HIDP), BF16),
                   jax.ShapeDtypeStruct((t_rows, HIDP), BF16), jax.ShapeDtypeStruct((t_rows, D), BF16)],
        scratch_shapes=[pltpu.VMEM((tm, D), F32)],
        compiler_params=_cparams(("arbitrary", "arbitrary")))(dx2, x1, gamma, gpre, upre, wg, wu, wd, token)


def _inproj_bwd(dz, dx1, x, gamma, w, tm=1024, nb=2):
    t_rows = x.shape[0]
    tm = min(tm, t_rows)
    oc = w.shape[2]
    tn = nb * oc
    nn = NDEV // nb

    def body(dz_ref, d1_ref, x_ref, g_ref, w_ref, dx_ref, dgam_ref, h_ref, acc_ref):
        i = pl.program_id(0)
        j = pl.program_id(1)

        @pl.when(j == 0)
        def _():
            acc_ref[...] = jnp.zeros_like(acc_ref)

        @pl.when((i == 0) & (j == 0))
        def _():
            dgam_ref[...] = jnp.zeros_like(dgam_ref)

        for q in range(nb):
            acc_ref[...] += _dot_nt(dz_ref[:, oc * q:oc * (q + 1)], w_ref[q])

        @pl.when(j == nn - 1)
        def _():
            x = x_ref[...]
            h_ref[...] = _rms(x, g_ref[...]).astype(BF16)
            dx, dgam = _rms_bwd_tail(x, g_ref[...], acc_ref[...])
            dx_ref[...] = d1_ref[...] + dx
            dgam_ref[...] += dgam

    row_d = pl.BlockSpec((tm, D), lambda i, j: (i, 0))
    return _call(
        body, name="inproj_bwd", grid=(t_rows // tm, nn),
        in_specs=[pl.BlockSpec((tm, tn), lambda i, j: (i, j)), row_d, row_d, pl.BlockSpec((1, D), lambda i, j: (0, 0)),
                  pl.BlockSpec((nb, D, oc), lambda i, j: (j, 0, 0))],
        out_specs=[row_d, pl.BlockSpec((1, D), lambda i, j: (0, 0)), row_d],
        out_shape=[jax.ShapeDtypeStruct((t_rows, D), F32), jax.ShapeDtypeStruct((1, D), F32),
                   jax.ShapeDtypeStruct((t_rows, D), BF16)],
        scratch_shapes=[pltpu.VMEM((tm, D), F32)],
        compiler_params=_cparams(("arbitrary", "arbitrary")))(dz, dx1, x, gamma, w)


def _matmul_tn(a, b, name, owner_cols=None, tt=2048):
    t_rows, k = a.shape
    n = b.shape[1]
    tk = min(k, 1024)
    tt = min(tt, t_rows)
    nt = t_rows // tt
    if owner_cols is None:
        tn, nb = min(n, 1024), None
        out_spec = pl.BlockSpec((tk, tn), lambda i, j, t: (i, j))
        out_shape = jax.ShapeDtypeStruct((k, n), BF16)
    else:
        nb = min(n // owner_cols, max(1, 1280 // owner_cols))
        tn = nb * owner_cols
        out_spec = pl.BlockSpec((nb, tk, owner_cols), lambda i, j, t: (j, i, 0))
        out_shape = jax.ShapeDtypeStruct((n // owner_cols, k, owner_cols), BF16)

    def body(a_ref, b_ref, o_ref, acc_ref):
        t = pl.program_id(2)

        @pl.when(t == 0)
        def _():
            acc_ref[...] = jnp.zeros_like(acc_ref)

        acc_ref[...] += _dot_tn(a_ref[...], b_ref[...])

        @pl.when(t == nt - 1)
        def _():
            if nb is None:
                o_ref[...] = acc_ref[...].astype(BF16)
            else:
                for q in range(nb):
                    o_ref[q] = acc_ref[:, owner_cols * q:owner_cols * (q + 1)].astype(BF16)

    return _call(
        body, name=name, grid=(k // tk, n // tn, nt),
        in_specs=[pl.BlockSpec((tt, tk), lambda i, j, t: (t, i)), pl.BlockSpec((tt, tn), lambda i, j, t: (t, j))],
        out_specs=out_spec, out_shape=out_shape,
        scratch_shapes=[pltpu.VMEM((tk, tn), F32)],
        compiler_params=_cparams(("parallel", "parallel", "arbitrary")))(a, b)


def _group_tn(a, b):
    t_rows = a.shape[0]
    gw = BW // 4

    def body(a_ref, b_ref, o_ref):
        o_ref[...] = _dot_tn(a_ref[...], b_ref[...])

    return _call(
        body, name="pool_group_tn", grid=(4,),
        in_specs=[pl.BlockSpec((t_rows, gw), lambda k: (0, k)), pl.BlockSpec((t_rows, gw), lambda k: (0, k))],
        out_specs=pl.BlockSpec((None, gw, gw), lambda k: (k, 0, 0)),
        out_shape=jax.ShapeDtypeStruct((4, gw, gw), F32),
        compiler_params=_cparams(("parallel",)))(a, b)


def _loss_head(x2, gamma, target, tm=512):
    t_rows = x2.shape[0]

    def body(x_ref, g_ref, t_ref, loss_ref, dx_ref, dgam_ref):
        @pl.when(pl.program_id(0) == 0)
        def _():
            loss_ref[...] = jnp.zeros_like(loss_ref)
            dgam_ref[...] = jnp.zeros_like(dgam_ref)

        def f(x, g):
            err = jnp.square(_rms(x, g) - t_ref[...])
            return 0.5 * jnp.sum(jnp.mean(err, axis=-1, keepdims=True), axis=0, keepdims=True)

        loss, vjp = jax.vjp(f, x_ref[...], g_ref[...])
        dx, dgam = vjp(jnp.ones((1, 1), F32))
        loss_ref[...] += jnp.broadcast_to(loss, (1, 128))
        dx_ref[...] = dx
        dgam_ref[...] += dgam

    row_d = pl.BlockSpec((tm, D), lambda i: (i, 0))
    return _call(
        body, name="loss_head", grid=(t_rows // tm,),
        in_specs=[row_d, pl.BlockSpec((1, D), lambda i: (0, 0)), row_d],
        out_specs=[pl.BlockSpec((1, 128), lambda i: (0, 0)), row_d, pl.BlockSpec((1, D), lambda i: (0, 0))],
        out_shape=[jax.ShapeDtypeStruct((1, 128), F32), jax.ShapeDtypeStruct((t_rows, D), F32),
                   jax.ShapeDtypeStruct((1, D), F32)],
        compiler_params=_cparams(("arbitrary",)))(x2, gamma, target)


NCHIP = NDEV // 2


def _coords():
    return lax.axis_index("x"), lax.axis_index("y"), lax.axis_index("c")


def _remote(src, dst, send_sem, recv_sem, peer):
    return pltpu.make_async_remote_copy(src_ref=src, dst_ref=dst, send_sem=send_sem, recv_sem=recv_sem,
                                        device_id=peer, device_id_type=MESH)


def _comm_call(name, srcs, out_shapes, n_rec, plan, aliases=None):
    ns, no = len(srcs), len(out_shapes)

    def body(*refs):
        ins, outs = refs[:ns], refs[ns:ns + no]
        loc_sem, send_sem, recv_sem = refs[ns + no:]
        x, y, c = _coords()
        recs = plan(ins, outs, x, y, c)
        assert len(recs) == n_rec
        for k, r in enumerate(recs):
            for src, dst in r.get("local", ()):
                pltpu.make_async_copy(src, dst, loc_sem.at[k]).start()
            for peer, src, dst in r.get("remote", ()):
                _remote(src, dst, send_sem.at[k], recv_sem.at[k], peer).start()
        for k, r in enumerate(recs):
            if r.get("recv_wait") is not None:
                w = r["recv_wait"]
                _remote(w, w, send_sem.at[k], recv_sem.at[k], (x, y, c)).wait_recv()
            if r.get("send_wait") is not None:
                w = r["send_wait"]
                _remote(w, w, send_sem.at[k], recv_sem.at[k], (x, y, c)).wait_send()
            if r.get("local_wait") is not None:
                w = r["local_wait"]
                pltpu.make_async_copy(w, w, loc_sem.at[k]).wait()

    return _call(
        body, name=name, in_specs=[ANY] * ns, out_specs=[ANY] * no, out_shape=out_shapes,
        input_output_aliases=aliases or {}, scratch_shapes=[pltpu.SemaphoreType.DMA((n_rec,))] * 3)(*srcs)


def _gather_call(srcs, out_shapes, items, aliases):
    ns, no, n = len(srcs), len(out_shapes), len(items)

    def body(*refs):
        ins, outs = refs[:ns], refs[ns:ns + no]
        loc, sib_s, sib_r, ici_s, ici_r, fwd_s, fwd_r = refs[ns + no:]
        x, y, c = _coords()
        me, sib = (x, y, c), (x, y, 1 - c)
        chips = [(1 - x, y), (x, 1 - y), (1 - x, 1 - y)]
        index = lambda px, py, pc: 4 * px + 2 * py + pc
        for k, (si, oi, shard, block, _) in enumerate(items):
            mine = block(outs[oi], index(*me))
            src = mine if shard is None else shard(ins[si])
            if shard is not None:
                pltpu.make_async_copy(src, mine, loc.at[k]).start()
            _remote(src, mine, sib_s.at[k], sib_r.at[k], sib).start()
            for chip in chips:
                _remote(src, mine, ici_s.at[k], ici_r.at[k], (*chip, c)).start()
        for k, (si, oi, _, block, blocks) in enumerate(items):
            three = blocks(outs[oi], 3)
            _remote(three, three, ici_s.at[k], ici_r.at[k], me).wait_recv()
            for chip in chips:
                landed = block(outs[oi], index(*chip, c))
                _remote(landed, landed, fwd_s.at[k], fwd_r.at[k], sib).start()
        for k, (si, oi, shard, _, blocks) in enumerate(items):
            one, three = blocks(outs[oi], 1), blocks(outs[oi], 3)
            _remote(one, one, sib_s.at[k], sib_r.at[k], me).wait()
            _remote(three, three, fwd_s.at[k], fwd_r.at[k], me).wait()
            _remote(three, three, ici_s.at[k], ici_r.at[k], me).wait_send()
            if shard is not None:
                pltpu.make_async_copy(one, one, loc.at[k]).wait()

    return _call(
        body, name="gather_weights", in_specs=[ANY] * ns, out_specs=[ANY] * no, out_shape=out_shapes,
        input_output_aliases=aliases, scratch_shapes=[pltpu.SemaphoreType.DMA((n,))] * 7)(*srcs)


_BIG = {
    "w_in": (True, D, IN_W // NDEV),
    "ssm_w_glu": (False, BW // NDEV, BW),
    "ssm_w_proj": (True, BW, D // NDEV),
    "conv_w_proj": (True, BW, D // NDEV),
    "pool_w_proj": (True, BW, D // NDEV),
    "w_out": (False, D // NDEV, D),
    "ffn_w_gate": (False, HPAD, D),
    "ffn_w_up": (False, HPAD, D),
    "ffn_w_down": (False, HPAD, D),
}
GROUP_IN = ("w_in",)
GROUP_MIX = ("ssm_w_glu", "ssm_w_proj", "conv_w_proj", "pool_w_proj", "w_out")
GROUP_FFN = ("ffn_w_gate", "ffn_w_up", "ffn_w_down")


def _gathered_shape(name):
    blocked, kk, nn = _BIG[name]
    return jax.ShapeDtypeStruct((NDEV, kk, nn) if blocked else (NDEV * kk, nn), BF16)


def _block_view(name):
    blocked, kk, _ = _BIG[name]
    if blocked:
        return lambda ref, q: ref.at[q]
    return lambda ref, q: ref.at[pl.ds(pl.multiple_of(q * kk, 16), kk), :]


def _blocks_view(name):
    blocked, kk, _ = _BIG[name]
    if blocked:
        return lambda ref, n: ref.at[pl.ds(0, n)]
    return lambda ref, n: ref.at[pl.ds(0, n * kk), :]


def _place_shards(weights, names, layer, me):
    cnt = len(names)

    def body(me_ref, *refs):
        w_refs, outs, stages, sem = refs[:cnt], refs[cnt:2 * cnt], refs[2 * cnt:3 * cnt], refs[3 * cnt]
        q = me_ref[0]
        copies = []
        for k, name in enumerate(names):
            _, kk, nn = _BIG[name]
            rows = w_refs[k].shape[0]
            stages[k][0:rows, :] = w_refs[k][...].astype(BF16)
            if rows < kk:
                stages[k][rows:kk, :] = jnp.zeros((kk - rows, nn), BF16)
            copies.append(pltpu.make_async_copy(stages[k], _block_view(name)(outs[k], q), sem.at[k]))
            copies[-1].start()
        for cp in copies:
            cp.wait()

    in_specs = [pl.BlockSpec(memory_space=pltpu.SMEM)]
    in_specs += [pl.BlockSpec((None,) + weights[n].shape[1:], lambda i: (layer, 0, 0)) for n in names]
    res = _call(
        body, name="place_shards", grid=(1,), in_specs=in_specs, out_specs=[ANY] * cnt,
        out_shape=[_gathered_shape(n) for n in names],
        scratch_shapes=[pltpu.VMEM(_BIG[n][1:], BF16) for n in names] + [pltpu.SemaphoreType.DMA((cnt,))],
        compiler_params=_cparams(("arbitrary",)))(me, *[weights[n] for n in names])
    return dict(zip(names, res))


def _gather_weights(placed, conv_dw, names):
    cnt = len(names)
    srcs = [placed[n] for n in names] + [conv_dw]
    outs = [_gathered_shape(n) for n in names] + [jax.ShapeDtypeStruct((NDEV,) + conv_dw.shape, conv_dw.dtype)]
    items = [(k, k, None, _block_view(n), _blocks_view(n)) for k, n in enumerate(names)]
    items.append((cnt, cnt, lambda ref: ref, lambda ref, q: ref.at[q], lambda ref, n: ref.at[pl.ds(0, n)]))
    res = _gather_call(srcs, outs, items, {k: k for k in range(cnt)})
    return dict(zip(names, res[:-1])), res[-1]


def _gather_start(placed, names, after, tag):
    def copies(src_refs, land_refs, x, y, c):
        me = 4 * x + 2 * y + c
        peers = [(x, y, 1 - c), (1 - x, y, c), (x, 1 - y, c), (1 - x, 1 - y, c)]
        out = []
        for k, n in enumerate(names):
            mine = _block_view(n)(land_refs[k], me)
            out.append([(peer, mine, mine) for peer in peers])
        return out

    return _split_start("gather_start_" + tag, [], [placed[n] for n in names], copies, after)


def _gather_finish(handle, names, after, tag):
    four = [functools.partial(lambda ref, bv: bv(ref, 4), bv=_blocks_view(n)) for n in names]
    lands = _split_wait("gather_wait_" + tag, handle, four, after)
    cnt = len(names)

    def plan(ins, out_refs, x, y, c):
        sib = (x, y, 1 - c)
        recs = []
        for k, n in enumerate(names):
            three = _blocks_view(n)(out_refs[k], 3)
            remote = []
            for px, py in [(1 - x, y), (x, 1 - y), (1 - x, 1 - y)]:
                landed = _block_view(n)(out_refs[k], 4 * px + 2 * py + c)
                remote.append((sib, landed, landed))
            recs.append(dict(remote=remote, send_wait=three, recv_wait=three))
        return recs

    res = _comm_call("gather_pair_" + tag, lands, [jax.ShapeDtypeStruct(l.shape, l.dtype) for l in lands], cnt, plan,
                     aliases={k: k for k in range(cnt)})
    return dict(zip(names, res))


def _pair_add(name, grads, rcv, core):
    nl = len(grads)
    _, kk, nn = grads[0].shape

    def body(c_ref, *refs):
        l = pl.program_id(0)
        own = refs[0][...]
        for j in range(1, nl):
            own = jnp.where(l == j, refs[j][...], own)
        refs[nl + 1][...] = (own.astype(F32) + refs[nl][...].astype(F32)).astype(BF16)

    gspec = lambda j: pl.BlockSpec((None, kk, nn), lambda l, h, c_ref: (jnp.where(l == j, 2 * h + c_ref[0], 0), 0, 0))
    rspec = pl.BlockSpec((None, None, kk, nn), lambda l, h, c_ref: (h, l, 0, 0))
    return _call(
        body, name="pair_add_" + name,
        grid_spec=pltpu.PrefetchScalarGridSpec(num_scalar_prefetch=1, grid=(nl, NCHIP),
                                               in_specs=[gspec(j) for j in range(nl)] + [rspec], out_specs=rspec),
        out_shape=jax.ShapeDtypeStruct(rcv.shape, BF16),
        compiler_params=_cparams(("arbitrary", "arbitrary")))(core, *grads, rcv)


def _pair_add_small(owned, lists, core):
    on, ln = list(owned), list(lists)
    flat = []
    for n in on:
        flat += list(owned[n][0]) + [owned[n][1]]
    for n in ln:
        flat += list(lists[n][0]) + [lists[n][1]]

    def body(c_ref, *refs):
        outs = refs[len(flat):]
        c = c_ref[0]
        pos = 0
        for k, n in enumerate(on):
            nl = len(owned[n][0])
            for h in range(NCHIP):
                for l in range(nl):
                    outs[k][h, l] = refs[pos + l][pl.ds(2 * h + c, 1)][0] + refs[pos + nl][h, l]
            pos += nl + 1
        for k, n in enumerate(ln):
            nl = len(lists[n][0])
            for l in range(nl):
                out = outs[len(on) + k]
                out[l] = (refs[pos + l][...] + refs[pos + nl][l]).astype(out.dtype)
            pos += nl + 1

    shapes = [jax.ShapeDtypeStruct(owned[n][1].shape, F32) for n in on]
    shapes += [jax.ShapeDtypeStruct(lists[n][1].shape, F32 if n == "final_norm" else BF16) for n in ln]
    res = _call(body, name="pair_add_small", out_shape=shapes,
                in_specs=[pl.BlockSpec(memory_space=pltpu.SMEM)] + [pl.BlockSpec(memory_space=pltpu.VMEM)] * len(flat),
                compiler_params=_cparams())(core, *flat)
    return dict(zip(on + ln, res))


def _pair_reduce(tag, big, by_owner, small, core):
    rs = {**big, **by_owner}
    srcs, outs, plans, rcv_at = [], [], [], {}
    for name, arrays in rs.items():
        rcv_at[name] = len(outs)
        outs.append(jax.ShapeDtypeStruct((NCHIP, len(arrays)) + arrays[0].shape[1:], arrays[0].dtype))
        for l, arr in enumerate(arrays):
            srcs.append(arr)
            plans.append((len(srcs) - 1, rcv_at[name], l, True))
    for name, arrays in small.items():
        rcv_at[name] = len(outs)
        outs.append(jax.ShapeDtypeStruct((len(arrays),) + arrays[0].shape, F32))
        for l, arr in enumerate(arrays):
            srcs.append(arr)
            plans.append((len(srcs) - 1, rcv_at[name], l, False))

    def plan_pair(ins, out_refs, x, y, c):
        sib = (x, y, 1 - c)
        recs = []
        for si, ro, l, slabs in plans:
            if slabs:
                four = out_refs[ro].at[pl.ds(0, NCHIP), l]
                recs.append(dict(remote=[(sib, ins[si].at[2 * h + 1 - c], out_refs[ro].at[h, l]) for h in range(NCHIP)],
                                 send_wait=four, recv_wait=four))
            else:
                dst = out_refs[ro].at[l]
                recs.append(dict(remote=[(sib, ins[si], dst)], send_wait=dst, recv_wait=dst))
        return recs

    res = _comm_call("pair_exchange_" + tag, srcs, outs, len(plans), plan_pair)
    part = {name: _pair_add(name, big[name], res[rcv_at[name]], core) for name in big}
    if by_owner or small:
        part.update(_pair_add_small({n: (by_owner[n], res[rcv_at[n]]) for n in by_owner},
                                    {n: (small[n], res[rcv_at[n]]) for n in small}, core))
    return part


def _chip_copies(src, land, slabbed, x, y, c):
    mine = 2 * x + y
    copies = []
    for step in range(1, NCHIP):
        h = (mine + step) % NCHIP
        copies.append(((h // 2, h % 2, c), src.at[h] if slabbed else src, land.at[mine]))
    return copies


def _chip_exchange(part, slabbed, keep_own):
    names = list(part)
    outs = [jax.ShapeDtypeStruct((() if n in slabbed else (NCHIP,)) + part[n].shape, part[n].dtype) for n in names]

    def plan(ins, out_refs, x, y, c):
        mine = 2 * x + y
        recs = []
        for k, n in enumerate(names):
            three = out_refs[k].at[pl.ds(0, NCHIP - 1)]
            rec = dict(remote=_chip_copies(ins[k], out_refs[k], n in slabbed, x, y, c), send_wait=three, recv_wait=three)
            if n in keep_own:
                rec["local"] = [(ins[k].at[mine] if n in slabbed else ins[k], out_refs[k].at[mine])]
                rec["local_wait"] = out_refs[k].at[0]
            recs.append(rec)
        return recs

    res = _comm_call("chip_exchange", [part[n] for n in names], outs, len(names), plan)
    return dict(zip(names, res))


HBM_SPEC = pl.BlockSpec(memory_space=pltpu.HBM)
SEM_SPEC = pl.BlockSpec(memory_space=pltpu.SEMAPHORE)
SPLIT_EFFECT = pltpu.SideEffectType.DATAFLOW_SIDE_EFFECTING


def _split_start(name, srcs, land_shapes, copies_fn, after):
    ns, n = len(srcs), len(land_shapes)
    lands = [pltpu.with_memory_space_constraint(s if isinstance(s, jax.Array) else lax.empty(s.shape, s.dtype), pltpu.HBM)
             for s in land_shapes]

    def body(*refs):
        src_refs, land_refs = refs[:ns], refs[ns:ns + n]
        send_sem, recv_sem = refs[ns + n + 1], refs[ns + n + 2]
        token = refs[-1]
        x, y, c = _coords()
        for k, copies in enumerate(copies_fn(src_refs, land_refs, x, y, c)):
            for peer, src, dst in copies:
                _remote(src, dst, send_sem.at[k], recv_sem.at[k], peer).start()
        token[...] = jnp.zeros_like(token)

    res = pl.pallas_call(
        body, name=name,
        out_shape=(pltpu.SemaphoreType.DMA((n,)), pltpu.SemaphoreType.DMA((n,)),
                   *[pltpu.HBM(s.shape, s.dtype) for s in land_shapes], jax.ShapeDtypeStruct((8, 128), F32)),
        in_specs=[HBM_SPEC] * (ns + n) + [ANY],
        out_specs=(SEM_SPEC, SEM_SPEC, *[HBM_SPEC] * n, pl.BlockSpec(memory_space=pltpu.VMEM)),
        input_output_aliases={ns + i: 2 + i for i in range(n)},
        compiler_params=pltpu.CompilerParams(has_side_effects=SPLIT_EFFECT),
    )(*[pltpu.with_memory_space_constraint(s, pltpu.HBM) for s in srcs], *lands, after)
    return dict(send=res[0], recv=res[1], srcs=list(srcs), lands=list(res[2:2 + n]), token=res[-1])


def _split_wait(name, handle, wait_views, after):
    n = len(handle["lands"])
    after = after if isinstance(after, (tuple, list)) else (after,)

    def body(*refs):
        land_refs = refs[:n]
        send_sem, recv_sem = refs[n], refs[n + 1]
        x, y, c = _coords()
        for k in range(n):
            w = wait_views[k](land_refs[k])
            cp = _remote(w, w, send_sem.at[k], recv_sem.at[k], (x, y, c))
            cp.wait_send()
            cp.wait_recv()

    res = pl.pallas_call(
        body, name=name,
        out_shape=tuple(pltpu.HBM(s.shape, s.dtype) for s in handle["lands"]),
        in_specs=[HBM_SPEC] * n + [SEM_SPEC, SEM_SPEC] + [ANY] * len(after), out_specs=tuple([HBM_SPEC] * n),
        input_output_aliases={i: i for i in range(n)},
        compiler_params=pltpu.CompilerParams(has_side_effects=SPLIT_EFFECT),
    )(*handle["lands"], handle["send"], handle["recv"], *after)
    return list(res)


def _adamw(w, g, m, v):
    m = ADAM_B1 * m + (1.0 - ADAM_B1) * g
    v = ADAM_B2 * v + (1.0 - ADAM_B2) * jnp.square(g)
    m_hat = m / (1.0 - ADAM_B1 ** ADAM_STEP)
    v_hat = v / (1.0 - ADAM_B2 ** ADAM_STEP)
    delta = -ADAM_LR * (m_hat / (jnp.sqrt(v_hat) + ADAM_EPS) + ADAM_WD * w)
    return delta, m, v


def _chip_start(part, tag):
    names = list(part)

    def copies(src_refs, land_refs, x, y, c):
        return [_chip_copies(src_refs[k], land_refs[k], True, x, y, c) for k in range(len(names))]

    shapes = [jax.ShapeDtypeStruct(part[n].shape, part[n].dtype) for n in names]
    return names, _split_start("chip_start_" + tag, [part[n] for n in names], shapes, copies, part[names[0]])


def _chip_wait(names, handle, after, tag):
    three = [lambda ref: ref.at[pl.ds(0, NCHIP - 1)]] * len(names)
    return dict(zip(names, _split_wait("chip_wait_" + tag, handle, three, after)))


def _sum_senders(ref):
    g = ref[0].astype(F32)
    for h in range(1, NCHIP):
        g = g + ref[h].astype(F32)
    return g


def _adam_big(name, own, recv, w, m, v, tk, chip):
    nl = len(own)
    kk, nn = w.shape[1], w.shape[2]
    nnp = own[0].shape[3]

    def body(chip_ref, *refs):
        l = pl.program_id(0)
        g = None
        for step in range(NCHIP):
            val = refs[step][...]
            for q in range(1, nl):
                val = jnp.where(l == q, refs[NCHIP * q + step][...], val)
            g = val.astype(F32) if g is None else g + val.astype(F32)
        w_ref, m_ref, v_ref, g_ref, d_ref, mo_ref, vo_ref = refs[NCHIP * nl:]
        g = g[:, :nn]
        delta, m2, v2 = _adamw(w_ref[...], g, m_ref[...], v_ref[...])
        g_ref[...] = g
        d_ref[...] = delta
        mo_ref[...] = m2
        vo_ref[...] = v2

    def slab(q, step):
        return pl.BlockSpec((None, None, tk, nnp), lambda l, i, chip_ref: (
            jnp.where(l == q, (chip_ref[0] + step) % NCHIP, 0), 0, jnp.where(l == q, i, 0), 0))

    in_specs, operands = [], []
    for q in range(nl):
        in_specs += [slab(q, step) for step in range(NCHIP)]
        operands += [own[q]] + [recv[q]] * (NCHIP - 1)
    wspec = pl.BlockSpec((None, tk, nn), lambda l, i, chip_ref: (l, i, 0))
    shape = jax.ShapeDtypeStruct(w.shape, F32)
    return _call(
        body, name="adamw_" + name,
        grid_spec=pltpu.PrefetchScalarGridSpec(num_scalar_prefetch=1, grid=(nl, kk // tk),
                                               in_specs=in_specs + [wspec] * 3, out_specs=[wspec] * 4),
        out_shape=[shape] * 4, compiler_params=_cparams(("arbitrary", "arbitrary")))(chip, *operands, w, m, v)


def _adam_small(names, recv, w, m, v):
    n = len(names)

    def body(*refs):
        r, ww, mm, vv = refs[:n], refs[n:2 * n], refs[2 * n:3 * n], refs[3 * n:4 * n]
        outs = refs[4 * n:]
        for k in range(n):
            for l in range(r[k].shape[1]):
                g = r[k][0, l].astype(F32)
                for h in range(1, NCHIP):
                    g = g + r[k][h, l].astype(F32)
                per_layer = ww[k].shape[1:]
                if g.shape[0] > per_layer[0] and g.shape[1:] == per_layer[1:]:
                    g = g[:per_layer[0]]
                at = l if g.shape == per_layer else pl.ds(l, 1)
                delta, m2, v2 = _adamw(ww[k][at], g, mm[k][at], vv[k][at])
                outs[k][at] = g
                outs[n + k][at] = delta
                outs[2 * n + k][at] = m2
                outs[3 * n + k][at] = v2

    shapes = [jax.ShapeDtypeStruct(w[k].shape, F32) for k in names]
    res = _call(body, name="adamw_small", out_shape=shapes * 4, compiler_params=_cparams())(
        *[recv[k] for k in names], *[w[k] for k in names], *[m[k] for k in names], *[v[k] for k in names])
    return {k: (res[i], res[n + i], res[2 * n + i], res[3 * n + i]) for i, k in enumerate(names)}


def _expand_b(bt):
    eye = jnp.eye(GB, dtype=bt.dtype)
    return jnp.einsum("jgpn,gh->jgphn", bt.reshape(NBLK, GB, SGRP, NSTATE), eye).reshape(NBLK, GB * SGRP, NS)


def _extract_b(db):
    x = db.reshape(NBLK, GB, SGRP, GB, NSTATE)
    eye = jnp.eye(GB, dtype=db.dtype)
    return jnp.einsum("jgphn,gh->jgpn", x, eye).reshape(NGRP, SGRP, NSTATE)


def _expand_c(c):
    ct = jnp.transpose(c, (0, 2, 1)).reshape(NBLK, GB, NSTATE, SGRP)
    eye = jnp.eye(GB, dtype=c.dtype)
    return jnp.einsum("jgnp,gh->jgnhp", ct, eye).reshape(NBLK, NS, GB * SGRP)


def _extract_c(dc):
    x = dc.reshape(NBLK, GB, NSTATE, GB, SGRP)
    eye = jnp.eye(GB, dtype=dc.dtype)
    d = jnp.einsum("jgnhp,gh->jgnp", x, eye).reshape(NGRP, NSTATE, SGRP)
    return jnp.transpose(d, (0, 2, 1))


_SMALL = ("norm1", "b_gate", "ssm_a_re", "ssm_a_im", "ssm_log_dt", "ssm_b_re", "ssm_b_im", "ssm_c_re", "ssm_c_im",
          "ssm_d", "ssm_b_glu", "conv_b_dw", "conv_ln_g", "conv_ln_b", "pool_w_group", "pool_scale", "norm2")
_ADAM_TK = {"w_in": 256, "ssm_w_glu": 64, "ssm_w_proj": 512, "conv_w_proj": 512, "pool_w_proj": 512, "w_out": 128,
            "ffn_w_gate": HSH, "ffn_w_up": HSH, "ffn_w_down": HSH}
_OUT_ORDER = ("norm1", "w_in", "b_gate", "ssm_a_re", "ssm_a_im", "ssm_log_dt", "ssm_b_re", "ssm_b_im", "ssm_c_re",
              "ssm_c_im", "ssm_d", "ssm_w_glu", "ssm_b_glu", "ssm_w_proj", "conv_w_dw", "conv_b_dw", "conv_ln_g",
              "conv_ln_b", "conv_w_proj", "pool_w_group", "pool_scale", "pool_w_proj", "w_out", "norm2", "ffn_w_gate",
              "ffn_w_up", "ffn_w_down", "final_norm")


def _layer_fwd(x, p, token, late_params=None):
    z = _inproj_fwd(x, p["norm1"], p["w_in"], token)
    yssm, hre, him = _ssm_fwd(z, p)
    cv = _conv_fwd(z, p["conv_w"], p["conv_b"])
    pbar = _pool_fwd(z)
    if late_params is not None:
        more, token = late_params((yssm, cv, pbar))
        p = {**p, **more}
    x1 = _merge_fwd(x, yssm, cv, pbar, z, p, token)
    x2, gpre, upre = _ffn_fwd(x1, p["norm2"], p["wg"], p["wu"], p["wd"])
    return x2, dict(x=x, z=z, yssm=yssm, hre=hre, him=him, cv=cv, pbar=pbar, x1=x1, gpre=gpre, upre=upre), p


def _layer_bwd(dx, p, s, token, after_ffn=None):
    big, small = {}, {}
    dx1, d_norm2, dgp, dup, act, h2 = _ffn_bwd(dx, s["x1"], p["norm2"], s["gpre"], s["upre"], p["wg"], p["wu"], p["wd"],
                                               token)
    big["ffn_w_gate"] = _matmul_tn(dgp, h2, "tn_gate").reshape(NDEV, HPAD, D)
    big["ffn_w_up"] = _matmul_tn(dup, h2, "tn_up").reshape(NDEV, HPAD, D)
    big["ffn_w_down"] = _matmul_tn(act, dx, "tn_down").reshape(NDEV, HPAD, D)
    (dy, dcv, dpb, dzg, a_g, a_outa, a_hs, a_pb, a_pc, a_mg, c_glu, c_ya, c_yb, c_p, c_yc,
     d_bglu, d_lng, d_lnb, d_scale, d_bgate) = _merge_bwd(dx1, s["x"], s["yssm"], s["cv"], s["pbar"], s["z"], p,
                                                          after_ffn(dict(big)) if after_ffn else token)
    big["ssm_w_glu"] = _matmul_tn(a_g, c_glu, "tn_glu").reshape(NDEV, BW // NDEV, BW)
    big["ssm_w_proj"] = _matmul_tn(a_outa, c_ya, "tn_ssm_proj", D // NDEV)
    big["conv_w_proj"] = _matmul_tn(a_hs, c_yb, "tn_conv_proj", D // NDEV)
    big["pool_w_proj"] = _matmul_tn(a_pc, c_yc, "tn_pool_proj", D // NDEV)
    big["w_out"] = _matmul_tn(a_mg, dx1, "tn_out").reshape(NDEV, D // NDEV, D)
    d_wgrp = _group_tn(a_pb, c_p)
    du_a, dbr, dbi, dcr, dci, dd, dar, dai, dldt = _ssm_bwd(dy, s["z"], s["hre"], s["him"], p)
    dva, dvb, dw8, dcb = _conv_bwd(dcv, s["z"], p["conv_w"])
    du_c = _pool_bwd(dpb)
    dz = jnp.concatenate([du_a, dva, dvb, du_c, dzg], axis=1)
    dx0, d_norm1, h = _inproj_bwd(dz, dx1, s["x"], p["norm1"], p["w_in"])
    big["w_in"] = _matmul_tn(h, dz, "tn_in", IN_W // NDEV)
    small["norm1"] = d_norm1
    small["b_gate"] = d_bgate
    small["ssm_a_re"] = dar.reshape(NGRP, NSTATE)
    small["ssm_a_im"] = dai.reshape(NGRP, NSTATE)
    small["ssm_log_dt"] = dldt.reshape(NBLK, 8, 128)[:, 0, :GB].reshape(1, NGRP)
    small["ssm_b_re"] = _extract_b(dbr)
    small["ssm_b_im"] = _extract_b(dbi)
    small["ssm_c_re"] = _extract_c(dcr)
    small["ssm_c_im"] = _extract_c(dci)
    small["ssm_d"] = dd.reshape(NGRP, SGRP)
    small["ssm_b_glu"] = d_bglu
    small["conv_b_dw"] = dcb
    small["conv_ln_g"] = d_lng
    small["conv_ln_b"] = d_lnb
    small["pool_w_group"] = d_wgrp
    small["pool_scale"] = d_scale
    small["norm2"] = d_norm2
    return dx0, big, dw8, small


def _train_step(a):
    t_rows = a["x"].shape[1]
    x0 = a["x"].reshape(t_rows, D)
    target = a["loss_target"].reshape(t_rows, D)

    tr = lambda w: jnp.transpose(w, (0, 2, 1))
    weights = {name: (tr(a[name]) if name in ("ffn_w_gate", "ffn_w_up") else a[name]) for name in _BIG}
    core = lax.axis_index("c").astype(jnp.int32).reshape(1)
    chip = (2 * lax.axis_index("x") + lax.axis_index("y")).astype(jnp.int32).reshape(1)
    me = 2 * chip + core
    no_token = jnp.zeros((8, 128), F32)
    row = lambda v: v.reshape(1, -1)
    rest = GROUP_MIX + GROUP_FFN
    first, dw_all = _gather_weights(_place_shards(weights, GROUP_IN, 0, me),
                                    a["conv_w_dw"].reshape(DEPTH, CONV_K, BW // NDEV), GROUP_IN)
    conv_w = jnp.transpose(dw_all, (1, 2, 0, 3)).reshape(DEPTH, CONV_K, BW)
    go_rest0 = _gather_start(_place_shards(weights, rest, 0, me), rest, dw_all, "rest0")
    going = {}

    def early_params(l, w_in):
        return dict(
            norm1=row(a["norm1"][l]), w_in=w_in,
            are=row(a["ssm_a_re"][l]), aim=row(a["ssm_a_im"][l]),
            ldt=row(jnp.repeat(a["ssm_log_dt"][l], NSTATE)),
            bexp_re=_expand_b(jnp.transpose(a["ssm_b_re"][l], (0, 2, 1))),
            bexp_im=_expand_b(jnp.transpose(a["ssm_b_im"][l], (0, 2, 1))),
            cexp_re=_expand_c(a["ssm_c_re"][l]), cexp_im=_expand_c(a["ssm_c_im"][l]),
            dskip=row(a["ssm_d"][l]), conv_w=conv_w[l], conv_b=row(a["conv_b_dw"][l]))

    def late_params(l, handle, tag, then_start):
        def get(after):
            full = _gather_finish(handle, rest, after, tag)
            token = then_start(full["ffn_w_down"]) if then_start else no_token
            return dict(
                wglu=full["ssm_w_glu"], bglu=row(a["ssm_b_glu"][l]), wpa=full["ssm_w_proj"],
                lng=row(a["conv_ln_g"][l]), lnb=row(a["conv_ln_b"][l]), wpb=full["conv_w_proj"],
                wgrp=a["pool_w_group"][l].astype(BF16), scale=row(a["pool_scale"][l]), wpc=full["pool_w_proj"],
                bgate=row(a["b_gate"][l]), wout=full["w_out"],
                norm2=row(a["norm2"][l]), wg=full["ffn_w_gate"], wu=full["ffn_w_up"], wd=full["ffn_w_down"]), token
        return get

    def start_in1(after):
        going["in1"] = _gather_start(_place_shards(weights, GROUP_IN, 1, me), GROUP_IN, after, "in1")
        return going["in1"]["token"]

    x, s0, p0 = _layer_fwd(x0, early_params(0, first["w_in"]), go_rest0["token"], late_params(0, go_rest0, "rest0", start_in1))
    w_in1 = _gather_finish(going["in1"], GROUP_IN, x, "in1")["w_in"]
    go_rest1 = _gather_start(_place_shards(weights, rest, 1, me), rest, w_in1, "rest1")
    x, s1, p1 = _layer_fwd(x, early_params(1, w_in1), go_rest1["token"], late_params(1, go_rest1, "rest1", None))
    params, saved = [p0, p1], [s0, s1]

    loss_part, dx, d_final = _loss_head(x, a["final_norm"].reshape(1, D), target)
    loss = lax.psum(loss_part[0, 0], ("x", "y", "c"))

    dx, gb1, go1, gs1 = _layer_bwd(dx, params[1], saved[1], no_token)
    part1 = _pair_reduce("late", {n: [g] for n, g in gb1.items()}, {}, {}, core)
    rs_names, rs_handle = _chip_start(part1, "late")
    early = {}

    def ffn_grads_leave_early(ffn):
        early["part"] = _pair_reduce("ffn", {n: [g] for n, g in ffn.items()}, {}, {}, core)
        early["names"], early["handle"] = _chip_start(early["part"], "ffn")
        return early["handle"]["token"]

    dx, gb0, go0, gs0 = _layer_bwd(dx, params[0], saved[0], rs_handle["token"], ffn_grads_leave_early)
    grad_x = dx.reshape(1, t_rows, D)
    small = {n: [gs0[n], gs1[n]] for n in _SMALL}
    small["final_norm"] = [d_final]
    part0 = _pair_reduce("rest", {n: [g] for n, g in gb0.items() if n not in early["part"]}, {"conv_w_dw": [go0, go1]},
                         small, core)
    recv = _chip_exchange(part0, set(_BIG) | {"conv_w_dw"}, {"conv_w_dw"} | set(small))
    recv1 = _chip_wait(rs_names, rs_handle, dx, "late")
    recv.update(_chip_wait(early["names"], early["handle"], dx, "ffn"))
    part0.update(early["part"])

    results = {}
    for name in _BIG:
        fix = tr if name in ("ffn_w_gate", "ffn_w_up") else (lambda t: t)
        res = _adam_big(name, [part0[name], part1[name]], [recv[name], recv1[name]], fix(a[name]), fix(a["m_" + name]),
                        fix(a["v_" + name]), _ADAM_TK[name], chip)
        results[name] = tuple(fix(r) for r in res)

    lay = {
        "ssm_b_re": lambda v: jnp.transpose(v, (0, 1, 3, 2)), "ssm_b_im": lambda v: jnp.transpose(v, (0, 1, 3, 2)),
        "conv_w_dw": lambda v: v.reshape(DEPTH, CONV_K, BW // NDEV), "final_norm": lambda v: v.reshape(1, 1, D),
    }
    names = _SMALL + ("conv_w_dw", "final_norm")
    relay = lambda k, v: lay[k](v) if k in lay else v
    sm = _adam_small(names, recv, {k: relay(k, a[k]) for k in names}, {k: relay(k, a["m_" + k]) for k in names},
                     {k: relay(k, a["v_" + k]) for k in names})
    for k in names:
        back = (lambda r: jnp.transpose(r, (0, 1, 3, 2))) if k in ("ssm_b_re", "ssm_b_im") else (lambda r: r.reshape(a[k].shape))
        results[k] = tuple(back(r) for r in sm[k])

    outs = [loss, grad_x]
    for part in range(4):
        outs += [results[k][part] for k in _OUT_ORDER]
    return tuple(outs)


def kernel(x, norm1, w_in, b_gate, ssm_a_re, ssm_a_im, ssm_log_dt, ssm_b_re, ssm_b_im, ssm_c_re, ssm_c_im, ssm_d, ssm_w_glu, ssm_b_glu, ssm_w_proj, conv_w_dw, conv_b_dw, conv_ln_g, conv_ln_b, conv_w_proj, pool_w_group, pool_scale, pool_w_proj, w_out, norm2, ffn_w_gate, ffn_w_up, ffn_w_down, final_norm, loss_target, m_norm1, m_w_in, m_b_gate, m_ssm_a_re, m_ssm_a_im, m_ssm_log_dt, m_ssm_b_re, m_ssm_b_im, m_ssm_c_re, m_ssm_c_im, m_ssm_d, m_ssm_w_glu, m_ssm_b_glu, m_ssm_w_proj, m_conv_w_dw, m_conv_b_dw, m_conv_ln_g, m_conv_ln_b, m_conv_w_proj, m_pool_w_group, m_pool_scale, m_pool_w_proj, m_w_out, m_norm2, m_ffn_w_gate, m_ffn_w_up, m_ffn_w_down, m_final_norm, v_norm1, v_w_in, v_b_gate, v_ssm_a_re, v_ssm_a_im, v_ssm_log_dt, v_ssm_b_re, v_ssm_b_im, v_ssm_c_re, v_ssm_c_im, v_ssm_d, v_ssm_w_glu, v_ssm_b_glu, v_ssm_w_proj, v_conv_w_dw, v_conv_b_dw, v_conv_ln_g, v_conv_ln_b, v_conv_w_proj, v_pool_w_group, v_pool_scale, v_pool_w_proj, v_w_out, v_norm2, v_ffn_w_gate, v_ffn_w_up, v_ffn_w_down, v_final_norm):
    return _train_step(dict(locals()))
```

```python
import functools

import jax
import jax.numpy as jnp
from jax import lax
from jax.experimental import pallas as pl
from jax.experimental.pallas import tpu as pltpu

F32 = jnp.float32
BF16 = jnp.bfloat16

NDEV = 8
DEPTH = 2
D = 1024
BW = 512
NSTATE = 64
SGRP = 16
NGRP = BW // SGRP
GB = 8
NBLK = NGRP // GB
NS = GB * NSTATE
CONV_K = 31
HALO = 32
PHALO = 16
IN_W = 5120
HID = 2816
HSH = HID // NDEV
HPAD = 384
HIDP = HPAD * NDEV
EPS = 1e-6
VMEM_LIMIT = 56 * 1024 * 1024

ADAM_LR, ADAM_B1, ADAM_B2, ADAM_EPS, ADAM_WD, ADAM_STEP = 0.001, 0.9, 0.999, 1e-08, 0.01, 10

MESH = pl.DeviceIdType.MESH
ANY = pl.BlockSpec(memory_space=pl.ANY)


def _call(body, **kw):
    return pl.pallas_call(body, **kw)


def _cparams(sem=None):
    return pltpu.CompilerParams(dimension_semantics=sem, vmem_limit_bytes=VMEM_LIMIT)


def _dot(a, b):
    return jnp.dot(a.astype(BF16), b.astype(BF16), preferred_element_type=F32)


def _dot_nt(a, b):
    return lax.dot_general(a.astype(BF16), b.astype(BF16), (((1,), (1,)), ((), ())), preferred_element_type=F32)


def _dot_tn(a, b):
    return lax.dot_general(a.astype(BF16), b.astype(BF16), (((0,), (0,)), ((), ())), preferred_element_type=F32)


@jax.custom_vjp
def _mm(a, w):
    return _dot(a, w)


def _mm_fwd(a, w):
    return _dot(a, w), w


def _mm_bwd(w, ct):
    return _dot_nt(ct, w), jnp.zeros_like(w)


_mm.defvjp(_mm_fwd, _mm_bwd)


def _rms(x, g):
    return x * lax.rsqrt(jnp.mean(x * x, axis=-1, keepdims=True) + EPS) * g


def _disc(are, aim, ldt):
    dt = jnp.exp(ldt)
    mag = jnp.exp(dt * are)
    ang = dt * aim
    abr = mag * jnp.cos(ang)
    abi = mag * jnp.sin(ang)
    den = are * are + aim * aim
    nr = abr - 1.0
    fr = (nr * are + abi * aim) / den
    fi = (abi * are - nr * aim) / den
    return abr, abi, fr, fi


def _bbar(fr, fi, br, bi):
    return fr * br - fi * bi, fr * bi + fi * br


def _cmul(ar, ai, br, bi):
    return ar * br - ai * bi, ar * bi + ai * br


def _scan_rows(re_ref, im_ref, ar, ai, n_rows, reverse, hre_ref=None, him_ref=None):
    n = ar.shape[1]
    shape = (8, n)
    rows = lax.broadcasted_iota(jnp.int32, shape, 0)
    a1 = (jnp.broadcast_to(ar, shape), jnp.broadcast_to(ai, shape))
    a2 = _cmul(*a1, *a1)
    a4 = _cmul(*a2, *a2)
    pr = jnp.zeros(shape, F32)
    pi = jnp.zeros(shape, F32)
    pw = a1
    for k in range(8):
        sel = rows == ((7 - k) if reverse else k)
        pr = jnp.where(sel, pw[0], pr)
        pi = jnp.where(sel, pw[1], pi)
        pw = _cmul(*pw, *a1)
    nt = n_rows // 8
    with_acc = hre_ref is not None

    def body(i, carry):
        cr, ci = carry[0], carry[1]
        t = (nt - 1 - i) if reverse else i
        off = pl.multiple_of(t * 8, 8)
        xr = re_ref[pl.ds(off, 8), :]
        xi = im_ref[pl.ds(off, 8), :]
        for k, (kr, ki) in ((1, a1), (2, a2), (4, a4)):
            if reverse:
                keep, sh = rows < 8 - k, 8 - k
            else:
                keep, sh = rows >= k, k
            sr = jnp.where(keep, pltpu.roll(xr, sh, 0), 0.0)
            si = jnp.where(keep, pltpu.roll(xi, sh, 0), 0.0)
            xr, xi = xr + kr * sr - ki * si, xi + kr * si + ki * sr
        xr, xi = xr + pr * cr - pi * ci, xi + pr * ci + pi * cr
        re_ref[pl.ds(off, 8), :] = xr
        im_ref[pl.ds(off, 8), :] = xi
        edge = 0 if reverse else 7
        out = (jnp.broadcast_to(xr[edge:edge + 1, :], shape), jnp.broadcast_to(xi[edge:edge + 1, :], shape))
        if with_acc:
            hr = hre_ref[pl.ds(off, 8), :]
            hi = him_ref[pl.ds(off, 8), :]
            offp = pl.multiple_of(jnp.maximum(t - 1, 0) * 8, 8)
            live = jnp.where(t > 0, 1.0, 0.0)
            lr = jnp.broadcast_to(hre_ref[pl.ds(offp, 8), :][7:8, :], shape) * live
            li = jnp.broadcast_to(him_ref[pl.ds(offp, 8), :][7:8, :], shape) * live
            hpr = jnp.where(rows == 0, lr, pltpu.roll(hr, 1, 0))
            hpi = jnp.where(rows == 0, li, pltpu.roll(hi, 1, 0))
            out = out + (carry[2] + xr * hpr + xi * hpi, carry[3] + xi * hpr - xr * hpi)
        return out

    zero = jnp.zeros(shape, F32)
    init = (zero, zero, zero, zero) if with_acc else (zero, zero)
    res = lax.fori_loop(0, nt, body, init)
    return res[2:] if with_acc else None


TOKEN_SPEC = pl.BlockSpec((8, 128), lambda i, j: (0, 0))


def _inproj_fwd(x, gamma, w, token, tm=1024, nb=2):
    t_rows = x.shape[0]
    tm = min(tm, t_rows)
    oc = w.shape[2]
    n = NDEV * oc
    tn = nb * oc

    def body(x_ref, g_ref, w_ref, token_ref, z_ref, h_ref):
        @pl.when(pl.program_id(1) == 0)
        def _():
            h_ref[...] = _rms(x_ref[...], g_ref[...]).astype(BF16)
        for q in range(nb):
            z_ref[:, oc * q:oc * (q + 1)] = jnp.dot(h_ref[...], w_ref[q], preferred_element_type=F32)

    return _call(
        body, name="inproj_fwd", grid=(t_rows // tm, n // tn),
        in_specs=[pl.BlockSpec((tm, D), lambda i, j: (i, 0)), pl.BlockSpec((1, D), lambda i, j: (0, 0)),
                  pl.BlockSpec((nb, D, oc), lambda i, j: (j, 0, 0)), TOKEN_SPEC],
        out_specs=[pl.BlockSpec((tm, tn), lambda i, j: (i, j)), pl.BlockSpec((tm, D), lambda i, j: (i, 0))],
        out_shape=[jax.ShapeDtypeStruct((t_rows, n), F32), jax.ShapeDtypeStruct((t_rows, D), BF16)],
        compiler_params=_cparams(("parallel", "arbitrary")))(x, gamma, w, token)


def _ssm_specs(t_rows):
    row = pl.BlockSpec((1, NS), lambda j: (0, j))
    return dict(
        u=pl.BlockSpec((t_rows, GB * SGRP), lambda j: (0, j)),
        row=row,
        bexp=pl.BlockSpec((None, GB * SGRP, NS), lambda j: (j, 0, 0)),
        cexp=pl.BlockSpec((None, NS, GB * SGRP), lambda j: (j, 0, 0)),
        d=pl.BlockSpec((1, GB * SGRP), lambda j: (0, j)),
        h=pl.BlockSpec((t_rows, NS), lambda j: (0, j)),
    )


def _ssm_fwd(z, p):
    t_rows = z.shape[0]
    s = _ssm_specs(t_rows)

    def body(u_ref, are_ref, aim_ref, ldt_ref, br_ref, bi_ref, cr_ref, ci_ref, d_ref, y_ref, hr_ref, hi_ref):
        abr, abi, fr, fi = _disc(are_ref[...], aim_ref[...], ldt_ref[...])
        bbr, bbi = _bbar(fr, fi, br_ref[...], bi_ref[...])
        u = u_ref[...]
        hr_ref[...] = _dot(u, bbr)
        hi_ref[...] = _dot(u, bbi)
        _scan_rows(hr_ref, hi_ref, abr, abi, t_rows, False)
        y_ref[...] = _dot(hr_ref[...], cr_ref[...]) - _dot(hi_ref[...], ci_ref[...]) + d_ref[...] * u

    return _call(
        body, name="ssm_fwd", grid=(NBLK,),
        in_specs=[s["u"], s["row"], s["row"], s["row"], s["bexp"], s["bexp"], s["cexp"], s["cexp"], s["d"]],
        out_specs=[s["u"], s["h"], s["h"]],
        out_shape=[jax.ShapeDtypeStruct((t_rows, BW), F32), jax.ShapeDtypeStruct((t_rows, NGRP * NSTATE), F32),
                   jax.ShapeDtypeStruct((t_rows, NGRP * NSTATE), F32)],
        compiler_params=_cparams(("parallel",)))(
            z, p["are"], p["aim"], p["ldt"], p["bexp_re"], p["bexp_im"], p["cexp_re"], p["cexp_im"], p["dskip"])


def _ssm_bwd(dy, z, hre, him, p, token):
    t_rows = z.shape[0]
    s = _ssm_specs(t_rows)
    nstates = NGRP * NSTATE

    def body(dy_ref, u_ref, hr_ref, hi_ref, are_ref, aim_ref, ldt_ref, br_ref, bi_ref, cr_ref, ci_ref, d_ref, token_ref,
             du_ref, dbr_ref, dbi_ref, dcr_ref, dci_ref, dd_ref, dar_ref, dai_ref, dldt_ref, lr_ref, li_ref):
        rows3 = (are_ref[...], aim_ref[...], ldt_ref[...])
        (abr, abi, fr, fi), disc_vjp = jax.vjp(_disc, *rows3)
        (bbr, bbi), bbar_vjp = jax.vjp(_bbar, fr, fi, br_ref[...], bi_ref[...])
        dy = dy_ref[...]
        u = u_ref[...]
        lr_ref[...] = _dot_nt(dy, cr_ref[...])
        li_ref[...] = -_dot_nt(dy, ci_ref[...])
        dcr_ref[...] = _dot_tn(hr_ref[...], dy)
        dci_ref[...] = -_dot_tn(hi_ref[...], dy)
        dd_ref[...] = jnp.sum(dy * u, axis=0, keepdims=True)
        acc_r, acc_i = _scan_rows(lr_ref, li_ref, abr, -abi, t_rows, True, hr_ref, hi_ref)
        dabr = jnp.sum(acc_r, axis=0, keepdims=True)
        dabi = jnp.sum(acc_i, axis=0, keepdims=True)
        lam_r = lr_ref[...]
        lam_i = li_ref[...]
        du = d_ref[...] * dy + _dot_nt(lam_r, bbr) + _dot_nt(lam_i, bbi)
        du_ref[...] = du.astype(BF16)
        dbbr = _dot_tn(u, lam_r)
        dbbi = _dot_tn(u, lam_i)
        dfr, dfi, dbr, dbi = bbar_vjp((dbbr, dbbi))
        dbr_ref[...] = dbr
        dbi_ref[...] = dbi
        dar, dai, dldt = disc_vjp((dabr, dabi, dfr, dfi))
        dar_ref[...] = dar
        dai_ref[...] = dai
        lane_grp = lax.broadcasted_iota(jnp.int32, (NS, 128), 0) // NSTATE
        col = lax.broadcasted_iota(jnp.int32, (NS, 128), 1)
        seg = jnp.where(lane_grp == col, 1.0, 0.0).astype(F32)
        dldt_ref[...] = jnp.dot(jnp.broadcast_to(dldt, (8, NS)), seg, preferred_element_type=F32,
                                precision=lax.Precision.HIGHEST)

    dyspec = pl.BlockSpec((t_rows, GB * SGRP), lambda j: (0, j))
    return _call(
        body, name="ssm_bwd", grid=(NBLK,),
        in_specs=[dyspec, s["u"], s["h"], s["h"], s["row"], s["row"], s["row"], s["bexp"], s["bexp"], s["cexp"],
                  s["cexp"], s["d"], pl.BlockSpec((8, 128), lambda j: (0, 0))],
        out_specs=[dyspec, s["bexp"], s["bexp"], s["cexp"], s["cexp"], s["d"], s["row"], s["row"],
                   pl.BlockSpec((8, 128), lambda j: (j, 0))],
        out_shape=[jax.ShapeDtypeStruct((t_rows, BW), BF16),
                   jax.ShapeDtypeStruct((NBLK, GB * SGRP, NS), F32), jax.ShapeDtypeStruct((NBLK, GB * SGRP, NS), F32),
                   jax.ShapeDtypeStruct((NBLK, NS, GB * SGRP), F32), jax.ShapeDtypeStruct((NBLK, NS, GB * SGRP), F32),
                   jax.ShapeDtypeStruct((1, BW), F32), jax.ShapeDtypeStruct((1, nstates), F32),
                   jax.ShapeDtypeStruct((1, nstates), F32), jax.ShapeDtypeStruct((NBLK * 8, 128), F32)],
        scratch_shapes=[pltpu.VMEM((t_rows, NS), F32), pltpu.VMEM((t_rows, NS), F32)],
        compiler_params=_cparams(("parallel",)))(
            dy, z, hre, him, p["are"], p["aim"], p["ldt"], p["bexp_re"], p["bexp_im"], p["cexp_re"], p["cexp_im"],
            p["dskip"], token)


def _conv_fwd(z, w, b, tm=256):
    t_rows = z.shape[0]
    hb = tm // HALO

    def body(va_ref, vb_ref, ha_ref, hb_ref, w_ref, b_ref, o_ref, win_ref):
        live = jnp.where(pl.program_id(0) > 0, 1.0, 0.0)
        win_ref[0:HALO, :] = ha_ref[...] * jax.nn.sigmoid(hb_ref[...]) * live
        win_ref[HALO:HALO + tm, :] = va_ref[...] * jax.nn.sigmoid(vb_ref[...])
        acc = jnp.broadcast_to(b_ref[...], (tm, BW))
        for k in range(CONV_K):
            acc = acc + w_ref[k:k + 1, :] * win_ref[pl.ds(HALO - (CONV_K - 1) + k, tm), :]
        o_ref[...] = acc

    halo = lambda col: pl.BlockSpec((HALO, BW), lambda i: (jnp.maximum(i * hb - 1, 0), col))
    return _call(
        body, name="conv_fwd", grid=(t_rows // tm,),
        in_specs=[pl.BlockSpec((tm, BW), lambda i: (i, 1)), pl.BlockSpec((tm, BW), lambda i: (i, 2)), halo(1), halo(2),
                  pl.BlockSpec((CONV_K, BW), lambda i: (0, 0)), pl.BlockSpec((1, BW), lambda i: (0, 0))],
        out_specs=pl.BlockSpec((tm, BW), lambda i: (i, 0)),
        out_shape=jax.ShapeDtypeStruct((t_rows, BW), F32),
        scratch_shapes=[pltpu.VMEM((HALO + tm, BW), F32)],
        compiler_params=_cparams(("parallel",)))(z, z, z, z, w, b)


def _conv_bwd(dcv, z, w, tm=256):
    t_rows = z.shape[0]
    nt = t_rows // tm
    hb = tm // HALO
    csh = BW // NDEV

    def body(d_ref, dn_ref, va_ref, vb_ref, ha_ref, hb_ref, w_ref, dva_ref, dvb_ref, dw8_ref, db_ref,
             hwin_ref, dwin_ref, dw_ref):
        i = pl.program_id(0)

        @pl.when(i == 0)
        def _():
            dw_ref[...] = jnp.zeros_like(dw_ref)
            db_ref[...] = jnp.zeros_like(db_ref)

        live_prev = jnp.where(i > 0, 1.0, 0.0)
        live_next = jnp.where(i < nt - 1, 1.0, 0.0)
        va = va_ref[...]
        sig = jax.nn.sigmoid(vb_ref[...])
        hwin_ref[0:HALO, :] = ha_ref[...] * jax.nn.sigmoid(hb_ref[...]) * live_prev
        hwin_ref[HALO:HALO + tm, :] = va * sig
        d = d_ref[...]
        dwin_ref[0:tm, :] = d
        dwin_ref[tm:tm + HALO, :] = dn_ref[...] * live_next
        dh = jnp.zeros((tm, BW), F32)
        dws = []
        for k in range(CONV_K):
            dh = dh + w_ref[k:k + 1, :] * dwin_ref[pl.ds(CONV_K - 1 - k, tm), :]
            dws.append(jnp.sum(d * hwin_ref[pl.ds(HALO - (CONV_K - 1) + k, tm), :], axis=0, keepdims=True))
        dws.append(jnp.zeros((1, BW), F32))
        dw_ref[...] += jnp.concatenate(dws, axis=0)
        db_ref[...] += jnp.sum(d, axis=0, keepdims=True)
        dva_ref[...] = (dh * sig).astype(BF16)
        dvb_ref[...] = (dh * va * sig * (1.0 - sig)).astype(BF16)

        @pl.when(i == nt - 1)
        def _():
            acc = dw_ref[...]
            for q in range(NDEV):
                dw8_ref[q] = acc[:, csh * q:csh * (q + 1)]

    halo = lambda col: pl.BlockSpec((HALO, BW), lambda i: (jnp.maximum(i * hb - 1, 0), col))
    return _call(
        body, name="conv_bwd", grid=(nt,),
        in_specs=[pl.BlockSpec((tm, BW), lambda i: (i, 0)),
                  pl.BlockSpec((HALO, BW), lambda i: (jnp.minimum((i + 1) * hb, t_rows // HALO - 1), 0)),
                  pl.BlockSpec((tm, BW), lambda i: (i, 1)), pl.BlockSpec((tm, BW), lambda i: (i, 2)), halo(1), halo(2),
                  pl.BlockSpec((CONV_K, BW), lambda i: (0, 0))],
        out_specs=[pl.BlockSpec((tm, BW), lambda i: (i, 0)), pl.BlockSpec((tm, BW), lambda i: (i, 0)),
                   pl.BlockSpec((NDEV, 32, csh), lambda i: (0, 0, 0)), pl.BlockSpec((1, BW), lambda i: (0, 0))],
        out_shape=[jax.ShapeDtypeStruct((t_rows, BW), BF16), jax.ShapeDtypeStruct((t_rows, BW), BF16),
                   jax.ShapeDtypeStruct((NDEV, 32, csh), F32), jax.ShapeDtypeStruct((1, BW), F32)],
        scratch_shapes=[pltpu.VMEM((HALO + tm, BW), F32), pltpu.VMEM((tm + HALO, BW), F32), pltpu.VMEM((32, BW), F32)],
        compiler_params=_cparams(("arbitrary",)))(dcv, dcv, z, z, z, z, w)


def _pool_rows(i, tm, n_rows, first_row):
    grp = lax.broadcasted_iota(jnp.int32, (1, BW), 1) // (BW // 4)
    wlen = jnp.where(grp == 0, 2.0, jnp.where(grp == 1, 4.0, jnp.where(grp == 2, 8.0, 16.0)))
    t = (i * tm + first_row + lax.broadcasted_iota(jnp.int32, (n_rows, 1), 0)).astype(F32)
    return grp, 1.0 / jnp.minimum(t + 1.0, wlen)


def _pool_pick(grp, s2, s4, s8, s16):
    return jnp.where(grp == 0, s2, jnp.where(grp == 1, s4, jnp.where(grp == 2, s8, s16)))


def _pool_fwd(z, tm=256):
    t_rows = z.shape[0]
    hb = tm // PHALO

    def body(u_ref, h_ref, o_ref):
        i = pl.program_id(0)
        u = u_ref[...]
        win = jnp.concatenate([h_ref[...] * jnp.where(i > 0, 1.0, 0.0), u], axis=0)
        s2 = win + pltpu.roll(win, 1, 0)
        s4 = s2 + pltpu.roll(s2, 2, 0)
        s8 = s4 + pltpu.roll(s4, 4, 0)
        s16 = s8 + pltpu.roll(s8, 8, 0)
        grp, inv = _pool_rows(i, tm, tm, 0)
        o_ref[...] = _pool_pick(grp, s2, s4, s8, s16)[PHALO:, :] * inv - u

    return _call(
        body, name="pool_fwd", grid=(t_rows // tm,),
        in_specs=[pl.BlockSpec((tm, BW), lambda i: (i, 3)),
                  pl.BlockSpec((PHALO, BW), lambda i: (jnp.maximum(i * hb - 1, 0), 3))],
        out_specs=pl.BlockSpec((tm, BW), lambda i: (i, 0)),
        out_shape=jax.ShapeDtypeStruct((t_rows, BW), F32),
        compiler_params=_cparams(("parallel",)))(z, z)


def _pool_bwd(dp, tm=256):
    t_rows = dp.shape[0]
    nt = t_rows // tm
    hb = tm // PHALO
    ln = tm + PHALO

    def body(d_ref, dn_ref, o_ref):
        i = pl.program_id(0)
        d = d_ref[...]
        grp, inv = _pool_rows(i, tm, ln, 0)
        win = jnp.concatenate([d, dn_ref[...] * jnp.where(i < nt - 1, 1.0, 0.0)], axis=0) * inv
        s2 = win + pltpu.roll(win, ln - 1, 0)
        s4 = s2 + pltpu.roll(s2, ln - 2, 0)
        s8 = s4 + pltpu.roll(s4, ln - 4, 0)
        s16 = s8 + pltpu.roll(s8, ln - 8, 0)
        o_ref[...] = (_pool_pick(grp, s2, s4, s8, s16)[:tm, :] - d).astype(BF16)

    return _call(
        body, name="pool_bwd", grid=(nt,),
        in_specs=[pl.BlockSpec((tm, BW), lambda i: (i, 0)),
                  pl.BlockSpec((PHALO, BW), lambda i: (jnp.minimum((i + 1) * hb, t_rows // PHALO - 1), 0))],
        out_specs=pl.BlockSpec((tm, BW), lambda i: (i, 0)),
        out_shape=jax.ShapeDtypeStruct((t_rows, BW), BF16),
        compiler_params=_cparams(("parallel",)))(dp, dp)


_MERGE_W = ("wglu", "bglu", "wpa", "lng", "lnb", "wpb", "wgrp", "scale", "wpc", "bgate", "wout")
_MERGE_SMALL = ("bglu", "lng", "lnb", "scale", "bgate")
_MERGE_BLOCKED = ("wpa", "wpb", "wpc")


def _merge_load(name, ref):
    if name in _MERGE_BLOCKED:
        return jnp.concatenate([ref[q] for q in range(NDEV)], axis=1)
    return ref[...]


def _merge_math(x, yssm, cv, pbar, zg, w, taps):
    t_glu, t_ya, t_yb, t_p, t_yc = taps
    g = jax.nn.gelu(yssm)
    outa = g * jax.nn.sigmoid(_mm(g, w["wglu"]) + t_glu + w["bglu"])
    ya = _mm(outa, w["wpa"]) + t_ya
    mu = jnp.mean(cv, axis=-1, keepdims=True)
    var = jnp.mean(jnp.square(cv - mu), axis=-1, keepdims=True)
    hs = jax.nn.silu((cv - mu) * lax.rsqrt(var + EPS) * w["lng"] + w["lnb"])
    yb = _mm(hs, w["wpb"]) + t_yb
    gw = BW // 4
    pk = jnp.concatenate([_mm(pbar[:, gw * k:gw * (k + 1)], w["wgrp"][k]) for k in range(4)], axis=1) + t_p
    pc = pk * w["scale"]
    yc = _mm(pc, w["wpc"]) + t_yc
    gates = jax.nn.sigmoid(zg + w["bgate"])
    merged = gates[:, :D] * ya + gates[:, D:2 * D] * yb + gates[:, 2 * D:] * yc
    x1 = x + _mm(merged, w["wout"])
    acts = tuple(a.astype(BF16) for a in (g, outa, hs, pbar, pc, merged))
    return x1, acts


def _merge_specs(tm, p):
    rows = lambda width, col=0: pl.BlockSpec((tm, width), lambda i, c=col: (i, c))
    data = [rows(D), rows(BW), rows(BW), rows(BW), rows(D, 2), rows(D, 3), rows(D, 4)]
    wspecs = []
    for name in _MERGE_W:
        nd = p[name].ndim
        wspecs.append(pl.BlockSpec(p[name].shape, lambda i, nd=nd: (0,) * nd))
    return rows, data, wspecs


def _merge_fwd(x, yssm, cv, pbar, z, p, token, tm=256):
    t_rows = x.shape[0]
    rows, data, wspecs = _merge_specs(tm, p)

    def body(x_ref, y_ref, cv_ref, pb_ref, za_ref, zb_ref, zc_ref, *rest):
        w = {name: _merge_load(name, r) for name, r in zip(_MERGE_W, rest[:len(_MERGE_W)])}
        o_ref = rest[len(_MERGE_W) + 1]
        taps = (0.0, 0.0, 0.0, 0.0, 0.0)
        zg = jnp.concatenate([za_ref[...], zb_ref[...], zc_ref[...]], axis=1)
        o_ref[...] = _merge_math(x_ref[...], y_ref[...], cv_ref[...], pb_ref[...], zg, w, taps)[0]

    return _call(
        body, name="merge_fwd", grid=(t_rows // tm,),
        in_specs=data + wspecs + [pl.BlockSpec((8, 128), lambda i: (0, 0))], out_specs=rows(D),
        out_shape=jax.ShapeDtypeStruct((t_rows, D), F32),
        compiler_params=_cparams(("parallel",)))(x, yssm, cv, pbar, z, z, z, *[p[n] for n in _MERGE_W], token)


def _merge_bwd(dx1, x, yssm, cv, pbar, z, p, token, tm=256):
    t_rows = x.shape[0]
    rows, data, wspecs = _merge_specs(tm, p)
    nw = len(_MERGE_W)

    def body(dx_ref, x_ref, y_ref, cv_ref, pb_ref, za_ref, zb_ref, zc_ref, *rest):
        w = {name: _merge_load(name, r) for name, r in zip(_MERGE_W, rest[:nw])}
        zg = jnp.concatenate([za_ref[...], zb_ref[...], zc_ref[...]], axis=1)
        outs = rest[nw + 1:]
        small = {n: w[n] for n in _MERGE_SMALL}
        taps = (jnp.zeros((tm, BW), F32), jnp.zeros((tm, D), F32), jnp.zeros((tm, D), F32),
                jnp.zeros((tm, BW), F32), jnp.zeros((tm, D), F32))

        def f(yssm_, cv_, pbar_, zg_, small_, taps_):
            return _merge_math(x_ref[...], yssm_, cv_, pbar_, zg_, {**w, **small_}, taps_)

        _, vjp, acts = jax.vjp(f, y_ref[...], cv_ref[...], pb_ref[...], zg, small, taps, has_aux=True)
        dy, dcv, dpb, dzg, dsmall, dtaps = vjp(dx_ref[...])
        outs[0][...] = dy
        outs[1][...] = dcv
        outs[2][...] = dpb
        outs[3][...] = dzg.astype(BF16)
        for k in range(6):
            outs[4 + k][...] = acts[k]
        for k in range(5):
            outs[10 + k][...] = dtaps[k].astype(BF16)

        @pl.when(pl.program_id(0) == 0)
        def _():
            for k in range(5):
                outs[15 + k][...] = jnp.zeros_like(outs[15 + k])

        for k, n in enumerate(_MERGE_SMALL):
            outs[15 + k][...] += dsmall[n]

    f32o = lambda width: jax.ShapeDtypeStruct((t_rows, width), F32)
    bfo = lambda width: jax.ShapeDtypeStruct((t_rows, width), BF16)
    small_shapes = [jax.ShapeDtypeStruct(p[n].shape, F32) for n in _MERGE_SMALL]
    small_specs = [pl.BlockSpec(p[n].shape, lambda i: (0, 0)) for n in _MERGE_SMALL]
    out_shape = ([f32o(BW), f32o(BW), f32o(BW), bfo(3 * D)]
                 + [bfo(BW), bfo(BW), bfo(BW), bfo(BW), bfo(BW), bfo(D)]
                 + [bfo(BW), bfo(D), bfo(D), bfo(BW), bfo(D)] + small_shapes)
    out_specs = ([rows(BW), rows(BW), rows(BW), rows(3 * D)]
                 + [rows(BW)] * 5 + [rows(D)]
                 + [rows(BW), rows(D), rows(D), rows(BW), rows(D)] + small_specs)
    return _call(
        body, name="merge_bwd", grid=(t_rows // tm,),
        in_specs=[rows(D)] + data + wspecs + [pl.BlockSpec((8, 128), lambda i: (0, 0))], out_specs=out_specs,
        out_shape=out_shape, compiler_params=_cparams(("arbitrary",)))(
            dx1, x, yssm, cv, pbar, z, z, z, *[p[n] for n in _MERGE_W], token)


def _ffn_fwd(x1, gamma, wg, wu, wd, tm=1024, th=512):
    t_rows = x1.shape[0]
    tm = min(tm, t_rows)
    nh = HIDP // th

    def body(x_ref, g_ref, wg_ref, wu_ref, wd_ref, o_ref, gp_ref, up_ref, h_ref, acc_ref):
        j = pl.program_id(1)

        @pl.when(j == 0)
        def _():
            h_ref[...] = _rms(x_ref[...], g_ref[...]).astype(BF16)
            acc_ref[...] = jnp.zeros_like(acc_ref)

        gp = _dot_nt(h_ref[...], wg_ref[...])
        up = _dot_nt(h_ref[...], wu_ref[...])
        gp_ref[...] = gp
        up_ref[...] = up
        acc_ref[...] += _dot(jax.nn.silu(gp) * up, wd_ref[...])

        @pl.when(j == nh - 1)
        def _():
            o_ref[...] = x_ref[...] + acc_ref[...]

    return _call(
        body, name="ffn_fwd", grid=(t_rows // tm, nh),
        in_specs=[pl.BlockSpec((tm, D), lambda i, j: (i, 0)), pl.BlockSpec((1, D), lambda i, j: (0, 0)),
                  pl.BlockSpec((th, D), lambda i, j: (j, 0)), pl.BlockSpec((th, D), lambda i, j: (j, 0)),
                  pl.BlockSpec((th, D), lambda i, j: (j, 0))],
        out_specs=[pl.BlockSpec((tm, D), lambda i, j: (i, 0)), pl.BlockSpec((tm, th), lambda i, j: (i, j)),
                   pl.BlockSpec((tm, th), lambda i, j: (i, j))],
        out_shape=[jax.ShapeDtypeStruct((t_rows, D), F32), jax.ShapeDtypeStruct((t_rows, HIDP), F32),
                   jax.ShapeDtypeStruct((t_rows, HIDP), F32)],
        scratch_shapes=[pltpu.VMEM((tm, D), BF16), pltpu.VMEM((tm, D), F32)],
        compiler_params=_cparams(("parallel", "arbitrary")))(x1, gamma, wg, wu, wd)


def _rms_bwd_tail(x, gamma, dh):
    _, vjp = jax.vjp(_rms, x, gamma)
    return vjp(dh)


def _ffn_bwd(dx2, x1, gamma, gpre, upre, wg, wu, wd, token, tm=512, th=1024):
    t_rows = x1.shape[0]
    nh = HIDP // th

    def body(d_ref, x_ref, g_ref, gp_ref, up_ref, wg_ref, wu_ref, wd_ref, token_ref,
             dx_ref, dgam_ref, dgp_ref, dup_ref, act_ref, h_ref, acc_ref):
        i = pl.program_id(0)
        j = pl.program_id(1)

        @pl.when(j == 0)
        def _():
            acc_ref[...] = jnp.zeros_like(acc_ref)

        @pl.when((i == 0) & (j == 0))
        def _():
            dgam_ref[...] = jnp.zeros_like(dgam_ref)

        dact = _dot_nt(d_ref[...], wd_ref[...])
        gp = gp_ref[...]
        up = up_ref[...]
        sg = jax.nn.sigmoid(gp)
        silu = gp * sg
        dgp = (dact * up * (sg * (1.0 + gp * (1.0 - sg)))).astype(BF16)
        dup = (dact * silu).astype(BF16)
        dgp_ref[...] = dgp
        dup_ref[...] = dup
        act_ref[...] = (silu * up).astype(BF16)
        acc_ref[...] += _dot(dgp, wg_ref[...]) + _dot(dup, wu_ref[...])

        @pl.when(j == nh - 1)
        def _():
            x = x_ref[...]
            h_ref[...] = _rms(x, g_ref[...]).astype(BF16)
            dx, dgam = _rms_bwd_tail(x, g_ref[...], acc_ref[...])
            dx_ref[...] = d_ref[...] + dx
            dgam_ref[...] += dgam

    row_d = pl.BlockSpec((tm, D), lambda i, j: (i, 0))
    row_h = pl.BlockSpec((tm, th), lambda i, j: (i, j))
    return _call(
        body, name="ffn_bwd", grid=(t_rows // tm, nh),
        in_specs=[row_d, row_d, pl.BlockSpec((1, D), lambda i, j: (0, 0)), row_h, row_h,
                  pl.BlockSpec((th, D), lambda i, j: (j, 0)), pl.BlockSpec((th, D), lambda i, j: (j, 0)),
                  pl.BlockSpec((th, D), lambda i, j: (j, 0)), TOKEN_SPEC],
        out_specs=[row_d, pl.BlockSpec((1, D), lambda i, j: (0, 0)), row_h, row_h, row_h, row_d],
        out_shape=[jax.ShapeDtypeStruct((t_rows, D), F32), jax.ShapeDtypeStruct((1, D), F32),
                   jax.ShapeDtypeStruct((t_rows, HIDP), BF16), jax.ShapeDtypeStruct((t_rows, HIDP), BF16),
                   jax.ShapeDtypeStruct((t_rows, HIDP), BF16), jax.ShapeDtypeStruct((t_rows, D), BF16)],
        scratch_shapes=[pltpu.VMEM((tm, D), F32)],
        compiler_params=_cparams(("arbitrary", "arbitrary")))(dx2, x1, gamma, gpre, upre, wg, wu, wd, token)


def _inproj_bwd(dz, dx1, x, gamma, w, token, tm=1024, nb=2):
    t_rows = x.shape[0]
    tm = min(tm, t_rows)
    oc = w.shape[2]
    tn = nb * oc
    nn = NDEV // nb

    def body(dz_ref, d1_ref, x_ref, g_ref, w_ref, token_ref, dx_ref, dgam_ref, acc_ref):
        i = pl.program_id(0)
        j = pl.program_id(1)

        @pl.when(j == 0)
        def _():
            acc_ref[...] = jnp.zeros_like(acc_ref)

        @pl.when((i == 0) & (j == 0))
        def _():
            dgam_ref[...] = jnp.zeros_like(dgam_ref)

        for q in range(nb):
            acc_ref[...] += _dot_nt(dz_ref[:, oc * q:oc * (q + 1)], w_ref[q])

        @pl.when(j == nn - 1)
        def _():
            x = x_ref[...]
            dx, dgam = _rms_bwd_tail(x, g_ref[...], acc_ref[...])
            dx_ref[...] = d1_ref[...] + dx
            dgam_ref[...] += dgam

    row_d = pl.BlockSpec((tm, D), lambda i, j: (i, 0))
    return _call(
        body, name="inproj_bwd", grid=(t_rows // tm, nn),
        in_specs=[pl.BlockSpec((tm, tn), lambda i, j: (i, j)), row_d, row_d, pl.BlockSpec((1, D), lambda i, j: (0, 0)),
                  pl.BlockSpec((nb, D, oc), lambda i, j: (j, 0, 0)), TOKEN_SPEC],
        out_specs=[row_d, pl.BlockSpec((1, D), lambda i, j: (0, 0))],
        out_shape=[jax.ShapeDtypeStruct((t_rows, D), F32), jax.ShapeDtypeStruct((1, D), F32)],
        scratch_shapes=[pltpu.VMEM((tm, D), F32)],
        compiler_params=_cparams(("arbitrary", "arbitrary")))(dz, dx1, x, gamma, w, token)


def _matmul_tn(a, b, name, owner_cols=None, tt=2048):
    t_rows, k = a.shape
    n = b.shape[1]
    tk = min(k, 1024)
    tt = min(tt, t_rows)
    nt = t_rows // tt
    if owner_cols is None:
        tn, nb = min(n, 1024), None
        out_spec = pl.BlockSpec((tk, tn), lambda i, j, t: (i, j))
        out_shape = jax.ShapeDtypeStruct((k, n), BF16)
    else:
        nb = min(n // owner_cols, max(1, 1280 // owner_cols))
        tn = nb * owner_cols
        out_spec = pl.BlockSpec((nb, tk, owner_cols), lambda i, j, t: (j, i, 0))
        out_shape = jax.ShapeDtypeStruct((n // owner_cols, k, owner_cols), BF16)

    def body(a_ref, b_ref, o_ref, acc_ref):
        t = pl.program_id(2)

        @pl.when(t == 0)
        def _():
            acc_ref[...] = jnp.zeros_like(acc_ref)

        acc_ref[...] += _dot_tn(a_ref[...], b_ref[...])

        @pl.when(t == nt - 1)
        def _():
            if nb is None:
                o_ref[...] = acc_ref[...].astype(BF16)
            else:
                for q in range(nb):
                    o_ref[q] = acc_ref[:, owner_cols * q:owner_cols * (q + 1)].astype(BF16)

    return _call(
        body, name=name, grid=(k // tk, n // tn, nt),
        in_specs=[pl.BlockSpec((tt, tk), lambda i, j, t: (t, i)), pl.BlockSpec((tt, tn), lambda i, j, t: (t, j))],
        out_specs=out_spec, out_shape=out_shape,
        scratch_shapes=[pltpu.VMEM((tk, tn), F32)],
        compiler_params=_cparams(("parallel", "parallel", "arbitrary")))(a, b)


def _group_tn(a, b):
    t_rows = a.shape[0]
    gw = BW // 4

    def body(a_ref, b_ref, o_ref):
        o_ref[...] = _dot_tn(a_ref[...], b_ref[...])

    return _call(
        body, name="pool_group_tn", grid=(4,),
        in_specs=[pl.BlockSpec((t_rows, gw), lambda k: (0, k)), pl.BlockSpec((t_rows, gw), lambda k: (0, k))],
        out_specs=pl.BlockSpec((None, gw, gw), lambda k: (k, 0, 0)),
        out_shape=jax.ShapeDtypeStruct((4, gw, gw), F32),
        compiler_params=_cparams(("parallel",)))(a, b)


def _loss_head(x2, gamma, target, tm=512):
    t_rows = x2.shape[0]

    def body(x_ref, g_ref, t_ref, loss_ref, dx_ref, dgam_ref):
        @pl.when(pl.program_id(0) == 0)
        def _():
            loss_ref[...] = jnp.zeros_like(loss_ref)
            dgam_ref[...] = jnp.zeros_like(dgam_ref)

        def f(x, g):
            err = jnp.square(_rms(x, g) - t_ref[...])
            return 0.5 * jnp.sum(jnp.mean(err, axis=-1, keepdims=True), axis=0, keepdims=True)

        loss, vjp = jax.vjp(f, x_ref[...], g_ref[...])
        dx, dgam = vjp(jnp.ones((1, 1), F32))
        loss_ref[...] += jnp.broadcast_to(loss, (1, 128))
        dx_ref[...] = dx
        dgam_ref[...] += dgam

    row_d = pl.BlockSpec((tm, D), lambda i: (i, 0))
    return _call(
        body, name="loss_head", grid=(t_rows // tm,),
        in_specs=[row_d, pl.BlockSpec((1, D), lambda i: (0, 0)), row_d],
        out_specs=[pl.BlockSpec((1, 128), lambda i: (0, 0)), row_d, pl.BlockSpec((1, D), lambda i: (0, 0))],
        out_shape=[jax.ShapeDtypeStruct((1, 128), F32), jax.ShapeDtypeStruct((t_rows, D), F32),
                   jax.ShapeDtypeStruct((1, D), F32)],
        compiler_params=_cparams(("arbitrary",)))(x2, gamma, target)


NCHIP = NDEV // 2


def _coords():
    return lax.axis_index("x"), lax.axis_index("y"), lax.axis_index("c")


def _remote(src, dst, send_sem, recv_sem, peer):
    return pltpu.make_async_remote_copy(src_ref=src, dst_ref=dst, send_sem=send_sem, recv_sem=recv_sem,
                                        device_id=peer, device_id_type=MESH)


def _comm_call(name, srcs, out_shapes, n_rec, plan, aliases=None):
    ns, no = len(srcs), len(out_shapes)

    def body(*refs):
        ins, outs = refs[:ns], refs[ns:ns + no]
        loc_sem, send_sem, recv_sem = refs[ns + no:]
        x, y, c = _coords()
        recs = plan(ins, outs, x, y, c)
        assert len(recs) == n_rec
        for k, r in enumerate(recs):
            for src, dst in r.get("local", ()):
                pltpu.make_async_copy(src, dst, loc_sem.at[k]).start()
            for peer, src, dst in r.get("remote", ()):
                _remote(src, dst, send_sem.at[k], recv_sem.at[k], peer).start()
        for k, r in enumerate(recs):
            if r.get("recv_wait") is not None:
                w = r["recv_wait"]
                _remote(w, w, send_sem.at[k], recv_sem.at[k], (x, y, c)).wait_recv()
            if r.get("send_wait") is not None:
                w = r["send_wait"]
                _remote(w, w, send_sem.at[k], recv_sem.at[k], (x, y, c)).wait_send()
            if r.get("local_wait") is not None:
                w = r["local_wait"]
                pltpu.make_async_copy(w, w, loc_sem.at[k]).wait()

    return _call(
        body, name=name, in_specs=[ANY] * ns, out_specs=[ANY] * no, out_shape=out_shapes,
        input_output_aliases=aliases or {}, scratch_shapes=[pltpu.SemaphoreType.DMA((n_rec,))] * 3)(*srcs)


def _gather_call(srcs, out_shapes, items, aliases):
    ns, no, n = len(srcs), len(out_shapes), len(items)

    def body(*refs):
        ins, outs = refs[:ns], refs[ns:ns + no]
        loc, sib_s, sib_r, ici_s, ici_r, fwd_s, fwd_r = refs[ns + no:]
        x, y, c = _coords()
        me, sib = (x, y, c), (x, y, 1 - c)
        chips = [(1 - x, y), (x, 1 - y), (1 - x, 1 - y)]
        index = lambda px, py, pc: 4 * px + 2 * py + pc
        for k, (si, oi, shard, block, _) in enumerate(items):
            mine = block(outs[oi], index(*me))
            src = mine if shard is None else shard(ins[si])
            if shard is not None:
                pltpu.make_async_copy(src, mine, loc.at[k]).start()
            _remote(src, mine, sib_s.at[k], sib_r.at[k], sib).start()
            for chip in chips:
                _remote(src, mine, ici_s.at[k], ici_r.at[k], (*chip, c)).start()
        for k, (si, oi, _, block, blocks) in enumerate(items):
            three = blocks(outs[oi], 3)
            _remote(three, three, ici_s.at[k], ici_r.at[k], me).wait_recv()
            for chip in chips:
                landed = block(outs[oi], index(*chip, c))
                _remote(landed, landed, fwd_s.at[k], fwd_r.at[k], sib).start()
        for k, (si, oi, shard, _, blocks) in enumerate(items):
            one, three = blocks(outs[oi], 1), blocks(outs[oi], 3)
            _remote(one, one, sib_s.at[k], sib_r.at[k], me).wait()
            _remote(three, three, fwd_s.at[k], fwd_r.at[k], me).wait()
            _remote(three, three, ici_s.at[k], ici_r.at[k], me).wait_send()
            if shard is not None:
                pltpu.make_async_copy(one, one, loc.at[k]).wait()

    return _call(
        body, name="gather_weights", in_specs=[ANY] * ns, out_specs=[ANY] * no, out_shape=out_shapes,
        input_output_aliases=aliases, scratch_shapes=[pltpu.SemaphoreType.DMA((n,))] * 7)(*srcs)


_BIG = {
    "w_in": (True, D, IN_W // NDEV),
    "ssm_w_glu": (False, BW // NDEV, BW),
    "ssm_w_proj": (True, BW, D // NDEV),
    "conv_w_proj": (True, BW, D // NDEV),
    "pool_w_proj": (True, BW, D // NDEV),
    "w_out": (False, D // NDEV, D),
    "ffn_w_gate": (False, HPAD, D),
    "ffn_w_up": (False, HPAD, D),
    "ffn_w_down": (False, HPAD, D),
}
GROUP_IN = ("w_in",)
GROUP_MIX = ("ssm_w_glu", "ssm_w_proj", "conv_w_proj", "pool_w_proj", "w_out")
GROUP_FFN = ("ffn_w_gate", "ffn_w_up", "ffn_w_down")


def _gathered_shape(name):
    blocked, kk, nn = _BIG[name]
    return jax.ShapeDtypeStruct((NDEV, kk, nn) if blocked else (NDEV * kk, nn), BF16)


def _block_view(name):
    blocked, kk, _ = _BIG[name]
    if blocked:
        return lambda ref, q: ref.at[q]
    return lambda ref, q: ref.at[pl.ds(pl.multiple_of(q * kk, 16), kk), :]


def _blocks_view(name):
    blocked, kk, _ = _BIG[name]
    if blocked:
        return lambda ref, n: ref.at[pl.ds(0, n)]
    return lambda ref, n: ref.at[pl.ds(0, n * kk), :]


def _place_shards(weights, names, layer, me):
    cnt = len(names)

    def body(me_ref, *refs):
        w_refs, outs, stages, sem = refs[:cnt], refs[cnt:2 * cnt], refs[2 * cnt:3 * cnt], refs[3 * cnt]
        q = me_ref[0]
        copies = []
        for k, name in enumerate(names):
            _, kk, nn = _BIG[name]
            rows = w_refs[k].shape[0]
            stages[k][0:rows, :] = w_refs[k][...].astype(BF16)
            if rows < kk:
                stages[k][rows:kk, :] = jnp.zeros((kk - rows, nn), BF16)
            copies.append(pltpu.make_async_copy(stages[k], _block_view(name)(outs[k], q), sem.at[k]))
            copies[-1].start()
        for cp in copies:
            cp.wait()

    in_specs = [pl.BlockSpec(memory_space=pltpu.SMEM)]
    in_specs += [pl.BlockSpec((None,) + weights[n].shape[1:], lambda i: (layer, 0, 0)) for n in names]
    res = _call(
        body, name="place_shards", grid=(1,), in_specs=in_specs, out_specs=[ANY] * cnt,
        out_shape=[_gathered_shape(n) for n in names],
        scratch_shapes=[pltpu.VMEM(_BIG[n][1:], BF16) for n in names] + [pltpu.SemaphoreType.DMA((cnt,))],
        compiler_params=_cparams(("arbitrary",)))(me, *[weights[n] for n in names])
    return dict(zip(names, res))


def _gather_weights(placed, conv_dw, names):
    cnt = len(names)
    srcs = [placed[n] for n in names] + [conv_dw]
    outs = [_gathered_shape(n) for n in names] + [jax.ShapeDtypeStruct((NDEV,) + conv_dw.shape, conv_dw.dtype)]
    items = [(k, k, None, _block_view(n), _blocks_view(n)) for k, n in enumerate(names)]
    items.append((cnt, cnt, lambda ref: ref, lambda ref, q: ref.at[q], lambda ref, n: ref.at[pl.ds(0, n)]))
    res = _gather_call(srcs, outs, items, {k: k for k in range(cnt)})
    return dict(zip(names, res[:-1])), res[-1]


def _gather_start(placed, names, after, tag):
    def copies(src_refs, land_refs, x, y, c):
        me = 4 * x + 2 * y + c
        peers = [(x, y, 1 - c), (1 - x, y, c), (x, 1 - y, c), (1 - x, 1 - y, c)]
        out = []
        for k, n in enumerate(names):
            mine = _block_view(n)(land_refs[k], me)
            out.append([(peer, mine, mine) for peer in peers])
        return out

    return _split_start("gather_start_" + tag, [], [placed[n] for n in names], copies, after)


def _gather_finish(handle, names, after, tag):
    four = [functools.partial(lambda ref, bv: bv(ref, 4), bv=_blocks_view(n)) for n in names]
    lands = _split_wait("gather_wait_" + tag, handle, four, after)
    cnt = len(names)

    def plan(ins, out_refs, x, y, c):
        sib = (x, y, 1 - c)
        recs = []
        for k, n in enumerate(names):
            three = _blocks_view(n)(out_refs[k], 3)
            remote = []
            for px, py in [(1 - x, y), (x, 1 - y), (1 - x, 1 - y)]:
                landed = _block_view(n)(out_refs[k], 4 * px + 2 * py + c)
                remote.append((sib, landed, landed))
            recs.append(dict(remote=remote, send_wait=three, recv_wait=three))
        return recs

    res = _comm_call("gather_pair_" + tag, lands, [jax.ShapeDtypeStruct(l.shape, l.dtype) for l in lands], cnt, plan,
                     aliases={k: k for k in range(cnt)})
    return dict(zip(names, res))


def _pair_add(name, grads, rcv, core):
    nl = len(grads)
    _, kk, nn = grads[0].shape

    def body(c_ref, *refs):
        l = pl.program_id(0)
        own = refs[0][...]
        for j in range(1, nl):
            own = jnp.where(l == j, refs[j][...], own)
        refs[nl + 1][...] = (own.astype(F32) + refs[nl][...].astype(F32)).astype(BF16)

    gspec = lambda j: pl.BlockSpec((None, kk, nn), lambda l, h, c_ref: (jnp.where(l == j, 2 * h + c_ref[0], 0), 0, 0))
    rspec = pl.BlockSpec((None, None, kk, nn), lambda l, h, c_ref: (h, l, 0, 0))
    return _call(
        body, name="pair_add_" + name,
        grid_spec=pltpu.PrefetchScalarGridSpec(num_scalar_prefetch=1, grid=(nl, NCHIP),
                                               in_specs=[gspec(j) for j in range(nl)] + [rspec], out_specs=rspec),
        out_shape=jax.ShapeDtypeStruct(rcv.shape, BF16),
        compiler_params=_cparams(("arbitrary", "arbitrary")))(core, *grads, rcv)


def _pair_add_small(owned, lists, core):
    on, ln = list(owned), list(lists)
    flat = []
    for n in on:
        flat += list(owned[n][0]) + [owned[n][1]]
    for n in ln:
        flat += list(lists[n][0]) + [lists[n][1]]

    def body(c_ref, *refs):
        outs = refs[len(flat):]
        c = c_ref[0]
        pos = 0
        for k, n in enumerate(on):
            nl = len(owned[n][0])
            for h in range(NCHIP):
                for l in range(nl):
                    outs[k][h, l] = refs[pos + l][pl.ds(2 * h + c, 1)][0] + refs[pos + nl][h, l]
            pos += nl + 1
        for k, n in enumerate(ln):
            nl = len(lists[n][0])
            for l in range(nl):
                out = outs[len(on) + k]
                out[l] = (refs[pos + l][...] + refs[pos + nl][l]).astype(out.dtype)
            pos += nl + 1

    shapes = [jax.ShapeDtypeStruct(owned[n][1].shape, F32) for n in on]
    shapes += [jax.ShapeDtypeStruct(lists[n][1].shape, F32 if n == "final_norm" else BF16) for n in ln]
    res = _call(body, name="pair_add_small", out_shape=shapes,
                in_specs=[pl.BlockSpec(memory_space=pltpu.SMEM)] + [pl.BlockSpec(memory_space=pltpu.VMEM)] * len(flat),
                compiler_params=_cparams())(core, *flat)
    return dict(zip(on + ln, res))


def _pair_reduce(tag, big, by_owner, small, core):
    rs = {**big, **by_owner}
    srcs, outs, plans, rcv_at = [], [], [], {}
    for name, arrays in rs.items():
        rcv_at[name] = len(outs)
        outs.append(jax.ShapeDtypeStruct((NCHIP, len(arrays)) + arrays[0].shape[1:], arrays[0].dtype))
        for l, arr in enumerate(arrays):
            srcs.append(arr)
            plans.append((len(srcs) - 1, rcv_at[name], l, True))
    for name, arrays in small.items():
        rcv_at[name] = len(outs)
        outs.append(jax.ShapeDtypeStruct((len(arrays),) + arrays[0].shape, F32))
        for l, arr in enumerate(arrays):
            srcs.append(arr)
            plans.append((len(srcs) - 1, rcv_at[name], l, False))

    def plan_pair(ins, out_refs, x, y, c):
        sib = (x, y, 1 - c)
        recs = []
        for si, ro, l, slabs in plans:
            if slabs:
                four = out_refs[ro].at[pl.ds(0, NCHIP), l]
                recs.append(dict(remote=[(sib, ins[si].at[2 * h + 1 - c], out_refs[ro].at[h, l]) for h in range(NCHIP)],
                                 send_wait=four, recv_wait=four))
            else:
                dst = out_refs[ro].at[l]
                recs.append(dict(remote=[(sib, ins[si], dst)], send_wait=dst, recv_wait=dst))
        return recs

    res = _comm_call("pair_exchange_" + tag, srcs, outs, len(plans), plan_pair)
    part = {name: _pair_add(name, big[name], res[rcv_at[name]], core) for name in big}
    if by_owner or small:
        part.update(_pair_add_small({n: (by_owner[n], res[rcv_at[n]]) for n in by_owner},
                                    {n: (small[n], res[rcv_at[n]]) for n in small}, core))
    return part


def _chip_copies(src, land, slabbed, x, y, c):
    mine = 2 * x + y
    copies = []
    for step in range(1, NCHIP):
        h = (mine + step) % NCHIP
        copies.append(((h // 2, h % 2, c), src.at[h] if slabbed else src, land.at[mine]))
    return copies


def _chip_exchange(part, slabbed, keep_own):
    names = list(part)
    outs = [jax.ShapeDtypeStruct((() if n in slabbed else (NCHIP,)) + part[n].shape, part[n].dtype) for n in names]

    def plan(ins, out_refs, x, y, c):
        mine = 2 * x + y
        recs = []
        for k, n in enumerate(names):
            three = out_refs[k].at[pl.ds(0, NCHIP - 1)]
            rec = dict(remote=_chip_copies(ins[k], out_refs[k], n in slabbed, x, y, c), send_wait=three, recv_wait=three)
            if n in keep_own:
                rec["local"] = [(ins[k].at[mine] if n in slabbed else ins[k], out_refs[k].at[mine])]
                rec["local_wait"] = out_refs[k].at[0]
            recs.append(rec)
        return recs

    res = _comm_call("chip_exchange", [part[n] for n in names], outs, len(names), plan)
    return dict(zip(names, res))


HBM_SPEC = pl.BlockSpec(memory_space=pltpu.HBM)
SEM_SPEC = pl.BlockSpec(memory_space=pltpu.SEMAPHORE)
SPLIT_EFFECT = pltpu.SideEffectType.DATAFLOW_SIDE_EFFECTING


def _split_start(name, srcs, land_shapes, copies_fn, after):
    ns, n = len(srcs), len(land_shapes)
    lands = [pltpu.with_memory_space_constraint(s if isinstance(s, jax.Array) else lax.empty(s.shape, s.dtype), pltpu.HBM)
             for s in land_shapes]

    def body(*refs):
        src_refs, land_refs = refs[:ns], refs[ns:ns + n]
        send_sem, recv_sem = refs[ns + n + 1], refs[ns + n + 2]
        token = refs[-1]
        x, y, c = _coords()
        for k, copies in enumerate(copies_fn(src_refs, land_refs, x, y, c)):
            for peer, src, dst in copies:
                _remote(src, dst, send_sem.at[k], recv_sem.at[k], peer).start()
        token[...] = jnp.zeros_like(token)

    res = pl.pallas_call(
        body, name=name,
        out_shape=(pltpu.SemaphoreType.DMA((n,)), pltpu.SemaphoreType.DMA((n,)),
                   *[pltpu.HBM(s.shape, s.dtype) for s in land_shapes], jax.ShapeDtypeStruct((8, 128), F32)),
        in_specs=[HBM_SPEC] * (ns + n) + [ANY],
        out_specs=(SEM_SPEC, SEM_SPEC, *[HBM_SPEC] * n, pl.BlockSpec(memory_space=pltpu.VMEM)),
        input_output_aliases={ns + i: 2 + i for i in range(n)},
        compiler_params=pltpu.CompilerParams(has_side_effects=SPLIT_EFFECT),
    )(*[pltpu.with_memory_space_constraint(s, pltpu.HBM) for s in srcs], *lands, after)
    return dict(send=res[0], recv=res[1], srcs=list(srcs), lands=list(res[2:2 + n]), token=res[-1])


def _split_wait(name, handle, wait_views, after):
    n = len(handle["lands"])
    after = after if isinstance(after, (tuple, list)) else (after,)

    def body(*refs):
        land_refs = refs[:n]
        send_sem, recv_sem = refs[n], refs[n + 1]
        x, y, c = _coords()
        for k in range(n):
            w = wait_views[k](land_refs[k])
            cp = _remote(w, w, send_sem.at[k], recv_sem.at[k], (x, y, c))
            cp.wait_send()
            cp.wait_recv()

    res = pl.pallas_call(
        body, name=name,
        out_shape=tuple(pltpu.HBM(s.shape, s.dtype) for s in handle["lands"]),
        in_specs=[HBM_SPEC] * n + [SEM_SPEC, SEM_SPEC] + [ANY] * len(after), out_specs=tuple([HBM_SPEC] * n),
        input_output_aliases={i: i for i in range(n)},
        compiler_params=pltpu.CompilerParams(has_side_effects=SPLIT_EFFECT),
    )(*handle["lands"], handle["send"], handle["recv"], *after)
    return list(res)


def _adamw(w, g, m, v):
    m = ADAM_B1 * m + (1.0 - ADAM_B1) * g
    v = ADAM_B2 * v + (1.0 - ADAM_B2) * jnp.square(g)
    m_hat = m / (1.0 - ADAM_B1 ** ADAM_STEP)
    v_hat = v / (1.0 - ADAM_B2 ** ADAM_STEP)
    delta = -ADAM_LR * (m_hat / (jnp.sqrt(v_hat) + ADAM_EPS) + ADAM_WD * w)
    return delta, m, v


def _chip_start(part, tag):
    names = list(part)

    def copies(src_refs, land_refs, x, y, c):
        return [_chip_copies(src_refs[k], land_refs[k], True, x, y, c) for k in range(len(names))]

    shapes = [jax.ShapeDtypeStruct(part[n].shape, part[n].dtype) for n in names]
    return names, _split_start("chip_start_" + tag, [part[n] for n in names], shapes, copies, part[names[0]])


def _chip_wait(names, handle, after, tag):
    three = [lambda ref: ref.at[pl.ds(0, NCHIP - 1)]] * len(names)
    return dict(zip(names, _split_wait("chip_wait_" + tag, handle, three, after)))


def _sum_senders(ref):
    g = ref[0].astype(F32)
    for h in range(1, NCHIP):
        g = g + ref[h].astype(F32)
    return g


def _adam_big(name, own, recv, w, m, v, tk, chip):
    nl = len(own)
    kk, nn = w.shape[1], w.shape[2]
    nnp = own[0].shape[3]

    def body(chip_ref, *refs):
        l = pl.program_id(0)
        g = None
        for step in range(NCHIP):
            val = refs[step][...]
            for q in range(1, nl):
                val = jnp.where(l == q, refs[NCHIP * q + step][...], val)
            g = val.astype(F32) if g is None else g + val.astype(F32)
        w_ref, m_ref, v_ref, g_ref, d_ref, mo_ref, vo_ref = refs[NCHIP * nl:]
        g = g[:, :nn]
        delta, m2, v2 = _adamw(w_ref[...], g, m_ref[...], v_ref[...])
        g_ref[...] = g
        d_ref[...] = delta
        mo_ref[...] = m2
        vo_ref[...] = v2

    def slab(q, step):
        return pl.BlockSpec((None, None, tk, nnp), lambda l, i, chip_ref: (
            jnp.where(l == q, (chip_ref[0] + step) % NCHIP, 0), 0, jnp.where(l == q, i, 0), 0))

    in_specs, operands = [], []
    for q in range(nl):
        in_specs += [slab(q, step) for step in range(NCHIP)]
        operands += [own[q]] + [recv[q]] * (NCHIP - 1)
    wspec = pl.BlockSpec((None, tk, nn), lambda l, i, chip_ref: (l, i, 0))
    shape = jax.ShapeDtypeStruct(w.shape, F32)
    return _call(
        body, name="adamw_" + name,
        grid_spec=pltpu.PrefetchScalarGridSpec(num_scalar_prefetch=1, grid=(nl, kk // tk),
                                               in_specs=in_specs + [wspec] * 3, out_specs=[wspec] * 4),
        out_shape=[shape] * 4, compiler_params=_cparams(("arbitrary", "arbitrary")))(chip, *operands, w, m, v)


def _adam_small(names, recv, w, m, v):
    n = len(names)

    def body(*refs):
        r, ww, mm, vv = refs[:n], refs[n:2 * n], refs[2 * n:3 * n], refs[3 * n:4 * n]
        outs = refs[4 * n:]
        for k in range(n):
            for l in range(r[k].shape[1]):
                g = r[k][0, l].astype(F32)
                for h in range(1, NCHIP):
                    g = g + r[k][h, l].astype(F32)
                per_layer = ww[k].shape[1:]
                if g.shape[0] > per_layer[0] and g.shape[1:] == per_layer[1:]:
                    g = g[:per_layer[0]]
                at = l if g.shape == per_layer else pl.ds(l, 1)
                delta, m2, v2 = _adamw(ww[k][at], g, mm[k][at], vv[k][at])
                outs[k][at] = g
                outs[n + k][at] = delta
                outs[2 * n + k][at] = m2
                outs[3 * n + k][at] = v2

    shapes = [jax.ShapeDtypeStruct(w[k].shape, F32) for k in names]
    res = _call(body, name="adamw_small", out_shape=shapes * 4, compiler_params=_cparams())(
        *[recv[k] for k in names], *[w[k] for k in names], *[m[k] for k in names], *[v[k] for k in names])
    return {k: (res[i], res[n + i], res[2 * n + i], res[3 * n + i]) for i, k in enumerate(names)}


def _expand_b(bt):
    eye = jnp.eye(GB, dtype=bt.dtype)
    return jnp.einsum("jgpn,gh->jgphn", bt.reshape(NBLK, GB, SGRP, NSTATE), eye).reshape(NBLK, GB * SGRP, NS)


def _extract_b(db):
    x = db.reshape(NBLK, GB, SGRP, GB, NSTATE)
    eye = jnp.eye(GB, dtype=db.dtype)
    return jnp.einsum("jgphn,gh->jgpn", x, eye).reshape(NGRP, SGRP, NSTATE)


def _expand_c(c):
    ct = jnp.transpose(c, (0, 2, 1)).reshape(NBLK, GB, NSTATE, SGRP)
    eye = jnp.eye(GB, dtype=c.dtype)
    return jnp.einsum("jgnp,gh->jgnhp", ct, eye).reshape(NBLK, NS, GB * SGRP)


def _extract_c(dc):
    x = dc.reshape(NBLK, GB, NSTATE, GB, SGRP)
    eye = jnp.eye(GB, dtype=dc.dtype)
    d = jnp.einsum("jgnhp,gh->jgnp", x, eye).reshape(NGRP, NSTATE, SGRP)
    return jnp.transpose(d, (0, 2, 1))


_SMALL = ("norm1", "b_gate", "ssm_a_re", "ssm_a_im", "ssm_log_dt", "ssm_b_re", "ssm_b_im", "ssm_c_re", "ssm_c_im",
          "ssm_d", "ssm_b_glu", "conv_b_dw", "conv_ln_g", "conv_ln_b", "pool_w_group", "pool_scale", "norm2")
_ADAM_TK = {"w_in": 256, "ssm_w_glu": 64, "ssm_w_proj": 512, "conv_w_proj": 512, "pool_w_proj": 512, "w_out": 128,
            "ffn_w_gate": HSH, "ffn_w_up": HSH, "ffn_w_down": HSH}
_OUT_ORDER = ("norm1", "w_in", "b_gate", "ssm_a_re", "ssm_a_im", "ssm_log_dt", "ssm_b_re", "ssm_b_im", "ssm_c_re",
              "ssm_c_im", "ssm_d", "ssm_w_glu", "ssm_b_glu", "ssm_w_proj", "conv_w_dw", "conv_b_dw", "conv_ln_g",
              "conv_ln_b", "conv_w_proj", "pool_w_group", "pool_scale", "pool_w_proj", "w_out", "norm2", "ffn_w_gate",
              "ffn_w_up", "ffn_w_down", "final_norm")


def _layer_fwd(x, p, token, late_params=None):
    z, h = _inproj_fwd(x, p["norm1"], p["w_in"], token)
    yssm, hre, him = _ssm_fwd(z, p)
    cv = _conv_fwd(z, p["conv_w"], p["conv_b"])
    pbar = _pool_fwd(z)
    if late_params is not None:
        more, token = late_params((yssm, cv, pbar))
        p = {**p, **more}
    x1 = _merge_fwd(x, yssm, cv, pbar, z, p, token)
    x2, gpre, upre = _ffn_fwd(x1, p["norm2"], p["wg"], p["wu"], p["wd"])
    return x2, dict(x=x, h=h, z=z, yssm=yssm, hre=hre, him=him, cv=cv, pbar=pbar, x1=x1, gpre=gpre, upre=upre), p


def _layer_bwd(dx, p, s, token, leave=None):
    big, small = {}, {}
    go = (lambda tag, names: leave(tag, {n: big[n] for n in names})) if leave else (lambda tag, names: token)
    dx1, d_norm2, dgp, dup, act, h2 = _ffn_bwd(dx, s["x1"], p["norm2"], s["gpre"], s["upre"], p["wg"], p["wu"], p["wd"],
                                               token)
    big["ffn_w_gate"] = _matmul_tn(dgp, h2, "tn_gate").reshape(NDEV, HPAD, D)
    big["ffn_w_up"] = _matmul_tn(dup, h2, "tn_up").reshape(NDEV, HPAD, D)
    big["ffn_w_down"] = _matmul_tn(act, dx, "tn_down").reshape(NDEV, HPAD, D)
    (dy, dcv, dpb, dzg, a_g, a_outa, a_hs, a_pb, a_pc, a_mg, c_glu, c_ya, c_yb, c_p, c_yc,
     d_bglu, d_lng, d_lnb, d_scale, d_bgate) = _merge_bwd(dx1, s["x"], s["yssm"], s["cv"], s["pbar"], s["z"], p,
                                                          go("ffn", GROUP_FFN))
    big["ssm_w_glu"] = _matmul_tn(a_g, c_glu, "tn_glu").reshape(NDEV, BW // NDEV, BW)
    big["ssm_w_proj"] = _matmul_tn(a_outa, c_ya, "tn_ssm_proj", D // NDEV)
    big["conv_w_proj"] = _matmul_tn(a_hs, c_yb, "tn_conv_proj", D // NDEV)
    big["pool_w_proj"] = _matmul_tn(a_pc, c_yc, "tn_pool_proj", D // NDEV)
    big["w_out"] = _matmul_tn(a_mg, dx1, "tn_out").reshape(NDEV, D // NDEV, D)
    d_wgrp = _group_tn(a_pb, c_p)
    du_a, dbr, dbi, dcr, dci, dd, dar, dai, dldt = _ssm_bwd(dy, s["z"], s["hre"], s["him"], p, go("mix", GROUP_MIX))
    dva, dvb, dw8, dcb = _conv_bwd(dcv, s["z"], p["conv_w"])
    du_c = _pool_bwd(dpb)
    dz = jnp.concatenate([du_a, dva, dvb, du_c, dzg], axis=1)
    big["w_in"] = _matmul_tn(s["h"], dz, "tn_in", IN_W // NDEV)
    dx0, d_norm1 = _inproj_bwd(dz, dx1, s["x"], p["norm1"], p["w_in"], go("in", GROUP_IN))
    small["norm1"] = d_norm1
    small["b_gate"] = d_bgate
    small["ssm_a_re"] = dar.reshape(NGRP, NSTATE)
    small["ssm_a_im"] = dai.reshape(NGRP, NSTATE)
    small["ssm_log_dt"] = dldt.reshape(NBLK, 8, 128)[:, 0, :GB].reshape(1, NGRP)
    small["ssm_b_re"] = _extract_b(dbr)
    small["ssm_b_im"] = _extract_b(dbi)
    small["ssm_c_re"] = _extract_c(dcr)
    small["ssm_c_im"] = _extract_c(dci)
    small["ssm_d"] = dd.reshape(NGRP, SGRP)
    small["ssm_b_glu"] = d_bglu
    small["conv_b_dw"] = dcb
    small["conv_ln_g"] = d_lng
    small["conv_ln_b"] = d_lnb
    small["pool_w_group"] = d_wgrp
    small["pool_scale"] = d_scale
    small["norm2"] = d_norm2
    return dx0, big, dw8, small


def _train_step(a):
    t_rows = a["x"].shape[1]
    x0 = a["x"].reshape(t_rows, D)
    target = a["loss_target"].reshape(t_rows, D)

    tr = lambda w: jnp.transpose(w, (0, 2, 1))
    weights = {name: (tr(a[name]) if name in ("ffn_w_gate", "ffn_w_up") else a[name]) for name in _BIG}
    core = lax.axis_index("c").astype(jnp.int32).reshape(1)
    chip = (2 * lax.axis_index("x") + lax.axis_index("y")).astype(jnp.int32).reshape(1)
    me = 2 * chip + core
    no_token = jnp.zeros((8, 128), F32)
    row = lambda v: v.reshape(1, -1)
    rest = GROUP_MIX + GROUP_FFN
    first, dw_all = _gather_weights(_place_shards(weights, GROUP_IN, 0, me),
                                    a["conv_w_dw"].reshape(DEPTH, CONV_K, BW // NDEV), GROUP_IN)
    conv_w = jnp.transpose(dw_all, (1, 2, 0, 3)).reshape(DEPTH, CONV_K, BW)
    go_rest0 = _gather_start(_place_shards(weights, rest, 0, me), rest, dw_all, "rest0")
    going = {}

    def early_params(l, w_in):
        return dict(
            norm1=row(a["norm1"][l]), w_in=w_in,
            are=row(a["ssm_a_re"][l]), aim=row(a["ssm_a_im"][l]),
            ldt=row(jnp.repeat(a["ssm_log_dt"][l], NSTATE)),
            bexp_re=_expand_b(jnp.transpose(a["ssm_b_re"][l], (0, 2, 1))),
            bexp_im=_expand_b(jnp.transpose(a["ssm_b_im"][l], (0, 2, 1))),
            cexp_re=_expand_c(a["ssm_c_re"][l]), cexp_im=_expand_c(a["ssm_c_im"][l]),
            dskip=row(a["ssm_d"][l]), conv_w=conv_w[l], conv_b=row(a["conv_b_dw"][l]))

    def late_params(l, handle, tag, then_start):
        def get(after):
            full = _gather_finish(handle, rest, after, tag)
            token = then_start(full["ffn_w_down"]) if then_start else no_token
            return dict(
                wglu=full["ssm_w_glu"], bglu=row(a["ssm_b_glu"][l]), wpa=full["ssm_w_proj"],
                lng=row(a["conv_ln_g"][l]), lnb=row(a["conv_ln_b"][l]), wpb=full["conv_w_proj"],
                wgrp=a["pool_w_group"][l].astype(BF16), scale=row(a["pool_scale"][l]), wpc=full["pool_w_proj"],
                bgate=row(a["b_gate"][l]), wout=full["w_out"],
                norm2=row(a["norm2"][l]), wg=full["ffn_w_gate"], wu=full["ffn_w_up"], wd=full["ffn_w_down"]), token
        return get

    def start_in1(after):
        going["in1"] = _gather_start(_place_shards(weights, GROUP_IN, 1, me), GROUP_IN, after, "in1")
        return going["in1"]["token"]

    x, s0, p0 = _layer_fwd(x0, early_params(0, first["w_in"]), go_rest0["token"], late_params(0, go_rest0, "rest0", start_in1))
    w_in1 = _gather_finish(going["in1"], GROUP_IN, x, "in1")["w_in"]
    go_rest1 = _gather_start(_place_shards(weights, rest, 1, me), rest, w_in1, "rest1")
    x, s1, p1 = _layer_fwd(x, early_params(1, w_in1), go_rest1["token"], late_params(1, go_rest1, "rest1", None))
    params, saved = [p0, p1], [s0, s1]

    loss_part, dx, d_final = _loss_head(x, a["final_norm"].reshape(1, D), target)
    loss = lax.psum(loss_part[0, 0], ("x", "y", "c"))

    dx, gb1, go1, gs1 = _layer_bwd(dx, params[1], saved[1], no_token)
    part1 = _pair_reduce("late", {n: [g] for n, g in gb1.items()}, {}, {}, core)
    rs_names, rs_handle = _chip_start(part1, "late")
    part0, gone = {}, []

    def leave(tag, grads):
        part = _pair_reduce(tag, {n: [g] for n, g in grads.items()}, {}, {}, core)
        part0.update(part)
        gone.append(_chip_start(part, tag) + (tag,))
        return gone[-1][1]["token"]

    dx, gb0, go0, gs0 = _layer_bwd(dx, params[0], saved[0], rs_handle["token"], leave)
    grad_x = dx.reshape(1, t_rows, D)
    small = {n: [gs0[n], gs1[n]] for n in _SMALL}
    small["final_norm"] = [d_final]
    part_small = _pair_reduce("rest", {}, {"conv_w_dw": [go0, go1]}, small, core)
    recv = _chip_exchange(part_small, {"conv_w_dw"}, set(part_small))
    recv1 = _chip_wait(rs_names, rs_handle, dx, "late")
    for names_, handle, tag in gone:
        recv.update(_chip_wait(names_, handle, dx, tag))

    results = {}
    for name in _BIG:
        fix = tr if name in ("ffn_w_gate", "ffn_w_up") else (lambda t: t)
        res = _adam_big(name, [part0[name], part1[name]], [recv[name], recv1[name]], fix(a[name]), fix(a["m_" + name]),
                        fix(a["v_" + name]), _ADAM_TK[name], chip)
        results[name] = tuple(fix(r) for r in res)

    lay = {
        "ssm_b_re": lambda v: jnp.transpose(v, (0, 1, 3, 2)), "ssm_b_im": lambda v: jnp.transpose(v, (0, 1, 3, 2)),
        "conv_w_dw": lambda v: v.reshape(DEPTH, CONV_K, BW // NDEV), "final_norm": lambda v: v.reshape(1, 1, D),
    }
    names = _SMALL + ("conv_w_dw", "final_norm")
    relay = lambda k, v: lay[k](v) if k in lay else v
    sm = _adam_small(names, recv, {k: relay(k, a[k]) for k in names}, {k: relay(k, a["m_" + k]) for k in names},
                     {k: relay(k, a["v_" + k]) for k in names})
    for k in names:
        back = (lambda r: jnp.transpose(r, (0, 1, 3, 2))) if k in ("ssm_b_re", "ssm_b_im") else (lambda r: r.reshape(a[k].shape))
        results[k] = tuple(back(r) for r in sm[k])

    outs = [loss, grad_x]
    for part in range(4):
        outs += [results[k][part] for k in _OUT_ORDER]
    return tuple(outs)


def kernel(x, norm1, w_in, b_gate, ssm_a_re, ssm_a_im, ssm_log_dt, ssm_b_re, ssm_b_im, ssm_c_re, ssm_c_im, ssm_d, ssm_w_glu, ssm_b_glu, ssm_w_proj, conv_w_dw, conv_b_dw, conv_ln_g, conv_ln_b, conv_w_proj, pool_w_group, pool_scale, pool_w_proj, w_out, norm2, ffn_w_gate, ffn_w_up, ffn_w_down, final_norm, loss_target, m_norm1, m_w_in, m_b_gate, m_ssm_a_re, m_ssm_a_im, m_ssm_log_dt, m_ssm_b_re, m_ssm_b_im, m_ssm_c_re, m_ssm_c_im, m_ssm_d, m_ssm_w_glu, m_ssm_b_glu, m_ssm_w_proj, m_conv_w_dw, m_conv_b_dw, m_conv_ln_g, m_conv_ln_b, m_conv_w_proj, m_pool_w_group, m_pool_scale, m_pool_w_proj, m_w_out, m_norm2, m_ffn_w_gate, m_ffn_w_up, m_ffn_w_down, m_final_norm, v_norm1, v_w_in, v_b_gate, v_ssm_a_re, v_ssm_a_im, v_ssm_log_dt, v_ssm_b_re, v_ssm_b_im, v_ssm_c_re, v_ssm_c_im, v_ssm_d, v_ssm_w_glu, v_ssm_b_glu, v_ssm_w_proj, v_conv_w_dw, v_conv_b_dw, v_conv_ln_g, v_conv_ln_b, v_conv_w_proj, v_pool_w_group, v_pool_scale, v_pool_w_proj, v_w_out, v_norm2, v_ffn_w_gate, v_ffn_w_up, v_ffn_w_down, v_final_norm):
    return _train_step(dict(locals()))
```

```python
import functools

import jax
import jax.numpy as jnp
from jax import lax
from jax.experimental import pallas as pl
from jax.experimental.pallas import tpu as pltpu

F32 = jnp.float32
BF16 = jnp.bfloat16

NDEV = 8
DEPTH = 2
D = 1024
BW = 512
NSTATE = 64
SGRP = 16
NGRP = BW // SGRP
GB = 8
NBLK = NGRP // GB
NS = GB * NSTATE
CONV_K = 31
HALO = 32
PHALO = 16
IN_W = 5120
HID = 2816
HSH = HID // NDEV
HPAD = 384
HIDP = HPAD * NDEV
EPS = 1e-6
VMEM_LIMIT = 56 * 1024 * 1024

ADAM_LR, ADAM_B1, ADAM_B2, ADAM_EPS, ADAM_WD, ADAM_STEP = 0.001, 0.9, 0.999, 1e-08, 0.01, 10

MESH = pl.DeviceIdType.MESH
ANY = pl.BlockSpec(memory_space=pl.ANY)


def _call(body, **kw):
    return pl.pallas_call(body, **kw)


def _cparams(sem=None):
    return pltpu.CompilerParams(dimension_semantics=sem, vmem_limit_bytes=VMEM_LIMIT)


def _dot(a, b):
    return jnp.dot(a.astype(BF16), b.astype(BF16), preferred_element_type=F32)


def _dot_nt(a, b):
    return lax.dot_general(a.astype(BF16), b.astype(BF16), (((1,), (1,)), ((), ())), preferred_element_type=F32)


def _dot_tn(a, b):
    return lax.dot_general(a.astype(BF16), b.astype(BF16), (((0,), (0,)), ((), ())), preferred_element_type=F32)


@jax.custom_vjp
def _mm(a, w):
    return _dot(a, w)


def _mm_fwd(a, w):
    return _dot(a, w), w


def _mm_bwd(w, ct):
    return _dot_nt(ct, w), jnp.zeros_like(w)


_mm.defvjp(_mm_fwd, _mm_bwd)


def _rms(x, g):
    return x * lax.rsqrt(jnp.mean(x * x, axis=-1, keepdims=True) + EPS) * g


def _disc(are, aim, ldt):
    dt = jnp.exp(ldt)
    mag = jnp.exp(dt * are)
    ang = dt * aim
    abr = mag * jnp.cos(ang)
    abi = mag * jnp.sin(ang)
    den = are * are + aim * aim
    nr = abr - 1.0
    fr = (nr * are + abi * aim) / den
    fi = (abi * are - nr * aim) / den
    return abr, abi, fr, fi


def _bbar(fr, fi, br, bi):
    return fr * br - fi * bi, fr * bi + fi * br


def _cmul(ar, ai, br, bi):
    return ar * br - ai * bi, ar * bi + ai * br


def _scan_rows(re_ref, im_ref, ar, ai, n_rows, reverse, hre_ref=None, him_ref=None):
    n = ar.shape[1]
    shape = (8, n)
    rows = lax.broadcasted_iota(jnp.int32, shape, 0)
    a1 = (jnp.broadcast_to(ar, shape), jnp.broadcast_to(ai, shape))
    a2 = _cmul(*a1, *a1)
    a4 = _cmul(*a2, *a2)
    pr = jnp.zeros(shape, F32)
    pi = jnp.zeros(shape, F32)
    pw = a1
    for k in range(8):
        sel = rows == ((7 - k) if reverse else k)
        pr = jnp.where(sel, pw[0], pr)
        pi = jnp.where(sel, pw[1], pi)
        pw = _cmul(*pw, *a1)
    nt = n_rows // 8
    with_acc = hre_ref is not None

    def body(i, carry):
        cr, ci = carry[0], carry[1]
        t = (nt - 1 - i) if reverse else i
        off = pl.multiple_of(t * 8, 8)
        xr = re_ref[pl.ds(off, 8), :]
        xi = im_ref[pl.ds(off, 8), :]
        for k, (kr, ki) in ((1, a1), (2, a2), (4, a4)):
            if reverse:
                keep, sh = rows < 8 - k, 8 - k
            else:
                keep, sh = rows >= k, k
            sr = jnp.where(keep, pltpu.roll(xr, sh, 0), 0.0)
            si = jnp.where(keep, pltpu.roll(xi, sh, 0), 0.0)
            xr, xi = xr + kr * sr - ki * si, xi + kr * si + ki * sr
        xr, xi = xr + pr * cr - pi * ci, xi + pr * ci + pi * cr
        re_ref[pl.ds(off, 8), :] = xr
        im_ref[pl.ds(off, 8), :] = xi
        edge = 0 if reverse else 7
        out = (jnp.broadcast_to(xr[edge:edge + 1, :], shape), jnp.broadcast_to(xi[edge:edge + 1, :], shape))
        if with_acc:
            hr = hre_ref[pl.ds(off, 8), :]
            hi = him_ref[pl.ds(off, 8), :]
            offp = pl.multiple_of(jnp.maximum(t - 1, 0) * 8, 8)
            live = jnp.where(t > 0, 1.0, 0.0)
            lr = jnp.broadcast_to(hre_ref[pl.ds(offp, 8), :][7:8, :], shape) * live
            li = jnp.broadcast_to(him_ref[pl.ds(offp, 8), :][7:8, :], shape) * live
            hpr = jnp.where(rows == 0, lr, pltpu.roll(hr, 1, 0))
            hpi = jnp.where(rows == 0, li, pltpu.roll(hi, 1, 0))
            out = out + (carry[2] + xr * hpr + xi * hpi, carry[3] + xi * hpr - xr * hpi)
        return out

    zero = jnp.zeros(shape, F32)
    init = (zero, zero, zero, zero) if with_acc else (zero, zero)
    res = lax.fori_loop(0, nt, body, init)
    return res[2:] if with_acc else None


TOKEN_SPEC = pl.BlockSpec((8, 128), lambda i, j: (0, 0))


def _inproj_fwd(x, gamma, w, token, tm=1024, nb=2):
    t_rows = x.shape[0]
    tm = min(tm, t_rows)
    oc = w.shape[2]
    n = NDEV * oc
    tn = nb * oc

    def body(x_ref, g_ref, w_ref, token_ref, z_ref, h_ref):
        @pl.when(pl.program_id(1) == 0)
        def _():
            h_ref[...] = _rms(x_ref[...], g_ref[...]).astype(BF16)
        for q in range(nb):
            z_ref[:, oc * q:oc * (q + 1)] = jnp.dot(h_ref[...], w_ref[q], preferred_element_type=F32)

    return _call(
        body, name="inproj_fwd", grid=(t_rows // tm, n // tn),
        in_specs=[pl.BlockSpec((tm, D), lambda i, j: (i, 0)), pl.BlockSpec((1, D), lambda i, j: (0, 0)),
                  pl.BlockSpec((nb, D, oc), lambda i, j: (j, 0, 0)), TOKEN_SPEC],
        out_specs=[pl.BlockSpec((tm, tn), lambda i, j: (i, j)), pl.BlockSpec((tm, D), lambda i, j: (i, 0))],
        out_shape=[jax.ShapeDtypeStruct((t_rows, n), F32), jax.ShapeDtypeStruct((t_rows, D), BF16)],
        compiler_params=_cparams(("parallel", "arbitrary")))(x, gamma, w, token)


def _ssm_specs(t_rows):
    row = pl.BlockSpec((1, NS), lambda j: (0, j))
    return dict(
        u=pl.BlockSpec((t_rows, GB * SGRP), lambda j: (0, j)),
        row=row,
        bexp=pl.BlockSpec((None, GB * SGRP, NS), lambda j: (j, 0, 0)),
        cexp=pl.BlockSpec((None, NS, GB * SGRP), lambda j: (j, 0, 0)),
        d=pl.BlockSpec((1, GB * SGRP), lambda j: (0, j)),
        h=pl.BlockSpec((t_rows, NS), lambda j: (0, j)),
    )


def _ssm_fwd(z, p):
    t_rows = z.shape[0]
    s = _ssm_specs(t_rows)

    def body(u_ref, are_ref, aim_ref, ldt_ref, br_ref, bi_ref, cr_ref, ci_ref, d_ref, y_ref, hr_ref, hi_ref):
        abr, abi, fr, fi = _disc(are_ref[...], aim_ref[...], ldt_ref[...])
        bbr, bbi = _bbar(fr, fi, br_ref[...], bi_ref[...])
        u = u_ref[...]
        hr_ref[...] = _dot(u, bbr)
        hi_ref[...] = _dot(u, bbi)
        _scan_rows(hr_ref, hi_ref, abr, abi, t_rows, False)
        y_ref[...] = _dot(hr_ref[...], cr_ref[...]) - _dot(hi_ref[...], ci_ref[...]) + d_ref[...] * u

    return _call(
        body, name="ssm_fwd", grid=(NBLK,),
        in_specs=[s["u"], s["row"], s["row"], s["row"], s["bexp"], s["bexp"], s["cexp"], s["cexp"], s["d"]],
        out_specs=[s["u"], s["h"], s["h"]],
        out_shape=[jax.ShapeDtypeStruct((t_rows, BW), F32), jax.ShapeDtypeStruct((t_rows, NGRP * NSTATE), F32),
                   jax.ShapeDtypeStruct((t_rows, NGRP * NSTATE), F32)],
        compiler_params=_cparams(("parallel",)))(
            z, p["are"], p["aim"], p["ldt"], p["bexp_re"], p["bexp_im"], p["cexp_re"], p["cexp_im"], p["dskip"])


def _ssm_bwd(dy, z, hre, him, p, token):
    t_rows = z.shape[0]
    s = _ssm_specs(t_rows)
    nstates = NGRP * NSTATE

    def body(dy_ref, u_ref, hr_ref, hi_ref, are_ref, aim_ref, ldt_ref, br_ref, bi_ref, cr_ref, ci_ref, d_ref, token_ref,
             du_ref, dbr_ref, dbi_ref, dcr_ref, dci_ref, dd_ref, dar_ref, dai_ref, dldt_ref, lr_ref, li_ref):
        rows3 = (are_ref[...], aim_ref[...], ldt_ref[...])
        (abr, abi, fr, fi), disc_vjp = jax.vjp(_disc, *rows3)
        (bbr, bbi), bbar_vjp = jax.vjp(_bbar, fr, fi, br_ref[...], bi_ref[...])
        dy = dy_ref[...]
        u = u_ref[...]
        lr_ref[...] = _dot_nt(dy, cr_ref[...])
        li_ref[...] = -_dot_nt(dy, ci_ref[...])
        dcr_ref[...] = _dot_tn(hr_ref[...], dy)
        dci_ref[...] = -_dot_tn(hi_ref[...], dy)
        dd_ref[...] = jnp.sum(dy * u, axis=0, keepdims=True)
        acc_r, acc_i = _scan_rows(lr_ref, li_ref, abr, -abi, t_rows, True, hr_ref, hi_ref)
        dabr = jnp.sum(acc_r, axis=0, keepdims=True)
        dabi = jnp.sum(acc_i, axis=0, keepdims=True)
        lam_r = lr_ref[...]
        lam_i = li_ref[...]
        du = d_ref[...] * dy + _dot_nt(lam_r, bbr) + _dot_nt(lam_i, bbi)
        du_ref[...] = du.astype(BF16)
        dbbr = _dot_tn(u, lam_r)
        dbbi = _dot_tn(u, lam_i)
        dfr, dfi, dbr, dbi = bbar_vjp((dbbr, dbbi))
        dbr_ref[...] = dbr
        dbi_ref[...] = dbi
        dar, dai, dldt = disc_vjp((dabr, dabi, dfr, dfi))
        dar_ref[...] = dar
        dai_ref[...] = dai
        lane_grp = lax.broadcasted_iota(jnp.int32, (NS, 128), 0) // NSTATE
        col = lax.broadcasted_iota(jnp.int32, (NS, 128), 1)
        seg = jnp.where(lane_grp == col, 1.0, 0.0).astype(F32)
        dldt_ref[...] = jnp.dot(jnp.broadcast_to(dldt, (8, NS)), seg, preferred_element_type=F32,
                                precision=lax.Precision.HIGHEST)

    dyspec = pl.BlockSpec((t_rows, GB * SGRP), lambda j: (0, j))
    return _call(
        body, name="ssm_bwd", grid=(NBLK,),
        in_specs=[dyspec, s["u"], s["h"], s["h"], s["row"], s["row"], s["row"], s["bexp"], s["bexp"], s["cexp"],
                  s["cexp"], s["d"], pl.BlockSpec((8, 128), lambda j: (0, 0))],
        out_specs=[dyspec, s["bexp"], s["bexp"], s["cexp"], s["cexp"], s["d"], s["row"], s["row"],
                   pl.BlockSpec((8, 128), lambda j: (j, 0))],
        out_shape=[jax.ShapeDtypeStruct((t_rows, BW), BF16),
                   jax.ShapeDtypeStruct((NBLK, GB * SGRP, NS), F32), jax.ShapeDtypeStruct((NBLK, GB * SGRP, NS), F32),
                   jax.ShapeDtypeStruct((NBLK, NS, GB * SGRP), F32), jax.ShapeDtypeStruct((NBLK, NS, GB * SGRP), F32),
                   jax.ShapeDtypeStruct((1, BW), F32), jax.ShapeDtypeStruct((1, nstates), F32),
                   jax.ShapeDtypeStruct((1, nstates), F32), jax.ShapeDtypeStruct((NBLK * 8, 128), F32)],
        scratch_shapes=[pltpu.VMEM((t_rows, NS), F32), pltpu.VMEM((t_rows, NS), F32)],
        compiler_params=_cparams(("parallel",)))(
            dy, z, hre, him, p["are"], p["aim"], p["ldt"], p["bexp_re"], p["bexp_im"], p["cexp_re"], p["cexp_im"],
            p["dskip"], token)


def _conv_fwd(z, w, b, tm=256):
    t_rows = z.shape[0]
    hb = tm // HALO

    def body(va_ref, vb_ref, ha_ref, hb_ref, w_ref, b_ref, o_ref, win_ref):
        live = jnp.where(pl.program_id(0) > 0, 1.0, 0.0)
        win_ref[0:HALO, :] = ha_ref[...] * jax.nn.sigmoid(hb_ref[...]) * live
        win_ref[HALO:HALO + tm, :] = va_ref[...] * jax.nn.sigmoid(vb_ref[...])
        acc = jnp.broadcast_to(b_ref[...], (tm, BW))
        for k in range(CONV_K):
            acc = acc + w_ref[k:k + 1, :] * win_ref[pl.ds(HALO - (CONV_K - 1) + k, tm), :]
        o_ref[...] = acc

    halo = lambda col: pl.BlockSpec((HALO, BW), lambda i: (jnp.maximum(i * hb - 1, 0), col))
    return _call(
        body, name="conv_fwd", grid=(t_rows // tm,),
        in_specs=[pl.BlockSpec((tm, BW), lambda i: (i, 1)), pl.BlockSpec((tm, BW), lambda i: (i, 2)), halo(1), halo(2),
                  pl.BlockSpec((CONV_K, BW), lambda i: (0, 0)), pl.BlockSpec((1, BW), lambda i: (0, 0))],
        out_specs=pl.BlockSpec((tm, BW), lambda i: (i, 0)),
        out_shape=jax.ShapeDtypeStruct((t_rows, BW), F32),
        scratch_shapes=[pltpu.VMEM((HALO + tm, BW), F32)],
        compiler_params=_cparams(("parallel",)))(z, z, z, z, w, b)


def _conv_bwd(dcv, z, w, tm=256):
    t_rows = z.shape[0]
    nt = t_rows // tm
    hb = tm // HALO
    csh = BW // NDEV

    def body(d_ref, dn_ref, va_ref, vb_ref, ha_ref, hb_ref, w_ref, dva_ref, dvb_ref, dw8_ref, db_ref,
             hwin_ref, dwin_ref, dw_ref):
        i = pl.program_id(0)

        @pl.when(i == 0)
        def _():
            dw_ref[...] = jnp.zeros_like(dw_ref)
            db_ref[...] = jnp.zeros_like(db_ref)

        live_prev = jnp.where(i > 0, 1.0, 0.0)
        live_next = jnp.where(i < nt - 1, 1.0, 0.0)
        va = va_ref[...]
        sig = jax.nn.sigmoid(vb_ref[...])
        hwin_ref[0:HALO, :] = ha_ref[...] * jax.nn.sigmoid(hb_ref[...]) * live_prev
        hwin_ref[HALO:HALO + tm, :] = va * sig
        d = d_ref[...]
        dwin_ref[0:tm, :] = d
        dwin_ref[tm:tm + HALO, :] = dn_ref[...] * live_next
        dh = jnp.zeros((tm, BW), F32)
        dws = []
        for k in range(CONV_K):
            dh = dh + w_ref[k:k + 1, :] * dwin_ref[pl.ds(CONV_K - 1 - k, tm), :]
            dws.append(jnp.sum(d * hwin_ref[pl.ds(HALO - (CONV_K - 1) + k, tm), :], axis=0, keepdims=True))
        dws.append(jnp.zeros((1, BW), F32))
        dw_ref[...] += jnp.concatenate(dws, axis=0)
        db_ref[...] += jnp.sum(d, axis=0, keepdims=True)
        dva_ref[...] = (dh * sig).astype(BF16)
        dvb_ref[...] = (dh * va * sig * (1.0 - sig)).astype(BF16)

        @pl.when(i == nt - 1)
        def _():
            acc = dw_ref[...]
            for q in range(NDEV):
                dw8_ref[q] = acc[:, csh * q:csh * (q + 1)]

    halo = lambda col: pl.BlockSpec((HALO, BW), lambda i: (jnp.maximum(i * hb - 1, 0), col))
    return _call(
        body, name="conv_bwd", grid=(nt,),
        in_specs=[pl.BlockSpec((tm, BW), lambda i: (i, 0)),
                  pl.BlockSpec((HALO, BW), lambda i: (jnp.minimum((i + 1) * hb, t_rows // HALO - 1), 0)),
                  pl.BlockSpec((tm, BW), lambda i: (i, 1)), pl.BlockSpec((tm, BW), lambda i: (i, 2)), halo(1), halo(2),
                  pl.BlockSpec((CONV_K, BW), lambda i: (0, 0))],
        out_specs=[pl.BlockSpec((tm, BW), lambda i: (i, 0)), pl.BlockSpec((tm, BW), lambda i: (i, 0)),
                   pl.BlockSpec((NDEV, 32, csh), lambda i: (0, 0, 0)), pl.BlockSpec((1, BW), lambda i: (0, 0))],
        out_shape=[jax.ShapeDtypeStruct((t_rows, BW), BF16), jax.ShapeDtypeStruct((t_rows, BW), BF16),
                   jax.ShapeDtypeStruct((NDEV, 32, csh), F32), jax.ShapeDtypeStruct((1, BW), F32)],
        scratch_shapes=[pltpu.VMEM((HALO + tm, BW), F32), pltpu.VMEM((tm + HALO, BW), F32), pltpu.VMEM((32, BW), F32)],
        compiler_params=_cparams(("arbitrary",)))(dcv, dcv, z, z, z, z, w)


def _pool_rows(i, tm, n_rows, first_row):
    grp = lax.broadcasted_iota(jnp.int32, (1, BW), 1) // (BW // 4)
    wlen = jnp.where(grp == 0, 2.0, jnp.where(grp == 1, 4.0, jnp.where(grp == 2, 8.0, 16.0)))
    t = (i * tm + first_row + lax.broadcasted_iota(jnp.int32, (n_rows, 1), 0)).astype(F32)
    return grp, 1.0 / jnp.minimum(t + 1.0, wlen)


def _pool_pick(grp, s2, s4, s8, s16):
    return jnp.where(grp == 0, s2, jnp.where(grp == 1, s4, jnp.where(grp == 2, s8, s16)))


def _pool_fwd(z, tm=256):
    t_rows = z.shape[0]
    hb = tm // PHALO

    def body(u_ref, h_ref, o_ref):
        i = pl.program_id(0)
        u = u_ref[...]
        win = jnp.concatenate([h_ref[...] * jnp.where(i > 0, 1.0, 0.0), u], axis=0)
        s2 = win + pltpu.roll(win, 1, 0)
        s4 = s2 + pltpu.roll(s2, 2, 0)
        s8 = s4 + pltpu.roll(s4, 4, 0)
        s16 = s8 + pltpu.roll(s8, 8, 0)
        grp, inv = _pool_rows(i, tm, tm, 0)
        o_ref[...] = _pool_pick(grp, s2, s4, s8, s16)[PHALO:, :] * inv - u

    return _call(
        body, name="pool_fwd", grid=(t_rows // tm,),
        in_specs=[pl.BlockSpec((tm, BW), lambda i: (i, 3)),
                  pl.BlockSpec((PHALO, BW), lambda i: (jnp.maximum(i * hb - 1, 0), 3))],
        out_specs=pl.BlockSpec((tm, BW), lambda i: (i, 0)),
        out_shape=jax.ShapeDtypeStruct((t_rows, BW), F32),
        compiler_params=_cparams(("parallel",)))(z, z)


def _pool_bwd(dp, tm=256):
    t_rows = dp.shape[0]
    nt = t_rows // tm
    hb = tm // PHALO
    ln = tm + PHALO

    def body(d_ref, dn_ref, o_ref):
        i = pl.program_id(0)
        d = d_ref[...]
        grp, inv = _pool_rows(i, tm, ln, 0)
        win = jnp.concatenate([d, dn_ref[...] * jnp.where(i < nt - 1, 1.0, 0.0)], axis=0) * inv
        s2 = win + pltpu.roll(win, ln - 1, 0)
        s4 = s2 + pltpu.roll(s2, ln - 2, 0)
        s8 = s4 + pltpu.roll(s4, ln - 4, 0)
        s16 = s8 + pltpu.roll(s8, ln - 8, 0)
        o_ref[...] = (_pool_pick(grp, s2, s4, s8, s16)[:tm, :] - d).astype(BF16)

    return _call(
        body, name="pool_bwd", grid=(nt,),
        in_specs=[pl.BlockSpec((tm, BW), lambda i: (i, 0)),
                  pl.BlockSpec((PHALO, BW), lambda i: (jnp.minimum((i + 1) * hb, t_rows // PHALO - 1), 0))],
        out_specs=pl.BlockSpec((tm, BW), lambda i: (i, 0)),
        out_shape=jax.ShapeDtypeStruct((t_rows, BW), BF16),
        compiler_params=_cparams(("parallel",)))(dp, dp)


_MERGE_W = ("wglu", "bglu", "wpa", "lng", "lnb", "wpb", "wgrp", "scale", "wpc", "bgate", "wout")
_MERGE_SMALL = ("bglu", "lng", "lnb", "scale", "bgate")
_MERGE_BLOCKED = ("wpa", "wpb", "wpc")


def _merge_load(name, ref):
    if name in _MERGE_BLOCKED:
        return jnp.concatenate([ref[q] for q in range(NDEV)], axis=1)
    return ref[...]


def _merge_math(x, yssm, cv, pbar, zg, w, taps):
    t_glu, t_ya, t_yb, t_p, t_yc = taps
    g = jax.nn.gelu(yssm)
    outa = g * jax.nn.sigmoid(_mm(g, w["wglu"]) + t_glu + w["bglu"])
    ya = _mm(outa, w["wpa"]) + t_ya
    mu = jnp.mean(cv, axis=-1, keepdims=True)
    var = jnp.mean(jnp.square(cv - mu), axis=-1, keepdims=True)
    hs = jax.nn.silu((cv - mu) * lax.rsqrt(var + EPS) * w["lng"] + w["lnb"])
    yb = _mm(hs, w["wpb"]) + t_yb
    gw = BW // 4
    pk = jnp.concatenate([_mm(pbar[:, gw * k:gw * (k + 1)], w["wgrp"][k]) for k in range(4)], axis=1) + t_p
    pc = pk * w["scale"]
    yc = _mm(pc, w["wpc"]) + t_yc
    gates = jax.nn.sigmoid(zg + w["bgate"])
    merged = gates[:, :D] * ya + gates[:, D:2 * D] * yb + gates[:, 2 * D:] * yc
    x1 = x + _mm(merged, w["wout"])
    acts = tuple(a.astype(BF16) for a in (g, outa, hs, pbar, pc, merged))
    return x1, acts


def _merge_specs(tm, p):
    rows = lambda width, col=0: pl.BlockSpec((tm, width), lambda i, c=col: (i, c))
    data = [rows(D), rows(BW), rows(BW), rows(BW), rows(D, 2), rows(D, 3), rows(D, 4)]
    wspecs = []
    for name in _MERGE_W:
        nd = p[name].ndim
        wspecs.append(pl.BlockSpec(p[name].shape, lambda i, nd=nd: (0,) * nd))
    return rows, data, wspecs


def _merge_fwd(x, yssm, cv, pbar, z, p, token, tm=512):
    t_rows = x.shape[0]
    rows, data, wspecs = _merge_specs(tm, p)

    def body(x_ref, y_ref, cv_ref, pb_ref, za_ref, zb_ref, zc_ref, *rest):
        w = {name: _merge_load(name, r) for name, r in zip(_MERGE_W, rest[:len(_MERGE_W)])}
        o_ref = rest[len(_MERGE_W) + 1]
        taps = (0.0, 0.0, 0.0, 0.0, 0.0)
        zg = jnp.concatenate([za_ref[...], zb_ref[...], zc_ref[...]], axis=1)
        o_ref[...] = _merge_math(x_ref[...], y_ref[...], cv_ref[...], pb_ref[...], zg, w, taps)[0]

    return _call(
        body, name="merge_fwd", grid=(t_rows // tm,),
        in_specs=data + wspecs + [pl.BlockSpec((8, 128), lambda i: (0, 0))], out_specs=rows(D),
        out_shape=jax.ShapeDtypeStruct((t_rows, D), F32),
        compiler_params=_cparams(("parallel",)))(x, yssm, cv, pbar, z, z, z, *[p[n] for n in _MERGE_W], token)


def _merge_bwd(dx1, x, yssm, cv, pbar, z, p, token, tm=256):
    t_rows = x.shape[0]
    rows, data, wspecs = _merge_specs(tm, p)
    nw = len(_MERGE_W)

    def body(dx_ref, x_ref, y_ref, cv_ref, pb_ref, za_ref, zb_ref, zc_ref, *rest):
        w = {name: _merge_load(name, r) for name, r in zip(_MERGE_W, rest[:nw])}
        zg = jnp.concatenate([za_ref[...], zb_ref[...], zc_ref[...]], axis=1)
        outs = rest[nw + 1:]
        small = {n: w[n] for n in _MERGE_SMALL}
        taps = (jnp.zeros((tm, BW), F32), jnp.zeros((tm, D), F32), jnp.zeros((tm, D), F32),
                jnp.zeros((tm, BW), F32), jnp.zeros((tm, D), F32))

        def f(yssm_, cv_, pbar_, zg_, small_, taps_):
            return _merge_math(x_ref[...], yssm_, cv_, pbar_, zg_, {**w, **small_}, taps_)

        _, vjp, acts = jax.vjp(f, y_ref[...], cv_ref[...], pb_ref[...], zg, small, taps, has_aux=True)
        dy, dcv, dpb, dzg, dsmall, dtaps = vjp(dx_ref[...])
        outs[0][...] = dy
        outs[1][...] = dcv
        outs[2][...] = dpb
        outs[3][...] = dzg.astype(BF16)
        for k in range(6):
            outs[4 + k][...] = acts[k]
        for k in range(5):
            outs[10 + k][...] = dtaps[k].astype(BF16)

        @pl.when(pl.program_id(0) == 0)
        def _():
            for k in range(5):
                outs[15 + k][...] = jnp.zeros_like(outs[15 + k])

        for k, n in enumerate(_MERGE_SMALL):
            outs[15 + k][...] += dsmall[n]

    f32o = lambda width: jax.ShapeDtypeStruct((t_rows, width), F32)
    bfo = lambda width: jax.ShapeDtypeStruct((t_rows, width), BF16)
    small_shapes = [jax.ShapeDtypeStruct(p[n].shape, F32) for n in _MERGE_SMALL]
    small_specs = [pl.BlockSpec(p[n].shape, lambda i: (0, 0)) for n in _MERGE_SMALL]
    out_shape = ([f32o(BW), f32o(BW), f32o(BW), bfo(3 * D)]
                 + [bfo(BW), bfo(BW), bfo(BW), bfo(BW), bfo(BW), bfo(D)]
                 + [bfo(BW), bfo(D), bfo(D), bfo(BW), bfo(D)] + small_shapes)
    out_specs = ([rows(BW), rows(BW), rows(BW), rows(3 * D)]
                 + [rows(BW)] * 5 + [rows(D)]
                 + [rows(BW), rows(D), rows(D), rows(BW), rows(D)] + small_specs)
    return _call(
        body, name="merge_bwd", grid=(t_rows // tm,),
        in_specs=[rows(D)] + data + wspecs + [pl.BlockSpec((8, 128), lambda i: (0, 0))], out_specs=out_specs,
        out_shape=out_shape, compiler_params=_cparams(("arbitrary",)))(
            dx1, x, yssm, cv, pbar, z, z, z, *[p[n] for n in _MERGE_W], token)


def _ffn_fwd(x1, gamma, wg, wu, wd, tm=1024, th=512):
    t_rows = x1.shape[0]
    tm = min(tm, t_rows)
    nh = HIDP // th

    def body(x_ref, g_ref, wg_ref, wu_ref, wd_ref, o_ref, gp_ref, up_ref, h_ref, acc_ref):
        j = pl.program_id(1)

        @pl.when(j == 0)
        def _():
            h_ref[...] = _rms(x_ref[...], g_ref[...]).astype(BF16)
            acc_ref[...] = jnp.zeros_like(acc_ref)

        gp = _dot_nt(h_ref[...], wg_ref[...])
        up = _dot_nt(h_ref[...], wu_ref[...])
        gp_ref[...] = gp.astype(BF16)
        up_ref[...] = up.astype(BF16)
        acc_ref[...] += _dot(jax.nn.silu(gp) * up, wd_ref[...])

        @pl.when(j == nh - 1)
        def _():
            o_ref[...] = x_ref[...] + acc_ref[...]

    return _call(
        body, name="ffn_fwd", grid=(t_rows // tm, nh),
        in_specs=[pl.BlockSpec((tm, D), lambda i, j: (i, 0)), pl.BlockSpec((1, D), lambda i, j: (0, 0)),
                  pl.BlockSpec((th, D), lambda i, j: (j, 0)), pl.BlockSpec((th, D), lambda i, j: (j, 0)),
                  pl.BlockSpec((th, D), lambda i, j: (j, 0))],
        out_specs=[pl.BlockSpec((tm, D), lambda i, j: (i, 0)), pl.BlockSpec((tm, th), lambda i, j: (i, j)),
                   pl.BlockSpec((tm, th), lambda i, j: (i, j)), pl.BlockSpec((tm, D), lambda i, j: (i, 0))],
        out_shape=[jax.ShapeDtypeStruct((t_rows, D), F32), jax.ShapeDtypeStruct((t_rows, HIDP), BF16),
                   jax.ShapeDtypeStruct((t_rows, HIDP), BF16), jax.ShapeDtypeStruct((t_rows, D), BF16)],
        scratch_shapes=[pltpu.VMEM((tm, D), F32)],
        compiler_params=_cparams(("parallel", "arbitrary")))(x1, gamma, wg, wu, wd)


def _rms_bwd_tail(x, gamma, dh):
    _, vjp = jax.vjp(_rms, x, gamma)
    return vjp(dh)


def _ffn_bwd(dx2, x1, gamma, gpre, upre, wg, wu, wd, token, tm=1024, th=512):
    t_rows = x1.shape[0]
    tm = min(tm, t_rows)
    nh = HIDP // th

    def body(d_ref, x_ref, g_ref, gp_ref, up_ref, wg_ref, wu_ref, wd_ref, token_ref,
             dx_ref, dgam_ref, dgp_ref, dup_ref, act_ref, acc_ref):
        i = pl.program_id(0)
        j = pl.program_id(1)

        @pl.when(j == 0)
        def _():
            acc_ref[...] = jnp.zeros_like(acc_ref)

        @pl.when((i == 0) & (j == 0))
        def _():
            dgam_ref[...] = jnp.zeros_like(dgam_ref)

        dact = _dot_nt(d_ref[...], wd_ref[...])
        gp = gp_ref[...].astype(F32)
        up = up_ref[...].astype(F32)
        sg = jax.nn.sigmoid(gp)
        silu = gp * sg
        dgp = (dact * up * (sg * (1.0 + gp * (1.0 - sg)))).astype(BF16)
        dup = (dact * silu).astype(BF16)
        dgp_ref[...] = dgp
        dup_ref[...] = dup
        act_ref[...] = (silu * up).astype(BF16)
        acc_ref[...] += _dot(dgp, wg_ref[...]) + _dot(dup, wu_ref[...])

        @pl.when(j == nh - 1)
        def _():
            x = x_ref[...]
            dx, dgam = _rms_bwd_tail(x, g_ref[...], acc_ref[...])
            dx_ref[...] = d_ref[...] + dx
            dgam_ref[...] += dgam

    row_d = pl.BlockSpec((tm, D), lambda i, j: (i, 0))
    row_h = pl.BlockSpec((tm, th), lambda i, j: (i, j))
    return _call(
        body, name="ffn_bwd", grid=(t_rows // tm, nh),
        in_specs=[row_d, row_d, pl.BlockSpec((1, D), lambda i, j: (0, 0)), row_h, row_h,
                  pl.BlockSpec((th, D), lambda i, j: (j, 0)), pl.BlockSpec((th, D), lambda i, j: (j, 0)),
                  pl.BlockSpec((th, D), lambda i, j: (j, 0)), TOKEN_SPEC],
        out_specs=[row_d, pl.BlockSpec((1, D), lambda i, j: (0, 0)), row_h, row_h, row_h],
        out_shape=[jax.ShapeDtypeStruct((t_rows, D), F32), jax.ShapeDtypeStruct((1, D), F32),
                   jax.ShapeDtypeStruct((t_rows, HIDP), BF16), jax.ShapeDtypeStruct((t_rows, HIDP), BF16),
                   jax.ShapeDtypeStruct((t_rows, HIDP), BF16)],
        scratch_shapes=[pltpu.VMEM((tm, D), F32)],
        compiler_params=_cparams(("arbitrary", "arbitrary")))(dx2, x1, gamma, gpre, upre, wg, wu, wd, token)


def _inproj_bwd(dz, dx1, x, gamma, w, token, tm=1024, nb=2):
    t_rows = x.shape[0]
    tm = min(tm, t_rows)
    oc = w.shape[2]
    tn = nb * oc
    nn = NDEV // nb

    def body(dz_ref, d1_ref, x_ref, g_ref, w_ref, token_ref, dx_ref, dgam_ref, acc_ref):
        i = pl.program_id(0)
        j = pl.program_id(1)

        @pl.when(j == 0)
        def _():
            acc_ref[...] = jnp.zeros_like(acc_ref)

        @pl.when((i == 0) & (j == 0))
        def _():
            dgam_ref[...] = jnp.zeros_like(dgam_ref)

        for q in range(nb):
            acc_ref[...] += _dot_nt(dz_ref[:, oc * q:oc * (q + 1)], w_ref[q])

        @pl.when(j == nn - 1)
        def _():
            x = x_ref[...]
            dx, dgam = _rms_bwd_tail(x, g_ref[...], acc_ref[...])
            dx_ref[...] = d1_ref[...] + dx
            dgam_ref[...] += dgam

    row_d = pl.BlockSpec((tm, D), lambda i, j: (i, 0))
    return _call(
        body, name="inproj_bwd", grid=(t_rows // tm, nn),
        in_specs=[pl.BlockSpec((tm, tn), lambda i, j: (i, j)), row_d, row_d, pl.BlockSpec((1, D), lambda i, j: (0, 0)),
                  pl.BlockSpec((nb, D, oc), lambda i, j: (j, 0, 0)), TOKEN_SPEC],
        out_specs=[row_d, pl.BlockSpec((1, D), lambda i, j: (0, 0))],
        out_shape=[jax.ShapeDtypeStruct((t_rows, D), F32), jax.ShapeDtypeStruct((1, D), F32)],
        scratch_shapes=[pltpu.VMEM((tm, D), F32)],
        compiler_params=_cparams(("arbitrary", "arbitrary")))(dz, dx1, x, gamma, w, token)


def _matmul_tn(a, b, name, owner_cols=None, tt=2048):
    t_rows, k = a.shape
    n = b.shape[1]
    tk = min(k, 1024)
    tt = min(tt, t_rows)
    nt = t_rows // tt
    if owner_cols is None:
        tn, nb = min(n, 1024), None
        out_spec = pl.BlockSpec((tk, tn), lambda i, j, t: (i, j))
        out_shape = jax.ShapeDtypeStruct((k, n), BF16)
    else:
        nb = min(n // owner_cols, max(1, 1280 // owner_cols))
        tn = nb * owner_cols
        out_spec = pl.BlockSpec((nb, tk, owner_cols), lambda i, j, t: (j, i, 0))
        out_shape = jax.ShapeDtypeStruct((n // owner_cols, k, owner_cols), BF16)

    def body(a_ref, b_ref, o_ref, acc_ref):
        t = pl.program_id(2)

        @pl.when(t == 0)
        def _():
            acc_ref[...] = jnp.zeros_like(acc_ref)

        acc_ref[...] += _dot_tn(a_ref[...], b_ref[...])

        @pl.when(t == nt - 1)
        def _():
            if nb is None:
                o_ref[...] = acc_ref[...].astype(BF16)
            else:
                for q in range(nb):
                    o_ref[q] = acc_ref[:, owner_cols * q:owner_cols * (q + 1)].astype(BF16)

    return _call(
        body, name=name, grid=(k // tk, n // tn, nt),
        in_specs=[pl.BlockSpec((tt, tk), lambda i, j, t: (t, i)), pl.BlockSpec((tt, tn), lambda i, j, t: (t, j))],
        out_specs=out_spec, out_shape=out_shape,
        scratch_shapes=[pltpu.VMEM((tk, tn), F32)],
        compiler_params=_cparams(("parallel", "parallel", "arbitrary")))(a, b)


def _group_tn(a, b):
    t_rows = a.shape[0]
    gw = BW // 4

    def body(a_ref, b_ref, o_ref):
        o_ref[...] = _dot_tn(a_ref[...], b_ref[...])

    return _call(
        body, name="pool_group_tn", grid=(4,),
        in_specs=[pl.BlockSpec((t_rows, gw), lambda k: (0, k)), pl.BlockSpec((t_rows, gw), lambda k: (0, k))],
        out_specs=pl.BlockSpec((None, gw, gw), lambda k: (k, 0, 0)),
        out_shape=jax.ShapeDtypeStruct((4, gw, gw), F32),
        compiler_params=_cparams(("parallel",)))(a, b)


def _loss_head(x2, gamma, target, tm=512):
    t_rows = x2.shape[0]

    def body(x_ref, g_ref, t_ref, loss_ref, dx_ref, dgam_ref):
        @pl.when(pl.program_id(0) == 0)
        def _():
            loss_ref[...] = jnp.zeros_like(loss_ref)
            dgam_ref[...] = jnp.zeros_like(dgam_ref)

        def f(x, g):
            err = jnp.square(_rms(x, g) - t_ref[...])
            return 0.5 * jnp.sum(jnp.mean(err, axis=-1, keepdims=True), axis=0, keepdims=True)

        loss, vjp = jax.vjp(f, x_ref[...], g_ref[...])
        dx, dgam = vjp(jnp.ones((1, 1), F32))
        loss_ref[...] += jnp.broadcast_to(loss, (1, 128))
        dx_ref[...] = dx
        dgam_ref[...] += dgam

    row_d = pl.BlockSpec((tm, D), lambda i: (i, 0))
    return _call(
        body, name="loss_head", grid=(t_rows // tm,),
        in_specs=[row_d, pl.BlockSpec((1, D), lambda i: (0, 0)), row_d],
        out_specs=[pl.BlockSpec((1, 128), lambda i: (0, 0)), row_d, pl.BlockSpec((1, D), lambda i: (0, 0))],
        out_shape=[jax.ShapeDtypeStruct((1, 128), F32), jax.ShapeDtypeStruct((t_rows, D), F32),
                   jax.ShapeDtypeStruct((1, D), F32)],
        compiler_params=_cparams(("arbitrary",)))(x2, gamma, target)


NCHIP = NDEV // 2


def _coords():
    return lax.axis_index("x"), lax.axis_index("y"), lax.axis_index("c")


def _remote(src, dst, send_sem, recv_sem, peer):
    return pltpu.make_async_remote_copy(src_ref=src, dst_ref=dst, send_sem=send_sem, recv_sem=recv_sem,
                                        device_id=peer, device_id_type=MESH)


def _comm_call(name, srcs, out_shapes, n_rec, plan, aliases=None):
    ns, no = len(srcs), len(out_shapes)

    def body(*refs):
        ins, outs = refs[:ns], refs[ns:ns + no]
        loc_sem, send_sem, recv_sem = refs[ns + no:]
        x, y, c = _coords()
        recs = plan(ins, outs, x, y, c)
        assert len(recs) == n_rec
        for k, r in enumerate(recs):
            for src, dst in r.get("local", ()):
                pltpu.make_async_copy(src, dst, loc_sem.at[k]).start()
            for peer, src, dst in r.get("remote", ()):
                _remote(src, dst, send_sem.at[k], recv_sem.at[k], peer).start()
        for k, r in enumerate(recs):
            if r.get("recv_wait") is not None:
                w = r["recv_wait"]
                _remote(w, w, send_sem.at[k], recv_sem.at[k], (x, y, c)).wait_recv()
            if r.get("send_wait") is not None:
                w = r["send_wait"]
                _remote(w, w, send_sem.at[k], recv_sem.at[k], (x, y, c)).wait_send()
            if r.get("local_wait") is not None:
                w = r["local_wait"]
                pltpu.make_async_copy(w, w, loc_sem.at[k]).wait()

    return _call(
        body, name=name, in_specs=[ANY] * ns, out_specs=[ANY] * no, out_shape=out_shapes,
        input_output_aliases=aliases or {}, scratch_shapes=[pltpu.SemaphoreType.DMA((n_rec,))] * 3)(*srcs)


def _gather_call(srcs, out_shapes, items, aliases):
    ns, no, n = len(srcs), len(out_shapes), len(items)

    def body(*refs):
        ins, outs = refs[:ns], refs[ns:ns + no]
        loc, sib_s, sib_r, ici_s, ici_r, fwd_s, fwd_r = refs[ns + no:]
        x, y, c = _coords()
        me, sib = (x, y, c), (x, y, 1 - c)
        chips = [(1 - x, y), (x, 1 - y), (1 - x, 1 - y)]
        index = lambda px, py, pc: 4 * px + 2 * py + pc
        for k, (si, oi, shard, block, _) in enumerate(items):
            mine = block(outs[oi], index(*me))
            src = mine if shard is None else shard(ins[si])
            if shard is not None:
                pltpu.make_async_copy(src, mine, loc.at[k]).start()
            _remote(src, mine, sib_s.at[k], sib_r.at[k], sib).start()
            for chip in chips:
                _remote(src, mine, ici_s.at[k], ici_r.at[k], (*chip, c)).start()
        for k, (si, oi, _, block, blocks) in enumerate(items):
            three = blocks(outs[oi], 3)
            _remote(three, three, ici_s.at[k], ici_r.at[k], me).wait_recv()
            for chip in chips:
                landed = block(outs[oi], index(*chip, c))
                _remote(landed, landed, fwd_s.at[k], fwd_r.at[k], sib).start()
        for k, (si, oi, shard, _, blocks) in enumerate(items):
            one, three = blocks(outs[oi], 1), blocks(outs[oi], 3)
            _remote(one, one, sib_s.at[k], sib_r.at[k], me).wait()
            _remote(three, three, fwd_s.at[k], fwd_r.at[k], me).wait()
            _remote(three, three, ici_s.at[k], ici_r.at[k], me).wait_send()
            if shard is not None:
                pltpu.make_async_copy(one, one, loc.at[k]).wait()

    return _call(
        body, name="gather_weights", in_specs=[ANY] * ns, out_specs=[ANY] * no, out_shape=out_shapes,
        input_output_aliases=aliases, scratch_shapes=[pltpu.SemaphoreType.DMA((n,))] * 7)(*srcs)


_BIG = {
    "w_in": (True, D, IN_W // NDEV),
    "ssm_w_glu": (False, BW // NDEV, BW),
    "ssm_w_proj": (True, BW, D // NDEV),
    "conv_w_proj": (True, BW, D // NDEV),
    "pool_w_proj": (True, BW, D // NDEV),
    "w_out": (False, D // NDEV, D),
    "ffn_w_gate": (False, HPAD, D),
    "ffn_w_up": (False, HPAD, D),
    "ffn_w_down": (False, HPAD, D),
}
GROUP_IN = ("w_in",)
GROUP_MIX = ("ssm_w_glu", "ssm_w_proj", "conv_w_proj", "pool_w_proj", "w_out")
GROUP_FFN = ("ffn_w_gate", "ffn_w_up", "ffn_w_down")


def _gathered_shape(name):
    blocked, kk, nn = _BIG[name]
    return jax.ShapeDtypeStruct((NDEV, kk, nn) if blocked else (NDEV * kk, nn), BF16)


def _block_view(name):
    blocked, kk, _ = _BIG[name]
    if blocked:
        return lambda ref, q: ref.at[q]
    return lambda ref, q: ref.at[pl.ds(pl.multiple_of(q * kk, 16), kk), :]


def _blocks_view(name):
    blocked, kk, _ = _BIG[name]
    if blocked:
        return lambda ref, n: ref.at[pl.ds(0, n)]
    return lambda ref, n: ref.at[pl.ds(0, n * kk), :]


def _place_shards(weights, names, layer, me):
    cnt = len(names)

    def body(me_ref, *refs):
        w_refs, outs, stages, sem = refs[:cnt], refs[cnt:2 * cnt], refs[2 * cnt:3 * cnt], refs[3 * cnt]
        q = me_ref[0]
        copies = []
        for k, name in enumerate(names):
            _, kk, nn = _BIG[name]
            rows = w_refs[k].shape[0]
            stages[k][0:rows, :] = w_refs[k][...].astype(BF16)
            if rows < kk:
                stages[k][rows:kk, :] = jnp.zeros((kk - rows, nn), BF16)
            copies.append(pltpu.make_async_copy(stages[k], _block_view(name)(outs[k], q), sem.at[k]))
            copies[-1].start()
        for cp in copies:
            cp.wait()

    in_specs = [pl.BlockSpec(memory_space=pltpu.SMEM)]
    in_specs += [pl.BlockSpec((None,) + weights[n].shape[1:], lambda i: (layer, 0, 0)) for n in names]
    res = _call(
        body, name="place_shards", grid=(1,), in_specs=in_specs, out_specs=[ANY] * cnt,
        out_shape=[_gathered_shape(n) for n in names],
        scratch_shapes=[pltpu.VMEM(_BIG[n][1:], BF16) for n in names] + [pltpu.SemaphoreType.DMA((cnt,))],
        compiler_params=_cparams(("arbitrary",)))(me, *[weights[n] for n in names])
    return dict(zip(names, res))


def _gather_weights(placed, conv_dw, names):
    cnt = len(names)
    srcs = [placed[n] for n in names] + [conv_dw]
    outs = [_gathered_shape(n) for n in names] + [jax.ShapeDtypeStruct((NDEV,) + conv_dw.shape, conv_dw.dtype)]
    items = [(k, k, None, _block_view(n), _blocks_view(n)) for k, n in enumerate(names)]
    items.append((cnt, cnt, lambda ref: ref, lambda ref, q: ref.at[q], lambda ref, n: ref.at[pl.ds(0, n)]))
    res = _gather_call(srcs, outs, items, {k: k for k in range(cnt)})
    return dict(zip(names, res[:-1])), res[-1]


def _gather_start(placed, names, after, tag):
    def copies(src_refs, land_refs, x, y, c):
        me = 4 * x + 2 * y + c
        peers = [(x, y, 1 - c), (1 - x, y, c), (x, 1 - y, c), (1 - x, 1 - y, c)]
        out = []
        for k, n in enumerate(names):
            mine = _block_view(n)(land_refs[k], me)
            out.append([(peer, mine, mine) for peer in peers])
        return out

    return _split_start("gather_start_" + tag, [], [placed[n] for n in names], copies, after)


def _gather_finish(handle, names, after, tag):
    four = [functools.partial(lambda ref, bv: bv(ref, 4), bv=_blocks_view(n)) for n in names]
    lands = _split_wait("gather_wait_" + tag, handle, four, after)
    cnt = len(names)

    def plan(ins, out_refs, x, y, c):
        sib = (x, y, 1 - c)
        recs = []
        for k, n in enumerate(names):
            three = _blocks_view(n)(out_refs[k], 3)
            remote = []
            for px, py in [(1 - x, y), (x, 1 - y), (1 - x, 1 - y)]:
                landed = _block_view(n)(out_refs[k], 4 * px + 2 * py + c)
                remote.append((sib, landed, landed))
            recs.append(dict(remote=remote, send_wait=three, recv_wait=three))
        return recs

    res = _comm_call("gather_pair_" + tag, lands, [jax.ShapeDtypeStruct(l.shape, l.dtype) for l in lands], cnt, plan,
                     aliases={k: k for k in range(cnt)})
    return dict(zip(names, res))


def _pair_add(name, grads, rcv, core):
    nl = len(grads)
    _, kk, nn = grads[0].shape

    def body(c_ref, *refs):
        l = pl.program_id(0)
        own = refs[0][...]
        for j in range(1, nl):
            own = jnp.where(l == j, refs[j][...], own)
        refs[nl + 1][...] = (own.astype(F32) + refs[nl][...].astype(F32)).astype(BF16)

    gspec = lambda j: pl.BlockSpec((None, kk, nn), lambda l, h, c_ref: (jnp.where(l == j, 2 * h + c_ref[0], 0), 0, 0))
    rspec = pl.BlockSpec((None, None, kk, nn), lambda l, h, c_ref: (h, l, 0, 0))
    return _call(
        body, name="pair_add_" + name,
        grid_spec=pltpu.PrefetchScalarGridSpec(num_scalar_prefetch=1, grid=(nl, NCHIP),
                                               in_specs=[gspec(j) for j in range(nl)] + [rspec], out_specs=rspec),
        out_shape=jax.ShapeDtypeStruct(rcv.shape, BF16),
        compiler_params=_cparams(("arbitrary", "arbitrary")))(core, *grads, rcv)


def _pair_add_small(owned, lists, core):
    on, ln = list(owned), list(lists)
    flat = []
    for n in on:
        flat += list(owned[n][0]) + [owned[n][1]]
    for n in ln:
        flat += list(lists[n][0]) + [lists[n][1]]

    def body(c_ref, *refs):
        outs = refs[len(flat):]
        c = c_ref[0]
        pos = 0
        for k, n in enumerate(on):
            nl = len(owned[n][0])
            for h in range(NCHIP):
                for l in range(nl):
                    outs[k][h, l] = refs[pos + l][pl.ds(2 * h + c, 1)][0] + refs[pos + nl][h, l]
            pos += nl + 1
        for k, n in enumerate(ln):
            nl = len(lists[n][0])
            for l in range(nl):
                out = outs[len(on) + k]
                out[l] = (refs[pos + l][...] + refs[pos + nl][l]).astype(out.dtype)
            pos += nl + 1

    shapes = [jax.ShapeDtypeStruct(owned[n][1].shape, F32) for n in on]
    shapes += [jax.ShapeDtypeStruct(lists[n][1].shape, F32 if n == "final_norm" else BF16) for n in ln]
    res = _call(body, name="pair_add_small", out_shape=shapes,
                in_specs=[pl.BlockSpec(memory_space=pltpu.SMEM)] + [pl.BlockSpec(memory_space=pltpu.VMEM)] * len(flat),
                compiler_params=_cparams())(core, *flat)
    return dict(zip(on + ln, res))


def _pair_reduce(tag, big, by_owner, small, core):
    rs = {**big, **by_owner}
    srcs, outs, plans, rcv_at = [], [], [], {}
    for name, arrays in rs.items():
        rcv_at[name] = len(outs)
        outs.append(jax.ShapeDtypeStruct((NCHIP, len(arrays)) + arrays[0].shape[1:], arrays[0].dtype))
        for l, arr in enumerate(arrays):
            srcs.append(arr)
            plans.append((len(srcs) - 1, rcv_at[name], l, True))
    for name, arrays in small.items():
        rcv_at[name] = len(outs)
        outs.append(jax.ShapeDtypeStruct((len(arrays),) + arrays[0].shape, F32))
        for l, arr in enumerate(arrays):
            srcs.append(arr)
            plans.append((len(srcs) - 1, rcv_at[name], l, False))

    def plan_pair(ins, out_refs, x, y, c):
        sib = (x, y, 1 - c)
        recs = []
        for si, ro, l, slabs in plans:
            if slabs:
                four = out_refs[ro].at[pl.ds(0, NCHIP), l]
                recs.append(dict(remote=[(sib, ins[si].at[2 * h + 1 - c], out_refs[ro].at[h, l]) for h in range(NCHIP)],
                                 send_wait=four, recv_wait=four))
            else:
                dst = out_refs[ro].at[l]
                recs.append(dict(remote=[(sib, ins[si], dst)], send_wait=dst, recv_wait=dst))
        return recs

    res = _comm_call("pair_exchange_" + tag, srcs, outs, len(plans), plan_pair)
    part = {name: _pair_add(name, big[name], res[rcv_at[name]], core) for name in big}
    if by_owner or small:
        part.update(_pair_add_small({n: (by_owner[n], res[rcv_at[n]]) for n in by_owner},
                                    {n: (small[n], res[rcv_at[n]]) for n in small}, core))
    return part


def _chip_copies(src, land, slabbed, x, y, c):
    mine = 2 * x + y
    copies = []
    for step in range(1, NCHIP):
        h = (mine + step) % NCHIP
        copies.append(((h // 2, h % 2, c), src.at[h] if slabbed else src, land.at[mine]))
    return copies


def _chip_exchange(part, slabbed, keep_own):
    names = list(part)
    outs = [jax.ShapeDtypeStruct((() if n in slabbed else (NCHIP,)) + part[n].shape, part[n].dtype) for n in names]

    def plan(ins, out_refs, x, y, c):
        mine = 2 * x + y
        recs = []
        for k, n in enumerate(names):
            three = out_refs[k].at[pl.ds(0, NCHIP - 1)]
            rec = dict(remote=_chip_copies(ins[k], out_refs[k], n in slabbed, x, y, c), send_wait=three, recv_wait=three)
            if n in keep_own:
                rec["local"] = [(ins[k].at[mine] if n in slabbed else ins[k], out_refs[k].at[mine])]
                rec["local_wait"] = out_refs[k].at[0]
            recs.append(rec)
        return recs

    res = _comm_call("chip_exchange", [part[n] for n in names], outs, len(names), plan)
    return dict(zip(names, res))


HBM_SPEC = pl.BlockSpec(memory_space=pltpu.HBM)
SEM_SPEC = pl.BlockSpec(memory_space=pltpu.SEMAPHORE)
SPLIT_EFFECT = pltpu.SideEffectType.DATAFLOW_SIDE_EFFECTING


def _split_start(name, srcs, land_shapes, copies_fn, after):
    ns, n = len(srcs), len(land_shapes)
    lands = [pltpu.with_memory_space_constraint(s if isinstance(s, jax.Array) else lax.empty(s.shape, s.dtype), pltpu.HBM)
             for s in land_shapes]

    def body(*refs):
        src_refs, land_refs = refs[:ns], refs[ns:ns + n]
        send_sem, recv_sem = refs[ns + n + 1], refs[ns + n + 2]
        token = refs[-1]
        x, y, c = _coords()
        for k, copies in enumerate(copies_fn(src_refs, land_refs, x, y, c)):
            for peer, src, dst in copies:
                _remote(src, dst, send_sem.at[k], recv_sem.at[k], peer).start()
        token[...] = jnp.zeros_like(token)

    res = pl.pallas_call(
        body, name=name,
        out_shape=(pltpu.SemaphoreType.DMA((n,)), pltpu.SemaphoreType.DMA((n,)),
                   *[pltpu.HBM(s.shape, s.dtype) for s in land_shapes], jax.ShapeDtypeStruct((8, 128), F32)),
        in_specs=[HBM_SPEC] * (ns + n) + [ANY],
        out_specs=(SEM_SPEC, SEM_SPEC, *[HBM_SPEC] * n, pl.BlockSpec(memory_space=pltpu.VMEM)),
        input_output_aliases={ns + i: 2 + i for i in range(n)},
        compiler_params=pltpu.CompilerParams(has_side_effects=SPLIT_EFFECT),
    )(*[pltpu.with_memory_space_constraint(s, pltpu.HBM) for s in srcs], *lands, after)
    return dict(send=res[0], recv=res[1], srcs=list(srcs), lands=list(res[2:2 + n]), token=res[-1])


def _split_wait(name, handle, wait_views, after):
    n = len(handle["lands"])
    after = after if isinstance(after, (tuple, list)) else (after,)

    def body(*refs):
        land_refs = refs[:n]
        send_sem, recv_sem = refs[n], refs[n + 1]
        x, y, c = _coords()
        for k in range(n):
            w = wait_views[k](land_refs[k])
            cp = _remote(w, w, send_sem.at[k], recv_sem.at[k], (x, y, c))
            cp.wait_send()
            cp.wait_recv()

    res = pl.pallas_call(
        body, name=name,
        out_shape=tuple(pltpu.HBM(s.shape, s.dtype) for s in handle["lands"]),
        in_specs=[HBM_SPEC] * n + [SEM_SPEC, SEM_SPEC] + [ANY] * len(after), out_specs=tuple([HBM_SPEC] * n),
        input_output_aliases={i: i for i in range(n)},
        compiler_params=pltpu.CompilerParams(has_side_effects=SPLIT_EFFECT),
    )(*handle["lands"], handle["send"], handle["recv"], *after)
    return list(res)


def _adamw(w, g, m, v):
    m = ADAM_B1 * m + (1.0 - ADAM_B1) * g
    v = ADAM_B2 * v + (1.0 - ADAM_B2) * jnp.square(g)
    m_hat = m / (1.0 - ADAM_B1 ** ADAM_STEP)
    v_hat = v / (1.0 - ADAM_B2 ** ADAM_STEP)
    delta = -ADAM_LR * (m_hat / (jnp.sqrt(v_hat) + ADAM_EPS) + ADAM_WD * w)
    return delta, m, v


def _chip_start(part, tag):
    names = list(part)

    def copies(src_refs, land_refs, x, y, c):
        return [_chip_copies(src_refs[k], land_refs[k], True, x, y, c) for k in range(len(names))]

    shapes = [jax.ShapeDtypeStruct(part[n].shape, part[n].dtype) for n in names]
    return names, _split_start("chip_start_" + tag, [part[n] for n in names], shapes, copies, part[names[0]])


def _chip_wait(names, handle, after, tag):
    three = [lambda ref: ref.at[pl.ds(0, NCHIP - 1)]] * len(names)
    return dict(zip(names, _split_wait("chip_wait_" + tag, handle, three, after)))


def _sum_senders(ref):
    g = ref[0].astype(F32)
    for h in range(1, NCHIP):
        g = g + ref[h].astype(F32)
    return g


def _adam_big(name, own, recv, w, m, v, tk, chip):
    nl = len(own)
    kk, nn = w.shape[1], w.shape[2]
    nnp = own[0].shape[3]

    def body(chip_ref, *refs):
        l = pl.program_id(0)
        g = None
        for step in range(NCHIP):
            val = refs[step][...]
            for q in range(1, nl):
                val = jnp.where(l == q, refs[NCHIP * q + step][...], val)
            g = val.astype(F32) if g is None else g + val.astype(F32)
        w_ref, m_ref, v_ref, g_ref, d_ref, mo_ref, vo_ref = refs[NCHIP * nl:]
        g = g[:, :nn]
        delta, m2, v2 = _adamw(w_ref[...], g, m_ref[...], v_ref[...])
        g_ref[...] = g
        d_ref[...] = delta
        mo_ref[...] = m2
        vo_ref[...] = v2

    def slab(q, step):
        return pl.BlockSpec((None, None, tk, nnp), lambda l, i, chip_ref: (
            jnp.where(l == q, (chip_ref[0] + step) % NCHIP, 0), 0, jnp.where(l == q, i, 0), 0))

    in_specs, operands = [], []
    for q in range(nl):
        in_specs += [slab(q, step) for step in range(NCHIP)]
        operands += [own[q]] + [recv[q]] * (NCHIP - 1)
    wspec = pl.BlockSpec((None, tk, nn), lambda l, i, chip_ref: (l, i, 0))
    shape = jax.ShapeDtypeStruct(w.shape, F32)
    return _call(
        body, name="adamw_" + name,
        grid_spec=pltpu.PrefetchScalarGridSpec(num_scalar_prefetch=1, grid=(nl, kk // tk),
                                               in_specs=in_specs + [wspec] * 3, out_specs=[wspec] * 4),
        out_shape=[shape] * 4, compiler_params=_cparams(("arbitrary", "arbitrary")))(chip, *operands, w, m, v)


def _adam_small(names, recv, w, m, v):
    n = len(names)

    def body(*refs):
        r, ww, mm, vv = refs[:n], refs[n:2 * n], refs[2 * n:3 * n], refs[3 * n:4 * n]
        outs = refs[4 * n:]
        for k in range(n):
            for l in range(r[k].shape[1]):
                g = r[k][0, l].astype(F32)
                for h in range(1, NCHIP):
                    g = g + r[k][h, l].astype(F32)
                per_layer = ww[k].shape[1:]
                if g.shape[0] > per_layer[0] and g.shape[1:] == per_layer[1:]:
                    g = g[:per_layer[0]]
                at = l if g.shape == per_layer else pl.ds(l, 1)
                delta, m2, v2 = _adamw(ww[k][at], g, mm[k][at], vv[k][at])
                outs[k][at] = g
                outs[n + k][at] = delta
                outs[2 * n + k][at] = m2
                outs[3 * n + k][at] = v2

    shapes = [jax.ShapeDtypeStruct(w[k].shape, F32) for k in names]
    res = _call(body, name="adamw_small", out_shape=shapes * 4, compiler_params=_cparams())(
        *[recv[k] for k in names], *[w[k] for k in names], *[m[k] for k in names], *[v[k] for k in names])
    return {k: (res[i], res[n + i], res[2 * n + i], res[3 * n + i]) for i, k in enumerate(names)}


def _expand_b(bt):
    eye = jnp.eye(GB, dtype=bt.dtype)
    return jnp.einsum("jgpn,gh->jgphn", bt.reshape(NBLK, GB, SGRP, NSTATE), eye).reshape(NBLK, GB * SGRP, NS)


def _extract_b(db):
    x = db.reshape(NBLK, GB, SGRP, GB, NSTATE)
    eye = jnp.eye(GB, dtype=db.dtype)
    return jnp.einsum("jgphn,gh->jgpn", x, eye).reshape(NGRP, SGRP, NSTATE)


def _expand_c(c):
    ct = jnp.transpose(c, (0, 2, 1)).reshape(NBLK, GB, NSTATE, SGRP)
    eye = jnp.eye(GB, dtype=c.dtype)
    return jnp.einsum("jgnp,gh->jgnhp", ct, eye).reshape(NBLK, NS, GB * SGRP)


def _extract_c(dc):
    x = dc.reshape(NBLK, GB, NSTATE, GB, SGRP)
    eye = jnp.eye(GB, dtype=dc.dtype)
    d = jnp.einsum("jgnhp,gh->jgnp", x, eye).reshape(NGRP, NSTATE, SGRP)
    return jnp.transpose(d, (0, 2, 1))


_SMALL = ("norm1", "b_gate", "ssm_a_re", "ssm_a_im", "ssm_log_dt", "ssm_b_re", "ssm_b_im", "ssm_c_re", "ssm_c_im",
          "ssm_d", "ssm_b_glu", "conv_b_dw", "conv_ln_g", "conv_ln_b", "pool_w_group", "pool_scale", "norm2")
_ADAM_TK = {"w_in": 256, "ssm_w_glu": 64, "ssm_w_proj": 512, "conv_w_proj": 512, "pool_w_proj": 512, "w_out": 128,
            "ffn_w_gate": HSH, "ffn_w_up": HSH, "ffn_w_down": HSH}
_OUT_ORDER = ("norm1", "w_in", "b_gate", "ssm_a_re", "ssm_a_im", "ssm_log_dt", "ssm_b_re", "ssm_b_im", "ssm_c_re",
              "ssm_c_im", "ssm_d", "ssm_w_glu", "ssm_b_glu", "ssm_w_proj", "conv_w_dw", "conv_b_dw", "conv_ln_g",
              "conv_ln_b", "conv_w_proj", "pool_w_group", "pool_scale", "pool_w_proj", "w_out", "norm2", "ffn_w_gate",
              "ffn_w_up", "ffn_w_down", "final_norm")


def _layer_fwd(x, p, token, late_params=None):
    z, h = _inproj_fwd(x, p["norm1"], p["w_in"], token)
    yssm, hre, him = _ssm_fwd(z, p)
    cv = _conv_fwd(z, p["conv_w"], p["conv_b"])
    pbar = _pool_fwd(z)
    if late_params is not None:
        more, token = late_params((yssm, cv, pbar))
        p = {**p, **more}
    x1 = _merge_fwd(x, yssm, cv, pbar, z, p, token)
    x2, gpre, upre, h2 = _ffn_fwd(x1, p["norm2"], p["wg"], p["wu"], p["wd"])
    return x2, dict(x=x, h=h, z=z, yssm=yssm, hre=hre, him=him, cv=cv, pbar=pbar, x1=x1, gpre=gpre, upre=upre, h2=h2), p


def _layer_bwd(dx, p, s, token, leave=None):
    big, small = {}, {}
    go = (lambda tag, names: leave(tag, {n: big[n] for n in names})) if leave else (lambda tag, names: token)
    h2 = s["h2"]
    dx1, d_norm2, dgp, dup, act = _ffn_bwd(dx, s["x1"], p["norm2"], s["gpre"], s["upre"], p["wg"], p["wu"], p["wd"],
                                               token)
    big["ffn_w_gate"] = _matmul_tn(dgp, h2, "tn_gate").reshape(NDEV, HPAD, D)
    big["ffn_w_up"] = _matmul_tn(dup, h2, "tn_up").reshape(NDEV, HPAD, D)
    big["ffn_w_down"] = _matmul_tn(act, dx, "tn_down").reshape(NDEV, HPAD, D)
    (dy, dcv, dpb, dzg, a_g, a_outa, a_hs, a_pb, a_pc, a_mg, c_glu, c_ya, c_yb, c_p, c_yc,
     d_bglu, d_lng, d_lnb, d_scale, d_bgate) = _merge_bwd(dx1, s["x"], s["yssm"], s["cv"], s["pbar"], s["z"], p,
                                                          go("ffn", GROUP_FFN))
    big["ssm_w_glu"] = _matmul_tn(a_g, c_glu, "tn_glu").reshape(NDEV, BW // NDEV, BW)
    big["ssm_w_proj"] = _matmul_tn(a_outa, c_ya, "tn_ssm_proj", D // NDEV)
    big["conv_w_proj"] = _matmul_tn(a_hs, c_yb, "tn_conv_proj", D // NDEV)
    big["pool_w_proj"] = _matmul_tn(a_pc, c_yc, "tn_pool_proj", D // NDEV)
    big["w_out"] = _matmul_tn(a_mg, dx1, "tn_out").reshape(NDEV, D // NDEV, D)
    d_wgrp = _group_tn(a_pb, c_p)
    du_a, dbr, dbi, dcr, dci, dd, dar, dai, dldt = _ssm_bwd(dy, s["z"], s["hre"], s["him"], p, go("mix", GROUP_MIX))
    dva, dvb, dw8, dcb = _conv_bwd(dcv, s["z"], p["conv_w"])
    du_c = _pool_bwd(dpb)
    dz = jnp.concatenate([du_a, dva, dvb, du_c, dzg], axis=1)
    big["w_in"] = _matmul_tn(s["h"], dz, "tn_in", IN_W // NDEV)
    dx0, d_norm1 = _inproj_bwd(dz, dx1, s["x"], p["norm1"], p["w_in"], go("in", GROUP_IN))
    small["norm1"] = d_norm1
    small["b_gate"] = d_bgate
    small["ssm_a_re"] = dar.reshape(NGRP, NSTATE)
    small["ssm_a_im"] = dai.reshape(NGRP, NSTATE)
    small["ssm_log_dt"] = dldt.reshape(NBLK, 8, 128)[:, 0, :GB].reshape(1, NGRP)
    small["ssm_b_re"] = _extract_b(dbr)
    small["ssm_b_im"] = _extract_b(dbi)
    small["ssm_c_re"] = _extract_c(dcr)
    small["ssm_c_im"] = _extract_c(dci)
    small["ssm_d"] = dd.reshape(NGRP, SGRP)
    small["ssm_b_glu"] = d_bglu
    small["conv_b_dw"] = dcb
    small["conv_ln_g"] = d_lng
    small["conv_ln_b"] = d_lnb
    small["pool_w_group"] = d_wgrp
    small["pool_scale"] = d_scale
    small["norm2"] = d_norm2
    return dx0, big, dw8, small


def _train_step(a):
    t_rows = a["x"].shape[1]
    x0 = a["x"].reshape(t_rows, D)
    target = a["loss_target"].reshape(t_rows, D)

    tr = lambda w: jnp.transpose(w, (0, 2, 1))
    weights = {name: (tr(a[name]) if name in ("ffn_w_gate", "ffn_w_up") else a[name]) for name in _BIG}
    core = lax.axis_index("c").astype(jnp.int32).reshape(1)
    chip = (2 * lax.axis_index("x") + lax.axis_index("y")).astype(jnp.int32).reshape(1)
    me = 2 * chip + core
    no_token = jnp.zeros((8, 128), F32)
    row = lambda v: v.reshape(1, -1)
    rest = GROUP_MIX + GROUP_FFN
    first, dw_all = _gather_weights(_place_shards(weights, GROUP_IN, 0, me),
                                    a["conv_w_dw"].reshape(DEPTH, CONV_K, BW // NDEV), GROUP_IN)
    conv_w = jnp.transpose(dw_all, (1, 2, 0, 3)).reshape(DEPTH, CONV_K, BW)
    go_rest0 = _gather_start(_place_shards(weights, rest, 0, me), rest, dw_all, "rest0")
    going = {}

    def early_params(l, w_in):
        return dict(
            norm1=row(a["norm1"][l]), w_in=w_in,
            are=row(a["ssm_a_re"][l]), aim=row(a["ssm_a_im"][l]),
            ldt=row(jnp.repeat(a["ssm_log_dt"][l], NSTATE)),
            bexp_re=_expand_b(jnp.transpose(a["ssm_b_re"][l], (0, 2, 1))),
            bexp_im=_expand_b(jnp.transpose(a["ssm_b_im"][l], (0, 2, 1))),
            cexp_re=_expand_c(a["ssm_c_re"][l]), cexp_im=_expand_c(a["ssm_c_im"][l]),
            dskip=row(a["ssm_d"][l]), conv_w=conv_w[l], conv_b=row(a["conv_b_dw"][l]))

    def late_params(l, handle, tag, then_start):
        def get(after):
            full = _gather_finish(handle, rest, after, tag)
            token = then_start(full["ffn_w_down"]) if then_start else no_token
            return dict(
                wglu=full["ssm_w_glu"], bglu=row(a["ssm_b_glu"][l]), wpa=full["ssm_w_proj"],
                lng=row(a["conv_ln_g"][l]), lnb=row(a["conv_ln_b"][l]), wpb=full["conv_w_proj"],
                wgrp=a["pool_w_group"][l].astype(BF16), scale=row(a["pool_scale"][l]), wpc=full["pool_w_proj"],
                bgate=row(a["b_gate"][l]), wout=full["w_out"],
                norm2=row(a["norm2"][l]), wg=full["ffn_w_gate"], wu=full["ffn_w_up"], wd=full["ffn_w_down"]), token
        return get

    def start_in1(after):
        going["in1"] = _gather_start(_place_shards(weights, GROUP_IN, 1, me), GROUP_IN, after, "in1")
        return going["in1"]["token"]

    x, s0, p0 = _layer_fwd(x0, early_params(0, first["w_in"]), go_rest0["token"], late_params(0, go_rest0, "rest0", start_in1))
    w_in1 = _gather_finish(going["in1"], GROUP_IN, x, "in1")["w_in"]
    go_rest1 = _gather_start(_place_shards(weights, rest, 1, me), rest, w_in1, "rest1")
    x, s1, p1 = _layer_fwd(x, early_params(1, w_in1), go_rest1["token"], late_params(1, go_rest1, "rest1", None))
    params, saved = [p0, p1], [s0, s1]

    loss_part, dx, d_final = _loss_head(x, a["final_norm"].reshape(1, D), target)
    loss = lax.psum(loss_part[0, 0], ("x", "y", "c"))

    dx, gb1, go1, gs1 = _layer_bwd(dx, params[1], saved[1], no_token)
    part1 = _pair_reduce("late", {n: [g] for n, g in gb1.items()}, {}, {}, core)
    rs_names, rs_handle = _chip_start(part1, "late")
    part0, gone = {}, []

    def leave(tag, grads):
        part = _pair_reduce(tag, {n: [g] for n, g in grads.items()}, {}, {}, core)
        part0.update(part)
        gone.append(_chip_start(part, tag) + (tag,))
        return gone[-1][1]["token"]

    dx, gb0, go0, gs0 = _layer_bwd(dx, params[0], saved[0], rs_handle["token"], leave)
    grad_x = dx.reshape(1, t_rows, D)
    small = {n: [gs0[n], gs1[n]] for n in _SMALL}
    small["final_norm"] = [d_final]
    part_small = _pair_reduce("rest", {}, {"conv_w_dw": [go0, go1]}, small, core)
    recv = _chip_exchange(part_small, {"conv_w_dw"}, set(part_small))
    recv1 = _chip_wait(rs_names, rs_handle, dx, "late")
    for names_, handle, tag in gone:
        recv.update(_chip_wait(names_, handle, dx, tag))

    results = {}
    for name in _BIG:
        fix = tr if name in ("ffn_w_gate", "ffn_w_up") else (lambda t: t)
        res = _adam_big(name, [part0[name], part1[name]], [recv[name], recv1[name]], fix(a[name]), fix(a["m_" + name]),
                        fix(a["v_" + name]), _ADAM_TK[name], chip)
        results[name] = tuple(fix(r) for r in res)

    lay = {
        "ssm_b_re": lambda v: jnp.transpose(v, (0, 1, 3, 2)), "ssm_b_im": lambda v: jnp.transpose(v, (0, 1, 3, 2)),
        "conv_w_dw": lambda v: v.reshape(DEPTH, CONV_K, BW // NDEV), "final_norm": lambda v: v.reshape(1, 1, D),
    }
    names = _SMALL + ("conv_w_dw", "final_norm")
    relay = lambda k, v: lay[k](v) if k in lay else v
    sm = _adam_small(names, recv, {k: relay(k, a[k]) for k in names}, {k: relay(k, a["m_" + k]) for k in names},
                     {k: relay(k, a["v_" + k]) for k in names})
    for k in names:
        back = (lambda r: jnp.transpose(r, (0, 1, 3, 2))) if k in ("ssm_b_re", "ssm_b_im") else (lambda r: r.reshape(a[k].shape))
        results[k] = tuple(back(r) for r in sm[k])

    outs = [loss, grad_x]
    for part in range(4):
        outs += [results[k][part] for k in _OUT_ORDER]
    return tuple(outs)


def kernel(x, norm1, w_in, b_gate, ssm_a_re, ssm_a_im, ssm_log_dt, ssm_b_re, ssm_b_im, ssm_c_re, ssm_c_im, ssm_d, ssm_w_glu, ssm_b_glu, ssm_w_proj, conv_w_dw, conv_b_dw, conv_ln_g, conv_ln_b, conv_w_proj, pool_w_group, pool_scale, pool_w_proj, w_out, norm2, ffn_w_gate, ffn_w_up, ffn_w_down, final_norm, loss_target, m_norm1, m_w_in, m_b_gate, m_ssm_a_re, m_ssm_a_im, m_ssm_log_dt, m_ssm_b_re, m_ssm_b_im, m_ssm_c_re, m_ssm_c_im, m_ssm_d, m_ssm_w_glu, m_ssm_b_glu, m_ssm_w_proj, m_conv_w_dw, m_conv_b_dw, m_conv_ln_g, m_conv_ln_b, m_conv_w_proj, m_pool_w_group, m_pool_scale, m_pool_w_proj, m_w_out, m_norm2, m_ffn_w_gate, m_ffn_w_up, m_ffn_w_down, m_final_norm, v_norm1, v_w_in, v_b_gate, v_ssm_a_re, v_ssm_a_im, v_ssm_log_dt, v_ssm_b_re, v_ssm_b_im, v_ssm_c_re, v_ssm_c_im, v_ssm_d, v_ssm_w_glu, v_ssm_b_glu, v_ssm_w_proj, v_conv_w_dw, v_conv_b_dw, v_conv_ln_g, v_conv_ln_b, v_conv_w_proj, v_pool_w_group, v_pool_scale, v_pool_w_proj, v_w_out, v_norm2, v_ffn_w_gate, v_ffn_w_up, v_ffn_w_down, v_final_norm):
    return _train_step(dict(locals()))
```

```python
import functools

import jax
import jax.numpy as jnp
from jax import lax
from jax.experimental import pallas as pl
from jax.experimental.pallas import tpu as pltpu

F32 = jnp.float32
BF16 = jnp.bfloat16

NDEV = 8
DEPTH = 2
D = 1024
BW = 512
NSTATE = 64
SGRP = 16
NGRP = BW // SGRP
GB = 8
NBLK = NGRP // GB
NS = GB * NSTATE
CONV_K = 31
HALO = 32
PHALO = 16
IN_W = 5120
HID = 2816
HSH = HID // NDEV
HPAD = 384
HIDP = HPAD * NDEV
EPS = 1e-6
VMEM_LIMIT = 56 * 1024 * 1024

ADAM_LR, ADAM_B1, ADAM_B2, ADAM_EPS, ADAM_WD, ADAM_STEP = 0.001, 0.9, 0.999, 1e-08, 0.01, 10

MESH = pl.DeviceIdType.MESH
ANY = pl.BlockSpec(memory_space=pl.ANY)


def _call(body, **kw):
    return pl.pallas_call(body, **kw)


def _cparams(sem=None):
    return pltpu.CompilerParams(dimension_semantics=sem, vmem_limit_bytes=VMEM_LIMIT)


def _dot(a, b):
    return jnp.dot(a.astype(BF16), b.astype(BF16), preferred_element_type=F32)


def _dot_nt(a, b):
    return lax.dot_general(a.astype(BF16), b.astype(BF16), (((1,), (1,)), ((), ())), preferred_element_type=F32)


def _dot_tn(a, b):
    return lax.dot_general(a.astype(BF16), b.astype(BF16), (((0,), (0,)), ((), ())), preferred_element_type=F32)


@jax.custom_vjp
def _mm(a, w):
    return _dot(a, w)


def _mm_fwd(a, w):
    return _dot(a, w), w


def _mm_bwd(w, ct):
    return _dot_nt(ct, w), jnp.zeros_like(w)


_mm.defvjp(_mm_fwd, _mm_bwd)


def _rms(x, g):
    return x * lax.rsqrt(jnp.mean(x * x, axis=-1, keepdims=True) + EPS) * g


def _disc(are, aim, ldt):
    dt = jnp.exp(ldt)
    mag = jnp.exp(dt * are)
    ang = dt * aim
    abr = mag * jnp.cos(ang)
    abi = mag * jnp.sin(ang)
    den = are * are + aim * aim
    nr = abr - 1.0
    fr = (nr * are + abi * aim) / den
    fi = (abi * are - nr * aim) / den
    return abr, abi, fr, fi


def _bbar(fr, fi, br, bi):
    return fr * br - fi * bi, fr * bi + fi * br


def _cmul(ar, ai, br, bi):
    return ar * br - ai * bi, ar * bi + ai * br


def _scan_rows(re_ref, im_ref, ar, ai, n_rows, reverse, hre_ref=None, him_ref=None):
    n = ar.shape[1]
    shape = (8, n)
    rows = lax.broadcasted_iota(jnp.int32, shape, 0)
    a1 = (jnp.broadcast_to(ar, shape), jnp.broadcast_to(ai, shape))
    a2 = _cmul(*a1, *a1)
    a4 = _cmul(*a2, *a2)
    pr = jnp.zeros(shape, F32)
    pi = jnp.zeros(shape, F32)
    pw = a1
    for k in range(8):
        sel = rows == ((7 - k) if reverse else k)
        pr = jnp.where(sel, pw[0], pr)
        pi = jnp.where(sel, pw[1], pi)
        pw = _cmul(*pw, *a1)
    nt = n_rows // 8
    with_acc = hre_ref is not None

    def body(i, carry):
        cr, ci = carry[0], carry[1]
        t = (nt - 1 - i) if reverse else i
        off = pl.multiple_of(t * 8, 8)
        xr = re_ref[pl.ds(off, 8), :]
        xi = im_ref[pl.ds(off, 8), :]
        for k, (kr, ki) in ((1, a1), (2, a2), (4, a4)):
            if reverse:
                keep, sh = rows < 8 - k, 8 - k
            else:
                keep, sh = rows >= k, k
            sr = jnp.where(keep, pltpu.roll(xr, sh, 0), 0.0)
            si = jnp.where(keep, pltpu.roll(xi, sh, 0), 0.0)
            xr, xi = xr + kr * sr - ki * si, xi + kr * si + ki * sr
        xr, xi = xr + pr * cr - pi * ci, xi + pr * ci + pi * cr
        re_ref[pl.ds(off, 8), :] = xr
        im_ref[pl.ds(off, 8), :] = xi
        edge = 0 if reverse else 7
        out = (jnp.broadcast_to(xr[edge:edge + 1, :], shape), jnp.broadcast_to(xi[edge:edge + 1, :], shape))
        if with_acc:
            hr = hre_ref[pl.ds(off, 8), :]
            hi = him_ref[pl.ds(off, 8), :]
            offp = pl.multiple_of(jnp.maximum(t - 1, 0) * 8, 8)
            live = jnp.where(t > 0, 1.0, 0.0)
            lr = jnp.broadcast_to(hre_ref[pl.ds(offp, 8), :][7:8, :], shape) * live
            li = jnp.broadcast_to(him_ref[pl.ds(offp, 8), :][7:8, :], shape) * live
            hpr = jnp.where(rows == 0, lr, pltpu.roll(hr, 1, 0))
            hpi = jnp.where(rows == 0, li, pltpu.roll(hi, 1, 0))
            out = out + (carry[2] + xr * hpr + xi * hpi, carry[3] + xi * hpr - xr * hpi)
        return out

    zero = jnp.zeros(shape, F32)
    init = (zero, zero, zero, zero) if with_acc else (zero, zero)
    res = lax.fori_loop(0, nt, body, init)
    return res[2:] if with_acc else None


TOKEN_SPEC = pl.BlockSpec((8, 128), lambda i, j: (0, 0))


def _inproj_fwd(x, gamma, w, token, tm=1024, nb=2):
    t_rows = x.shape[0]
    tm = min(tm, t_rows)
    oc = w.shape[2]
    n = NDEV * oc
    tn = nb * oc

    def body(x_ref, g_ref, w_ref, token_ref, z_ref, h_ref):
        @pl.when(pl.program_id(1) == 0)
        def _():
            h_ref[...] = _rms(x_ref[...], g_ref[...]).astype(BF16)
        for q in range(nb):
            z_ref[:, oc * q:oc * (q + 1)] = jnp.dot(h_ref[...], w_ref[q], preferred_element_type=F32)

    return _call(
        body, name="inproj_fwd", grid=(t_rows // tm, n // tn),
        in_specs=[pl.BlockSpec((tm, D), lambda i, j: (i, 0)), pl.BlockSpec((1, D), lambda i, j: (0, 0)),
                  pl.BlockSpec((nb, D, oc), lambda i, j: (j, 0, 0)), TOKEN_SPEC],
        out_specs=[pl.BlockSpec((tm, tn), lambda i, j: (i, j)), pl.BlockSpec((tm, D), lambda i, j: (i, 0))],
        out_shape=[jax.ShapeDtypeStruct((t_rows, n), F32), jax.ShapeDtypeStruct((t_rows, D), BF16)],
        compiler_params=_cparams(("parallel", "arbitrary")))(x, gamma, w, token)


def _ssm_specs(t_rows):
    row = pl.BlockSpec((1, NS), lambda j: (0, j))
    return dict(
        u=pl.BlockSpec((t_rows, GB * SGRP), lambda j: (0, j)),
        row=row,
        bexp=pl.BlockSpec((None, GB * SGRP, NS), lambda j: (j, 0, 0)),
        cexp=pl.BlockSpec((None, NS, GB * SGRP), lambda j: (j, 0, 0)),
        d=pl.BlockSpec((1, GB * SGRP), lambda j: (0, j)),
        h=pl.BlockSpec((t_rows, NS), lambda j: (0, j)),
    )


def _ssm_fwd(z, p):
    t_rows = z.shape[0]
    s = _ssm_specs(t_rows)

    def body(u_ref, are_ref, aim_ref, ldt_ref, br_ref, bi_ref, cr_ref, ci_ref, d_ref, y_ref, hr_ref, hi_ref):
        abr, abi, fr, fi = _disc(are_ref[...], aim_ref[...], ldt_ref[...])
        bbr, bbi = _bbar(fr, fi, br_ref[...], bi_ref[...])
        u = u_ref[...]
        hr_ref[...] = _dot(u, bbr)
        hi_ref[...] = _dot(u, bbi)
        _scan_rows(hr_ref, hi_ref, abr, abi, t_rows, False)
        y_ref[...] = _dot(hr_ref[...], cr_ref[...]) - _dot(hi_ref[...], ci_ref[...]) + d_ref[...] * u

    return _call(
        body, name="ssm_fwd", grid=(NBLK,),
        in_specs=[s["u"], s["row"], s["row"], s["row"], s["bexp"], s["bexp"], s["cexp"], s["cexp"], s["d"]],
        out_specs=[s["u"], s["h"], s["h"]],
        out_shape=[jax.ShapeDtypeStruct((t_rows, BW), F32), jax.ShapeDtypeStruct((t_rows, NGRP * NSTATE), F32),
                   jax.ShapeDtypeStruct((t_rows, NGRP * NSTATE), F32)],
        compiler_params=_cparams(("parallel",)))(
            z, p["are"], p["aim"], p["ldt"], p["bexp_re"], p["bexp_im"], p["cexp_re"], p["cexp_im"], p["dskip"])


def _ssm_bwd(dy, z, hre, him, p, token):
    t_rows = z.shape[0]
    s = _ssm_specs(t_rows)
    nstates = NGRP * NSTATE

    def body(dy_ref, u_ref, hr_ref, hi_ref, are_ref, aim_ref, ldt_ref, br_ref, bi_ref, cr_ref, ci_ref, d_ref, token_ref,
             du_ref, dbr_ref, dbi_ref, dcr_ref, dci_ref, dd_ref, dar_ref, dai_ref, dldt_ref, lr_ref, li_ref):
        rows3 = (are_ref[...], aim_ref[...], ldt_ref[...])
        (abr, abi, fr, fi), disc_vjp = jax.vjp(_disc, *rows3)
        (bbr, bbi), bbar_vjp = jax.vjp(_bbar, fr, fi, br_ref[...], bi_ref[...])
        dy = dy_ref[...]
        u = u_ref[...]
        lr_ref[...] = _dot_nt(dy, cr_ref[...])
        li_ref[...] = -_dot_nt(dy, ci_ref[...])
        dcr_ref[...] = _dot_tn(hr_ref[...], dy)
        dci_ref[...] = -_dot_tn(hi_ref[...], dy)
        dd_ref[...] = jnp.sum(dy * u, axis=0, keepdims=True)
        acc_r, acc_i = _scan_rows(lr_ref, li_ref, abr, -abi, t_rows, True, hr_ref, hi_ref)
        dabr = jnp.sum(acc_r, axis=0, keepdims=True)
        dabi = jnp.sum(acc_i, axis=0, keepdims=True)
        lam_r = lr_ref[...]
        lam_i = li_ref[...]
        du = d_ref[...] * dy + _dot_nt(lam_r, bbr) + _dot_nt(lam_i, bbi)
        du_ref[...] = du.astype(BF16)
        dbbr = _dot_tn(u, lam_r)
        dbbi = _dot_tn(u, lam_i)
        dfr, dfi, dbr, dbi = bbar_vjp((dbbr, dbbi))
        dbr_ref[...] = dbr
        dbi_ref[...] = dbi
        dar, dai, dldt = disc_vjp((dabr, dabi, dfr, dfi))
        dar_ref[...] = dar
        dai_ref[...] = dai
        lane_grp = lax.broadcasted_iota(jnp.int32, (NS, 128), 0) // NSTATE
        col = lax.broadcasted_iota(jnp.int32, (NS, 128), 1)
        seg = jnp.where(lane_grp == col, 1.0, 0.0).astype(F32)
        dldt_ref[...] = jnp.dot(jnp.broadcast_to(dldt, (8, NS)), seg, preferred_element_type=F32,
                                precision=lax.Precision.HIGHEST)

    dyspec = pl.BlockSpec((t_rows, GB * SGRP), lambda j: (0, j))
    return _call(
        body, name="ssm_bwd", grid=(NBLK,),
        in_specs=[dyspec, s["u"], s["h"], s["h"], s["row"], s["row"], s["row"], s["bexp"], s["bexp"], s["cexp"],
                  s["cexp"], s["d"], pl.BlockSpec((8, 128), lambda j: (0, 0))],
        out_specs=[dyspec, s["bexp"], s["bexp"], s["cexp"], s["cexp"], s["d"], s["row"], s["row"],
                   pl.BlockSpec((8, 128), lambda j: (j, 0))],
        out_shape=[jax.ShapeDtypeStruct((t_rows, BW), BF16),
                   jax.ShapeDtypeStruct((NBLK, GB * SGRP, NS), F32), jax.ShapeDtypeStruct((NBLK, GB * SGRP, NS), F32),
                   jax.ShapeDtypeStruct((NBLK, NS, GB * SGRP), F32), jax.ShapeDtypeStruct((NBLK, NS, GB * SGRP), F32),
                   jax.ShapeDtypeStruct((1, BW), F32), jax.ShapeDtypeStruct((1, nstates), F32),
                   jax.ShapeDtypeStruct((1, nstates), F32), jax.ShapeDtypeStruct((NBLK * 8, 128), F32)],
        scratch_shapes=[pltpu.VMEM((t_rows, NS), F32), pltpu.VMEM((t_rows, NS), F32)],
        compiler_params=_cparams(("parallel",)))(
            dy, z, hre, him, p["are"], p["aim"], p["ldt"], p["bexp_re"], p["bexp_im"], p["cexp_re"], p["cexp_im"],
            p["dskip"], token)


def _conv_fwd(z, w, b, tm=256):
    t_rows = z.shape[0]
    hb = tm // HALO

    def body(va_ref, vb_ref, ha_ref, hb_ref, w_ref, b_ref, o_ref, win_ref):
        live = jnp.where(pl.program_id(0) > 0, 1.0, 0.0)
        win_ref[0:HALO, :] = ha_ref[...] * jax.nn.sigmoid(hb_ref[...]) * live
        win_ref[HALO:HALO + tm, :] = va_ref[...] * jax.nn.sigmoid(vb_ref[...])
        acc = jnp.broadcast_to(b_ref[...], (tm, BW))
        for k in range(CONV_K):
            acc = acc + w_ref[k:k + 1, :] * win_ref[pl.ds(HALO - (CONV_K - 1) + k, tm), :]
        o_ref[...] = acc

    halo = lambda col: pl.BlockSpec((HALO, BW), lambda i: (jnp.maximum(i * hb - 1, 0), col))
    return _call(
        body, name="conv_fwd", grid=(t_rows // tm,),
        in_specs=[pl.BlockSpec((tm, BW), lambda i: (i, 1)), pl.BlockSpec((tm, BW), lambda i: (i, 2)), halo(1), halo(2),
                  pl.BlockSpec((CONV_K, BW), lambda i: (0, 0)), pl.BlockSpec((1, BW), lambda i: (0, 0))],
        out_specs=pl.BlockSpec((tm, BW), lambda i: (i, 0)),
        out_shape=jax.ShapeDtypeStruct((t_rows, BW), F32),
        scratch_shapes=[pltpu.VMEM((HALO + tm, BW), F32)],
        compiler_params=_cparams(("parallel",)))(z, z, z, z, w, b)


def _conv_bwd(dcv, z, w, tm=256):
    t_rows = z.shape[0]
    nt = t_rows // tm
    hb = tm // HALO
    csh = BW // NDEV

    def body(d_ref, dn_ref, va_ref, vb_ref, ha_ref, hb_ref, w_ref, dva_ref, dvb_ref, dw8_ref, db_ref,
             hwin_ref, dwin_ref, dw_ref):
        i = pl.program_id(0)

        @pl.when(i == 0)
        def _():
            dw_ref[...] = jnp.zeros_like(dw_ref)
            db_ref[...] = jnp.zeros_like(db_ref)

        live_prev = jnp.where(i > 0, 1.0, 0.0)
        live_next = jnp.where(i < nt - 1, 1.0, 0.0)
        va = va_ref[...]
        sig = jax.nn.sigmoid(vb_ref[...])
        hwin_ref[0:HALO, :] = ha_ref[...] * jax.nn.sigmoid(hb_ref[...]) * live_prev
        hwin_ref[HALO:HALO + tm, :] = va * sig
        d = d_ref[...]
        dwin_ref[0:tm, :] = d
        dwin_ref[tm:tm + HALO, :] = dn_ref[...] * live_next
        dh = jnp.zeros((tm, BW), F32)
        dws = []
        for k in range(CONV_K):
            dh = dh + w_ref[k:k + 1, :] * dwin_ref[pl.ds(CONV_K - 1 - k, tm), :]
            dws.append(jnp.sum(d * hwin_ref[pl.ds(HALO - (CONV_K - 1) + k, tm), :], axis=0, keepdims=True))
        dws.append(jnp.zeros((1, BW), F32))
        dw_ref[...] += jnp.concatenate(dws, axis=0)
        db_ref[...] += jnp.sum(d, axis=0, keepdims=True)
        dva_ref[...] = (dh * sig).astype(BF16)
        dvb_ref[...] = (dh * va * sig * (1.0 - sig)).astype(BF16)

        @pl.when(i == nt - 1)
        def _():
            acc = dw_ref[...]
            for q in range(NDEV):
                dw8_ref[q] = acc[:, csh * q:csh * (q + 1)]

    halo = lambda col: pl.BlockSpec((HALO, BW), lambda i: (jnp.maximum(i * hb - 1, 0), col))
    return _call(
        body, name="conv_bwd", grid=(nt,),
        in_specs=[pl.BlockSpec((tm, BW), lambda i: (i, 0)),
                  pl.BlockSpec((HALO, BW), lambda i: (jnp.minimum((i + 1) * hb, t_rows // HALO - 1), 0)),
                  pl.BlockSpec((tm, BW), lambda i: (i, 1)), pl.BlockSpec((tm, BW), lambda i: (i, 2)), halo(1), halo(2),
                  pl.BlockSpec((CONV_K, BW), lambda i: (0, 0))],
        out_specs=[pl.BlockSpec((tm, BW), lambda i: (i, 0)), pl.BlockSpec((tm, BW), lambda i: (i, 0)),
                   pl.BlockSpec((NDEV, 32, csh), lambda i: (0, 0, 0)), pl.BlockSpec((1, BW), lambda i: (0, 0))],
        out_shape=[jax.ShapeDtypeStruct((t_rows, BW), BF16), jax.ShapeDtypeStruct((t_rows, BW), BF16),
                   jax.ShapeDtypeStruct((NDEV, 32, csh), F32), jax.ShapeDtypeStruct((1, BW), F32)],
        scratch_shapes=[pltpu.VMEM((HALO + tm, BW), F32), pltpu.VMEM((tm + HALO, BW), F32), pltpu.VMEM((32, BW), F32)],
        compiler_params=_cparams(("arbitrary",)))(dcv, dcv, z, z, z, z, w)


def _pool_rows(i, tm, n_rows, first_row):
    grp = lax.broadcasted_iota(jnp.int32, (1, BW), 1) // (BW // 4)
    wlen = jnp.where(grp == 0, 2.0, jnp.where(grp == 1, 4.0, jnp.where(grp == 2, 8.0, 16.0)))
    t = (i * tm + first_row + lax.broadcasted_iota(jnp.int32, (n_rows, 1), 0)).astype(F32)
    return grp, 1.0 / jnp.minimum(t + 1.0, wlen)


def _pool_pick(grp, s2, s4, s8, s16):
    return jnp.where(grp == 0, s2, jnp.where(grp == 1, s4, jnp.where(grp == 2, s8, s16)))


def _pool_fwd(z, tm=256):
    t_rows = z.shape[0]
    hb = tm // PHALO

    def body(u_ref, h_ref, o_ref):
        i = pl.program_id(0)
        u = u_ref[...]
        win = jnp.concatenate([h_ref[...] * jnp.where(i > 0, 1.0, 0.0), u], axis=0)
        s2 = win + pltpu.roll(win, 1, 0)
        s4 = s2 + pltpu.roll(s2, 2, 0)
        s8 = s4 + pltpu.roll(s4, 4, 0)
        s16 = s8 + pltpu.roll(s8, 8, 0)
        grp, inv = _pool_rows(i, tm, tm, 0)
        o_ref[...] = _pool_pick(grp, s2, s4, s8, s16)[PHALO:, :] * inv - u

    return _call(
        body, name="pool_fwd", grid=(t_rows // tm,),
        in_specs=[pl.BlockSpec((tm, BW), lambda i: (i, 3)),
                  pl.BlockSpec((PHALO, BW), lambda i: (jnp.maximum(i * hb - 1, 0), 3))],
        out_specs=pl.BlockSpec((tm, BW), lambda i: (i, 0)),
        out_shape=jax.ShapeDtypeStruct((t_rows, BW), F32),
        compiler_params=_cparams(("parallel",)))(z, z)


def _pool_bwd(dp, tm=256):
    t_rows = dp.shape[0]
    nt = t_rows // tm
    hb = tm // PHALO
    ln = tm + PHALO

    def body(d_ref, dn_ref, o_ref):
        i = pl.program_id(0)
        d = d_ref[...]
        grp, inv = _pool_rows(i, tm, ln, 0)
        win = jnp.concatenate([d, dn_ref[...] * jnp.where(i < nt - 1, 1.0, 0.0)], axis=0) * inv
        s2 = win + pltpu.roll(win, ln - 1, 0)
        s4 = s2 + pltpu.roll(s2, ln - 2, 0)
        s8 = s4 + pltpu.roll(s4, ln - 4, 0)
        s16 = s8 + pltpu.roll(s8, ln - 8, 0)
        o_ref[...] = (_pool_pick(grp, s2, s4, s8, s16)[:tm, :] - d).astype(BF16)

    return _call(
        body, name="pool_bwd", grid=(nt,),
        in_specs=[pl.BlockSpec((tm, BW), lambda i: (i, 0)),
                  pl.BlockSpec((PHALO, BW), lambda i: (jnp.minimum((i + 1) * hb, t_rows // PHALO - 1), 0))],
        out_specs=pl.BlockSpec((tm, BW), lambda i: (i, 0)),
        out_shape=jax.ShapeDtypeStruct((t_rows, BW), BF16),
        compiler_params=_cparams(("parallel",)))(dp, dp)


_MERGE_W = ("wglu", "bglu", "wpa", "lng", "lnb", "wpb", "wgrp", "scale", "wpc", "bgate", "wout")
_MERGE_SMALL = ("bglu", "lng", "lnb", "scale", "bgate")
_MERGE_BLOCKED = ("wpa", "wpb", "wpc")


def _merge_load(name, ref):
    if name in _MERGE_BLOCKED:
        return jnp.concatenate([ref[q] for q in range(NDEV)], axis=1)
    return ref[...]


def _merge_math(x, yssm, cv, pbar, zg, w, taps):
    t_glu, t_ya, t_yb, t_p, t_yc = taps
    g = jax.nn.gelu(yssm)
    outa = g * jax.nn.sigmoid(_mm(g, w["wglu"]) + t_glu + w["bglu"])
    ya = _mm(outa, w["wpa"]) + t_ya
    mu = jnp.mean(cv, axis=-1, keepdims=True)
    var = jnp.mean(jnp.square(cv - mu), axis=-1, keepdims=True)
    hs = jax.nn.silu((cv - mu) * lax.rsqrt(var + EPS) * w["lng"] + w["lnb"])
    yb = _mm(hs, w["wpb"]) + t_yb
    gw = BW // 4
    pk = jnp.concatenate([_mm(pbar[:, gw * k:gw * (k + 1)], w["wgrp"][k]) for k in range(4)], axis=1) + t_p
    pc = pk * w["scale"]
    yc = _mm(pc, w["wpc"]) + t_yc
    gates = jax.nn.sigmoid(zg + w["bgate"])
    merged = gates[:, :D] * ya + gates[:, D:2 * D] * yb + gates[:, 2 * D:] * yc
    x1 = x + _mm(merged, w["wout"])
    acts = tuple(a.astype(BF16) for a in (g, outa, hs, pbar, pc, merged))
    return x1, acts


def _merge_specs(tm, p):
    rows = lambda width, col=0: pl.BlockSpec((tm, width), lambda i, c=col: (i, c))
    data = [rows(D), rows(BW), rows(BW), rows(BW), rows(D, 2), rows(D, 3), rows(D, 4)]
    wspecs = []
    for name in _MERGE_W:
        nd = p[name].ndim
        wspecs.append(pl.BlockSpec(p[name].shape, lambda i, nd=nd: (0,) * nd))
    return rows, data, wspecs


def _merge_fwd(x, yssm, cv, pbar, z, p, token, tm=512):
    t_rows = x.shape[0]
    rows, data, wspecs = _merge_specs(tm, p)

    def body(x_ref, y_ref, cv_ref, pb_ref, za_ref, zb_ref, zc_ref, *rest):
        w = {name: _merge_load(name, r) for name, r in zip(_MERGE_W, rest[:len(_MERGE_W)])}
        o_ref = rest[len(_MERGE_W) + 1]
        taps = (0.0, 0.0, 0.0, 0.0, 0.0)
        zg = jnp.concatenate([za_ref[...], zb_ref[...], zc_ref[...]], axis=1)
        o_ref[...] = _merge_math(x_ref[...], y_ref[...], cv_ref[...], pb_ref[...], zg, w, taps)[0]

    return _call(
        body, name="merge_fwd", grid=(t_rows // tm,),
        in_specs=data + wspecs + [pl.BlockSpec((8, 128), lambda i: (0, 0))], out_specs=rows(D),
        out_shape=jax.ShapeDtypeStruct((t_rows, D), F32),
        compiler_params=_cparams(("parallel",)))(x, yssm, cv, pbar, z, z, z, *[p[n] for n in _MERGE_W], token)


def _merge_bwd(dx1, x, yssm, cv, pbar, z, p, token, tm=256):
    t_rows = x.shape[0]
    rows, data, wspecs = _merge_specs(tm, p)
    nw = len(_MERGE_W)

    def body(dx_ref, x_ref, y_ref, cv_ref, pb_ref, za_ref, zb_ref, zc_ref, *rest):
        w = {name: _merge_load(name, r) for name, r in zip(_MERGE_W, rest[:nw])}
        zg = jnp.concatenate([za_ref[...], zb_ref[...], zc_ref[...]], axis=1)
        outs = rest[nw + 1:]
        small = {n: w[n] for n in _MERGE_SMALL}
        taps = (jnp.zeros((tm, BW), F32), jnp.zeros((tm, D), F32), jnp.zeros((tm, D), F32),
                jnp.zeros((tm, BW), F32), jnp.zeros((tm, D), F32))

        def f(yssm_, cv_, pbar_, zg_, small_, taps_):
            return _merge_math(x_ref[...], yssm_, cv_, pbar_, zg_, {**w, **small_}, taps_)

        _, vjp, acts = jax.vjp(f, y_ref[...], cv_ref[...], pb_ref[...], zg, small, taps, has_aux=True)
        dy, dcv, dpb, dzg, dsmall, dtaps = vjp(dx_ref[...])
        outs[0][...] = dy
        outs[1][...] = dcv
        outs[2][...] = dpb
        outs[3][:, 4 * BW:] = dzg.astype(BF16)
        for k in range(6):
            outs[4 + k][...] = acts[k]
        for k in range(5):
            outs[10 + k][...] = dtaps[k].astype(BF16)

        @pl.when(pl.program_id(0) == 0)
        def _():
            for k in range(5):
                outs[15 + k][...] = jnp.zeros_like(outs[15 + k])

        for k, n in enumerate(_MERGE_SMALL):
            outs[15 + k][...] += dsmall[n]

    f32o = lambda width: jax.ShapeDtypeStruct((t_rows, width), F32)
    bfo = lambda width: jax.ShapeDtypeStruct((t_rows, width), BF16)
    small_shapes = [jax.ShapeDtypeStruct(p[n].shape, F32) for n in _MERGE_SMALL]
    small_specs = [pl.BlockSpec(p[n].shape, lambda i: (0, 0)) for n in _MERGE_SMALL]
    out_shape = ([f32o(BW), f32o(BW), f32o(BW), bfo(IN_W)]
                 + [bfo(BW), bfo(BW), bfo(BW), bfo(BW), bfo(BW), bfo(D)]
                 + [bfo(BW), bfo(D), bfo(D), bfo(BW), bfo(D)] + small_shapes)
    out_specs = ([rows(BW), rows(BW), rows(BW), rows(IN_W)]
                 + [rows(BW)] * 5 + [rows(D)]
                 + [rows(BW), rows(D), rows(D), rows(BW), rows(D)] + small_specs)
    return _call(
        body, name="merge_bwd", grid=(t_rows // tm,),
        in_specs=[rows(D)] + data + wspecs + [pl.BlockSpec((8, 128), lambda i: (0, 0))], out_specs=out_specs,
        out_shape=out_shape, compiler_params=_cparams(("arbitrary",)))(
            dx1, x, yssm, cv, pbar, z, z, z, *[p[n] for n in _MERGE_W], token)


def _ffn_fwd(x1, gamma, wg, wu, wd, tm=1024, th=512):
    t_rows = x1.shape[0]
    tm = min(tm, t_rows)
    nh = HIDP // th

    def body(x_ref, g_ref, wg_ref, wu_ref, wd_ref, o_ref, gp_ref, up_ref, h_ref, acc_ref):
        j = pl.program_id(1)

        @pl.when(j == 0)
        def _():
            h_ref[...] = _rms(x_ref[...], g_ref[...]).astype(BF16)
            acc_ref[...] = jnp.zeros_like(acc_ref)

        gp = _dot_nt(h_ref[...], wg_ref[...])
        up = _dot_nt(h_ref[...], wu_ref[...])
        gp_ref[...] = gp.astype(BF16)
        up_ref[...] = up.astype(BF16)
        acc_ref[...] += _dot(jax.nn.silu(gp) * up, wd_ref[...])

        @pl.when(j == nh - 1)
        def _():
            o_ref[...] = x_ref[...] + acc_ref[...]

    return _call(
        body, name="ffn_fwd", grid=(t_rows // tm, nh),
        in_specs=[pl.BlockSpec((tm, D), lambda i, j: (i, 0)), pl.BlockSpec((1, D), lambda i, j: (0, 0)),
                  pl.BlockSpec((th, D), lambda i, j: (j, 0)), pl.BlockSpec((th, D), lambda i, j: (j, 0)),
                  pl.BlockSpec((th, D), lambda i, j: (j, 0))],
        out_specs=[pl.BlockSpec((tm, D), lambda i, j: (i, 0)), pl.BlockSpec((tm, th), lambda i, j: (i, j)),
                   pl.BlockSpec((tm, th), lambda i, j: (i, j)), pl.BlockSpec((tm, D), lambda i, j: (i, 0))],
        out_shape=[jax.ShapeDtypeStruct((t_rows, D), F32), jax.ShapeDtypeStruct((t_rows, HIDP), BF16),
                   jax.ShapeDtypeStruct((t_rows, HIDP), BF16), jax.ShapeDtypeStruct((t_rows, D), BF16)],
        scratch_shapes=[pltpu.VMEM((tm, D), F32)],
        compiler_params=_cparams(("parallel", "arbitrary")))(x1, gamma, wg, wu, wd)


def _rms_bwd_tail(x, gamma, dh):
    _, vjp = jax.vjp(_rms, x, gamma)
    return vjp(dh)


def _ffn_bwd(dx2, x1, gamma, gpre, upre, wg, wu, wd, token, tm=1024, th=512):
    t_rows = x1.shape[0]
    tm = min(tm, t_rows)
    nh = HIDP // th

    def body(d_ref, x_ref, g_ref, gp_ref, up_ref, wg_ref, wu_ref, wd_ref, token_ref,
             dx_ref, dgam_ref, dgp_ref, dup_ref, act_ref, acc_ref):
        i = pl.program_id(0)
        j = pl.program_id(1)

        @pl.when(j == 0)
        def _():
            acc_ref[...] = jnp.zeros_like(acc_ref)

        @pl.when((i == 0) & (j == 0))
        def _():
            dgam_ref[...] = jnp.zeros_like(dgam_ref)

        dact = _dot_nt(d_ref[...], wd_ref[...])
        gp = gp_ref[...].astype(F32)
        up = up_ref[...].astype(F32)
        sg = jax.nn.sigmoid(gp)
        silu = gp * sg
        dgp = (dact * up * (sg * (1.0 + gp * (1.0 - sg)))).astype(BF16)
        dup = (dact * silu).astype(BF16)
        dgp_ref[...] = dgp
        dup_ref[...] = dup
        act_ref[...] = (silu * up).astype(BF16)
        acc_ref[...] += _dot(dgp, wg_ref[...]) + _dot(dup, wu_ref[...])

        @pl.when(j == nh - 1)
        def _():
            x = x_ref[...]
            dx, dgam = _rms_bwd_tail(x, g_ref[...], acc_ref[...])
            dx_ref[...] = d_ref[...] + dx
            dgam_ref[...] += dgam

    row_d = pl.BlockSpec((tm, D), lambda i, j: (i, 0))
    row_h = pl.BlockSpec((tm, th), lambda i, j: (i, j))
    return _call(
        body, name="ffn_bwd", grid=(t_rows // tm, nh),
        in_specs=[row_d, row_d, pl.BlockSpec((1, D), lambda i, j: (0, 0)), row_h, row_h,
                  pl.BlockSpec((th, D), lambda i, j: (j, 0)), pl.BlockSpec((th, D), lambda i, j: (j, 0)),
                  pl.BlockSpec((th, D), lambda i, j: (j, 0)), TOKEN_SPEC],
        out_specs=[row_d, pl.BlockSpec((1, D), lambda i, j: (0, 0)), row_h, row_h, row_h],
        out_shape=[jax.ShapeDtypeStruct((t_rows, D), F32), jax.ShapeDtypeStruct((1, D), F32),
                   jax.ShapeDtypeStruct((t_rows, HIDP), BF16), jax.ShapeDtypeStruct((t_rows, HIDP), BF16),
                   jax.ShapeDtypeStruct((t_rows, HIDP), BF16)],
        scratch_shapes=[pltpu.VMEM((tm, D), F32)],
        compiler_params=_cparams(("arbitrary", "arbitrary")))(dx2, x1, gamma, gpre, upre, wg, wu, wd, token)


def _inproj_bwd(dz, dx1, x, gamma, w, token, tm=1024, nb=2):
    t_rows = x.shape[0]
    tm = min(tm, t_rows)
    oc = w.shape[2]
    tn = nb * oc
    nn = NDEV // nb

    def body(dz_ref, d1_ref, x_ref, g_ref, w_ref, token_ref, dx_ref, dgam_ref, acc_ref):
        i = pl.program_id(0)
        j = pl.program_id(1)

        @pl.when(j == 0)
        def _():
            acc_ref[...] = jnp.zeros_like(acc_ref)

        @pl.when((i == 0) & (j == 0))
        def _():
            dgam_ref[...] = jnp.zeros_like(dgam_ref)

        for q in range(nb):
            acc_ref[...] += _dot_nt(dz_ref[:, oc * q:oc * (q + 1)], w_ref[q])

        @pl.when(j == nn - 1)
        def _():
            x = x_ref[...]
            dx, dgam = _rms_bwd_tail(x, g_ref[...], acc_ref[...])
            dx_ref[...] = d1_ref[...] + dx
            dgam_ref[...] += dgam

    row_d = pl.BlockSpec((tm, D), lambda i, j: (i, 0))
    return _call(
        body, name="inproj_bwd", grid=(t_rows // tm, nn),
        in_specs=[pl.BlockSpec((tm, tn), lambda i, j: (i, j)), row_d, row_d, pl.BlockSpec((1, D), lambda i, j: (0, 0)),
                  pl.BlockSpec((nb, D, oc), lambda i, j: (j, 0, 0)), TOKEN_SPEC],
        out_specs=[row_d, pl.BlockSpec((1, D), lambda i, j: (0, 0))],
        out_shape=[jax.ShapeDtypeStruct((t_rows, D), F32), jax.ShapeDtypeStruct((1, D), F32)],
        scratch_shapes=[pltpu.VMEM((tm, D), F32)],
        compiler_params=_cparams(("arbitrary", "arbitrary")))(dz, dx1, x, gamma, w, token)


def _matmul_tn(a, b, name, owner_cols=None, tt=2048):
    t_rows, k = a.shape
    n = b.shape[1]
    tk = min(k, 1024)
    tt = min(tt, t_rows)
    nt = t_rows // tt
    if owner_cols is None:
        tn, nb = min(n, 1024), None
        out_spec = pl.BlockSpec((tk, tn), lambda i, j, t: (i, j))
        out_shape = jax.ShapeDtypeStruct((k, n), BF16)
    else:
        nb = min(n // owner_cols, max(1, 1280 // owner_cols))
        tn = nb * owner_cols
        out_spec = pl.BlockSpec((nb, tk, owner_cols), lambda i, j, t: (j, i, 0))
        out_shape = jax.ShapeDtypeStruct((n // owner_cols, k, owner_cols), BF16)

    def body(a_ref, b_ref, o_ref, acc_ref):
        t = pl.program_id(2)

        @pl.when(t == 0)
        def _():
            acc_ref[...] = jnp.zeros_like(acc_ref)

        acc_ref[...] += _dot_tn(a_ref[...], b_ref[...])

        @pl.when(t == nt - 1)
        def _():
            if nb is None:
                o_ref[...] = acc_ref[...].astype(BF16)
            else:
                for q in range(nb):
                    o_ref[q] = acc_ref[:, owner_cols * q:owner_cols * (q + 1)].astype(BF16)

    return _call(
        body, name=name, grid=(k // tk, n // tn, nt),
        in_specs=[pl.BlockSpec((tt, tk), lambda i, j, t: (t, i)), pl.BlockSpec((tt, tn), lambda i, j, t: (t, j))],
        out_specs=out_spec, out_shape=out_shape,
        scratch_shapes=[pltpu.VMEM((tk, tn), F32)],
        compiler_params=_cparams(("parallel", "parallel", "arbitrary")))(a, b)


def _group_tn(a, b):
    t_rows = a.shape[0]
    gw = BW // 4

    def body(a_ref, b_ref, o_ref):
        o_ref[...] = _dot_tn(a_ref[...], b_ref[...])

    return _call(
        body, name="pool_group_tn", grid=(4,),
        in_specs=[pl.BlockSpec((t_rows, gw), lambda k: (0, k)), pl.BlockSpec((t_rows, gw), lambda k: (0, k))],
        out_specs=pl.BlockSpec((None, gw, gw), lambda k: (k, 0, 0)),
        out_shape=jax.ShapeDtypeStruct((4, gw, gw), F32),
        compiler_params=_cparams(("parallel",)))(a, b)


def _loss_head(x2, gamma, target, tm=512):
    t_rows = x2.shape[0]

    def body(x_ref, g_ref, t_ref, loss_ref, dx_ref, dgam_ref):
        @pl.when(pl.program_id(0) == 0)
        def _():
            loss_ref[...] = jnp.zeros_like(loss_ref)
            dgam_ref[...] = jnp.zeros_like(dgam_ref)

        def f(x, g):
            err = jnp.square(_rms(x, g) - t_ref[...])
            return 0.5 * jnp.sum(jnp.mean(err, axis=-1, keepdims=True), axis=0, keepdims=True)

        loss, vjp = jax.vjp(f, x_ref[...], g_ref[...])
        dx, dgam = vjp(jnp.ones((1, 1), F32))
        loss_ref[...] += jnp.broadcast_to(loss, (1, 128))
        dx_ref[...] = dx
        dgam_ref[...] += dgam

    row_d = pl.BlockSpec((tm, D), lambda i: (i, 0))
    return _call(
        body, name="loss_head", grid=(t_rows // tm,),
        in_specs=[row_d, pl.BlockSpec((1, D), lambda i: (0, 0)), row_d],
        out_specs=[pl.BlockSpec((1, 128), lambda i: (0, 0)), row_d, pl.BlockSpec((1, D), lambda i: (0, 0))],
        out_shape=[jax.ShapeDtypeStruct((1, 128), F32), jax.ShapeDtypeStruct((t_rows, D), F32),
                   jax.ShapeDtypeStruct((1, D), F32)],
        compiler_params=_cparams(("arbitrary",)))(x2, gamma, target)


NCHIP = NDEV // 2


def _coords():
    return lax.axis_index("x"), lax.axis_index("y"), lax.axis_index("c")


def _remote(src, dst, send_sem, recv_sem, peer):
    return pltpu.make_async_remote_copy(src_ref=src, dst_ref=dst, send_sem=send_sem, recv_sem=recv_sem,
                                        device_id=peer, device_id_type=MESH)


def _comm_call(name, srcs, out_shapes, n_rec, plan, aliases=None):
    ns, no = len(srcs), len(out_shapes)

    def body(*refs):
        ins, outs = refs[:ns], refs[ns:ns + no]
        loc_sem, send_sem, recv_sem = refs[ns + no:]
        x, y, c = _coords()
        recs = plan(ins, outs, x, y, c)
        assert len(recs) == n_rec
        for k, r in enumerate(recs):
            for src, dst in r.get("local", ()):
                pltpu.make_async_copy(src, dst, loc_sem.at[k]).start()
            for peer, src, dst in r.get("remote", ()):
                _remote(src, dst, send_sem.at[k], recv_sem.at[k], peer).start()
        for k, r in enumerate(recs):
            if r.get("recv_wait") is not None:
                w = r["recv_wait"]
                _remote(w, w, send_sem.at[k], recv_sem.at[k], (x, y, c)).wait_recv()
            if r.get("send_wait") is not None:
                w = r["send_wait"]
                _remote(w, w, send_sem.at[k], recv_sem.at[k], (x, y, c)).wait_send()
            if r.get("local_wait") is not None:
                w = r["local_wait"]
                pltpu.make_async_copy(w, w, loc_sem.at[k]).wait()

    return _call(
        body, name=name, in_specs=[ANY] * ns, out_specs=[ANY] * no, out_shape=out_shapes,
        input_output_aliases=aliases or {}, scratch_shapes=[pltpu.SemaphoreType.DMA((n_rec,))] * 3)(*srcs)


def _gather_call(srcs, out_shapes, items, aliases):
    ns, no, n = len(srcs), len(out_shapes), len(items)

    def body(*refs):
        ins, outs = refs[:ns], refs[ns:ns + no]
        loc, sib_s, sib_r, ici_s, ici_r, fwd_s, fwd_r = refs[ns + no:]
        x, y, c = _coords()
        me, sib = (x, y, c), (x, y, 1 - c)
        chips = [(1 - x, y), (x, 1 - y), (1 - x, 1 - y)]
        index = lambda px, py, pc: 4 * px + 2 * py + pc
        for k, (si, oi, shard, block, _) in enumerate(items):
            mine = block(outs[oi], index(*me))
            src = mine if shard is None else shard(ins[si])
            if shard is not None:
                pltpu.make_async_copy(src, mine, loc.at[k]).start()
            _remote(src, mine, sib_s.at[k], sib_r.at[k], sib).start()
            for chip in chips:
                _remote(src, mine, ici_s.at[k], ici_r.at[k], (*chip, c)).start()
        for k, (si, oi, _, block, blocks) in enumerate(items):
            three = blocks(outs[oi], 3)
            _remote(three, three, ici_s.at[k], ici_r.at[k], me).wait_recv()
            for chip in chips:
                landed = block(outs[oi], index(*chip, c))
                _remote(landed, landed, fwd_s.at[k], fwd_r.at[k], sib).start()
        for k, (si, oi, shard, _, blocks) in enumerate(items):
            one, three = blocks(outs[oi], 1), blocks(outs[oi], 3)
            _remote(one, one, sib_s.at[k], sib_r.at[k], me).wait()
            _remote(three, three, fwd_s.at[k], fwd_r.at[k], me).wait()
            _remote(three, three, ici_s.at[k], ici_r.at[k], me).wait_send()
            if shard is not None:
                pltpu.make_async_copy(one, one, loc.at[k]).wait()

    return _call(
        body, name="gather_weights", in_specs=[ANY] * ns, out_specs=[ANY] * no, out_shape=out_shapes,
        input_output_aliases=aliases, scratch_shapes=[pltpu.SemaphoreType.DMA((n,))] * 7)(*srcs)


_BIG = {
    "w_in": (True, D, IN_W // NDEV),
    "ssm_w_glu": (False, BW // NDEV, BW),
    "ssm_w_proj": (True, BW, D // NDEV),
    "conv_w_proj": (True, BW, D // NDEV),
    "pool_w_proj": (True, BW, D // NDEV),
    "w_out": (False, D // NDEV, D),
    "ffn_w_gate": (False, HPAD, D),
    "ffn_w_up": (False, HPAD, D),
    "ffn_w_down": (False, HPAD, D),
}
GROUP_IN = ("w_in",)
GROUP_MIX = ("ssm_w_glu", "ssm_w_proj", "conv_w_proj", "pool_w_proj", "w_out")
GROUP_FFN = ("ffn_w_gate", "ffn_w_up", "ffn_w_down")


def _gathered_shape(name):
    blocked, kk, nn = _BIG[name]
    return jax.ShapeDtypeStruct((NDEV, kk, nn) if blocked else (NDEV * kk, nn), BF16)


def _block_view(name):
    blocked, kk, _ = _BIG[name]
    if blocked:
        return lambda ref, q: ref.at[q]
    return lambda ref, q: ref.at[pl.ds(pl.multiple_of(q * kk, 16), kk), :]


def _blocks_view(name):
    blocked, kk, _ = _BIG[name]
    if blocked:
        return lambda ref, n: ref.at[pl.ds(0, n)]
    return lambda ref, n: ref.at[pl.ds(0, n * kk), :]


def _place_shards(weights, names, layer, me):
    cnt = len(names)

    def body(me_ref, *refs):
        w_refs, outs, stages, sem = refs[:cnt], refs[cnt:2 * cnt], refs[2 * cnt:3 * cnt], refs[3 * cnt]
        q = me_ref[0]
        copies = []
        for k, name in enumerate(names):
            _, kk, nn = _BIG[name]
            rows = w_refs[k].shape[0]
            stages[k][0:rows, :] = w_refs[k][...].astype(BF16)
            if rows < kk:
                stages[k][rows:kk, :] = jnp.zeros((kk - rows, nn), BF16)
            copies.append(pltpu.make_async_copy(stages[k], _block_view(name)(outs[k], q), sem.at[k]))
            copies[-1].start()
        for cp in copies:
            cp.wait()

    in_specs = [pl.BlockSpec(memory_space=pltpu.SMEM)]
    in_specs += [pl.BlockSpec((None,) + weights[n].shape[1:], lambda i: (layer, 0, 0)) for n in names]
    res = _call(
        body, name="place_shards", grid=(1,), in_specs=in_specs, out_specs=[ANY] * cnt,
        out_shape=[_gathered_shape(n) for n in names],
        scratch_shapes=[pltpu.VMEM(_BIG[n][1:], BF16) for n in names] + [pltpu.SemaphoreType.DMA((cnt,))],
        compiler_params=_cparams(("arbitrary",)))(me, *[weights[n] for n in names])
    return dict(zip(names, res))


def _gather_weights(placed, conv_dw, names):
    cnt = len(names)
    srcs = [placed[n] for n in names] + [conv_dw]
    outs = [_gathered_shape(n) for n in names] + [jax.ShapeDtypeStruct((NDEV,) + conv_dw.shape, conv_dw.dtype)]
    items = [(k, k, None, _block_view(n), _blocks_view(n)) for k, n in enumerate(names)]
    items.append((cnt, cnt, lambda ref: ref, lambda ref, q: ref.at[q], lambda ref, n: ref.at[pl.ds(0, n)]))
    res = _gather_call(srcs, outs, items, {k: k for k in range(cnt)})
    return dict(zip(names, res[:-1])), res[-1]


def _gather_start(placed, names, after, tag):
    def copies(src_refs, land_refs, x, y, c):
        me = 4 * x + 2 * y + c
        peers = [(x, y, 1 - c), (1 - x, y, c), (x, 1 - y, c), (1 - x, 1 - y, c)]
        out = []
        for k, n in enumerate(names):
            mine = _block_view(n)(land_refs[k], me)
            out.append([(peer, mine, mine) for peer in peers])
        return out

    return _split_start("gather_start_" + tag, [], [placed[n] for n in names], copies, after)


def _gather_finish(handle, names, after, tag):
    four = [functools.partial(lambda ref, bv: bv(ref, 4), bv=_blocks_view(n)) for n in names]
    lands = _split_wait("gather_wait_" + tag, handle, four, after)
    cnt = len(names)

    def plan(ins, out_refs, x, y, c):
        sib = (x, y, 1 - c)
        recs = []
        for k, n in enumerate(names):
            three = _blocks_view(n)(out_refs[k], 3)
            remote = []
            for px, py in [(1 - x, y), (x, 1 - y), (1 - x, 1 - y)]:
                landed = _block_view(n)(out_refs[k], 4 * px + 2 * py + c)
                remote.append((sib, landed, landed))
            recs.append(dict(remote=remote, send_wait=three, recv_wait=three))
        return recs

    res = _comm_call("gather_pair_" + tag, lands, [jax.ShapeDtypeStruct(l.shape, l.dtype) for l in lands], cnt, plan,
                     aliases={k: k for k in range(cnt)})
    return dict(zip(names, res))


def _pair_add(name, grads, rcv, core):
    nl = len(grads)
    _, kk, nn = grads[0].shape

    def body(c_ref, *refs):
        l = pl.program_id(0)
        own = refs[0][...]
        for j in range(1, nl):
            own = jnp.where(l == j, refs[j][...], own)
        refs[nl + 1][...] = (own.astype(F32) + refs[nl][...].astype(F32)).astype(BF16)

    gspec = lambda j: pl.BlockSpec((None, kk, nn), lambda l, h, c_ref: (jnp.where(l == j, 2 * h + c_ref[0], 0), 0, 0))
    rspec = pl.BlockSpec((None, None, kk, nn), lambda l, h, c_ref: (h, l, 0, 0))
    return _call(
        body, name="pair_add_" + name,
        grid_spec=pltpu.PrefetchScalarGridSpec(num_scalar_prefetch=1, grid=(nl, NCHIP),
                                               in_specs=[gspec(j) for j in range(nl)] + [rspec], out_specs=rspec),
        out_shape=jax.ShapeDtypeStruct(rcv.shape, BF16),
        compiler_params=_cparams(("arbitrary", "arbitrary")))(core, *grads, rcv)


def _pair_add_small(owned, lists, core):
    on, ln = list(owned), list(lists)
    flat = []
    for n in on:
        flat += list(owned[n][0]) + [owned[n][1]]
    for n in ln:
        flat += list(lists[n][0]) + [lists[n][1]]

    def body(c_ref, *refs):
        outs = refs[len(flat):]
        c = c_ref[0]
        pos = 0
        for k, n in enumerate(on):
            nl = len(owned[n][0])
            for h in range(NCHIP):
                for l in range(nl):
                    outs[k][h, l] = refs[pos + l][pl.ds(2 * h + c, 1)][0] + refs[pos + nl][h, l]
            pos += nl + 1
        for k, n in enumerate(ln):
            nl = len(lists[n][0])
            for l in range(nl):
                out = outs[len(on) + k]
                out[l] = (refs[pos + l][...] + refs[pos + nl][l]).astype(out.dtype)
            pos += nl + 1

    shapes = [jax.ShapeDtypeStruct(owned[n][1].shape, F32) for n in on]
    shapes += [jax.ShapeDtypeStruct(lists[n][1].shape, F32 if n == "final_norm" else BF16) for n in ln]
    res = _call(body, name="pair_add_small", out_shape=shapes,
                in_specs=[pl.BlockSpec(memory_space=pltpu.SMEM)] + [pl.BlockSpec(memory_space=pltpu.VMEM)] * len(flat),
                compiler_params=_cparams())(core, *flat)
    return dict(zip(on + ln, res))


def _pair_reduce(tag, big, by_owner, small, core):
    rs = {**big, **by_owner}
    srcs, outs, plans, rcv_at = [], [], [], {}
    for name, arrays in rs.items():
        rcv_at[name] = len(outs)
        outs.append(jax.ShapeDtypeStruct((NCHIP, len(arrays)) + arrays[0].shape[1:], arrays[0].dtype))
        for l, arr in enumerate(arrays):
            srcs.append(arr)
            plans.append((len(srcs) - 1, rcv_at[name], l, True))
    for name, arrays in small.items():
        rcv_at[name] = len(outs)
        outs.append(jax.ShapeDtypeStruct((len(arrays),) + arrays[0].shape, F32))
        for l, arr in enumerate(arrays):
            srcs.append(arr)
            plans.append((len(srcs) - 1, rcv_at[name], l, False))

    def plan_pair(ins, out_refs, x, y, c):
        sib = (x, y, 1 - c)
        recs = []
        for si, ro, l, slabs in plans:
            if slabs:
                four = out_refs[ro].at[pl.ds(0, NCHIP), l]
                recs.append(dict(remote=[(sib, ins[si].at[2 * h + 1 - c], out_refs[ro].at[h, l]) for h in range(NCHIP)],
                                 send_wait=four, recv_wait=four))
            else:
                dst = out_refs[ro].at[l]
                recs.append(dict(remote=[(sib, ins[si], dst)], send_wait=dst, recv_wait=dst))
        return recs

    res = _comm_call("pair_exchange_" + tag, srcs, outs, len(plans), plan_pair)
    part = {name: _pair_add(name, big[name], res[rcv_at[name]], core) for name in big}
    if by_owner or small:
        part.update(_pair_add_small({n: (by_owner[n], res[rcv_at[n]]) for n in by_owner},
                                    {n: (small[n], res[rcv_at[n]]) for n in small}, core))
    return part


def _chip_copies(src, land, slabbed, x, y, c):
    mine = 2 * x + y
    copies = []
    for step in range(1, NCHIP):
        h = (mine + step) % NCHIP
        copies.append(((h // 2, h % 2, c), src.at[h] if slabbed else src, land.at[mine]))
    return copies


def _chip_exchange(part, slabbed, keep_own):
    names = list(part)
    outs = [jax.ShapeDtypeStruct((() if n in slabbed else (NCHIP,)) + part[n].shape, part[n].dtype) for n in names]

    def plan(ins, out_refs, x, y, c):
        mine = 2 * x + y
        recs = []
        for k, n in enumerate(names):
            three = out_refs[k].at[pl.ds(0, NCHIP - 1)]
            rec = dict(remote=_chip_copies(ins[k], out_refs[k], n in slabbed, x, y, c), send_wait=three, recv_wait=three)
            if n in keep_own:
                rec["local"] = [(ins[k].at[mine] if n in slabbed else ins[k], out_refs[k].at[mine])]
                rec["local_wait"] = out_refs[k].at[0]
            recs.append(rec)
        return recs

    res = _comm_call("chip_exchange", [part[n] for n in names], outs, len(names), plan)
    return dict(zip(names, res))


HBM_SPEC = pl.BlockSpec(memory_space=pltpu.HBM)
SEM_SPEC = pl.BlockSpec(memory_space=pltpu.SEMAPHORE)
SPLIT_EFFECT = pltpu.SideEffectType.DATAFLOW_SIDE_EFFECTING


def _split_start(name, srcs, land_shapes, copies_fn, after):
    ns, n = len(srcs), len(land_shapes)
    lands = [pltpu.with_memory_space_constraint(s if isinstance(s, jax.Array) else lax.empty(s.shape, s.dtype), pltpu.HBM)
             for s in land_shapes]

    def body(*refs):
        src_refs, land_refs = refs[:ns], refs[ns:ns + n]
        send_sem, recv_sem = refs[ns + n + 1], refs[ns + n + 2]
        token = refs[-1]
        x, y, c = _coords()
        for k, copies in enumerate(copies_fn(src_refs, land_refs, x, y, c)):
            for peer, src, dst in copies:
                _remote(src, dst, send_sem.at[k], recv_sem.at[k], peer).start()
        token[...] = jnp.zeros_like(token)

    res = pl.pallas_call(
        body, name=name,
        out_shape=(pltpu.SemaphoreType.DMA((n,)), pltpu.SemaphoreType.DMA((n,)),
                   *[pltpu.HBM(s.shape, s.dtype) for s in land_shapes], jax.ShapeDtypeStruct((8, 128), F32)),
        in_specs=[HBM_SPEC] * (ns + n) + [ANY],
        out_specs=(SEM_SPEC, SEM_SPEC, *[HBM_SPEC] * n, pl.BlockSpec(memory_space=pltpu.VMEM)),
        input_output_aliases={ns + i: 2 + i for i in range(n)},
        compiler_params=pltpu.CompilerParams(has_side_effects=SPLIT_EFFECT),
    )(*[pltpu.with_memory_space_constraint(s, pltpu.HBM) for s in srcs], *lands, after)
    return dict(send=res[0], recv=res[1], srcs=list(srcs), lands=list(res[2:2 + n]), token=res[-1])


def _split_wait(name, handle, wait_views, after):
    n = len(handle["lands"])
    after = after if isinstance(after, (tuple, list)) else (after,)

    def body(*refs):
        land_refs = refs[:n]
        send_sem, recv_sem = refs[n], refs[n + 1]
        x, y, c = _coords()
        for k in range(n):
            w = wait_views[k](land_refs[k])
            cp = _remote(w, w, send_sem.at[k], recv_sem.at[k], (x, y, c))
            cp.wait_send()
            cp.wait_recv()

    res = pl.pallas_call(
        body, name=name,
        out_shape=tuple(pltpu.HBM(s.shape, s.dtype) for s in handle["lands"]),
        in_specs=[HBM_SPEC] * n + [SEM_SPEC, SEM_SPEC] + [ANY] * len(after), out_specs=tuple([HBM_SPEC] * n),
        input_output_aliases={i: i for i in range(n)},
        compiler_params=pltpu.CompilerParams(has_side_effects=SPLIT_EFFECT),
    )(*handle["lands"], handle["send"], handle["recv"], *after)
    return list(res)


def _adamw(w, g, m, v):
    m = ADAM_B1 * m + (1.0 - ADAM_B1) * g
    v = ADAM_B2 * v + (1.0 - ADAM_B2) * jnp.square(g)
    m_hat = m / (1.0 - ADAM_B1 ** ADAM_STEP)
    v_hat = v / (1.0 - ADAM_B2 ** ADAM_STEP)
    delta = -ADAM_LR * (m_hat / (jnp.sqrt(v_hat) + ADAM_EPS) + ADAM_WD * w)
    return delta, m, v


def _chip_start(part, tag):
    names = list(part)

    def copies(src_refs, land_refs, x, y, c):
        return [_chip_copies(src_refs[k], land_refs[k], True, x, y, c) for k in range(len(names))]

    shapes = [jax.ShapeDtypeStruct(part[n].shape, part[n].dtype) for n in names]
    return names, _split_start("chip_start_" + tag, [part[n] for n in names], shapes, copies, part[names[0]])


def _chip_wait(names, handle, after, tag):
    three = [lambda ref: ref.at[pl.ds(0, NCHIP - 1)]] * len(names)
    return dict(zip(names, _split_wait("chip_wait_" + tag, handle, three, after)))


def _adam_big(name, own, recv, w, m, v, tk, chip):
    nl = len(own)
    kk, nn = w.shape[1], w.shape[2]
    nnp = own[0].shape[3]

    def body(chip_ref, *refs):
        l = pl.program_id(0)
        g = None
        for step in range(NCHIP):
            val = refs[step][...]
            for q in range(1, nl):
                val = jnp.where(l == q, refs[NCHIP * q + step][...], val)
            g = val.astype(F32) if g is None else g + val.astype(F32)
        w_ref, m_ref, v_ref, g_ref, d_ref, mo_ref, vo_ref = refs[NCHIP * nl:]
        g = g[:, :nn]
        delta, m2, v2 = _adamw(w_ref[...], g, m_ref[...], v_ref[...])
        g_ref[...] = g
        d_ref[...] = delta
        mo_ref[...] = m2
        vo_ref[...] = v2

    def slab(q, step):
        return pl.BlockSpec((None, None, tk, nnp), lambda l, i, chip_ref: (
            jnp.where(l == q, (chip_ref[0] + step) % NCHIP, 0), 0, jnp.where(l == q, i, 0), 0))

    in_specs, operands = [], []
    for q in range(nl):
        in_specs += [slab(q, step) for step in range(NCHIP)]
        operands += [own[q]] + [recv[q]] * (NCHIP - 1)
    wspec = pl.BlockSpec((None, tk, nn), lambda l, i, chip_ref: (l, i, 0))
    shape = jax.ShapeDtypeStruct(w.shape, F32)
    return _call(
        body, name="adamw_" + name,
        grid_spec=pltpu.PrefetchScalarGridSpec(num_scalar_prefetch=1, grid=(nl, kk // tk),
                                               in_specs=in_specs + [wspec] * 3, out_specs=[wspec] * 4),
        out_shape=[shape] * 4, compiler_params=_cparams(("arbitrary", "arbitrary")))(chip, *operands, w, m, v)


def _adam_small(names, recv, w, m, v):
    n = len(names)

    def body(*refs):
        r, ww, mm, vv = refs[:n], refs[n:2 * n], refs[2 * n:3 * n], refs[3 * n:4 * n]
        outs = refs[4 * n:]
        for k in range(n):
            for l in range(r[k].shape[1]):
                g = r[k][0, l].astype(F32)
                for h in range(1, NCHIP):
                    g = g + r[k][h, l].astype(F32)
                per_layer = ww[k].shape[1:]
                if g.shape[0] > per_layer[0] and g.shape[1:] == per_layer[1:]:
                    g = g[:per_layer[0]]
                at = l if g.shape == per_layer else pl.ds(l, 1)
                delta, m2, v2 = _adamw(ww[k][at], g, mm[k][at], vv[k][at])
                outs[k][at] = g
                outs[n + k][at] = delta
                outs[2 * n + k][at] = m2
                outs[3 * n + k][at] = v2

    shapes = [jax.ShapeDtypeStruct(w[k].shape, F32) for k in names]
    res = _call(body, name="adamw_small", out_shape=shapes * 4, compiler_params=_cparams())(
        *[recv[k] for k in names], *[w[k] for k in names], *[m[k] for k in names], *[v[k] for k in names])
    return {k: (res[i], res[n + i], res[2 * n + i], res[3 * n + i]) for i, k in enumerate(names)}


def _expand_b(bt):
    eye = jnp.eye(GB, dtype=bt.dtype)
    return jnp.einsum("jgpn,gh->jgphn", bt.reshape(NBLK, GB, SGRP, NSTATE), eye).reshape(NBLK, GB * SGRP, NS)


def _extract_b(db):
    x = db.reshape(NBLK, GB, SGRP, GB, NSTATE)
    eye = jnp.eye(GB, dtype=db.dtype)
    return jnp.einsum("jgphn,gh->jgpn", x, eye).reshape(NGRP, SGRP, NSTATE)


def _expand_c(c):
    ct = jnp.transpose(c, (0, 2, 1)).reshape(NBLK, GB, NSTATE, SGRP)
    eye = jnp.eye(GB, dtype=c.dtype)
    return jnp.einsum("jgnp,gh->jgnhp", ct, eye).reshape(NBLK, NS, GB * SGRP)


def _extract_c(dc):
    x = dc.reshape(NBLK, GB, NSTATE, GB, SGRP)
    eye = jnp.eye(GB, dtype=dc.dtype)
    d = jnp.einsum("jgnhp,gh->jgnp", x, eye).reshape(NGRP, NSTATE, SGRP)
    return jnp.transpose(d, (0, 2, 1))


_SMALL = ("norm1", "b_gate", "ssm_a_re", "ssm_a_im", "ssm_log_dt", "ssm_b_re", "ssm_b_im", "ssm_c_re", "ssm_c_im",
          "ssm_d", "ssm_b_glu", "conv_b_dw", "conv_ln_g", "conv_ln_b", "pool_w_group", "pool_scale", "norm2")
_ADAM_TK = {"w_in": 256, "ssm_w_glu": 64, "ssm_w_proj": 512, "conv_w_proj": 512, "pool_w_proj": 512, "w_out": 128,
            "ffn_w_gate": HSH, "ffn_w_up": HSH, "ffn_w_down": HSH}
_OUT_ORDER = ("norm1", "w_in", "b_gate", "ssm_a_re", "ssm_a_im", "ssm_log_dt", "ssm_b_re", "ssm_b_im", "ssm_c_re",
              "ssm_c_im", "ssm_d", "ssm_w_glu", "ssm_b_glu", "ssm_w_proj", "conv_w_dw", "conv_b_dw", "conv_ln_g",
              "conv_ln_b", "conv_w_proj", "pool_w_group", "pool_scale", "pool_w_proj", "w_out", "norm2", "ffn_w_gate",
              "ffn_w_up", "ffn_w_down", "final_norm")


def _layer_fwd(x, p, token, late_params=None):
    z, h = _inproj_fwd(x, p["norm1"], p["w_in"], token)
    yssm, hre, him = _ssm_fwd(z, p)
    cv = _conv_fwd(z, p["conv_w"], p["conv_b"])
    pbar = _pool_fwd(z)
    if late_params is not None:
        more, token = late_params((yssm, cv, pbar))
        p = {**p, **more}
    x1 = _merge_fwd(x, yssm, cv, pbar, z, p, token)
    x2, gpre, upre, h2 = _ffn_fwd(x1, p["norm2"], p["wg"], p["wu"], p["wd"])
    return x2, dict(x=x, h=h, z=z, yssm=yssm, hre=hre, him=him, cv=cv, pbar=pbar, x1=x1, gpre=gpre, upre=upre, h2=h2), p


def _layer_bwd(dx, p, s, token, leave=None):
    big, small = {}, {}
    go = (lambda tag, names: leave(tag, {n: big[n] for n in names})) if leave else (lambda tag, names: token)
    h2 = s["h2"]
    dx1, d_norm2, dgp, dup, act = _ffn_bwd(dx, s["x1"], p["norm2"], s["gpre"], s["upre"], p["wg"], p["wu"], p["wd"],
                                               token)
    big["ffn_w_gate"] = _matmul_tn(dgp, h2, "tn_gate").reshape(NDEV, HPAD, D)
    big["ffn_w_up"] = _matmul_tn(dup, h2, "tn_up").reshape(NDEV, HPAD, D)
    big["ffn_w_down"] = _matmul_tn(act, dx, "tn_down").reshape(NDEV, HPAD, D)
    (dy, dcv, dpb, dzg, a_g, a_outa, a_hs, a_pb, a_pc, a_mg, c_glu, c_ya, c_yb, c_p, c_yc,
     d_bglu, d_lng, d_lnb, d_scale, d_bgate) = _merge_bwd(dx1, s["x"], s["yssm"], s["cv"], s["pbar"], s["z"], p,
                                                          go("ffn", GROUP_FFN))
    big["ssm_w_glu"] = _matmul_tn(a_g, c_glu, "tn_glu").reshape(NDEV, BW // NDEV, BW)
    big["ssm_w_proj"] = _matmul_tn(a_outa, c_ya, "tn_ssm_proj", D // NDEV)
    big["conv_w_proj"] = _matmul_tn(a_hs, c_yb, "tn_conv_proj", D // NDEV)
    big["pool_w_proj"] = _matmul_tn(a_pc, c_yc, "tn_pool_proj", D // NDEV)
    big["w_out"] = _matmul_tn(a_mg, dx1, "tn_out").reshape(NDEV, D // NDEV, D)
    d_wgrp = _group_tn(a_pb, c_p)
    du_a, dbr, dbi, dcr, dci, dd, dar, dai, dldt = _ssm_bwd(dy, s["z"], s["hre"], s["him"], p, go("mix", GROUP_MIX))
    dva, dvb, dw8, dcb = _conv_bwd(dcv, s["z"], p["conv_w"])
    du_c = _pool_bwd(dpb)
    dz = dzg
    for k, piece in enumerate((du_a, dva, dvb, du_c)):
        dz = lax.dynamic_update_slice(dz, piece, (0, k * BW))
    big["w_in"] = _matmul_tn(s["h"], dz, "tn_in", IN_W // NDEV)
    dx0, d_norm1 = _inproj_bwd(dz, dx1, s["x"], p["norm1"], p["w_in"], go("in", GROUP_IN))
    small["norm1"] = d_norm1
    small["b_gate"] = d_bgate
    small["ssm_a_re"] = dar.reshape(NGRP, NSTATE)
    small["ssm_a_im"] = dai.reshape(NGRP, NSTATE)
    small["ssm_log_dt"] = dldt.reshape(NBLK, 8, 128)[:, 0, :GB].reshape(1, NGRP)
    small["ssm_b_re"] = _extract_b(dbr)
    small["ssm_b_im"] = _extract_b(dbi)
    small["ssm_c_re"] = _extract_c(dcr)
    small["ssm_c_im"] = _extract_c(dci)
    small["ssm_d"] = dd.reshape(NGRP, SGRP)
    small["ssm_b_glu"] = d_bglu
    small["conv_b_dw"] = dcb
    small["conv_ln_g"] = d_lng
    small["conv_ln_b"] = d_lnb
    small["pool_w_group"] = d_wgrp
    small["pool_scale"] = d_scale
    small["norm2"] = d_norm2
    return dx0, big, dw8, small


def _train_step(a):
    t_rows = a["x"].shape[1]
    x0 = a["x"].reshape(t_rows, D)
    target = a["loss_target"].reshape(t_rows, D)

    tr = lambda w: jnp.transpose(w, (0, 2, 1))
    weights = {name: (tr(a[name]) if name in ("ffn_w_gate", "ffn_w_up") else a[name]) for name in _BIG}
    core = lax.axis_index("c").astype(jnp.int32).reshape(1)
    chip = (2 * lax.axis_index("x") + lax.axis_index("y")).astype(jnp.int32).reshape(1)
    me = 2 * chip + core
    no_token = jnp.zeros((8, 128), F32)
    row = lambda v: v.reshape(1, -1)
    rest = GROUP_MIX + GROUP_FFN
    first, dw_all = _gather_weights(_place_shards(weights, GROUP_IN, 0, me),
                                    a["conv_w_dw"].reshape(DEPTH, CONV_K, BW // NDEV), GROUP_IN)
    conv_w = jnp.transpose(dw_all, (1, 2, 0, 3)).reshape(DEPTH, CONV_K, BW)
    go_rest0 = _gather_start(_place_shards(weights, rest, 0, me), rest, dw_all, "rest0")
    going = {}

    def early_params(l, w_in):
        return dict(
            norm1=row(a["norm1"][l]), w_in=w_in,
            are=row(a["ssm_a_re"][l]), aim=row(a["ssm_a_im"][l]),
            ldt=row(jnp.repeat(a["ssm_log_dt"][l], NSTATE)),
            bexp_re=_expand_b(jnp.transpose(a["ssm_b_re"][l], (0, 2, 1))),
            bexp_im=_expand_b(jnp.transpose(a["ssm_b_im"][l], (0, 2, 1))),
            cexp_re=_expand_c(a["ssm_c_re"][l]), cexp_im=_expand_c(a["ssm_c_im"][l]),
            dskip=row(a["ssm_d"][l]), conv_w=conv_w[l], conv_b=row(a["conv_b_dw"][l]))

    def late_params(l, handle, tag, then_start):
        def get(after):
            full = _gather_finish(handle, rest, after, tag)
            token = then_start(full["ffn_w_down"]) if then_start else no_token
            return dict(
                wglu=full["ssm_w_glu"], bglu=row(a["ssm_b_glu"][l]), wpa=full["ssm_w_proj"],
                lng=row(a["conv_ln_g"][l]), lnb=row(a["conv_ln_b"][l]), wpb=full["conv_w_proj"],
                wgrp=a["pool_w_group"][l].astype(BF16), scale=row(a["pool_scale"][l]), wpc=full["pool_w_proj"],
                bgate=row(a["b_gate"][l]), wout=full["w_out"],
                norm2=row(a["norm2"][l]), wg=full["ffn_w_gate"], wu=full["ffn_w_up"], wd=full["ffn_w_down"]), token
        return get

    def start_in1(after):
        going["in1"] = _gather_start(_place_shards(weights, GROUP_IN, 1, me), GROUP_IN, after, "in1")
        return going["in1"]["token"]

    x, s0, p0 = _layer_fwd(x0, early_params(0, first["w_in"]), go_rest0["token"], late_params(0, go_rest0, "rest0", start_in1))
    w_in1 = _gather_finish(going["in1"], GROUP_IN, x, "in1")["w_in"]
    go_rest1 = _gather_start(_place_shards(weights, rest, 1, me), rest, w_in1, "rest1")
    x, s1, p1 = _layer_fwd(x, early_params(1, w_in1), go_rest1["token"], late_params(1, go_rest1, "rest1", None))
    params, saved = [p0, p1], [s0, s1]

    loss_part, dx, d_final = _loss_head(x, a["final_norm"].reshape(1, D), target)
    loss = lax.psum(loss_part[0, 0], ("x", "y", "c"))

    dx, gb1, go1, gs1 = _layer_bwd(dx, params[1], saved[1], no_token)
    part1 = _pair_reduce("late", {n: [g] for n, g in gb1.items()}, {}, {}, core)
    rs_names, rs_handle = _chip_start(part1, "late")
    part0, gone = {}, []

    def leave(tag, grads):
        part = _pair_reduce(tag, {n: [g] for n, g in grads.items()}, {}, {}, core)
        part0.update(part)
        gone.append(_chip_start(part, tag) + (tag,))
        return gone[-1][1]["token"]

    dx, gb0, go0, gs0 = _layer_bwd(dx, params[0], saved[0], rs_handle["token"], leave)
    grad_x = dx.reshape(1, t_rows, D)
    small = {n: [gs0[n], gs1[n]] for n in _SMALL}
    small["final_norm"] = [d_final]
    part_small = _pair_reduce("rest", {}, {"conv_w_dw": [go0, go1]}, small, core)
    recv = _chip_exchange(part_small, {"conv_w_dw"}, set(part_small))
    recv1 = _chip_wait(rs_names, rs_handle, dx, "late")
    for names_, handle, tag in gone:
        recv.update(_chip_wait(names_, handle, dx, tag))

    results = {}
    for name in _BIG:
        fix = tr if name in ("ffn_w_gate", "ffn_w_up") else (lambda t: t)
        res = _adam_big(name, [part0[name], part1[name]], [recv[name], recv1[name]], fix(a[name]), fix(a["m_" + name]),
                        fix(a["v_" + name]), _ADAM_TK[name], chip)
        results[name] = tuple(fix(r) for r in res)

    lay = {
        "ssm_b_re": lambda v: jnp.transpose(v, (0, 1, 3, 2)), "ssm_b_im": lambda v: jnp.transpose(v, (0, 1, 3, 2)),
        "conv_w_dw": lambda v: v.reshape(DEPTH, CONV_K, BW // NDEV), "final_norm": lambda v: v.reshape(1, 1, D),
    }
    names = _SMALL + ("conv_w_dw", "final_norm")
    relay = lambda k, v: lay[k](v) if k in lay else v
    sm = _adam_small(names, recv, {k: relay(k, a[k]) for k in names}, {k: relay(k, a["m_" + k]) for k in names},
                     {k: relay(k, a["v_" + k]) for k in names})
    for k in names:
        back = (lambda r: jnp.transpose(r, (0, 1, 3, 2))) if k in ("ssm_b_re", "ssm_b_im") else (lambda r: r.reshape(a[k].shape))
        results[k] = tuple(back(r) for r in sm[k])

    outs = [loss, grad_x]
    for part in range(4):
        outs += [results[k][part] for k in _OUT_ORDER]
    return tuple(outs)


def kernel(x, norm1, w_in, b_gate, ssm_a_re, ssm_a_im, ssm_log_dt, ssm_b_re, ssm_b_im, ssm_c_re, ssm_c_im, ssm_d, ssm_w_glu, ssm_b_glu, ssm_w_proj, conv_w_dw, conv_b_dw, conv_ln_g, conv_ln_b, conv_w_proj, pool_w_group, pool_scale, pool_w_proj, w_out, norm2, ffn_w_gate, ffn_w_up, ffn_w_down, final_norm, loss_target, m_norm1, m_w_in, m_b_gate, m_ssm_a_re, m_ssm_a_im, m_ssm_log_dt, m_ssm_b_re, m_ssm_b_im, m_ssm_c_re, m_ssm_c_im, m_ssm_d, m_ssm_w_glu, m_ssm_b_glu, m_ssm_w_proj, m_conv_w_dw, m_conv_b_dw, m_conv_ln_g, m_conv_ln_b, m_conv_w_proj, m_pool_w_group, m_pool_scale, m_pool_w_proj, m_w_out, m_norm2, m_ffn_w_gate, m_ffn_w_up, m_ffn_w_down, m_final_norm, v_norm1, v_w_in, v_b_gate, v_ssm_a_re, v_ssm_a_im, v_ssm_log_dt, v_ssm_b_re, v_ssm_b_im, v_ssm_c_re, v_ssm_c_im, v_ssm_d, v_ssm_w_glu, v_ssm_b_glu, v_ssm_w_proj, v_conv_w_dw, v_conv_b_dw, v_conv_ln_g, v_conv_ln_b, v_conv_w_proj, v_pool_w_group, v_pool_scale, v_pool_w_proj, v_w_out, v_norm2, v_ffn_w_gate, v_ffn_w_up, v_ffn_w_down, v_final_norm):
    return _train_step(dict(locals()))
```

```python
import functools

import jax
import jax.numpy as jnp
from jax import lax
from jax.experimental import pallas as pl
from jax.experimental.pallas import tpu as pltpu

F32 = jnp.float32
BF16 = jnp.bfloat16

NDEV = 8
DEPTH = 2
D = 1024
BW = 512
NSTATE = 64
SGRP = 16
NGRP = BW // SGRP
GB = 8
NBLK = NGRP // GB
NS = GB * NSTATE
CONV_K = 31
HALO = 32
PHALO = 16
IN_W = 5120
HID = 2816
HSH = HID // NDEV
HPAD = 384
HIDP = HPAD * NDEV
EPS = 1e-6
VMEM_LIMIT = 56 * 1024 * 1024

ADAM_LR, ADAM_B1, ADAM_B2, ADAM_EPS, ADAM_WD, ADAM_STEP = 0.001, 0.9, 0.999, 1e-08, 0.01, 10

MESH = pl.DeviceIdType.MESH
ANY = pl.BlockSpec(memory_space=pl.ANY)


def _call(body, **kw):
    return pl.pallas_call(body, **kw)


def _cparams(sem=None):
    return pltpu.CompilerParams(dimension_semantics=sem, vmem_limit_bytes=VMEM_LIMIT)


def _dot(a, b):
    return jnp.dot(a.astype(BF16), b.astype(BF16), preferred_element_type=F32)


def _dot_nt(a, b):
    return lax.dot_general(a.astype(BF16), b.astype(BF16), (((1,), (1,)), ((), ())), preferred_element_type=F32)


def _dot_tn(a, b):
    return lax.dot_general(a.astype(BF16), b.astype(BF16), (((0,), (0,)), ((), ())), preferred_element_type=F32)


@jax.custom_vjp
def _mm(a, w):
    return _dot(a, w)


def _mm_fwd(a, w):
    return _dot(a, w), w


def _mm_bwd(w, ct):
    return _dot_nt(ct, w), jnp.zeros_like(w)


_mm.defvjp(_mm_fwd, _mm_bwd)


def _rms(x, g):
    return x * lax.rsqrt(jnp.mean(x * x, axis=-1, keepdims=True) + EPS) * g


def _disc(are, aim, ldt):
    dt = jnp.exp(ldt)
    mag = jnp.exp(dt * are)
    ang = dt * aim
    abr = mag * jnp.cos(ang)
    abi = mag * jnp.sin(ang)
    den = are * are + aim * aim
    nr = abr - 1.0
    fr = (nr * are + abi * aim) / den
    fi = (abi * are - nr * aim) / den
    return abr, abi, fr, fi


def _bbar(fr, fi, br, bi):
    return fr * br - fi * bi, fr * bi + fi * br


def _cmul(ar, ai, br, bi):
    return ar * br - ai * bi, ar * bi + ai * br


def _scan_rows(re_ref, im_ref, ar, ai, n_rows, reverse, hre_ref=None, him_ref=None):
    n = ar.shape[1]
    shape = (8, n)
    rows = lax.broadcasted_iota(jnp.int32, shape, 0)
    a1 = (jnp.broadcast_to(ar, shape), jnp.broadcast_to(ai, shape))
    a2 = _cmul(*a1, *a1)
    a4 = _cmul(*a2, *a2)
    pr = jnp.zeros(shape, F32)
    pi = jnp.zeros(shape, F32)
    pw = a1
    for k in range(8):
        sel = rows == ((7 - k) if reverse else k)
        pr = jnp.where(sel, pw[0], pr)
        pi = jnp.where(sel, pw[1], pi)
        pw = _cmul(*pw, *a1)
    nt = n_rows // 8
    with_acc = hre_ref is not None

    def body(i, carry):
        cr, ci = carry[0], carry[1]
        t = (nt - 1 - i) if reverse else i
        off = pl.multiple_of(t * 8, 8)
        xr = re_ref[pl.ds(off, 8), :]
        xi = im_ref[pl.ds(off, 8), :]
        for k, (kr, ki) in ((1, a1), (2, a2), (4, a4)):
            if reverse:
                keep, sh = rows < 8 - k, 8 - k
            else:
                keep, sh = rows >= k, k
            sr = jnp.where(keep, pltpu.roll(xr, sh, 0), 0.0)
            si = jnp.where(keep, pltpu.roll(xi, sh, 0), 0.0)
            xr, xi = xr + kr * sr - ki * si, xi + kr * si + ki * sr
        xr, xi = xr + pr * cr - pi * ci, xi + pr * ci + pi * cr
        re_ref[pl.ds(off, 8), :] = xr
        im_ref[pl.ds(off, 8), :] = xi
        edge = 0 if reverse else 7
        out = (jnp.broadcast_to(xr[edge:edge + 1, :], shape), jnp.broadcast_to(xi[edge:edge + 1, :], shape))
        if with_acc:
            hr = hre_ref[pl.ds(off, 8), :]
            hi = him_ref[pl.ds(off, 8), :]
            offp = pl.multiple_of(jnp.maximum(t - 1, 0) * 8, 8)
            live = jnp.where(t > 0, 1.0, 0.0)
            lr = jnp.broadcast_to(hre_ref[pl.ds(offp, 8), :][7:8, :], shape) * live
            li = jnp.broadcast_to(him_ref[pl.ds(offp, 8), :][7:8, :], shape) * live
            hpr = jnp.where(rows == 0, lr, pltpu.roll(hr, 1, 0))
            hpi = jnp.where(rows == 0, li, pltpu.roll(hi, 1, 0))
            out = out + (carry[2] + xr * hpr + xi * hpi, carry[3] + xi * hpr - xr * hpi)
        return out

    zero = jnp.zeros(shape, F32)
    init = (zero, zero, zero, zero) if with_acc else (zero, zero)
    res = lax.fori_loop(0, nt, body, init)
    return res[2:] if with_acc else None


TOKEN_SPEC = pl.BlockSpec((8, 128), lambda i, j: (0, 0))


def _inproj_fwd(x, gamma, w, token, tm=1024, nb=2):
    t_rows = x.shape[0]
    tm = min(tm, t_rows)
    oc = w.shape[2]
    n = NDEV * oc
    tn = nb * oc

    def body(x_ref, g_ref, w_ref, token_ref, z_ref, h_ref):
        @pl.when(pl.program_id(1) == 0)
        def _():
            h_ref[...] = _rms(x_ref[...], g_ref[...]).astype(BF16)
        for q in range(nb):
            z_ref[:, oc * q:oc * (q + 1)] = jnp.dot(h_ref[...], w_ref[q], preferred_element_type=F32)

    return _call(
        body, name="inproj_fwd", grid=(t_rows // tm, n // tn),
        in_specs=[pl.BlockSpec((tm, D), lambda i, j: (i, 0)), pl.BlockSpec((1, D), lambda i, j: (0, 0)),
                  pl.BlockSpec((nb, D, oc), lambda i, j: (j, 0, 0)), TOKEN_SPEC],
        out_specs=[pl.BlockSpec((tm, tn), lambda i, j: (i, j)), pl.BlockSpec((tm, D), lambda i, j: (i, 0))],
        out_shape=[jax.ShapeDtypeStruct((t_rows, n), F32), jax.ShapeDtypeStruct((t_rows, D), BF16)],
        compiler_params=_cparams(("parallel", "arbitrary")))(x, gamma, w, token)


def _ssm_specs(t_rows):
    row = pl.BlockSpec((1, NS), lambda j: (0, j))
    return dict(
        u=pl.BlockSpec((t_rows, GB * SGRP), lambda j: (0, j)),
        row=row,
        bexp=pl.BlockSpec((None, GB * SGRP, NS), lambda j: (j, 0, 0)),
        cexp=pl.BlockSpec((None, NS, GB * SGRP), lambda j: (j, 0, 0)),
        d=pl.BlockSpec((1, GB * SGRP), lambda j: (0, j)),
        h=pl.BlockSpec((t_rows, NS), lambda j: (0, j)),
    )


def _ssm_fwd(z, p):
    t_rows = z.shape[0]
    s = _ssm_specs(t_rows)

    def body(u_ref, are_ref, aim_ref, ldt_ref, br_ref, bi_ref, cr_ref, ci_ref, d_ref, y_ref, hr_ref, hi_ref):
        abr, abi, fr, fi = _disc(are_ref[...], aim_ref[...], ldt_ref[...])
        bbr, bbi = _bbar(fr, fi, br_ref[...], bi_ref[...])
        u = u_ref[...]
        hr_ref[...] = _dot(u, bbr)
        hi_ref[...] = _dot(u, bbi)
        _scan_rows(hr_ref, hi_ref, abr, abi, t_rows, False)
        y_ref[...] = _dot(hr_ref[...], cr_ref[...]) - _dot(hi_ref[...], ci_ref[...]) + d_ref[...] * u

    return _call(
        body, name="ssm_fwd", grid=(NBLK,),
        in_specs=[s["u"], s["row"], s["row"], s["row"], s["bexp"], s["bexp"], s["cexp"], s["cexp"], s["d"]],
        out_specs=[s["u"], s["h"], s["h"]],
        out_shape=[jax.ShapeDtypeStruct((t_rows, BW), F32), jax.ShapeDtypeStruct((t_rows, NGRP * NSTATE), F32),
                   jax.ShapeDtypeStruct((t_rows, NGRP * NSTATE), F32)],
        compiler_params=_cparams(("parallel",)))(
            z, p["are"], p["aim"], p["ldt"], p["bexp_re"], p["bexp_im"], p["cexp_re"], p["cexp_im"], p["dskip"])


def _ssm_bwd(dy, z, hre, him, p, token):
    t_rows = z.shape[0]
    s = _ssm_specs(t_rows)
    nstates = NGRP * NSTATE

    def body(dy_ref, u_ref, hr_ref, hi_ref, are_ref, aim_ref, ldt_ref, br_ref, bi_ref, cr_ref, ci_ref, d_ref, token_ref,
             du_ref, dbr_ref, dbi_ref, dcr_ref, dci_ref, dd_ref, dar_ref, dai_ref, dldt_ref, lr_ref, li_ref):
        rows3 = (are_ref[...], aim_ref[...], ldt_ref[...])
        (abr, abi, fr, fi), disc_vjp = jax.vjp(_disc, *rows3)
        (bbr, bbi), bbar_vjp = jax.vjp(_bbar, fr, fi, br_ref[...], bi_ref[...])
        dy = dy_ref[...]
        u = u_ref[...]
        lr_ref[...] = _dot_nt(dy, cr_ref[...])
        li_ref[...] = -_dot_nt(dy, ci_ref[...])
        dcr_ref[...] = _dot_tn(hr_ref[...], dy)
        dci_ref[...] = -_dot_tn(hi_ref[...], dy)
        dd_ref[...] = jnp.sum(dy * u, axis=0, keepdims=True)
        acc_r, acc_i = _scan_rows(lr_ref, li_ref, abr, -abi, t_rows, True, hr_ref, hi_ref)
        dabr = jnp.sum(acc_r, axis=0, keepdims=True)
        dabi = jnp.sum(acc_i, axis=0, keepdims=True)
        lam_r = lr_ref[...]
        lam_i = li_ref[...]
        du = d_ref[...] * dy + _dot_nt(lam_r, bbr) + _dot_nt(lam_i, bbi)
        du_ref[...] = du.astype(BF16)
        dbbr = _dot_tn(u, lam_r)
        dbbi = _dot_tn(u, lam_i)
        dfr, dfi, dbr, dbi = bbar_vjp((dbbr, dbbi))
        dbr_ref[...] = dbr
        dbi_ref[...] = dbi
        dar, dai, dldt = disc_vjp((dabr, dabi, dfr, dfi))
        dar_ref[...] = dar
        dai_ref[...] = dai
        lane_grp = lax.broadcasted_iota(jnp.int32, (NS, 128), 0) // NSTATE
        col = lax.broadcasted_iota(jnp.int32, (NS, 128), 1)
        seg = jnp.where(lane_grp == col, 1.0, 0.0).astype(F32)
        dldt_ref[...] = jnp.dot(jnp.broadcast_to(dldt, (8, NS)), seg, preferred_element_type=F32,
                                precision=lax.Precision.HIGHEST)

    dyspec = pl.BlockSpec((t_rows, GB * SGRP), lambda j: (0, j))
    return _call(
        body, name="ssm_bwd", grid=(NBLK,),
        in_specs=[dyspec, s["u"], s["h"], s["h"], s["row"], s["row"], s["row"], s["bexp"], s["bexp"], s["cexp"],
                  s["cexp"], s["d"], pl.BlockSpec((8, 128), lambda j: (0, 0))],
        out_specs=[dyspec, s["bexp"], s["bexp"], s["cexp"], s["cexp"], s["d"], s["row"], s["row"],
                   pl.BlockSpec((8, 128), lambda j: (j, 0))],
        out_shape=[jax.ShapeDtypeStruct((t_rows, BW), BF16),
                   jax.ShapeDtypeStruct((NBLK, GB * SGRP, NS), F32), jax.ShapeDtypeStruct((NBLK, GB * SGRP, NS), F32),
                   jax.ShapeDtypeStruct((NBLK, NS, GB * SGRP), F32), jax.ShapeDtypeStruct((NBLK, NS, GB * SGRP), F32),
                   jax.ShapeDtypeStruct((1, BW), F32), jax.ShapeDtypeStruct((1, nstates), F32),
                   jax.ShapeDtypeStruct((1, nstates), F32), jax.ShapeDtypeStruct((NBLK * 8, 128), F32)],
        scratch_shapes=[pltpu.VMEM((t_rows, NS), F32), pltpu.VMEM((t_rows, NS), F32)],
        compiler_params=_cparams(("parallel",)))(
            dy, z, hre, him, p["are"], p["aim"], p["ldt"], p["bexp_re"], p["bexp_im"], p["cexp_re"], p["cexp_im"],
            p["dskip"], token)


def _conv_fwd(z, w, b, tm=256):
    t_rows = z.shape[0]
    hb = tm // HALO

    def body(va_ref, vb_ref, ha_ref, hb_ref, w_ref, b_ref, o_ref, win_ref):
        live = jnp.where(pl.program_id(0) > 0, 1.0, 0.0)
        win_ref[0:HALO, :] = ha_ref[...] * jax.nn.sigmoid(hb_ref[...]) * live
        win_ref[HALO:HALO + tm, :] = va_ref[...] * jax.nn.sigmoid(vb_ref[...])
        acc = jnp.broadcast_to(b_ref[...], (tm, BW))
        for k in range(CONV_K):
            acc = acc + w_ref[k:k + 1, :] * win_ref[pl.ds(HALO - (CONV_K - 1) + k, tm), :]
        o_ref[...] = acc

    halo = lambda col: pl.BlockSpec((HALO, BW), lambda i: (jnp.maximum(i * hb - 1, 0), col))
    return _call(
        body, name="conv_fwd", grid=(t_rows // tm,),
        in_specs=[pl.BlockSpec((tm, BW), lambda i: (i, 1)), pl.BlockSpec((tm, BW), lambda i: (i, 2)), halo(1), halo(2),
                  pl.BlockSpec((CONV_K, BW), lambda i: (0, 0)), pl.BlockSpec((1, BW), lambda i: (0, 0))],
        out_specs=pl.BlockSpec((tm, BW), lambda i: (i, 0)),
        out_shape=jax.ShapeDtypeStruct((t_rows, BW), F32),
        scratch_shapes=[pltpu.VMEM((HALO + tm, BW), F32)],
        compiler_params=_cparams(("parallel",)))(z, z, z, z, w, b)


def _conv_bwd(dcv, z, w, tm=256):
    t_rows = z.shape[0]
    nt = t_rows // tm
    hb = tm // HALO
    csh = BW // NDEV

    def body(d_ref, dn_ref, va_ref, vb_ref, ha_ref, hb_ref, w_ref, dva_ref, dvb_ref, dw8_ref, db_ref,
             hwin_ref, dwin_ref, dw_ref):
        i = pl.program_id(0)

        @pl.when(i == 0)
        def _():
            dw_ref[...] = jnp.zeros_like(dw_ref)
            db_ref[...] = jnp.zeros_like(db_ref)

        live_prev = jnp.where(i > 0, 1.0, 0.0)
        live_next = jnp.where(i < nt - 1, 1.0, 0.0)
        va = va_ref[...]
        sig = jax.nn.sigmoid(vb_ref[...])
        hwin_ref[0:HALO, :] = ha_ref[...] * jax.nn.sigmoid(hb_ref[...]) * live_prev
        hwin_ref[HALO:HALO + tm, :] = va * sig
        d = d_ref[...]
        dwin_ref[0:tm, :] = d
        dwin_ref[tm:tm + HALO, :] = dn_ref[...] * live_next
        dh = jnp.zeros((tm, BW), F32)
        dws = []
        for k in range(CONV_K):
            dh = dh + w_ref[k:k + 1, :] * dwin_ref[pl.ds(CONV_K - 1 - k, tm), :]
            dws.append(jnp.sum(d * hwin_ref[pl.ds(HALO - (CONV_K - 1) + k, tm), :], axis=0, keepdims=True))
        dws.append(jnp.zeros((1, BW), F32))
        dw_ref[...] += jnp.concatenate(dws, axis=0)
        db_ref[...] += jnp.sum(d, axis=0, keepdims=True)
        dva_ref[...] = (dh * sig).astype(BF16)
        dvb_ref[...] = (dh * va * sig * (1.0 - sig)).astype(BF16)

        @pl.when(i == nt - 1)
        def _():
            acc = dw_ref[...]
            for q in range(NDEV):
                dw8_ref[q] = acc[:, csh * q:csh * (q + 1)]

    halo = lambda col: pl.BlockSpec((HALO, BW), lambda i: (jnp.maximum(i * hb - 1, 0), col))
    return _call(
        body, name="conv_bwd", grid=(nt,),
        in_specs=[pl.BlockSpec((tm, BW), lambda i: (i, 0)),
                  pl.BlockSpec((HALO, BW), lambda i: (jnp.minimum((i + 1) * hb, t_rows // HALO - 1), 0)),
                  pl.BlockSpec((tm, BW), lambda i: (i, 1)), pl.BlockSpec((tm, BW), lambda i: (i, 2)), halo(1), halo(2),
                  pl.BlockSpec((CONV_K, BW), lambda i: (0, 0))],
        out_specs=[pl.BlockSpec((tm, BW), lambda i: (i, 0)), pl.BlockSpec((tm, BW), lambda i: (i, 0)),
                   pl.BlockSpec((NDEV, 32, csh), lambda i: (0, 0, 0)), pl.BlockSpec((1, BW), lambda i: (0, 0))],
        out_shape=[jax.ShapeDtypeStruct((t_rows, BW), BF16), jax.ShapeDtypeStruct((t_rows, BW), BF16),
                   jax.ShapeDtypeStruct((NDEV, 32, csh), F32), jax.ShapeDtypeStruct((1, BW), F32)],
        scratch_shapes=[pltpu.VMEM((HALO + tm, BW), F32), pltpu.VMEM((tm + HALO, BW), F32), pltpu.VMEM((32, BW), F32)],
        compiler_params=_cparams(("arbitrary",)))(dcv, dcv, z, z, z, z, w)


def _pool_rows(i, tm, n_rows, first_row):
    grp = lax.broadcasted_iota(jnp.int32, (1, BW), 1) // (BW // 4)
    wlen = jnp.where(grp == 0, 2.0, jnp.where(grp == 1, 4.0, jnp.where(grp == 2, 8.0, 16.0)))
    t = (i * tm + first_row + lax.broadcasted_iota(jnp.int32, (n_rows, 1), 0)).astype(F32)
    return grp, 1.0 / jnp.minimum(t + 1.0, wlen)


def _pool_pick(grp, s2, s4, s8, s16):
    return jnp.where(grp == 0, s2, jnp.where(grp == 1, s4, jnp.where(grp == 2, s8, s16)))


def _pool_fwd(z, tm=256):
    t_rows = z.shape[0]
    hb = tm // PHALO

    def body(u_ref, h_ref, o_ref):
        i = pl.program_id(0)
        u = u_ref[...]
        win = jnp.concatenate([h_ref[...] * jnp.where(i > 0, 1.0, 0.0), u], axis=0)
        s2 = win + pltpu.roll(win, 1, 0)
        s4 = s2 + pltpu.roll(s2, 2, 0)
        s8 = s4 + pltpu.roll(s4, 4, 0)
        s16 = s8 + pltpu.roll(s8, 8, 0)
        grp, inv = _pool_rows(i, tm, tm, 0)
        o_ref[...] = _pool_pick(grp, s2, s4, s8, s16)[PHALO:, :] * inv - u

    return _call(
        body, name="pool_fwd", grid=(t_rows // tm,),
        in_specs=[pl.BlockSpec((tm, BW), lambda i: (i, 3)),
                  pl.BlockSpec((PHALO, BW), lambda i: (jnp.maximum(i * hb - 1, 0), 3))],
        out_specs=pl.BlockSpec((tm, BW), lambda i: (i, 0)),
        out_shape=jax.ShapeDtypeStruct((t_rows, BW), F32),
        compiler_params=_cparams(("parallel",)))(z, z)


def _pool_bwd(dp, tm=256):
    t_rows = dp.shape[0]
    nt = t_rows // tm
    hb = tm // PHALO
    ln = tm + PHALO

    def body(d_ref, dn_ref, o_ref):
        i = pl.program_id(0)
        d = d_ref[...]
        grp, inv = _pool_rows(i, tm, ln, 0)
        win = jnp.concatenate([d, dn_ref[...] * jnp.where(i < nt - 1, 1.0, 0.0)], axis=0) * inv
        s2 = win + pltpu.roll(win, ln - 1, 0)
        s4 = s2 + pltpu.roll(s2, ln - 2, 0)
        s8 = s4 + pltpu.roll(s4, ln - 4, 0)
        s16 = s8 + pltpu.roll(s8, ln - 8, 0)
        o_ref[...] = (_pool_pick(grp, s2, s4, s8, s16)[:tm, :] - d).astype(BF16)

    return _call(
        body, name="pool_bwd", grid=(nt,),
        in_specs=[pl.BlockSpec((tm, BW), lambda i: (i, 0)),
                  pl.BlockSpec((PHALO, BW), lambda i: (jnp.minimum((i + 1) * hb, t_rows // PHALO - 1), 0))],
        out_specs=pl.BlockSpec((tm, BW), lambda i: (i, 0)),
        out_shape=jax.ShapeDtypeStruct((t_rows, BW), BF16),
        compiler_params=_cparams(("parallel",)))(dp, dp)


_MERGE_W = ("wglu", "bglu", "wpa", "lng", "lnb", "wpb", "wgrp", "scale", "wpc", "bgate", "wout")
_MERGE_SMALL = ("bglu", "lng", "lnb", "scale", "bgate")
_MERGE_BLOCKED = ("wpa", "wpb", "wpc")


def _merge_load(name, ref):
    if name in _MERGE_BLOCKED:
        return jnp.concatenate([ref[q] for q in range(NDEV)], axis=1)
    return ref[...]


def _merge_math(x, yssm, cv, pbar, zg, w, taps):
    t_glu, t_ya, t_yb, t_p, t_yc = taps
    g = jax.nn.gelu(yssm)
    outa = g * jax.nn.sigmoid(_mm(g, w["wglu"]) + t_glu + w["bglu"])
    ya = _mm(outa, w["wpa"]) + t_ya
    mu = jnp.mean(cv, axis=-1, keepdims=True)
    var = jnp.mean(jnp.square(cv - mu), axis=-1, keepdims=True)
    hs = jax.nn.silu((cv - mu) * lax.rsqrt(var + EPS) * w["lng"] + w["lnb"])
    yb = _mm(hs, w["wpb"]) + t_yb
    gw = BW // 4
    pk = jnp.concatenate([_mm(pbar[:, gw * k:gw * (k + 1)], w["wgrp"][k]) for k in range(4)], axis=1) + t_p
    pc = pk * w["scale"]
    yc = _mm(pc, w["wpc"]) + t_yc
    gates = jax.nn.sigmoid(zg + w["bgate"])
    merged = gates[:, :D] * ya + gates[:, D:2 * D] * yb + gates[:, 2 * D:] * yc
    x1 = x + _mm(merged, w["wout"])
    acts = tuple(a.astype(BF16) for a in (g, outa, hs, pbar, pc, merged))
    return x1, acts


def _merge_specs(tm, p):
    rows = lambda width, col=0: pl.BlockSpec((tm, width), lambda i, c=col: (i, c))
    data = [rows(D), rows(BW), rows(BW), rows(BW), rows(D, 2), rows(D, 3), rows(D, 4)]
    wspecs = []
    for name in _MERGE_W:
        nd = p[name].ndim
        wspecs.append(pl.BlockSpec(p[name].shape, lambda i, nd=nd: (0,) * nd))
    return rows, data, wspecs


def _merge_fwd(x, yssm, cv, pbar, z, p, token, tm=512):
    t_rows = x.shape[0]
    rows, data, wspecs = _merge_specs(tm, p)

    def body(x_ref, y_ref, cv_ref, pb_ref, za_ref, zb_ref, zc_ref, *rest):
        w = {name: _merge_load(name, r) for name, r in zip(_MERGE_W, rest[:len(_MERGE_W)])}
        o_ref = rest[len(_MERGE_W) + 1]
        taps = (0.0, 0.0, 0.0, 0.0, 0.0)
        zg = jnp.concatenate([za_ref[...], zb_ref[...], zc_ref[...]], axis=1)
        o_ref[...] = _merge_math(x_ref[...], y_ref[...], cv_ref[...], pb_ref[...], zg, w, taps)[0]

    return _call(
        body, name="merge_fwd", grid=(t_rows // tm,),
        in_specs=data + wspecs + [pl.BlockSpec((8, 128), lambda i: (0, 0))], out_specs=rows(D),
        out_shape=jax.ShapeDtypeStruct((t_rows, D), F32),
        compiler_params=_cparams(("parallel",)))(x, yssm, cv, pbar, z, z, z, *[p[n] for n in _MERGE_W], token)


def _merge_bwd(dx1, x, yssm, cv, pbar, z, p, token, tm=256):
    t_rows = x.shape[0]
    rows, data, wspecs = _merge_specs(tm, p)
    nw = len(_MERGE_W)

    def body(dx_ref, x_ref, y_ref, cv_ref, pb_ref, za_ref, zb_ref, zc_ref, *rest):
        w = {name: _merge_load(name, r) for name, r in zip(_MERGE_W, rest[:nw])}
        zg = jnp.concatenate([za_ref[...], zb_ref[...], zc_ref[...]], axis=1)
        outs = rest[nw + 1:]
        small = {n: w[n] for n in _MERGE_SMALL}
        taps = (jnp.zeros((tm, BW), F32), jnp.zeros((tm, D), F32), jnp.zeros((tm, D), F32),
                jnp.zeros((tm, BW), F32), jnp.zeros((tm, D), F32))

        def f(yssm_, cv_, pbar_, zg_, small_, taps_):
            return _merge_math(x_ref[...], yssm_, cv_, pbar_, zg_, {**w, **small_}, taps_)

        _, vjp, acts = jax.vjp(f, y_ref[...], cv_ref[...], pb_ref[...], zg, small, taps, has_aux=True)
        dy, dcv, dpb, dzg, dsmall, dtaps = vjp(dx_ref[...])
        outs[0][...] = dy
        outs[1][...] = dcv
        outs[2][...] = dpb
        outs[3][:, 4 * BW:] = dzg.astype(BF16)
        for k in range(6):
            outs[4 + k][...] = acts[k]
        for k in range(5):
            outs[10 + k][...] = dtaps[k].astype(BF16)

        @pl.when(pl.program_id(0) == 0)
        def _():
            for k in range(5):
                outs[15 + k][...] = jnp.zeros_like(outs[15 + k])

        for k, n in enumerate(_MERGE_SMALL):
            outs[15 + k][...] += dsmall[n]

    f32o = lambda width: jax.ShapeDtypeStruct((t_rows, width), F32)
    bfo = lambda width: jax.ShapeDtypeStruct((t_rows, width), BF16)
    small_shapes = [jax.ShapeDtypeStruct(p[n].shape, F32) for n in _MERGE_SMALL]
    small_specs = [pl.BlockSpec(p[n].shape, lambda i: (0, 0)) for n in _MERGE_SMALL]
    out_shape = ([f32o(BW), f32o(BW), f32o(BW), bfo(IN_W)]
                 + [bfo(BW), bfo(BW), bfo(BW), bfo(BW), bfo(BW), bfo(D)]
                 + [bfo(BW), bfo(D), bfo(D), bfo(BW), bfo(D)] + small_shapes)
    out_specs = ([rows(BW), rows(BW), rows(BW), rows(IN_W)]
                 + [rows(BW)] * 5 + [rows(D)]
                 + [rows(BW), rows(D), rows(D), rows(BW), rows(D)] + small_specs)
    return _call(
        body, name="merge_bwd", grid=(t_rows // tm,),
        in_specs=[rows(D)] + data + wspecs + [pl.BlockSpec((8, 128), lambda i: (0, 0))], out_specs=out_specs,
        out_shape=out_shape, compiler_params=_cparams(("arbitrary",)))(
            dx1, x, yssm, cv, pbar, z, z, z, *[p[n] for n in _MERGE_W], token)


def _ffn_fwd(x1, gamma, wg, wu, wd, tm=1024, th=512):
    t_rows = x1.shape[0]
    tm = min(tm, t_rows)
    nh = HIDP // th

    def body(x_ref, g_ref, wg_ref, wu_ref, wd_ref, o_ref, gp_ref, up_ref, h_ref, acc_ref):
        j = pl.program_id(1)

        @pl.when(j == 0)
        def _():
            h_ref[...] = _rms(x_ref[...], g_ref[...]).astype(BF16)
            acc_ref[...] = jnp.zeros_like(acc_ref)

        gp = _dot_nt(h_ref[...], wg_ref[...])
        up = _dot_nt(h_ref[...], wu_ref[...])
        gp_ref[...] = gp.astype(BF16)
        up_ref[...] = up.astype(BF16)
        acc_ref[...] += _dot(jax.nn.silu(gp) * up, wd_ref[...])

        @pl.when(j == nh - 1)
        def _():
            o_ref[...] = x_ref[...] + acc_ref[...]

    return _call(
        body, name="ffn_fwd", grid=(t_rows // tm, nh),
        in_specs=[pl.BlockSpec((tm, D), lambda i, j: (i, 0)), pl.BlockSpec((1, D), lambda i, j: (0, 0)),
                  pl.BlockSpec((th, D), lambda i, j: (j, 0)), pl.BlockSpec((th, D), lambda i, j: (j, 0)),
                  pl.BlockSpec((th, D), lambda i, j: (j, 0))],
        out_specs=[pl.BlockSpec((tm, D), lambda i, j: (i, 0)), pl.BlockSpec((tm, th), lambda i, j: (i, j)),
                   pl.BlockSpec((tm, th), lambda i, j: (i, j)), pl.BlockSpec((tm, D), lambda i, j: (i, 0))],
        out_shape=[jax.ShapeDtypeStruct((t_rows, D), F32), jax.ShapeDtypeStruct((t_rows, HIDP), BF16),
                   jax.ShapeDtypeStruct((t_rows, HIDP), BF16), jax.ShapeDtypeStruct((t_rows, D), BF16)],
        scratch_shapes=[pltpu.VMEM((tm, D), F32)],
        compiler_params=_cparams(("parallel", "arbitrary")))(x1, gamma, wg, wu, wd)


def _rms_bwd_tail(x, gamma, dh):
    _, vjp = jax.vjp(_rms, x, gamma)
    return vjp(dh)


def _ffn_bwd(dx2, x1, gamma, gpre, upre, wg, wu, wd, token, tm=1024, th=512):
    t_rows = x1.shape[0]
    tm = min(tm, t_rows)
    nh = HIDP // th

    def body(d_ref, x_ref, g_ref, gp_ref, up_ref, wg_ref, wu_ref, wd_ref, token_ref,
             dx_ref, dgam_ref, dgp_ref, dup_ref, act_ref, acc_ref):
        i = pl.program_id(0)
        j = pl.program_id(1)

        @pl.when(j == 0)
        def _():
            acc_ref[...] = jnp.zeros_like(acc_ref)

        @pl.when((i == 0) & (j == 0))
        def _():
            dgam_ref[...] = jnp.zeros_like(dgam_ref)

        dact = _dot_nt(d_ref[...], wd_ref[...])
        gp = gp_ref[...].astype(F32)
        up = up_ref[...].astype(F32)
        sg = jax.nn.sigmoid(gp)
        silu = gp * sg
        dgp = (dact * up * (sg * (1.0 + gp * (1.0 - sg)))).astype(BF16)
        dup = (dact * silu).astype(BF16)
        dgp_ref[...] = dgp
        dup_ref[...] = dup
        act_ref[...] = (silu * up).astype(BF16)
        acc_ref[...] += _dot(dgp, wg_ref[...]) + _dot(dup, wu_ref[...])

        @pl.when(j == nh - 1)
        def _():
            x = x_ref[...]
            dx, dgam = _rms_bwd_tail(x, g_ref[...], acc_ref[...])
            dx_ref[...] = d_ref[...] + dx
            dgam_ref[...] += dgam

    row_d = pl.BlockSpec((tm, D), lambda i, j: (i, 0))
    row_h = pl.BlockSpec((tm, th), lambda i, j: (i, j))
    return _call(
        body, name="ffn_bwd", grid=(t_rows // tm, nh),
        in_specs=[row_d, row_d, pl.BlockSpec((1, D), lambda i, j: (0, 0)), row_h, row_h,
                  pl.BlockSpec((th, D), lambda i, j: (j, 0)), pl.BlockSpec((th, D), lambda i, j: (j, 0)),
                  pl.BlockSpec((th, D), lambda i, j: (j, 0)), TOKEN_SPEC],
        out_specs=[row_d, pl.BlockSpec((1, D), lambda i, j: (0, 0)), row_h, row_h, row_h],
        out_shape=[jax.ShapeDtypeStruct((t_rows, D), F32), jax.ShapeDtypeStruct((1, D), F32),
                   jax.ShapeDtypeStruct((t_rows, HIDP), BF16), jax.ShapeDtypeStruct((t_rows, HIDP), BF16),
                   jax.ShapeDtypeStruct((t_rows, HIDP), BF16)],
        scratch_shapes=[pltpu.VMEM((tm, D), F32)],
        compiler_params=_cparams(("arbitrary", "arbitrary")))(dx2, x1, gamma, gpre, upre, wg, wu, wd, token)


def _inproj_bwd(dz, dx1, x, gamma, w, token, tm=1024, nb=2):
    t_rows = x.shape[0]
    tm = min(tm, t_rows)
    oc = w.shape[2]
    tn = nb * oc
    nn = NDEV // nb

    def body(dz_ref, d1_ref, x_ref, g_ref, w_ref, token_ref, dx_ref, dgam_ref, acc_ref):
        i = pl.program_id(0)
        j = pl.program_id(1)

        @pl.when(j == 0)
        def _():
            acc_ref[...] = jnp.zeros_like(acc_ref)

        @pl.when((i == 0) & (j == 0))
        def _():
            dgam_ref[...] = jnp.zeros_like(dgam_ref)

        for q in range(nb):
            acc_ref[...] += _dot_nt(dz_ref[:, oc * q:oc * (q + 1)], w_ref[q])

        @pl.when(j == nn - 1)
        def _():
            x = x_ref[...]
            dx, dgam = _rms_bwd_tail(x, g_ref[...], acc_ref[...])
            dx_ref[...] = d1_ref[...] + dx
            dgam_ref[...] += dgam

    row_d = pl.BlockSpec((tm, D), lambda i, j: (i, 0))
    return _call(
        body, name="inproj_bwd", grid=(t_rows // tm, nn),
        in_specs=[pl.BlockSpec((tm, tn), lambda i, j: (i, j)), row_d, row_d, pl.BlockSpec((1, D), lambda i, j: (0, 0)),
                  pl.BlockSpec((nb, D, oc), lambda i, j: (j, 0, 0)), TOKEN_SPEC],
        out_specs=[row_d, pl.BlockSpec((1, D), lambda i, j: (0, 0))],
        out_shape=[jax.ShapeDtypeStruct((t_rows, D), F32), jax.ShapeDtypeStruct((1, D), F32)],
        scratch_shapes=[pltpu.VMEM((tm, D), F32)],
        compiler_params=_cparams(("arbitrary", "arbitrary")))(dz, dx1, x, gamma, w, token)


def _matmul_tn(a, b, name, owner_cols=None, tt=2048):
    t_rows, k = a.shape
    n = b.shape[1]
    tk = min(k, 1024)
    tt = min(tt, t_rows)
    nt = t_rows // tt
    if owner_cols is None:
        tn, nb = min(n, 1024), None
        out_spec = pl.BlockSpec((tk, tn), lambda i, j, t: (i, j))
        out_shape = jax.ShapeDtypeStruct((k, n), BF16)
    else:
        nb = min(n // owner_cols, max(1, 1280 // owner_cols))
        tn = nb * owner_cols
        out_spec = pl.BlockSpec((nb, tk, owner_cols), lambda i, j, t: (j, i, 0))
        out_shape = jax.ShapeDtypeStruct((n // owner_cols, k, owner_cols), BF16)

    def body(a_ref, b_ref, o_ref, acc_ref):
        t = pl.program_id(2)

        @pl.when(t == 0)
        def _():
            acc_ref[...] = jnp.zeros_like(acc_ref)

        acc_ref[...] += _dot_tn(a_ref[...], b_ref[...])

        @pl.when(t == nt - 1)
        def _():
            if nb is None:
                o_ref[...] = acc_ref[...].astype(BF16)
            else:
                for q in range(nb):
                    o_ref[q] = acc_ref[:, owner_cols * q:owner_cols * (q + 1)].astype(BF16)

    return _call(
        body, name=name, grid=(k // tk, n // tn, nt),
        in_specs=[pl.BlockSpec((tt, tk), lambda i, j, t: (t, i)), pl.BlockSpec((tt, tn), lambda i, j, t: (t, j))],
        out_specs=out_spec, out_shape=out_shape,
        scratch_shapes=[pltpu.VMEM((tk, tn), F32)],
        compiler_params=_cparams(("parallel", "parallel", "arbitrary")))(a, b)


def _group_tn(a, b):
    t_rows = a.shape[0]
    gw = BW // 4

    def body(a_ref, b_ref, o_ref):
        o_ref[...] = _dot_tn(a_ref[...], b_ref[...])

    return _call(
        body, name="pool_group_tn", grid=(4,),
        in_specs=[pl.BlockSpec((t_rows, gw), lambda k: (0, k)), pl.BlockSpec((t_rows, gw), lambda k: (0, k))],
        out_specs=pl.BlockSpec((None, gw, gw), lambda k: (k, 0, 0)),
        out_shape=jax.ShapeDtypeStruct((4, gw, gw), F32),
        compiler_params=_cparams(("parallel",)))(a, b)


def _loss_head(x2, gamma, target, tm=512):
    t_rows = x2.shape[0]

    def body(x_ref, g_ref, t_ref, loss_ref, dx_ref, dgam_ref):
        @pl.when(pl.program_id(0) == 0)
        def _():
            loss_ref[...] = jnp.zeros_like(loss_ref)
            dgam_ref[...] = jnp.zeros_like(dgam_ref)

        def f(x, g):
            err = jnp.square(_rms(x, g) - t_ref[...])
            return 0.5 * jnp.sum(jnp.mean(err, axis=-1, keepdims=True), axis=0, keepdims=True)

        loss, vjp = jax.vjp(f, x_ref[...], g_ref[...])
        dx, dgam = vjp(jnp.ones((1, 1), F32))
        loss_ref[...] += jnp.broadcast_to(loss, (1, 128))
        dx_ref[...] = dx
        dgam_ref[...] += dgam

    row_d = pl.BlockSpec((tm, D), lambda i: (i, 0))
    return _call(
        body, name="loss_head", grid=(t_rows // tm,),
        in_specs=[row_d, pl.BlockSpec((1, D), lambda i: (0, 0)), row_d],
        out_specs=[pl.BlockSpec((1, 128), lambda i: (0, 0)), row_d, pl.BlockSpec((1, D), lambda i: (0, 0))],
        out_shape=[jax.ShapeDtypeStruct((1, 128), F32), jax.ShapeDtypeStruct((t_rows, D), F32),
                   jax.ShapeDtypeStruct((1, D), F32)],
        compiler_params=_cparams(("arbitrary",)))(x2, gamma, target)


NCHIP = NDEV // 2


def _coords():
    return lax.axis_index("x"), lax.axis_index("y"), lax.axis_index("c")


def _remote(src, dst, send_sem, recv_sem, peer):
    return pltpu.make_async_remote_copy(src_ref=src, dst_ref=dst, send_sem=send_sem, recv_sem=recv_sem,
                                        device_id=peer, device_id_type=MESH)


def _comm_call(name, srcs, out_shapes, n_rec, plan, aliases=None):
    ns, no = len(srcs), len(out_shapes)

    def body(*refs):
        ins, outs = refs[:ns], refs[ns:ns + no]
        loc_sem, send_sem, recv_sem = refs[ns + no:]
        x, y, c = _coords()
        recs = plan(ins, outs, x, y, c)
        assert len(recs) == n_rec
        for k, r in enumerate(recs):
            for src, dst in r.get("local", ()):
                pltpu.make_async_copy(src, dst, loc_sem.at[k]).start()
            for peer, src, dst in r.get("remote", ()):
                _remote(src, dst, send_sem.at[k], recv_sem.at[k], peer).start()
        for k, r in enumerate(recs):
            if r.get("recv_wait") is not None:
                w = r["recv_wait"]
                _remote(w, w, send_sem.at[k], recv_sem.at[k], (x, y, c)).wait_recv()
            if r.get("send_wait") is not None:
                w = r["send_wait"]
                _remote(w, w, send_sem.at[k], recv_sem.at[k], (x, y, c)).wait_send()
            if r.get("local_wait") is not None:
                w = r["local_wait"]
                pltpu.make_async_copy(w, w, loc_sem.at[k]).wait()

    return _call(
        body, name=name, in_specs=[ANY] * ns, out_specs=[ANY] * no, out_shape=out_shapes,
        input_output_aliases=aliases or {}, scratch_shapes=[pltpu.SemaphoreType.DMA((n_rec,))] * 3)(*srcs)


def _gather_call(srcs, out_shapes, items, aliases):
    ns, no, n = len(srcs), len(out_shapes), len(items)

    def body(*refs):
        ins, outs = refs[:ns], refs[ns:ns + no]
        loc, sib_s, sib_r, ici_s, ici_r, fwd_s, fwd_r = refs[ns + no:]
        x, y, c = _coords()
        me, sib = (x, y, c), (x, y, 1 - c)
        chips = [(1 - x, y), (x, 1 - y), (1 - x, 1 - y)]
        index = lambda px, py, pc: 4 * px + 2 * py + pc
        for k, (si, oi, shard, block, _) in enumerate(items):
            mine = block(outs[oi], index(*me))
            src = mine if shard is None else shard(ins[si])
            if shard is not None:
                pltpu.make_async_copy(src, mine, loc.at[k]).start()
            _remote(src, mine, sib_s.at[k], sib_r.at[k], sib).start()
            for chip in chips:
                _remote(src, mine, ici_s.at[k], ici_r.at[k], (*chip, c)).start()
        for k, (si, oi, _, block, blocks) in enumerate(items):
            three = blocks(outs[oi], 3)
            _remote(three, three, ici_s.at[k], ici_r.at[k], me).wait_recv()
            for chip in chips:
                landed = block(outs[oi], index(*chip, c))
                _remote(landed, landed, fwd_s.at[k], fwd_r.at[k], sib).start()
        for k, (si, oi, shard, _, blocks) in enumerate(items):
            one, three = blocks(outs[oi], 1), blocks(outs[oi], 3)
            _remote(one, one, sib_s.at[k], sib_r.at[k], me).wait()
            _remote(three, three, fwd_s.at[k], fwd_r.at[k], me).wait()
            _remote(three, three, ici_s.at[k], ici_r.at[k], me).wait_send()
            if shard is not None:
                pltpu.make_async_copy(one, one, loc.at[k]).wait()

    return _call(
        body, name="gather_weights", in_specs=[ANY] * ns, out_specs=[ANY] * no, out_shape=out_shapes,
        input_output_aliases=aliases, scratch_shapes=[pltpu.SemaphoreType.DMA((n,))] * 7)(*srcs)


_BIG = {
    "w_in": (True, D, IN_W // NDEV),
    "ssm_w_glu": (False, BW // NDEV, BW),
    "ssm_w_proj": (True, BW, D // NDEV),
    "conv_w_proj": (True, BW, D // NDEV),
    "pool_w_proj": (True, BW, D // NDEV),
    "w_out": (False, D // NDEV, D),
    "ffn_w_gate": (False, HPAD, D),
    "ffn_w_up": (False, HPAD, D),
    "ffn_w_down": (False, HPAD, D),
}
GROUP_IN = ("w_in",)
GROUP_MIX = ("ssm_w_glu", "ssm_w_proj", "conv_w_proj", "pool_w_proj", "w_out")
GROUP_FFN = ("ffn_w_gate", "ffn_w_up", "ffn_w_down")


def _gathered_shape(name):
    blocked, kk, nn = _BIG[name]
    return jax.ShapeDtypeStruct((NDEV, kk, nn) if blocked else (NDEV * kk, nn), BF16)


def _block_view(name):
    blocked, kk, _ = _BIG[name]
    if blocked:
        return lambda ref, q: ref.at[q]
    return lambda ref, q: ref.at[pl.ds(pl.multiple_of(q * kk, 16), kk), :]


def _blocks_view(name):
    blocked, kk, _ = _BIG[name]
    if blocked:
        return lambda ref, n: ref.at[pl.ds(0, n)]
    return lambda ref, n: ref.at[pl.ds(0, n * kk), :]


def _place_shards(weights, names, layer, me):
    cnt = len(names)

    def body(me_ref, *refs):
        w_refs, outs, stages, sem = refs[:cnt], refs[cnt:2 * cnt], refs[2 * cnt:3 * cnt], refs[3 * cnt]
        q = me_ref[0]
        copies = []
        for k, name in enumerate(names):
            _, kk, nn = _BIG[name]
            rows = w_refs[k].shape[0]
            stages[k][0:rows, :] = w_refs[k][...].astype(BF16)
            if rows < kk:
                stages[k][rows:kk, :] = jnp.zeros((kk - rows, nn), BF16)
            copies.append(pltpu.make_async_copy(stages[k], _block_view(name)(outs[k], q), sem.at[k]))
            copies[-1].start()
        for cp in copies:
            cp.wait()

    in_specs = [pl.BlockSpec(memory_space=pltpu.SMEM)]
    in_specs += [pl.BlockSpec((None,) + weights[n].shape[1:], lambda i: (layer, 0, 0)) for n in names]
    res = _call(
        body, name="place_shards", grid=(1,), in_specs=in_specs, out_specs=[ANY] * cnt,
        out_shape=[_gathered_shape(n) for n in names],
        scratch_shapes=[pltpu.VMEM(_BIG[n][1:], BF16) for n in names] + [pltpu.SemaphoreType.DMA((cnt,))],
        compiler_params=_cparams(("arbitrary",)))(me, *[weights[n] for n in names])
    return dict(zip(names, res))


def _gather_weights(placed, conv_dw, names):
    cnt = len(names)
    srcs = [placed[n] for n in names] + [conv_dw]
    outs = [_gathered_shape(n) for n in names] + [jax.ShapeDtypeStruct((NDEV,) + conv_dw.shape, conv_dw.dtype)]
    items = [(k, k, None, _block_view(n), _blocks_view(n)) for k, n in enumerate(names)]
    items.append((cnt, cnt, lambda ref: ref, lambda ref, q: ref.at[q], lambda ref, n: ref.at[pl.ds(0, n)]))
    res = _gather_call(srcs, outs, items, {k: k for k in range(cnt)})
    return dict(zip(names, res[:-1])), res[-1]


def _gather_start(placed, names, after, tag):
    def copies(src_refs, land_refs, x, y, c):
        me = 4 * x + 2 * y + c
        peers = [(x, y, 1 - c), (1 - x, y, c), (x, 1 - y, c), (1 - x, 1 - y, c)]
        out = []
        for k, n in enumerate(names):
            mine = _block_view(n)(land_refs[k], me)
            out.append([(peer, mine, mine) for peer in peers])
        return out

    return _split_start("gather_start_" + tag, [], [placed[n] for n in names], copies, after)


def _gather_finish(handle, names, after, tag):
    four = [functools.partial(lambda ref, bv: bv(ref, 4), bv=_blocks_view(n)) for n in names]
    lands = _split_wait("gather_wait_" + tag, handle, four, after)
    cnt = len(names)

    def plan(ins, out_refs, x, y, c):
        sib = (x, y, 1 - c)
        recs = []
        for k, n in enumerate(names):
            three = _blocks_view(n)(out_refs[k], 3)
            remote = []
            for px, py in [(1 - x, y), (x, 1 - y), (1 - x, 1 - y)]:
                landed = _block_view(n)(out_refs[k], 4 * px + 2 * py + c)
                remote.append((sib, landed, landed))
            recs.append(dict(remote=remote, send_wait=three, recv_wait=three))
        return recs

    res = _comm_call("gather_pair_" + tag, lands, [jax.ShapeDtypeStruct(l.shape, l.dtype) for l in lands], cnt, plan,
                     aliases={k: k for k in range(cnt)})
    return dict(zip(names, res))


def _pair_add(name, grads, rcv, core):
    nl = len(grads)
    _, kk, nn = grads[0].shape

    def body(c_ref, *refs):
        l = pl.program_id(0)
        own = refs[0][...]
        for j in range(1, nl):
            own = jnp.where(l == j, refs[j][...], own)
        refs[nl + 1][...] = (own.astype(F32) + refs[nl][...].astype(F32)).astype(BF16)

    gspec = lambda j: pl.BlockSpec((None, kk, nn), lambda l, h, c_ref: (jnp.where(l == j, 2 * h + c_ref[0], 0), 0, 0))
    rspec = pl.BlockSpec((None, None, kk, nn), lambda l, h, c_ref: (h, l, 0, 0))
    return _call(
        body, name="pair_add_" + name,
        grid_spec=pltpu.PrefetchScalarGridSpec(num_scalar_prefetch=1, grid=(nl, NCHIP),
                                               in_specs=[gspec(j) for j in range(nl)] + [rspec], out_specs=rspec),
        out_shape=jax.ShapeDtypeStruct(rcv.shape, BF16),
        compiler_params=_cparams(("arbitrary", "arbitrary")))(core, *grads, rcv)


def _pair_add_small(owned, lists, core):
    on, ln = list(owned), list(lists)
    flat = []
    for n in on:
        flat += list(owned[n][0]) + [owned[n][1]]
    for n in ln:
        flat += list(lists[n][0]) + [lists[n][1]]

    def body(c_ref, *refs):
        outs = refs[len(flat):]
        c = c_ref[0]
        pos = 0
        for k, n in enumerate(on):
            nl = len(owned[n][0])
            for h in range(NCHIP):
                for l in range(nl):
                    outs[k][h, l] = refs[pos + l][pl.ds(2 * h + c, 1)][0] + refs[pos + nl][h, l]
            pos += nl + 1
        for k, n in enumerate(ln):
            nl = len(lists[n][0])
            for l in range(nl):
                out = outs[len(on) + k]
                out[l] = (refs[pos + l][...] + refs[pos + nl][l]).astype(out.dtype)
            pos += nl + 1

    shapes = [jax.ShapeDtypeStruct(owned[n][1].shape, F32) for n in on]
    shapes += [jax.ShapeDtypeStruct(lists[n][1].shape, F32 if n == "final_norm" else BF16) for n in ln]
    res = _call(body, name="pair_add_small", out_shape=shapes,
                in_specs=[pl.BlockSpec(memory_space=pltpu.SMEM)] + [pl.BlockSpec(memory_space=pltpu.VMEM)] * len(flat),
                compiler_params=_cparams())(core, *flat)
    return dict(zip(on + ln, res))


def _pair_reduce(tag, big, by_owner, small, core):
    rs = {**big, **by_owner}
    srcs, outs, plans, rcv_at = [], [], [], {}
    for name, arrays in rs.items():
        rcv_at[name] = len(outs)
        outs.append(jax.ShapeDtypeStruct((NCHIP, len(arrays)) + arrays[0].shape[1:], arrays[0].dtype))
        for l, arr in enumerate(arrays):
            srcs.append(arr)
            plans.append((len(srcs) - 1, rcv_at[name], l, True))
    for name, arrays in small.items():
        rcv_at[name] = len(outs)
        outs.append(jax.ShapeDtypeStruct((len(arrays),) + arrays[0].shape, F32))
        for l, arr in enumerate(arrays):
            srcs.append(arr)
            plans.append((len(srcs) - 1, rcv_at[name], l, False))

    def plan_pair(ins, out_refs, x, y, c):
        sib = (x, y, 1 - c)
        recs = []
        for si, ro, l, slabs in plans:
            if slabs:
                four = out_refs[ro].at[pl.ds(0, NCHIP), l]
                recs.append(dict(remote=[(sib, ins[si].at[2 * h + 1 - c], out_refs[ro].at[h, l]) for h in range(NCHIP)],
                                 send_wait=four, recv_wait=four))
            else:
                dst = out_refs[ro].at[l]
                recs.append(dict(remote=[(sib, ins[si], dst)], send_wait=dst, recv_wait=dst))
        return recs

    res = _comm_call("pair_exchange_" + tag, srcs, outs, len(plans), plan_pair)
    part = {name: _pair_add(name, big[name], res[rcv_at[name]], core) for name in big}
    if by_owner or small:
        part.update(_pair_add_small({n: (by_owner[n], res[rcv_at[n]]) for n in by_owner},
                                    {n: (small[n], res[rcv_at[n]]) for n in small}, core))
    return part


def _pair_start(tag, big, after):
    names = list(big)
    srcs = [big[n][0] for n in names]
    lands = [jax.ShapeDtypeStruct((NCHIP, 1) + g.shape[1:], g.dtype) for g in srcs]

    def copies(src_refs, land_refs, x, y, c):
        sib = (x, y, 1 - c)
        return [[(sib, src_refs[k].at[2 * h + 1 - c], land_refs[k].at[h, 0]) for h in range(NCHIP)]
                for k in range(len(names))]

    return names, _split_start("pair_start_" + tag, srcs, lands, copies, after)


def _pair_finish(tag, names, handle, big, core, after):
    landed = _split_wait("pair_wait_" + tag, handle, [lambda ref: ref] * len(names), after)
    return {n: _pair_add(n, big[n], rcv, core) for n, rcv in zip(names, landed)}


def _chip_copies(src, land, slabbed, x, y, c):
    mine = 2 * x + y
    copies = []
    for step in range(1, NCHIP):
        h = (mine + step) % NCHIP
        copies.append(((h // 2, h % 2, c), src.at[h] if slabbed else src, land.at[mine]))
    return copies


def _chip_exchange(part, slabbed, keep_own):
    names = list(part)
    outs = [jax.ShapeDtypeStruct((() if n in slabbed else (NCHIP,)) + part[n].shape, part[n].dtype) for n in names]

    def plan(ins, out_refs, x, y, c):
        mine = 2 * x + y
        recs = []
        for k, n in enumerate(names):
            three = out_refs[k].at[pl.ds(0, NCHIP - 1)]
            rec = dict(remote=_chip_copies(ins[k], out_refs[k], n in slabbed, x, y, c), send_wait=three, recv_wait=three)
            if n in keep_own:
                rec["local"] = [(ins[k].at[mine] if n in slabbed else ins[k], out_refs[k].at[mine])]
                rec["local_wait"] = out_refs[k].at[0]
            recs.append(rec)
        return recs

    res = _comm_call("chip_exchange", [part[n] for n in names], outs, len(names), plan)
    return dict(zip(names, res))


HBM_SPEC = pl.BlockSpec(memory_space=pltpu.HBM)
SEM_SPEC = pl.BlockSpec(memory_space=pltpu.SEMAPHORE)
SPLIT_EFFECT = pltpu.SideEffectType.DATAFLOW_SIDE_EFFECTING


def _split_start(name, srcs, land_shapes, copies_fn, after):
    ns, n = len(srcs), len(land_shapes)
    lands = [pltpu.with_memory_space_constraint(s if isinstance(s, jax.Array) else lax.empty(s.shape, s.dtype), pltpu.HBM)
             for s in land_shapes]

    def body(*refs):
        src_refs, land_refs = refs[:ns], refs[ns:ns + n]
        send_sem, recv_sem = refs[ns + n + 1], refs[ns + n + 2]
        token = refs[-1]
        x, y, c = _coords()
        for k, copies in enumerate(copies_fn(src_refs, land_refs, x, y, c)):
            for peer, src, dst in copies:
                _remote(src, dst, send_sem.at[k], recv_sem.at[k], peer).start()
        token[...] = jnp.zeros_like(token)

    res = pl.pallas_call(
        body, name=name,
        out_shape=(pltpu.SemaphoreType.DMA((n,)), pltpu.SemaphoreType.DMA((n,)),
                   *[pltpu.HBM(s.shape, s.dtype) for s in land_shapes], jax.ShapeDtypeStruct((8, 128), F32)),
        in_specs=[HBM_SPEC] * (ns + n) + [ANY],
        out_specs=(SEM_SPEC, SEM_SPEC, *[HBM_SPEC] * n, pl.BlockSpec(memory_space=pltpu.VMEM)),
        input_output_aliases={ns + i: 2 + i for i in range(n)},
        compiler_params=pltpu.CompilerParams(has_side_effects=SPLIT_EFFECT),
    )(*[pltpu.with_memory_space_constraint(s, pltpu.HBM) for s in srcs], *lands, after)
    return dict(send=res[0], recv=res[1], srcs=list(srcs), lands=list(res[2:2 + n]), token=res[-1])


def _split_wait(name, handle, wait_views, after):
    n = len(handle["lands"])
    after = after if isinstance(after, (tuple, list)) else (after,)

    def body(*refs):
        land_refs = refs[:n]
        send_sem, recv_sem = refs[n], refs[n + 1]
        x, y, c = _coords()
        for k in range(n):
            w = wait_views[k](land_refs[k])
            cp = _remote(w, w, send_sem.at[k], recv_sem.at[k], (x, y, c))
            cp.wait_send()
            cp.wait_recv()

    res = pl.pallas_call(
        body, name=name,
        out_shape=tuple(pltpu.HBM(s.shape, s.dtype) for s in handle["lands"]),
        in_specs=[HBM_SPEC] * n + [SEM_SPEC, SEM_SPEC] + [ANY] * len(after), out_specs=tuple([HBM_SPEC] * n),
        input_output_aliases={i: i for i in range(n)},
        compiler_params=pltpu.CompilerParams(has_side_effects=SPLIT_EFFECT),
    )(*handle["lands"], handle["send"], handle["recv"], *after)
    return list(res)


def _adamw(w, g, m, v):
    m = ADAM_B1 * m + (1.0 - ADAM_B1) * g
    v = ADAM_B2 * v + (1.0 - ADAM_B2) * jnp.square(g)
    m_hat = m / (1.0 - ADAM_B1 ** ADAM_STEP)
    v_hat = v / (1.0 - ADAM_B2 ** ADAM_STEP)
    delta = -ADAM_LR * (m_hat / (jnp.sqrt(v_hat) + ADAM_EPS) + ADAM_WD * w)
    return delta, m, v


def _chip_start(part, tag, after=None):
    names = list(part)
    after = part[names[0]] if after is None else after

    def copies(src_refs, land_refs, x, y, c):
        return [_chip_copies(src_refs[k], land_refs[k], True, x, y, c) for k in range(len(names))]

    shapes = [jax.ShapeDtypeStruct(part[n].shape, part[n].dtype) for n in names]
    return names, _split_start("chip_start_" + tag, [part[n] for n in names], shapes, copies, after)


def _chip_wait(names, handle, after, tag):
    three = [lambda ref: ref.at[pl.ds(0, NCHIP - 1)]] * len(names)
    return dict(zip(names, _split_wait("chip_wait_" + tag, handle, three, after)))


def _adam_big(name, own, recv, w, m, v, tk, chip):
    nl = len(own)
    kk, nn = w.shape[1], w.shape[2]
    nnp = own[0].shape[3]

    def body(chip_ref, *refs):
        l = pl.program_id(0)
        g = None
        for step in range(NCHIP):
            val = refs[step][...]
            for q in range(1, nl):
                val = jnp.where(l == q, refs[NCHIP * q + step][...], val)
            g = val.astype(F32) if g is None else g + val.astype(F32)
        w_ref, m_ref, v_ref, g_ref, d_ref, mo_ref, vo_ref = refs[NCHIP * nl:]
        g = g[:, :nn]
        delta, m2, v2 = _adamw(w_ref[...], g, m_ref[...], v_ref[...])
        g_ref[...] = g
        d_ref[...] = delta
        mo_ref[...] = m2
        vo_ref[...] = v2

    def slab(q, step):
        return pl.BlockSpec((None, None, tk, nnp), lambda l, i, chip_ref: (
            jnp.where(l == q, (chip_ref[0] + step) % NCHIP, 0), 0, jnp.where(l == q, i, 0), 0))

    in_specs, operands = [], []
    for q in range(nl):
        in_specs += [slab(q, step) for step in range(NCHIP)]
        operands += [own[q]] + [recv[q]] * (NCHIP - 1)
    wspec = pl.BlockSpec((None, tk, nn), lambda l, i, chip_ref: (l, i, 0))
    shape = jax.ShapeDtypeStruct(w.shape, F32)
    return _call(
        body, name="adamw_" + name,
        grid_spec=pltpu.PrefetchScalarGridSpec(num_scalar_prefetch=1, grid=(nl, kk // tk),
                                               in_specs=in_specs + [wspec] * 3, out_specs=[wspec] * 4),
        out_shape=[shape] * 4, compiler_params=_cparams(("arbitrary", "arbitrary")))(chip, *operands, w, m, v)


def _adam_small(names, recv, w, m, v):
    n = len(names)

    def body(*refs):
        r, ww, mm, vv = refs[:n], refs[n:2 * n], refs[2 * n:3 * n], refs[3 * n:4 * n]
        outs = refs[4 * n:]
        for k in range(n):
            for l in range(r[k].shape[1]):
                g = r[k][0, l].astype(F32)
                for h in range(1, NCHIP):
                    g = g + r[k][h, l].astype(F32)
                per_layer = ww[k].shape[1:]
                if g.shape[0] > per_layer[0] and g.shape[1:] == per_layer[1:]:
                    g = g[:per_layer[0]]
                at = l if g.shape == per_layer else pl.ds(l, 1)
                delta, m2, v2 = _adamw(ww[k][at], g, mm[k][at], vv[k][at])
                outs[k][at] = g
                outs[n + k][at] = delta
                outs[2 * n + k][at] = m2
                outs[3 * n + k][at] = v2

    shapes = [jax.ShapeDtypeStruct(w[k].shape, F32) for k in names]
    res = _call(body, name="adamw_small", out_shape=shapes * 4, compiler_params=_cparams())(
        *[recv[k] for k in names], *[w[k] for k in names], *[m[k] for k in names], *[v[k] for k in names])
    return {k: (res[i], res[n + i], res[2 * n + i], res[3 * n + i]) for i, k in enumerate(names)}


def _expand_b(bt):
    eye = jnp.eye(GB, dtype=bt.dtype)
    return jnp.einsum("jgpn,gh->jgphn", bt.reshape(NBLK, GB, SGRP, NSTATE), eye).reshape(NBLK, GB * SGRP, NS)


def _extract_b(db):
    x = db.reshape(NBLK, GB, SGRP, GB, NSTATE)
    eye = jnp.eye(GB, dtype=db.dtype)
    return jnp.einsum("jgphn,gh->jgpn", x, eye).reshape(NGRP, SGRP, NSTATE)


def _expand_c(c):
    ct = jnp.transpose(c, (0, 2, 1)).reshape(NBLK, GB, NSTATE, SGRP)
    eye = jnp.eye(GB, dtype=c.dtype)
    return jnp.einsum("jgnp,gh->jgnhp", ct, eye).reshape(NBLK, NS, GB * SGRP)


def _extract_c(dc):
    x = dc.reshape(NBLK, GB, NSTATE, GB, SGRP)
    eye = jnp.eye(GB, dtype=dc.dtype)
    d = jnp.einsum("jgnhp,gh->jgnp", x, eye).reshape(NGRP, NSTATE, SGRP)
    return jnp.transpose(d, (0, 2, 1))


_SMALL = ("norm1", "b_gate", "ssm_a_re", "ssm_a_im", "ssm_log_dt", "ssm_b_re", "ssm_b_im", "ssm_c_re", "ssm_c_im",
          "ssm_d", "ssm_b_glu", "conv_b_dw", "conv_ln_g", "conv_ln_b", "pool_w_group", "pool_scale", "norm2")
_ADAM_TK = {"w_in": 256, "ssm_w_glu": 64, "ssm_w_proj": 512, "conv_w_proj": 512, "pool_w_proj": 512, "w_out": 128,
            "ffn_w_gate": HSH, "ffn_w_up": HSH, "ffn_w_down": HSH}
_OUT_ORDER = ("norm1", "w_in", "b_gate", "ssm_a_re", "ssm_a_im", "ssm_log_dt", "ssm_b_re", "ssm_b_im", "ssm_c_re",
              "ssm_c_im", "ssm_d", "ssm_w_glu", "ssm_b_glu", "ssm_w_proj", "conv_w_dw", "conv_b_dw", "conv_ln_g",
              "conv_ln_b", "conv_w_proj", "pool_w_group", "pool_scale", "pool_w_proj", "w_out", "norm2", "ffn_w_gate",
              "ffn_w_up", "ffn_w_down", "final_norm")


def _layer_fwd(x, p, token, late_params=None):
    z, h = _inproj_fwd(x, p["norm1"], p["w_in"], token)
    yssm, hre, him = _ssm_fwd(z, p)
    cv = _conv_fwd(z, p["conv_w"], p["conv_b"])
    pbar = _pool_fwd(z)
    if late_params is not None:
        more, token = late_params((yssm, cv, pbar))
        p = {**p, **more}
    x1 = _merge_fwd(x, yssm, cv, pbar, z, p, token)
    x2, gpre, upre, h2 = _ffn_fwd(x1, p["norm2"], p["wg"], p["wu"], p["wd"])
    return x2, dict(x=x, h=h, z=z, yssm=yssm, hre=hre, him=him, cv=cv, pbar=pbar, x1=x1, gpre=gpre, upre=upre, h2=h2), p


def _layer_bwd(dx, p, s, token, leave=None):
    big, small = {}, {}
    go = (lambda tag, names: leave(tag, {n: big[n] for n in names})) if leave else (lambda tag, names: token)
    h2 = s["h2"]
    dx1, d_norm2, dgp, dup, act = _ffn_bwd(dx, s["x1"], p["norm2"], s["gpre"], s["upre"], p["wg"], p["wu"], p["wd"],
                                               token)
    big["ffn_w_gate"] = _matmul_tn(dgp, h2, "tn_gate").reshape(NDEV, HPAD, D)
    big["ffn_w_up"] = _matmul_tn(dup, h2, "tn_up").reshape(NDEV, HPAD, D)
    big["ffn_w_down"] = _matmul_tn(act, dx, "tn_down").reshape(NDEV, HPAD, D)
    (dy, dcv, dpb, dzg, a_g, a_outa, a_hs, a_pb, a_pc, a_mg, c_glu, c_ya, c_yb, c_p, c_yc,
     d_bglu, d_lng, d_lnb, d_scale, d_bgate) = _merge_bwd(dx1, s["x"], s["yssm"], s["cv"], s["pbar"], s["z"], p,
                                                          go("ffn", GROUP_FFN))
    big["ssm_w_glu"] = _matmul_tn(a_g, c_glu, "tn_glu").reshape(NDEV, BW // NDEV, BW)
    big["ssm_w_proj"] = _matmul_tn(a_outa, c_ya, "tn_ssm_proj", D // NDEV)
    big["conv_w_proj"] = _matmul_tn(a_hs, c_yb, "tn_conv_proj", D // NDEV)
    big["pool_w_proj"] = _matmul_tn(a_pc, c_yc, "tn_pool_proj", D // NDEV)
    big["w_out"] = _matmul_tn(a_mg, dx1, "tn_out").reshape(NDEV, D // NDEV, D)
    d_wgrp = _group_tn(a_pb, c_p)
    du_a, dbr, dbi, dcr, dci, dd, dar, dai, dldt = _ssm_bwd(dy, s["z"], s["hre"], s["him"], p, go("mix", GROUP_MIX))
    dva, dvb, dw8, dcb = _conv_bwd(dcv, s["z"], p["conv_w"])
    du_c = _pool_bwd(dpb)
    dz = dzg
    for k, piece in enumerate((du_a, dva, dvb, du_c)):
        dz = lax.dynamic_update_slice(dz, piece, (0, k * BW))
    big["w_in"] = _matmul_tn(s["h"], dz, "tn_in", IN_W // NDEV)
    dx0, d_norm1 = _inproj_bwd(dz, dx1, s["x"], p["norm1"], p["w_in"], go("in", GROUP_IN))
    small["norm1"] = d_norm1
    small["b_gate"] = d_bgate
    small["ssm_a_re"] = dar.reshape(NGRP, NSTATE)
    small["ssm_a_im"] = dai.reshape(NGRP, NSTATE)
    small["ssm_log_dt"] = dldt.reshape(NBLK, 8, 128)[:, 0, :GB].reshape(1, NGRP)
    small["ssm_b_re"] = _extract_b(dbr)
    small["ssm_b_im"] = _extract_b(dbi)
    small["ssm_c_re"] = _extract_c(dcr)
    small["ssm_c_im"] = _extract_c(dci)
    small["ssm_d"] = dd.reshape(NGRP, SGRP)
    small["ssm_b_glu"] = d_bglu
    small["conv_b_dw"] = dcb
    small["conv_ln_g"] = d_lng
    small["conv_ln_b"] = d_lnb
    small["pool_w_group"] = d_wgrp
    small["pool_scale"] = d_scale
    small["norm2"] = d_norm2
    return dx0, big, dw8, small


def _train_step(a):
    t_rows = a["x"].shape[1]
    x0 = a["x"].reshape(t_rows, D)
    target = a["loss_target"].reshape(t_rows, D)

    tr = lambda w: jnp.transpose(w, (0, 2, 1))
    weights = {name: (tr(a[name]) if name in ("ffn_w_gate", "ffn_w_up") else a[name]) for name in _BIG}
    core = lax.axis_index("c").astype(jnp.int32).reshape(1)
    chip = (2 * lax.axis_index("x") + lax.axis_index("y")).astype(jnp.int32).reshape(1)
    me = 2 * chip + core
    no_token = jnp.zeros((8, 128), F32)
    row = lambda v: v.reshape(1, -1)
    rest = GROUP_MIX + GROUP_FFN
    first, dw_all = _gather_weights(_place_shards(weights, GROUP_IN, 0, me),
                                    a["conv_w_dw"].reshape(DEPTH, CONV_K, BW // NDEV), GROUP_IN)
    conv_w = jnp.transpose(dw_all, (1, 2, 0, 3)).reshape(DEPTH, CONV_K, BW)
    go_rest0 = _gather_start(_place_shards(weights, rest, 0, me), rest, dw_all, "rest0")
    going = {}

    def early_params(l, w_in):
        return dict(
            norm1=row(a["norm1"][l]), w_in=w_in,
            are=row(a["ssm_a_re"][l]), aim=row(a["ssm_a_im"][l]),
            ldt=row(jnp.repeat(a["ssm_log_dt"][l], NSTATE)),
            bexp_re=_expand_b(jnp.transpose(a["ssm_b_re"][l], (0, 2, 1))),
            bexp_im=_expand_b(jnp.transpose(a["ssm_b_im"][l], (0, 2, 1))),
            cexp_re=_expand_c(a["ssm_c_re"][l]), cexp_im=_expand_c(a["ssm_c_im"][l]),
            dskip=row(a["ssm_d"][l]), conv_w=conv_w[l], conv_b=row(a["conv_b_dw"][l]))

    def late_params(l, handle, tag, then_start):
        def get(after):
            full = _gather_finish(handle, rest, after, tag)
            token = then_start(full["ffn_w_down"]) if then_start else no_token
            return dict(
                wglu=full["ssm_w_glu"], bglu=row(a["ssm_b_glu"][l]), wpa=full["ssm_w_proj"],
                lng=row(a["conv_ln_g"][l]), lnb=row(a["conv_ln_b"][l]), wpb=full["conv_w_proj"],
                wgrp=a["pool_w_group"][l].astype(BF16), scale=row(a["pool_scale"][l]), wpc=full["pool_w_proj"],
                bgate=row(a["b_gate"][l]), wout=full["w_out"],
                norm2=row(a["norm2"][l]), wg=full["ffn_w_gate"], wu=full["ffn_w_up"], wd=full["ffn_w_down"]), token
        return get

    def start_in1(after):
        going["in1"] = _gather_start(_place_shards(weights, GROUP_IN, 1, me), GROUP_IN, after, "in1")
        return going["in1"]["token"]

    x, s0, p0 = _layer_fwd(x0, early_params(0, first["w_in"]), go_rest0["token"], late_params(0, go_rest0, "rest0", start_in1))
    w_in1 = _gather_finish(going["in1"], GROUP_IN, x, "in1")["w_in"]
    go_rest1 = _gather_start(_place_shards(weights, rest, 1, me), rest, w_in1, "rest1")
    x, s1, p1 = _layer_fwd(x, early_params(1, w_in1), go_rest1["token"], late_params(1, go_rest1, "rest1", None))
    params, saved = [p0, p1], [s0, s1]

    loss_part, dx, d_final = _loss_head(x, a["final_norm"].reshape(1, D), target)
    loss = lax.psum(loss_part[0, 0], ("x", "y", "c"))

    parts, swapping, gone = {}, [], []

    def swap(tag, grads, after):
        big = {n: [g] for n, g in grads.items()}
        names_, handle = _pair_start(tag, big, after)
        swapping.append((tag, names_, handle, big))
        return handle["token"]

    def send_on(after):
        tag, names_, handle, big = swapping.pop(0)
        parts[tag] = _pair_finish(tag, names_, handle, big, core, after)
        gone.append(_chip_start(parts[tag], tag) + (tag,))
        return gone[-1][1]["token"]

    def leave(tag, grads):
        token = send_on(next(iter(grads.values())))
        if tag != "in":
            return swap(tag, grads, token)
        parts[tag] = _pair_reduce(tag, {n: [g] for n, g in grads.items()}, {}, {}, core)
        gone.append(_chip_start(parts[tag], tag, token) + (tag,))
        return gone[-1][1]["token"]

    dx, gb1, go1, gs1 = _layer_bwd(dx, params[1], saved[1], no_token)
    dx, gb0, go0, gs0 = _layer_bwd(dx, params[0], saved[0], swap("late", gb1, dx), leave)
    grad_x = dx.reshape(1, t_rows, D)
    small = {n: [gs0[n], gs1[n]] for n in _SMALL}
    small["final_norm"] = [d_final]
    part_small = _pair_reduce("rest", {}, {"conv_w_dw": [go0, go1]}, small, core)
    recv = _chip_exchange(part_small, {"conv_w_dw"}, set(part_small))
    part0, part1, recv1 = {}, parts["late"], None
    for names_, handle, tag in gone:
        landed = _chip_wait(names_, handle, dx, tag)
        if tag == "late":
            recv1 = landed
        else:
            recv.update(landed)
            part0.update(parts[tag])

    results = {}
    for name in _BIG:
        fix = tr if name in ("ffn_w_gate", "ffn_w_up") else (lambda t: t)
        res = _adam_big(name, [part0[name], part1[name]], [recv[name], recv1[name]], fix(a[name]), fix(a["m_" + name]),
                        fix(a["v_" + name]), _ADAM_TK[name], chip)
        results[name] = tuple(fix(r) for r in res)

    lay = {
        "ssm_b_re": lambda v: jnp.transpose(v, (0, 1, 3, 2)), "ssm_b_im": lambda v: jnp.transpose(v, (0, 1, 3, 2)),
        "conv_w_dw": lambda v: v.reshape(DEPTH, CONV_K, BW // NDEV), "final_norm": lambda v: v.reshape(1, 1, D),
    }
    names = _SMALL + ("conv_w_dw", "final_norm")
    relay = lambda k, v: lay[k](v) if k in lay else v
    sm = _adam_small(names, recv, {k: relay(k, a[k]) for k in names}, {k: relay(k, a["m_" + k]) for k in names},
                     {k: relay(k, a["v_" + k]) for k in names})
    for k in names:
        back = (lambda r: jnp.transpose(r, (0, 1, 3, 2))) if k in ("ssm_b_re", "ssm_b_im") else (lambda r: r.reshape(a[k].shape))
        results[k] = tuple(back(r) for r in sm[k])

    outs = [loss, grad_x]
    for part in range(4):
        outs += [results[k][part] for k in _OUT_ORDER]
    return tuple(outs)


def kernel(x, norm1, w_in, b_gate, ssm_a_re, ssm_a_im, ssm_log_dt, ssm_b_re, ssm_b_im, ssm_c_re, ssm_c_im, ssm_d, ssm_w_glu, ssm_b_glu, ssm_w_proj, conv_w_dw, conv_b_dw, conv_ln_g, conv_ln_b, conv_w_proj, pool_w_group, pool_scale, pool_w_proj, w_out, norm2, ffn_w_gate, ffn_w_up, ffn_w_down, final_norm, loss_target, m_norm1, m_w_in, m_b_gate, m_ssm_a_re, m_ssm_a_im, m_ssm_log_dt, m_ssm_b_re, m_ssm_b_im, m_ssm_c_re, m_ssm_c_im, m_ssm_d, m_ssm_w_glu, m_ssm_b_glu, m_ssm_w_proj, m_conv_w_dw, m_conv_b_dw, m_conv_ln_g, m_conv_ln_b, m_conv_w_proj, m_pool_w_group, m_pool_scale, m_pool_w_proj, m_w_out, m_norm2, m_ffn_w_gate, m_ffn_w_up, m_ffn_w_down, m_final_norm, v_norm1, v_w_in, v_b_gate, v_ssm_a_re, v_ssm_a_im, v_ssm_log_dt, v_ssm_b_re, v_ssm_b_im, v_ssm_c_re, v_ssm_c_im, v_ssm_d, v_ssm_w_glu, v_ssm_b_glu, v_ssm_w_proj, v_conv_w_dw, v_conv_b_dw, v_conv_ln_g, v_conv_ln_b, v_conv_w_proj, v_pool_w_group, v_pool_scale, v_pool_w_proj, v_w_out, v_norm2, v_ffn_w_gate, v_ffn_w_up, v_ffn_w_down, v_final_norm):
    return _train_step(dict(locals()))
```

```python
import functools

import jax
import jax.numpy as jnp
from jax import lax
from jax.experimental import pallas as pl
from jax.experimental.pallas import tpu as pltpu

F32 = jnp.float32
BF16 = jnp.bfloat16

NDEV = 8
DEPTH = 2
D = 1024
BW = 512
NSTATE = 64
SGRP = 16
NGRP = BW // SGRP
GB = 8
NBLK = NGRP // GB
NS = GB * NSTATE
CONV_K = 31
HALO = 32
PHALO = 16
IN_W = 5120
HID = 2816
HSH = HID // NDEV
HPAD = 384
HIDP = HPAD * NDEV
EPS = 1e-6
VMEM_LIMIT = 56 * 1024 * 1024

ADAM_LR, ADAM_B1, ADAM_B2, ADAM_EPS, ADAM_WD, ADAM_STEP = 0.001, 0.9, 0.999, 1e-08, 0.01, 10

MESH = pl.DeviceIdType.MESH
ANY = pl.BlockSpec(memory_space=pl.ANY)


def _call(body, **kw):
    return pl.pallas_call(body, **kw)


def _cparams(sem=None):
    return pltpu.CompilerParams(dimension_semantics=sem, vmem_limit_bytes=VMEM_LIMIT)


def _dot(a, b):
    return jnp.dot(a.astype(BF16), b.astype(BF16), preferred_element_type=F32)


def _dot_nt(a, b):
    return lax.dot_general(a.astype(BF16), b.astype(BF16), (((1,), (1,)), ((), ())), preferred_element_type=F32)


def _dot_tn(a, b):
    return lax.dot_general(a.astype(BF16), b.astype(BF16), (((0,), (0,)), ((), ())), preferred_element_type=F32)


@jax.custom_vjp
def _mm(a, w):
    return _dot(a, w)


def _mm_fwd(a, w):
    return _dot(a, w), w


def _mm_bwd(w, ct):
    return _dot_nt(ct, w), jnp.zeros_like(w)


_mm.defvjp(_mm_fwd, _mm_bwd)


def _rms(x, g):
    return x * lax.rsqrt(jnp.mean(x * x, axis=-1, keepdims=True) + EPS) * g


def _disc(are, aim, ldt):
    dt = jnp.exp(ldt)
    mag = jnp.exp(dt * are)
    ang = dt * aim
    abr = mag * jnp.cos(ang)
    abi = mag * jnp.sin(ang)
    den = are * are + aim * aim
    nr = abr - 1.0
    fr = (nr * are + abi * aim) / den
    fi = (abi * are - nr * aim) / den
    return abr, abi, fr, fi


def _bbar(fr, fi, br, bi):
    return fr * br - fi * bi, fr * bi + fi * br


def _cmul(ar, ai, br, bi):
    return ar * br - ai * bi, ar * bi + ai * br


def _scan_rows(re_ref, im_ref, ar, ai, n_rows, reverse, hre_ref=None, him_ref=None):
    n = ar.shape[1]
    shape = (8, n)
    rows = lax.broadcasted_iota(jnp.int32, shape, 0)
    a1 = (jnp.broadcast_to(ar, shape), jnp.broadcast_to(ai, shape))
    a2 = _cmul(*a1, *a1)
    a4 = _cmul(*a2, *a2)
    pr = jnp.zeros(shape, F32)
    pi = jnp.zeros(shape, F32)
    pw = a1
    for k in range(8):
        sel = rows == ((7 - k) if reverse else k)
        pr = jnp.where(sel, pw[0], pr)
        pi = jnp.where(sel, pw[1], pi)
        pw = _cmul(*pw, *a1)
    nt = n_rows // 8
    with_acc = hre_ref is not None

    def body(i, carry):
        cr, ci = carry[0], carry[1]
        t = (nt - 1 - i) if reverse else i
        off = pl.multiple_of(t * 8, 8)
        xr = re_ref[pl.ds(off, 8), :]
        xi = im_ref[pl.ds(off, 8), :]
        for k, (kr, ki) in ((1, a1), (2, a2), (4, a4)):
            if reverse:
                keep, sh = rows < 8 - k, 8 - k
            else:
                keep, sh = rows >= k, k
            sr = jnp.where(keep, pltpu.roll(xr, sh, 0), 0.0)
            si = jnp.where(keep, pltpu.roll(xi, sh, 0), 0.0)
            xr, xi = xr + kr * sr - ki * si, xi + kr * si + ki * sr
        xr, xi = xr + pr * cr - pi * ci, xi + pr * ci + pi * cr
        re_ref[pl.ds(off, 8), :] = xr
        im_ref[pl.ds(off, 8), :] = xi
        edge = 0 if reverse else 7
        out = (jnp.broadcast_to(xr[edge:edge + 1, :], shape), jnp.broadcast_to(xi[edge:edge + 1, :], shape))
        if with_acc:
            hr = hre_ref[pl.ds(off, 8), :]
            hi = him_ref[pl.ds(off, 8), :]
            offp = pl.multiple_of(jnp.maximum(t - 1, 0) * 8, 8)
            live = jnp.where(t > 0, 1.0, 0.0)
            lr = jnp.broadcast_to(hre_ref[pl.ds(offp, 8), :][7:8, :], shape) * live
            li = jnp.broadcast_to(him_ref[pl.ds(offp, 8), :][7:8, :], shape) * live
            hpr = jnp.where(rows == 0, lr, pltpu.roll(hr, 1, 0))
            hpi = jnp.where(rows == 0, li, pltpu.roll(hi, 1, 0))
            out = out + (carry[2] + xr * hpr + xi * hpi, carry[3] + xi * hpr - xr * hpi)
        return out

    zero = jnp.zeros(shape, F32)
    init = (zero, zero, zero, zero) if with_acc else (zero, zero)
    res = lax.fori_loop(0, nt, body, init)
    return res[2:] if with_acc else None


TOKEN_SPEC = pl.BlockSpec((8, 128), lambda i, j: (0, 0))


def _inproj_fwd(x, gamma, w, token, tm=1024, nb=2):
    t_rows = x.shape[0]
    tm = min(tm, t_rows)
    oc = w.shape[2]
    n = NDEV * oc
    tn = nb * oc

    def body(x_ref, g_ref, w_ref, token_ref, z_ref, h_ref):
        @pl.when(pl.program_id(1) == 0)
        def _():
            h_ref[...] = _rms(x_ref[...], g_ref[...]).astype(BF16)
        for q in range(nb):
            z_ref[:, oc * q:oc * (q + 1)] = jnp.dot(h_ref[...], w_ref[q], preferred_element_type=F32)

    return _call(
        body, name="inproj_fwd", grid=(t_rows // tm, n // tn),
        in_specs=[pl.BlockSpec((tm, D), lambda i, j: (i, 0)), pl.BlockSpec((1, D), lambda i, j: (0, 0)),
                  pl.BlockSpec((nb, D, oc), lambda i, j: (j, 0, 0)), TOKEN_SPEC],
        out_specs=[pl.BlockSpec((tm, tn), lambda i, j: (i, j)), pl.BlockSpec((tm, D), lambda i, j: (i, 0))],
        out_shape=[jax.ShapeDtypeStruct((t_rows, n), F32), jax.ShapeDtypeStruct((t_rows, D), BF16)],
        compiler_params=_cparams(("parallel", "arbitrary")))(x, gamma, w, token)


def _ssm_specs(t_rows):
    row = pl.BlockSpec((1, NS), lambda j: (0, j))
    return dict(
        u=pl.BlockSpec((t_rows, GB * SGRP), lambda j: (0, j)),
        row=row,
        bexp=pl.BlockSpec((None, GB * SGRP, NS), lambda j: (j, 0, 0)),
        cexp=pl.BlockSpec((None, NS, GB * SGRP), lambda j: (j, 0, 0)),
        d=pl.BlockSpec((1, GB * SGRP), lambda j: (0, j)),
        h=pl.BlockSpec((t_rows, NS), lambda j: (0, j)),
    )


def _ssm_fwd(z, p):
    t_rows = z.shape[0]
    s = _ssm_specs(t_rows)

    def body(u_ref, are_ref, aim_ref, ldt_ref, br_ref, bi_ref, cr_ref, ci_ref, d_ref, y_ref, hr_ref, hi_ref):
        abr, abi, fr, fi = _disc(are_ref[...], aim_ref[...], ldt_ref[...])
        bbr, bbi = _bbar(fr, fi, br_ref[...], bi_ref[...])
        u = u_ref[...]
        hr_ref[...] = _dot(u, bbr)
        hi_ref[...] = _dot(u, bbi)
        _scan_rows(hr_ref, hi_ref, abr, abi, t_rows, False)
        y_ref[...] = _dot(hr_ref[...], cr_ref[...]) - _dot(hi_ref[...], ci_ref[...]) + d_ref[...] * u

    return _call(
        body, name="ssm_fwd", grid=(NBLK,),
        in_specs=[s["u"], s["row"], s["row"], s["row"], s["bexp"], s["bexp"], s["cexp"], s["cexp"], s["d"]],
        out_specs=[s["u"], s["h"], s["h"]],
        out_shape=[jax.ShapeDtypeStruct((t_rows, BW), F32), jax.ShapeDtypeStruct((t_rows, NGRP * NSTATE), F32),
                   jax.ShapeDtypeStruct((t_rows, NGRP * NSTATE), F32)],
        compiler_params=_cparams(("parallel",)))(
            z, p["are"], p["aim"], p["ldt"], p["bexp_re"], p["bexp_im"], p["cexp_re"], p["cexp_im"], p["dskip"])


def _ssm_bwd(dy, z, hre, him, p, token):
    t_rows = z.shape[0]
    s = _ssm_specs(t_rows)
    nstates = NGRP * NSTATE

    def body(dy_ref, u_ref, hr_ref, hi_ref, are_ref, aim_ref, ldt_ref, br_ref, bi_ref, cr_ref, ci_ref, d_ref, token_ref,
             du_ref, dbr_ref, dbi_ref, dcr_ref, dci_ref, dd_ref, dar_ref, dai_ref, dldt_ref, lr_ref, li_ref):
        rows3 = (are_ref[...], aim_ref[...], ldt_ref[...])
        (abr, abi, fr, fi), disc_vjp = jax.vjp(_disc, *rows3)
        (bbr, bbi), bbar_vjp = jax.vjp(_bbar, fr, fi, br_ref[...], bi_ref[...])
        dy = dy_ref[...]
        u = u_ref[...]
        lr_ref[...] = _dot_nt(dy, cr_ref[...])
        li_ref[...] = -_dot_nt(dy, ci_ref[...])
        dcr_ref[...] = _dot_tn(hr_ref[...], dy)
        dci_ref[...] = -_dot_tn(hi_ref[...], dy)
        dd_ref[...] = jnp.sum(dy * u, axis=0, keepdims=True)
        acc_r, acc_i = _scan_rows(lr_ref, li_ref, abr, -abi, t_rows, True, hr_ref, hi_ref)
        dabr = jnp.sum(acc_r, axis=0, keepdims=True)
        dabi = jnp.sum(acc_i, axis=0, keepdims=True)
        lam_r = lr_ref[...]
        lam_i = li_ref[...]
        du = d_ref[...] * dy + _dot_nt(lam_r, bbr) + _dot_nt(lam_i, bbi)
        du_ref[...] = du.astype(BF16)
        dbbr = _dot_tn(u, lam_r)
        dbbi = _dot_tn(u, lam_i)
        dfr, dfi, dbr, dbi = bbar_vjp((dbbr, dbbi))
        dbr_ref[...] = dbr
        dbi_ref[...] = dbi
        dar, dai, dldt = disc_vjp((dabr, dabi, dfr, dfi))
        dar_ref[...] = dar
        dai_ref[...] = dai
        lane_grp = lax.broadcasted_iota(jnp.int32, (NS, 128), 0) // NSTATE
        col = lax.broadcasted_iota(jnp.int32, (NS, 128), 1)
        seg = jnp.where(lane_grp == col, 1.0, 0.0).astype(F32)
        dldt_ref[...] = jnp.dot(jnp.broadcast_to(dldt, (8, NS)), seg, preferred_element_type=F32,
                                precision=lax.Precision.HIGHEST)

    dyspec = pl.BlockSpec((t_rows, GB * SGRP), lambda j: (0, j))
    return _call(
        body, name="ssm_bwd", grid=(NBLK,),
        in_specs=[dyspec, s["u"], s["h"], s["h"], s["row"], s["row"], s["row"], s["bexp"], s["bexp"], s["cexp"],
                  s["cexp"], s["d"], pl.BlockSpec((8, 128), lambda j: (0, 0))],
        out_specs=[dyspec, s["bexp"], s["bexp"], s["cexp"], s["cexp"], s["d"], s["row"], s["row"],
                   pl.BlockSpec((8, 128), lambda j: (j, 0))],
        out_shape=[jax.ShapeDtypeStruct((t_rows, BW), BF16),
                   jax.ShapeDtypeStruct((NBLK, GB * SGRP, NS), F32), jax.ShapeDtypeStruct((NBLK, GB * SGRP, NS), F32),
                   jax.ShapeDtypeStruct((NBLK, NS, GB * SGRP), F32), jax.ShapeDtypeStruct((NBLK, NS, GB * SGRP), F32),
                   jax.ShapeDtypeStruct((1, BW), F32), jax.ShapeDtypeStruct((1, nstates), F32),
                   jax.ShapeDtypeStruct((1, nstates), F32), jax.ShapeDtypeStruct((NBLK * 8, 128), F32)],
        scratch_shapes=[pltpu.VMEM((t_rows, NS), F32), pltpu.VMEM((t_rows, NS), F32)],
        compiler_params=_cparams(("parallel",)))(
            dy, z, hre, him, p["are"], p["aim"], p["ldt"], p["bexp_re"], p["bexp_im"], p["cexp_re"], p["cexp_im"],
            p["dskip"], token)


def _conv_fwd(z, w, b, tm=256):
    t_rows = z.shape[0]
    hb = tm // HALO

    def body(va_ref, vb_ref, ha_ref, hb_ref, w_ref, b_ref, o_ref, win_ref):
        live = jnp.where(pl.program_id(0) > 0, 1.0, 0.0)
        win_ref[0:HALO, :] = ha_ref[...] * jax.nn.sigmoid(hb_ref[...]) * live
        win_ref[HALO:HALO + tm, :] = va_ref[...] * jax.nn.sigmoid(vb_ref[...])
        acc = jnp.broadcast_to(b_ref[...], (tm, BW))
        for k in range(CONV_K):
            acc = acc + w_ref[k:k + 1, :] * win_ref[pl.ds(HALO - (CONV_K - 1) + k, tm), :]
        o_ref[...] = acc

    halo = lambda col: pl.BlockSpec((HALO, BW), lambda i: (jnp.maximum(i * hb - 1, 0), col))
    return _call(
        body, name="conv_fwd", grid=(t_rows // tm,),
        in_specs=[pl.BlockSpec((tm, BW), lambda i: (i, 1)), pl.BlockSpec((tm, BW), lambda i: (i, 2)), halo(1), halo(2),
                  pl.BlockSpec((CONV_K, BW), lambda i: (0, 0)), pl.BlockSpec((1, BW), lambda i: (0, 0))],
        out_specs=pl.BlockSpec((tm, BW), lambda i: (i, 0)),
        out_shape=jax.ShapeDtypeStruct((t_rows, BW), F32),
        scratch_shapes=[pltpu.VMEM((HALO + tm, BW), F32)],
        compiler_params=_cparams(("parallel",)))(z, z, z, z, w, b)


def _conv_bwd(dcv, z, w, tm=256):
    t_rows = z.shape[0]
    nt = t_rows // tm
    hb = tm // HALO
    csh = BW // NDEV

    def body(d_ref, dn_ref, va_ref, vb_ref, ha_ref, hb_ref, w_ref, dva_ref, dvb_ref, dw8_ref, db_ref,
             hwin_ref, dwin_ref, dw_ref):
        i = pl.program_id(0)

        @pl.when(i == 0)
        def _():
            dw_ref[...] = jnp.zeros_like(dw_ref)
            db_ref[...] = jnp.zeros_like(db_ref)

        live_prev = jnp.where(i > 0, 1.0, 0.0)
        live_next = jnp.where(i < nt - 1, 1.0, 0.0)
        va = va_ref[...]
        sig = jax.nn.sigmoid(vb_ref[...])
        hwin_ref[0:HALO, :] = ha_ref[...] * jax.nn.sigmoid(hb_ref[...]) * live_prev
        hwin_ref[HALO:HALO + tm, :] = va * sig
        d = d_ref[...]
        dwin_ref[0:tm, :] = d
        dwin_ref[tm:tm + HALO, :] = dn_ref[...] * live_next
        dh = jnp.zeros((tm, BW), F32)
        dws = []
        for k in range(CONV_K):
            dh = dh + w_ref[k:k + 1, :] * dwin_ref[pl.ds(CONV_K - 1 - k, tm), :]
            dws.append(jnp.sum(d * hwin_ref[pl.ds(HALO - (CONV_K - 1) + k, tm), :], axis=0, keepdims=True))
        dws.append(jnp.zeros((1, BW), F32))
        dw_ref[...] += jnp.concatenate(dws, axis=0)
        db_ref[...] += jnp.sum(d, axis=0, keepdims=True)
        dva_ref[...] = (dh * sig).astype(BF16)
        dvb_ref[...] = (dh * va * sig * (1.0 - sig)).astype(BF16)

        @pl.when(i == nt - 1)
        def _():
            acc = dw_ref[...]
            for q in range(NDEV):
                dw8_ref[q] = acc[:, csh * q:csh * (q + 1)]

    halo = lambda col: pl.BlockSpec((HALO, BW), lambda i: (jnp.maximum(i * hb - 1, 0), col))
    return _call(
        body, name="conv_bwd", grid=(nt,),
        in_specs=[pl.BlockSpec((tm, BW), lambda i: (i, 0)),
                  pl.BlockSpec((HALO, BW), lambda i: (jnp.minimum((i + 1) * hb, t_rows // HALO - 1), 0)),
                  pl.BlockSpec((tm, BW), lambda i: (i, 1)), pl.BlockSpec((tm, BW), lambda i: (i, 2)), halo(1), halo(2),
                  pl.BlockSpec((CONV_K, BW), lambda i: (0, 0))],
        out_specs=[pl.BlockSpec((tm, BW), lambda i: (i, 0)), pl.BlockSpec((tm, BW), lambda i: (i, 0)),
                   pl.BlockSpec((NDEV, 32, csh), lambda i: (0, 0, 0)), pl.BlockSpec((1, BW), lambda i: (0, 0))],
        out_shape=[jax.ShapeDtypeStruct((t_rows, BW), BF16), jax.ShapeDtypeStruct((t_rows, BW), BF16),
                   jax.ShapeDtypeStruct((NDEV, 32, csh), F32), jax.ShapeDtypeStruct((1, BW), F32)],
        scratch_shapes=[pltpu.VMEM((HALO + tm, BW), F32), pltpu.VMEM((tm + HALO, BW), F32), pltpu.VMEM((32, BW), F32)],
        compiler_params=_cparams(("arbitrary",)))(dcv, dcv, z, z, z, z, w)


def _pool_rows(i, tm, n_rows, first_row):
    grp = lax.broadcasted_iota(jnp.int32, (1, BW), 1) // (BW // 4)
    wlen = jnp.where(grp == 0, 2.0, jnp.where(grp == 1, 4.0, jnp.where(grp == 2, 8.0, 16.0)))
    t = (i * tm + first_row + lax.broadcasted_iota(jnp.int32, (n_rows, 1), 0)).astype(F32)
    return grp, 1.0 / jnp.minimum(t + 1.0, wlen)


def _pool_pick(grp, s2, s4, s8, s16):
    return jnp.where(grp == 0, s2, jnp.where(grp == 1, s4, jnp.where(grp == 2, s8, s16)))


def _pool_fwd(z, tm=256):
    t_rows = z.shape[0]
    hb = tm // PHALO

    def body(u_ref, h_ref, o_ref):
        i = pl.program_id(0)
        u = u_ref[...]
        win = jnp.concatenate([h_ref[...] * jnp.where(i > 0, 1.0, 0.0), u], axis=0)
        s2 = win + pltpu.roll(win, 1, 0)
        s4 = s2 + pltpu.roll(s2, 2, 0)
        s8 = s4 + pltpu.roll(s4, 4, 0)
        s16 = s8 + pltpu.roll(s8, 8, 0)
        grp, inv = _pool_rows(i, tm, tm, 0)
        o_ref[...] = _pool_pick(grp, s2, s4, s8, s16)[PHALO:, :] * inv - u

    return _call(
        body, name="pool_fwd", grid=(t_rows // tm,),
        in_specs=[pl.BlockSpec((tm, BW), lambda i: (i, 3)),
                  pl.BlockSpec((PHALO, BW), lambda i: (jnp.maximum(i * hb - 1, 0), 3))],
        out_specs=pl.BlockSpec((tm, BW), lambda i: (i, 0)),
        out_shape=jax.ShapeDtypeStruct((t_rows, BW), F32),
        compiler_params=_cparams(("parallel",)))(z, z)


def _pool_bwd(dp, tm=256):
    t_rows = dp.shape[0]
    nt = t_rows // tm
    hb = tm // PHALO
    ln = tm + PHALO

    def body(d_ref, dn_ref, o_ref):
        i = pl.program_id(0)
        d = d_ref[...]
        grp, inv = _pool_rows(i, tm, ln, 0)
        win = jnp.concatenate([d, dn_ref[...] * jnp.where(i < nt - 1, 1.0, 0.0)], axis=0) * inv
        s2 = win + pltpu.roll(win, ln - 1, 0)
        s4 = s2 + pltpu.roll(s2, ln - 2, 0)
        s8 = s4 + pltpu.roll(s4, ln - 4, 0)
        s16 = s8 + pltpu.roll(s8, ln - 8, 0)
        o_ref[...] = (_pool_pick(grp, s2, s4, s8, s16)[:tm, :] - d).astype(BF16)

    return _call(
        body, name="pool_bwd", grid=(nt,),
        in_specs=[pl.BlockSpec((tm, BW), lambda i: (i, 0)),
                  pl.BlockSpec((PHALO, BW), lambda i: (jnp.minimum((i + 1) * hb, t_rows // PHALO - 1), 0))],
        out_specs=pl.BlockSpec((tm, BW), lambda i: (i, 0)),
        out_shape=jax.ShapeDtypeStruct((t_rows, BW), BF16),
        compiler_params=_cparams(("parallel",)))(dp, dp)


_MERGE_W = ("wglu", "bglu", "wpa", "lng", "lnb", "wpb", "wgrp", "scale", "wpc", "bgate", "wout")
_MERGE_SMALL = ("bglu", "lng", "lnb", "scale", "bgate")
_MERGE_BLOCKED = ("wpa", "wpb", "wpc")


def _merge_load(name, ref):
    if name in _MERGE_BLOCKED:
        return jnp.concatenate([ref[q] for q in range(NDEV)], axis=1)
    return ref[...]


def _merge_math(x, yssm, cv, pbar, zg, w, taps):
    t_glu, t_ya, t_yb, t_p, t_yc = taps
    g = jax.nn.gelu(yssm)
    outa = g * jax.nn.sigmoid(_mm(g, w["wglu"]) + t_glu + w["bglu"])
    ya = _mm(outa, w["wpa"]) + t_ya
    mu = jnp.mean(cv, axis=-1, keepdims=True)
    var = jnp.mean(jnp.square(cv - mu), axis=-1, keepdims=True)
    hs = jax.nn.silu((cv - mu) * lax.rsqrt(var + EPS) * w["lng"] + w["lnb"])
    yb = _mm(hs, w["wpb"]) + t_yb
    gw = BW // 4
    pk = jnp.concatenate([_mm(pbar[:, gw * k:gw * (k + 1)], w["wgrp"][k]) for k in range(4)], axis=1) + t_p
    pc = pk * w["scale"]
    yc = _mm(pc, w["wpc"]) + t_yc
    gates = jax.nn.sigmoid(zg + w["bgate"])
    merged = gates[:, :D] * ya + gates[:, D:2 * D] * yb + gates[:, 2 * D:] * yc
    x1 = x + _mm(merged, w["wout"])
    acts = tuple(a.astype(BF16) for a in (g, outa, hs, pbar, pc, merged))
    return x1, acts


def _merge_specs(tm, p):
    rows = lambda width, col=0: pl.BlockSpec((tm, width), lambda i, c=col: (i, c))
    data = [rows(D), rows(BW), rows(BW), rows(BW), rows(D, 2), rows(D, 3), rows(D, 4)]
    wspecs = []
    for name in _MERGE_W:
        nd = p[name].ndim
        wspecs.append(pl.BlockSpec(p[name].shape, lambda i, nd=nd: (0,) * nd))
    return rows, data, wspecs


def _merge_fwd(x, yssm, cv, pbar, z, p, token, tm=512):
    t_rows = x.shape[0]
    rows, data, wspecs = _merge_specs(tm, p)

    def body(x_ref, y_ref, cv_ref, pb_ref, za_ref, zb_ref, zc_ref, *rest):
        w = {name: _merge_load(name, r) for name, r in zip(_MERGE_W, rest[:len(_MERGE_W)])}
        o_ref = rest[len(_MERGE_W) + 1]
        taps = (0.0, 0.0, 0.0, 0.0, 0.0)
        zg = jnp.concatenate([za_ref[...], zb_ref[...], zc_ref[...]], axis=1)
        o_ref[...] = _merge_math(x_ref[...], y_ref[...], cv_ref[...], pb_ref[...], zg, w, taps)[0]

    return _call(
        body, name="merge_fwd", grid=(t_rows // tm,),
        in_specs=data + wspecs + [pl.BlockSpec((8, 128), lambda i: (0, 0))], out_specs=rows(D),
        out_shape=jax.ShapeDtypeStruct((t_rows, D), F32),
        compiler_params=_cparams(("parallel",)))(x, yssm, cv, pbar, z, z, z, *[p[n] for n in _MERGE_W], token)


def _merge_bwd(dx1, x, yssm, cv, pbar, z, p, token, tm=256):
    t_rows = x.shape[0]
    rows, data, wspecs = _merge_specs(tm, p)
    nw = len(_MERGE_W)

    def body(dx_ref, x_ref, y_ref, cv_ref, pb_ref, za_ref, zb_ref, zc_ref, *rest):
        w = {name: _merge_load(name, r) for name, r in zip(_MERGE_W, rest[:nw])}
        zg = jnp.concatenate([za_ref[...], zb_ref[...], zc_ref[...]], axis=1)
        outs = rest[nw + 1:]
        small = {n: w[n] for n in _MERGE_SMALL}
        taps = (jnp.zeros((tm, BW), F32), jnp.zeros((tm, D), F32), jnp.zeros((tm, D), F32),
                jnp.zeros((tm, BW), F32), jnp.zeros((tm, D), F32))

        def f(yssm_, cv_, pbar_, zg_, small_, taps_):
            return _merge_math(x_ref[...], yssm_, cv_, pbar_, zg_, {**w, **small_}, taps_)

        _, vjp, acts = jax.vjp(f, y_ref[...], cv_ref[...], pb_ref[...], zg, small, taps, has_aux=True)
        dy, dcv, dpb, dzg, dsmall, dtaps = vjp(dx_ref[...])
        outs[0][...] = dy
        outs[1][...] = dcv
        outs[2][...] = dpb
        outs[3][:, 4 * BW:] = dzg.astype(BF16)
        for k in range(6):
            outs[4 + k][...] = acts[k]
        for k in range(5):
            outs[10 + k][...] = dtaps[k].astype(BF16)

        @pl.when(pl.program_id(0) == 0)
        def _():
            for k in range(5):
                outs[15 + k][...] = jnp.zeros_like(outs[15 + k])

        for k, n in enumerate(_MERGE_SMALL):
            outs[15 + k][...] += dsmall[n]

    f32o = lambda width: jax.ShapeDtypeStruct((t_rows, width), F32)
    bfo = lambda width: jax.ShapeDtypeStruct((t_rows, width), BF16)
    small_shapes = [jax.ShapeDtypeStruct(p[n].shape, F32) for n in _MERGE_SMALL]
    small_specs = [pl.BlockSpec(p[n].shape, lambda i: (0, 0)) for n in _MERGE_SMALL]
    out_shape = ([f32o(BW), f32o(BW), f32o(BW), bfo(IN_W)]
                 + [bfo(BW), bfo(BW), bfo(BW), bfo(BW), bfo(BW), bfo(D)]
                 + [bfo(BW), bfo(D), bfo(D), bfo(BW), bfo(D)] + small_shapes)
    out_specs = ([rows(BW), rows(BW), rows(BW), rows(IN_W)]
                 + [rows(BW)] * 5 + [rows(D)]
                 + [rows(BW), rows(D), rows(D), rows(BW), rows(D)] + small_specs)
    return _call(
        body, name="merge_bwd", grid=(t_rows // tm,),
        in_specs=[rows(D)] + data + wspecs + [pl.BlockSpec((8, 128), lambda i: (0, 0))], out_specs=out_specs,
        out_shape=out_shape, compiler_params=_cparams(("arbitrary",)))(
            dx1, x, yssm, cv, pbar, z, z, z, *[p[n] for n in _MERGE_W], token)


def _ffn_fwd(x1, gamma, wg, wu, wd, tm=1024, th=512):
    t_rows = x1.shape[0]
    tm = min(tm, t_rows)
    nh = HIDP // th

    def body(x_ref, g_ref, wg_ref, wu_ref, wd_ref, o_ref, gp_ref, up_ref, h_ref, acc_ref):
        j = pl.program_id(1)

        @pl.when(j == 0)
        def _():
            h_ref[...] = _rms(x_ref[...], g_ref[...]).astype(BF16)
            acc_ref[...] = jnp.zeros_like(acc_ref)

        gp = _dot_nt(h_ref[...], wg_ref[...])
        up = _dot_nt(h_ref[...], wu_ref[...])
        gp_ref[...] = gp.astype(BF16)
        up_ref[...] = up.astype(BF16)
        acc_ref[...] += _dot(jax.nn.silu(gp) * up, wd_ref[...])

        @pl.when(j == nh - 1)
        def _():
            o_ref[...] = x_ref[...] + acc_ref[...]

    return _call(
        body, name="ffn_fwd", grid=(t_rows // tm, nh),
        in_specs=[pl.BlockSpec((tm, D), lambda i, j: (i, 0)), pl.BlockSpec((1, D), lambda i, j: (0, 0)),
                  pl.BlockSpec((th, D), lambda i, j: (j, 0)), pl.BlockSpec((th, D), lambda i, j: (j, 0)),
                  pl.BlockSpec((th, D), lambda i, j: (j, 0))],
        out_specs=[pl.BlockSpec((tm, D), lambda i, j: (i, 0)), pl.BlockSpec((tm, th), lambda i, j: (i, j)),
                   pl.BlockSpec((tm, th), lambda i, j: (i, j)), pl.BlockSpec((tm, D), lambda i, j: (i, 0))],
        out_shape=[jax.ShapeDtypeStruct((t_rows, D), F32), jax.ShapeDtypeStruct((t_rows, HIDP), BF16),
                   jax.ShapeDtypeStruct((t_rows, HIDP), BF16), jax.ShapeDtypeStruct((t_rows, D), BF16)],
        scratch_shapes=[pltpu.VMEM((tm, D), F32)],
        compiler_params=_cparams(("parallel", "arbitrary")))(x1, gamma, wg, wu, wd)


def _rms_bwd_tail(x, gamma, dh):
    _, vjp = jax.vjp(_rms, x, gamma)
    return vjp(dh)


def _ffn_bwd(dx2, x1, gamma, gpre, upre, wg, wu, wd, token, tm=1024, th=512):
    t_rows = x1.shape[0]
    tm = min(tm, t_rows)
    nh = HIDP // th

    def body(d_ref, x_ref, g_ref, gp_ref, up_ref, wg_ref, wu_ref, wd_ref, token_ref,
             dx_ref, dgam_ref, dgp_ref, dup_ref, act_ref, acc_ref):
        i = pl.program_id(0)
        j = pl.program_id(1)

        @pl.when(j == 0)
        def _():
            acc_ref[...] = jnp.zeros_like(acc_ref)

        @pl.when((i == 0) & (j == 0))
        def _():
            dgam_ref[...] = jnp.zeros_like(dgam_ref)

        dact = _dot_nt(d_ref[...], wd_ref[...])
        gp = gp_ref[...].astype(F32)
        up = up_ref[...].astype(F32)
        sg = jax.nn.sigmoid(gp)
        silu = gp * sg
        dgp = (dact * up * (sg * (1.0 + gp * (1.0 - sg)))).astype(BF16)
        dup = (dact * silu).astype(BF16)
        dgp_ref[...] = dgp
        dup_ref[...] = dup
        act_ref[...] = (silu * up).astype(BF16)
        acc_ref[...] += _dot(dgp, wg_ref[...]) + _dot(dup, wu_ref[...])

        @pl.when(j == nh - 1)
        def _():
            x = x_ref[...]
            dx, dgam = _rms_bwd_tail(x, g_ref[...], acc_ref[...])
            dx_ref[...] = d_ref[...] + dx
            dgam_ref[...] += dgam

    row_d = pl.BlockSpec((tm, D), lambda i, j: (i, 0))
    row_h = pl.BlockSpec((tm, th), lambda i, j: (i, j))
    return _call(
        body, name="ffn_bwd", grid=(t_rows // tm, nh),
        in_specs=[row_d, row_d, pl.BlockSpec((1, D), lambda i, j: (0, 0)), row_h, row_h,
                  pl.BlockSpec((th, D), lambda i, j: (j, 0)), pl.BlockSpec((th, D), lambda i, j: (j, 0)),
                  pl.BlockSpec((th, D), lambda i, j: (j, 0)), TOKEN_SPEC],
        out_specs=[row_d, pl.BlockSpec((1, D), lambda i, j: (0, 0)), row_h, row_h, row_h],
        out_shape=[jax.ShapeDtypeStruct((t_rows, D), F32), jax.ShapeDtypeStruct((1, D), F32),
                   jax.ShapeDtypeStruct((t_rows, HIDP), BF16), jax.ShapeDtypeStruct((t_rows, HIDP), BF16),
                   jax.ShapeDtypeStruct((t_rows, HIDP), BF16)],
        scratch_shapes=[pltpu.VMEM((tm, D), F32)],
        compiler_params=_cparams(("arbitrary", "arbitrary")))(dx2, x1, gamma, gpre, upre, wg, wu, wd, token)


def _inproj_bwd(dz, dx1, x, gamma, w, token, tm=1024, nb=2):
    t_rows = x.shape[0]
    tm = min(tm, t_rows)
    oc = w.shape[2]
    tn = nb * oc
    nn = NDEV // nb

    def body(dz_ref, d1_ref, x_ref, g_ref, w_ref, token_ref, dx_ref, dgam_ref, acc_ref):
        i = pl.program_id(0)
        j = pl.program_id(1)

        @pl.when(j == 0)
        def _():
            acc_ref[...] = jnp.zeros_like(acc_ref)

        @pl.when((i == 0) & (j == 0))
        def _():
            dgam_ref[...] = jnp.zeros_like(dgam_ref)

        for q in range(nb):
            acc_ref[...] += _dot_nt(dz_ref[:, oc * q:oc * (q + 1)], w_ref[q])

        @pl.when(j == nn - 1)
        def _():
            x = x_ref[...]
            dx, dgam = _rms_bwd_tail(x, g_ref[...], acc_ref[...])
            dx_ref[...] = d1_ref[...] + dx
            dgam_ref[...] += dgam

    row_d = pl.BlockSpec((tm, D), lambda i, j: (i, 0))
    return _call(
        body, name="inproj_bwd", grid=(t_rows // tm, nn),
        in_specs=[pl.BlockSpec((tm, tn), lambda i, j: (i, j)), row_d, row_d, pl.BlockSpec((1, D), lambda i, j: (0, 0)),
                  pl.BlockSpec((nb, D, oc), lambda i, j: (j, 0, 0)), TOKEN_SPEC],
        out_specs=[row_d, pl.BlockSpec((1, D), lambda i, j: (0, 0))],
        out_shape=[jax.ShapeDtypeStruct((t_rows, D), F32), jax.ShapeDtypeStruct((1, D), F32)],
        scratch_shapes=[pltpu.VMEM((tm, D), F32)],
        compiler_params=_cparams(("arbitrary", "arbitrary")))(dz, dx1, x, gamma, w, token)


def _matmul_tn(a, b, name, owner_cols=None, tt=2048):
    t_rows, k = a.shape
    n = b.shape[1]
    tk = min(k, 1024)
    tt = min(tt, t_rows)
    nt = t_rows // tt
    if owner_cols is None:
        tn, nb = min(n, 1024), None
        out_spec = pl.BlockSpec((tk, tn), lambda i, j, t: (i, j))
        out_shape = jax.ShapeDtypeStruct((k, n), BF16)
    else:
        nb = min(n // owner_cols, max(1, 1280 // owner_cols))
        tn = nb * owner_cols
        out_spec = pl.BlockSpec((nb, tk, owner_cols), lambda i, j, t: (j, i, 0))
        out_shape = jax.ShapeDtypeStruct((n // owner_cols, k, owner_cols), BF16)

    def body(a_ref, b_ref, o_ref, acc_ref):
        t = pl.program_id(2)

        @pl.when(t == 0)
        def _():
            acc_ref[...] = jnp.zeros_like(acc_ref)

        acc_ref[...] += _dot_tn(a_ref[...], b_ref[...])

        @pl.when(t == nt - 1)
        def _():
            if nb is None:
                o_ref[...] = acc_ref[...].astype(BF16)
            else:
                for q in range(nb):
                    o_ref[q] = acc_ref[:, owner_cols * q:owner_cols * (q + 1)].astype(BF16)

    return _call(
        body, name=name, grid=(k // tk, n // tn, nt),
        in_specs=[pl.BlockSpec((tt, tk), lambda i, j, t: (t, i)), pl.BlockSpec((tt, tn), lambda i, j, t: (t, j))],
        out_specs=out_spec, out_shape=out_shape,
        scratch_shapes=[pltpu.VMEM((tk, tn), F32)],
        compiler_params=_cparams(("parallel", "parallel", "arbitrary")))(a, b)


def _group_tn(a, b):
    t_rows = a.shape[0]
    gw = BW // 4

    def body(a_ref, b_ref, o_ref):
        o_ref[...] = _dot_tn(a_ref[...], b_ref[...])

    return _call(
        body, name="pool_group_tn", grid=(4,),
        in_specs=[pl.BlockSpec((t_rows, gw), lambda k: (0, k)), pl.BlockSpec((t_rows, gw), lambda k: (0, k))],
        out_specs=pl.BlockSpec((None, gw, gw), lambda k: (k, 0, 0)),
        out_shape=jax.ShapeDtypeStruct((4, gw, gw), F32),
        compiler_params=_cparams(("parallel",)))(a, b)


def _loss_head(x2, gamma, target, tm=512):
    t_rows = x2.shape[0]

    def body(x_ref, g_ref, t_ref, loss_ref, dx_ref, dgam_ref):
        @pl.when(pl.program_id(0) == 0)
        def _():
            loss_ref[...] = jnp.zeros_like(loss_ref)
            dgam_ref[...] = jnp.zeros_like(dgam_ref)

        def f(x, g):
            err = jnp.square(_rms(x, g) - t_ref[...])
            return 0.5 * jnp.sum(jnp.mean(err, axis=-1, keepdims=True), axis=0, keepdims=True)

        loss, vjp = jax.vjp(f, x_ref[...], g_ref[...])
        dx, dgam = vjp(jnp.ones((1, 1), F32))
        loss_ref[...] += jnp.broadcast_to(loss, (1, 128))
        dx_ref[...] = dx
        dgam_ref[...] += dgam

    row_d = pl.BlockSpec((tm, D), lambda i: (i, 0))
    return _call(
        body, name="loss_head", grid=(t_rows // tm,),
        in_specs=[row_d, pl.BlockSpec((1, D), lambda i: (0, 0)), row_d],
        out_specs=[pl.BlockSpec((1, 128), lambda i: (0, 0)), row_d, pl.BlockSpec((1, D), lambda i: (0, 0))],
        out_shape=[jax.ShapeDtypeStruct((1, 128), F32), jax.ShapeDtypeStruct((t_rows, D), F32),
                   jax.ShapeDtypeStruct((1, D), F32)],
        compiler_params=_cparams(("arbitrary",)))(x2, gamma, target)


NCHIP = NDEV // 2


def _coords():
    return lax.axis_index("x"), lax.axis_index("y"), lax.axis_index("c")


def _remote(src, dst, send_sem, recv_sem, peer):
    return pltpu.make_async_remote_copy(src_ref=src, dst_ref=dst, send_sem=send_sem, recv_sem=recv_sem,
                                        device_id=peer, device_id_type=MESH)


def _comm_call(name, srcs, out_shapes, n_rec, plan, aliases=None):
    ns, no = len(srcs), len(out_shapes)

    def body(*refs):
        ins, outs = refs[:ns], refs[ns:ns + no]
        loc_sem, send_sem, recv_sem = refs[ns + no:]
        x, y, c = _coords()
        recs = plan(ins, outs, x, y, c)
        assert len(recs) == n_rec
        for k, r in enumerate(recs):
            for src, dst in r.get("local", ()):
                pltpu.make_async_copy(src, dst, loc_sem.at[k]).start()
            for peer, src, dst in r.get("remote", ()):
                _remote(src, dst, send_sem.at[k], recv_sem.at[k], peer).start()
        for k, r in enumerate(recs):
            if r.get("recv_wait") is not None:
                w = r["recv_wait"]
                _remote(w, w, send_sem.at[k], recv_sem.at[k], (x, y, c)).wait_recv()
            if r.get("send_wait") is not None:
                w = r["send_wait"]
                _remote(w, w, send_sem.at[k], recv_sem.at[k], (x, y, c)).wait_send()
            if r.get("local_wait") is not None:
                w = r["local_wait"]
                pltpu.make_async_copy(w, w, loc_sem.at[k]).wait()

    return _call(
        body, name=name, in_specs=[ANY] * ns, out_specs=[ANY] * no, out_shape=out_shapes,
        input_output_aliases=aliases or {}, scratch_shapes=[pltpu.SemaphoreType.DMA((n_rec,))] * 3)(*srcs)


def _gather_call(srcs, out_shapes, items, aliases):
    ns, no, n = len(srcs), len(out_shapes), len(items)

    def body(*refs):
        ins, outs = refs[:ns], refs[ns:ns + no]
        loc, sib_s, sib_r, ici_s, ici_r, fwd_s, fwd_r = refs[ns + no:]
        x, y, c = _coords()
        me, sib = (x, y, c), (x, y, 1 - c)
        chips = [(1 - x, y), (x, 1 - y), (1 - x, 1 - y)]
        index = lambda px, py, pc: 4 * px + 2 * py + pc
        for k, (si, oi, shard, block, _) in enumerate(items):
            mine = block(outs[oi], index(*me))
            src = mine if shard is None else shard(ins[si])
            if shard is not None:
                pltpu.make_async_copy(src, mine, loc.at[k]).start()
            _remote(src, mine, sib_s.at[k], sib_r.at[k], sib).start()
            for chip in chips:
                _remote(src, mine, ici_s.at[k], ici_r.at[k], (*chip, c)).start()
        for k, (si, oi, _, block, blocks) in enumerate(items):
            three = blocks(outs[oi], 3)
            _remote(three, three, ici_s.at[k], ici_r.at[k], me).wait_recv()
            for chip in chips:
                landed = block(outs[oi], index(*chip, c))
                _remote(landed, landed, fwd_s.at[k], fwd_r.at[k], sib).start()
        for k, (si, oi, shard, _, blocks) in enumerate(items):
            one, three = blocks(outs[oi], 1), blocks(outs[oi], 3)
            _remote(one, one, sib_s.at[k], sib_r.at[k], me).wait()
            _remote(three, three, fwd_s.at[k], fwd_r.at[k], me).wait()
            _remote(three, three, ici_s.at[k], ici_r.at[k], me).wait_send()
            if shard is not None:
                pltpu.make_async_copy(one, one, loc.at[k]).wait()

    return _call(
        body, name="gather_weights", in_specs=[ANY] * ns, out_specs=[ANY] * no, out_shape=out_shapes,
        input_output_aliases=aliases, scratch_shapes=[pltpu.SemaphoreType.DMA((n,))] * 7)(*srcs)


_BIG = {
    "w_in": (True, D, IN_W // NDEV),
    "ssm_w_glu": (False, BW // NDEV, BW),
    "ssm_w_proj": (True, BW, D // NDEV),
    "conv_w_proj": (True, BW, D // NDEV),
    "pool_w_proj": (True, BW, D // NDEV),
    "w_out": (False, D // NDEV, D),
    "ffn_w_gate": (False, HPAD, D),
    "ffn_w_up": (False, HPAD, D),
    "ffn_w_down": (False, HPAD, D),
}
GROUP_IN = ("w_in",)
GROUP_MIX = ("ssm_w_glu", "ssm_w_proj", "conv_w_proj", "pool_w_proj", "w_out")
GROUP_FFN = ("ffn_w_gate", "ffn_w_up", "ffn_w_down")


def _gathered_shape(name):
    blocked, kk, nn = _BIG[name]
    return jax.ShapeDtypeStruct((NDEV, kk, nn) if blocked else (NDEV * kk, nn), BF16)


def _block_view(name):
    blocked, kk, _ = _BIG[name]
    if blocked:
        return lambda ref, q: ref.at[q]
    return lambda ref, q: ref.at[pl.ds(pl.multiple_of(q * kk, 16), kk), :]


def _blocks_view(name):
    blocked, kk, _ = _BIG[name]
    if blocked:
        return lambda ref, n: ref.at[pl.ds(0, n)]
    return lambda ref, n: ref.at[pl.ds(0, n * kk), :]


def _place_shards(weights, names, layer, me):
    cnt = len(names)

    def body(me_ref, *refs):
        w_refs, outs, stages, sem = refs[:cnt], refs[cnt:2 * cnt], refs[2 * cnt:3 * cnt], refs[3 * cnt]
        q = me_ref[0]
        copies = []
        for k, name in enumerate(names):
            _, kk, nn = _BIG[name]
            rows = w_refs[k].shape[0]
            stages[k][0:rows, :] = w_refs[k][...].astype(BF16)
            if rows < kk:
                stages[k][rows:kk, :] = jnp.zeros((kk - rows, nn), BF16)
            copies.append(pltpu.make_async_copy(stages[k], _block_view(name)(outs[k], q), sem.at[k]))
            copies[-1].start()
        for cp in copies:
            cp.wait()

    in_specs = [pl.BlockSpec(memory_space=pltpu.SMEM)]
    in_specs += [pl.BlockSpec((None,) + weights[n].shape[1:], lambda i: (layer, 0, 0)) for n in names]
    res = _call(
        body, name="place_shards", grid=(1,), in_specs=in_specs, out_specs=[ANY] * cnt,
        out_shape=[_gathered_shape(n) for n in names],
        scratch_shapes=[pltpu.VMEM(_BIG[n][1:], BF16) for n in names] + [pltpu.SemaphoreType.DMA((cnt,))],
        compiler_params=_cparams(("arbitrary",)))(me, *[weights[n] for n in names])
    return dict(zip(names, res))


def _gather_weights(placed, conv_dw, names):
    cnt = len(names)
    srcs = [placed[n] for n in names] + [conv_dw]
    outs = [_gathered_shape(n) for n in names] + [jax.ShapeDtypeStruct((NDEV,) + conv_dw.shape, conv_dw.dtype)]
    items = [(k, k, None, _block_view(n), _blocks_view(n)) for k, n in enumerate(names)]
    items.append((cnt, cnt, lambda ref: ref, lambda ref, q: ref.at[q], lambda ref, n: ref.at[pl.ds(0, n)]))
    res = _gather_call(srcs, outs, items, {k: k for k in range(cnt)})
    return dict(zip(names, res[:-1])), res[-1]


def _gather_start(placed, names, after, tag):
    def copies(src_refs, land_refs, x, y, c):
        me = 4 * x + 2 * y + c
        peers = [(x, y, 1 - c), (1 - x, y, c), (x, 1 - y, c), (1 - x, 1 - y, c)]
        out = []
        for k, n in enumerate(names):
            mine = _block_view(n)(land_refs[k], me)
            out.append([(peer, mine, mine) for peer in peers])
        return out

    return _split_start("gather_start_" + tag, [], [placed[n] for n in names], copies, after)


def _gather_finish(handle, names, after, tag):
    four = [functools.partial(lambda ref, bv: bv(ref, 4), bv=_blocks_view(n)) for n in names]
    lands = _split_wait("gather_wait_" + tag, handle, four, after)
    cnt = len(names)

    def plan(ins, out_refs, x, y, c):
        sib = (x, y, 1 - c)
        recs = []
        for k, n in enumerate(names):
            three = _blocks_view(n)(out_refs[k], 3)
            remote = []
            for px, py in [(1 - x, y), (x, 1 - y), (1 - x, 1 - y)]:
                landed = _block_view(n)(out_refs[k], 4 * px + 2 * py + c)
                remote.append((sib, landed, landed))
            recs.append(dict(remote=remote, send_wait=three, recv_wait=three))
        return recs

    res = _comm_call("gather_pair_" + tag, lands, [jax.ShapeDtypeStruct(l.shape, l.dtype) for l in lands], cnt, plan,
                     aliases={k: k for k in range(cnt)})
    return dict(zip(names, res))


def _pair_add(name, grads, rcv, core):
    nl = len(grads)
    _, kk, nn = grads[0].shape

    def body(c_ref, *refs):
        l = pl.program_id(0)
        own = refs[0][...]
        for j in range(1, nl):
            own = jnp.where(l == j, refs[j][...], own)
        refs[nl + 1][...] = (own.astype(F32) + refs[nl][...].astype(F32)).astype(BF16)

    gspec = lambda j: pl.BlockSpec((None, kk, nn), lambda l, h, c_ref: (jnp.where(l == j, 2 * h + c_ref[0], 0), 0, 0))
    rspec = pl.BlockSpec((None, None, kk, nn), lambda l, h, c_ref: (h, l, 0, 0))
    return _call(
        body, name="pair_add_" + name,
        grid_spec=pltpu.PrefetchScalarGridSpec(num_scalar_prefetch=1, grid=(nl, NCHIP),
                                               in_specs=[gspec(j) for j in range(nl)] + [rspec], out_specs=rspec),
        out_shape=jax.ShapeDtypeStruct(rcv.shape, BF16),
        compiler_params=_cparams(("arbitrary", "arbitrary")))(core, *grads, rcv)


def _pair_add_small(owned, lists, core, chip=None):
    on, ln = list(owned), list(lists)
    flat = []
    for n in on:
        flat += list(owned[n][0]) + [owned[n][1]]
    for n in ln:
        flat += list(lists[n][0]) + [lists[n][1]]

    def body(c_ref, h_ref, *refs):
        outs = refs[len(flat):]
        c = c_ref[0]
        pos = 0
        for k, n in enumerate(on):
            nl = len(owned[n][0])
            for h in range(NCHIP):
                for l in range(nl):
                    outs[k][h, l] = refs[pos + l][pl.ds(2 * h + c, 1)][0] + refs[pos + nl][h, l]
            pos += nl + 1
        for k, n in enumerate(ln):
            nl = len(lists[n][0])
            for l in range(nl):
                out = outs[len(on) + k]
                val = (refs[pos + l][...] + refs[pos + nl][l]).astype(out.dtype)
                if chip is None:
                    out[l] = val
                else:
                    out[h_ref[0], l] = val
            pos += nl + 1

    shapes = [jax.ShapeDtypeStruct(owned[n][1].shape, F32) for n in on]
    lead = () if chip is None else (NCHIP,)
    shapes += [jax.ShapeDtypeStruct(lead + lists[n][1].shape, F32 if n == "final_norm" else BF16) for n in ln]
    res = _call(body, name="pair_add_small", out_shape=shapes,
                in_specs=[pl.BlockSpec(memory_space=pltpu.SMEM)] * 2 + [pl.BlockSpec(memory_space=pltpu.VMEM)] * len(flat),
                compiler_params=_cparams())(core, core if chip is None else chip, *flat)
    return dict(zip(on + ln, res))


def _pair_reduce(tag, big, by_owner, small, core, chip=None):
    rs = {**big, **by_owner}
    srcs, outs, plans, rcv_at = [], [], [], {}
    for name, arrays in rs.items():
        rcv_at[name] = len(outs)
        outs.append(jax.ShapeDtypeStruct((NCHIP, len(arrays)) + arrays[0].shape[1:], arrays[0].dtype))
        for l, arr in enumerate(arrays):
            srcs.append(arr)
            plans.append((len(srcs) - 1, rcv_at[name], l, True))
    for name, arrays in small.items():
        rcv_at[name] = len(outs)
        outs.append(jax.ShapeDtypeStruct((len(arrays),) + arrays[0].shape, F32))
        for l, arr in enumerate(arrays):
            srcs.append(arr)
            plans.append((len(srcs) - 1, rcv_at[name], l, False))

    def plan_pair(ins, out_refs, x, y, c):
        sib = (x, y, 1 - c)
        recs = []
        for si, ro, l, slabs in plans:
            if slabs:
                four = out_refs[ro].at[pl.ds(0, NCHIP), l]
                recs.append(dict(remote=[(sib, ins[si].at[2 * h + 1 - c], out_refs[ro].at[h, l]) for h in range(NCHIP)],
                                 send_wait=four, recv_wait=four))
            else:
                dst = out_refs[ro].at[l]
                recs.append(dict(remote=[(sib, ins[si], dst)], send_wait=dst, recv_wait=dst))
        return recs

    res = _comm_call("pair_exchange_" + tag, srcs, outs, len(plans), plan_pair)
    part = {name: _pair_add(name, big[name], res[rcv_at[name]], core) for name in big}
    if by_owner or small:
        part.update(_pair_add_small({n: (by_owner[n], res[rcv_at[n]]) for n in by_owner},
                                    {n: (small[n], res[rcv_at[n]]) for n in small}, core, chip))
    return part


def _pair_start(tag, big, after):
    names = list(big)
    srcs = [big[n][0] for n in names]
    lands = [jax.ShapeDtypeStruct((NCHIP, 1) + g.shape[1:], g.dtype) for g in srcs]

    def copies(src_refs, land_refs, x, y, c):
        sib = (x, y, 1 - c)
        return [[(sib, src_refs[k].at[2 * h + 1 - c], land_refs[k].at[h, 0]) for h in range(NCHIP)]
                for k in range(len(names))]

    return names, _split_start("pair_start_" + tag, srcs, lands, copies, after)


def _pair_finish(tag, names, handle, big, core, after):
    landed = _split_wait("pair_wait_" + tag, handle, [lambda ref: ref] * len(names), after)
    return {n: _pair_add(n, big[n], rcv, core) for n, rcv in zip(names, landed)}


def _chip_copies(src, land, slabbed, x, y, c):
    mine = 2 * x + y
    copies = []
    for step in range(1, NCHIP):
        h = (mine + step) % NCHIP
        copies.append(((h // 2, h % 2, c), src.at[h] if slabbed else src, land.at[mine]))
    return copies


def _chip_exchange(part, slabbed, keep_own):
    names = list(part)
    outs = [jax.ShapeDtypeStruct((() if n in slabbed else (NCHIP,)) + part[n].shape, part[n].dtype) for n in names]

    def plan(ins, out_refs, x, y, c):
        mine = 2 * x + y
        recs = []
        for k, n in enumerate(names):
            three = out_refs[k].at[pl.ds(0, NCHIP - 1)]
            rec = dict(remote=_chip_copies(ins[k], out_refs[k], n in slabbed, x, y, c), send_wait=three, recv_wait=three)
            if n in keep_own:
                rec["local"] = [(ins[k].at[mine] if n in slabbed else ins[k], out_refs[k].at[mine])]
                rec["local_wait"] = out_refs[k].at[0]
            recs.append(rec)
        return recs

    res = _comm_call("chip_exchange", [part[n] for n in names], outs, len(names), plan)
    return dict(zip(names, res))


HBM_SPEC = pl.BlockSpec(memory_space=pltpu.HBM)
SEM_SPEC = pl.BlockSpec(memory_space=pltpu.SEMAPHORE)
SPLIT_EFFECT = pltpu.SideEffectType.DATAFLOW_SIDE_EFFECTING


def _split_start(name, srcs, land_shapes, copies_fn, after):
    ns, n = len(srcs), len(land_shapes)
    lands = [pltpu.with_memory_space_constraint(s if isinstance(s, jax.Array) else lax.empty(s.shape, s.dtype), pltpu.HBM)
             for s in land_shapes]

    def body(*refs):
        src_refs, land_refs = refs[:ns], refs[ns:ns + n]
        send_sem, recv_sem = refs[ns + n + 1], refs[ns + n + 2]
        token = refs[-1]
        x, y, c = _coords()
        for k, copies in enumerate(copies_fn(src_refs, land_refs, x, y, c)):
            for peer, src, dst in copies:
                _remote(src, dst, send_sem.at[k], recv_sem.at[k], peer).start()
        token[...] = jnp.zeros_like(token)

    res = pl.pallas_call(
        body, name=name,
        out_shape=(pltpu.SemaphoreType.DMA((n,)), pltpu.SemaphoreType.DMA((n,)),
                   *[pltpu.HBM(s.shape, s.dtype) for s in land_shapes], jax.ShapeDtypeStruct((8, 128), F32)),
        in_specs=[HBM_SPEC] * (ns + n) + [ANY],
        out_specs=(SEM_SPEC, SEM_SPEC, *[HBM_SPEC] * n, pl.BlockSpec(memory_space=pltpu.VMEM)),
        input_output_aliases={ns + i: 2 + i for i in range(n)},
        compiler_params=pltpu.CompilerParams(has_side_effects=SPLIT_EFFECT),
    )(*[pltpu.with_memory_space_constraint(s, pltpu.HBM) for s in srcs], *lands, after)
    return dict(send=res[0], recv=res[1], srcs=list(srcs), lands=list(res[2:2 + n]), token=res[-1])


def _split_wait(name, handle, wait_views, after):
    n = len(handle["lands"])
    after = after if isinstance(after, (tuple, list)) else (after,)

    def body(*refs):
        land_refs = refs[:n]
        send_sem, recv_sem = refs[n], refs[n + 1]
        x, y, c = _coords()
        for k in range(n):
            w = wait_views[k](land_refs[k])
            cp = _remote(w, w, send_sem.at[k], recv_sem.at[k], (x, y, c))
            cp.wait_send()
            cp.wait_recv()

    res = pl.pallas_call(
        body, name=name,
        out_shape=tuple(pltpu.HBM(s.shape, s.dtype) for s in handle["lands"]),
        in_specs=[HBM_SPEC] * n + [SEM_SPEC, SEM_SPEC] + [ANY] * len(after), out_specs=tuple([HBM_SPEC] * n),
        input_output_aliases={i: i for i in range(n)},
        compiler_params=pltpu.CompilerParams(has_side_effects=SPLIT_EFFECT),
    )(*handle["lands"], handle["send"], handle["recv"], *after)
    return list(res)


def _adamw(w, g, m, v):
    m = ADAM_B1 * m + (1.0 - ADAM_B1) * g
    v = ADAM_B2 * v + (1.0 - ADAM_B2) * jnp.square(g)
    m_hat = m / (1.0 - ADAM_B1 ** ADAM_STEP)
    v_hat = v / (1.0 - ADAM_B2 ** ADAM_STEP)
    delta = -ADAM_LR * (m_hat / (jnp.sqrt(v_hat) + ADAM_EPS) + ADAM_WD * w)
    return delta, m, v


def _chip_start(part, tag, after=None):
    names = list(part)
    after = part[names[0]] if after is None else after

    def copies(src_refs, land_refs, x, y, c):
        return [_chip_copies(src_refs[k], land_refs[k], True, x, y, c) for k in range(len(names))]

    shapes = [jax.ShapeDtypeStruct(part[n].shape, part[n].dtype) for n in names]
    return names, _split_start("chip_start_" + tag, [part[n] for n in names], shapes, copies, after)


def _small_start(part, after):
    names = list(part)

    def copies(src_refs, land_refs, x, y, c):
        mine = 2 * x + y
        out = []
        for k in range(len(names)):
            slot = land_refs[k].at[mine]
            peers = [(mine + step) % NCHIP for step in range(1, NCHIP)]
            out.append([((h // 2, h % 2, c), slot, slot) for h in peers])
        return out

    return names, _split_start("small_start", [], [part[n] for n in names], copies, after)


def _chip_wait(names, handle, after, tag):
    three = [lambda ref: ref.at[pl.ds(0, NCHIP - 1)]] * len(names)
    return dict(zip(names, _split_wait("chip_wait_" + tag, handle, three, after)))


def _adam_big(name, own, recv, w, m, v, tk, chip, token):
    nl = len(own)
    kk, nn = w.shape[1], w.shape[2]
    nnp = own[0].shape[3]

    def body(chip_ref, *refs):
        l = pl.program_id(0)
        g = None
        for step in range(NCHIP):
            val = refs[step][...]
            for q in range(1, nl):
                val = jnp.where(l == q, refs[NCHIP * q + step][...], val)
            g = val.astype(F32) if g is None else g + val.astype(F32)
        w_ref, m_ref, v_ref, _, g_ref, d_ref, mo_ref, vo_ref = refs[NCHIP * nl:]
        g = g[:, :nn]
        delta, m2, v2 = _adamw(w_ref[...], g, m_ref[...], v_ref[...])
        g_ref[...] = g
        d_ref[...] = delta
        mo_ref[...] = m2
        vo_ref[...] = v2

    def slab(q, step):
        return pl.BlockSpec((None, None, tk, nnp), lambda l, i, chip_ref: (
            jnp.where(l == q, (chip_ref[0] + step) % NCHIP, 0), 0, jnp.where(l == q, i, 0), 0))

    in_specs, operands = [], []
    for q in range(nl):
        in_specs += [slab(q, step) for step in range(NCHIP)]
        operands += [own[q]] + [recv[q]] * (NCHIP - 1)
    wspec = pl.BlockSpec((None, tk, nn), lambda l, i, chip_ref: (l, i, 0))
    shape = jax.ShapeDtypeStruct(w.shape, F32)
    return _call(
        body, name="adamw_" + name,
        grid_spec=pltpu.PrefetchScalarGridSpec(num_scalar_prefetch=1, grid=(nl, kk // tk),
                                               in_specs=in_specs + [wspec] * 3 + [
                                                   pl.BlockSpec((8, 128), lambda l, i, chip_ref: (0, 0))],
                                               out_specs=[wspec] * 4),
        out_shape=[shape] * 4, compiler_params=_cparams(("arbitrary", "arbitrary")))(chip, *operands, w, m, v, token)


def _adam_small(names, recv, w, m, v, me):
    n = len(names)

    def body(me_ref, *refs):
        r, ww, mm, vv = refs[:n], refs[n:2 * n], refs[2 * n:3 * n], refs[3 * n:4 * n]
        outs = refs[4 * n:]

        def term(k, h, l):
            return (r[k][h, l, me_ref[0]] if names[k] == "conv_w_dw" else r[k][h, l]).astype(F32)

        for k in range(n):
            for l in range(r[k].shape[1]):
                g = term(k, 0, l)
                for h in range(1, NCHIP):
                    g = g + term(k, h, l)
                per_layer = ww[k].shape[1:]
                if g.shape[0] > per_layer[0] and g.shape[1:] == per_layer[1:]:
                    g = g[:per_layer[0]]
                at = l if g.shape == per_layer else pl.ds(l, 1)
                delta, m2, v2 = _adamw(ww[k][at], g, mm[k][at], vv[k][at])
                outs[k][at] = g
                outs[n + k][at] = delta
                outs[2 * n + k][at] = m2
                outs[3 * n + k][at] = v2

    shapes = [jax.ShapeDtypeStruct(w[k].shape, F32) for k in names]
    res = _call(body, name="adamw_small", out_shape=shapes * 4,
                in_specs=[pl.BlockSpec(memory_space=pltpu.SMEM)] + [pl.BlockSpec(memory_space=pltpu.VMEM)] * (4 * n),
                compiler_params=_cparams())(
        me, *[recv[k] for k in names], *[w[k] for k in names], *[m[k] for k in names], *[v[k] for k in names])
    return {k: (res[i], res[n + i], res[2 * n + i], res[3 * n + i]) for i, k in enumerate(names)}


def _expand_b(bt):
    eye = jnp.eye(GB, dtype=bt.dtype)
    return jnp.einsum("jgpn,gh->jgphn", bt.reshape(NBLK, GB, SGRP, NSTATE), eye).reshape(NBLK, GB * SGRP, NS)


def _extract_b(db):
    x = db.reshape(NBLK, GB, SGRP, GB, NSTATE)
    eye = jnp.eye(GB, dtype=db.dtype)
    return jnp.einsum("jgphn,gh->jgpn", x, eye).reshape(NGRP, SGRP, NSTATE)


def _expand_c(c):
    ct = jnp.transpose(c, (0, 2, 1)).reshape(NBLK, GB, NSTATE, SGRP)
    eye = jnp.eye(GB, dtype=c.dtype)
    return jnp.einsum("jgnp,gh->jgnhp", ct, eye).reshape(NBLK, NS, GB * SGRP)


def _extract_c(dc):
    x = dc.reshape(NBLK, GB, NSTATE, GB, SGRP)
    eye = jnp.eye(GB, dtype=dc.dtype)
    d = jnp.einsum("jgnhp,gh->jgnp", x, eye).reshape(NGRP, NSTATE, SGRP)
    return jnp.transpose(d, (0, 2, 1))


_SMALL = ("norm1", "b_gate", "ssm_a_re", "ssm_a_im", "ssm_log_dt", "ssm_b_re", "ssm_b_im", "ssm_c_re", "ssm_c_im",
          "ssm_d", "ssm_b_glu", "conv_b_dw", "conv_ln_g", "conv_ln_b", "pool_w_group", "pool_scale", "norm2")
_ADAM_TK = {"w_in": 256, "ssm_w_glu": 64, "ssm_w_proj": 512, "conv_w_proj": 512, "pool_w_proj": 512, "w_out": 128,
            "ffn_w_gate": HSH, "ffn_w_up": HSH, "ffn_w_down": HSH}
_OUT_ORDER = ("norm1", "w_in", "b_gate", "ssm_a_re", "ssm_a_im", "ssm_log_dt", "ssm_b_re", "ssm_b_im", "ssm_c_re",
              "ssm_c_im", "ssm_d", "ssm_w_glu", "ssm_b_glu", "ssm_w_proj", "conv_w_dw", "conv_b_dw", "conv_ln_g",
              "conv_ln_b", "conv_w_proj", "pool_w_group", "pool_scale", "pool_w_proj", "w_out", "norm2", "ffn_w_gate",
              "ffn_w_up", "ffn_w_down", "final_norm")


def _layer_fwd(x, p, token, late_params=None):
    z, h = _inproj_fwd(x, p["norm1"], p["w_in"], token)
    yssm, hre, him = _ssm_fwd(z, p)
    cv = _conv_fwd(z, p["conv_w"], p["conv_b"])
    pbar = _pool_fwd(z)
    if late_params is not None:
        more, token = late_params((yssm, cv, pbar))
        p = {**p, **more}
    x1 = _merge_fwd(x, yssm, cv, pbar, z, p, token)
    x2, gpre, upre, h2 = _ffn_fwd(x1, p["norm2"], p["wg"], p["wu"], p["wd"])
    return x2, dict(x=x, h=h, z=z, yssm=yssm, hre=hre, him=him, cv=cv, pbar=pbar, x1=x1, gpre=gpre, upre=upre, h2=h2), p


def _layer_bwd(dx, p, s, token, leave=None):
    big, small = {}, {}
    go = (lambda tag, names: leave(tag, {n: big[n] for n in names})) if leave else (lambda tag, names: token)
    h2 = s["h2"]
    dx1, d_norm2, dgp, dup, act = _ffn_bwd(dx, s["x1"], p["norm2"], s["gpre"], s["upre"], p["wg"], p["wu"], p["wd"],
                                               token)
    big["ffn_w_gate"] = _matmul_tn(dgp, h2, "tn_gate").reshape(NDEV, HPAD, D)
    big["ffn_w_up"] = _matmul_tn(dup, h2, "tn_up").reshape(NDEV, HPAD, D)
    big["ffn_w_down"] = _matmul_tn(act, dx, "tn_down").reshape(NDEV, HPAD, D)
    (dy, dcv, dpb, dzg, a_g, a_outa, a_hs, a_pb, a_pc, a_mg, c_glu, c_ya, c_yb, c_p, c_yc,
     d_bglu, d_lng, d_lnb, d_scale, d_bgate) = _merge_bwd(dx1, s["x"], s["yssm"], s["cv"], s["pbar"], s["z"], p,
                                                          go("ffn", GROUP_FFN))
    big["ssm_w_glu"] = _matmul_tn(a_g, c_glu, "tn_glu").reshape(NDEV, BW // NDEV, BW)
    big["ssm_w_proj"] = _matmul_tn(a_outa, c_ya, "tn_ssm_proj", D // NDEV)
    big["conv_w_proj"] = _matmul_tn(a_hs, c_yb, "tn_conv_proj", D // NDEV)
    big["pool_w_proj"] = _matmul_tn(a_pc, c_yc, "tn_pool_proj", D // NDEV)
    big["w_out"] = _matmul_tn(a_mg, dx1, "tn_out").reshape(NDEV, D // NDEV, D)
    d_wgrp = _group_tn(a_pb, c_p)
    du_a, dbr, dbi, dcr, dci, dd, dar, dai, dldt = _ssm_bwd(dy, s["z"], s["hre"], s["him"], p, go("mix", GROUP_MIX))
    dva, dvb, dw8, dcb = _conv_bwd(dcv, s["z"], p["conv_w"])
    du_c = _pool_bwd(dpb)
    dz = dzg
    for k, piece in enumerate((du_a, dva, dvb, du_c)):
        dz = lax.dynamic_update_slice(dz, piece, (0, k * BW))
    big["w_in"] = _matmul_tn(s["h"], dz, "tn_in", IN_W // NDEV)
    dx0, d_norm1 = _inproj_bwd(dz, dx1, s["x"], p["norm1"], p["w_in"], go("in", GROUP_IN))
    small["norm1"] = d_norm1
    small["b_gate"] = d_bgate
    small["ssm_a_re"] = dar.reshape(NGRP, NSTATE)
    small["ssm_a_im"] = dai.reshape(NGRP, NSTATE)
    small["ssm_log_dt"] = dldt.reshape(NBLK, 8, 128)[:, 0, :GB].reshape(1, NGRP)
    small["ssm_b_re"] = _extract_b(dbr)
    small["ssm_b_im"] = _extract_b(dbi)
    small["ssm_c_re"] = _extract_c(dcr)
    small["ssm_c_im"] = _extract_c(dci)
    small["ssm_d"] = dd.reshape(NGRP, SGRP)
    small["ssm_b_glu"] = d_bglu
    small["conv_b_dw"] = dcb
    small["conv_ln_g"] = d_lng
    small["conv_ln_b"] = d_lnb
    small["pool_w_group"] = d_wgrp
    small["pool_scale"] = d_scale
    small["norm2"] = d_norm2
    return dx0, big, dw8, small


def _train_step(a):
    t_rows = a["x"].shape[1]
    x0 = a["x"].reshape(t_rows, D)
    target = a["loss_target"].reshape(t_rows, D)

    tr = lambda w: jnp.transpose(w, (0, 2, 1))
    weights = {name: (tr(a[name]) if name in ("ffn_w_gate", "ffn_w_up") else a[name]) for name in _BIG}
    core = lax.axis_index("c").astype(jnp.int32).reshape(1)
    chip = (2 * lax.axis_index("x") + lax.axis_index("y")).astype(jnp.int32).reshape(1)
    me = 2 * chip + core
    no_token = jnp.zeros((8, 128), F32)
    row = lambda v: v.reshape(1, -1)
    rest = GROUP_MIX + GROUP_FFN
    first, dw_all = _gather_weights(_place_shards(weights, GROUP_IN, 0, me),
                                    a["conv_w_dw"].reshape(DEPTH, CONV_K, BW // NDEV), GROUP_IN)
    conv_w = jnp.transpose(dw_all, (1, 2, 0, 3)).reshape(DEPTH, CONV_K, BW)
    go_rest0 = _gather_start(_place_shards(weights, rest, 0, me), rest, dw_all, "rest0")
    going = {}

    def early_params(l, w_in):
        return dict(
            norm1=row(a["norm1"][l]), w_in=w_in,
            are=row(a["ssm_a_re"][l]), aim=row(a["ssm_a_im"][l]),
            ldt=row(jnp.repeat(a["ssm_log_dt"][l], NSTATE)),
            bexp_re=_expand_b(jnp.transpose(a["ssm_b_re"][l], (0, 2, 1))),
            bexp_im=_expand_b(jnp.transpose(a["ssm_b_im"][l], (0, 2, 1))),
            cexp_re=_expand_c(a["ssm_c_re"][l]), cexp_im=_expand_c(a["ssm_c_im"][l]),
            dskip=row(a["ssm_d"][l]), conv_w=conv_w[l], conv_b=row(a["conv_b_dw"][l]))

    def late_params(l, handle, tag, then_start):
        def get(after):
            full = _gather_finish(handle, rest, after, tag)
            token = then_start(full["ffn_w_down"]) if then_start else no_token
            return dict(
                wglu=full["ssm_w_glu"], bglu=row(a["ssm_b_glu"][l]), wpa=full["ssm_w_proj"],
                lng=row(a["conv_ln_g"][l]), lnb=row(a["conv_ln_b"][l]), wpb=full["conv_w_proj"],
                wgrp=a["pool_w_group"][l].astype(BF16), scale=row(a["pool_scale"][l]), wpc=full["pool_w_proj"],
                bgate=row(a["b_gate"][l]), wout=full["w_out"],
                norm2=row(a["norm2"][l]), wg=full["ffn_w_gate"], wu=full["ffn_w_up"], wd=full["ffn_w_down"]), token
        return get

    def start_in1(after):
        going["in1"] = _gather_start(_place_shards(weights, GROUP_IN, 1, me), GROUP_IN, after, "in1")
        return going["in1"]["token"]

    x, s0, p0 = _layer_fwd(x0, early_params(0, first["w_in"]), go_rest0["token"], late_params(0, go_rest0, "rest0", start_in1))
    w_in1 = _gather_finish(going["in1"], GROUP_IN, x, "in1")["w_in"]
    go_rest1 = _gather_start(_place_shards(weights, rest, 1, me), rest, w_in1, "rest1")
    x, s1, p1 = _layer_fwd(x, early_params(1, w_in1), go_rest1["token"], late_params(1, go_rest1, "rest1", None))
    params, saved = [p0, p1], [s0, s1]

    loss_part, dx, d_final = _loss_head(x, a["final_norm"].reshape(1, D), target)
    loss = lax.psum(loss_part[0, 0], ("x", "y", "c"))

    parts, swapping, gone = {}, [], []

    def swap(tag, grads, after):
        big = {n: [g] for n, g in grads.items()}
        names_, handle = _pair_start(tag, big, after)
        swapping.append((tag, names_, handle, big))
        return handle["token"]

    def send_on(after):
        tag, names_, handle, big = swapping.pop(0)
        parts[tag] = _pair_finish(tag, names_, handle, big, core, after)
        gone.append(_chip_start(parts[tag], tag) + (tag,))
        return gone[-1][1]["token"]

    def leave(tag, grads):
        token = send_on(next(iter(grads.values())))
        if tag != "in":
            return swap(tag, grads, token)
        parts[tag] = _pair_reduce(tag, {n: [g] for n, g in grads.items()}, {}, {}, core)
        gone.append(_chip_start(parts[tag], tag, token) + (tag,))
        return gone[-1][1]["token"]

    dx, gb1, go1, gs1 = _layer_bwd(dx, params[1], saved[1], no_token)
    dx, gb0, go0, gs0 = _layer_bwd(dx, params[0], saved[0], swap("late", gb1, dx), leave)
    grad_x = dx.reshape(1, t_rows, D)
    small = {n: [gs0[n], gs1[n]] for n in _SMALL}
    small["final_norm"] = [d_final]
    small["conv_w_dw"] = [go0, go1]
    small_names, small_going = _small_start(_pair_reduce("rest", {}, {}, small, core, chip), dx)
    recv, part0, part1, recv1 = {}, {}, parts["late"], None
    for names_, handle, tag in gone:
        landed = _chip_wait(names_, handle, dx, tag)
        if tag == "late":
            recv1 = landed
        else:
            recv.update(landed)
            part0.update(parts[tag])

    results = {}
    for name in _BIG:
        fix = tr if name in ("ffn_w_gate", "ffn_w_up") else (lambda t: t)
        res = _adam_big(name, [part0[name], part1[name]], [recv[name], recv1[name]], fix(a[name]), fix(a["m_" + name]),
                        fix(a["v_" + name]), _ADAM_TK[name], chip, small_going["token"])
        results[name] = tuple(fix(r) for r in res)
    three = [lambda ref: ref.at[pl.ds(0, NCHIP - 1)]] * len(small_names)
    recv_small = dict(zip(small_names, _split_wait("small_wait", small_going, three,
                                                   tuple(results[name][0] for name in _BIG))))

    lay = {
        "ssm_b_re": lambda v: jnp.transpose(v, (0, 1, 3, 2)), "ssm_b_im": lambda v: jnp.transpose(v, (0, 1, 3, 2)),
        "conv_w_dw": lambda v: v.reshape(DEPTH, CONV_K, BW // NDEV), "final_norm": lambda v: v.reshape(1, 1, D),
    }
    names = _SMALL + ("conv_w_dw", "final_norm")
    relay = lambda k, v: lay[k](v) if k in lay else v
    sm = _adam_small(names, recv_small, {k: relay(k, a[k]) for k in names}, {k: relay(k, a["m_" + k]) for k in names},
                     {k: relay(k, a["v_" + k]) for k in names}, me)
    for k in names:
        back = (lambda r: jnp.transpose(r, (0, 1, 3, 2))) if k in ("ssm_b_re", "ssm_b_im") else (lambda r: r.reshape(a[k].shape))
        results[k] = tuple(back(r) for r in sm[k])

    outs = [loss, grad_x]
    for part in range(4):
        outs += [results[k][part] for k in _OUT_ORDER]
    return tuple(outs)


def kernel(x, norm1, w_in, b_gate, ssm_a_re, ssm_a_im, ssm_log_dt, ssm_b_re, ssm_b_im, ssm_c_re, ssm_c_im, ssm_d, ssm_w_glu, ssm_b_glu, ssm_w_proj, conv_w_dw, conv_b_dw, conv_ln_g, conv_ln_b, conv_w_proj, pool_w_group, pool_scale, pool_w_proj, w_out, norm2, ffn_w_gate, ffn_w_up, ffn_w_down, final_norm, loss_target, m_norm1, m_w_in, m_b_gate, m_ssm_a_re, m_ssm_a_im, m_ssm_log_dt, m_ssm_b_re, m_ssm_b_im, m_ssm_c_re, m_ssm_c_im, m_ssm_d, m_ssm_w_glu, m_ssm_b_glu, m_ssm_w_proj, m_conv_w_dw, m_conv_b_dw, m_conv_ln_g, m_conv_ln_b, m_conv_w_proj, m_pool_w_group, m_pool_scale, m_pool_w_proj, m_w_out, m_norm2, m_ffn_w_gate, m_ffn_w_up, m_ffn_w_down, m_final_norm, v_norm1, v_w_in, v_b_gate, v_ssm_a_re, v_ssm_a_im, v_ssm_log_dt, v_ssm_b_re, v_ssm_b_im, v_ssm_c_re, v_ssm_c_im, v_ssm_d, v_ssm_w_glu, v_ssm_b_glu, v_ssm_w_proj, v_conv_w_dw, v_conv_b_dw, v_conv_ln_g, v_conv_ln_b, v_conv_w_proj, v_pool_w_group, v_pool_scale, v_pool_w_proj, v_w_out, v_norm2, v_ffn_w_gate, v_ffn_w_up, v_ffn_w_down, v_final_norm):
    return _train_step(dict(locals()))
```
